```python
import math
import jax, jax.numpy as jnp
from jax import lax
import numpy as np

D_MODEL = 1024
BATCH = 8
SEQ = 4096
DEPTH = 2

HEAD_DIM = 64
N_MEM = 256
MEM_HEADS = 4
MEM_WIDTH = MEM_HEADS * HEAD_DIM
RWKV_HEADS = 12
RWKV_WIDTH = RWKV_HEADS * HEAD_DIM
DECAY_LORA = 64
AAA_LORA = 64
GATE_LORA = 128
SHIFT_WIDTH = 3 * RWKV_WIDTH + DECAY_LORA + AAA_LORA + GATE_LORA
A_IN_WIDTH = SHIFT_WIDTH + MEM_WIDTH
A_OUT_WIDTH = RWKV_WIDTH + MEM_WIDTH
DIL_GROUPS = ((128, 1), (512, 4), (2048, 16))
DIL_HEADS_PER_GROUP = 4
DIL_HEADS = DIL_HEADS_PER_GROUP * len(DIL_GROUPS)
DIL_WIDTH = DIL_HEADS * HEAD_DIM
DIL_BLOCK = 128
B_IN_WIDTH = DIL_WIDTH + MEM_WIDTH
B_OUT_WIDTH = DIL_HEADS_PER_GROUP * HEAD_DIM + MEM_WIDTH
D_FF = 2816
CONV_WIDTH = 3
ROPE_THETA = 10000.0
RMS_EPS = 1e-6
LNX_EPS = 64e-5
NEG_INF = -1e30
N_A = DEPTH // 2
N_B = DEPTH - N_A

kernel_name = "yoco_rwkv7_dilated_memory_convffn"


def f32(a):
    return a.astype(jnp.float32)


def rmsnorm(x, g):
    xf = f32(x)
    return xf * lax.rsqrt(jnp.mean(xf * xf, axis=-1, keepdims=True) + RMS_EPS) * f32(g)


def rope_tables(T):
    inv = ROPE_THETA ** (-jnp.arange(0, HEAD_DIM, 2, dtype=jnp.float32) / HEAD_DIM)
    ang = jnp.arange(T, dtype=jnp.float32)[:, None] * inv[None, :]
    return jnp.cos(ang)[:, None, :], jnp.sin(ang)[:, None, :]


def apply_rope(z, cos, sin):
    half = HEAD_DIM // 2
    z1, z2 = z[..., :half], z[..., half:]
    return jnp.concatenate([z1 * cos - z2 * sin, z2 * cos + z1 * sin], axis=-1)


def rwkv7_time_mix(p, mu, w0, w2, a0, a2, g2, k_k, k_a, r_k, lnx_w, lnx_b):
    B, T, _ = p.shape
    prev = jnp.pad(p[:, :-1], ((0, 0), (1, 0), (0, 0)))
    xs = p + (prev - p) * f32(mu)
    cuts = np.cumsum([RWKV_WIDTH, RWKV_WIDTH, RWKV_WIDTH, DECAY_LORA, AAA_LORA]).tolist()
    r, k, v, wd, ad, gd = jnp.split(xs, cuts, axis=-1)
    w_log = -jax.nn.softplus(-(f32(w0) + jnp.tanh(wd) @ f32(w2))) - 0.5
    decay = jnp.exp(-jnp.exp(w_log))
    a = jax.nn.sigmoid(f32(a0) + ad @ f32(a2))
    g = jax.nn.sigmoid(gd) @ f32(g2)
    heads = lambda z: z.reshape(B, T, RWKV_HEADS, HEAD_DIM)
    kk = heads(k * f32(k_k))
    kk = kk / jnp.maximum(jnp.linalg.norm(kk, axis=-1, keepdims=True), 1e-12)
    k = k * (1.0 + (a - 1.0) * f32(k_a))
    r_h, k_h, v_h, w_h, a_h = heads(r), heads(k), heads(v), heads(decay), heads(a)

    def step(S, inp):
        r_t, w_t, k_t, v_t, kk_t, a_t = inp
        sa = jnp.einsum('bhvk,bhk->bhv', S, -kk_t)
        S = (S * w_t[:, :, None, :] + sa[..., None] * (kk_t * a_t)[:, :, None, :]
             + v_t[..., None] * k_t[:, :, None, :])
        return S, jnp.einsum('bhvk,bhk->bhv', S, r_t)

    seq = tuple(jnp.moveaxis(z, 1, 0) for z in (r_h, w_h, k_h, v_h, kk, a_h))
    S0 = jnp.zeros((B, RWKV_HEADS, HEAD_DIM, HEAD_DIM), jnp.float32)
    _, ys = lax.scan(step, S0, seq)
    y = jnp.moveaxis(ys, 0, 1)
    m = jnp.mean(y, axis=-1, keepdims=True)
    var = jnp.mean((y - m) ** 2, axis=-1, keepdims=True)
    y = ((y - m) * lax.rsqrt(var + LNX_EPS)).reshape(B, T, RWKV_WIDTH) * f32(lnx_w) + f32(lnx_b)
    bonus = (jnp.sum(r_h * k_h * f32(r_k), axis=-1, keepdims=True) * v_h).reshape(B, T, RWKV_WIDTH)
    return (y + bonus) * g


def dilated_group_attention(q, k, v, window, dilation):
    B, T, H, Dh = q.shape
    back = window // dilation
    span = dilation * DIL_BLOCK
    Tp = ((T + span - 1) // span) * span
    U = Tp // dilation
    nb = U // DIL_BLOCK

    def to_blocks(z):
        z = jnp.pad(z, ((0, 0), (0, Tp - T), (0, 0), (0, 0)))
        z = z.reshape(B, U, dilation, H, Dh).transpose(0, 3, 2, 1, 4)
        return z.reshape(B, H, dilation, nb, DIL_BLOCK, Dh)

    def with_prev(z):
        prev = jnp.pad(z, ((0, 0), (0, 0), (0, 0), (1, 0), (0, 0), (0, 0)))[:, :, :, :-1]
        return jnp.concatenate([prev, z], axis=4)

    qb = to_blocks(q)
    kw, vw = with_prev(to_blocks(k)), with_prev(to_blocks(v))
    s = jnp.einsum('bhrnid,bhrnjd->bhrnij', qb, kw) / math.sqrt(Dh)
    i = jnp.arange(DIL_BLOCK)[:, None]
    jj = jnp.arange(2 * DIL_BLOCK)[None, :]
    dist = DIL_BLOCK + i - jj
    band = (dist >= 0) & (dist <= back)
    valid = band[None] & ((jnp.arange(nb) > 0)[:, None, None] | (jj >= DIL_BLOCK)[None])
    s = jnp.where(valid, s, NEG_INF)
    mx = jnp.max(s, axis=-1, keepdims=True)
    pr = jnp.exp(s - mx)
    den = jnp.sum(pr, axis=-1, keepdims=True)
    o = jnp.einsum('bhrnij,bhrnjd->bhrnid', pr, vw) / den
    lse = (mx + jnp.log(den))[..., 0]
    o = o.reshape(B, H, dilation, U, Dh).transpose(0, 3, 2, 1, 4).reshape(B, Tp, H, Dh)[:, :T]
    lse = lse.reshape(B, H, dilation, U).transpose(0, 3, 2, 1).reshape(B, Tp, H)[:, :T]
    return o, lse


def dilated_attention(q, k, v):
    B, T = q.shape[:2]
    outs, lses = [], []
    for gi, (win, dil) in enumerate(DIL_GROUPS):
        hs = slice(gi * DIL_HEADS_PER_GROUP, (gi + 1) * DIL_HEADS_PER_GROUP)
        o, l = dilated_group_attention(q[:, :, hs], k[:, :, hs], v[:, :, hs], win, dil)
        outs.append(o)
        lses.append(l)
    wgt = jax.nn.softmax(jnp.stack(lses, axis=0), axis=0)
    o = jnp.sum(wgt[..., None] * jnp.stack(outs, axis=0), axis=0)
    return o.reshape(B, T, DIL_HEADS_PER_GROUP * HEAD_DIM)


def memory_attention(q, mem, norm_g, w_kv, q_norm, k_norm):
    B, T, _ = q.shape
    M = mem.shape[1]
    kv = f32(rmsnorm(mem, norm_g).astype(mem.dtype) @ w_kv)
    k = rmsnorm(kv[..., :MEM_WIDTH].reshape(B, M, MEM_HEADS, HEAD_DIM), k_norm)
    v = kv[..., MEM_WIDTH:].reshape(B, M, MEM_HEADS, HEAD_DIM)
    qh = rmsnorm(q.reshape(B, T, MEM_HEADS, HEAD_DIM), q_norm)
    s = jnp.einsum('bthd,bmhd->bhtm', qh, k) / math.sqrt(HEAD_DIM)
    p = jax.nn.softmax(s, axis=-1)
    return jnp.einsum('bhtm,bmhd->bthd', p, v).reshape(B, T, MEM_WIDTH)


def conv_ffn(x, g, w_up, conv_w, conv_b, w_down):
    T = x.shape[1]
    u = f32(rmsnorm(x, g).astype(x.dtype) @ w_up)
    up = jnp.pad(u, ((0, 0), (CONV_WIDTH - 1, 0), (0, 0)))
    c = f32(conv_b) + sum(f32(conv_w[j]) * up[:, j:j + T] for j in range(CONV_WIDTH))
    gate, val = jnp.split(c, 2, axis=-1)
    z = jax.nn.silu(gate) * val
    return z.astype(x.dtype) @ w_down


def _fwd_setup_inputs(seed: int = 0) -> dict:
    key = jax.random.key(seed)
    ks = iter(jax.random.split(key, 40))
    D = D_MODEL

    def nrm(shape, scale):
        return scale * jax.random.normal(next(ks), shape, jnp.float32)

    def unif(shape, lo, hi):
        return jax.random.uniform(next(ks), shape, jnp.float32, lo, hi)

    def gain(shape):
        return 1.0 + nrm(shape, 0.02)

    return {
        "x": nrm((BATCH, SEQ, D), 1.0),
        "mem": nrm((BATCH, N_MEM, D), 1.0),
        "attn_norm": gain((DEPTH, D)),
        "a_w_in": nrm((N_A, D, A_IN_WIDTH), D ** -0.5),
        "a_mu": unif((N_A, SHIFT_WIDTH), 0.0, 1.0),
        "a_w0": unif((N_A, RWKV_WIDTH), -5.0, 0.0),
        "a_w2": nrm((N_A, DECAY_LORA, RWKV_WIDTH), 0.5 * DECAY_LORA ** -0.5),
        "a_a0": nrm((N_A, RWKV_WIDTH), 0.5),
        "a_a2": nrm((N_A, AAA_LORA, RWKV_WIDTH), AAA_LORA ** -0.5),
        "a_g2": nrm((N_A, GATE_LORA, RWKV_WIDTH), GATE_LORA ** -0.5),
        "a_k_k": 0.85 + nrm((N_A, RWKV_WIDTH), 0.05),
        "a_k_a": 1.0 + nrm((N_A, RWKV_WIDTH), 0.05),
        "a_r_k": nrm((N_A, RWKV_HEADS, HEAD_DIM), 0.1),
        "a_lnx_w": gain((N_A, RWKV_WIDTH)),
        "a_lnx_b": nrm((N_A, RWKV_WIDTH), 0.02),
        "a_w_out": nrm((N_A, A_OUT_WIDTH, D), A_OUT_WIDTH ** -0.5),
        "kv_norm": gain((D,)),
        "kv_w": nrm((D, 2 * DIL_WIDTH), D ** -0.5),
        "kv_k_norm": gain((HEAD_DIM,)),
        "b_w_in": nrm((N_B, D, B_IN_WIDTH), D ** -0.5),
        "b_q_norm": gain((N_B, HEAD_DIM)),
        "b_w_out": nrm((N_B, B_OUT_WIDTH, D), B_OUT_WIDTH ** -0.5),
        "mem_norm": gain((DEPTH, D)),
        "mem_w_kv": nrm((DEPTH, D, 2 * MEM_WIDTH), D ** -0.5),
        "mem_q_norm": gain((DEPTH, HEAD_DIM)),
        "mem_k_norm": gain((DEPTH, HEAD_DIM)),
        "ffn_norm": gain((DEPTH, D)),
        "ffn_w_up": nrm((DEPTH, D, 2 * D_FF), D ** -0.5),
        "ffn_conv_w": nrm((DEPTH, CONV_WIDTH, 2 * D_FF), CONV_WIDTH ** -0.5),
        "ffn_conv_b": nrm((DEPTH, 2 * D_FF), 0.02),
        "ffn_w_down": nrm((DEPTH, D_FF, D), D_FF ** -0.5),
    }


def _fwd_reference(x, mem, attn_norm, a_w_in, a_mu, a_w0, a_w2, a_a0, a_a2, a_g2, a_k_k, a_k_a,
              a_r_k, a_lnx_w, a_lnx_b, a_w_out, kv_norm, kv_w, kv_k_norm, b_w_in, b_q_norm,
              b_w_out, mem_norm, mem_w_kv, mem_q_norm, mem_k_norm, ffn_norm, ffn_w_up,
              ffn_conv_w, ffn_conv_b, ffn_w_down):
    B, T, _ = x.shape
    cos, sin = rope_tables(T)
    k_sh = v_sh = None
    for i in range(DEPTH):
        h = rmsnorm(x, attn_norm[i]).astype(x.dtype)
        if i < N_A:
            j = i
            p = f32(h @ a_w_in[j])
            y_mix = rwkv7_time_mix(p[..., :SHIFT_WIDTH], a_mu[j], a_w0[j], a_w2[j], a_a0[j],
                                   a_a2[j], a_g2[j], a_k_k[j], a_k_a[j], a_r_k[j],
                                   a_lnx_w[j], a_lnx_b[j])
            q_mem = p[..., SHIFT_WIDTH:]
            w_out = a_w_out[j]
        else:
            j = i - N_A
            if j == 0:
                kvp = f32(rmsnorm(x, kv_norm).astype(x.dtype) @ kv_w)
                k_sh = apply_rope(rmsnorm(kvp[..., :DIL_WIDTH].reshape(B, T, DIL_HEADS, HEAD_DIM),
                                          kv_k_norm), cos, sin)
                v_sh = kvp[..., DIL_WIDTH:].reshape(B, T, DIL_HEADS, HEAD_DIM)
            p = f32(h @ b_w_in[j])
            q = apply_rope(rmsnorm(p[..., :DIL_WIDTH].reshape(B, T, DIL_HEADS, HEAD_DIM),
                                   b_q_norm[j]), cos, sin)
            y_mix = dilated_attention(q, k_sh, v_sh)
            q_mem = p[..., DIL_WIDTH:]
            w_out = b_w_out[j]
        y_mem = memory_attention(q_mem, mem, mem_norm[i], mem_w_kv[i], mem_q_norm[i], mem_k_norm[i])
        y = jnp.concatenate([y_mix, y_mem], axis=-1).astype(x.dtype)
        x = x + y @ w_out
        x = x + conv_ffn(x, ffn_norm[i], ffn_w_up[i], ffn_conv_w[i], ffn_conv_b[i], ffn_w_down[i])
    return x


import jax as _jax
import jax.numpy as _jnp

TWIN_FORMAT = 'train_step'
FWD_PARAMS = ['x', 'mem', 'attn_norm', 'a_w_in', 'a_mu', 'a_w0', 'a_w2', 'a_a0', 'a_a2', 'a_g2', 'a_k_k', 'a_k_a', 'a_r_k', 'a_lnx_w', 'a_lnx_b', 'a_w_out', 'kv_norm', 'kv_w', 'kv_k_norm', 'b_w_in', 'b_q_norm', 'b_w_out', 'mem_norm', 'mem_w_kv', 'mem_q_norm', 'mem_k_norm', 'ffn_norm', 'ffn_w_up', 'ffn_conv_w', 'ffn_conv_b', 'ffn_w_down']
TWIN_WEIGHTS = ['attn_norm', 'a_w_in', 'a_mu', 'a_w0', 'a_w2', 'a_a0', 'a_a2', 'a_g2', 'a_k_k', 'a_k_a', 'a_r_k', 'a_lnx_w', 'a_lnx_b', 'a_w_out', 'kv_norm', 'kv_w', 'kv_k_norm', 'b_w_in', 'b_q_norm', 'b_w_out', 'mem_norm', 'mem_w_kv', 'mem_q_norm', 'mem_k_norm', 'ffn_norm', 'ffn_w_up', 'ffn_conv_w', 'ffn_conv_b', 'ffn_w_down']
TWIN_DIFF_INPUT = 'x'
TWIN_INPUTS = ['x', 'mem', 'attn_norm', 'a_w_in', 'a_mu', 'a_w0', 'a_w2', 'a_a0', 'a_a2', 'a_g2', 'a_k_k', 'a_k_a', 'a_r_k', 'a_lnx_w', 'a_lnx_b', 'a_w_out', 'kv_norm', 'kv_w', 'kv_k_norm', 'b_w_in', 'b_q_norm', 'b_w_out', 'mem_norm', 'mem_w_kv', 'mem_q_norm', 'mem_k_norm', 'ffn_norm', 'ffn_w_up', 'ffn_conv_w', 'ffn_conv_b', 'ffn_w_down', 'loss_target', 'm_attn_norm', 'm_a_w_in', 'm_a_mu', 'm_a_w0', 'm_a_w2', 'm_a_a0', 'm_a_a2', 'm_a_g2', 'm_a_k_k', 'm_a_k_a', 'm_a_r_k', 'm_a_lnx_w', 'm_a_lnx_b', 'm_a_w_out', 'm_kv_norm', 'm_kv_w', 'm_kv_k_norm', 'm_b_w_in', 'm_b_q_norm', 'm_b_w_out', 'm_mem_norm', 'm_mem_w_kv', 'm_mem_q_norm', 'm_mem_k_norm', 'm_ffn_norm', 'm_ffn_w_up', 'm_ffn_conv_w', 'm_ffn_conv_b', 'm_ffn_w_down', 'v_attn_norm', 'v_a_w_in', 'v_a_mu', 'v_a_w0', 'v_a_w2', 'v_a_a0', 'v_a_a2', 'v_a_g2', 'v_a_k_k', 'v_a_k_a', 'v_a_r_k', 'v_a_lnx_w', 'v_a_lnx_b', 'v_a_w_out', 'v_kv_norm', 'v_kv_w', 'v_kv_k_norm', 'v_b_w_in', 'v_b_q_norm', 'v_b_w_out', 'v_mem_norm', 'v_mem_w_kv', 'v_mem_q_norm', 'v_mem_k_norm', 'v_ffn_norm', 'v_ffn_w_up', 'v_ffn_conv_w', 'v_ffn_conv_b', 'v_ffn_w_down']
TWIN_OUTPUTS = ['loss', 'grad_x', 'grad_attn_norm', 'grad_a_w_in', 'grad_a_mu', 'grad_a_w0', 'grad_a_w2', 'grad_a_a0', 'grad_a_a2', 'grad_a_g2', 'grad_a_k_k', 'grad_a_k_a', 'grad_a_r_k', 'grad_a_lnx_w', 'grad_a_lnx_b', 'grad_a_w_out', 'grad_kv_norm', 'grad_kv_w', 'grad_kv_k_norm', 'grad_b_w_in', 'grad_b_q_norm', 'grad_b_w_out', 'grad_mem_norm', 'grad_mem_w_kv', 'grad_mem_q_norm', 'grad_mem_k_norm', 'grad_ffn_norm', 'grad_ffn_w_up', 'grad_ffn_conv_w', 'grad_ffn_conv_b', 'grad_ffn_w_down', 'delta_attn_norm', 'delta_a_w_in', 'delta_a_mu', 'delta_a_w0', 'delta_a_w2', 'delta_a_a0', 'delta_a_a2', 'delta_a_g2', 'delta_a_k_k', 'delta_a_k_a', 'delta_a_r_k', 'delta_a_lnx_w', 'delta_a_lnx_b', 'delta_a_w_out', 'delta_kv_norm', 'delta_kv_w', 'delta_kv_k_norm', 'delta_b_w_in', 'delta_b_q_norm', 'delta_b_w_out', 'delta_mem_norm', 'delta_mem_w_kv', 'delta_mem_q_norm', 'delta_mem_k_norm', 'delta_ffn_norm', 'delta_ffn_w_up', 'delta_ffn_conv_w', 'delta_ffn_conv_b', 'delta_ffn_w_down', 'new_m_attn_norm', 'new_m_a_w_in', 'new_m_a_mu', 'new_m_a_w0', 'new_m_a_w2', 'new_m_a_a0', 'new_m_a_a2', 'new_m_a_g2', 'new_m_a_k_k', 'new_m_a_k_a', 'new_m_a_r_k', 'new_m_a_lnx_w', 'new_m_a_lnx_b', 'new_m_a_w_out', 'new_m_kv_norm', 'new_m_kv_w', 'new_m_kv_k_norm', 'new_m_b_w_in', 'new_m_b_q_norm', 'new_m_b_w_out', 'new_m_mem_norm', 'new_m_mem_w_kv', 'new_m_mem_q_norm', 'new_m_mem_k_norm', 'new_m_ffn_norm', 'new_m_ffn_w_up', 'new_m_ffn_conv_w', 'new_m_ffn_conv_b', 'new_m_ffn_w_down', 'new_v_attn_norm', 'new_v_a_w_in', 'new_v_a_mu', 'new_v_a_w0', 'new_v_a_w2', 'new_v_a_a0', 'new_v_a_a2', 'new_v_a_g2', 'new_v_a_k_k', 'new_v_a_k_a', 'new_v_a_r_k', 'new_v_a_lnx_w', 'new_v_a_lnx_b', 'new_v_a_w_out', 'new_v_kv_norm', 'new_v_kv_w', 'new_v_kv_k_norm', 'new_v_b_w_in', 'new_v_b_q_norm', 'new_v_b_w_out', 'new_v_mem_norm', 'new_v_mem_w_kv', 'new_v_mem_q_norm', 'new_v_mem_k_norm', 'new_v_ffn_norm', 'new_v_ffn_w_up', 'new_v_ffn_conv_w', 'new_v_ffn_conv_b', 'new_v_ffn_w_down']
TWIN_LEAF_KINDS = {'loss': 'loss', 'grad_x': 'grad_x', 'grad_attn_norm': 'grad_w', 'grad_a_w_in': 'grad_w', 'grad_a_mu': 'grad_w', 'grad_a_w0': 'grad_w', 'grad_a_w2': 'grad_w', 'grad_a_a0': 'grad_w', 'grad_a_a2': 'grad_w', 'grad_a_g2': 'grad_w', 'grad_a_k_k': 'grad_w', 'grad_a_k_a': 'grad_w', 'grad_a_r_k': 'grad_w', 'grad_a_lnx_w': 'grad_w', 'grad_a_lnx_b': 'grad_w', 'grad_a_w_out': 'grad_w', 'grad_kv_norm': 'grad_w', 'grad_kv_w': 'grad_w', 'grad_kv_k_norm': 'grad_w', 'grad_b_w_in': 'grad_w', 'grad_b_q_norm': 'grad_w', 'grad_b_w_out': 'grad_w', 'grad_mem_norm': 'grad_w', 'grad_mem_w_kv': 'grad_w', 'grad_mem_q_norm': 'grad_w', 'grad_mem_k_norm': 'grad_w', 'grad_ffn_norm': 'grad_w', 'grad_ffn_w_up': 'grad_w', 'grad_ffn_conv_w': 'grad_w', 'grad_ffn_conv_b': 'grad_w', 'grad_ffn_w_down': 'grad_w', 'delta_attn_norm': 'delta_w', 'delta_a_w_in': 'delta_w', 'delta_a_mu': 'delta_w', 'delta_a_w0': 'delta_w', 'delta_a_w2': 'delta_w', 'delta_a_a0': 'delta_w', 'delta_a_a2': 'delta_w', 'delta_a_g2': 'delta_w', 'delta_a_k_k': 'delta_w', 'delta_a_k_a': 'delta_w', 'delta_a_r_k': 'delta_w', 'delta_a_lnx_w': 'delta_w', 'delta_a_lnx_b': 'delta_w', 'delta_a_w_out': 'delta_w', 'delta_kv_norm': 'delta_w', 'delta_kv_w': 'delta_w', 'delta_kv_k_norm': 'delta_w', 'delta_b_w_in': 'delta_w', 'delta_b_q_norm': 'delta_w', 'delta_b_w_out': 'delta_w', 'delta_mem_norm': 'delta_w', 'delta_mem_w_kv': 'delta_w', 'delta_mem_q_norm': 'delta_w', 'delta_mem_k_norm': 'delta_w', 'delta_ffn_norm': 'delta_w', 'delta_ffn_w_up': 'delta_w', 'delta_ffn_conv_w': 'delta_w', 'delta_ffn_conv_b': 'delta_w', 'delta_ffn_w_down': 'delta_w', 'new_m_attn_norm': 'new_m', 'new_m_a_w_in': 'new_m', 'new_m_a_mu': 'new_m', 'new_m_a_w0': 'new_m', 'new_m_a_w2': 'new_m', 'new_m_a_a0': 'new_m', 'new_m_a_a2': 'new_m', 'new_m_a_g2': 'new_m', 'new_m_a_k_k': 'new_m', 'new_m_a_k_a': 'new_m', 'new_m_a_r_k': 'new_m', 'new_m_a_lnx_w': 'new_m', 'new_m_a_lnx_b': 'new_m', 'new_m_a_w_out': 'new_m', 'new_m_kv_norm': 'new_m', 'new_m_kv_w': 'new_m', 'new_m_kv_k_norm': 'new_m', 'new_m_b_w_in': 'new_m', 'new_m_b_q_norm': 'new_m', 'new_m_b_w_out': 'new_m', 'new_m_mem_norm': 'new_m', 'new_m_mem_w_kv': 'new_m', 'new_m_mem_q_norm': 'new_m', 'new_m_mem_k_norm': 'new_m', 'new_m_ffn_norm': 'new_m', 'new_m_ffn_w_up': 'new_m', 'new_m_ffn_conv_w': 'new_m', 'new_m_ffn_conv_b': 'new_m', 'new_m_ffn_w_down': 'new_m', 'new_v_attn_norm': 'new_v', 'new_v_a_w_in': 'new_v', 'new_v_a_mu': 'new_v', 'new_v_a_w0': 'new_v', 'new_v_a_w2': 'new_v', 'new_v_a_a0': 'new_v', 'new_v_a_a2': 'new_v', 'new_v_a_g2': 'new_v', 'new_v_a_k_k': 'new_v', 'new_v_a_k_a': 'new_v', 'new_v_a_r_k': 'new_v', 'new_v_a_lnx_w': 'new_v', 'new_v_a_lnx_b': 'new_v', 'new_v_a_w_out': 'new_v', 'new_v_kv_norm': 'new_v', 'new_v_kv_w': 'new_v', 'new_v_kv_k_norm': 'new_v', 'new_v_b_w_in': 'new_v', 'new_v_b_q_norm': 'new_v', 'new_v_b_w_out': 'new_v', 'new_v_mem_norm': 'new_v', 'new_v_mem_w_kv': 'new_v', 'new_v_mem_q_norm': 'new_v', 'new_v_mem_k_norm': 'new_v', 'new_v_ffn_norm': 'new_v', 'new_v_ffn_w_up': 'new_v', 'new_v_ffn_conv_w': 'new_v', 'new_v_ffn_conv_b': 'new_v', 'new_v_ffn_w_down': 'new_v'}


def _forward(args):
    return _fwd_reference(*[args[k] for k in FWD_PARAMS])


def _output_shape():
    out = _jax.eval_shape(lambda: _forward(_fwd_setup_inputs(0)))
    return out.shape, out.dtype

N_MICROBATCH = 1
ADAM_LR = 0.001
ADAM_B1 = 0.9
ADAM_B2 = 0.999
ADAM_EPS = 1e-08
ADAM_WD = 0.01
ADAM_STEP = 10
PER_EXAMPLE_BATCH_AXIS = {'x': 0, 'mem': 0, 'loss_target': 0}
SHARED_INPUTS = []
_WEIGHT_DTYPES = {'attn_norm': _jnp.float32, 'a_w_in': _jnp.float32, 'a_mu': _jnp.float32, 'a_w0': _jnp.float32, 'a_w2': _jnp.float32, 'a_a0': _jnp.float32, 'a_a2': _jnp.float32, 'a_g2': _jnp.float32, 'a_k_k': _jnp.float32, 'a_k_a': _jnp.float32, 'a_r_k': _jnp.float32, 'a_lnx_w': _jnp.float32, 'a_lnx_b': _jnp.float32, 'a_w_out': _jnp.float32, 'kv_norm': _jnp.float32, 'kv_w': _jnp.float32, 'kv_k_norm': _jnp.float32, 'b_w_in': _jnp.float32, 'b_q_norm': _jnp.float32, 'b_w_out': _jnp.float32, 'mem_norm': _jnp.float32, 'mem_w_kv': _jnp.float32, 'mem_q_norm': _jnp.float32, 'mem_k_norm': _jnp.float32, 'ffn_norm': _jnp.float32, 'ffn_w_up': _jnp.float32, 'ffn_conv_w': _jnp.float32, 'ffn_conv_b': _jnp.float32, 'ffn_w_down': _jnp.float32}
MOMENT_SCALE = {'attn_norm': 9.469260e-01, 'a_w_in': 3.698895e-01, 'a_mu': 3.699694e+00, 'a_w0': 2.047778e-01, 'a_w2': 2.233600e-02, 'a_a0': 3.877158e-01, 'a_a2': 1.221917e-01, 'a_g2': 8.392881e+00, 'a_k_k': 3.182796e-01, 'a_k_a': 8.008593e-01, 'a_r_k': 6.816591e+00, 'a_lnx_w': 1.437957e+01, 'a_lnx_b': 6.782889e+00, 'a_w_out': 4.316281e-01, 'kv_norm': 1.358069e-01, 'kv_w': 7.432727e-02, 'kv_k_norm': 1.037809e+00, 'b_w_in': 6.505719e-02, 'b_q_norm': 1.037846e+00, 'b_w_out': 1.029076e-01, 'mem_norm': 1.895925e-01, 'mem_w_kv': 2.209276e-01, 'mem_q_norm': 1.627335e+00, 'mem_k_norm': 1.618586e+00, 'ffn_norm': 2.604511e+01, 'ffn_w_up': 2.510547e-01, 'ffn_conv_w': 3.548421e+00, 'ffn_conv_b': 3.186608e+00, 'ffn_w_down': 3.500625e-01}


def _to_microbatches(a, axis):
    t = _jnp.moveaxis(a, axis, 0)
    t = t.reshape((N_MICROBATCH, t.shape[0] // N_MICROBATCH) + t.shape[1:])
    return _jnp.moveaxis(t, 1, axis + 1)


def setup_inputs(seed: int = 0) -> dict:
    inp = _fwd_setup_inputs(seed)
    key = _jax.random.fold_in(_jax.random.key(seed), 7919)
    shape, _ = _output_shape()
    out = dict(inp)
    out["loss_target"] = _jax.random.normal(_jax.random.fold_in(key, 0), shape, _jnp.float32)
    for i, name in enumerate(TWIN_WEIGHTS):
        w = inp[name].astype(_jnp.float32)
        if MOMENT_SCALE is None:
            s = _jnp.sqrt(_jnp.mean(_jnp.square(w)) + 1e-30)
        else:
            s = MOMENT_SCALE[name]
        km, kv = _jax.random.split(_jax.random.fold_in(key, i + 1))
        out[name] = w
        out["m_" + name] = s * _jax.random.normal(km, w.shape, _jnp.float32)
        out["v_" + name] = (s * s) * _jax.random.uniform(kv, w.shape, _jnp.float32, 0.5, 1.5)
    if N_MICROBATCH > 1:
        for name, axis in PER_EXAMPLE_BATCH_AXIS.items():
            out[name] = _to_microbatches(out[name], axis)
    return {'x': out['x'], 'mem': out['mem'], 'attn_norm': out['attn_norm'], 'a_w_in': out['a_w_in'], 'a_mu': out['a_mu'], 'a_w0': out['a_w0'], 'a_w2': out['a_w2'], 'a_a0': out['a_a0'], 'a_a2': out['a_a2'], 'a_g2': out['a_g2'], 'a_k_k': out['a_k_k'], 'a_k_a': out['a_k_a'], 'a_r_k': out['a_r_k'], 'a_lnx_w': out['a_lnx_w'], 'a_lnx_b': out['a_lnx_b'], 'a_w_out': out['a_w_out'], 'kv_norm': out['kv_norm'], 'kv_w': out['kv_w'], 'kv_k_norm': out['kv_k_norm'], 'b_w_in': out['b_w_in'], 'b_q_norm': out['b_q_norm'], 'b_w_out': out['b_w_out'], 'mem_norm': out['mem_norm'], 'mem_w_kv': out['mem_w_kv'], 'mem_q_norm': out['mem_q_norm'], 'mem_k_norm': out['mem_k_norm'], 'ffn_norm': out['ffn_norm'], 'ffn_w_up': out['ffn_w_up'], 'ffn_conv_w': out['ffn_conv_w'], 'ffn_conv_b': out['ffn_conv_b'], 'ffn_w_down': out['ffn_w_down'], 'loss_target': out['loss_target'], 'm_attn_norm': out['m_attn_norm'], 'm_a_w_in': out['m_a_w_in'], 'm_a_mu': out['m_a_mu'], 'm_a_w0': out['m_a_w0'], 'm_a_w2': out['m_a_w2'], 'm_a_a0': out['m_a_a0'], 'm_a_a2': out['m_a_a2'], 'm_a_g2': out['m_a_g2'], 'm_a_k_k': out['m_a_k_k'], 'm_a_k_a': out['m_a_k_a'], 'm_a_r_k': out['m_a_r_k'], 'm_a_lnx_w': out['m_a_lnx_w'], 'm_a_lnx_b': out['m_a_lnx_b'], 'm_a_w_out': out['m_a_w_out'], 'm_kv_norm': out['m_kv_norm'], 'm_kv_w': out['m_kv_w'], 'm_kv_k_norm': out['m_kv_k_norm'], 'm_b_w_in': out['m_b_w_in'], 'm_b_q_norm': out['m_b_q_norm'], 'm_b_w_out': out['m_b_w_out'], 'm_mem_norm': out['m_mem_norm'], 'm_mem_w_kv': out['m_mem_w_kv'], 'm_mem_q_norm': out['m_mem_q_norm'], 'm_mem_k_norm': out['m_mem_k_norm'], 'm_ffn_norm': out['m_ffn_norm'], 'm_ffn_w_up': out['m_ffn_w_up'], 'm_ffn_conv_w': out['m_ffn_conv_w'], 'm_ffn_conv_b': out['m_ffn_conv_b'], 'm_ffn_w_down': out['m_ffn_w_down'], 'v_attn_norm': out['v_attn_norm'], 'v_a_w_in': out['v_a_w_in'], 'v_a_mu': out['v_a_mu'], 'v_a_w0': out['v_a_w0'], 'v_a_w2': out['v_a_w2'], 'v_a_a0': out['v_a_a0'], 'v_a_a2': out['v_a_a2'], 'v_a_g2': out['v_a_g2'], 'v_a_k_k': out['v_a_k_k'], 'v_a_k_a': out['v_a_k_a'], 'v_a_r_k': out['v_a_r_k'], 'v_a_lnx_w': out['v_a_lnx_w'], 'v_a_lnx_b': out['v_a_lnx_b'], 'v_a_w_out': out['v_a_w_out'], 'v_kv_norm': out['v_kv_norm'], 'v_kv_w': out['v_kv_w'], 'v_kv_k_norm': out['v_kv_k_norm'], 'v_b_w_in': out['v_b_w_in'], 'v_b_q_norm': out['v_b_q_norm'], 'v_b_w_out': out['v_b_w_out'], 'v_mem_norm': out['v_mem_norm'], 'v_mem_w_kv': out['v_mem_w_kv'], 'v_mem_q_norm': out['v_mem_q_norm'], 'v_mem_k_norm': out['v_mem_k_norm'], 'v_ffn_norm': out['v_ffn_norm'], 'v_ffn_w_up': out['v_ffn_w_up'], 'v_ffn_conv_w': out['v_ffn_conv_w'], 'v_ffn_conv_b': out['v_ffn_conv_b'], 'v_ffn_w_down': out['v_ffn_w_down']}


def _loss(weights, diff, rest, loss_target):
    with _jax.named_scope("forward"):
        args = {**rest, TWIN_DIFF_INPUT: diff, **{k: w.astype(_WEIGHT_DTYPES[k]) for k, w in weights.items()}}
        y = _forward(args)
    with _jax.named_scope("loss_head"):
        err = _jnp.square(y.astype(_jnp.float32) - loss_target)
        return 0.5 * _jnp.sum(_jnp.mean(err, axis=-1)) if err.ndim else 0.5 * err


def _adamw(w, g, m, v):
    m = ADAM_B1 * m + (1.0 - ADAM_B1) * g
    v = ADAM_B2 * v + (1.0 - ADAM_B2) * _jnp.square(g)
    m_hat = m / (1.0 - ADAM_B1 ** ADAM_STEP)
    v_hat = v / (1.0 - ADAM_B2 ** ADAM_STEP)
    delta = -ADAM_LR * (m_hat / (_jnp.sqrt(v_hat) + ADAM_EPS) + ADAM_WD * w)
    return delta, m, v


def reference(x, mem, attn_norm, a_w_in, a_mu, a_w0, a_w2, a_a0, a_a2, a_g2, a_k_k, a_k_a, a_r_k, a_lnx_w, a_lnx_b, a_w_out, kv_norm, kv_w, kv_k_norm, b_w_in, b_q_norm, b_w_out, mem_norm, mem_w_kv, mem_q_norm, mem_k_norm, ffn_norm, ffn_w_up, ffn_conv_w, ffn_conv_b, ffn_w_down, loss_target, m_attn_norm, m_a_w_in, m_a_mu, m_a_w0, m_a_w2, m_a_a0, m_a_a2, m_a_g2, m_a_k_k, m_a_k_a, m_a_r_k, m_a_lnx_w, m_a_lnx_b, m_a_w_out, m_kv_norm, m_kv_w, m_kv_k_norm, m_b_w_in, m_b_q_norm, m_b_w_out, m_mem_norm, m_mem_w_kv, m_mem_q_norm, m_mem_k_norm, m_ffn_norm, m_ffn_w_up, m_ffn_conv_w, m_ffn_conv_b, m_ffn_w_down, v_attn_norm, v_a_w_in, v_a_mu, v_a_w0, v_a_w2, v_a_a0, v_a_a2, v_a_g2, v_a_k_k, v_a_k_a, v_a_r_k, v_a_lnx_w, v_a_lnx_b, v_a_w_out, v_kv_norm, v_kv_w, v_kv_k_norm, v_b_w_in, v_b_q_norm, v_b_w_out, v_mem_norm, v_mem_w_kv, v_mem_q_norm, v_mem_k_norm, v_ffn_norm, v_ffn_w_up, v_ffn_conv_w, v_ffn_conv_b, v_ffn_w_down):
    given = dict(x=x, mem=mem, attn_norm=attn_norm, a_w_in=a_w_in, a_mu=a_mu, a_w0=a_w0, a_w2=a_w2, a_a0=a_a0, a_a2=a_a2, a_g2=a_g2, a_k_k=a_k_k, a_k_a=a_k_a, a_r_k=a_r_k, a_lnx_w=a_lnx_w, a_lnx_b=a_lnx_b, a_w_out=a_w_out, kv_norm=kv_norm, kv_w=kv_w, kv_k_norm=kv_k_norm, b_w_in=b_w_in, b_q_norm=b_q_norm, b_w_out=b_w_out, mem_norm=mem_norm, mem_w_kv=mem_w_kv, mem_q_norm=mem_q_norm, mem_k_norm=mem_k_norm, ffn_norm=ffn_norm, ffn_w_up=ffn_w_up, ffn_conv_w=ffn_conv_w, ffn_conv_b=ffn_conv_b, ffn_w_down=ffn_w_down, loss_target=loss_target, m_attn_norm=m_attn_norm, m_a_w_in=m_a_w_in, m_a_mu=m_a_mu, m_a_w0=m_a_w0, m_a_w2=m_a_w2, m_a_a0=m_a_a0, m_a_a2=m_a_a2, m_a_g2=m_a_g2, m_a_k_k=m_a_k_k, m_a_k_a=m_a_k_a, m_a_r_k=m_a_r_k, m_a_lnx_w=m_a_lnx_w, m_a_lnx_b=m_a_lnx_b, m_a_w_out=m_a_w_out, m_kv_norm=m_kv_norm, m_kv_w=m_kv_w, m_kv_k_norm=m_kv_k_norm, m_b_w_in=m_b_w_in, m_b_q_norm=m_b_q_norm, m_b_w_out=m_b_w_out, m_mem_norm=m_mem_norm, m_mem_w_kv=m_mem_w_kv, m_mem_q_norm=m_mem_q_norm, m_mem_k_norm=m_mem_k_norm, m_ffn_norm=m_ffn_norm, m_ffn_w_up=m_ffn_w_up, m_ffn_conv_w=m_ffn_conv_w, m_ffn_conv_b=m_ffn_conv_b, m_ffn_w_down=m_ffn_w_down, v_attn_norm=v_attn_norm, v_a_w_in=v_a_w_in, v_a_mu=v_a_mu, v_a_w0=v_a_w0, v_a_w2=v_a_w2, v_a_a0=v_a_a0, v_a_a2=v_a_a2, v_a_g2=v_a_g2, v_a_k_k=v_a_k_k, v_a_k_a=v_a_k_a, v_a_r_k=v_a_r_k, v_a_lnx_w=v_a_lnx_w, v_a_lnx_b=v_a_lnx_b, v_a_w_out=v_a_w_out, v_kv_norm=v_kv_norm, v_kv_w=v_kv_w, v_kv_k_norm=v_kv_k_norm, v_b_w_in=v_b_w_in, v_b_q_norm=v_b_q_norm, v_b_w_out=v_b_w_out, v_mem_norm=v_mem_norm, v_mem_w_kv=v_mem_w_kv, v_mem_q_norm=v_mem_q_norm, v_mem_k_norm=v_mem_k_norm, v_ffn_norm=v_ffn_norm, v_ffn_w_up=v_ffn_w_up, v_ffn_conv_w=v_ffn_conv_w, v_ffn_conv_b=v_ffn_conv_b, v_ffn_w_down=v_ffn_w_down)
    weights = {n: given[n] for n in TWIN_WEIGHTS}
    shared = {n: given[n] for n in SHARED_INPUTS}
    per_example = {n: given[n] for n in ['x', 'mem']}
    grad_fn = _jax.value_and_grad(_loss, argnums=(0, 1))

    def one_microbatch(ex, loss_target):
        ex = dict(ex)
        diff = ex.pop(TWIN_DIFF_INPUT)
        return grad_fn(weights, diff, {**shared, **ex}, loss_target)

    if N_MICROBATCH == 1:
        loss, (grad_w, grad_x) = one_microbatch(per_example, given["loss_target"])
    else:
        def body(carry, xs):
            loss_sum, grad_sum = carry
            l_k, (gw_k, gx_k) = one_microbatch(xs[0], xs[1])
            with _jax.named_scope("update"):
                return (loss_sum + l_k, _jax.tree.map(_jnp.add, grad_sum, gw_k)), gx_k

        init = (_jnp.zeros((), _jnp.float32), _jax.tree.map(_jnp.zeros_like, weights))
        (loss, grad_w), grad_x = _jax.lax.scan(body, init, (per_example, given["loss_target"]))
    with _jax.named_scope("update"):
        delta_w, new_m, new_v = {}, {}, {}
        for n in TWIN_WEIGHTS:
            delta_w[n], new_m[n], new_v[n] = _adamw(weights[n], grad_w[n], given["m_" + n], given["v_" + n])
    return (loss, grad_x, *[grad_w[n] for n in TWIN_WEIGHTS], *[delta_w[n] for n in TWIN_WEIGHTS],
            *[new_m[n] for n in TWIN_WEIGHTS], *[new_v[n] for n in TWIN_WEIGHTS])
```

```python
import functools
import math

import jax
import jax.numpy as jnp
import numpy as np
from jax import lax
from jax.experimental import pallas as pl
from jax.experimental.pallas import tpu as pltpu

F32 = jnp.float32
BF16 = jnp.bfloat16
HI = lax.Precision.HIGHEST

N_DEV = 8
D_MODEL = 1024
HEAD_DIM = 64
N_MEM = 256
MEM_HEADS = 4
MEM_WIDTH = 256
RWKV_HEADS = 12
RWKV_WIDTH = 768
SHIFT_WIDTH = 2560
DIL_GROUPS = ((128, 1), (512, 4), (2048, 16))
DIL_BLOCK = 128
DIL_WIDTH = 768
D_FF = 2816
RMS_EPS = 1e-6
LNX_EPS = 64e-5
NEG_INF = -1e30
ROPE_THETA = 10000.0
ADAM_LR, ADAM_B1, ADAM_B2, ADAM_EPS, ADAM_WD, ADAM_STEP = 0.001, 0.9, 0.999, 1e-08, 0.01, 10

CHUNK = 64
VMEM_LIMIT_V7X = 48 * 1024 * 1024


def _cparams(sem):
    return pltpu.CompilerParams(dimension_semantics=sem, vmem_limit_bytes=VMEM_LIMIT_V7X)


def _pick(n, cands):
    for c in cands:
        if n % c == 0:
            return c
    return n


def _dg(a, b, ca, cb, batch):
    dims = (((ca,), (cb,)), ((0,), (0,))) if batch else (((ca,), (cb,)), ((), ()))
    return lax.dot_general(a.astype(BF16), b.astype(BF16), dims, preferred_element_type=F32)


@jax.custom_vjp
def mm_nn(a, b):
    n = a.ndim
    return _dg(a, b, n - 1, n - 2, n == 3)


def _mm_nn_fwd(a, b):
    return mm_nn(a, b), (a, b)


def _mm_nn_bwd(res, g):
    a, b = res
    n = a.ndim
    return _dg(g, b, n - 1, n - 1, n == 3), _dg(a, g, n - 2, n - 2, n == 3)


mm_nn.defvjp(_mm_nn_fwd, _mm_nn_bwd)


@jax.custom_vjp
def mm_nt(a, b):
    n = a.ndim
    return _dg(a, b, n - 1, n - 1, n == 3)


def _mm_nt_fwd(a, b):
    return mm_nt(a, b), (a, b)


def _mm_nt_bwd(res, g):
    a, b = res
    n = a.ndim
    return _dg(g, b, n - 1, n - 2, n == 3), _dg(g, a, n - 2, n - 2, n == 3)


mm_nt.defvjp(_mm_nt_fwd, _mm_nt_bwd)


def mmh(a, b):
    n = a.ndim
    dims = (((n - 1,), (n - 2,)), ((0,), (0,))) if n == 3 else (((1,), (0,)), ((), ()))
    return lax.dot_general(a, b, dims, precision=HI, preferred_element_type=F32)


def mmh_nt(a, b):
    n = a.ndim
    dims = (((n - 1,), (n - 1,)), ((0,), (0,))) if n == 3 else (((1,), (1,)), ((), ()))
    return lax.dot_general(a, b, dims, precision=HI, preferred_element_type=F32)


def mmh_tn(a, b):
    n = a.ndim
    dims = (((n - 2,), (n - 2,)), ((0,), (0,))) if n == 3 else (((0,), (0,)), ((), ()))
    return lax.dot_general(a, b, dims, precision=HI, preferred_element_type=F32)


def matmul(a, b, mode, name, residual=None):
    if mode == "nn":
        (M, K), (_, N) = a.shape, b.shape
    elif mode == "nt":
        (M, K), (N, _) = a.shape, b.shape
    else:
        (K, M), (_, N) = a.shape, b.shape
    tm = _pick(M, (512, 256, 128))
    tn = _pick(N, (512, 256, 128))
    tk = _pick(K, (1024, 512, 256, 128))
    nk = K // tk
    if mode == "nn":
        a_spec = pl.BlockSpec((tm, tk), lambda i, j, k: (i, k))
        b_spec = pl.BlockSpec((tk, tn), lambda i, j, k: (k, j))
        dims = (((1,), (0,)), ((), ()))
    elif mode == "nt":
        a_spec = pl.BlockSpec((tm, tk), lambda i, j, k: (i, k))
        b_spec = pl.BlockSpec((tn, tk), lambda i, j, k: (j, k))
        dims = (((1,), (1,)), ((), ()))
    else:
        a_spec = pl.BlockSpec((tk, tm), lambda i, j, k: (k, i))
        b_spec = pl.BlockSpec((tk, tn), lambda i, j, k: (k, j))
        dims = (((0,), (0,)), ((), ()))
    o_spec = pl.BlockSpec((tm, tn), lambda i, j, k: (i, j))
    has_res = residual is not None

    def body(*refs):
        if has_res:
            a_ref, b_ref, r_ref, o_ref, acc_ref = refs
        else:
            a_ref, b_ref, o_ref, acc_ref = refs
        k = pl.program_id(2)

        @pl.when(k == 0)
        def _():
            acc_ref[...] = jnp.zeros_like(acc_ref)

        acc_ref[...] += lax.dot_general(a_ref[...].astype(BF16), b_ref[...].astype(BF16), dims,
                                        preferred_element_type=F32)

        @pl.when(k == nk - 1)
        def _():
            if has_res:
                o_ref[...] = acc_ref[...] + r_ref[...]
            else:
                o_ref[...] = acc_ref[...]

    ins = [a, b] + ([residual] if has_res else [])
    in_specs = [a_spec, b_spec] + ([o_spec] if has_res else [])
    return pl.pallas_call(
        body, grid=(M // tm, N // tn, nk), in_specs=in_specs, out_specs=o_spec,
        out_shape=jax.ShapeDtypeStruct((M, N), F32), scratch_shapes=[pltpu.VMEM((tm, tn), F32)],
        compiler_params=_cparams(("parallel", "parallel", "arbitrary")), name=name)(*ins)


def _tok_spec(shape, tile):
    nd = len(shape)
    return pl.BlockSpec(shape[:-2] + (tile, shape[-1]), lambda i: (0,) * (nd - 2) + (i, 0))


def _full_spec(shape):
    nd = len(shape)
    return pl.BlockSpec(shape, lambda i: (0,) * nd)


def _blk(x, tile):
    return jax.ShapeDtypeStruct(x.shape[:-2] + (tile, x.shape[-1]), x.dtype)


def stage_fwd(f, xs, ps, cts, cfs, tile, name, out_dtypes=None):
    tiled, full = list(xs) + list(cts), list(ps) + list(cfs)
    nx, np_, nct = len(xs), len(ps), len(cts)
    T = tiled[0].shape[-2]
    order = lambda t, fl: list(t[:nx]) + list(fl[:np_]) + list(t[nx:]) + list(fl[np_:])
    out_avals = jax.eval_shape(f, *order([_blk(x, tile) for x in tiled], full))
    if out_dtypes is None:
        out_dtypes = [o.dtype for o in out_avals]
    out_shape = [jax.ShapeDtypeStruct(o.shape[:-2] + (T, o.shape[-1]), dt) for o, dt in zip(out_avals, out_dtypes)]
    nt, nf = len(tiled), len(full)

    def body(*refs):
        tv = [r[...] for r in refs[:nt]]
        fv = [r[...] for r in refs[nt:nt + nf]]
        res = f(*order(tv, fv))
        for o_ref, r in zip(refs[nt + nf:], res):
            o_ref[...] = r.astype(o_ref.dtype)

    return pl.pallas_call(
        body, grid=(T // tile,),
        in_specs=[_tok_spec(x.shape, tile) for x in tiled] + [_full_spec(p.shape) for p in full],
        out_specs=[_tok_spec(o.shape, tile) for o in out_shape], out_shape=out_shape,
        compiler_params=_cparams(("parallel",)), name=name)(*tiled, *full)


def stage_bwd(f, xs, ps, cts, cfs, gs, tile, name):
    gs = [list(g) if isinstance(g, (list, tuple)) else [g] for g in gs]
    g_flat = [a for g in gs for a in g]
    tiled, full = list(xs) + list(cts) + g_flat, list(ps) + list(cfs)
    nx, np_, nct = len(xs), len(ps), len(cts)
    T = tiled[0].shape[-2]
    nt, nf = len(tiled), len(full)
    out_shape = [jax.ShapeDtypeStruct(x.shape, F32) for x in xs] + [jax.ShapeDtypeStruct(p.shape, F32) for p in ps]

    def body(*refs):
        tv = [r[...] for r in refs[:nt]]
        fv = [r[...] for r in refs[nt:nt + nf]]
        outs = refs[nt + nf:]
        xv, ctv, gparts = tv[:nx], tv[nx:nx + nct], tv[nx + nct:]
        gv = []
        for g in gs:
            gv.append(functools.reduce(lambda a, b: a + b, gparts[:len(g)]))
            gparts = gparts[len(g):]
        pv, cfv = fv[:np_], fv[np_:]
        _, vjp = jax.vjp(lambda *xp: f(*xp, *ctv, *cfv), *xv, *pv)
        d = vjp(tuple(gv))
        for o_ref, r in zip(outs[:nx], d[:nx]):
            o_ref[...] = r

        @pl.when(pl.program_id(0) == 0)
        def _():
            for o_ref in outs[nx:]:
                o_ref[...] = jnp.zeros_like(o_ref)

        for o_ref, r in zip(outs[nx:], d[nx:]):
            o_ref[...] += r

    res = pl.pallas_call(
        body, grid=(T // tile,),
        in_specs=[_tok_spec(x.shape, tile) for x in tiled] + [_full_spec(p.shape) for p in full],
        out_specs=[_tok_spec(x.shape, tile) for x in xs] + [_full_spec(p.shape) for p in ps], out_shape=out_shape,
        compiler_params=_cparams(("arbitrary",)), name=name)(*tiled, *full)
    return list(res[:nx]), list(res[nx:])


def _rms(x, g, eps=RMS_EPS):
    return x * lax.rsqrt(jnp.mean(x * x, axis=-1, keepdims=True) + eps) * g


def f_rmsnorm(x, g):
    return (_rms(x, g),)


def f_rmsnorm_res(x, g):
    return _rms(x, g), x


def f_rmsnorm2(x, g1, g2):
    n = x * lax.rsqrt(jnp.mean(x * x, axis=-1, keepdims=True) + RMS_EPS)
    return n * g1, n * g2


def f_rmsnorm2_res(x, g1, g2):
    return f_rmsnorm2(x, g1, g2) + (x,)


def _sigmoid(x):
    return 1.0 / (1.0 + jnp.exp(-x))


def _softplus(x):
    return jnp.maximum(x, 0.0) + jnp.log(1.0 + jnp.exp(-jnp.abs(x)))


def f_rwkv_pre(pr, pk, pv, pwd, pad, pgd, qr, qk, qv, qwd, qad, qgd,
               mu_r, mu_k, mu_v, mu_wd, mu_ad, mu_gd, w0, w2, a0, a2, g2, k_k, k_a, seg, seg_t):
    xr = pr + (qr - pr) * mu_r
    xk = pk + (qk - pk) * mu_k
    xv = pv + (qv - pv) * mu_v
    xwd = pwd + (qwd - pwd) * mu_wd
    xad = pad + (qad - pad) * mu_ad
    xgd = pgd + (qgd - pgd) * mu_gd
    w_log = -_softplus(-(w0 + mm_nn(jnp.tanh(xwd), w2))) - 0.5
    lw = -jnp.exp(w_log)
    a = _sigmoid(a0 + mm_nn(xad, a2))
    g = mm_nn(_sigmoid(xgd), g2)
    kkr = xk * k_k
    inv = lax.rsqrt(jnp.maximum(mmh(kkr * kkr, seg), 1e-24))
    kk = kkr * mmh(inv, seg_t)
    k2 = xk * (1.0 + (a - 1.0) * k_a)
    return xr, lw, k2, xv, kk, kk * a, g


def f_rwkv_post(y, r, k2, v, g, lnx_w, lnx_b, r_k, seg, seg_t):
    inv_n = 1.0 / HEAD_DIM
    m = mmh(mmh(y, seg) * inv_n, seg_t)
    yc = y - m
    rstd = lax.rsqrt(mmh(yc * yc, seg) * inv_n + LNX_EPS)
    yn = yc * mmh(rstd, seg_t) * lnx_w + lnx_b
    bonus = mmh(mmh(r * k2 * r_k, seg), seg_t) * v
    return ((yn + bonus) * g,)


def f_headnorm(z, g):
    return (_rms(z, g),)


def f_qkprep(z, g, cos, sin, rot):
    h, t, d = z.shape
    zn = _rms(z, g)
    zr = mmh(zn.reshape(h * t, d), rot).reshape(h, t, d)
    return (zn * cos + zr * sin,)


def f_memattn(q, k, v, q_norm):
    qh = _rms(q, q_norm)
    s = mm_nt(qh, k) * (1.0 / math.sqrt(HEAD_DIM))
    s = s - jnp.max(s, axis=-1, keepdims=True)
    p = jnp.exp(s)
    p = p / jnp.sum(p, axis=-1, keepdims=True)
    return (mm_nn(p, v),)


def f_mix(o1, o2, o3, l1, l2, l3):
    mx = jnp.maximum(jnp.maximum(l1, l2), l3)
    e1, e2, e3 = jnp.exp(l1 - mx), jnp.exp(l2 - mx), jnp.exp(l3 - mx)
    return ((e1 * o1 + e2 * o2 + e3 * o3) / (e1 + e2 + e3),)


def _chunk_masks(L):
    t = lax.broadcasted_iota(jnp.int32, (L, L), 0)
    s = lax.broadcasted_iota(jnp.int32, (L, L), 1)
    return t, s


def f_rwkv_chunk(h0, r, lw, k, v, kk, b):
    H, L, _ = r.shape
    t, s = _chunk_masks(L)
    one = jnp.ones((), F32)
    incl = jnp.where(t >= s, one, 0.0)
    strict = jnp.where(t > s, one, 0.0)
    eye = jnp.where(t == s, one, 0.0)
    cum = mmh(jnp.broadcast_to(incl, (H, L, L)), lw)
    w_in = jnp.exp(cum)
    w_ex = jnp.exp(cum - lw)
    w_inv = jnp.exp(-cum)
    rt, kkt, kt, bt = r * w_in, kk * w_ex, k * w_inv, b * w_inv
    a_b = mmh_nt(kkt, bt) * strict
    a_k = mmh_nt(kkt, kt) * strict
    m_k = mmh_nt(rt, kt) * incl
    m_b = mmh_nt(rt, bt) * incl
    blk = lambda sh: jnp.where((t >> sh) == (s >> sh), one, 0.0)
    n0 = a_b * blk(3)
    x = eye - n0
    n2 = mmh(n0, n0)
    x = x + mmh(x, n2)
    x = x + mmh(x, mmh(n2, n2))
    for sh in (3, 4, 5):
        if (1 << sh) >= L:
            break
        off = a_b * (blk(sh + 1) - blk(sh))
        x = x - mmh(x, mmh(off, x))
    p = mmh(x, kkt)
    q = mmh(x, mmh(a_k, v))
    u = mmh(p, h0) + q
    y = mmh(rt, h0) + mmh(m_k, v) - mmh(m_b, u)
    w_last = jnp.exp(jnp.sum(lw, axis=1))
    h1 = w_last[:, :, None] * (h0 + mmh_tn(kt, v) - mmh_tn(bt, u))
    return y, h1


def rwkv_scan_fwd(r, lw, k, v, kk, b):
    H, T, N = r.shape
    nc = T // CHUNK
    seq = pl.BlockSpec((H, CHUNK, N), lambda c: (0, c, 0))

    def body(r_ref, lw_ref, k_ref, v_ref, kk_ref, b_ref, y_ref, hs_ref, h_scr):
        @pl.when(pl.program_id(0) == 0)
        def _():
            h_scr[...] = jnp.zeros_like(h_scr)

        h0 = h_scr[...]
        hs_ref[0] = h0
        y, h1 = f_rwkv_chunk(h0, r_ref[...], lw_ref[...], k_ref[...], v_ref[...], kk_ref[...], b_ref[...])
        y_ref[...] = y
        h_scr[...] = h1

    return pl.pallas_call(
        body, grid=(nc,), in_specs=[seq] * 6,
        out_specs=[seq, pl.BlockSpec((1, H, N, N), lambda c: (c, 0, 0, 0))],
        out_shape=[jax.ShapeDtypeStruct((H, T, N), F32), jax.ShapeDtypeStruct((nc, H, N, N), F32)],
        scratch_shapes=[pltpu.VMEM((H, N, N), F32)],
        compiler_params=_cparams(("arbitrary",)), name="rwkv_scan_fwd")(r, lw, k, v, kk, b)


def rwkv_scan_bwd(r, lw, k, v, kk, b, hs, dy):
    H, T, N = r.shape
    nc = T // CHUNK
    seq = pl.BlockSpec((H, CHUNK, N), lambda c: (0, nc - 1 - c, 0))

    def body(r_ref, lw_ref, k_ref, v_ref, kk_ref, b_ref, hs_ref, dy_ref, *rest):
        outs, dh_scr = rest[:6], rest[6]

        @pl.when(pl.program_id(0) == 0)
        def _():
            dh_scr[...] = jnp.zeros_like(dh_scr)

        _, vjp = jax.vjp(f_rwkv_chunk, hs_ref[0], r_ref[...], lw_ref[...], k_ref[...], v_ref[...], kk_ref[...],
                         b_ref[...])
        d = vjp((dy_ref[...], dh_scr[...]))
        dh_scr[...] = d[0]
        for o_ref, g in zip(outs, d[1:]):
            o_ref[...] = g

    return pl.pallas_call(
        body, grid=(nc,), in_specs=[seq] * 6 + [pl.BlockSpec((1, H, N, N), lambda c: (nc - 1 - c, 0, 0, 0)), seq],
        out_specs=[seq] * 6, out_shape=[jax.ShapeDtypeStruct((H, T, N), F32)] * 6,
        scratch_shapes=[pltpu.VMEM((H, N, N), F32)],
        compiler_params=_cparams(("arbitrary",)), name="rwkv_scan_bwd")(r, lw, k, v, kk, b, hs, dy)


DIL_ROWS = 4


def _f_dilattn(has_prev, q, kc, kp, vc, vp):
    scale = 1.0 / math.sqrt(HEAD_DIM)
    sc = mm_nt(q, kc) * scale
    sp = mm_nt(q, kp) * scale
    i = lax.broadcasted_iota(jnp.int32, (DIL_BLOCK, DIL_BLOCK), 0)
    j = lax.broadcasted_iota(jnp.int32, (DIL_BLOCK, DIL_BLOCK), 1)
    sc = jnp.where(j <= i, sc, NEG_INF)
    sp = jnp.where(jnp.logical_and(i <= j, has_prev), sp, NEG_INF)
    mx = jnp.maximum(jnp.max(sc, axis=-1, keepdims=True), jnp.max(sp, axis=-1, keepdims=True))
    pc, pp = jnp.exp(sc - mx), jnp.exp(sp - mx)
    den = jnp.sum(pc, axis=-1, keepdims=True) + jnp.sum(pp, axis=-1, keepdims=True)
    o = (mm_nn(pc, vc) + mm_nn(pp, vp)) / den
    return o, mx + jnp.log(den)


def _dil_specs(R, U):
    cur = pl.BlockSpec((DIL_ROWS, DIL_BLOCK, HEAD_DIM), lambda g, n: (g, n, 0))
    prev = pl.BlockSpec((DIL_ROWS, DIL_BLOCK, HEAD_DIM), lambda g, n: (g, jnp.maximum(n - 1, 0), 0))
    lse = pl.BlockSpec((DIL_ROWS, DIL_BLOCK, 1), lambda g, n: (g, n, 0))
    return cur, prev, lse, (R // DIL_ROWS, U // DIL_BLOCK)


def dil_fwd(q, k, v, name):
    R, U, _ = q.shape
    cur, prev, lse, grid = _dil_specs(R, U)

    def body(q_ref, kc_ref, kp_ref, vc_ref, vp_ref, o_ref, l_ref):
        o, l = _f_dilattn(pl.program_id(1) > 0, q_ref[...], kc_ref[...], kp_ref[...], vc_ref[...], vp_ref[...])
        o_ref[...] = o
        l_ref[...] = l

    return pl.pallas_call(
        body, grid=grid, in_specs=[cur, cur, prev, cur, prev], out_specs=[cur, lse],
        out_shape=[jax.ShapeDtypeStruct((R, U, HEAD_DIM), F32), jax.ShapeDtypeStruct((R, U, 1), F32)],
        compiler_params=_cparams(("parallel", "parallel")), name=name)(q, k, k, v, v)


def dil_bwd(q, k, v, do, dl, name):
    R, U, _ = q.shape
    cur, prev, lse, grid = _dil_specs(R, U)

    def body(q_ref, kc_ref, kp_ref, vc_ref, vp_ref, do_ref, dl_ref, *outs):
        f = functools.partial(_f_dilattn, pl.program_id(1) > 0)
        _, vjp = jax.vjp(f, q_ref[...], kc_ref[...], kp_ref[...], vc_ref[...], vp_ref[...])
        for o_ref, g in zip(outs, vjp((do_ref[...], dl_ref[...]))):
            o_ref[...] = g

    return pl.pallas_call(
        body, grid=grid, in_specs=[cur, cur, prev, cur, prev, cur, lse], out_specs=[cur] * 5,
        out_shape=[jax.ShapeDtypeStruct((R, U, HEAD_DIM), F32)] * 5,
        compiler_params=_cparams(("parallel", "parallel")), name=name)(q, k, k, v, v, do, dl)


CONV_TILE = 128


def _conv3(u, h6, h7, w, b):
    rows = lax.broadcasted_iota(jnp.int32, (u.shape[0], 1), 0)
    s1 = jnp.where(rows == 0, h7, pltpu.roll(u, 1, 0))
    s2 = jnp.where(rows == 0, h6, jnp.where(rows == 1, h7, pltpu.roll(u, 2, 0)))
    return b + w[0:1] * s2 + w[1:2] * s1 + w[2:3] * u, s1, s2


def _conv_halves(u_ref, h_ref, cw_ref, cb_ref):
    F = D_FF
    first = pl.program_id(0) > 0
    res = []
    for lo in (0, F):
        h = h_ref[:, lo:lo + F]
        h6 = jnp.where(first, h[6:7], 0.0)
        h7 = jnp.where(first, h[7:8], 0.0)
        u = u_ref[:, lo:lo + F]
        res.append((u,) + _conv3(u, h6, h7, cw_ref[:, lo:lo + F], cb_ref[:, lo:lo + F]))
    return res


def _halo_before(C):
    return pl.BlockSpec((8, C), lambda i: (jnp.maximum(i * (CONV_TILE // 8) - 1, 0), 0))


def convgate_fwd(u, cw, cb, name):
    T, C = u.shape
    F = C // 2

    def body(u_ref, h_ref, cw_ref, cb_ref, z_ref):
        (_, cg, _, _), (_, cv, _, _) = _conv_halves(u_ref, h_ref, cw_ref, cb_ref)
        z_ref[...] = (cg * _sigmoid(cg) * cv).astype(BF16)

    return pl.pallas_call(
        body, grid=(T // CONV_TILE,),
        in_specs=[pl.BlockSpec((CONV_TILE, C), lambda i: (i, 0)), _halo_before(C), _full_spec(cw.shape), _full_spec(cb.shape)],
        out_specs=pl.BlockSpec((CONV_TILE, F), lambda i: (i, 0)), out_shape=jax.ShapeDtypeStruct((T, F), BF16),
        compiler_params=_cparams(("parallel",)), name=name)(u, u, cw, cb)


def convgate_bwd_c(u, cw, cb, dz, name):
    T, C = u.shape
    F = C // 2

    def body(u_ref, h_ref, cw_ref, cb_ref, dz_ref, dc_ref, dcw_ref, dcb_ref):
        (ug, cg, g1, g2), (uv, cv, v1, v2) = _conv_halves(u_ref, h_ref, cw_ref, cb_ref)
        dz = dz_ref[...]
        sg = _sigmoid(cg)
        dgate = dz * cv * sg * (1.0 + cg * (1.0 - sg))
        dval = dz * cg * sg
        dc_ref[:, :F] = dgate
        dc_ref[:, F:] = dval

        @pl.when(pl.program_id(0) == 0)
        def _():
            dcw_ref[...] = jnp.zeros_like(dcw_ref)
            dcb_ref[...] = jnp.zeros_like(dcb_ref)

        for lo, d, u0, s1, s2 in ((0, dgate, ug, g1, g2), (F, dval, uv, v1, v2)):
            dcb_ref[:, lo:lo + F] += jnp.sum(d, axis=0, keepdims=True)
            dcw_ref[0:1, lo:lo + F] += jnp.sum(d * s2, axis=0, keepdims=True)
            dcw_ref[1:2, lo:lo + F] += jnp.sum(d * s1, axis=0, keepdims=True)
            dcw_ref[2:3, lo:lo + F] += jnp.sum(d * u0, axis=0, keepdims=True)

    return pl.pallas_call(
        body, grid=(T // CONV_TILE,),
        in_specs=[pl.BlockSpec((CONV_TILE, C), lambda i: (i, 0)), _halo_before(C), _full_spec(cw.shape), _full_spec(cb.shape),
                  pl.BlockSpec((CONV_TILE, F), lambda i: (i, 0))],
        out_specs=[pl.BlockSpec((CONV_TILE, C), lambda i: (i, 0)), _full_spec(cw.shape), _full_spec(cb.shape)],
        out_shape=[jax.ShapeDtypeStruct((T, C), F32), jax.ShapeDtypeStruct(cw.shape, F32), jax.ShapeDtypeStruct(cb.shape, F32)],
        compiler_params=_cparams(("arbitrary",)), name=name)(u, u, cw, cb, dz)


def convgate_bwd_u(dc, cw, name):
    T, C = dc.shape
    n = T // CONV_TILE

    def body(dc_ref, nx_ref, cw_ref, du_ref):
        last = pl.program_id(0) < n - 1
        rows = lax.broadcasted_iota(jnp.int32, (CONV_TILE, 1), 0)
        for lo in (0, C // 2):
            sl = slice(lo, lo + C // 2)
            d = dc_ref[:, sl]
            n0 = jnp.where(last, nx_ref[0:1, sl], 0.0)
            n1 = jnp.where(last, nx_ref[1:2, sl], 0.0)
            up1 = jnp.where(rows == CONV_TILE - 1, n0, pltpu.roll(d, CONV_TILE - 1, 0))
            up2 = jnp.where(rows == CONV_TILE - 1, n1, jnp.where(rows == CONV_TILE - 2, n0, pltpu.roll(d, CONV_TILE - 2, 0)))
            du_ref[:, sl] = cw_ref[2:3, sl] * d + cw_ref[1:2, sl] * up1 + cw_ref[0:1, sl] * up2

    nxt = pl.BlockSpec((8, C), lambda i: (jnp.minimum((i + 1) * (CONV_TILE // 8), T // 8 - 1), 0))
    return pl.pallas_call(
        body, grid=(n,), in_specs=[pl.BlockSpec((CONV_TILE, C), lambda i: (i, 0)), nxt, _full_spec(cw.shape)],
        out_specs=pl.BlockSpec((CONV_TILE, C), lambda i: (i, 0)), out_shape=jax.ShapeDtypeStruct((T, C), F32),
        compiler_params=_cparams(("parallel",)), name=name)(dc, dc, cw)


def loss_head(y, tgt):
    T, D = y.shape
    tile = 256

    def body(y_ref, t_ref, l_ref, d_ref):
        d = y_ref[...] - t_ref[...]
        d_ref[...] = d * (1.0 / D)

        @pl.when(pl.program_id(0) == 0)
        def _():
            l_ref[...] = jnp.zeros_like(l_ref)

        l_ref[...] += (0.5 / D) * jnp.sum(d * d)

    row = pl.BlockSpec((tile, D), lambda i: (i, 0))
    return pl.pallas_call(
        body, grid=(T // tile,), in_specs=[row, row], out_specs=[pl.BlockSpec((8, 128), lambda i: (0, 0)), row],
        out_shape=[jax.ShapeDtypeStruct((8, 128), F32), jax.ShapeDtypeStruct((T, D), F32)],
        compiler_params=_cparams(("arbitrary",)), name="loss_head")(y, tgt)


def adamw(gparts, w, m, v):
    _, R, C = gparts.shape
    tile = _pick(R, (512, 256, 128, 64, 32, 16, 8))
    c1 = 1.0 / (1.0 - ADAM_B1 ** ADAM_STEP)
    c2 = 1.0 / (1.0 - ADAM_B2 ** ADAM_STEP)

    def body(g_ref, w_ref, m_ref, v_ref, go_ref, d_ref, mo_ref, vo_ref):
        g = g_ref[0]
        for s in range(1, N_DEV):
            g = g + g_ref[s]
        m1 = ADAM_B1 * m_ref[...] + (1.0 - ADAM_B1) * g
        v1 = ADAM_B2 * v_ref[...] + (1.0 - ADAM_B2) * (g * g)
        go_ref[...] = g
        mo_ref[...] = m1
        vo_ref[...] = v1
        d_ref[...] = -ADAM_LR * ((m1 * c1) / (jnp.sqrt(v1 * c2) + ADAM_EPS) + ADAM_WD * w_ref[...])

    row = pl.BlockSpec((tile, C), lambda i: (i, 0))
    return pl.pallas_call(
        body, grid=(R // tile,), in_specs=[pl.BlockSpec((N_DEV, tile, C), lambda i: (0, i, 0)), row, row, row],
        out_specs=[row] * 4, out_shape=[jax.ShapeDtypeStruct((R, C), F32)] * 4,
        compiler_params=_cparams(("parallel",)), name="adamw")(gparts, w, m, v)


def _peers():
    x, y, c = lax.axis_index("x"), lax.axis_index("y"), lax.axis_index("c")
    peers = []
    for k in range(1, N_DEV):
        px = 1 - x if k & 4 else x
        py = 1 - y if k & 2 else y
        pc = 1 - c if k & 1 else c
        peers.append(((px, py, pc), 4 * px + 2 * py + pc))
    return 4 * x + 2 * y + c, peers


_ANY = pl.BlockSpec(memory_space=pl.ANY)


def all_gather(xs, name):
    n = len(xs)

    def body(*refs):
        x_refs, o_refs = refs[:n], refs[n:2 * n]
        send_sems, recv_sems, local_sems = refs[2 * n:]
        me, peers = _peers()
        local = [pltpu.make_async_copy(x_refs[a], o_refs[a].at[me], local_sems.at[a]) for a in range(n)]
        for cp in local:
            cp.start()

        def copy(a, k, slot):
            return pltpu.make_async_remote_copy(
                src_ref=x_refs[a], dst_ref=o_refs[a].at[slot], send_sem=send_sems.at[a * 7 + k], recv_sem=recv_sems.at[a * 7 + k],
                device_id=peers[k][0], device_id_type=pl.DeviceIdType.MESH)

        sends = [copy(a, k, me) for a in range(n) for k in range(7)]
        for cp in sends:
            cp.start()
        for a in range(n):
            for k in range(7):
                copy(a, k, peers[k][1]).wait_recv()
        for cp in sends:
            cp.wait_send()
        for cp in local:
            cp.wait()

    return pl.pallas_call(
        body, in_specs=[_ANY] * n, out_specs=[_ANY] * n,
        out_shape=[jax.ShapeDtypeStruct((N_DEV,) + x.shape, x.dtype) for x in xs],
        scratch_shapes=[pltpu.SemaphoreType.DMA((7 * n,)), pltpu.SemaphoreType.DMA((7 * n,)), pltpu.SemaphoreType.DMA((n,))],
        name=name)(*xs)


def all_to_all(x, name):
    def body(x_ref, o_ref, send_sems, recv_sems, local_sem):
        me, peers = _peers()
        local = pltpu.make_async_copy(x_ref.at[me], o_ref.at[me], local_sem)
        local.start()

        def copy(k, src_slot, dst_slot):
            return pltpu.make_async_remote_copy(
                src_ref=x_ref.at[src_slot], dst_ref=o_ref.at[dst_slot], send_sem=send_sems.at[k], recv_sem=recv_sems.at[k],
                device_id=peers[k][0], device_id_type=pl.DeviceIdType.MESH)

        sends = [copy(k, peers[k][1], me) for k in range(7)]
        for cp in sends:
            cp.start()
        for k in range(7):
            copy(k, me, peers[k][1]).wait_recv()
        for cp in sends:
            cp.wait_send()
        local.wait()

    return pl.pallas_call(
        body, in_specs=[_ANY], out_specs=_ANY, out_shape=jax.ShapeDtypeStruct(x.shape, x.dtype),
        scratch_shapes=[pltpu.SemaphoreType.DMA((7,)), pltpu.SemaphoreType.DMA((7,)), pltpu.SemaphoreType.DMA],
        name=name)(x)


def _heads(z, h):
    return z.reshape(z.shape[0], h, HEAD_DIM).transpose(1, 0, 2)


def _unheads(z):
    return z.transpose(1, 0, 2).reshape(z.shape[1], z.shape[0] * HEAD_DIM)


def _regroup(z, d):
    h, T, w = z.shape
    return z.reshape(h, T // d, d, w).transpose(0, 2, 1, 3).reshape(h * d, T // d, w)


def _ungroup(z, d):
    hd, U, w = z.shape
    return z.reshape(hd // d, d, U, w).transpose(0, 2, 1, 3).reshape(hd // d, U * d, w)


def _shift_down(z):
    return jnp.concatenate([jnp.zeros_like(z[:1]), z[:-1]], axis=0)


def _shift_up(z):
    return jnp.concatenate([z[1:], jnp.zeros_like(z[:1])], axis=0)


def _segments(width):
    seg = np.zeros((width, 128), np.float32)
    seg[np.arange(width), np.arange(width) // HEAD_DIM] = 1.0
    return jnp.asarray(seg), jnp.asarray(seg.T)


def _rope_consts(T):
    inv = ROPE_THETA ** (-jnp.arange(0, HEAD_DIM, 2, dtype=F32) / HEAD_DIM)
    ang = jnp.arange(T, dtype=F32)[:, None] * inv[None, :]
    cos, sin = jnp.cos(ang), jnp.sin(ang)
    rot = np.zeros((HEAD_DIM, HEAD_DIM), np.float32)
    half = HEAD_DIM // 2
    rot[np.arange(half) + half, np.arange(half)] = -1.0
    rot[np.arange(half), np.arange(half) + half] = 1.0
    return jnp.concatenate([cos, cos], axis=1), jnp.concatenate([sin, sin], axis=1), jnp.asarray(rot)


RW_CUTS = (0, 768, 1536, 2304, 2368, 2432, 2560)
RW_TILE = 128
ROW_TILE = 256


def _local_step(x0, memx, tgt, P):
    T = x0.shape[0]
    G = {}
    seg, seg_t = _segments(RWKV_WIDTH)
    cos, sin, rot = _rope_consts(T)
    row = lambda v: v.reshape(1, -1)

    def mem_fwd(i, qmem):
        memn = stage_fwd(f_rmsnorm, [memx], [P["mem_norm"][i:i + 1]], [], [], N_MEM, f"mem{i}_norm", [BF16])[0]
        kvm = matmul(memn, P["mem_w_kv"][i], "nn", f"mem{i}_kv")
        kraw, vm = _heads(kvm[:, :MEM_WIDTH], MEM_HEADS), _heads(kvm[:, MEM_WIDTH:], MEM_HEADS)
        km = stage_fwd(f_headnorm, [kraw], [P["mem_k_norm"][i:i + 1]], [], [], N_MEM, f"mem{i}_knorm")[0]
        qh = _heads(qmem, MEM_HEADS)
        om = stage_fwd(f_memattn, [qh], [km, vm, P["mem_q_norm"][i:i + 1]], [], [], ROW_TILE, f"mem{i}_attn")[0]
        return _unheads(om), (memn, kraw, vm, km, qh)

    def mem_bwd(i, saved, dymem):
        memn, kraw, vm, km, qh = saved
        (dqh,), (dkm, dvm, g_qn) = stage_bwd(f_memattn, [qh], [km, vm, P["mem_q_norm"][i:i + 1]], [], [],
                                             [_heads(dymem, MEM_HEADS)], ROW_TILE, f"mem{i}_attn_bwd")
        (dkraw,), (g_kn,) = stage_bwd(f_headnorm, [kraw], [P["mem_k_norm"][i:i + 1]], [], [], [dkm], N_MEM, f"mem{i}_knorm_bwd")
        dkvm = jnp.concatenate([_unheads(dkraw), _unheads(dvm)], axis=1)
        g_w = matmul(memn, dkvm, "tn", f"mem{i}_kv_dw")
        dmemn = matmul(dkvm, P["mem_w_kv"][i], "nt", f"mem{i}_kv_dx")
        _, (g_mn,) = stage_bwd(f_rmsnorm, [memx], [P["mem_norm"][i:i + 1]], [], [], [dmemn], N_MEM, f"mem{i}_norm_bwd")
        return _unheads(dqh), g_mn, g_w, g_qn, g_kn

    def ffn_fwd(i, xin):
        hn = stage_fwd(f_rmsnorm, [xin], [P["ffn_norm"][i:i + 1]], [], [], ROW_TILE, f"ffn{i}_norm", [BF16])[0]
        u = matmul(hn, P["ffn_w_up"][i], "nn", f"ffn{i}_up")
        z = convgate_fwd(u, P["ffn_conv_w"][i], P["ffn_conv_b"][i:i + 1], f"ffn{i}_conv")
        return matmul(z, P["ffn_w_down"][i], "nn", f"ffn{i}_down", residual=xin), (hn, u, z)

    def ffn_bwd(i, xin, saved, dxo):
        hn, u, z = saved
        dz = matmul(dxo, P["ffn_w_down"][i], "nt", f"ffn{i}_down_dx")
        g_down = matmul(z, dxo, "tn", f"ffn{i}_down_dw")
        dc, g_cw, g_cb = convgate_bwd_c(u, P["ffn_conv_w"][i], P["ffn_conv_b"][i:i + 1], dz, f"ffn{i}_conv_bwd_c")
        du = convgate_bwd_u(dc, P["ffn_conv_w"][i], f"ffn{i}_conv_bwd_u")
        dhn = matmul(du, P["ffn_w_up"][i], "nt", f"ffn{i}_up_dx")
        g_up = matmul(hn, du, "tn", f"ffn{i}_up_dw")
        (dxin,), (g_n,) = stage_bwd(f_rmsnorm_res, [xin], [P["ffn_norm"][i:i + 1]], [], [], [dhn, dxo], ROW_TILE, f"ffn{i}_norm_bwd")
        return dxin, g_n, g_up, g_cw, g_cb, g_down

    h0 = stage_fwd(f_rmsnorm, [x0], [P["attn_norm"][0:1]], [], [], ROW_TILE, "l0_norm", [BF16])[0]
    p0 = matmul(h0, P["a_w_in"][0], "nn", "l0_in")
    cur = [p0[:, a:b] for a, b in zip(RW_CUTS[:-1], RW_CUTS[1:])]
    prev = [_shift_down(c) for c in cur]
    mu = [P["a_mu"][:, a:b] for a, b in zip(RW_CUTS[:-1], RW_CUTS[1:])]
    pre_ps = mu + [P["a_w0"], P["a_w2"][0], P["a_a0"], P["a_a2"][0], P["a_g2"][0], P["a_k_k"], P["a_k_a"]]
    r, lw, k2, v, kk, b, g = stage_fwd(f_rwkv_pre, cur + prev, pre_ps, [], [seg, seg_t], RW_TILE, "l0_rwkv_pre")
    hm = lambda z: _heads(z, RWKV_HEADS)
    scan_in = [hm(z) for z in (r, lw, k2, v, kk, b)]
    y_h, h_states = rwkv_scan_fwd(*scan_in)
    y_s = _unheads(y_h)
    post_ps = [P["a_lnx_w"], P["a_lnx_b"], P["a_r_k"].reshape(1, RWKV_WIDTH)]
    ymix0 = stage_fwd(f_rwkv_post, [y_s, r, k2, v, g], post_ps, [], [seg, seg_t], RW_TILE, "l0_rwkv_post")[0]
    ymem0, mem0_saved = mem_fwd(0, p0[:, SHIFT_WIDTH:])
    ycat0 = jnp.concatenate([ymix0, ymem0], axis=1).astype(BF16)
    x1 = matmul(ycat0, P["a_w_out"][0], "nn", "l0_out", residual=x0)
    x2, ffn0_saved = ffn_fwd(0, x1)

    hk, h1 = stage_fwd(f_rmsnorm2, [x2], [row(P["kv_norm"]), P["attn_norm"][1:2]], [], [], ROW_TILE, "l1_norm", [BF16, BF16])
    kvp = matmul(hk, P["kv_w"], "nn", "l1_kv")
    p1 = matmul(h1, P["b_w_in"][0], "nn", "l1_in")
    kraw, vsh, qraw = hm(kvp[:, :DIL_WIDTH]), hm(kvp[:, DIL_WIDTH:]), hm(p1[:, :DIL_WIDTH])
    ksh = stage_fwd(f_qkprep, [kraw], [row(P["kv_k_norm"])], [cos, sin], [rot], ROW_TILE, "l1_kprep")[0]
    q = stage_fwd(f_qkprep, [qraw], [P["b_q_norm"]], [cos, sin], [rot], ROW_TILE, "l1_qprep")[0]
    grp, outs, lses = [], [], []
    for gi, (_, d) in enumerate(DIL_GROUPS):
        hs = slice(4 * gi, 4 * gi + 4)
        qg, kg, vg = _regroup(q[hs], d), _regroup(ksh[hs], d), _regroup(vsh[hs], d)
        og, lg = dil_fwd(qg, kg, vg, f"l1_dil{gi}")
        grp.append((qg, kg, vg))
        outs.append(_ungroup(og, d))
        lses.append(_ungroup(lg, d))
    omix = stage_fwd(f_mix, outs + lses, [], [], [], ROW_TILE, "l1_mix")[0]
    ymem1, mem1_saved = mem_fwd(1, p1[:, DIL_WIDTH:])
    ycat1 = jnp.concatenate([_unheads(omix), ymem1], axis=1).astype(BF16)
    x3 = matmul(ycat1, P["b_w_out"][0], "nn", "l1_out", residual=x2)
    x4, ffn1_saved = ffn_fwd(1, x3)
    loss_part, dx4 = loss_head(x4, tgt)

    dx3, gn1, gup1, gcw1, gcb1, gdown1 = ffn_bwd(1, x3, ffn1_saved, dx4)
    dycat1 = matmul(dx3, P["b_w_out"][0], "nt", "l1_out_dx")
    G["b_w_out"] = matmul(ycat1, dx3, "tn", "l1_out_dw")[None]
    dqmem1, gmn1, gmw1, gmq1, gmk1 = mem_bwd(1, mem1_saved, dycat1[:, MEM_WIDTH:])
    dmix, _ = stage_bwd(f_mix, outs + lses, [], [], [], [_heads(dycat1[:, :MEM_WIDTH], 4)], ROW_TILE, "l1_mix_bwd")
    dq, dk, dv = [], [], []
    for gi, (_, d) in enumerate(DIL_GROUPS):
        qg, kg, vg = grp[gi]
        dqg, dkc, dkp, dvc, dvp = dil_bwd(qg, kg, vg, _regroup(dmix[gi], d), _regroup(dmix[3 + gi], d), f"l1_dil{gi}_bwd")
        back = lambda c, p: c + jnp.concatenate([p[:, DIL_BLOCK:], jnp.zeros_like(p[:, :DIL_BLOCK])], axis=1)
        dq.append(_ungroup(dqg, d))
        dk.append(_ungroup(back(dkc, dkp), d))
        dv.append(_ungroup(back(dvc, dvp), d))
    dq, dk, dv = jnp.concatenate(dq, 0), jnp.concatenate(dk, 0), jnp.concatenate(dv, 0)
    (dqraw,), (g_bq,) = stage_bwd(f_qkprep, [qraw], [P["b_q_norm"]], [cos, sin], [rot], [dq], ROW_TILE, "l1_qprep_bwd")
    (dkraw,), (g_kk,) = stage_bwd(f_qkprep, [kraw], [row(P["kv_k_norm"])], [cos, sin], [rot], [dk], ROW_TILE, "l1_kprep_bwd")
    dp1 = jnp.concatenate([_unheads(dqraw), dqmem1], axis=1)
    dkvp = jnp.concatenate([_unheads(dkraw), _unheads(dv)], axis=1)
    dh1 = matmul(dp1, P["b_w_in"][0], "nt", "l1_in_dx")
    G["b_w_in"] = matmul(h1, dp1, "tn", "l1_in_dw")[None]
    dhk = matmul(dkvp, P["kv_w"], "nt", "l1_kv_dx")
    G["kv_w"] = matmul(hk, dkvp, "tn", "l1_kv_dw")
    (dx2,), (g_kvn, g_an1) = stage_bwd(f_rmsnorm2_res, [x2], [row(P["kv_norm"]), P["attn_norm"][1:2]], [], [],
                                       [dhk, dh1, dx3], ROW_TILE, "l1_norm_bwd")

    dx1, gn0, gup0, gcw0, gcb0, gdown0 = ffn_bwd(0, x1, ffn0_saved, dx2)
    dycat0 = matmul(dx1, P["a_w_out"][0], "nt", "l0_out_dx")
    G["a_w_out"] = matmul(ycat0, dx1, "tn", "l0_out_dw")[None]
    dqmem0, gmn0, gmw0, gmq0, gmk0 = mem_bwd(0, mem0_saved, dycat0[:, RWKV_WIDTH:])
    (dy_s, dr_a, dk_a, dv_a, dg), (g_lw, g_lb, g_rk) = stage_bwd(
        f_rwkv_post, [y_s, r, k2, v, g], post_ps, [], [seg, seg_t], [dycat0[:, :RWKV_WIDTH]], RW_TILE, "l0_rwkv_post_bwd")
    dr_b, dlw, dk_b, dv_b, dkk, db = [_unheads(z) for z in rwkv_scan_bwd(*scan_in, h_states, hm(dy_s))]
    dpre, gpre = stage_bwd(f_rwkv_pre, cur + prev, pre_ps, [], [seg, seg_t],
                           [[dr_a, dr_b], dlw, [dk_a, dk_b], [dv_a, dv_b], dkk, db, dg], RW_TILE, "l0_rwkv_pre_bwd")
    dp_rw = jnp.concatenate(dpre[:6], axis=1) + _shift_up(jnp.concatenate(dpre[6:], axis=1))
    dp0 = jnp.concatenate([dp_rw, dqmem0], axis=1)
    dh0 = matmul(dp0, P["a_w_in"][0], "nt", "l0_in_dx")
    G["a_w_in"] = matmul(h0, dp0, "tn", "l0_in_dw")[None]
    (dx0,), (g_an0,) = stage_bwd(f_rmsnorm_res, [x0], [P["attn_norm"][0:1]], [], [], [dh0, dx1], ROW_TILE, "l0_norm_bwd")

    G["attn_norm"] = jnp.concatenate([g_an0, g_an1], axis=0)
    G["a_mu"] = jnp.concatenate(gpre[:6], axis=1)
    G["a_w0"], G["a_w2"], G["a_a0"], G["a_a2"], G["a_g2"] = gpre[6], gpre[7][None], gpre[8], gpre[9][None], gpre[10][None]
    G["a_k_k"], G["a_k_a"] = gpre[11], gpre[12]
    G["a_r_k"] = g_rk.reshape(1, RWKV_HEADS, HEAD_DIM)
    G["a_lnx_w"], G["a_lnx_b"] = g_lw, g_lb
    G["kv_norm"], G["kv_k_norm"], G["b_q_norm"] = g_kvn.reshape(-1), g_kk.reshape(-1), g_bq
    G["mem_norm"] = jnp.concatenate([gmn0, gmn1], axis=0)
    G["mem_w_kv"] = jnp.stack([gmw0, gmw1])
    G["mem_q_norm"] = jnp.concatenate([gmq0, gmq1], axis=0)
    G["mem_k_norm"] = jnp.concatenate([gmk0, gmk1], axis=0)
    G["ffn_norm"] = jnp.concatenate([gn0, gn1], axis=0)
    G["ffn_w_up"] = jnp.stack([gup0, gup1])
    G["ffn_conv_w"] = jnp.stack([gcw0, gcw1])
    G["ffn_conv_b"] = jnp.concatenate([gcb0, gcb1], axis=0)
    G["ffn_w_down"] = jnp.stack([gdown0, gdown1])
    return loss_part, dx0, G


PARAMS = (("attn_norm", None), ("a_w_in", 2), ("a_mu", 1), ("a_w0", 1), ("a_w2", 2), ("a_a0", 1), ("a_a2", 2), ("a_g2", 2),
          ("a_k_k", 1), ("a_k_a", 1), ("a_r_k", None), ("a_lnx_w", 1), ("a_lnx_b", 1), ("a_w_out", 1), ("kv_norm", None),
          ("kv_w", 1), ("kv_k_norm", None), ("b_w_in", 1), ("b_q_norm", None), ("b_w_out", 2), ("mem_norm", None),
          ("mem_w_kv", 1), ("mem_q_norm", None), ("mem_k_norm", None), ("ffn_norm", None), ("ffn_w_up", 2),
          ("ffn_conv_w", 2), ("ffn_conv_b", None), ("ffn_w_down", 1))
BIG = ("a_w_in", "a_w_out", "kv_w", "b_w_in", "b_w_out", "mem_w_kv", "ffn_w_up", "ffn_w_down")
AXIS = dict(PARAMS)
SMALL = tuple(n for n, _ in PARAMS if n not in BIG)
SMALL_SHARDED = tuple(n for n in SMALL if AXIS[n] is not None)
PACK_QUANTUM = 256 * 128


def _from_shards(xs, axis):
    full = jnp.moveaxis(xs, 0, axis)
    sh = full.shape
    return full.reshape(sh[:axis] + (sh[axis] * sh[axis + 1],) + sh[axis + 2:])


def _to_shards(g, axis):
    sh = g.shape
    return jnp.moveaxis(g.reshape(sh[:axis] + (N_DEV, sh[axis] // N_DEV) + sh[axis + 1:]), axis, 0)


def _pack(parts, lead=0):
    ld = parts[0].shape[:lead]
    flat = jnp.concatenate([p.reshape(ld + (-1,)) for p in parts], axis=-1)
    pad = (-flat.shape[-1]) % PACK_QUANTUM
    flat = jnp.pad(flat, [(0, 0)] * lead + [(0, pad)])
    return flat.reshape(ld + (-1, 128))


def _unpack(packed, shapes, lead=0):
    ld = packed.shape[:lead]
    flat = packed.reshape(ld + (-1,))
    out, off = [], 0
    for s in shapes:
        n = math.prod(s)
        out.append(flat[..., off:off + n].reshape(ld + tuple(s)))
        off += n
    return out


def kernel(x, mem, attn_norm, a_w_in, a_mu, a_w0, a_w2, a_a0, a_a2, a_g2, a_k_k, a_k_a, a_r_k, a_lnx_w, a_lnx_b, a_w_out, kv_norm, kv_w, kv_k_norm, b_w_in, b_q_norm, b_w_out, mem_norm, mem_w_kv, mem_q_norm, mem_k_norm, ffn_norm, ffn_w_up, ffn_conv_w, ffn_conv_b, ffn_w_down, loss_target, m_attn_norm, m_a_w_in, m_a_mu, m_a_w0, m_a_w2, m_a_a0, m_a_a2, m_a_g2, m_a_k_k, m_a_k_a, m_a_r_k, m_a_lnx_w, m_a_lnx_b, m_a_w_out, m_kv_norm, m_kv_w, m_kv_k_norm, m_b_w_in, m_b_q_norm, m_b_w_out, m_mem_norm, m_mem_w_kv, m_mem_q_norm, m_mem_k_norm, m_ffn_norm, m_ffn_w_up, m_ffn_conv_w, m_ffn_conv_b, m_ffn_w_down, v_attn_norm, v_a_w_in, v_a_mu, v_a_w0, v_a_w2, v_a_a0, v_a_a2, v_a_g2, v_a_k_k, v_a_k_a, v_a_r_k, v_a_lnx_w, v_a_lnx_b, v_a_w_out, v_kv_norm, v_kv_w, v_kv_k_norm, v_b_w_in, v_b_q_norm, v_b_w_out, v_mem_norm, v_mem_w_kv, v_mem_q_norm, v_mem_k_norm, v_ffn_norm, v_ffn_w_up, v_ffn_conv_w, v_ffn_conv_b, v_ffn_w_down):
    names = [n for n, _ in PARAMS]
    vals = (attn_norm, a_w_in, a_mu, a_w0, a_w2, a_a0, a_a2, a_g2, a_k_k, a_k_a, a_r_k, a_lnx_w, a_lnx_b, a_w_out, kv_norm, kv_w, kv_k_norm, b_w_in, b_q_norm, b_w_out, mem_norm, mem_w_kv, mem_q_norm, mem_k_norm, ffn_norm, ffn_w_up, ffn_conv_w, ffn_conv_b, ffn_w_down)
    m_vals = (m_attn_norm, m_a_w_in, m_a_mu, m_a_w0, m_a_w2, m_a_a0, m_a_a2, m_a_g2, m_a_k_k, m_a_k_a, m_a_r_k, m_a_lnx_w, m_a_lnx_b, m_a_w_out, m_kv_norm, m_kv_w, m_kv_k_norm, m_b_w_in, m_b_q_norm, m_b_w_out, m_mem_norm, m_mem_w_kv, m_mem_q_norm, m_mem_k_norm, m_ffn_norm, m_ffn_w_up, m_ffn_conv_w, m_ffn_conv_b, m_ffn_w_down)
    v_vals = (v_attn_norm, v_a_w_in, v_a_mu, v_a_w0, v_a_w2, v_a_a0, v_a_a2, v_a_g2, v_a_k_k, v_a_k_a, v_a_r_k, v_a_lnx_w, v_a_lnx_b, v_a_w_out, v_kv_norm, v_kv_w, v_kv_k_norm, v_b_w_in, v_b_q_norm, v_b_w_out, v_mem_norm, v_mem_w_kv, v_mem_q_norm, v_mem_k_norm, v_ffn_norm, v_ffn_w_up, v_ffn_conv_w, v_ffn_conv_b, v_ffn_w_down)
    W, M, V = dict(zip(names, vals)), dict(zip(names, m_vals)), dict(zip(names, v_vals))
    me = 4 * lax.axis_index("x") + 2 * lax.axis_index("y") + lax.axis_index("c")
    order = BIG + SMALL
    shard_shapes = [W[n].shape for n in order]

    w_big = _pack([W[n] for n in BIG])
    got_big, got_small = all_gather([w_big.astype(BF16), _pack([W[n] for n in SMALL_SHARDED])], "gather_weights")
    P = {n: W[n] for n in SMALL}
    for n, s in zip(BIG, _unpack(got_big, [W[n].shape for n in BIG], lead=1)):
        P[n] = _from_shards(s, AXIS[n])
    for n, s in zip(SMALL_SHARDED, _unpack(got_small, [W[n].shape for n in SMALL_SHARDED], lead=1)):
        P[n] = _from_shards(s, AXIS[n])

    loss_part, dx0, G = _local_step(x[0], mem[0], loss_target[0], P)
    loss = lax.psum(loss_part[0, 0], ("x", "y", "c"))

    send_big = _pack([_to_shards(G[n], AXIS[n]) for n in BIG], lead=1)
    got_gbig = all_to_all(send_big, "scatter_grads")
    (got_gsmall,) = all_gather([_pack([G[n] for n in SMALL])], "gather_small_grads")
    small_parts = []
    for n, g in zip(SMALL, _unpack(got_gsmall, [G[n].shape for n in SMALL], lead=1)):
        if AXIS[n] is not None:
            s = W[n].shape[AXIS[n]]
            g = lax.dynamic_slice_in_dim(g, me * s, s, axis=AXIS[n] + 1)
        small_parts.append(g)
    n_big = got_gbig.shape[1]
    g_all = jnp.concatenate([got_gbig, _pack(small_parts, lead=1)], axis=1)
    packed = lambda D: jnp.concatenate([_pack([D[n] for n in BIG]), _pack([D[n] for n in SMALL])], axis=0)
    res = adamw(g_all, packed(W), packed(M), packed(V))
    outs = []
    for r in res:
        parts = _unpack(r[:n_big], [W[n].shape for n in BIG]) + _unpack(r[n_big:], [W[n].shape for n in SMALL])
        by_name = dict(zip(order, parts))
        outs.append([by_name[n] for n in names])
    return (loss, dx0[None], *outs[0], *outs[1], *outs[2], *outs[3])
```

```python
import functools
import math

import jax
import jax.numpy as jnp
import numpy as np
from jax import lax
from jax.experimental import pallas as pl
from jax.experimental.pallas import tpu as pltpu

F32 = jnp.float32
BF16 = jnp.bfloat16
HI = lax.Precision.HIGHEST
H3 = lax.Precision.HIGH

N_DEV = 8
D_MODEL = 1024
HEAD_DIM = 64
N_MEM = 256
MEM_HEADS = 4
MEM_WIDTH = 256
RWKV_HEADS = 12
RWKV_WIDTH = 768
SHIFT_WIDTH = 2560
DIL_GROUPS = ((128, 1), (512, 4), (2048, 16))
DIL_BLOCK = 128
DIL_WIDTH = 768
D_FF = 2816
RMS_EPS = 1e-6
LNX_EPS = 64e-5
NEG_INF = -1e30
ROPE_THETA = 10000.0
ADAM_LR, ADAM_B1, ADAM_B2, ADAM_EPS, ADAM_WD, ADAM_STEP = 0.001, 0.9, 0.999, 1e-08, 0.01, 10

CHUNK = 64
SCAN_GROUPS = 2
VMEM_LIMIT_V7X = 48 * 1024 * 1024


def _cparams(sem):
    return pltpu.CompilerParams(dimension_semantics=sem, vmem_limit_bytes=VMEM_LIMIT_V7X)


def _pick(n, cands):
    for c in cands:
        if n % c == 0:
            return c
    return n


def _dg(a, b, ca, cb, batch):
    dims = (((ca,), (cb,)), ((0,), (0,))) if batch else (((ca,), (cb,)), ((), ()))
    return lax.dot_general(a.astype(BF16), b.astype(BF16), dims, preferred_element_type=F32)


@jax.custom_vjp
def mm_nn(a, b):
    n = a.ndim
    return _dg(a, b, n - 1, n - 2, n == 3)


def _mm_nn_fwd(a, b):
    return mm_nn(a, b), (a, b)


def _mm_nn_bwd(res, g):
    a, b = res
    n = a.ndim
    return _dg(g, b, n - 1, n - 1, n == 3), _dg(a, g, n - 2, n - 2, n == 3)


mm_nn.defvjp(_mm_nn_fwd, _mm_nn_bwd)


@jax.custom_vjp
def mm_nt(a, b):
    n = a.ndim
    return _dg(a, b, n - 1, n - 1, n == 3)


def _mm_nt_fwd(a, b):
    return mm_nt(a, b), (a, b)


def _mm_nt_bwd(res, g):
    a, b = res
    n = a.ndim
    return _dg(g, b, n - 1, n - 2, n == 3), _dg(g, a, n - 2, n - 2, n == 3)


mm_nt.defvjp(_mm_nt_fwd, _mm_nt_bwd)


def mmh(a, b, precision=H3):
    n = a.ndim
    dims = (((n - 1,), (n - 2,)), ((0,), (0,))) if n == 3 else (((1,), (0,)), ((), ()))
    return lax.dot_general(a, b, dims, precision=precision, preferred_element_type=F32)


def mmh_nt(a, b):
    n = a.ndim
    dims = (((n - 1,), (n - 1,)), ((0,), (0,))) if n == 3 else (((1,), (1,)), ((), ()))
    return lax.dot_general(a, b, dims, precision=H3, preferred_element_type=F32)


def mmh_tn(a, b):
    n = a.ndim
    dims = (((n - 2,), (n - 2,)), ((0,), (0,))) if n == 3 else (((0,), (0,)), ((), ()))
    return lax.dot_general(a, b, dims, precision=H3, preferred_element_type=F32)


def matmul(a, b, mode, name, residual=None):
    if mode == "nn":
        (M, K), (_, N) = a.shape, b.shape
    elif mode == "nt":
        (M, K), (N, _) = a.shape, b.shape
    else:
        (K, M), (_, N) = a.shape, b.shape
    tm = _pick(M, (512, 256, 128))
    tn = _pick(N, (512, 256, 128))
    tk = _pick(K, (1024, 512, 256, 128))
    nk = K // tk
    if mode == "nn":
        a_spec = pl.BlockSpec((tm, tk), lambda i, j, k: (i, k))
        b_spec = pl.BlockSpec((tk, tn), lambda i, j, k: (k, j))
        dims = (((1,), (0,)), ((), ()))
    elif mode == "nt":
        a_spec = pl.BlockSpec((tm, tk), lambda i, j, k: (i, k))
        b_spec = pl.BlockSpec((tn, tk), lambda i, j, k: (j, k))
        dims = (((1,), (1,)), ((), ()))
    else:
        a_spec = pl.BlockSpec((tk, tm), lambda i, j, k: (k, i))
        b_spec = pl.BlockSpec((tk, tn), lambda i, j, k: (k, j))
        dims = (((0,), (0,)), ((), ()))
    o_spec = pl.BlockSpec((tm, tn), lambda i, j, k: (i, j))
    has_res = residual is not None

    def body(*refs):
        if has_res:
            a_ref, b_ref, r_ref, o_ref, acc_ref = refs
        else:
            a_ref, b_ref, o_ref, acc_ref = refs
        k = pl.program_id(2)

        @pl.when(k == 0)
        def _():
            acc_ref[...] = jnp.zeros_like(acc_ref)

        acc_ref[...] += lax.dot_general(a_ref[...].astype(BF16), b_ref[...].astype(BF16), dims,
                                        preferred_element_type=F32)

        @pl.when(k == nk - 1)
        def _():
            if has_res:
                o_ref[...] = acc_ref[...] + r_ref[...]
            else:
                o_ref[...] = acc_ref[...]

    ins = [a, b] + ([residual] if has_res else [])
    in_specs = [a_spec, b_spec] + ([o_spec] if has_res else [])
    return pl.pallas_call(
        body, grid=(M // tm, N // tn, nk), in_specs=in_specs, out_specs=o_spec,
        out_shape=jax.ShapeDtypeStruct((M, N), F32), scratch_shapes=[pltpu.VMEM((tm, tn), F32)],
        compiler_params=_cparams(("parallel", "parallel", "arbitrary")), name=name)(*ins)


def _tok_spec(shape, tile):
    nd = len(shape)
    return pl.BlockSpec(shape[:-2] + (tile, shape[-1]), lambda i: (0,) * (nd - 2) + (i, 0))


def _full_spec(shape):
    nd = len(shape)
    return pl.BlockSpec(shape, lambda i: (0,) * nd)


def _blk(x, tile):
    return jax.ShapeDtypeStruct(x.shape[:-2] + (tile, x.shape[-1]), x.dtype)


def stage_fwd(f, xs, ps, cts, cfs, tile, name, out_dtypes=None):
    tiled, full = list(xs) + list(cts), list(ps) + list(cfs)
    nx, np_, nct = len(xs), len(ps), len(cts)
    T = tiled[0].shape[-2]
    order = lambda t, fl: list(t[:nx]) + list(fl[:np_]) + list(t[nx:]) + list(fl[np_:])
    out_avals = jax.eval_shape(f, *order([_blk(x, tile) for x in tiled], full))
    if out_dtypes is None:
        out_dtypes = [o.dtype for o in out_avals]
    out_shape = [jax.ShapeDtypeStruct(o.shape[:-2] + (T, o.shape[-1]), dt) for o, dt in zip(out_avals, out_dtypes)]
    nt, nf = len(tiled), len(full)

    def body(*refs):
        tv = [r[...] for r in refs[:nt]]
        fv = [r[...] for r in refs[nt:nt + nf]]
        res = f(*order(tv, fv))
        for o_ref, r in zip(refs[nt + nf:], res):
            o_ref[...] = r.astype(o_ref.dtype)

    return pl.pallas_call(
        body, grid=(T // tile,),
        in_specs=[_tok_spec(x.shape, tile) for x in tiled] + [_full_spec(p.shape) for p in full],
        out_specs=[_tok_spec(o.shape, tile) for o in out_shape], out_shape=out_shape,
        compiler_params=_cparams(("parallel",)), name=name)(*tiled, *full)


def stage_bwd(f, xs, ps, cts, cfs, gs, tile, name):
    gs = [list(g) if isinstance(g, (list, tuple)) else [g] for g in gs]
    g_flat = [a for g in gs for a in g]
    tiled, full = list(xs) + list(cts) + g_flat, list(ps) + list(cfs)
    nx, np_, nct = len(xs), len(ps), len(cts)
    T = tiled[0].shape[-2]
    nt, nf = len(tiled), len(full)
    out_shape = [jax.ShapeDtypeStruct(x.shape, F32) for x in xs] + [jax.ShapeDtypeStruct(p.shape, F32) for p in ps]

    def body(*refs):
        tv = [r[...] for r in refs[:nt]]
        fv = [r[...] for r in refs[nt:nt + nf]]
        outs = refs[nt + nf:]
        xv, ctv, gparts = tv[:nx], tv[nx:nx + nct], tv[nx + nct:]
        gv = []
        for g in gs:
            gv.append(functools.reduce(lambda a, b: a + b, gparts[:len(g)]))
            gparts = gparts[len(g):]
        pv, cfv = fv[:np_], fv[np_:]
        _, vjp = jax.vjp(lambda *xp: f(*xp, *ctv, *cfv), *xv, *pv)
        d = vjp(tuple(gv))
        for o_ref, r in zip(outs[:nx], d[:nx]):
            o_ref[...] = r

        @pl.when(pl.program_id(0) == 0)
        def _():
            for o_ref in outs[nx:]:
                o_ref[...] = jnp.zeros_like(o_ref)

        for o_ref, r in zip(outs[nx:], d[nx:]):
            o_ref[...] += r

    res = pl.pallas_call(
        body, grid=(T // tile,),
        in_specs=[_tok_spec(x.shape, tile) for x in tiled] + [_full_spec(p.shape) for p in full],
        out_specs=[_tok_spec(x.shape, tile) for x in xs] + [_full_spec(p.shape) for p in ps], out_shape=out_shape,
        compiler_params=_cparams(("arbitrary",)), name=name)(*tiled, *full)
    return list(res[:nx]), list(res[nx:])


def _rms(x, g, eps=RMS_EPS):
    return x * lax.rsqrt(jnp.mean(x * x, axis=-1, keepdims=True) + eps) * g


def f_rmsnorm(x, g):
    return (_rms(x, g),)


def f_rmsnorm_res(x, g):
    return _rms(x, g), x


def f_rmsnorm2(x, g1, g2):
    n = x * lax.rsqrt(jnp.mean(x * x, axis=-1, keepdims=True) + RMS_EPS)
    return n * g1, n * g2


def f_rmsnorm2_res(x, g1, g2):
    return f_rmsnorm2(x, g1, g2) + (x,)


def _sigmoid(x):
    return 1.0 / (1.0 + jnp.exp(-x))


def _softplus(x):
    return jnp.maximum(x, 0.0) + jnp.log(1.0 + jnp.exp(-jnp.abs(x)))


def f_rwkv_pre(pr, pk, pv, pwd, pad, pgd, qr, qk, qv, qwd, qad, qgd,
               mu_r, mu_k, mu_v, mu_wd, mu_ad, mu_gd, w0, w2, a0, a2, g2, k_k, k_a, seg, seg_t):
    xr = pr + (qr - pr) * mu_r
    xk = pk + (qk - pk) * mu_k
    xv = pv + (qv - pv) * mu_v
    xwd = pwd + (qwd - pwd) * mu_wd
    xad = pad + (qad - pad) * mu_ad
    xgd = pgd + (qgd - pgd) * mu_gd
    w_log = -_softplus(-(w0 + mm_nn(jnp.tanh(xwd), w2))) - 0.5
    lw = -jnp.exp(w_log)
    a = _sigmoid(a0 + mm_nn(xad, a2))
    g = mm_nn(_sigmoid(xgd), g2)
    kkr = xk * k_k
    inv = lax.rsqrt(jnp.maximum(mmh(kkr * kkr, seg), 1e-24))
    kk = kkr * mmh(inv, seg_t)
    k2 = xk * (1.0 + (a - 1.0) * k_a)
    return xr, lw, k2, xv, kk, kk * a, g


def f_rwkv_post(y, r, k2, v, g, lnx_w, lnx_b, r_k, seg, seg_t):
    inv_n = 1.0 / HEAD_DIM
    m = mmh(mmh(y, seg) * inv_n, seg_t)
    yc = y - m
    rstd = lax.rsqrt(mmh(yc * yc, seg) * inv_n + LNX_EPS)
    yn = yc * mmh(rstd, seg_t) * lnx_w + lnx_b
    bonus = mmh(mmh(r * k2 * r_k, seg), seg_t) * v
    return ((yn + bonus) * g,)


def f_headnorm(z, g):
    return (_rms(z, g),)


def f_qkprep(z, g, cos, sin, rot):
    h, t, d = z.shape
    zn = _rms(z, g)
    zr = mmh(zn.reshape(h * t, d), rot).reshape(h, t, d)
    return (zn * cos + zr * sin,)


def f_memattn(q, k, v, q_norm):
    qh = _rms(q, q_norm)
    s = mm_nt(qh, k) * (1.0 / math.sqrt(HEAD_DIM))
    s = s - jnp.max(s, axis=-1, keepdims=True)
    p = jnp.exp(s)
    p = p / jnp.sum(p, axis=-1, keepdims=True)
    return (mm_nn(p, v),)


def f_mix(o1, o2, o3, l1, l2, l3):
    mx = jnp.maximum(jnp.maximum(l1, l2), l3)
    e1, e2, e3 = jnp.exp(l1 - mx), jnp.exp(l2 - mx), jnp.exp(l3 - mx)
    return ((e1 * o1 + e2 * o2 + e3 * o3) / (e1 + e2 + e3),)


def _chunk_masks(L):
    t = lax.broadcasted_iota(jnp.int32, (L, L), 0)
    s = lax.broadcasted_iota(jnp.int32, (L, L), 1)
    return t, s


def f_rwkv_chunk(h0, r, lw, k, v, kk, b):
    H, L, _ = r.shape
    t, s = _chunk_masks(L)
    one = jnp.ones((), F32)
    incl = jnp.where(t >= s, one, 0.0)
    strict = jnp.where(t > s, one, 0.0)
    eye = jnp.where(t == s, one, 0.0)
    cum = mmh(jnp.broadcast_to(incl, (H, L, L)), lw, HI)
    w_in = jnp.exp(cum)
    w_ex = jnp.exp(cum - lw)
    w_inv = jnp.exp(-cum)
    rt, kkt, kt, bt = r * w_in, kk * w_ex, k * w_inv, b * w_inv
    a_b = mmh_nt(kkt, bt) * strict
    a_k = mmh_nt(kkt, kt) * strict
    m_k = mmh_nt(rt, kt) * incl
    m_b = mmh_nt(rt, bt) * incl
    blk = lambda sh: jnp.where((t >> sh) == (s >> sh), one, 0.0)
    n0 = a_b * blk(3)
    x = eye - n0
    n2 = mmh(n0, n0)
    x = x + mmh(x, n2)
    x = x + mmh(x, mmh(n2, n2))
    for sh in (3, 4, 5):
        if (1 << sh) >= L:
            break
        off = a_b * (blk(sh + 1) - blk(sh))
        x = x - mmh(x, mmh(off, x))
    p = mmh(x, kkt)
    q = mmh(x, mmh(a_k, v))
    u = mmh(p, h0) + q
    y = mmh(rt, h0) + mmh(m_k, v) - mmh(m_b, u)
    w_last = jnp.exp(jnp.sum(lw, axis=1))
    h1 = w_last[:, :, None] * (h0 + mmh_tn(kt, v) - mmh_tn(bt, u))
    return y, h1


def _ex_split(ex, refs, n_in, n_out):
    n = ex.n
    ins, ex_in = refs[:n_in], refs[n_in:n_in + n]
    outs, ex_out = refs[n_in + n:n_in + n + n_out], refs[n_in + n + n_out:n_in + 2 * n + n_out]
    rest = refs[n_in + 2 * n + n_out:]
    return ins, outs, rest[:len(rest) - 3], (ex_in, ex_out) + tuple(rest[len(rest) - 3:])


def rwkv_scan_fwd(r, lw, k, v, kk, b, ex):
    H, T, N = r.shape
    nc, hg = T // CHUNK, H // SCAN_GROUPS
    seq = pl.BlockSpec((hg, CHUNK, N), lambda g, c: (g, c, 0))

    def body(*refs):
        (r_ref, lw_ref, k_ref, v_ref, kk_ref, b_ref), (y_ref, hs_ref), (h_scr,), ex_refs = _ex_split(ex, refs, 6, 2)
        g, c = pl.program_id(0), pl.program_id(1)

        @pl.when(jnp.logical_and(g == 0, c == 0))
        def _():
            ex.start(*ex_refs)

        @pl.when(c == 0)
        def _():
            h_scr[...] = jnp.zeros_like(h_scr)

        h0 = h_scr[...]
        hs_ref[0] = h0
        y, h1 = f_rwkv_chunk(h0, r_ref[...], lw_ref[...], k_ref[...], v_ref[...], kk_ref[...], b_ref[...])
        y_ref[...] = y
        h_scr[...] = h1

        @pl.when(jnp.logical_and(g == SCAN_GROUPS - 1, c == nc - 1))
        def _():
            ex.wait(*ex_refs)

    res = pl.pallas_call(
        body, grid=(SCAN_GROUPS, nc), in_specs=[seq] * 6 + [_ANY] * ex.n,
        out_specs=[seq, pl.BlockSpec((1, hg, N, N), lambda g, c: (c, g, 0, 0))] + [_ANY] * ex.n,
        out_shape=[jax.ShapeDtypeStruct((H, T, N), F32), jax.ShapeDtypeStruct((nc, H, N, N), F32)] + ex.out_shape(),
        scratch_shapes=[pltpu.VMEM((hg, N, N), F32)] + ex.scratch(),
        compiler_params=_cparams(("arbitrary", "arbitrary")), name="rwkv_scan_fwd")(r, lw, k, v, kk, b, *ex.operands())
    return res[0], res[1], list(res[2:])


def rwkv_scan_bwd(r, lw, k, v, kk, b, hs, dy, ex):
    H, T, N = r.shape
    nc, hg = T // CHUNK, H // SCAN_GROUPS
    seq = pl.BlockSpec((hg, CHUNK, N), lambda g, c: (g, nc - 1 - c, 0))

    def body(*refs):
        (r_ref, lw_ref, k_ref, v_ref, kk_ref, b_ref, hs_ref, dy_ref), outs, (dh_scr,), ex_refs = _ex_split(ex, refs, 8, 6)
        g, c = pl.program_id(0), pl.program_id(1)

        @pl.when(jnp.logical_and(g == 0, c == 0))
        def _():
            ex.start(*ex_refs)

        @pl.when(c == 0)
        def _():
            dh_scr[...] = jnp.zeros_like(dh_scr)

        _, vjp = jax.vjp(f_rwkv_chunk, hs_ref[0], r_ref[...], lw_ref[...], k_ref[...], v_ref[...], kk_ref[...],
                         b_ref[...])
        d = vjp((dy_ref[...], dh_scr[...]))
        dh_scr[...] = d[0]
        for o_ref, dz in zip(outs, d[1:]):
            o_ref[...] = dz

        @pl.when(jnp.logical_and(g == SCAN_GROUPS - 1, c == nc - 1))
        def _():
            ex.wait(*ex_refs)

    res = pl.pallas_call(
        body, grid=(SCAN_GROUPS, nc),
        in_specs=[seq] * 6 + [pl.BlockSpec((1, hg, N, N), lambda g, c: (nc - 1 - c, g, 0, 0)), seq] + [_ANY] * ex.n,
        out_specs=[seq] * 6 + [_ANY] * ex.n, out_shape=[jax.ShapeDtypeStruct((H, T, N), F32)] * 6 + ex.out_shape(),
        scratch_shapes=[pltpu.VMEM((hg, N, N), F32)] + ex.scratch(),
        compiler_params=_cparams(("arbitrary", "arbitrary")), name="rwkv_scan_bwd")(r, lw, k, v, kk, b, hs, dy, *ex.operands())
    return list(res[:6]), list(res[6:])


DIL_ROWS = 4


def _f_dilattn(has_prev, q, kc, kp, vc, vp):
    scale = 1.0 / math.sqrt(HEAD_DIM)
    sc = mm_nt(q, kc) * scale
    sp = mm_nt(q, kp) * scale
    i = lax.broadcasted_iota(jnp.int32, (DIL_BLOCK, DIL_BLOCK), 0)
    j = lax.broadcasted_iota(jnp.int32, (DIL_BLOCK, DIL_BLOCK), 1)
    sc = jnp.where(j <= i, sc, NEG_INF)
    sp = jnp.where(jnp.logical_and(i <= j, has_prev), sp, NEG_INF)
    mx = jnp.maximum(jnp.max(sc, axis=-1, keepdims=True), jnp.max(sp, axis=-1, keepdims=True))
    pc, pp = jnp.exp(sc - mx), jnp.exp(sp - mx)
    den = jnp.sum(pc, axis=-1, keepdims=True) + jnp.sum(pp, axis=-1, keepdims=True)
    o = (mm_nn(pc, vc) + mm_nn(pp, vp)) / den
    return o, mx + jnp.log(den)


def _dil_specs(R, U):
    cur = pl.BlockSpec((DIL_ROWS, DIL_BLOCK, HEAD_DIM), lambda g, n: (g, n, 0))
    prev = pl.BlockSpec((DIL_ROWS, DIL_BLOCK, HEAD_DIM), lambda g, n: (g, jnp.maximum(n - 1, 0), 0))
    lse = pl.BlockSpec((DIL_ROWS, DIL_BLOCK, 1), lambda g, n: (g, n, 0))
    return cur, prev, lse, (R // DIL_ROWS, U // DIL_BLOCK)


def dil_fwd(q, k, v, name):
    R, U, _ = q.shape
    cur, prev, lse, grid = _dil_specs(R, U)

    def body(q_ref, kc_ref, kp_ref, vc_ref, vp_ref, o_ref, l_ref):
        o, l = _f_dilattn(pl.program_id(1) > 0, q_ref[...], kc_ref[...], kp_ref[...], vc_ref[...], vp_ref[...])
        o_ref[...] = o
        l_ref[...] = l

    return pl.pallas_call(
        body, grid=grid, in_specs=[cur, cur, prev, cur, prev], out_specs=[cur, lse],
        out_shape=[jax.ShapeDtypeStruct((R, U, HEAD_DIM), F32), jax.ShapeDtypeStruct((R, U, 1), F32)],
        compiler_params=_cparams(("parallel", "parallel")), name=name)(q, k, k, v, v)


def dil_bwd(q, k, v, do, dl, name):
    R, U, _ = q.shape
    cur, prev, lse, grid = _dil_specs(R, U)

    def body(q_ref, kc_ref, kp_ref, vc_ref, vp_ref, do_ref, dl_ref, *outs):
        f = functools.partial(_f_dilattn, pl.program_id(1) > 0)
        _, vjp = jax.vjp(f, q_ref[...], kc_ref[...], kp_ref[...], vc_ref[...], vp_ref[...])
        for o_ref, g in zip(outs, vjp((do_ref[...], dl_ref[...]))):
            o_ref[...] = g

    return pl.pallas_call(
        body, grid=grid, in_specs=[cur, cur, prev, cur, prev, cur, lse], out_specs=[cur] * 5,
        out_shape=[jax.ShapeDtypeStruct((R, U, HEAD_DIM), F32)] * 5,
        compiler_params=_cparams(("parallel", "parallel")), name=name)(q, k, k, v, v, do, dl)


CONV_TILE = 128


def _conv3(u, h6, h7, w, b):
    rows = lax.broadcasted_iota(jnp.int32, (u.shape[0], 1), 0)
    s1 = jnp.where(rows == 0, h7, pltpu.roll(u, 1, 0))
    s2 = jnp.where(rows == 0, h6, jnp.where(rows == 1, h7, pltpu.roll(u, 2, 0)))
    return b + w[0:1] * s2 + w[1:2] * s1 + w[2:3] * u, s1, s2


def _conv_halves(u_ref, h_ref, cw_ref, cb_ref):
    F = D_FF
    first = pl.program_id(0) > 0
    res = []
    for lo in (0, F):
        h = h_ref[:, lo:lo + F]
        h6 = jnp.where(first, h[6:7], 0.0)
        h7 = jnp.where(first, h[7:8], 0.0)
        u = u_ref[:, lo:lo + F]
        res.append((u,) + _conv3(u, h6, h7, cw_ref[:, lo:lo + F], cb_ref[:, lo:lo + F]))
    return res


def _halo_before(C):
    return pl.BlockSpec((8, C), lambda i: (jnp.maximum(i * (CONV_TILE // 8) - 1, 0), 0))


def convgate_fwd(u, cw, cb, name):
    T, C = u.shape
    F = C // 2

    def body(u_ref, h_ref, cw_ref, cb_ref, z_ref):
        (_, cg, _, _), (_, cv, _, _) = _conv_halves(u_ref, h_ref, cw_ref, cb_ref)
        z_ref[...] = (cg * _sigmoid(cg) * cv).astype(BF16)

    return pl.pallas_call(
        body, grid=(T // CONV_TILE,),
        in_specs=[pl.BlockSpec((CONV_TILE, C), lambda i: (i, 0)), _halo_before(C), _full_spec(cw.shape), _full_spec(cb.shape)],
        out_specs=pl.BlockSpec((CONV_TILE, F), lambda i: (i, 0)), out_shape=jax.ShapeDtypeStruct((T, F), BF16),
        compiler_params=_cparams(("parallel",)), name=name)(u, u, cw, cb)


def convgate_bwd_c(u, cw, cb, dz, name):
    T, C = u.shape
    F = C // 2

    def body(u_ref, h_ref, cw_ref, cb_ref, dz_ref, dc_ref, dcw_ref, dcb_ref):
        (ug, cg, g1, g2), (uv, cv, v1, v2) = _conv_halves(u_ref, h_ref, cw_ref, cb_ref)
        dz = dz_ref[...]
        sg = _sigmoid(cg)
        dgate = dz * cv * sg * (1.0 + cg * (1.0 - sg))
        dval = dz * cg * sg
        dc_ref[:, :F] = dgate
        dc_ref[:, F:] = dval

        @pl.when(pl.program_id(0) == 0)
        def _():
            dcw_ref[...] = jnp.zeros_like(dcw_ref)
            dcb_ref[...] = jnp.zeros_like(dcb_ref)

        for lo, d, u0, s1, s2 in ((0, dgate, ug, g1, g2), (F, dval, uv, v1, v2)):
            dcb_ref[:, lo:lo + F] += jnp.sum(d, axis=0, keepdims=True)
            dcw_ref[0:1, lo:lo + F] += jnp.sum(d * s2, axis=0, keepdims=True)
            dcw_ref[1:2, lo:lo + F] += jnp.sum(d * s1, axis=0, keepdims=True)
            dcw_ref[2:3, lo:lo + F] += jnp.sum(d * u0, axis=0, keepdims=True)

    return pl.pallas_call(
        body, grid=(T // CONV_TILE,),
        in_specs=[pl.BlockSpec((CONV_TILE, C), lambda i: (i, 0)), _halo_before(C), _full_spec(cw.shape), _full_spec(cb.shape),
                  pl.BlockSpec((CONV_TILE, F), lambda i: (i, 0))],
        out_specs=[pl.BlockSpec((CONV_TILE, C), lambda i: (i, 0)), _full_spec(cw.shape), _full_spec(cb.shape)],
        out_shape=[jax.ShapeDtypeStruct((T, C), F32), jax.ShapeDtypeStruct(cw.shape, F32), jax.ShapeDtypeStruct(cb.shape, F32)],
        compiler_params=_cparams(("arbitrary",)), name=name)(u, u, cw, cb, dz)


def convgate_bwd_u(dc, cw, name):
    T, C = dc.shape
    n = T // CONV_TILE

    def body(dc_ref, nx_ref, cw_ref, du_ref):
        last = pl.program_id(0) < n - 1
        rows = lax.broadcasted_iota(jnp.int32, (CONV_TILE, 1), 0)
        for lo in (0, C // 2):
            sl = slice(lo, lo + C // 2)
            d = dc_ref[:, sl]
            n0 = jnp.where(last, nx_ref[0:1, sl], 0.0)
            n1 = jnp.where(last, nx_ref[1:2, sl], 0.0)
            up1 = jnp.where(rows == CONV_TILE - 1, n0, pltpu.roll(d, CONV_TILE - 1, 0))
            up2 = jnp.where(rows == CONV_TILE - 1, n1, jnp.where(rows == CONV_TILE - 2, n0, pltpu.roll(d, CONV_TILE - 2, 0)))
            du_ref[:, sl] = cw_ref[2:3, sl] * d + cw_ref[1:2, sl] * up1 + cw_ref[0:1, sl] * up2

    nxt = pl.BlockSpec((8, C), lambda i: (jnp.minimum((i + 1) * (CONV_TILE // 8), T // 8 - 1), 0))
    return pl.pallas_call(
        body, grid=(n,), in_specs=[pl.BlockSpec((CONV_TILE, C), lambda i: (i, 0)), nxt, _full_spec(cw.shape)],
        out_specs=pl.BlockSpec((CONV_TILE, C), lambda i: (i, 0)), out_shape=jax.ShapeDtypeStruct((T, C), F32),
        compiler_params=_cparams(("parallel",)), name=name)(dc, dc, cw)


def loss_head(y, tgt):
    T, D = y.shape
    tile = 256

    def body(y_ref, t_ref, l_ref, d_ref):
        d = y_ref[...] - t_ref[...]
        d_ref[...] = d * (1.0 / D)

        @pl.when(pl.program_id(0) == 0)
        def _():
            l_ref[...] = jnp.zeros_like(l_ref)

        l_ref[...] += (0.5 / D) * jnp.sum(d * d)

    row = pl.BlockSpec((tile, D), lambda i: (i, 0))
    return pl.pallas_call(
        body, grid=(T // tile,), in_specs=[row, row], out_specs=[pl.BlockSpec((8, 128), lambda i: (0, 0)), row],
        out_shape=[jax.ShapeDtypeStruct((8, 128), F32), jax.ShapeDtypeStruct((T, D), F32)],
        compiler_params=_cparams(("arbitrary",)), name="loss_head")(y, tgt)


def sum_parts(parts, name):
    S, R, C = parts.shape
    tile = _pick(R, (256, 128, 64, 32, 16, 8))

    def body(p_ref, o_ref):
        acc = p_ref[0]
        for s in range(1, S):
            acc = acc + p_ref[s]
        o_ref[...] = acc

    return pl.pallas_call(
        body, grid=(R // tile,), in_specs=[pl.BlockSpec((S, tile, C), lambda i: (0, i, 0))],
        out_specs=pl.BlockSpec((tile, C), lambda i: (i, 0)), out_shape=jax.ShapeDtypeStruct((R, C), F32),
        compiler_params=_cparams(("parallel",)), name=name)(parts)


def adamw(gparts, w, m, v, name):
    S, R, C = gparts.shape
    tile = _pick(R, (256, 128, 64, 32, 16, 8))
    c1 = 1.0 / (1.0 - ADAM_B1 ** ADAM_STEP)
    c2 = 1.0 / (1.0 - ADAM_B2 ** ADAM_STEP)

    def body(g_ref, w_ref, m_ref, v_ref, go_ref, d_ref, mo_ref, vo_ref):
        g = g_ref[0]
        for s in range(1, S):
            g = g + g_ref[s]
        m1 = ADAM_B1 * m_ref[...] + (1.0 - ADAM_B1) * g
        v1 = ADAM_B2 * v_ref[...] + (1.0 - ADAM_B2) * (g * g)
        go_ref[...] = g
        mo_ref[...] = m1
        vo_ref[...] = v1
        d_ref[...] = -ADAM_LR * ((m1 * c1) / (jnp.sqrt(v1 * c2) + ADAM_EPS) + ADAM_WD * w_ref[...])

    row = pl.BlockSpec((tile, C), lambda i: (i, 0))
    return pl.pallas_call(
        body, grid=(R // tile,), in_specs=[pl.BlockSpec((S, tile, C), lambda i: (0, i, 0)), row, row, row],
        out_specs=[row] * 4, out_shape=[jax.ShapeDtypeStruct((R, C), F32)] * 4,
        compiler_params=_cparams(("parallel",)), name=name)(gparts, w, m, v)


def _peers():
    x, y, c = lax.axis_index("x"), lax.axis_index("y"), lax.axis_index("c")
    peers = []
    for k in range(1, N_DEV):
        px = 1 - x if k & 4 else x
        py = 1 - y if k & 2 else y
        pc = 1 - c if k & 1 else c
        peers.append(((px, py, pc), 4 * px + 2 * py + pc))
    return 4 * x + 2 * y + c, peers


_ANY = pl.BlockSpec(memory_space=pl.ANY)


class Exchange:
    def __init__(self, gathers=(), scatters=()):
        self.gathers, self.scatters = list(gathers), list(scatters)
        self.n = len(self.gathers) + len(self.scatters)

    def operands(self):
        return self.gathers + self.scatters

    def out_shape(self):
        return ([jax.ShapeDtypeStruct((N_DEV,) + x.shape, x.dtype) for x in self.gathers]
                + [jax.ShapeDtypeStruct(x.shape, x.dtype) for x in self.scatters])

    def scratch(self):
        n = max(self.n, 1)
        return [pltpu.SemaphoreType.DMA((7 * n,)), pltpu.SemaphoreType.DMA((7 * n,)), pltpu.SemaphoreType.DMA((n,))]

    def _copies(self, in_refs, out_refs, send_sems, recv_sems, local_sems):
        if self.n == 0:
            return [], [], []
        me, peers = _peers()
        ng = len(self.gathers)
        local, sends, recvs = [], [], []
        for a in range(self.n):
            x, o = in_refs[a], out_refs[a]
            mine = x if a < ng else x.at[me]
            local.append(pltpu.make_async_copy(mine, o.at[me], local_sems.at[a]))
            for k in range(7):
                peer, slot = peers[k]
                sems = dict(send_sem=send_sems.at[7 * a + k], recv_sem=recv_sems.at[7 * a + k], device_id=peer,
                            device_id_type=pl.DeviceIdType.MESH)
                sends.append(pltpu.make_async_remote_copy(src_ref=x if a < ng else x.at[slot], dst_ref=o.at[me], **sems))
                recvs.append(pltpu.make_async_remote_copy(src_ref=mine, dst_ref=o.at[slot], **sems))
        return local, sends, recvs

    def start(self, *refs):
        local, sends, _ = self._copies(*refs)
        for cp in local + sends:
            cp.start()

    def wait(self, *refs):
        local, sends, recvs = self._copies(*refs)
        for cp in recvs:
            cp.wait_recv()
        for cp in sends:
            cp.wait_send()
        for cp in local:
            cp.wait()


def exchange(ex, name):
    n = ex.n

    def body(*refs):
        args = (refs[:n], refs[n:2 * n]) + tuple(refs[2 * n:])
        ex.start(*args)
        ex.wait(*args)

    return pl.pallas_call(body, in_specs=[_ANY] * n, out_specs=[_ANY] * n, out_shape=ex.out_shape(),
                          scratch_shapes=ex.scratch(), name=name)(*ex.operands())


def _heads(z, h):
    return z.reshape(z.shape[0], h, HEAD_DIM).transpose(1, 0, 2)


def _unheads(z):
    return z.transpose(1, 0, 2).reshape(z.shape[1], z.shape[0] * HEAD_DIM)


def _regroup(z, d):
    h, T, w = z.shape
    return z.reshape(h, T // d, d, w).transpose(0, 2, 1, 3).reshape(h * d, T // d, w)


def _ungroup(z, d):
    hd, U, w = z.shape
    return z.reshape(hd // d, d, U, w).transpose(0, 2, 1, 3).reshape(hd // d, U * d, w)


def _shift_down(z):
    return jnp.concatenate([jnp.zeros_like(z[:1]), z[:-1]], axis=0)


def _shift_up(z):
    return jnp.concatenate([z[1:], jnp.zeros_like(z[:1])], axis=0)


def _segments(width):
    seg = np.zeros((width, 128), np.float32)
    seg[np.arange(width), np.arange(width) // HEAD_DIM] = 1.0
    return jnp.asarray(seg), jnp.asarray(seg.T)


def _rope_consts(T):
    inv = ROPE_THETA ** (-jnp.arange(0, HEAD_DIM, 2, dtype=F32) / HEAD_DIM)
    ang = jnp.arange(T, dtype=F32)[:, None] * inv[None, :]
    cos, sin = jnp.cos(ang), jnp.sin(ang)
    rot = np.zeros((HEAD_DIM, HEAD_DIM), np.float32)
    half = HEAD_DIM // 2
    rot[np.arange(half) + half, np.arange(half)] = -1.0
    rot[np.arange(half), np.arange(half) + half] = 1.0
    return jnp.concatenate([cos, cos], axis=1), jnp.concatenate([sin, sin], axis=1), jnp.asarray(rot)


RW_CUTS = (0, 768, 1536, 2304, 2368, 2432, 2560)
RW_TILE = 128
ROW_TILE = 256


def _local_step(x0, memx, tgt, P, ex_weights=None, weights_done=None, ex_grads=None):
    T = x0.shape[0]
    P = dict(P)
    G = {}
    seg, seg_t = _segments(RWKV_WIDTH)
    cos, sin, rot = _rope_consts(T)
    row = lambda v: v.reshape(1, -1)

    def mem_fwd(i, qmem):
        memn = stage_fwd(f_rmsnorm, [memx], [P["mem_norm"][i:i + 1]], [], [], N_MEM, f"mem{i}_norm", [BF16])[0]
        kvm = matmul(memn, P["mem_w_kv"][i], "nn", f"mem{i}_kv")
        kraw, vm = _heads(kvm[:, :MEM_WIDTH], MEM_HEADS), _heads(kvm[:, MEM_WIDTH:], MEM_HEADS)
        km = stage_fwd(f_headnorm, [kraw], [P["mem_k_norm"][i:i + 1]], [], [], N_MEM, f"mem{i}_knorm")[0]
        qh = _heads(qmem, MEM_HEADS)
        om = stage_fwd(f_memattn, [qh], [km, vm, P["mem_q_norm"][i:i + 1]], [], [], ROW_TILE, f"mem{i}_attn")[0]
        return _unheads(om), (memn, kraw, vm, km, qh)

    def mem_bwd(i, saved, dymem):
        memn, kraw, vm, km, qh = saved
        (dqh,), (dkm, dvm, g_qn) = stage_bwd(f_memattn, [qh], [km, vm, P["mem_q_norm"][i:i + 1]], [], [],
                                             [_heads(dymem, MEM_HEADS)], ROW_TILE, f"mem{i}_attn_bwd")
        (dkraw,), (g_kn,) = stage_bwd(f_headnorm, [kraw], [P["mem_k_norm"][i:i + 1]], [], [], [dkm], N_MEM, f"mem{i}_knorm_bwd")
        dkvm = jnp.concatenate([_unheads(dkraw), _unheads(dvm)], axis=1)
        g_w = matmul(memn, dkvm, "tn", f"mem{i}_kv_dw")
        dmemn = matmul(dkvm, P["mem_w_kv"][i], "nt", f"mem{i}_kv_dx")
        _, (g_mn,) = stage_bwd(f_rmsnorm, [memx], [P["mem_norm"][i:i + 1]], [], [], [dmemn], N_MEM, f"mem{i}_norm_bwd")
        return _unheads(dqh), g_mn, g_w, g_qn, g_kn

    def ffn_fwd(i, xin):
        hn = stage_fwd(f_rmsnorm, [xin], [P["ffn_norm"][i:i + 1]], [], [], ROW_TILE, f"ffn{i}_norm", [BF16])[0]
        u = matmul(hn, P["ffn_w_up"][i], "nn", f"ffn{i}_up")
        z = convgate_fwd(u, P["ffn_conv_w"][i], P["ffn_conv_b"][i:i + 1], f"ffn{i}_conv")
        return matmul(z, P["ffn_w_down"][i], "nn", f"ffn{i}_down", residual=xin), (hn, u, z)

    def ffn_bwd(i, xin, saved, dxo):
        hn, u, z = saved
        dz = matmul(dxo, P["ffn_w_down"][i], "nt", f"ffn{i}_down_dx")
        g_down = matmul(z, dxo, "tn", f"ffn{i}_down_dw")
        dc, g_cw, g_cb = convgate_bwd_c(u, P["ffn_conv_w"][i], P["ffn_conv_b"][i:i + 1], dz, f"ffn{i}_conv_bwd_c")
        du = convgate_bwd_u(dc, P["ffn_conv_w"][i], f"ffn{i}_conv_bwd_u")
        dhn = matmul(du, P["ffn_w_up"][i], "nt", f"ffn{i}_up_dx")
        g_up = matmul(hn, du, "tn", f"ffn{i}_up_dw")
        (dxin,), (g_n,) = stage_bwd(f_rmsnorm_res, [xin], [P["ffn_norm"][i:i + 1]], [], [], [dhn, dxo], ROW_TILE, f"ffn{i}_norm_bwd")
        return dxin, g_n, g_up, g_cw, g_cb, g_down

    h0 = stage_fwd(f_rmsnorm, [x0], [P["attn_norm"][0:1]], [], [], ROW_TILE, "l0_norm", [BF16])[0]
    p0 = matmul(h0, P["a_w_in"][0], "nn", "l0_in")
    cur = [p0[:, a:b] for a, b in zip(RW_CUTS[:-1], RW_CUTS[1:])]
    prev = [_shift_down(c) for c in cur]
    mu = [P["a_mu"][:, a:b] for a, b in zip(RW_CUTS[:-1], RW_CUTS[1:])]
    pre_ps = mu + [P["a_w0"], P["a_w2"][0], P["a_a0"], P["a_a2"][0], P["a_g2"][0], P["a_k_k"], P["a_k_a"]]
    r, lw, k2, v, kk, b, g = stage_fwd(f_rwkv_pre, cur + prev, pre_ps, [], [seg, seg_t], RW_TILE, "l0_rwkv_pre")
    hm = lambda z: _heads(z, RWKV_HEADS)
    scan_in = [hm(z) for z in (r, lw, k2, v, kk, b)]
    y_h, h_states, got = rwkv_scan_fwd(*scan_in, ex_weights or Exchange())
    if weights_done is not None:
        P.update(weights_done(got))
    y_s = _unheads(y_h)
    post_ps = [P["a_lnx_w"], P["a_lnx_b"], P["a_r_k"].reshape(1, RWKV_WIDTH)]
    ymix0 = stage_fwd(f_rwkv_post, [y_s, r, k2, v, g], post_ps, [], [seg, seg_t], RW_TILE, "l0_rwkv_post")[0]
    ymem0, mem0_saved = mem_fwd(0, p0[:, SHIFT_WIDTH:])
    ycat0 = jnp.concatenate([ymix0, ymem0], axis=1).astype(BF16)
    x1 = matmul(ycat0, P["a_w_out"][0], "nn", "l0_out", residual=x0)
    x2, ffn0_saved = ffn_fwd(0, x1)

    hk, h1 = stage_fwd(f_rmsnorm2, [x2], [row(P["kv_norm"]), P["attn_norm"][1:2]], [], [], ROW_TILE, "l1_norm", [BF16, BF16])
    kvp = matmul(hk, P["kv_w"][0], "nn", "l1_kv")
    p1 = matmul(h1, P["b_w_in"][0], "nn", "l1_in")
    kraw, vsh, qraw = hm(kvp[:, :DIL_WIDTH]), hm(kvp[:, DIL_WIDTH:]), hm(p1[:, :DIL_WIDTH])
    ksh = stage_fwd(f_qkprep, [kraw], [row(P["kv_k_norm"])], [cos, sin], [rot], ROW_TILE, "l1_kprep")[0]
    q = stage_fwd(f_qkprep, [qraw], [P["b_q_norm"]], [cos, sin], [rot], ROW_TILE, "l1_qprep")[0]
    grp, outs, lses = [], [], []
    for gi, (_, d) in enumerate(DIL_GROUPS):
        hs = slice(4 * gi, 4 * gi + 4)
        qg, kg, vg = _regroup(q[hs], d), _regroup(ksh[hs], d), _regroup(vsh[hs], d)
        og, lg = dil_fwd(qg, kg, vg, f"l1_dil{gi}")
        grp.append((qg, kg, vg))
        outs.append(_ungroup(og, d))
        lses.append(_ungroup(lg, d))
    omix = stage_fwd(f_mix, outs + lses, [], [], [], ROW_TILE, "l1_mix")[0]
    ymem1, mem1_saved = mem_fwd(1, p1[:, DIL_WIDTH:])
    ycat1 = jnp.concatenate([_unheads(omix), ymem1], axis=1).astype(BF16)
    x3 = matmul(ycat1, P["b_w_out"][0], "nn", "l1_out", residual=x2)
    x4, ffn1_saved = ffn_fwd(1, x3)
    loss_part, dx4 = loss_head(x4, tgt)

    dx3, gn1, gup1, gcw1, gcb1, gdown1 = ffn_bwd(1, x3, ffn1_saved, dx4)
    dycat1 = matmul(dx3, P["b_w_out"][0], "nt", "l1_out_dx")
    G["b_w_out"] = [matmul(ycat1, dx3, "tn", "l1_out_dw")]
    dqmem1, gmn1, gmw1, gmq1, gmk1 = mem_bwd(1, mem1_saved, dycat1[:, MEM_WIDTH:])
    dmix, _ = stage_bwd(f_mix, outs + lses, [], [], [], [_heads(dycat1[:, :MEM_WIDTH], 4)], ROW_TILE, "l1_mix_bwd")
    dq, dk, dv = [], [], []
    for gi, (_, d) in enumerate(DIL_GROUPS):
        qg, kg, vg = grp[gi]
        dqg, dkc, dkp, dvc, dvp = dil_bwd(qg, kg, vg, _regroup(dmix[gi], d), _regroup(dmix[3 + gi], d), f"l1_dil{gi}_bwd")
        back = lambda c, p: c + jnp.concatenate([p[:, DIL_BLOCK:], jnp.zeros_like(p[:, :DIL_BLOCK])], axis=1)
        dq.append(_ungroup(dqg, d))
        dk.append(_ungroup(back(dkc, dkp), d))
        dv.append(_ungroup(back(dvc, dvp), d))
    dq, dk, dv = jnp.concatenate(dq, 0), jnp.concatenate(dk, 0), jnp.concatenate(dv, 0)
    (dqraw,), (g_bq,) = stage_bwd(f_qkprep, [qraw], [P["b_q_norm"]], [cos, sin], [rot], [dq], ROW_TILE, "l1_qprep_bwd")
    (dkraw,), (g_kk,) = stage_bwd(f_qkprep, [kraw], [row(P["kv_k_norm"])], [cos, sin], [rot], [dk], ROW_TILE, "l1_kprep_bwd")
    dp1 = jnp.concatenate([_unheads(dqraw), dqmem1], axis=1)
    dkvp = jnp.concatenate([_unheads(dkraw), _unheads(dv)], axis=1)
    dh1 = matmul(dp1, P["b_w_in"][0], "nt", "l1_in_dx")
    G["b_w_in"] = [matmul(h1, dp1, "tn", "l1_in_dw")]
    dhk = matmul(dkvp, P["kv_w"][0], "nt", "l1_kv_dx")
    G["kv_w"] = [matmul(hk, dkvp, "tn", "l1_kv_dw")]
    (dx2,), (g_kvn, g_an1) = stage_bwd(f_rmsnorm2_res, [x2], [row(P["kv_norm"]), P["attn_norm"][1:2]], [], [],
                                       [dhk, dh1, dx3], ROW_TILE, "l1_norm_bwd")

    dx1, gn0, gup0, gcw0, gcb0, gdown0 = ffn_bwd(0, x1, ffn0_saved, dx2)
    dycat0 = matmul(dx1, P["a_w_out"][0], "nt", "l0_out_dx")
    G["a_w_out"] = [matmul(ycat0, dx1, "tn", "l0_out_dw")]
    dqmem0, gmn0, gmw0, gmq0, gmk0 = mem_bwd(0, mem0_saved, dycat0[:, RWKV_WIDTH:])
    (dy_s, dr_a, dk_a, dv_a, dg), (g_lw, g_lb, g_rk) = stage_bwd(
        f_rwkv_post, [y_s, r, k2, v, g], post_ps, [], [seg, seg_t], [dycat0[:, :RWKV_WIDTH]], RW_TILE, "l0_rwkv_post_bwd")
    G["mem_w_kv"], G["ffn_w_up"], G["ffn_w_down"] = [gmw0, gmw1], [gup0, gup1], [gdown0, gdown1]
    d_scan, G["_exchanged"] = rwkv_scan_bwd(*scan_in, h_states, hm(dy_s), ex_grads(G) if ex_grads else Exchange())
    dr_b, dlw, dk_b, dv_b, dkk, db = [_unheads(z) for z in d_scan]
    dpre, gpre = stage_bwd(f_rwkv_pre, cur + prev, pre_ps, [], [seg, seg_t],
                           [[dr_a, dr_b], dlw, [dk_a, dk_b], [dv_a, dv_b], dkk, db, dg], RW_TILE, "l0_rwkv_pre_bwd")
    dp_rw = jnp.concatenate(dpre[:6], axis=1) + _shift_up(jnp.concatenate(dpre[6:], axis=1))
    dp0 = jnp.concatenate([dp_rw, dqmem0], axis=1)
    dh0 = matmul(dp0, P["a_w_in"][0], "nt", "l0_in_dx")
    G["a_w_in"] = [matmul(h0, dp0, "tn", "l0_in_dw")]
    (dx0,), (g_an0,) = stage_bwd(f_rmsnorm_res, [x0], [P["attn_norm"][0:1]], [], [], [dh0, dx1], ROW_TILE, "l0_norm_bwd")

    G["attn_norm"] = jnp.concatenate([g_an0, g_an1], axis=0)
    G["a_mu"] = jnp.concatenate(gpre[:6], axis=1)
    G["a_w0"], G["a_w2"], G["a_a0"], G["a_a2"], G["a_g2"] = gpre[6], gpre[7][None], gpre[8], gpre[9][None], gpre[10][None]
    G["a_k_k"], G["a_k_a"] = gpre[11], gpre[12]
    G["a_r_k"] = g_rk.reshape(1, RWKV_HEADS, HEAD_DIM)
    G["a_lnx_w"], G["a_lnx_b"] = g_lw, g_lb
    G["kv_norm"], G["kv_k_norm"], G["b_q_norm"] = g_kvn.reshape(-1), g_kk.reshape(-1), g_bq
    G["mem_norm"] = jnp.concatenate([gmn0, gmn1], axis=0)
    G["mem_w_kv"] = [gmw0, gmw1]
    G["mem_q_norm"] = jnp.concatenate([gmq0, gmq1], axis=0)
    G["mem_k_norm"] = jnp.concatenate([gmk0, gmk1], axis=0)
    G["ffn_norm"] = jnp.concatenate([gn0, gn1], axis=0)
    G["ffn_w_up"] = [gup0, gup1]
    G["ffn_conv_w"] = jnp.stack([gcw0, gcw1])
    G["ffn_conv_b"] = jnp.concatenate([gcb0, gcb1], axis=0)
    G["ffn_w_down"] = [gdown0, gdown1]
    return loss_part, dx0, G


PARAMS = (("attn_norm", None), ("a_w_in", 2), ("a_mu", 1), ("a_w0", 1), ("a_w2", 2), ("a_a0", 1), ("a_a2", 2), ("a_g2", 2),
          ("a_k_k", 1), ("a_k_a", 1), ("a_r_k", None), ("a_lnx_w", 1), ("a_lnx_b", 1), ("a_w_out", 1), ("kv_norm", None),
          ("kv_w", 1), ("kv_k_norm", None), ("b_w_in", 1), ("b_q_norm", None), ("b_w_out", 2), ("mem_norm", None),
          ("mem_w_kv", 1), ("mem_q_norm", None), ("mem_k_norm", None), ("ffn_norm", None), ("ffn_w_up", 2),
          ("ffn_conv_w", 2), ("ffn_conv_b", None), ("ffn_w_down", 1))
BIG = ("a_w_in", "a_w_out", "kv_w", "b_w_in", "b_w_out", "mem_w_kv", "ffn_w_up", "ffn_w_down")
AXIS = dict(PARAMS)
SMALL = tuple(n for n, _ in PARAMS if n not in BIG)
SMALL_SHARDED = tuple(n for n in SMALL if AXIS[n] is not None)
PACK_QUANTUM = 256 * 128


def _from_shards(xs, axis):
    full = jnp.moveaxis(xs, 0, axis)
    sh = full.shape
    return full.reshape(sh[:axis] + (sh[axis] * sh[axis + 1],) + sh[axis + 2:])


def _to_shards(g, axis):
    sh = g.shape
    return jnp.moveaxis(g.reshape(sh[:axis] + (N_DEV, sh[axis] // N_DEV) + sh[axis + 1:]), axis, 0)


def _pack(parts, lead=0):
    ld = parts[0].shape[:lead]
    flat = jnp.concatenate([p.reshape(ld + (-1,)) for p in parts], axis=-1)
    pad = (-flat.shape[-1]) % PACK_QUANTUM
    flat = jnp.pad(flat, [(0, 0)] * lead + [(0, pad)])
    return flat.reshape(ld + (-1, 128))


def _unpack(packed, shapes, lead=0):
    ld = packed.shape[:lead]
    flat = packed.reshape(ld + (-1,))
    out, off = [], 0
    for s in shapes:
        n = math.prod(s)
        out.append(flat[..., off:off + n].reshape(ld + tuple(s)))
        off += n
    return out


def kernel(x, mem, attn_norm, a_w_in, a_mu, a_w0, a_w2, a_a0, a_a2, a_g2, a_k_k, a_k_a, a_r_k, a_lnx_w, a_lnx_b, a_w_out, kv_norm, kv_w, kv_k_norm, b_w_in, b_q_norm, b_w_out, mem_norm, mem_w_kv, mem_q_norm, mem_k_norm, ffn_norm, ffn_w_up, ffn_conv_w, ffn_conv_b, ffn_w_down, loss_target, m_attn_norm, m_a_w_in, m_a_mu, m_a_w0, m_a_w2, m_a_a0, m_a_a2, m_a_g2, m_a_k_k, m_a_k_a, m_a_r_k, m_a_lnx_w, m_a_lnx_b, m_a_w_out, m_kv_norm, m_kv_w, m_kv_k_norm, m_b_w_in, m_b_q_norm, m_b_w_out, m_mem_norm, m_mem_w_kv, m_mem_q_norm, m_mem_k_norm, m_ffn_norm, m_ffn_w_up, m_ffn_conv_w, m_ffn_conv_b, m_ffn_w_down, v_attn_norm, v_a_w_in, v_a_mu, v_a_w0, v_a_w2, v_a_a0, v_a_a2, v_a_g2, v_a_k_k, v_a_k_a, v_a_r_k, v_a_lnx_w, v_a_lnx_b, v_a_w_out, v_kv_norm, v_kv_w, v_kv_k_norm, v_b_w_in, v_b_q_norm, v_b_w_out, v_mem_norm, v_mem_w_kv, v_mem_q_norm, v_mem_k_norm, v_ffn_norm, v_ffn_w_up, v_ffn_conv_w, v_ffn_conv_b, v_ffn_w_down):
    names = [n for n, _ in PARAMS]
    vals = (attn_norm, a_w_in, a_mu, a_w0, a_w2, a_a0, a_a2, a_g2, a_k_k, a_k_a, a_r_k, a_lnx_w, a_lnx_b, a_w_out, kv_norm, kv_w, kv_k_norm, b_w_in, b_q_norm, b_w_out, mem_norm, mem_w_kv, mem_q_norm, mem_k_norm, ffn_norm, ffn_w_up, ffn_conv_w, ffn_conv_b, ffn_w_down)
    m_vals = (m_attn_norm, m_a_w_in, m_a_mu, m_a_w0, m_a_w2, m_a_a0, m_a_a2, m_a_g2, m_a_k_k, m_a_k_a, m_a_r_k, m_a_lnx_w, m_a_lnx_b, m_a_w_out, m_kv_norm, m_kv_w, m_kv_k_norm, m_b_w_in, m_b_q_norm, m_b_w_out, m_mem_norm, m_mem_w_kv, m_mem_q_norm, m_mem_k_norm, m_ffn_norm, m_ffn_w_up, m_ffn_conv_w, m_ffn_conv_b, m_ffn_w_down)
    v_vals = (v_attn_norm, v_a_w_in, v_a_mu, v_a_w0, v_a_w2, v_a_a0, v_a_a2, v_a_g2, v_a_k_k, v_a_k_a, v_a_r_k, v_a_lnx_w, v_a_lnx_b, v_a_w_out, v_kv_norm, v_kv_w, v_kv_k_norm, v_b_w_in, v_b_q_norm, v_b_w_out, v_mem_norm, v_mem_w_kv, v_mem_q_norm, v_mem_k_norm, v_ffn_norm, v_ffn_w_up, v_ffn_conv_w, v_ffn_conv_b, v_ffn_w_down)
    W, M, V = dict(zip(names, vals)), dict(zip(names, m_vals)), dict(zip(names, v_vals))
    me = 4 * lax.axis_index("x") + 2 * lax.axis_index("y") + lax.axis_index("c")
    layers = lambda D, n: [D[n]] if D[n].ndim == 2 else [D[n][i] for i in range(D[n].shape[0])]
    ax2 = lambda n: AXIS[n] - (W[n].ndim - 2)
    first = [("a_w_in", 0)]
    later = [(n, i) for n in BIG if n != "a_w_in" for i in range(len(layers(W, n)))]

    small_shapes = [W[n].shape for n in SMALL_SHARDED]
    got_w, got_small = exchange(Exchange(gathers=[W["a_w_in"][0].astype(BF16), _pack([W[n] for n in SMALL_SHARDED])]),
                                "gather_first")
    P = {n: W[n] for n in SMALL}
    P["a_w_in"] = [_from_shards(got_w, ax2("a_w_in"))]
    for n, s in zip(SMALL_SHARDED, _unpack(got_small, small_shapes, lead=1)):
        P[n] = _from_shards(s, AXIS[n])
    ex_weights = Exchange(gathers=[layers(W, n)[i].astype(BF16) for n, i in later])

    def weights_done(got):
        out = {}
        for (n, _), g in zip(later, got):
            out.setdefault(n, []).append(_from_shards(g, ax2(n)))
        return out

    ex_grads = lambda G: Exchange(scatters=[_to_shards(G[n][i], ax2(n)) for n, i in later])
    loss_part, dx0, G = _local_step(x[0], mem[0], loss_target[0], P, ex_weights, weights_done, ex_grads)
    loss = lax.psum(loss_part[0, 0], ("x", "y", "c"))
    gparts = dict(zip(later, G.pop("_exchanged")))
    got_gsmall, gparts[first[0]] = exchange(
        Exchange(gathers=[_pack([G[n] for n in SMALL])], scatters=[_to_shards(G["a_w_in"][0], ax2("a_w_in"))]), "exchange_last")

    results = {}
    for n in BIG:
        per_layer = [adamw(gparts[(n, i)], w, m, v, f"adamw_{n}{i}")
                     for i, (w, m, v) in enumerate(zip(layers(W, n), layers(M, n), layers(V, n)))]
        results[n] = [r[0] if W[n].ndim == 2 else jnp.stack(r) for r in zip(*per_layer)]
    g_small = sum_parts(got_gsmall, "sum_small_grads")
    mine = []
    for n, g in zip(SMALL, _unpack(g_small, [G[n].shape for n in SMALL])):
        if AXIS[n] is not None:
            s = W[n].shape[AXIS[n]]
            g = lax.dynamic_slice_in_dim(g, me * s, s, axis=AXIS[n])
        mine.append(g)
    res = adamw(_pack(mine)[None], _pack([W[n] for n in SMALL]), _pack([M[n] for n in SMALL]), _pack([V[n] for n in SMALL]),
                "adamw_small")
    for n, parts in zip(SMALL, zip(*[_unpack(r, [W[n].shape for n in SMALL]) for r in res])):
        results[n] = list(parts)
    outs = [[results[n][j] for n in names] for j in range(4)]
    return (loss, dx0[None], *outs[0], *outs[1], *outs[2], *outs[3])
```

```python
import functools
import math

import jax
import jax.numpy as jnp
import numpy as np
from jax import lax
from jax.experimental import pallas as pl
from jax.experimental.pallas import tpu as pltpu

F32 = jnp.float32
BF16 = jnp.bfloat16
HI = lax.Precision.HIGHEST
H3 = lax.Precision.HIGH

N_DEV = 8
D_MODEL = 1024
HEAD_DIM = 64
N_MEM = 256
MEM_HEADS = 4
MEM_WIDTH = 256
RWKV_HEADS = 12
RWKV_WIDTH = 768
SHIFT_WIDTH = 2560
DIL_GROUPS = ((128, 1), (512, 4), (2048, 16))
DIL_BLOCK = 128
DIL_WIDTH = 768
D_FF = 2816
RMS_EPS = 1e-6
LNX_EPS = 64e-5
NEG_INF = -1e30
ROPE_THETA = 10000.0
ADAM_LR, ADAM_B1, ADAM_B2, ADAM_EPS, ADAM_WD, ADAM_STEP = 0.001, 0.9, 0.999, 1e-08, 0.01, 10

CHUNK = 64
SCAN_GROUPS_FWD, SCAN_GROUPS_BWD = 1, 2
MM_TILE_CAP = 1408
VMEM_LIMIT_V7X = 48 * 1024 * 1024


def _cparams(sem):
    return pltpu.CompilerParams(dimension_semantics=sem, vmem_limit_bytes=VMEM_LIMIT_V7X)


def _pick(n, cands):
    for c in cands:
        if n % c == 0:
            return c
    return n


def _tile(n, cap):
    if n <= cap:
        return n
    for d in range(cap - cap % 128, 0, -128):
        if n % d == 0:
            return d
    return n


def _dg(a, b, ca, cb, batch):
    dims = (((ca,), (cb,)), ((0,), (0,))) if batch else (((ca,), (cb,)), ((), ()))
    return lax.dot_general(a.astype(BF16), b.astype(BF16), dims, preferred_element_type=F32)


@jax.custom_vjp
def mm_nn(a, b):
    n = a.ndim
    return _dg(a, b, n - 1, n - 2, n == 3)


def _mm_nn_fwd(a, b):
    return mm_nn(a, b), (a, b)


def _mm_nn_bwd(res, g):
    a, b = res
    n = a.ndim
    return _dg(g, b, n - 1, n - 1, n == 3), _dg(a, g, n - 2, n - 2, n == 3)


mm_nn.defvjp(_mm_nn_fwd, _mm_nn_bwd)


@jax.custom_vjp
def mm_nt(a, b):
    n = a.ndim
    return _dg(a, b, n - 1, n - 1, n == 3)


def _mm_nt_fwd(a, b):
    return mm_nt(a, b), (a, b)


def _mm_nt_bwd(res, g):
    a, b = res
    n = a.ndim
    return _dg(g, b, n - 1, n - 2, n == 3), _dg(g, a, n - 2, n - 2, n == 3)


mm_nt.defvjp(_mm_nt_fwd, _mm_nt_bwd)


def mmh(a, b, precision=H3):
    n = a.ndim
    dims = (((n - 1,), (n - 2,)), ((0,), (0,))) if n == 3 else (((1,), (0,)), ((), ()))
    return lax.dot_general(a, b, dims, precision=precision, preferred_element_type=F32)


def mmh_nt(a, b):
    n = a.ndim
    dims = (((n - 1,), (n - 1,)), ((0,), (0,))) if n == 3 else (((1,), (1,)), ((), ()))
    return lax.dot_general(a, b, dims, precision=H3, preferred_element_type=F32)


def mmh_tn(a, b):
    n = a.ndim
    dims = (((n - 2,), (n - 2,)), ((0,), (0,))) if n == 3 else (((0,), (0,)), ((), ()))
    return lax.dot_general(a, b, dims, precision=H3, preferred_element_type=F32)


def matmul(a, b, mode, name, residual=None):
    if mode == "nn":
        (M, K), (_, N) = a.shape, b.shape
    elif mode == "nt":
        (M, K), (N, _) = a.shape, b.shape
    else:
        (K, M), (_, N) = a.shape, b.shape
    tm = _tile(M, 2048 if mode == "nn" else MM_TILE_CAP)
    tn = _tile(N, 512 if mode == "nn" else MM_TILE_CAP)
    tk = _tile(K, MM_TILE_CAP if mode != "nt" else 512)
    nk = K // tk
    if mode == "nn":
        a_spec = pl.BlockSpec((tm, tk), lambda i, j, k: (i, k))
        b_spec = pl.BlockSpec((tk, tn), lambda i, j, k: (k, j))
        dims = (((1,), (0,)), ((), ()))
    elif mode == "nt":
        a_spec = pl.BlockSpec((tm, tk), lambda i, j, k: (i, k))
        b_spec = pl.BlockSpec((tn, tk), lambda i, j, k: (j, k))
        dims = (((1,), (1,)), ((), ()))
    else:
        a_spec = pl.BlockSpec((tk, tm), lambda i, j, k: (k, i))
        b_spec = pl.BlockSpec((tk, tn), lambda i, j, k: (k, j))
        dims = (((0,), (0,)), ((), ()))
    o_spec = pl.BlockSpec((tm, tn), lambda i, j, k: (i, j))
    has_res = residual is not None

    def body(*refs):
        if has_res:
            a_ref, b_ref, r_ref, o_ref, acc_ref = refs
        else:
            a_ref, b_ref, o_ref, acc_ref = refs
        k = pl.program_id(2)

        @pl.when(k == 0)
        def _():
            acc_ref[...] = jnp.zeros_like(acc_ref)

        acc_ref[...] += lax.dot_general(a_ref[...].astype(BF16), b_ref[...].astype(BF16), dims,
                                        preferred_element_type=F32)

        @pl.when(k == nk - 1)
        def _():
            if has_res:
                o_ref[...] = acc_ref[...] + r_ref[...]
            else:
                o_ref[...] = acc_ref[...]

    ins = [a, b] + ([residual] if has_res else [])
    in_specs = [a_spec, b_spec] + ([o_spec] if has_res else [])
    return pl.pallas_call(
        body, grid=(M // tm, N // tn, nk), in_specs=in_specs, out_specs=o_spec,
        out_shape=jax.ShapeDtypeStruct((M, N), F32), scratch_shapes=[pltpu.VMEM((tm, tn), F32)],
        compiler_params=_cparams(("parallel", "parallel", "arbitrary")), name=name)(*ins)


def _tok_spec(shape, tile):
    nd = len(shape)
    return pl.BlockSpec(shape[:-2] + (tile, shape[-1]), lambda i: (0,) * (nd - 2) + (i, 0))


def _full_spec(shape):
    nd = len(shape)
    return pl.BlockSpec(shape, lambda i: (0,) * nd)


def _blk(x, tile):
    return jax.ShapeDtypeStruct(x.shape[:-2] + (tile, x.shape[-1]), x.dtype)


def stage_fwd(f, xs, ps, cts, cfs, tile, name, out_dtypes=None):
    tiled, full = list(xs) + list(cts), list(ps) + list(cfs)
    nx, np_, nct = len(xs), len(ps), len(cts)
    T = tiled[0].shape[-2]
    order = lambda t, fl: list(t[:nx]) + list(fl[:np_]) + list(t[nx:]) + list(fl[np_:])
    out_avals = jax.eval_shape(f, *order([_blk(x, tile) for x in tiled], full))
    if out_dtypes is None:
        out_dtypes = [o.dtype for o in out_avals]
    out_shape = [jax.ShapeDtypeStruct(o.shape[:-2] + (T, o.shape[-1]), dt) for o, dt in zip(out_avals, out_dtypes)]
    nt, nf = len(tiled), len(full)

    def body(*refs):
        tv = [r[...] for r in refs[:nt]]
        fv = [r[...] for r in refs[nt:nt + nf]]
        res = f(*order(tv, fv))
        for o_ref, r in zip(refs[nt + nf:], res):
            o_ref[...] = r.astype(o_ref.dtype)

    return pl.pallas_call(
        body, grid=(T // tile,),
        in_specs=[_tok_spec(x.shape, tile) for x in tiled] + [_full_spec(p.shape) for p in full],
        out_specs=[_tok_spec(o.shape, tile) for o in out_shape], out_shape=out_shape,
        compiler_params=_cparams(("parallel",)), name=name)(*tiled, *full)


def stage_bwd(f, xs, ps, cts, cfs, gs, tile, name, bf16_copies=()):
    gs = [list(g) if isinstance(g, (list, tuple)) else [g] for g in gs]
    g_flat = [a for g in gs for a in g]
    tiled, full = list(xs) + list(cts) + g_flat, list(ps) + list(cfs)
    nx, np_, nct = len(xs), len(ps), len(cts)
    T = tiled[0].shape[-2]
    nt, nf = len(tiled), len(full)
    out_shape = ([jax.ShapeDtypeStruct(x.shape, F32) for x in xs] + [jax.ShapeDtypeStruct(p.shape, F32) for p in ps]
                 + [jax.ShapeDtypeStruct(xs[i].shape, BF16) for i in bf16_copies])

    def body(*refs):
        tv = [r[...] for r in refs[:nt]]
        fv = [r[...] for r in refs[nt:nt + nf]]
        outs = refs[nt + nf:]
        xv, ctv, gparts = tv[:nx], tv[nx:nx + nct], tv[nx + nct:]
        gv = []
        for g in gs:
            gv.append(functools.reduce(lambda a, b: a + b, gparts[:len(g)]))
            gparts = gparts[len(g):]
        pv, cfv = fv[:np_], fv[np_:]
        _, vjp = jax.vjp(lambda *xp: f(*xp, *ctv, *cfv), *xv, *pv)
        d = vjp(tuple(gv))
        for o_ref, r in zip(outs[:nx], d[:nx]):
            o_ref[...] = r
        for o_ref, i in zip(outs[nx + np_:], bf16_copies):
            o_ref[...] = d[i].astype(BF16)

        @pl.when(pl.program_id(0) == 0)
        def _():
            for o_ref in outs[nx:nx + np_]:
                o_ref[...] = jnp.zeros_like(o_ref)

        for o_ref, r in zip(outs[nx:nx + np_], d[nx:]):
            o_ref[...] += r

    res = pl.pallas_call(
        body, grid=(T // tile,),
        in_specs=[_tok_spec(x.shape, tile) for x in tiled] + [_full_spec(p.shape) for p in full],
        out_specs=([_tok_spec(x.shape, tile) for x in xs] + [_full_spec(p.shape) for p in ps]
                   + [_tok_spec(xs[i].shape, tile) for i in bf16_copies]), out_shape=out_shape,
        compiler_params=_cparams(("arbitrary",)), name=name)(*tiled, *full)
    if bf16_copies:
        return list(res[:nx]), list(res[nx:nx + np_]), list(res[nx + np_:])
    return list(res[:nx]), list(res[nx:])


def _rms(x, g, eps=RMS_EPS):
    return x * lax.rsqrt(jnp.mean(x * x, axis=-1, keepdims=True) + eps) * g


def f_rmsnorm(x, g):
    return (_rms(x, g),)


def f_rmsnorm_res(x, g):
    return _rms(x, g), x


def f_rmsnorm2(x, g1, g2):
    n = x * lax.rsqrt(jnp.mean(x * x, axis=-1, keepdims=True) + RMS_EPS)
    return n * g1, n * g2


def f_rmsnorm2_res(x, g1, g2):
    return f_rmsnorm2(x, g1, g2) + (x,)


def _sigmoid(x):
    return 1.0 / (1.0 + jnp.exp(-x))


def _softplus(x):
    return jnp.maximum(x, 0.0) + jnp.log(1.0 + jnp.exp(-jnp.abs(x)))


def f_rwkv_pre(pr, pk, pv, pwd, pad, pgd, qr, qk, qv, qwd, qad, qgd,
               mu_r, mu_k, mu_v, mu_wd, mu_ad, mu_gd, w0, w2, a0, a2, g2, k_k, k_a, seg, seg_t):
    xr = pr + (qr - pr) * mu_r
    xk = pk + (qk - pk) * mu_k
    xv = pv + (qv - pv) * mu_v
    xwd = pwd + (qwd - pwd) * mu_wd
    xad = pad + (qad - pad) * mu_ad
    xgd = pgd + (qgd - pgd) * mu_gd
    w_log = -_softplus(-(w0 + mm_nn(jnp.tanh(xwd), w2))) - 0.5
    lw = -jnp.exp(w_log)
    a = _sigmoid(a0 + mm_nn(xad, a2))
    g = mm_nn(_sigmoid(xgd), g2)
    kkr = xk * k_k
    inv = lax.rsqrt(jnp.maximum(mmh(kkr * kkr, seg), 1e-24))
    kk = kkr * mmh(inv, seg_t)
    k2 = xk * (1.0 + (a - 1.0) * k_a)
    return xr, lw, k2, xv, kk, kk * a, g


def f_rwkv_post(y, r, k2, v, g, lnx_w, lnx_b, r_k, seg, seg_t):
    inv_n = 1.0 / HEAD_DIM
    m = mmh(mmh(y, seg) * inv_n, seg_t)
    yc = y - m
    rstd = lax.rsqrt(mmh(yc * yc, seg) * inv_n + LNX_EPS)
    yn = yc * mmh(rstd, seg_t) * lnx_w + lnx_b
    bonus = mmh(mmh(r * k2 * r_k, seg), seg_t) * v
    return ((yn + bonus) * g,)


def f_headnorm(z, g):
    return (_rms(z, g),)


def f_qkprep(z, g, cos, sin, rot):
    h, t, d = z.shape
    zn = _rms(z, g)
    zr = mmh(zn.reshape(h * t, d), rot).reshape(h, t, d)
    return (zn * cos + zr * sin,)


def f_memattn(q, k, v, q_norm):
    qh = _rms(q, q_norm)
    s = mm_nt(qh, k) * (1.0 / math.sqrt(HEAD_DIM))
    s = s - jnp.max(s, axis=-1, keepdims=True)
    p = jnp.exp(s)
    p = p / jnp.sum(p, axis=-1, keepdims=True)
    return (mm_nn(p, v),)


def f_mix(o1, o2, o3, l1, l2, l3):
    mx = jnp.maximum(jnp.maximum(l1, l2), l3)
    e1, e2, e3 = jnp.exp(l1 - mx), jnp.exp(l2 - mx), jnp.exp(l3 - mx)
    return ((e1 * o1 + e2 * o2 + e3 * o3) / (e1 + e2 + e3),)


def _chunk_masks(L):
    t = lax.broadcasted_iota(jnp.int32, (L, L), 0)
    s = lax.broadcasted_iota(jnp.int32, (L, L), 1)
    return t, s


def _unit_lower_inverse(a):
    L = a.shape[-1]
    t, s = _chunk_masks(L)
    one = jnp.ones((), F32)
    blk = lambda sh: jnp.where((t >> sh) == (s >> sh), one, 0.0)
    n0 = a * blk(3)
    x = jnp.where(t == s, one, 0.0) - n0
    n2 = mmh(n0, n0)
    x = x + mmh(x, n2)
    x = x + mmh(x, mmh(n2, n2))
    for sh in (3, 4, 5):
        if (1 << sh) >= L:
            break
        off = a * (blk(sh + 1) - blk(sh))
        x = x - mmh(x, mmh(off, x))
    return x


@jax.custom_vjp
def _inverse_known(a, x):
    return x


def _inverse_known_fwd(a, x):
    return x, x


def _inverse_known_bwd(x, dx):
    return -mmh_nt(mmh_tn(x, dx), x), jnp.zeros_like(x)


_inverse_known.defvjp(_inverse_known_fwd, _inverse_known_bwd)


def f_rwkv_chunk(h0, r, lw, k, v, kk, b, x_known=None):
    H, L, _ = r.shape
    t, s = _chunk_masks(L)
    one = jnp.ones((), F32)
    incl = jnp.where(t >= s, one, 0.0)
    strict = jnp.where(t > s, one, 0.0)
    cum = mmh(jnp.broadcast_to(incl, (H, L, L)), lw, HI)
    w_in = jnp.exp(cum)
    w_ex = jnp.exp(cum - lw)
    w_inv = jnp.exp(-cum)
    rt, kkt, kt, bt = r * w_in, kk * w_ex, k * w_inv, b * w_inv
    a_b = mmh_nt(kkt, bt) * strict
    a_k = mmh_nt(kkt, kt) * strict
    m_k = mmh_nt(rt, kt) * incl
    m_b = mmh_nt(rt, bt) * incl
    x = _unit_lower_inverse(a_b) if x_known is None else _inverse_known(a_b, x_known)
    u = mmh(x, mmh(kkt, h0) + mmh(a_k, v))
    y = mmh(rt, h0) + mmh(m_k, v) - mmh(m_b, u)
    w_last = jnp.exp(jnp.sum(lw, axis=1))
    h1 = w_last[:, :, None] * (h0 + mmh_tn(kt, v) - mmh_tn(bt, u))
    return y, h1, x


def _ex_split(ex, refs, n_in, n_out):
    n = ex.n
    ins, ex_in = refs[:n_in], refs[n_in:n_in + n]
    outs, ex_out = refs[n_in + n:n_in + n + n_out], refs[n_in + n + n_out:n_in + 2 * n + n_out]
    rest = refs[n_in + 2 * n + n_out:]
    return ins, outs, rest[:len(rest) - 3], (ex_in, ex_out) + tuple(rest[len(rest) - 3:])


def rwkv_scan_fwd(r, lw, k, v, kk, b, ex):
    H, T, N = r.shape
    groups = SCAN_GROUPS_FWD
    nc, hg = T // CHUNK, H // groups
    seq = pl.BlockSpec((hg, CHUNK, N), lambda g, c: (g, c, 0))

    def body(*refs):
        (r_ref, lw_ref, k_ref, v_ref, kk_ref, b_ref), (y_ref, hs_ref, xs_ref), (h_scr,), ex_refs = _ex_split(ex, refs, 6, 3)
        g, c = pl.program_id(0), pl.program_id(1)

        @pl.when(jnp.logical_and(g == 0, c == 0))
        def _():
            ex.start(*ex_refs)

        @pl.when(c == 0)
        def _():
            h_scr[...] = jnp.zeros_like(h_scr)

        h0 = h_scr[...]
        hs_ref[0] = h0
        y, h1, x = f_rwkv_chunk(h0, r_ref[...], lw_ref[...], k_ref[...], v_ref[...], kk_ref[...], b_ref[...])
        y_ref[...] = y
        xs_ref[0] = x
        h_scr[...] = h1

        @pl.when(jnp.logical_and(g == groups - 1, c == nc - 1))
        def _():
            ex.wait(*ex_refs)

    res = pl.pallas_call(
        body, grid=(groups, nc), in_specs=[seq] * 6 + [_ANY] * ex.n,
        out_specs=[seq, pl.BlockSpec((1, hg, N, N), lambda g, c: (c, g, 0, 0)),
                   pl.BlockSpec((1, hg, CHUNK, CHUNK), lambda g, c: (c, g, 0, 0))] + [_ANY] * ex.n,
        out_shape=[jax.ShapeDtypeStruct((H, T, N), F32), jax.ShapeDtypeStruct((nc, H, N, N), F32),
                   jax.ShapeDtypeStruct((nc, H, CHUNK, CHUNK), F32)] + ex.out_shape(),
        scratch_shapes=[pltpu.VMEM((hg, N, N), F32)] + ex.scratch(),
        compiler_params=_cparams(("arbitrary", "arbitrary")), name="rwkv_scan_fwd")(r, lw, k, v, kk, b, *ex.operands())
    return res[0], (res[1], res[2]), list(res[3:])


def rwkv_scan_bwd(r, lw, k, v, kk, b, saved, dy, ex):
    H, T, N = r.shape
    groups = SCAN_GROUPS_BWD
    nc, hg = T // CHUNK, H // groups
    seq = pl.BlockSpec((hg, CHUNK, N), lambda g, c: (g, nc - 1 - c, 0))
    state = pl.BlockSpec((1, hg, N, N), lambda g, c: (nc - 1 - c, g, 0, 0))

    def body(*refs):
        (r_ref, lw_ref, k_ref, v_ref, kk_ref, b_ref, hs_ref, xs_ref, dy_ref), outs, (dh_scr,), ex_refs = _ex_split(ex, refs, 9, 6)
        g, c = pl.program_id(0), pl.program_id(1)

        @pl.when(jnp.logical_and(g == 0, c == 0))
        def _():
            ex.start(*ex_refs)

        @pl.when(c == 0)
        def _():
            dh_scr[...] = jnp.zeros_like(dh_scr)

        x_known = xs_ref[0]
        _, vjp = jax.vjp(lambda *a: f_rwkv_chunk(*a, x_known=x_known)[:2], hs_ref[0], r_ref[...], lw_ref[...], k_ref[...],
                         v_ref[...], kk_ref[...], b_ref[...])
        d = vjp((dy_ref[...], dh_scr[...]))
        dh_scr[...] = d[0]
        for o_ref, dz in zip(outs, d[1:]):
            o_ref[...] = dz

        @pl.when(jnp.logical_and(g == groups - 1, c == nc - 1))
        def _():
            ex.wait(*ex_refs)

    res = pl.pallas_call(
        body, grid=(groups, nc),
        in_specs=[seq] * 6 + [state, state, seq] + [_ANY] * ex.n,
        out_specs=[seq] * 6 + [_ANY] * ex.n, out_shape=[jax.ShapeDtypeStruct((H, T, N), F32)] * 6 + ex.out_shape(),
        scratch_shapes=[pltpu.VMEM((hg, N, N), F32)] + ex.scratch(),
        compiler_params=_cparams(("arbitrary", "arbitrary")), name="rwkv_scan_bwd")(r, lw, k, v, kk, b, *saved, dy, *ex.operands())
    return list(res[:6]), list(res[6:])


DIL_ROWS = 4


def _f_dilattn(has_prev, q, kc, kp, vc, vp):
    scale = 1.0 / math.sqrt(HEAD_DIM)
    sc = mm_nt(q, kc) * scale
    sp = mm_nt(q, kp) * scale
    i = lax.broadcasted_iota(jnp.int32, (DIL_BLOCK, DIL_BLOCK), 0)
    j = lax.broadcasted_iota(jnp.int32, (DIL_BLOCK, DIL_BLOCK), 1)
    sc = jnp.where(j <= i, sc, NEG_INF)
    sp = jnp.where(jnp.logical_and(i <= j, has_prev), sp, NEG_INF)
    mx = jnp.maximum(jnp.max(sc, axis=-1, keepdims=True), jnp.max(sp, axis=-1, keepdims=True))
    pc, pp = jnp.exp(sc - mx), jnp.exp(sp - mx)
    den = jnp.sum(pc, axis=-1, keepdims=True) + jnp.sum(pp, axis=-1, keepdims=True)
    o = (mm_nn(pc, vc) + mm_nn(pp, vp)) / den
    return o, mx + jnp.log(den)


def _dil_specs(R, U):
    cur = pl.BlockSpec((DIL_ROWS, DIL_BLOCK, HEAD_DIM), lambda g, n: (g, n, 0))
    prev = pl.BlockSpec((DIL_ROWS, DIL_BLOCK, HEAD_DIM), lambda g, n: (g, jnp.maximum(n - 1, 0), 0))
    lse = pl.BlockSpec((DIL_ROWS, DIL_BLOCK, 1), lambda g, n: (g, n, 0))
    return cur, prev, lse, (R // DIL_ROWS, U // DIL_BLOCK)


def dil_fwd(q, k, v, name):
    R, U, _ = q.shape
    cur, prev, lse, grid = _dil_specs(R, U)

    def body(q_ref, kc_ref, kp_ref, vc_ref, vp_ref, o_ref, l_ref):
        o, l = _f_dilattn(pl.program_id(1) > 0, q_ref[...], kc_ref[...], kp_ref[...], vc_ref[...], vp_ref[...])
        o_ref[...] = o
        l_ref[...] = l

    return pl.pallas_call(
        body, grid=grid, in_specs=[cur, cur, prev, cur, prev], out_specs=[cur, lse],
        out_shape=[jax.ShapeDtypeStruct((R, U, HEAD_DIM), F32), jax.ShapeDtypeStruct((R, U, 1), F32)],
        compiler_params=_cparams(("parallel", "parallel")), name=name)(q, k, k, v, v)


def dil_bwd(q, k, v, do, dl, name):
    R, U, _ = q.shape
    cur, prev, lse, grid = _dil_specs(R, U)

    def body(q_ref, kc_ref, kp_ref, vc_ref, vp_ref, do_ref, dl_ref, *outs):
        f = functools.partial(_f_dilattn, pl.program_id(1) > 0)
        _, vjp = jax.vjp(f, q_ref[...], kc_ref[...], kp_ref[...], vc_ref[...], vp_ref[...])
        for o_ref, g in zip(outs, vjp((do_ref[...], dl_ref[...]))):
            o_ref[...] = g

    return pl.pallas_call(
        body, grid=grid, in_specs=[cur, cur, prev, cur, prev, cur, lse], out_specs=[cur] * 5,
        out_shape=[jax.ShapeDtypeStruct((R, U, HEAD_DIM), F32)] * 5,
        compiler_params=_cparams(("parallel", "parallel")), name=name)(q, k, k, v, v, do, dl)


CONV_TILE = 128


def _conv3(u, h6, h7, w, b):
    rows = lax.broadcasted_iota(jnp.int32, (u.shape[0], 1), 0)
    s1 = jnp.where(rows == 0, h7, pltpu.roll(u, 1, 0))
    s2 = jnp.where(rows == 0, h6, jnp.where(rows == 1, h7, pltpu.roll(u, 2, 0)))
    return b + w[0:1] * s2 + w[1:2] * s1 + w[2:3] * u, s1, s2


def _conv_halves(u_ref, h_ref, cw_ref, cb_ref):
    F = D_FF
    first = pl.program_id(0) > 0
    res = []
    for lo in (0, F):
        h = h_ref[:, lo:lo + F]
        h6 = jnp.where(first, h[6:7], 0.0)
        h7 = jnp.where(first, h[7:8], 0.0)
        u = u_ref[:, lo:lo + F]
        res.append((u,) + _conv3(u, h6, h7, cw_ref[:, lo:lo + F], cb_ref[:, lo:lo + F]))
    return res


def _halo_before(C):
    return pl.BlockSpec((8, C), lambda i: (jnp.maximum(i * (CONV_TILE // 8) - 1, 0), 0))


def convgate_fwd(u, cw, cb, name):
    T, C = u.shape
    F = C // 2

    def body(u_ref, h_ref, cw_ref, cb_ref, z_ref):
        (_, cg, _, _), (_, cv, _, _) = _conv_halves(u_ref, h_ref, cw_ref, cb_ref)
        z_ref[...] = (cg * _sigmoid(cg) * cv).astype(BF16)

    return pl.pallas_call(
        body, grid=(T // CONV_TILE,),
        in_specs=[pl.BlockSpec((CONV_TILE, C), lambda i: (i, 0)), _halo_before(C), _full_spec(cw.shape), _full_spec(cb.shape)],
        out_specs=pl.BlockSpec((CONV_TILE, F), lambda i: (i, 0)), out_shape=jax.ShapeDtypeStruct((T, F), BF16),
        compiler_params=_cparams(("parallel",)), name=name)(u, u, cw, cb)


def convgate_bwd_c(u, cw, cb, dz, name):
    T, C = u.shape
    F = C // 2

    def body(u_ref, h_ref, cw_ref, cb_ref, dz_ref, dc_ref, dcw_ref, dcb_ref):
        (ug, cg, g1, g2), (uv, cv, v1, v2) = _conv_halves(u_ref, h_ref, cw_ref, cb_ref)
        dz = dz_ref[...]
        sg = _sigmoid(cg)
        dgate = dz * cv * sg * (1.0 + cg * (1.0 - sg))
        dval = dz * cg * sg
        dc_ref[:, :F] = dgate
        dc_ref[:, F:] = dval

        @pl.when(pl.program_id(0) == 0)
        def _():
            dcw_ref[...] = jnp.zeros_like(dcw_ref)
            dcb_ref[...] = jnp.zeros_like(dcb_ref)

        for lo, d, u0, s1, s2 in ((0, dgate, ug, g1, g2), (F, dval, uv, v1, v2)):
            dcb_ref[:, lo:lo + F] += jnp.sum(d, axis=0, keepdims=True)
            dcw_ref[0:1, lo:lo + F] += jnp.sum(d * s2, axis=0, keepdims=True)
            dcw_ref[1:2, lo:lo + F] += jnp.sum(d * s1, axis=0, keepdims=True)
            dcw_ref[2:3, lo:lo + F] += jnp.sum(d * u0, axis=0, keepdims=True)

    return pl.pallas_call(
        body, grid=(T // CONV_TILE,),
        in_specs=[pl.BlockSpec((CONV_TILE, C), lambda i: (i, 0)), _halo_before(C), _full_spec(cw.shape), _full_spec(cb.shape),
                  pl.BlockSpec((CONV_TILE, F), lambda i: (i, 0))],
        out_specs=[pl.BlockSpec((CONV_TILE, C), lambda i: (i, 0)), _full_spec(cw.shape), _full_spec(cb.shape)],
        out_shape=[jax.ShapeDtypeStruct((T, C), F32), jax.ShapeDtypeStruct(cw.shape, F32), jax.ShapeDtypeStruct(cb.shape, F32)],
        compiler_params=_cparams(("arbitrary",)), name=name)(u, u, cw, cb, dz)


def convgate_bwd_u(dc, cw, name):
    T, C = dc.shape
    n = T // CONV_TILE

    def body(dc_ref, nx_ref, cw_ref, du_ref):
        last = pl.program_id(0) < n - 1
        rows = lax.broadcasted_iota(jnp.int32, (CONV_TILE, 1), 0)
        for lo in (0, C // 2):
            sl = slice(lo, lo + C // 2)
            d = dc_ref[:, sl]
            n0 = jnp.where(last, nx_ref[0:1, sl], 0.0)
            n1 = jnp.where(last, nx_ref[1:2, sl], 0.0)
            up1 = jnp.where(rows == CONV_TILE - 1, n0, pltpu.roll(d, CONV_TILE - 1, 0))
            up2 = jnp.where(rows == CONV_TILE - 1, n1, jnp.where(rows == CONV_TILE - 2, n0, pltpu.roll(d, CONV_TILE - 2, 0)))
            du_ref[:, sl] = (cw_ref[2:3, sl] * d + cw_ref[1:2, sl] * up1 + cw_ref[0:1, sl] * up2).astype(BF16)

    nxt = pl.BlockSpec((8, C), lambda i: (jnp.minimum((i + 1) * (CONV_TILE // 8), T // 8 - 1), 0))
    return pl.pallas_call(
        body, grid=(n,), in_specs=[pl.BlockSpec((CONV_TILE, C), lambda i: (i, 0)), nxt, _full_spec(cw.shape)],
        out_specs=pl.BlockSpec((CONV_TILE, C), lambda i: (i, 0)), out_shape=jax.ShapeDtypeStruct((T, C), BF16),
        compiler_params=_cparams(("parallel",)), name=name)(dc, dc, cw)


def loss_head(y, tgt):
    T, D = y.shape
    tile = 256

    def body(y_ref, t_ref, l_ref, d_ref, db_ref):
        d = y_ref[...] - t_ref[...]
        d_ref[...] = d * (1.0 / D)
        db_ref[...] = (d * (1.0 / D)).astype(BF16)

        @pl.when(pl.program_id(0) == 0)
        def _():
            l_ref[...] = jnp.zeros_like(l_ref)

        l_ref[...] += (0.5 / D) * jnp.sum(d * d)

    row = pl.BlockSpec((tile, D), lambda i: (i, 0))
    return pl.pallas_call(
        body, grid=(T // tile,), in_specs=[row, row], out_specs=[pl.BlockSpec((8, 128), lambda i: (0, 0)), row, row],
        out_shape=[jax.ShapeDtypeStruct((8, 128), F32), jax.ShapeDtypeStruct((T, D), F32), jax.ShapeDtypeStruct((T, D), BF16)],
        compiler_params=_cparams(("arbitrary",)), name="loss_head")(y, tgt)


def sum_parts(parts, name):
    S, R, C = parts.shape
    tile = _pick(R, (256, 128, 64, 32, 16, 8))

    def body(p_ref, o_ref):
        acc = p_ref[0]
        for s in range(1, S):
            acc = acc + p_ref[s]
        o_ref[...] = acc

    return pl.pallas_call(
        body, grid=(R // tile,), in_specs=[pl.BlockSpec((S, tile, C), lambda i: (0, i, 0))],
        out_specs=pl.BlockSpec((tile, C), lambda i: (i, 0)), out_shape=jax.ShapeDtypeStruct((R, C), F32),
        compiler_params=_cparams(("parallel",)), name=name)(parts)


def adamw(gparts, w, m, v, name):
    S, R, C = gparts.shape
    tile = _pick(R, (256, 128, 64, 32, 16, 8))
    c1 = 1.0 / (1.0 - ADAM_B1 ** ADAM_STEP)
    c2 = 1.0 / (1.0 - ADAM_B2 ** ADAM_STEP)

    def body(g_ref, w_ref, m_ref, v_ref, go_ref, d_ref, mo_ref, vo_ref):
        g = g_ref[0]
        for s in range(1, S):
            g = g + g_ref[s]
        m1 = ADAM_B1 * m_ref[...] + (1.0 - ADAM_B1) * g
        v1 = ADAM_B2 * v_ref[...] + (1.0 - ADAM_B2) * (g * g)
        go_ref[...] = g
        mo_ref[...] = m1
        vo_ref[...] = v1
        d_ref[...] = -ADAM_LR * ((m1 * c1) / (jnp.sqrt(v1 * c2) + ADAM_EPS) + ADAM_WD * w_ref[...])

    row = pl.BlockSpec((tile, C), lambda i: (i, 0))
    return pl.pallas_call(
        body, grid=(R // tile,), in_specs=[pl.BlockSpec((S, tile, C), lambda i: (0, i, 0)), row, row, row],
        out_specs=[row] * 4, out_shape=[jax.ShapeDtypeStruct((R, C), F32)] * 4,
        compiler_params=_cparams(("parallel",)), name=name)(gparts, w, m, v)


def _peers():
    x, y, c = lax.axis_index("x"), lax.axis_index("y"), lax.axis_index("c")
    peers = []
    for k in range(1, N_DEV):
        px = 1 - x if k & 4 else x
        py = 1 - y if k & 2 else y
        pc = 1 - c if k & 1 else c
        peers.append(((px, py, pc), 4 * px + 2 * py + pc))
    return 4 * x + 2 * y + c, peers


_ANY = pl.BlockSpec(memory_space=pl.ANY)


class Exchange:
    def __init__(self, gathers=(), scatters=()):
        self.gathers, self.scatters = list(gathers), list(scatters)
        self.n = len(self.gathers) + len(self.scatters)

    def operands(self):
        return self.gathers + self.scatters

    def out_shape(self):
        return ([jax.ShapeDtypeStruct((N_DEV,) + x.shape, x.dtype) for x in self.gathers]
                + [jax.ShapeDtypeStruct(x.shape, x.dtype) for x in self.scatters])

    def scratch(self):
        n = max(self.n, 1)
        return [pltpu.SemaphoreType.DMA((7 * n,)), pltpu.SemaphoreType.DMA((7 * n,)), pltpu.SemaphoreType.DMA((n,))]

    def _copies(self, in_refs, out_refs, send_sems, recv_sems, local_sems):
        if self.n == 0:
            return [], [], []
        me, peers = _peers()
        ng = len(self.gathers)
        local, sends, recvs = [], [], []
        for a in range(self.n):
            x, o = in_refs[a], out_refs[a]
            mine = x if a < ng else x.at[me]
            local.append(pltpu.make_async_copy(mine, o.at[me], local_sems.at[a]))
            for k in range(7):
                peer, slot = peers[k]
                sems = dict(send_sem=send_sems.at[7 * a + k], recv_sem=recv_sems.at[7 * a + k], device_id=peer,
                            device_id_type=pl.DeviceIdType.MESH)
                sends.append(pltpu.make_async_remote_copy(src_ref=x if a < ng else x.at[slot], dst_ref=o.at[me], **sems))
                recvs.append(pltpu.make_async_remote_copy(src_ref=mine, dst_ref=o.at[slot], **sems))
        return local, sends, recvs

    def start(self, *refs):
        local, sends, _ = self._copies(*refs)
        for cp in local + sends:
            cp.start()

    def wait(self, *refs):
        local, sends, recvs = self._copies(*refs)
        for cp in recvs:
            cp.wait_recv()
        for cp in sends:
            cp.wait_send()
        for cp in local:
            cp.wait()


def exchange(ex, name):
    n = ex.n

    def body(*refs):
        args = (refs[:n], refs[n:2 * n]) + tuple(refs[2 * n:])
        ex.start(*args)
        ex.wait(*args)

    return pl.pallas_call(body, in_specs=[_ANY] * n, out_specs=[_ANY] * n, out_shape=ex.out_shape(),
                          scratch_shapes=ex.scratch(), name=name)(*ex.operands())


def _heads(z, h):
    return z.reshape(z.shape[0], h, HEAD_DIM).transpose(1, 0, 2)


def _unheads(z):
    return z.transpose(1, 0, 2).reshape(z.shape[1], z.shape[0] * HEAD_DIM)


def _regroup(z, d):
    h, T, w = z.shape
    return z.reshape(h, T // d, d, w).transpose(0, 2, 1, 3).reshape(h * d, T // d, w)


def _ungroup(z, d):
    hd, U, w = z.shape
    return z.reshape(hd // d, d, U, w).transpose(0, 2, 1, 3).reshape(hd // d, U * d, w)


def _shift_down(z):
    return jnp.concatenate([jnp.zeros_like(z[:1]), z[:-1]], axis=0)


def _shift_up(z):
    return jnp.concatenate([z[1:], jnp.zeros_like(z[:1])], axis=0)


def _segments(width):
    seg = np.zeros((width, 128), np.float32)
    seg[np.arange(width), np.arange(width) // HEAD_DIM] = 1.0
    return jnp.asarray(seg), jnp.asarray(seg.T)


def _rope_consts(T):
    inv = ROPE_THETA ** (-jnp.arange(0, HEAD_DIM, 2, dtype=F32) / HEAD_DIM)
    ang = jnp.arange(T, dtype=F32)[:, None] * inv[None, :]
    cos, sin = jnp.cos(ang), jnp.sin(ang)
    rot = np.zeros((HEAD_DIM, HEAD_DIM), np.float32)
    half = HEAD_DIM // 2
    rot[np.arange(half) + half, np.arange(half)] = -1.0
    rot[np.arange(half), np.arange(half) + half] = 1.0
    return jnp.concatenate([cos, cos], axis=1), jnp.concatenate([sin, sin], axis=1), jnp.asarray(rot)


RW_CUTS = (0, 768, 1536, 2304, 2368, 2432, 2560)
RW_TILE = 128
ROW_TILE = 256


def _local_step(x0, memx, tgt, P, ex_weights=None, weights_done=None, ex_grads=None):
    T = x0.shape[0]
    P = dict(P)
    G = {}
    seg, seg_t = _segments(RWKV_WIDTH)
    cos, sin, rot = _rope_consts(T)
    row = lambda v: v.reshape(1, -1)

    def mem_fwd(i, qmem):
        memn = stage_fwd(f_rmsnorm, [memx], [P["mem_norm"][i:i + 1]], [], [], N_MEM, f"mem{i}_norm", [BF16])[0]
        kvm = matmul(memn, P["mem_w_kv"][i], "nn", f"mem{i}_kv")
        kraw, vm = _heads(kvm[:, :MEM_WIDTH], MEM_HEADS), _heads(kvm[:, MEM_WIDTH:], MEM_HEADS)
        km = stage_fwd(f_headnorm, [kraw], [P["mem_k_norm"][i:i + 1]], [], [], N_MEM, f"mem{i}_knorm")[0]
        qh = _heads(qmem, MEM_HEADS)
        om = stage_fwd(f_memattn, [qh], [km, vm, P["mem_q_norm"][i:i + 1]], [], [], ROW_TILE, f"mem{i}_attn")[0]
        return _unheads(om), (memn, kraw, vm, km, qh)

    def mem_bwd(i, saved, dymem):
        memn, kraw, vm, km, qh = saved
        (dqh,), (dkm, dvm, g_qn) = stage_bwd(f_memattn, [qh], [km, vm, P["mem_q_norm"][i:i + 1]], [], [],
                                             [_heads(dymem, MEM_HEADS)], ROW_TILE, f"mem{i}_attn_bwd")
        (dkraw,), (g_kn,) = stage_bwd(f_headnorm, [kraw], [P["mem_k_norm"][i:i + 1]], [], [], [dkm], N_MEM, f"mem{i}_knorm_bwd")
        dkvm = jnp.concatenate([_unheads(dkraw), _unheads(dvm)], axis=1).astype(BF16)
        g_w = matmul(memn, dkvm, "tn", f"mem{i}_kv_dw")
        dmemn = matmul(dkvm, P["mem_w_kv"][i], "nt", f"mem{i}_kv_dx")
        _, (g_mn,) = stage_bwd(f_rmsnorm, [memx], [P["mem_norm"][i:i + 1]], [], [], [dmemn], N_MEM, f"mem{i}_norm_bwd")
        return _unheads(dqh), g_mn, g_w, g_qn, g_kn

    def ffn_fwd(i, xin):
        hn = stage_fwd(f_rmsnorm, [xin], [P["ffn_norm"][i:i + 1]], [], [], ROW_TILE, f"ffn{i}_norm", [BF16])[0]
        u = matmul(hn, P["ffn_w_up"][i], "nn", f"ffn{i}_up")
        z = convgate_fwd(u, P["ffn_conv_w"][i], P["ffn_conv_b"][i:i + 1], f"ffn{i}_conv")
        return matmul(z, P["ffn_w_down"][i], "nn", f"ffn{i}_down", residual=xin), (hn, u, z)

    def ffn_bwd(i, xin, saved, dxo, dxo_b):
        hn, u, z = saved
        dz = matmul(dxo_b, P["ffn_w_down"][i], "nt", f"ffn{i}_down_dx")
        g_down = matmul(z, dxo_b, "tn", f"ffn{i}_down_dw")
        dc, g_cw, g_cb = convgate_bwd_c(u, P["ffn_conv_w"][i], P["ffn_conv_b"][i:i + 1], dz, f"ffn{i}_conv_bwd_c")
        du = convgate_bwd_u(dc, P["ffn_conv_w"][i], f"ffn{i}_conv_bwd_u")
        dhn = matmul(du, P["ffn_w_up"][i], "nt", f"ffn{i}_up_dx")
        g_up = matmul(hn, du, "tn", f"ffn{i}_up_dw")
        (dxin,), (g_n,), (dxin_b,) = stage_bwd(f_rmsnorm_res, [xin], [P["ffn_norm"][i:i + 1]], [], [], [dhn, dxo], ROW_TILE,
                                               f"ffn{i}_norm_bwd", bf16_copies=(0,))
        return dxin, dxin_b, g_n, g_up, g_cw, g_cb, g_down

    h0 = stage_fwd(f_rmsnorm, [x0], [P["attn_norm"][0:1]], [], [], ROW_TILE, "l0_norm", [BF16])[0]
    p0 = matmul(h0, P["a_w_in"][0], "nn", "l0_in")
    cur = [p0[:, a:b] for a, b in zip(RW_CUTS[:-1], RW_CUTS[1:])]
    prev = [_shift_down(c) for c in cur]
    mu = [P["a_mu"][:, a:b] for a, b in zip(RW_CUTS[:-1], RW_CUTS[1:])]
    pre_ps = mu + [P["a_w0"], P["a_w2"][0], P["a_a0"], P["a_a2"][0], P["a_g2"][0], P["a_k_k"], P["a_k_a"]]
    r, lw, k2, v, kk, b, g = stage_fwd(f_rwkv_pre, cur + prev, pre_ps, [], [seg, seg_t], RW_TILE, "l0_rwkv_pre")
    hm = lambda z: _heads(z, RWKV_HEADS)
    scan_in = [hm(z) for z in (r, lw, k2, v, kk, b)]
    y_h, h_states, got = rwkv_scan_fwd(*scan_in, ex_weights or Exchange())
    if weights_done is not None:
        P.update(weights_done(got))
    y_s = _unheads(y_h)
    post_ps = [P["a_lnx_w"], P["a_lnx_b"], P["a_r_k"].reshape(1, RWKV_WIDTH)]
    ymix0 = stage_fwd(f_rwkv_post, [y_s, r, k2, v, g], post_ps, [], [seg, seg_t], RW_TILE, "l0_rwkv_post")[0]
    ymem0, mem0_saved = mem_fwd(0, p0[:, SHIFT_WIDTH:])
    ycat0 = jnp.concatenate([ymix0, ymem0], axis=1).astype(BF16)
    x1 = matmul(ycat0, P["a_w_out"][0], "nn", "l0_out", residual=x0)
    x2, ffn0_saved = ffn_fwd(0, x1)

    hk, h1 = stage_fwd(f_rmsnorm2, [x2], [row(P["kv_norm"]), P["attn_norm"][1:2]], [], [], ROW_TILE, "l1_norm", [BF16, BF16])
    kvp = matmul(hk, P["kv_w"][0], "nn", "l1_kv")
    p1 = matmul(h1, P["b_w_in"][0], "nn", "l1_in")
    kraw, vsh, qraw = hm(kvp[:, :DIL_WIDTH]), hm(kvp[:, DIL_WIDTH:]), hm(p1[:, :DIL_WIDTH])
    ksh = stage_fwd(f_qkprep, [kraw], [row(P["kv_k_norm"])], [cos, sin], [rot], ROW_TILE, "l1_kprep")[0]
    q = stage_fwd(f_qkprep, [qraw], [P["b_q_norm"]], [cos, sin], [rot], ROW_TILE, "l1_qprep")[0]
    grp, outs, lses = [], [], []
    for gi, (_, d) in enumerate(DIL_GROUPS):
        hs = slice(4 * gi, 4 * gi + 4)
        qg, kg, vg = _regroup(q[hs], d), _regroup(ksh[hs], d), _regroup(vsh[hs], d)
        og, lg = dil_fwd(qg, kg, vg, f"l1_dil{gi}")
        grp.append((qg, kg, vg))
        outs.append(_ungroup(og, d))
        lses.append(_ungroup(lg, d))
    omix = stage_fwd(f_mix, outs + lses, [], [], [], ROW_TILE, "l1_mix")[0]
    ymem1, mem1_saved = mem_fwd(1, p1[:, DIL_WIDTH:])
    ycat1 = jnp.concatenate([_unheads(omix), ymem1], axis=1).astype(BF16)
    x3 = matmul(ycat1, P["b_w_out"][0], "nn", "l1_out", residual=x2)
    x4, ffn1_saved = ffn_fwd(1, x3)
    loss_part, dx4, dx4_b = loss_head(x4, tgt)

    dx3, dx3_b, gn1, gup1, gcw1, gcb1, gdown1 = ffn_bwd(1, x3, ffn1_saved, dx4, dx4_b)
    dycat1 = matmul(dx3_b, P["b_w_out"][0], "nt", "l1_out_dx")
    G["b_w_out"] = [matmul(ycat1, dx3_b, "tn", "l1_out_dw")]
    dqmem1, gmn1, gmw1, gmq1, gmk1 = mem_bwd(1, mem1_saved, dycat1[:, MEM_WIDTH:])
    dmix, _ = stage_bwd(f_mix, outs + lses, [], [], [], [_heads(dycat1[:, :MEM_WIDTH], 4)], ROW_TILE, "l1_mix_bwd")
    dq, dk, dv = [], [], []
    for gi, (_, d) in enumerate(DIL_GROUPS):
        qg, kg, vg = grp[gi]
        dqg, dkc, dkp, dvc, dvp = dil_bwd(qg, kg, vg, _regroup(dmix[gi], d), _regroup(dmix[3 + gi], d), f"l1_dil{gi}_bwd")
        back = lambda c, p: c + jnp.concatenate([p[:, DIL_BLOCK:], jnp.zeros_like(p[:, :DIL_BLOCK])], axis=1)
        dq.append(_ungroup(dqg, d))
        dk.append(_ungroup(back(dkc, dkp), d))
        dv.append(_ungroup(back(dvc, dvp), d))
    dq, dk, dv = jnp.concatenate(dq, 0), jnp.concatenate(dk, 0), jnp.concatenate(dv, 0)
    (dqraw,), (g_bq,) = stage_bwd(f_qkprep, [qraw], [P["b_q_norm"]], [cos, sin], [rot], [dq], ROW_TILE, "l1_qprep_bwd")
    (dkraw,), (g_kk,) = stage_bwd(f_qkprep, [kraw], [row(P["kv_k_norm"])], [cos, sin], [rot], [dk], ROW_TILE, "l1_kprep_bwd")
    dp1 = jnp.concatenate([_unheads(dqraw), dqmem1], axis=1).astype(BF16)
    dkvp = jnp.concatenate([_unheads(dkraw), _unheads(dv)], axis=1).astype(BF16)
    dh1 = matmul(dp1, P["b_w_in"][0], "nt", "l1_in_dx")
    G["b_w_in"] = [matmul(h1, dp1, "tn", "l1_in_dw")]
    dhk = matmul(dkvp, P["kv_w"][0], "nt", "l1_kv_dx")
    G["kv_w"] = [matmul(hk, dkvp, "tn", "l1_kv_dw")]
    (dx2,), (g_kvn, g_an1), (dx2_b,) = stage_bwd(f_rmsnorm2_res, [x2], [row(P["kv_norm"]), P["attn_norm"][1:2]], [], [],
                                                 [dhk, dh1, dx3], ROW_TILE, "l1_norm_bwd", bf16_copies=(0,))

    dx1, dx1_b, gn0, gup0, gcw0, gcb0, gdown0 = ffn_bwd(0, x1, ffn0_saved, dx2, dx2_b)
    dycat0 = matmul(dx1_b, P["a_w_out"][0], "nt", "l0_out_dx")
    G["a_w_out"] = [matmul(ycat0, dx1_b, "tn", "l0_out_dw")]
    dqmem0, gmn0, gmw0, gmq0, gmk0 = mem_bwd(0, mem0_saved, dycat0[:, RWKV_WIDTH:])
    (dy_s, dr_a, dk_a, dv_a, dg), (g_lw, g_lb, g_rk) = stage_bwd(
        f_rwkv_post, [y_s, r, k2, v, g], post_ps, [], [seg, seg_t], [dycat0[:, :RWKV_WIDTH]], RW_TILE, "l0_rwkv_post_bwd")
    G["mem_w_kv"], G["ffn_w_up"], G["ffn_w_down"] = [gmw0, gmw1], [gup0, gup1], [gdown0, gdown1]
    d_scan, G["_exchanged"] = rwkv_scan_bwd(*scan_in, h_states, hm(dy_s), ex_grads(G) if ex_grads else Exchange())
    dr_b, dlw, dk_b, dv_b, dkk, db = [_unheads(z) for z in d_scan]
    dpre, gpre = stage_bwd(f_rwkv_pre, cur + prev, pre_ps, [], [seg, seg_t],
                           [[dr_a, dr_b], dlw, [dk_a, dk_b], [dv_a, dv_b], dkk, db, dg], RW_TILE, "l0_rwkv_pre_bwd")
    dp_rw = jnp.concatenate(dpre[:6], axis=1) + _shift_up(jnp.concatenate(dpre[6:], axis=1))
    dp0 = jnp.concatenate([dp_rw, dqmem0], axis=1).astype(BF16)
    dh0 = matmul(dp0, P["a_w_in"][0], "nt", "l0_in_dx")
    G["a_w_in"] = [matmul(h0, dp0, "tn", "l0_in_dw")]
    (dx0,), (g_an0,) = stage_bwd(f_rmsnorm_res, [x0], [P["attn_norm"][0:1]], [], [], [dh0, dx1], ROW_TILE, "l0_norm_bwd")

    G["attn_norm"] = jnp.concatenate([g_an0, g_an1], axis=0)
    G["a_mu"] = jnp.concatenate(gpre[:6], axis=1)
    G["a_w0"], G["a_w2"], G["a_a0"], G["a_a2"], G["a_g2"] = gpre[6], gpre[7][None], gpre[8], gpre[9][None], gpre[10][None]
    G["a_k_k"], G["a_k_a"] = gpre[11], gpre[12]
    G["a_r_k"] = g_rk.reshape(1, RWKV_HEADS, HEAD_DIM)
    G["a_lnx_w"], G["a_lnx_b"] = g_lw, g_lb
    G["kv_norm"], G["kv_k_norm"], G["b_q_norm"] = g_kvn.reshape(-1), g_kk.reshape(-1), g_bq
    G["mem_norm"] = jnp.concatenate([gmn0, gmn1], axis=0)
    G["mem_w_kv"] = [gmw0, gmw1]
    G["mem_q_norm"] = jnp.concatenate([gmq0, gmq1], axis=0)
    G["mem_k_norm"] = jnp.concatenate([gmk0, gmk1], axis=0)
    G["ffn_norm"] = jnp.concatenate([gn0, gn1], axis=0)
    G["ffn_w_up"] = [gup0, gup1]
    G["ffn_conv_w"] = jnp.stack([gcw0, gcw1])
    G["ffn_conv_b"] = jnp.concatenate([gcb0, gcb1], axis=0)
    G["ffn_w_down"] = [gdown0, gdown1]
    return loss_part, dx0, G


PARAMS = (("attn_norm", None), ("a_w_in", 2), ("a_mu", 1), ("a_w0", 1), ("a_w2", 2), ("a_a0", 1), ("a_a2", 2), ("a_g2", 2),
          ("a_k_k", 1), ("a_k_a", 1), ("a_r_k", None), ("a_lnx_w", 1), ("a_lnx_b", 1), ("a_w_out", 1), ("kv_norm", None),
          ("kv_w", 1), ("kv_k_norm", None), ("b_w_in", 1), ("b_q_norm", None), ("b_w_out", 2), ("mem_norm", None),
          ("mem_w_kv", 1), ("mem_q_norm", None), ("mem_k_norm", None), ("ffn_norm", None), ("ffn_w_up", 2),
          ("ffn_conv_w", 2), ("ffn_conv_b", None), ("ffn_w_down", 1))
BIG = ("a_w_in", "a_w_out", "kv_w", "b_w_in", "b_w_out", "mem_w_kv", "ffn_w_up", "ffn_w_down")
AXIS = dict(PARAMS)
SMALL = tuple(n for n, _ in PARAMS if n not in BIG)
SMALL_SHARDED = tuple(n for n in SMALL if AXIS[n] is not None)
PACK_QUANTUM = 256 * 128


def _from_shards(xs, axis):
    full = jnp.moveaxis(xs, 0, axis)
    sh = full.shape
    return full.reshape(sh[:axis] + (sh[axis] * sh[axis + 1],) + sh[axis + 2:])


def _to_shards(g, axis):
    sh = g.shape
    return jnp.moveaxis(g.reshape(sh[:axis] + (N_DEV, sh[axis] // N_DEV) + sh[axis + 1:]), axis, 0)


def _pack(parts, lead=0):
    ld = parts[0].shape[:lead]
    flat = jnp.concatenate([p.reshape(ld + (-1,)) for p in parts], axis=-1)
    pad = (-flat.shape[-1]) % PACK_QUANTUM
    flat = jnp.pad(flat, [(0, 0)] * lead + [(0, pad)])
    return flat.reshape(ld + (-1, 128))


def _unpack(packed, shapes, lead=0):
    ld = packed.shape[:lead]
    flat = packed.reshape(ld + (-1,))
    out, off = [], 0
    for s in shapes:
        n = math.prod(s)
        out.append(flat[..., off:off + n].reshape(ld + tuple(s)))
        off += n
    return out


def kernel(x, mem, attn_norm, a_w_in, a_mu, a_w0, a_w2, a_a0, a_a2, a_g2, a_k_k, a_k_a, a_r_k, a_lnx_w, a_lnx_b, a_w_out, kv_norm, kv_w, kv_k_norm, b_w_in, b_q_norm, b_w_out, mem_norm, mem_w_kv, mem_q_norm, mem_k_norm, ffn_norm, ffn_w_up, ffn_conv_w, ffn_conv_b, ffn_w_down, loss_target, m_attn_norm, m_a_w_in, m_a_mu, m_a_w0, m_a_w2, m_a_a0, m_a_a2, m_a_g2, m_a_k_k, m_a_k_a, m_a_r_k, m_a_lnx_w, m_a_lnx_b, m_a_w_out, m_kv_norm, m_kv_w, m_kv_k_norm, m_b_w_in, m_b_q_norm, m_b_w_out, m_mem_norm, m_mem_w_kv, m_mem_q_norm, m_mem_k_norm, m_ffn_norm, m_ffn_w_up, m_ffn_conv_w, m_ffn_conv_b, m_ffn_w_down, v_attn_norm, v_a_w_in, v_a_mu, v_a_w0, v_a_w2, v_a_a0, v_a_a2, v_a_g2, v_a_k_k, v_a_k_a, v_a_r_k, v_a_lnx_w, v_a_lnx_b, v_a_w_out, v_kv_norm, v_kv_w, v_kv_k_norm, v_b_w_in, v_b_q_norm, v_b_w_out, v_mem_norm, v_mem_w_kv, v_mem_q_norm, v_mem_k_norm, v_ffn_norm, v_ffn_w_up, v_ffn_conv_w, v_ffn_conv_b, v_ffn_w_down):
    names = [n for n, _ in PARAMS]
    vals = (attn_norm, a_w_in, a_mu, a_w0, a_w2, a_a0, a_a2, a_g2, a_k_k, a_k_a, a_r_k, a_lnx_w, a_lnx_b, a_w_out, kv_norm, kv_w, kv_k_norm, b_w_in, b_q_norm, b_w_out, mem_norm, mem_w_kv, mem_q_norm, mem_k_norm, ffn_norm, ffn_w_up, ffn_conv_w, ffn_conv_b, ffn_w_down)
    m_vals = (m_attn_norm, m_a_w_in, m_a_mu, m_a_w0, m_a_w2, m_a_a0, m_a_a2, m_a_g2, m_a_k_k, m_a_k_a, m_a_r_k, m_a_lnx_w, m_a_lnx_b, m_a_w_out, m_kv_norm, m_kv_w, m_kv_k_norm, m_b_w_in, m_b_q_norm, m_b_w_out, m_mem_norm, m_mem_w_kv, m_mem_q_norm, m_mem_k_norm, m_ffn_norm, m_ffn_w_up, m_ffn_conv_w, m_ffn_conv_b, m_ffn_w_down)
    v_vals = (v_attn_norm, v_a_w_in, v_a_mu, v_a_w0, v_a_w2, v_a_a0, v_a_a2, v_a_g2, v_a_k_k, v_a_k_a, v_a_r_k, v_a_lnx_w, v_a_lnx_b, v_a_w_out, v_kv_norm, v_kv_w, v_kv_k_norm, v_b_w_in, v_b_q_norm, v_b_w_out, v_mem_norm, v_mem_w_kv, v_mem_q_norm, v_mem_k_norm, v_ffn_norm, v_ffn_w_up, v_ffn_conv_w, v_ffn_conv_b, v_ffn_w_down)
    W, M, V = dict(zip(names, vals)), dict(zip(names, m_vals)), dict(zip(names, v_vals))
    me = 4 * lax.axis_index("x") + 2 * lax.axis_index("y") + lax.axis_index("c")
    layers = lambda D, n: [D[n]] if D[n].ndim == 2 else [D[n][i] for i in range(D[n].shape[0])]
    ax2 = lambda n: AXIS[n] - (W[n].ndim - 2)
    first = [("a_w_in", 0)]
    later = [(n, i) for n in BIG if n != "a_w_in" for i in range(len(layers(W, n)))]

    small_shapes = [W[n].shape for n in SMALL_SHARDED]
    got_w, got_small = exchange(Exchange(gathers=[W["a_w_in"][0].astype(BF16), _pack([W[n] for n in SMALL_SHARDED])]),
                                "gather_first")
    P = {n: W[n] for n in SMALL}
    P["a_w_in"] = [_from_shards(got_w, ax2("a_w_in"))]
    for n, s in zip(SMALL_SHARDED, _unpack(got_small, small_shapes, lead=1)):
        P[n] = _from_shards(s, AXIS[n])
    ex_weights = Exchange(gathers=[layers(W, n)[i].astype(BF16) for n, i in later])

    def weights_done(got):
        out = {}
        for (n, _), g in zip(later, got):
            out.setdefault(n, []).append(_from_shards(g, ax2(n)))
        return out

    ex_grads = lambda G: Exchange(scatters=[_to_shards(G[n][i], ax2(n)) for n, i in later])
    loss_part, dx0, G = _local_step(x[0], mem[0], loss_target[0], P, ex_weights, weights_done, ex_grads)
    loss = lax.psum(loss_part[0, 0], ("x", "y", "c"))
    gparts = dict(zip(later, G.pop("_exchanged")))
    got_gsmall, gparts[first[0]] = exchange(
        Exchange(gathers=[_pack([G[n] for n in SMALL])], scatters=[_to_shards(G["a_w_in"][0], ax2("a_w_in"))]), "exchange_last")

    results = {}
    for n in BIG:
        per_layer = [adamw(gparts[(n, i)], w, m, v, f"adamw_{n}{i}")
                     for i, (w, m, v) in enumerate(zip(layers(W, n), layers(M, n), layers(V, n)))]
        results[n] = [r[0] if W[n].ndim == 2 else jnp.stack(r) for r in zip(*per_layer)]
    g_small = sum_parts(got_gsmall, "sum_small_grads")
    mine = []
    for n, g in zip(SMALL, _unpack(g_small, [G[n].shape for n in SMALL])):
        if AXIS[n] is not None:
            s = W[n].shape[AXIS[n]]
            g = lax.dynamic_slice_in_dim(g, me * s, s, axis=AXIS[n])
        mine.append(g)
    res = adamw(_pack(mine)[None], _pack([W[n] for n in SMALL]), _pack([M[n] for n in SMALL]), _pack([V[n] for n in SMALL]),
                "adamw_small")
    for n, parts in zip(SMALL, zip(*[_unpack(r, [W[n].shape for n in SMALL]) for r in res])):
        results[n] = list(parts)
    outs = [[results[n][j] for n in names] for j in range(4)]
    return (loss, dx0[None], *outs[0], *outs[1], *outs[2], *outs[3])
```

```python
import functools
import math

import jax
import jax.numpy as jnp
import numpy as np
from jax import lax
from jax.experimental import pallas as pl
from jax.experimental.pallas import tpu as pltpu

F32 = jnp.float32
BF16 = jnp.bfloat16
HI = lax.Precision.HIGHEST
H3 = lax.Precision.HIGH

N_DEV = 8
D_MODEL = 1024
HEAD_DIM = 64
N_MEM = 256
MEM_HEADS = 4
MEM_WIDTH = 256
RWKV_HEADS = 12
RWKV_WIDTH = 768
SHIFT_WIDTH = 2560
DIL_GROUPS = ((128, 1), (512, 4), (2048, 16))
DIL_BLOCK = 128
DIL_WIDTH = 768
D_FF = 2816
RMS_EPS = 1e-6
LNX_EPS = 64e-5
NEG_INF = -1e30
ROPE_THETA = 10000.0
ADAM_LR, ADAM_B1, ADAM_B2, ADAM_EPS, ADAM_WD, ADAM_STEP = 0.001, 0.9, 0.999, 1e-08, 0.01, 10

CHUNK = 64
SCAN_GROUPS_FWD, SCAN_GROUPS_BWD = 1, 1
MM_TILE_CAP = 1408
VMEM_LIMIT_V7X = 48 * 1024 * 1024


def _cparams(sem):
    return pltpu.CompilerParams(dimension_semantics=sem, vmem_limit_bytes=VMEM_LIMIT_V7X)


def _pick(n, cands):
    for c in cands:
        if n % c == 0:
            return c
    return n


def _tile(n, cap):
    if n <= cap:
        return n
    for d in range(cap - cap % 128, 0, -128):
        if n % d == 0:
            return d
    return n


def _dg(a, b, ca, cb, batch):
    dims = (((ca,), (cb,)), ((0,), (0,))) if batch else (((ca,), (cb,)), ((), ()))
    return lax.dot_general(a.astype(BF16), b.astype(BF16), dims, preferred_element_type=F32)


@jax.custom_vjp
def mm_nn(a, b):
    n = a.ndim
    return _dg(a, b, n - 1, n - 2, n == 3)


def _mm_nn_fwd(a, b):
    return mm_nn(a, b), (a, b)


def _mm_nn_bwd(res, g):
    a, b = res
    n = a.ndim
    return _dg(g, b, n - 1, n - 1, n == 3), _dg(a, g, n - 2, n - 2, n == 3)


mm_nn.defvjp(_mm_nn_fwd, _mm_nn_bwd)


@jax.custom_vjp
def mm_nt(a, b):
    n = a.ndim
    return _dg(a, b, n - 1, n - 1, n == 3)


def _mm_nt_fwd(a, b):
    return mm_nt(a, b), (a, b)


def _mm_nt_bwd(res, g):
    a, b = res
    n = a.ndim
    return _dg(g, b, n - 1, n - 2, n == 3), _dg(g, a, n - 2, n - 2, n == 3)


mm_nt.defvjp(_mm_nt_fwd, _mm_nt_bwd)


def mmh(a, b, precision=H3):
    n = a.ndim
    dims = (((n - 1,), (n - 2,)), ((0,), (0,))) if n == 3 else (((1,), (0,)), ((), ()))
    return lax.dot_general(a, b, dims, precision=precision, preferred_element_type=F32)


def mmh_nt(a, b):
    n = a.ndim
    dims = (((n - 1,), (n - 1,)), ((0,), (0,))) if n == 3 else (((1,), (1,)), ((), ()))
    return lax.dot_general(a, b, dims, precision=H3, preferred_element_type=F32)


def mmh_tn(a, b):
    n = a.ndim
    dims = (((n - 2,), (n - 2,)), ((0,), (0,))) if n == 3 else (((0,), (0,)), ((), ()))
    return lax.dot_general(a, b, dims, precision=H3, preferred_element_type=F32)


def matmul(a, b, mode, name, residual=None):
    out_dtype = BF16 if mode == "tn" else F32
    if mode == "nn":
        (M, K), (_, N) = a.shape, b.shape
    elif mode == "nt":
        (M, K), (N, _) = a.shape, b.shape
    else:
        (K, M), (_, N) = a.shape, b.shape
    tm = _tile(M, 2048 if mode == "nn" else MM_TILE_CAP)
    tn = _tile(N, 512 if mode == "nn" else MM_TILE_CAP)
    tk = _tile(K, MM_TILE_CAP if mode != "nt" else 512)
    nk = K // tk
    if mode == "nn":
        a_spec = pl.BlockSpec((tm, tk), lambda i, j, k: (i, k))
        b_spec = pl.BlockSpec((tk, tn), lambda i, j, k: (k, j))
        dims = (((1,), (0,)), ((), ()))
    elif mode == "nt":
        a_spec = pl.BlockSpec((tm, tk), lambda i, j, k: (i, k))
        b_spec = pl.BlockSpec((tn, tk), lambda i, j, k: (j, k))
        dims = (((1,), (1,)), ((), ()))
    else:
        a_spec = pl.BlockSpec((tk, tm), lambda i, j, k: (k, i))
        b_spec = pl.BlockSpec((tk, tn), lambda i, j, k: (k, j))
        dims = (((0,), (0,)), ((), ()))
    o_spec = pl.BlockSpec((tm, tn), lambda i, j, k: (i, j))
    has_res = residual is not None

    def body(*refs):
        if has_res:
            a_ref, b_ref, r_ref, o_ref, acc_ref = refs
        else:
            a_ref, b_ref, o_ref, acc_ref = refs
        k = pl.program_id(2)

        @pl.when(k == 0)
        def _():
            acc_ref[...] = jnp.zeros_like(acc_ref)

        acc_ref[...] += lax.dot_general(a_ref[...].astype(BF16), b_ref[...].astype(BF16), dims,
                                        preferred_element_type=F32)

        @pl.when(k == nk - 1)
        def _():
            if has_res:
                o_ref[...] = (acc_ref[...] + r_ref[...]).astype(out_dtype)
            else:
                o_ref[...] = acc_ref[...].astype(out_dtype)

    ins = [a, b] + ([residual] if has_res else [])
    in_specs = [a_spec, b_spec] + ([o_spec] if has_res else [])
    return pl.pallas_call(
        body, grid=(M // tm, N // tn, nk), in_specs=in_specs, out_specs=o_spec,
        out_shape=jax.ShapeDtypeStruct((M, N), out_dtype), scratch_shapes=[pltpu.VMEM((tm, tn), F32)],
        compiler_params=_cparams(("parallel", "parallel", "arbitrary")), name=name)(*ins)


def _tok_spec(shape, tile):
    nd = len(shape)
    return pl.BlockSpec(shape[:-2] + (tile, shape[-1]), lambda i: (0,) * (nd - 2) + (i, 0))


def _full_spec(shape):
    nd = len(shape)
    return pl.BlockSpec(shape, lambda i: (0,) * nd)


def _blk(x, tile):
    return jax.ShapeDtypeStruct(x.shape[:-2] + (tile, x.shape[-1]), x.dtype)


def stage_fwd(f, xs, ps, cts, cfs, tile, name, out_dtypes=None):
    tiled, full = list(xs) + list(cts), list(ps) + list(cfs)
    nx, np_, nct = len(xs), len(ps), len(cts)
    T = tiled[0].shape[-2]
    order = lambda t, fl: list(t[:nx]) + list(fl[:np_]) + list(t[nx:]) + list(fl[np_:])
    out_avals = jax.eval_shape(f, *order([_blk(x, tile) for x in tiled], full))
    if out_dtypes is None:
        out_dtypes = [o.dtype for o in out_avals]
    out_shape = [jax.ShapeDtypeStruct(o.shape[:-2] + (T, o.shape[-1]), dt) for o, dt in zip(out_avals, out_dtypes)]
    nt, nf = len(tiled), len(full)

    def body(*refs):
        tv = [r[...] for r in refs[:nt]]
        fv = [r[...] for r in refs[nt:nt + nf]]
        res = f(*order(tv, fv))
        for o_ref, r in zip(refs[nt + nf:], res):
            o_ref[...] = r.astype(o_ref.dtype)

    return pl.pallas_call(
        body, grid=(T // tile,),
        in_specs=[_tok_spec(x.shape, tile) for x in tiled] + [_full_spec(p.shape) for p in full],
        out_specs=[_tok_spec(o.shape, tile) for o in out_shape], out_shape=out_shape,
        compiler_params=_cparams(("parallel",)), name=name)(*tiled, *full)


def stage_bwd(f, xs, ps, cts, cfs, gs, tile, name, bf16_copies=()):
    gs = [list(g) if isinstance(g, (list, tuple)) else [g] for g in gs]
    g_flat = [a for g in gs for a in g]
    tiled, full = list(xs) + list(cts) + g_flat, list(ps) + list(cfs)
    nx, np_, nct = len(xs), len(ps), len(cts)
    T = tiled[0].shape[-2]
    nt, nf = len(tiled), len(full)
    out_shape = ([jax.ShapeDtypeStruct(x.shape, F32) for x in xs] + [jax.ShapeDtypeStruct(p.shape, F32) for p in ps]
                 + [jax.ShapeDtypeStruct(xs[i].shape, BF16) for i in bf16_copies])

    def body(*refs):
        tv = [r[...] for r in refs[:nt]]
        fv = [r[...] for r in refs[nt:nt + nf]]
        outs = refs[nt + nf:]
        xv, ctv, gparts = tv[:nx], tv[nx:nx + nct], tv[nx + nct:]
        gv = []
        for g in gs:
            gv.append(functools.reduce(lambda a, b: a + b, gparts[:len(g)]))
            gparts = gparts[len(g):]
        pv, cfv = fv[:np_], fv[np_:]
        _, vjp = jax.vjp(lambda *xp: f(*xp, *ctv, *cfv), *xv, *pv)
        d = vjp(tuple(gv))
        for o_ref, r in zip(outs[:nx], d[:nx]):
            o_ref[...] = r
        for o_ref, i in zip(outs[nx + np_:], bf16_copies):
            o_ref[...] = d[i].astype(BF16)

        @pl.when(pl.program_id(0) == 0)
        def _():
            for o_ref in outs[nx:nx + np_]:
                o_ref[...] = jnp.zeros_like(o_ref)

        for o_ref, r in zip(outs[nx:nx + np_], d[nx:]):
            o_ref[...] += r

    res = pl.pallas_call(
        body, grid=(T // tile,),
        in_specs=[_tok_spec(x.shape, tile) for x in tiled] + [_full_spec(p.shape) for p in full],
        out_specs=([_tok_spec(x.shape, tile) for x in xs] + [_full_spec(p.shape) for p in ps]
                   + [_tok_spec(xs[i].shape, tile) for i in bf16_copies]), out_shape=out_shape,
        compiler_params=_cparams(("arbitrary",)), name=name)(*tiled, *full)
    if bf16_copies:
        return list(res[:nx]), list(res[nx:nx + np_]), list(res[nx + np_:])
    return list(res[:nx]), list(res[nx:])


def _rms(x, g, eps=RMS_EPS):
    return x * lax.rsqrt(jnp.mean(x * x, axis=-1, keepdims=True) + eps) * g


def f_rmsnorm(x, g):
    return (_rms(x, g),)


def f_rmsnorm_res(x, g):
    return _rms(x, g), x


def f_rmsnorm2(x, g1, g2):
    n = x * lax.rsqrt(jnp.mean(x * x, axis=-1, keepdims=True) + RMS_EPS)
    return n * g1, n * g2


def f_rmsnorm2_res(x, g1, g2):
    return f_rmsnorm2(x, g1, g2) + (x,)


def _sigmoid(x):
    return 1.0 / (1.0 + jnp.exp(-x))


def _softplus(x):
    return jnp.maximum(x, 0.0) + jnp.log(1.0 + jnp.exp(-jnp.abs(x)))


def f_rwkv_pre(pr, pk, pv, pwd, pad, pgd, qr, qk, qv, qwd, qad, qgd,
               mu_r, mu_k, mu_v, mu_wd, mu_ad, mu_gd, w0, w2, a0, a2, g2, k_k, k_a, seg, seg_t):
    xr = pr + (qr - pr) * mu_r
    xk = pk + (qk - pk) * mu_k
    xv = pv + (qv - pv) * mu_v
    xwd = pwd + (qwd - pwd) * mu_wd
    xad = pad + (qad - pad) * mu_ad
    xgd = pgd + (qgd - pgd) * mu_gd
    w_log = -_softplus(-(w0 + mm_nn(jnp.tanh(xwd), w2))) - 0.5
    lw = -jnp.exp(w_log)
    a = _sigmoid(a0 + mm_nn(xad, a2))
    g = mm_nn(_sigmoid(xgd), g2)
    kkr = xk * k_k
    inv = lax.rsqrt(jnp.maximum(mmh(kkr * kkr, seg), 1e-24))
    kk = kkr * mmh(inv, seg_t)
    k2 = xk * (1.0 + (a - 1.0) * k_a)
    return xr, lw, k2, xv, kk, kk * a, g


def f_rwkv_post(y, r, k2, v, g, lnx_w, lnx_b, r_k, seg, seg_t):
    inv_n = 1.0 / HEAD_DIM
    m = mmh(mmh(y, seg) * inv_n, seg_t)
    yc = y - m
    rstd = lax.rsqrt(mmh(yc * yc, seg) * inv_n + LNX_EPS)
    yn = yc * mmh(rstd, seg_t) * lnx_w + lnx_b
    bonus = mmh(mmh(r * k2 * r_k, seg), seg_t) * v
    return ((yn + bonus) * g,)


def f_headnorm(z, g):
    return (_rms(z, g),)


def f_qkprep(z, g, cos, sin, rot):
    h, t, d = z.shape
    zn = _rms(z, g)
    zr = mmh(zn.reshape(h * t, d), rot).reshape(h, t, d)
    return (zn * cos + zr * sin,)


def f_memattn(q, k, v, q_norm):
    qh = _rms(q, q_norm)
    s = mm_nt(qh, k) * (1.0 / math.sqrt(HEAD_DIM))
    s = s - jnp.max(s, axis=-1, keepdims=True)
    p = jnp.exp(s)
    p = p / jnp.sum(p, axis=-1, keepdims=True)
    return (mm_nn(p, v),)


def f_mix(o1, o2, o3, l1, l2, l3):
    mx = jnp.maximum(jnp.maximum(l1, l2), l3)
    e1, e2, e3 = jnp.exp(l1 - mx), jnp.exp(l2 - mx), jnp.exp(l3 - mx)
    return ((e1 * o1 + e2 * o2 + e3 * o3) / (e1 + e2 + e3),)


def _chunk_masks(L):
    t = lax.broadcasted_iota(jnp.int32, (L, L), 0)
    s = lax.broadcasted_iota(jnp.int32, (L, L), 1)
    return t, s


def _unit_lower_inverse(a):
    L = a.shape[-1]
    t, s = _chunk_masks(L)
    one = jnp.ones((), F32)
    blk = lambda sh: jnp.where((t >> sh) == (s >> sh), one, 0.0)
    n0 = a * blk(3)
    x = jnp.where(t == s, one, 0.0) - n0
    n2 = mmh(n0, n0)
    x = x + mmh(x, n2)
    x = x + mmh(x, mmh(n2, n2))
    for sh in (3, 4, 5):
        if (1 << sh) >= L:
            break
        off = a * (blk(sh + 1) - blk(sh))
        x = x - mmh(x, mmh(off, x))
    return x


@jax.custom_vjp
def _inverse_known(a, x):
    return x


def _inverse_known_fwd(a, x):
    return x, x


def _inverse_known_bwd(x, dx):
    return -mmh_nt(mmh_tn(x, dx), x), jnp.zeros_like(x)


_inverse_known.defvjp(_inverse_known_fwd, _inverse_known_bwd)


def _running_sum(x, reverse):
    L = x.shape[1]
    pos = lax.broadcasted_iota(jnp.int32, (1, L, 1), 1)
    step = 1
    while step < L:
        if reverse:
            x = x + jnp.where(pos < L - step, pltpu.roll(x, L - step, 1), 0.0)
        else:
            x = x + jnp.where(pos >= step, pltpu.roll(x, step, 1), 0.0)
        step *= 2
    return x


@jax.custom_vjp
def _cumsum_tokens(x):
    return _running_sum(x, False)


_cumsum_tokens.defvjp(lambda x: (_running_sum(x, False), None), lambda _, g: (_running_sum(g, True),))


def f_rwkv_chunk(s0, r, lw, k, v, kk, b, x_known=None):
    H, L, _ = r.shape
    t, s = _chunk_masks(L)
    one = jnp.ones((), F32)
    incl = jnp.where(t >= s, one, 0.0)
    strict = jnp.where(t > s, one, 0.0)
    cum = _cumsum_tokens(lw)
    w_in = jnp.exp(cum)
    w_ex = jnp.exp(cum - lw)
    w_inv = jnp.exp(-cum)
    rt, kkt, kt, bt = r * w_in, kk * w_ex, k * w_inv, b * w_inv
    a_b = mmh_nt(kkt, bt) * strict
    a_k = mmh_nt(kkt, kt) * strict
    m_k = mmh_nt(rt, kt) * incl
    m_b = mmh_nt(rt, bt) * incl
    x = _unit_lower_inverse(a_b) if x_known is None else _inverse_known(a_b, x_known)
    u = mmh(x, mmh_nt(kkt, s0) + mmh(a_k, v))
    y = mmh_nt(rt, s0) + mmh(m_k, v) - mmh(m_b, u)
    w_last = jnp.exp(jnp.sum(lw, axis=1, keepdims=True))
    s1 = (s0 + mmh_tn(v, kt) - mmh_tn(u, bt)) * w_last
    return y, s1, x


def _ex_split(ex, refs, n_in, n_out):
    n = ex.n
    ins, ex_in = refs[:n_in], refs[n_in:n_in + n]
    outs, ex_out = refs[n_in + n:n_in + n + n_out], refs[n_in + n + n_out:n_in + 2 * n + n_out]
    rest = refs[n_in + 2 * n + n_out:]
    return ins, outs, rest[:len(rest) - 3], (ex_in, ex_out) + tuple(rest[len(rest) - 3:])


def rwkv_scan_fwd(r, lw, k, v, kk, b, ex):
    H, T, N = r.shape
    groups = SCAN_GROUPS_FWD
    nc, hg = T // CHUNK, H // groups
    seq = pl.BlockSpec((hg, CHUNK, N), lambda g, c: (g, c, 0))

    def body(*refs):
        (r_ref, lw_ref, k_ref, v_ref, kk_ref, b_ref), (y_ref, hs_ref, xs_ref), (h_scr,), ex_refs = _ex_split(ex, refs, 6, 3)
        g, c = pl.program_id(0), pl.program_id(1)

        @pl.when(jnp.logical_and(g == 0, c == 0))
        def _():
            ex.start(*ex_refs)

        @pl.when(c == 0)
        def _():
            h_scr[...] = jnp.zeros_like(h_scr)

        h0 = h_scr[...]
        hs_ref[0] = h0
        y, h1, x = f_rwkv_chunk(h0, r_ref[...], lw_ref[...], k_ref[...], v_ref[...], kk_ref[...], b_ref[...])
        y_ref[...] = y
        xs_ref[0] = x
        h_scr[...] = h1

        @pl.when(jnp.logical_and(g == groups - 1, c == (3 * nc) // 4))
        def _():
            ex.forward(*ex_refs)

        @pl.when(jnp.logical_and(g == groups - 1, c == nc - 1))
        def _():
            ex.wait(*ex_refs)

    res = pl.pallas_call(
        body, grid=(groups, nc), in_specs=[seq] * 6 + [_ANY] * ex.n,
        out_specs=[seq, pl.BlockSpec((1, hg, N, N), lambda g, c: (c, g, 0, 0)),
                   pl.BlockSpec((1, hg, CHUNK, CHUNK), lambda g, c: (c, g, 0, 0))] + [_ANY] * ex.n,
        out_shape=[jax.ShapeDtypeStruct((H, T, N), F32), jax.ShapeDtypeStruct((nc, H, N, N), F32),
                   jax.ShapeDtypeStruct((nc, H, CHUNK, CHUNK), F32)] + ex.out_shape(),
        scratch_shapes=[pltpu.VMEM((hg, N, N), F32)] + ex.scratch(),
        compiler_params=_cparams(("arbitrary", "arbitrary")), name="rwkv_scan_fwd")(r, lw, k, v, kk, b, *ex.operands())
    return res[0], (res[1], res[2]), list(res[3:])


def rwkv_scan_bwd(r, lw, k, v, kk, b, saved, dy, ex):
    H, T, N = r.shape
    groups = SCAN_GROUPS_BWD
    nc, hg = T // CHUNK, H // groups
    seq = pl.BlockSpec((hg, CHUNK, N), lambda g, c: (g, nc - 1 - c, 0))
    state = pl.BlockSpec((1, hg, N, N), lambda g, c: (nc - 1 - c, g, 0, 0))

    def body(*refs):
        (r_ref, lw_ref, k_ref, v_ref, kk_ref, b_ref, hs_ref, xs_ref, dy_ref), outs, (dh_scr,), ex_refs = _ex_split(ex, refs, 9, 6)
        g, c = pl.program_id(0), pl.program_id(1)

        @pl.when(jnp.logical_and(g == 0, c == 0))
        def _():
            ex.start(*ex_refs)

        @pl.when(c == 0)
        def _():
            dh_scr[...] = jnp.zeros_like(dh_scr)

        x_known = xs_ref[0]
        _, vjp = jax.vjp(lambda *a: f_rwkv_chunk(*a, x_known=x_known)[:2], hs_ref[0], r_ref[...], lw_ref[...], k_ref[...],
                         v_ref[...], kk_ref[...], b_ref[...])
        d = vjp((dy_ref[...], dh_scr[...]))
        dh_scr[...] = d[0]
        for o_ref, dz in zip(outs, d[1:]):
            o_ref[...] = dz

        @pl.when(jnp.logical_and(g == groups - 1, c == nc - 1))
        def _():
            ex.forward(*ex_refs)
            ex.wait(*ex_refs)

    res = pl.pallas_call(
        body, grid=(groups, nc),
        in_specs=[seq] * 6 + [state, state, seq] + [_ANY] * ex.n,
        out_specs=[seq] * 6 + [_ANY] * ex.n, out_shape=[jax.ShapeDtypeStruct((H, T, N), F32)] * 6 + ex.out_shape(),
        scratch_shapes=[pltpu.VMEM((hg, N, N), F32)] + ex.scratch(),
        compiler_params=_cparams(("arbitrary", "arbitrary")), name="rwkv_scan_bwd")(r, lw, k, v, kk, b, *saved, dy, *ex.operands())
    return list(res[:6]), list(res[6:])


DIL_ROWS = 4


def _f_dilattn(has_prev, q, kc, kp, vc, vp):
    scale = 1.0 / math.sqrt(HEAD_DIM)
    sc = mm_nt(q, kc) * scale
    sp = mm_nt(q, kp) * scale
    i = lax.broadcasted_iota(jnp.int32, (DIL_BLOCK, DIL_BLOCK), 0)
    j = lax.broadcasted_iota(jnp.int32, (DIL_BLOCK, DIL_BLOCK), 1)
    sc = jnp.where(j <= i, sc, NEG_INF)
    sp = jnp.where(jnp.logical_and(i <= j, has_prev), sp, NEG_INF)
    mx = jnp.maximum(jnp.max(sc, axis=-1, keepdims=True), jnp.max(sp, axis=-1, keepdims=True))
    pc, pp = jnp.exp(sc - mx), jnp.exp(sp - mx)
    den = jnp.sum(pc, axis=-1, keepdims=True) + jnp.sum(pp, axis=-1, keepdims=True)
    o = (mm_nn(pc, vc) + mm_nn(pp, vp)) / den
    return o, mx + jnp.log(den)


def _dil_specs(R, U):
    cur = pl.BlockSpec((DIL_ROWS, DIL_BLOCK, HEAD_DIM), lambda g, n: (g, n, 0))
    prev = pl.BlockSpec((DIL_ROWS, DIL_BLOCK, HEAD_DIM), lambda g, n: (g, jnp.maximum(n - 1, 0), 0))
    lse = pl.BlockSpec((DIL_ROWS, DIL_BLOCK, 1), lambda g, n: (g, n, 0))
    return cur, prev, lse, (R // DIL_ROWS, U // DIL_BLOCK)


def dil_fwd(q, k, v, name):
    R, U, _ = q.shape
    cur, prev, lse, grid = _dil_specs(R, U)

    def body(q_ref, kc_ref, kp_ref, vc_ref, vp_ref, o_ref, l_ref):
        o, l = _f_dilattn(pl.program_id(1) > 0, q_ref[...], kc_ref[...], kp_ref[...], vc_ref[...], vp_ref[...])
        o_ref[...] = o
        l_ref[...] = l

    return pl.pallas_call(
        body, grid=grid, in_specs=[cur, cur, prev, cur, prev], out_specs=[cur, lse],
        out_shape=[jax.ShapeDtypeStruct((R, U, HEAD_DIM), F32), jax.ShapeDtypeStruct((R, U, 1), F32)],
        compiler_params=_cparams(("parallel", "parallel")), name=name)(q, k, k, v, v)


def dil_bwd(q, k, v, do, dl, name):
    R, U, _ = q.shape
    cur, prev, lse, grid = _dil_specs(R, U)

    def body(q_ref, kc_ref, kp_ref, vc_ref, vp_ref, do_ref, dl_ref, *outs):
        f = functools.partial(_f_dilattn, pl.program_id(1) > 0)
        _, vjp = jax.vjp(f, q_ref[...], kc_ref[...], kp_ref[...], vc_ref[...], vp_ref[...])
        for o_ref, g in zip(outs, vjp((do_ref[...], dl_ref[...]))):
            o_ref[...] = g

    return pl.pallas_call(
        body, grid=grid, in_specs=[cur, cur, prev, cur, prev, cur, lse], out_specs=[cur] * 5,
        out_shape=[jax.ShapeDtypeStruct((R, U, HEAD_DIM), F32)] * 5,
        compiler_params=_cparams(("parallel", "parallel")), name=name)(q, k, k, v, v, do, dl)


CONV_TILE = 128


def _conv3(u, h6, h7, w, b):
    rows = lax.broadcasted_iota(jnp.int32, (u.shape[0], 1), 0)
    s1 = jnp.where(rows == 0, h7, pltpu.roll(u, 1, 0))
    s2 = jnp.where(rows == 0, h6, jnp.where(rows == 1, h7, pltpu.roll(u, 2, 0)))
    return b + w[0:1] * s2 + w[1:2] * s1 + w[2:3] * u, s1, s2


def _conv_halves(u_ref, h_ref, cw_ref, cb_ref):
    F = D_FF
    first = pl.program_id(0) > 0
    res = []
    for lo in (0, F):
        h = h_ref[:, lo:lo + F]
        h6 = jnp.where(first, h[6:7], 0.0)
        h7 = jnp.where(first, h[7:8], 0.0)
        u = u_ref[:, lo:lo + F]
        res.append((u,) + _conv3(u, h6, h7, cw_ref[:, lo:lo + F], cb_ref[:, lo:lo + F]))
    return res


def _halo_before(C):
    return pl.BlockSpec((8, C), lambda i: (jnp.maximum(i * (CONV_TILE // 8) - 1, 0), 0))


def convgate_fwd(u, cw, cb, name):
    T, C = u.shape
    F = C // 2

    def body(u_ref, h_ref, cw_ref, cb_ref, z_ref):
        (_, cg, _, _), (_, cv, _, _) = _conv_halves(u_ref, h_ref, cw_ref, cb_ref)
        z_ref[...] = (cg * _sigmoid(cg) * cv).astype(BF16)

    return pl.pallas_call(
        body, grid=(T // CONV_TILE,),
        in_specs=[pl.BlockSpec((CONV_TILE, C), lambda i: (i, 0)), _halo_before(C), _full_spec(cw.shape), _full_spec(cb.shape)],
        out_specs=pl.BlockSpec((CONV_TILE, F), lambda i: (i, 0)), out_shape=jax.ShapeDtypeStruct((T, F), BF16),
        compiler_params=_cparams(("parallel",)), name=name)(u, u, cw, cb)


def convgate_bwd_c(u, cw, cb, dz, name):
    T, C = u.shape
    F = C // 2

    def body(u_ref, h_ref, cw_ref, cb_ref, dz_ref, dc_ref, dcw_ref, dcb_ref):
        (ug, cg, g1, g2), (uv, cv, v1, v2) = _conv_halves(u_ref, h_ref, cw_ref, cb_ref)
        dz = dz_ref[...]
        sg = _sigmoid(cg)
        dgate = dz * cv * sg * (1.0 + cg * (1.0 - sg))
        dval = dz * cg * sg
        dc_ref[:, :F] = dgate
        dc_ref[:, F:] = dval

        @pl.when(pl.program_id(0) == 0)
        def _():
            dcw_ref[...] = jnp.zeros_like(dcw_ref)
            dcb_ref[...] = jnp.zeros_like(dcb_ref)

        for lo, d, u0, s1, s2 in ((0, dgate, ug, g1, g2), (F, dval, uv, v1, v2)):
            dcb_ref[:, lo:lo + F] += jnp.sum(d, axis=0, keepdims=True)
            dcw_ref[0:1, lo:lo + F] += jnp.sum(d * s2, axis=0, keepdims=True)
            dcw_ref[1:2, lo:lo + F] += jnp.sum(d * s1, axis=0, keepdims=True)
            dcw_ref[2:3, lo:lo + F] += jnp.sum(d * u0, axis=0, keepdims=True)

    return pl.pallas_call(
        body, grid=(T // CONV_TILE,),
        in_specs=[pl.BlockSpec((CONV_TILE, C), lambda i: (i, 0)), _halo_before(C), _full_spec(cw.shape), _full_spec(cb.shape),
                  pl.BlockSpec((CONV_TILE, F), lambda i: (i, 0))],
        out_specs=[pl.BlockSpec((CONV_TILE, C), lambda i: (i, 0)), _full_spec(cw.shape), _full_spec(cb.shape)],
        out_shape=[jax.ShapeDtypeStruct((T, C), F32), jax.ShapeDtypeStruct(cw.shape, F32), jax.ShapeDtypeStruct(cb.shape, F32)],
        compiler_params=_cparams(("arbitrary",)), name=name)(u, u, cw, cb, dz)


def convgate_bwd_u(dc, cw, name):
    T, C = dc.shape
    n = T // CONV_TILE

    def body(dc_ref, nx_ref, cw_ref, du_ref):
        last = pl.program_id(0) < n - 1
        rows = lax.broadcasted_iota(jnp.int32, (CONV_TILE, 1), 0)
        for lo in (0, C // 2):
            sl = slice(lo, lo + C // 2)
            d = dc_ref[:, sl]
            n0 = jnp.where(last, nx_ref[0:1, sl], 0.0)
            n1 = jnp.where(last, nx_ref[1:2, sl], 0.0)
            up1 = jnp.where(rows == CONV_TILE - 1, n0, pltpu.roll(d, CONV_TILE - 1, 0))
            up2 = jnp.where(rows == CONV_TILE - 1, n1, jnp.where(rows == CONV_TILE - 2, n0, pltpu.roll(d, CONV_TILE - 2, 0)))
            du_ref[:, sl] = (cw_ref[2:3, sl] * d + cw_ref[1:2, sl] * up1 + cw_ref[0:1, sl] * up2).astype(BF16)

    nxt = pl.BlockSpec((8, C), lambda i: (jnp.minimum((i + 1) * (CONV_TILE // 8), T // 8 - 1), 0))
    return pl.pallas_call(
        body, grid=(n,), in_specs=[pl.BlockSpec((CONV_TILE, C), lambda i: (i, 0)), nxt, _full_spec(cw.shape)],
        out_specs=pl.BlockSpec((CONV_TILE, C), lambda i: (i, 0)), out_shape=jax.ShapeDtypeStruct((T, C), BF16),
        compiler_params=_cparams(("parallel",)), name=name)(dc, dc, cw)


def loss_head(y, tgt):
    T, D = y.shape
    tile = 256

    def body(y_ref, t_ref, l_ref, d_ref, db_ref):
        d = y_ref[...] - t_ref[...]
        d_ref[...] = d * (1.0 / D)
        db_ref[...] = (d * (1.0 / D)).astype(BF16)

        @pl.when(pl.program_id(0) == 0)
        def _():
            l_ref[...] = jnp.zeros_like(l_ref)

        l_ref[...] += (0.5 / D) * jnp.sum(d * d)

    row = pl.BlockSpec((tile, D), lambda i: (i, 0))
    return pl.pallas_call(
        body, grid=(T // tile,), in_specs=[row, row], out_specs=[pl.BlockSpec((8, 128), lambda i: (0, 0)), row, row],
        out_shape=[jax.ShapeDtypeStruct((8, 128), F32), jax.ShapeDtypeStruct((T, D), F32), jax.ShapeDtypeStruct((T, D), BF16)],
        compiler_params=_cparams(("arbitrary",)), name="loss_head")(y, tgt)


def sum_parts(parts, name):
    S, R, C = parts.shape
    tile = _pick(R, (256, 128, 64, 32, 16, 8))

    def body(p_ref, o_ref):
        acc = p_ref[0]
        for s in range(1, S):
            acc = acc + p_ref[s]
        o_ref[...] = acc

    return pl.pallas_call(
        body, grid=(R // tile,), in_specs=[pl.BlockSpec((S, tile, C), lambda i: (0, i, 0))],
        out_specs=pl.BlockSpec((tile, C), lambda i: (i, 0)), out_shape=jax.ShapeDtypeStruct((R, C), F32),
        compiler_params=_cparams(("parallel",)), name=name)(parts)


def adamw(gparts, w, m, v, name):
    S, R, C = gparts.shape
    tile = _pick(R, (256, 128, 64, 32, 16, 8))
    c1 = 1.0 / (1.0 - ADAM_B1 ** ADAM_STEP)
    c2 = 1.0 / (1.0 - ADAM_B2 ** ADAM_STEP)

    def body(g_ref, w_ref, m_ref, v_ref, go_ref, d_ref, mo_ref, vo_ref):
        g = g_ref[0].astype(F32)
        for s in range(1, S):
            g = g + g_ref[s].astype(F32)
        m1 = ADAM_B1 * m_ref[...] + (1.0 - ADAM_B1) * g
        v1 = ADAM_B2 * v_ref[...] + (1.0 - ADAM_B2) * (g * g)
        go_ref[...] = g
        mo_ref[...] = m1
        vo_ref[...] = v1
        d_ref[...] = -ADAM_LR * ((m1 * c1) / (jnp.sqrt(v1 * c2) + ADAM_EPS) + ADAM_WD * w_ref[...])

    row = pl.BlockSpec((tile, C), lambda i: (i, 0))
    return pl.pallas_call(
        body, grid=(R // tile,), in_specs=[pl.BlockSpec((S, tile, C), lambda i: (0, i, 0)), row, row, row],
        out_specs=[row] * 4, out_shape=[jax.ShapeDtypeStruct((R, C), F32)] * 4,
        compiler_params=_cparams(("parallel",)), name=name)(gparts, w, m, v)


def _peers():
    x, y, c = lax.axis_index("x"), lax.axis_index("y"), lax.axis_index("c")
    peers = []
    for k in range(1, N_DEV):
        px = 1 - x if k & 4 else x
        py = 1 - y if k & 2 else y
        pc = 1 - c if k & 1 else c
        peers.append(((px, py, pc), 4 * px + 2 * py + pc))
    return 4 * x + 2 * y + c, peers


_ANY = pl.BlockSpec(memory_space=pl.ANY)


class Exchange:
    def __init__(self, gathers=(), scatters=()):
        self.gathers, self.scatters = list(gathers), list(scatters)
        self.n = len(self.gathers) + len(self.scatters)

    def operands(self):
        return self.gathers + self.scatters

    def out_shape(self):
        return ([jax.ShapeDtypeStruct((N_DEV,) + x.shape, x.dtype) for x in self.gathers]
                + [jax.ShapeDtypeStruct(x.shape, x.dtype) for x in self.scatters])

    def scratch(self):
        n = max(self.n, 1)
        return [pltpu.SemaphoreType.DMA((7 * n,)), pltpu.SemaphoreType.DMA((7 * n,)), pltpu.SemaphoreType.DMA((n,))]

    def _copies(self, in_refs, out_refs, send_sems, recv_sems, local_sems):
        me, peers = _peers()
        ng = len(self.gathers)
        local, sends, recvs = [], [], []
        for a in range(self.n):
            x, o = in_refs[a], out_refs[a]
            mine = x if a < ng else x.at[me]
            local.append(pltpu.make_async_copy(mine, o.at[me], local_sems.at[a]))
            s_a, r_a = {}, {}
            for k in range(1, N_DEV):
                peer, slot = peers[k - 1]
                sems = dict(send_sem=send_sems.at[7 * a + k - 1], recv_sem=recv_sems.at[7 * a + k - 1],
                            device_id_type=pl.DeviceIdType.MESH)
                if a >= ng:
                    s_a[k] = pltpu.make_async_remote_copy(src_ref=x.at[slot], dst_ref=o.at[me], device_id=peer, **sems)
                elif k in FORWARDED:
                    came = o.at[peers[k - 2][1]]
                    s_a[k] = pltpu.make_async_remote_copy(src_ref=came, dst_ref=came, device_id=peers[0][0], **sems)
                else:
                    s_a[k] = pltpu.make_async_remote_copy(src_ref=x, dst_ref=o.at[me], device_id=peer, **sems)
                r_a[k] = pltpu.make_async_remote_copy(src_ref=mine, dst_ref=o.at[slot], device_id=peer, **sems)
            sends.append(s_a)
            recvs.append(r_a)
        return local, sends, recvs

    def start(self, *refs):
        if self.n == 0:
            return
        local, sends, _ = self._copies(*refs)
        for a in range(self.n):
            local[a].start()
            for k in range(1, N_DEV):
                if a >= len(self.gathers) or k not in FORWARDED:
                    sends[a][k].start()

    def forward(self, *refs):
        if not self.gathers:
            return
        _, sends, recvs = self._copies(*refs)
        for a in range(len(self.gathers)):
            for k in FORWARDED:
                recvs[a][k - 1].wait_recv()
                sends[a][k].start()

    def wait(self, *refs):
        if self.n == 0:
            return
        local, sends, recvs = self._copies(*refs)
        for a in range(self.n):
            waited_early = [f - 1 for f in FORWARDED] if a < len(self.gathers) else []
            for k in range(1, N_DEV):
                if k not in waited_early:
                    recvs[a][k].wait_recv()
            for k in range(1, N_DEV):
                sends[a][k].wait_send()
            local[a].wait()


FORWARDED = (3, 5, 7)


def exchange(ex, name):
    n = ex.n

    def body(*refs):
        args = (refs[:n], refs[n:2 * n]) + tuple(refs[2 * n:])
        ex.start(*args)
        ex.forward(*args)
        ex.wait(*args)

    return pl.pallas_call(body, in_specs=[_ANY] * n, out_specs=[_ANY] * n, out_shape=ex.out_shape(),
                          scratch_shapes=ex.scratch(), name=name)(*ex.operands())


def _heads(z, h):
    return z.reshape(z.shape[0], h, HEAD_DIM).transpose(1, 0, 2)


def _unheads(z):
    return z.transpose(1, 0, 2).reshape(z.shape[1], z.shape[0] * HEAD_DIM)


def _regroup(z, d):
    h, T, w = z.shape
    return z.reshape(h, T // d, d, w).transpose(0, 2, 1, 3).reshape(h * d, T // d, w)


def _ungroup(z, d):
    hd, U, w = z.shape
    return z.reshape(hd // d, d, U, w).transpose(0, 2, 1, 3).reshape(hd // d, U * d, w)


def _shift_down(z):
    return jnp.concatenate([jnp.zeros_like(z[:1]), z[:-1]], axis=0)


def _shift_up(z):
    return jnp.concatenate([z[1:], jnp.zeros_like(z[:1])], axis=0)


def _segments(width):
    seg = np.zeros((width, 128), np.float32)
    seg[np.arange(width), np.arange(width) // HEAD_DIM] = 1.0
    return jnp.asarray(seg), jnp.asarray(seg.T)


def _rope_consts(T):
    inv = ROPE_THETA ** (-jnp.arange(0, HEAD_DIM, 2, dtype=F32) / HEAD_DIM)
    ang = jnp.arange(T, dtype=F32)[:, None] * inv[None, :]
    cos, sin = jnp.cos(ang), jnp.sin(ang)
    rot = np.zeros((HEAD_DIM, HEAD_DIM), np.float32)
    half = HEAD_DIM // 2
    rot[np.arange(half) + half, np.arange(half)] = -1.0
    rot[np.arange(half), np.arange(half) + half] = 1.0
    return jnp.concatenate([cos, cos], axis=1), jnp.concatenate([sin, sin], axis=1), jnp.asarray(rot)


RW_CUTS = (0, 768, 1536, 2304, 2368, 2432, 2560)
RW_TILE = 128
ROW_TILE = 256


def _local_step(x0, memx, tgt, P, ex_weights=None, weights_done=None, ex_grads=None):
    T = x0.shape[0]
    P = dict(P)
    G = {}
    seg, seg_t = _segments(RWKV_WIDTH)
    cos, sin, rot = _rope_consts(T)
    row = lambda v: v.reshape(1, -1)

    def mem_fwd(i, qmem):
        memn = stage_fwd(f_rmsnorm, [memx], [P["mem_norm"][i:i + 1]], [], [], N_MEM, f"mem{i}_norm", [BF16])[0]
        kvm = matmul(memn, P["mem_w_kv"][i], "nn", f"mem{i}_kv")
        kraw, vm = _heads(kvm[:, :MEM_WIDTH], MEM_HEADS), _heads(kvm[:, MEM_WIDTH:], MEM_HEADS)
        km = stage_fwd(f_headnorm, [kraw], [P["mem_k_norm"][i:i + 1]], [], [], N_MEM, f"mem{i}_knorm")[0]
        qh = _heads(qmem, MEM_HEADS)
        om = stage_fwd(f_memattn, [qh], [km, vm, P["mem_q_norm"][i:i + 1]], [], [], ROW_TILE, f"mem{i}_attn")[0]
        return _unheads(om), (memn, kraw, vm, km, qh)

    def mem_bwd(i, saved, dymem):
        memn, kraw, vm, km, qh = saved
        (dqh,), (dkm, dvm, g_qn) = stage_bwd(f_memattn, [qh], [km, vm, P["mem_q_norm"][i:i + 1]], [], [],
                                             [_heads(dymem, MEM_HEADS)], ROW_TILE, f"mem{i}_attn_bwd")
        (dkraw,), (g_kn,) = stage_bwd(f_headnorm, [kraw], [P["mem_k_norm"][i:i + 1]], [], [], [dkm], N_MEM, f"mem{i}_knorm_bwd")
        dkvm = jnp.concatenate([_unheads(dkraw), _unheads(dvm)], axis=1).astype(BF16)
        g_w = matmul(memn, dkvm, "tn", f"mem{i}_kv_dw")
        dmemn = matmul(dkvm, P["mem_w_kv"][i], "nt", f"mem{i}_kv_dx")
        _, (g_mn,) = stage_bwd(f_rmsnorm, [memx], [P["mem_norm"][i:i + 1]], [], [], [dmemn], N_MEM, f"mem{i}_norm_bwd")
        return _unheads(dqh), g_mn, g_w, g_qn, g_kn

    def ffn_fwd(i, xin):
        hn = stage_fwd(f_rmsnorm, [xin], [P["ffn_norm"][i:i + 1]], [], [], ROW_TILE, f"ffn{i}_norm", [BF16])[0]
        u = matmul(hn, P["ffn_w_up"][i], "nn", f"ffn{i}_up")
        z = convgate_fwd(u, P["ffn_conv_w"][i], P["ffn_conv_b"][i:i + 1], f"ffn{i}_conv")
        return matmul(z, P["ffn_w_down"][i], "nn", f"ffn{i}_down", residual=xin), (hn, u, z)

    def ffn_bwd(i, xin, saved, dxo, dxo_b):
        hn, u, z = saved
        dz = matmul(dxo_b, P["ffn_w_down"][i], "nt", f"ffn{i}_down_dx")
        g_down = matmul(z, dxo_b, "tn", f"ffn{i}_down_dw")
        dc, g_cw, g_cb = convgate_bwd_c(u, P["ffn_conv_w"][i], P["ffn_conv_b"][i:i + 1], dz, f"ffn{i}_conv_bwd_c")
        du = convgate_bwd_u(dc, P["ffn_conv_w"][i], f"ffn{i}_conv_bwd_u")
        dhn = matmul(du, P["ffn_w_up"][i], "nt", f"ffn{i}_up_dx")
        g_up = matmul(hn, du, "tn", f"ffn{i}_up_dw")
        (dxin,), (g_n,), (dxin_b,) = stage_bwd(f_rmsnorm_res, [xin], [P["ffn_norm"][i:i + 1]], [], [], [dhn, dxo], ROW_TILE,
                                               f"ffn{i}_norm_bwd", bf16_copies=(0,))
        return dxin, dxin_b, g_n, g_up, g_cw, g_cb, g_down

    h0 = stage_fwd(f_rmsnorm, [x0], [P["attn_norm"][0:1]], [], [], ROW_TILE, "l0_norm", [BF16])[0]
    p0 = matmul(h0, P["a_w_in"][0], "nn", "l0_in")
    cur = [p0[:, a:b] for a, b in zip(RW_CUTS[:-1], RW_CUTS[1:])]
    prev = [_shift_down(c) for c in cur]
    mu = [P["a_mu"][:, a:b] for a, b in zip(RW_CUTS[:-1], RW_CUTS[1:])]
    pre_ps = mu + [P["a_w0"], P["a_w2"][0], P["a_a0"], P["a_a2"][0], P["a_g2"][0], P["a_k_k"], P["a_k_a"]]
    r, lw, k2, v, kk, b, g = stage_fwd(f_rwkv_pre, cur + prev, pre_ps, [], [seg, seg_t], RW_TILE, "l0_rwkv_pre")
    hm = lambda z: _heads(z, RWKV_HEADS)
    scan_in = [hm(z) for z in (r, lw, k2, v, kk, b)]
    y_h, h_states, got = rwkv_scan_fwd(*scan_in, ex_weights or Exchange())
    if weights_done is not None:
        P.update(weights_done(got))
    y_s = _unheads(y_h)
    post_ps = [P["a_lnx_w"], P["a_lnx_b"], P["a_r_k"].reshape(1, RWKV_WIDTH)]
    ymix0 = stage_fwd(f_rwkv_post, [y_s, r, k2, v, g], post_ps, [], [seg, seg_t], RW_TILE, "l0_rwkv_post")[0]
    ymem0, mem0_saved = mem_fwd(0, p0[:, SHIFT_WIDTH:])
    ycat0 = jnp.concatenate([ymix0, ymem0], axis=1).astype(BF16)
    x1 = matmul(ycat0, P["a_w_out"][0], "nn", "l0_out", residual=x0)
    x2, ffn0_saved = ffn_fwd(0, x1)

    hk, h1 = stage_fwd(f_rmsnorm2, [x2], [row(P["kv_norm"]), P["attn_norm"][1:2]], [], [], ROW_TILE, "l1_norm", [BF16, BF16])
    kvp = matmul(hk, P["kv_w"][0], "nn", "l1_kv")
    p1 = matmul(h1, P["b_w_in"][0], "nn", "l1_in")
    kraw, vsh, qraw = hm(kvp[:, :DIL_WIDTH]), hm(kvp[:, DIL_WIDTH:]), hm(p1[:, :DIL_WIDTH])
    ksh = stage_fwd(f_qkprep, [kraw], [row(P["kv_k_norm"])], [cos, sin], [rot], ROW_TILE, "l1_kprep")[0]
    q = stage_fwd(f_qkprep, [qraw], [P["b_q_norm"]], [cos, sin], [rot], ROW_TILE, "l1_qprep")[0]
    grp, outs, lses = [], [], []
    for gi, (_, d) in enumerate(DIL_GROUPS):
        hs = slice(4 * gi, 4 * gi + 4)
        qg, kg, vg = _regroup(q[hs], d), _regroup(ksh[hs], d), _regroup(vsh[hs], d)
        og, lg = dil_fwd(qg, kg, vg, f"l1_dil{gi}")
        grp.append((qg, kg, vg))
        outs.append(_ungroup(og, d))
        lses.append(_ungroup(lg, d))
    omix = stage_fwd(f_mix, outs + lses, [], [], [], ROW_TILE, "l1_mix")[0]
    ymem1, mem1_saved = mem_fwd(1, p1[:, DIL_WIDTH:])
    ycat1 = jnp.concatenate([_unheads(omix), ymem1], axis=1).astype(BF16)
    x3 = matmul(ycat1, P["b_w_out"][0], "nn", "l1_out", residual=x2)
    x4, ffn1_saved = ffn_fwd(1, x3)
    loss_part, dx4, dx4_b = loss_head(x4, tgt)

    dx3, dx3_b, gn1, gup1, gcw1, gcb1, gdown1 = ffn_bwd(1, x3, ffn1_saved, dx4, dx4_b)
    dycat1 = matmul(dx3_b, P["b_w_out"][0], "nt", "l1_out_dx")
    G["b_w_out"] = [matmul(ycat1, dx3_b, "tn", "l1_out_dw")]
    dqmem1, gmn1, gmw1, gmq1, gmk1 = mem_bwd(1, mem1_saved, dycat1[:, MEM_WIDTH:])
    dmix, _ = stage_bwd(f_mix, outs + lses, [], [], [], [_heads(dycat1[:, :MEM_WIDTH], 4)], ROW_TILE, "l1_mix_bwd")
    dq, dk, dv = [], [], []
    for gi, (_, d) in enumerate(DIL_GROUPS):
        qg, kg, vg = grp[gi]
        dqg, dkc, dkp, dvc, dvp = dil_bwd(qg, kg, vg, _regroup(dmix[gi], d), _regroup(dmix[3 + gi], d), f"l1_dil{gi}_bwd")
        back = lambda c, p: c + jnp.concatenate([p[:, DIL_BLOCK:], jnp.zeros_like(p[:, :DIL_BLOCK])], axis=1)
        dq.append(_ungroup(dqg, d))
        dk.append(_ungroup(back(dkc, dkp), d))
        dv.append(_ungroup(back(dvc, dvp), d))
    dq, dk, dv = jnp.concatenate(dq, 0), jnp.concatenate(dk, 0), jnp.concatenate(dv, 0)
    (dqraw,), (g_bq,) = stage_bwd(f_qkprep, [qraw], [P["b_q_norm"]], [cos, sin], [rot], [dq], ROW_TILE, "l1_qprep_bwd")
    (dkraw,), (g_kk,) = stage_bwd(f_qkprep, [kraw], [row(P["kv_k_norm"])], [cos, sin], [rot], [dk], ROW_TILE, "l1_kprep_bwd")
    dp1 = jnp.concatenate([_unheads(dqraw), dqmem1], axis=1).astype(BF16)
    dkvp = jnp.concatenate([_unheads(dkraw), _unheads(dv)], axis=1).astype(BF16)
    dh1 = matmul(dp1, P["b_w_in"][0], "nt", "l1_in_dx")
    G["b_w_in"] = [matmul(h1, dp1, "tn", "l1_in_dw")]
    dhk = matmul(dkvp, P["kv_w"][0], "nt", "l1_kv_dx")
    G["kv_w"] = [matmul(hk, dkvp, "tn", "l1_kv_dw")]
    (dx2,), (g_kvn, g_an1), (dx2_b,) = stage_bwd(f_rmsnorm2_res, [x2], [row(P["kv_norm"]), P["attn_norm"][1:2]], [], [],
                                                 [dhk, dh1, dx3], ROW_TILE, "l1_norm_bwd", bf16_copies=(0,))

    dx1, dx1_b, gn0, gup0, gcw0, gcb0, gdown0 = ffn_bwd(0, x1, ffn0_saved, dx2, dx2_b)
    dycat0 = matmul(dx1_b, P["a_w_out"][0], "nt", "l0_out_dx")
    G["a_w_out"] = [matmul(ycat0, dx1_b, "tn", "l0_out_dw")]
    dqmem0, gmn0, gmw0, gmq0, gmk0 = mem_bwd(0, mem0_saved, dycat0[:, RWKV_WIDTH:])
    (dy_s, dr_a, dk_a, dv_a, dg), (g_lw, g_lb, g_rk) = stage_bwd(
        f_rwkv_post, [y_s, r, k2, v, g], post_ps, [], [seg, seg_t], [dycat0[:, :RWKV_WIDTH]], RW_TILE, "l0_rwkv_post_bwd")
    G["mem_w_kv"], G["ffn_w_up"], G["ffn_w_down"] = [gmw0, gmw1], [gup0, gup1], [gdown0, gdown1]
    d_scan, G["_exchanged"] = rwkv_scan_bwd(*scan_in, h_states, hm(dy_s), ex_grads(G) if ex_grads else Exchange())
    dr_b, dlw, dk_b, dv_b, dkk, db = [_unheads(z) for z in d_scan]
    dpre, gpre = stage_bwd(f_rwkv_pre, cur + prev, pre_ps, [], [seg, seg_t],
                           [[dr_a, dr_b], dlw, [dk_a, dk_b], [dv_a, dv_b], dkk, db, dg], RW_TILE, "l0_rwkv_pre_bwd")
    dp_rw = jnp.concatenate(dpre[:6], axis=1) + _shift_up(jnp.concatenate(dpre[6:], axis=1))
    dp0 = jnp.concatenate([dp_rw, dqmem0], axis=1).astype(BF16)
    dh0 = matmul(dp0, P["a_w_in"][0], "nt", "l0_in_dx")
    G["a_w_in"] = [matmul(h0, dp0, "tn", "l0_in_dw")]
    (dx0,), (g_an0,) = stage_bwd(f_rmsnorm_res, [x0], [P["attn_norm"][0:1]], [], [], [dh0, dx1], ROW_TILE, "l0_norm_bwd")

    G["attn_norm"] = jnp.concatenate([g_an0, g_an1], axis=0)
    G["a_mu"] = jnp.concatenate(gpre[:6], axis=1)
    G["a_w0"], G["a_w2"], G["a_a0"], G["a_a2"], G["a_g2"] = gpre[6], gpre[7][None], gpre[8], gpre[9][None], gpre[10][None]
    G["a_k_k"], G["a_k_a"] = gpre[11], gpre[12]
    G["a_r_k"] = g_rk.reshape(1, RWKV_HEADS, HEAD_DIM)
    G["a_lnx_w"], G["a_lnx_b"] = g_lw, g_lb
    G["kv_norm"], G["kv_k_norm"], G["b_q_norm"] = g_kvn.reshape(-1), g_kk.reshape(-1), g_bq
    G["mem_norm"] = jnp.concatenate([gmn0, gmn1], axis=0)
    G["mem_w_kv"] = [gmw0, gmw1]
    G["mem_q_norm"] = jnp.concatenate([gmq0, gmq1], axis=0)
    G["mem_k_norm"] = jnp.concatenate([gmk0, gmk1], axis=0)
    G["ffn_norm"] = jnp.concatenate([gn0, gn1], axis=0)
    G["ffn_w_up"] = [gup0, gup1]
    G["ffn_conv_w"] = jnp.stack([gcw0, gcw1])
    G["ffn_conv_b"] = jnp.concatenate([gcb0, gcb1], axis=0)
    G["ffn_w_down"] = [gdown0, gdown1]
    return loss_part, dx0, G


PARAMS = (("attn_norm", None), ("a_w_in", 2), ("a_mu", 1), ("a_w0", 1), ("a_w2", 2), ("a_a0", 1), ("a_a2", 2), ("a_g2", 2),
          ("a_k_k", 1), ("a_k_a", 1), ("a_r_k", None), ("a_lnx_w", 1), ("a_lnx_b", 1), ("a_w_out", 1), ("kv_norm", None),
          ("kv_w", 1), ("kv_k_norm", None), ("b_w_in", 1), ("b_q_norm", None), ("b_w_out", 2), ("mem_norm", None),
          ("mem_w_kv", 1), ("mem_q_norm", None), ("mem_k_norm", None), ("ffn_norm", None), ("ffn_w_up", 2),
          ("ffn_conv_w", 2), ("ffn_conv_b", None), ("ffn_w_down", 1))
BIG = ("a_w_in", "a_w_out", "kv_w", "b_w_in", "b_w_out", "mem_w_kv", "ffn_w_up", "ffn_w_down")
AXIS = dict(PARAMS)
SMALL = tuple(n for n, _ in PARAMS if n not in BIG)
SMALL_SHARDED = tuple(n for n in SMALL if AXIS[n] is not None)
PACK_QUANTUM = 256 * 128


def _from_shards(xs, axis):
    full = jnp.moveaxis(xs, 0, axis)
    sh = full.shape
    return full.reshape(sh[:axis] + (sh[axis] * sh[axis + 1],) + sh[axis + 2:])


def _to_shards(g, axis):
    sh = g.shape
    return jnp.moveaxis(g.reshape(sh[:axis] + (N_DEV, sh[axis] // N_DEV) + sh[axis + 1:]), axis, 0)


def _pack(parts, lead=0):
    ld = parts[0].shape[:lead]
    flat = jnp.concatenate([p.reshape(ld + (-1,)) for p in parts], axis=-1)
    pad = (-flat.shape[-1]) % PACK_QUANTUM
    flat = jnp.pad(flat, [(0, 0)] * lead + [(0, pad)])
    return flat.reshape(ld + (-1, 128))


def _unpack(packed, shapes, lead=0):
    ld = packed.shape[:lead]
    flat = packed.reshape(ld + (-1,))
    out, off = [], 0
    for s in shapes:
        n = math.prod(s)
        out.append(flat[..., off:off + n].reshape(ld + tuple(s)))
        off += n
    return out


def kernel(x, mem, attn_norm, a_w_in, a_mu, a_w0, a_w2, a_a0, a_a2, a_g2, a_k_k, a_k_a, a_r_k, a_lnx_w, a_lnx_b, a_w_out, kv_norm, kv_w, kv_k_norm, b_w_in, b_q_norm, b_w_out, mem_norm, mem_w_kv, mem_q_norm, mem_k_norm, ffn_norm, ffn_w_up, ffn_conv_w, ffn_conv_b, ffn_w_down, loss_target, m_attn_norm, m_a_w_in, m_a_mu, m_a_w0, m_a_w2, m_a_a0, m_a_a2, m_a_g2, m_a_k_k, m_a_k_a, m_a_r_k, m_a_lnx_w, m_a_lnx_b, m_a_w_out, m_kv_norm, m_kv_w, m_kv_k_norm, m_b_w_in, m_b_q_norm, m_b_w_out, m_mem_norm, m_mem_w_kv, m_mem_q_norm, m_mem_k_norm, m_ffn_norm, m_ffn_w_up, m_ffn_conv_w, m_ffn_conv_b, m_ffn_w_down, v_attn_norm, v_a_w_in, v_a_mu, v_a_w0, v_a_w2, v_a_a0, v_a_a2, v_a_g2, v_a_k_k, v_a_k_a, v_a_r_k, v_a_lnx_w, v_a_lnx_b, v_a_w_out, v_kv_norm, v_kv_w, v_kv_k_norm, v_b_w_in, v_b_q_norm, v_b_w_out, v_mem_norm, v_mem_w_kv, v_mem_q_norm, v_mem_k_norm, v_ffn_norm, v_ffn_w_up, v_ffn_conv_w, v_ffn_conv_b, v_ffn_w_down):
    names = [n for n, _ in PARAMS]
    vals = (attn_norm, a_w_in, a_mu, a_w0, a_w2, a_a0, a_a2, a_g2, a_k_k, a_k_a, a_r_k, a_lnx_w, a_lnx_b, a_w_out, kv_norm, kv_w, kv_k_norm, b_w_in, b_q_norm, b_w_out, mem_norm, mem_w_kv, mem_q_norm, mem_k_norm, ffn_norm, ffn_w_up, ffn_conv_w, ffn_conv_b, ffn_w_down)
    m_vals = (m_attn_norm, m_a_w_in, m_a_mu, m_a_w0, m_a_w2, m_a_a0, m_a_a2, m_a_g2, m_a_k_k, m_a_k_a, m_a_r_k, m_a_lnx_w, m_a_lnx_b, m_a_w_out, m_kv_norm, m_kv_w, m_kv_k_norm, m_b_w_in, m_b_q_norm, m_b_w_out, m_mem_norm, m_mem_w_kv, m_mem_q_norm, m_mem_k_norm, m_ffn_norm, m_ffn_w_up, m_ffn_conv_w, m_ffn_conv_b, m_ffn_w_down)
    v_vals = (v_attn_norm, v_a_w_in, v_a_mu, v_a_w0, v_a_w2, v_a_a0, v_a_a2, v_a_g2, v_a_k_k, v_a_k_a, v_a_r_k, v_a_lnx_w, v_a_lnx_b, v_a_w_out, v_kv_norm, v_kv_w, v_kv_k_norm, v_b_w_in, v_b_q_norm, v_b_w_out, v_mem_norm, v_mem_w_kv, v_mem_q_norm, v_mem_k_norm, v_ffn_norm, v_ffn_w_up, v_ffn_conv_w, v_ffn_conv_b, v_ffn_w_down)
    W, M, V = dict(zip(names, vals)), dict(zip(names, m_vals)), dict(zip(names, v_vals))
    me = 4 * lax.axis_index("x") + 2 * lax.axis_index("y") + lax.axis_index("c")
    layers = lambda D, n: [D[n]] if D[n].ndim == 2 else [D[n][i] for i in range(D[n].shape[0])]
    ax2 = lambda n: AXIS[n] - (W[n].ndim - 2)
    first = [("a_w_in", 0)]
    later = [(n, i) for n in BIG if n != "a_w_in" for i in range(len(layers(W, n)))]

    small_shapes = [W[n].shape for n in SMALL_SHARDED]
    got_w, got_small = exchange(Exchange(gathers=[W["a_w_in"][0].astype(BF16), _pack([W[n] for n in SMALL_SHARDED])]),
                                "gather_first")
    P = {n: W[n] for n in SMALL}
    P["a_w_in"] = [_from_shards(got_w, ax2("a_w_in"))]
    for n, s in zip(SMALL_SHARDED, _unpack(got_small, small_shapes, lead=1)):
        P[n] = _from_shards(s, AXIS[n])
    ex_weights = Exchange(gathers=[layers(W, n)[i].astype(BF16) for n, i in later])

    def weights_done(got):
        out = {}
        for (n, _), g in zip(later, got):
            out.setdefault(n, []).append(_from_shards(g, ax2(n)))
        return out

    ex_grads = lambda G: Exchange(scatters=[_to_shards(G[n][i], ax2(n)) for n, i in later])
    loss_part, dx0, G = _local_step(x[0], mem[0], loss_target[0], P, ex_weights, weights_done, ex_grads)
    loss = lax.psum(loss_part[0, 0], ("x", "y", "c"))
    gparts = dict(zip(later, G.pop("_exchanged")))
    got_gsmall, gparts[first[0]] = exchange(
        Exchange(gathers=[_pack([G[n] for n in SMALL])], scatters=[_to_shards(G["a_w_in"][0], ax2("a_w_in"))]), "exchange_last")

    results = {}
    for n in BIG:
        per_layer = [adamw(gparts[(n, i)], w, m, v, f"adamw_{n}{i}")
                     for i, (w, m, v) in enumerate(zip(layers(W, n), layers(M, n), layers(V, n)))]
        results[n] = [r[0] if W[n].ndim == 2 else jnp.stack(r) for r in zip(*per_layer)]
    g_small = sum_parts(got_gsmall, "sum_small_grads")
    mine = []
    for n, g in zip(SMALL, _unpack(g_small, [G[n].shape for n in SMALL])):
        if AXIS[n] is not None:
            s = W[n].shape[AXIS[n]]
            g = lax.dynamic_slice_in_dim(g, me * s, s, axis=AXIS[n])
        mine.append(g)
    res = adamw(_pack(mine)[None], _pack([W[n] for n in SMALL]), _pack([M[n] for n in SMALL]), _pack([V[n] for n in SMALL]),
                "adamw_small")
    for n, parts in zip(SMALL, zip(*[_unpack(r, [W[n].shape for n in SMALL]) for r in res])):
        results[n] = list(parts)
    outs = [[results[n][j] for n in names] for j in range(4)]
    return (loss, dx0[None], *outs[0], *outs[1], *outs[2], *outs[3])
```

```python
import functools
import math

import jax
import jax.numpy as jnp
import numpy as np
from jax import lax
from jax.experimental import pallas as pl
from jax.experimental.pallas import tpu as pltpu

F32 = jnp.float32
BF16 = jnp.bfloat16
HI = lax.Precision.HIGHEST
H3 = lax.Precision.HIGH

N_DEV = 8
D_MODEL = 1024
HEAD_DIM = 64
N_MEM = 256
MEM_HEADS = 4
MEM_WIDTH = 256
RWKV_HEADS = 12
RWKV_WIDTH = 768
SHIFT_WIDTH = 2560
DIL_GROUPS = ((128, 1), (512, 4), (2048, 16))
DIL_BLOCK = 128
DIL_WIDTH = 768
D_FF = 2816
RMS_EPS = 1e-6
LNX_EPS = 64e-5
NEG_INF = -1e30
ROPE_THETA = 10000.0
ADAM_LR, ADAM_B1, ADAM_B2, ADAM_EPS, ADAM_WD, ADAM_STEP = 0.001, 0.9, 0.999, 1e-08, 0.01, 10

CHUNK = 64
SCAN_GROUPS_FWD, SCAN_GROUPS_BWD = 1, 1
MM_TILE_CAP = 1408
VMEM_LIMIT_V7X = 48 * 1024 * 1024


def _cparams(sem):
    return pltpu.CompilerParams(dimension_semantics=sem, vmem_limit_bytes=VMEM_LIMIT_V7X)


def _pick(n, cands):
    for c in cands:
        if n % c == 0:
            return c
    return n


def _tile(n, cap):
    if n <= cap:
        return n
    for d in range(cap - cap % 128, 0, -128):
        if n % d == 0:
            return d
    return n


def _dg(a, b, ca, cb, batch):
    dims = (((ca,), (cb,)), ((0,), (0,))) if batch else (((ca,), (cb,)), ((), ()))
    return lax.dot_general(a.astype(BF16), b.astype(BF16), dims, preferred_element_type=F32)


@jax.custom_vjp
def mm_nn(a, b):
    n = a.ndim
    return _dg(a, b, n - 1, n - 2, n == 3)


def _mm_nn_fwd(a, b):
    return mm_nn(a, b), (a, b)


def _mm_nn_bwd(res, g):
    a, b = res
    n = a.ndim
    return _dg(g, b, n - 1, n - 1, n == 3), _dg(a, g, n - 2, n - 2, n == 3)


mm_nn.defvjp(_mm_nn_fwd, _mm_nn_bwd)


@jax.custom_vjp
def mm_nt(a, b):
    n = a.ndim
    return _dg(a, b, n - 1, n - 1, n == 3)


def _mm_nt_fwd(a, b):
    return mm_nt(a, b), (a, b)


def _mm_nt_bwd(res, g):
    a, b = res
    n = a.ndim
    return _dg(g, b, n - 1, n - 2, n == 3), _dg(g, a, n - 2, n - 2, n == 3)


mm_nt.defvjp(_mm_nt_fwd, _mm_nt_bwd)


def mmh(a, b, precision=H3):
    n = a.ndim
    dims = (((n - 1,), (n - 2,)), ((0,), (0,))) if n == 3 else (((1,), (0,)), ((), ()))
    return lax.dot_general(a, b, dims, precision=precision, preferred_element_type=F32)


def mmh_nt(a, b):
    n = a.ndim
    dims = (((n - 1,), (n - 1,)), ((0,), (0,))) if n == 3 else (((1,), (1,)), ((), ()))
    return lax.dot_general(a, b, dims, precision=H3, preferred_element_type=F32)


def mmh_tn(a, b):
    n = a.ndim
    dims = (((n - 2,), (n - 2,)), ((0,), (0,))) if n == 3 else (((0,), (0,)), ((), ()))
    return lax.dot_general(a, b, dims, precision=H3, preferred_element_type=F32)


def matmul(a, b, mode, name, residual=None):
    out_dtype = BF16 if mode == "tn" else F32
    if mode == "nn":
        (M, K), (_, N) = a.shape, b.shape
    elif mode == "nt":
        (M, K), (N, _) = a.shape, b.shape
    else:
        (K, M), (_, N) = a.shape, b.shape
    tm = _tile(M, 2048 if mode == "nn" else MM_TILE_CAP)
    tn = _tile(N, 512 if mode == "nn" else MM_TILE_CAP)
    tk = _tile(K, MM_TILE_CAP if mode != "nt" else 512)
    nk = K // tk
    if mode == "nn":
        a_spec = pl.BlockSpec((tm, tk), lambda i, j, k: (i, k))
        b_spec = pl.BlockSpec((tk, tn), lambda i, j, k: (k, j))
        dims = (((1,), (0,)), ((), ()))
    elif mode == "nt":
        a_spec = pl.BlockSpec((tm, tk), lambda i, j, k: (i, k))
        b_spec = pl.BlockSpec((tn, tk), lambda i, j, k: (j, k))
        dims = (((1,), (1,)), ((), ()))
    else:
        a_spec = pl.BlockSpec((tk, tm), lambda i, j, k: (k, i))
        b_spec = pl.BlockSpec((tk, tn), lambda i, j, k: (k, j))
        dims = (((0,), (0,)), ((), ()))
    o_spec = pl.BlockSpec((tm, tn), lambda i, j, k: (i, j))
    has_res = residual is not None

    def body(*refs):
        if has_res:
            a_ref, b_ref, r_ref, o_ref, acc_ref = refs
        else:
            a_ref, b_ref, o_ref, acc_ref = refs
        k = pl.program_id(2)

        @pl.when(k == 0)
        def _():
            acc_ref[...] = jnp.zeros_like(acc_ref)

        acc_ref[...] += lax.dot_general(a_ref[...].astype(BF16), b_ref[...].astype(BF16), dims,
                                        preferred_element_type=F32)

        @pl.when(k == nk - 1)
        def _():
            if has_res:
                o_ref[...] = (acc_ref[...] + r_ref[...]).astype(out_dtype)
            else:
                o_ref[...] = acc_ref[...].astype(out_dtype)

    ins = [a, b] + ([residual] if has_res else [])
    in_specs = [a_spec, b_spec] + ([o_spec] if has_res else [])
    return pl.pallas_call(
        body, grid=(M // tm, N // tn, nk), in_specs=in_specs, out_specs=o_spec,
        out_shape=jax.ShapeDtypeStruct((M, N), out_dtype), scratch_shapes=[pltpu.VMEM((tm, tn), F32)],
        compiler_params=_cparams(("parallel", "parallel", "arbitrary")), name=name)(*ins)


class Cols:
    def __init__(self, arr, width, idx):
        self.arr, self.width, self.idx = arr, width, idx


def _arr(x):
    return x.arr if isinstance(x, Cols) else x


def _shape(x):
    return x.arr.shape[:-1] + (x.width,) if isinstance(x, Cols) else x.shape


def _col(x):
    return x.idx if isinstance(x, Cols) else 0


def _tok_spec(x, tile):
    shape, col = _shape(x), _col(x)
    return pl.BlockSpec(shape[:-2] + (tile, shape[-1]), lambda i: (0,) * (len(shape) - 2) + (i, col))


def _full_spec(x):
    shape, col = _shape(x), _col(x)
    return pl.BlockSpec(shape, lambda i: (0,) * (len(shape) - 1) + (col,))


def _halo_spec(x, tile):
    shape, col = _shape(x), _col(x)
    return pl.BlockSpec((8, shape[-1]), lambda i: (jnp.maximum(i * (tile // 8) - 1, 0), col))


def _blk(x, tile):
    shape = _shape(x)
    return jax.ShapeDtypeStruct(shape[:-2] + (tile, shape[-1]), _arr(x).dtype)


def _prev_rows(x, halo):
    rows = lax.broadcasted_iota(jnp.int32, (x.shape[0], 1), 0)
    before = jnp.where(pl.program_id(0) > 0, halo[7:8], 0.0)
    return jnp.where(rows == 0, before, pltpu.roll(x, 1, 0))


def stage_fwd(f, xs, ps, cts, cfs, tile, name, out_dtypes=None, with_prev=False):
    xs, ps, cts, cfs = list(xs), list(ps), list(cts), list(cfs)
    halos = xs if with_prev else []
    nx, nh, nct, np_ = len(xs), len(halos), len(cts), len(ps)
    T = _shape(xs[0])[-2]
    blk = [_blk(x, tile) for x in xs]
    out_avals = jax.eval_shape(f, *blk, *(blk if with_prev else []), *[_blk(p, _shape(p)[-2]) for p in ps],
                               *[_blk(c, tile) for c in cts], *[_blk(c, _shape(c)[-2]) for c in cfs])
    if out_dtypes is None:
        out_dtypes = [o.dtype for o in out_avals]
    out_shape = [jax.ShapeDtypeStruct(o.shape[:-2] + (T, o.shape[-1]), dt) for o, dt in zip(out_avals, out_dtypes)]
    n_in = nx + nh + nct + np_ + len(cfs)

    def body(*refs):
        vals = [r[...] for r in refs[:n_in]]
        xv, hv, rest = vals[:nx], vals[nx:nx + nh], vals[nx + nh:]
        ctv, pv, cfv = rest[:nct], rest[nct:nct + np_], rest[nct + np_:]
        prev = [_prev_rows(x, h) for x, h in zip(xv, hv)]
        res = f(*xv, *prev, *pv, *ctv, *cfv)
        for o_ref, r in zip(refs[n_in:], res):
            o_ref[...] = r.astype(o_ref.dtype)

    return pl.pallas_call(
        body, grid=(T // tile,),
        in_specs=([_tok_spec(x, tile) for x in xs] + [_halo_spec(x, tile) for x in halos] + [_tok_spec(c, tile) for c in cts]
                  + [_full_spec(p) for p in ps + cfs]),
        out_specs=[_tok_spec(o, tile) for o in out_shape], out_shape=out_shape,
        compiler_params=_cparams(("parallel",)), name=name)(*[_arr(a) for a in xs + halos + cts + ps + cfs])


def stage_bwd(f, xs, ps, cts, cfs, gs, tile, name, bf16_copies=(), with_prev=False):
    xs, ps, cts, cfs = list(xs), list(ps), list(cts), list(cfs)
    gs = [list(g) if isinstance(g, (list, tuple)) else [g] for g in gs]
    g_flat = [a for g in gs for a in g]
    halos = xs if with_prev else []
    nx, nh, nct, ng, np_ = len(xs), len(halos), len(cts), len(g_flat), len(ps)
    T = _shape(xs[0])[-2]
    dx_like = xs + halos
    out_shape = ([jax.ShapeDtypeStruct(_shape(x), F32) for x in dx_like] + [jax.ShapeDtypeStruct(_shape(p), F32) for p in ps]
                 + [jax.ShapeDtypeStruct(_shape(xs[i]), BF16) for i in bf16_copies])
    n_in = nx + nh + nct + ng + np_ + len(cfs)
    ndx = nx + nh

    def body(*refs):
        vals = [r[...] for r in refs[:n_in]]
        outs = refs[n_in:]
        xv, hv, rest = vals[:nx], vals[nx:nx + nh], vals[nx + nh:]
        ctv, gparts, pv, cfv = rest[:nct], rest[nct:nct + ng], rest[nct + ng:nct + ng + np_], rest[nct + ng + np_:]
        gv = []
        for g in gs:
            gv.append(functools.reduce(lambda a, b: a + b, gparts[:len(g)]))
            gparts = gparts[len(g):]
        prev = [_prev_rows(x, h) for x, h in zip(xv, hv)]
        _, vjp = jax.vjp(lambda *xp: f(*xp, *ctv, *cfv), *xv, *prev, *pv)
        d = vjp(tuple(gv))
        for o_ref, r in zip(outs[:ndx], d[:ndx]):
            o_ref[...] = r
        for o_ref, i in zip(outs[ndx + np_:], bf16_copies):
            o_ref[...] = d[i].astype(BF16)

        @pl.when(pl.program_id(0) == 0)
        def _():
            for o_ref in outs[ndx:ndx + np_]:
                o_ref[...] = jnp.zeros_like(o_ref)

        for o_ref, r in zip(outs[ndx:ndx + np_], d[ndx:]):
            o_ref[...] += r

    plain = lambda x: jax.ShapeDtypeStruct(_shape(x), F32)
    res = pl.pallas_call(
        body, grid=(T // tile,),
        in_specs=([_tok_spec(x, tile) for x in xs] + [_halo_spec(x, tile) for x in halos]
                  + [_tok_spec(c, tile) for c in cts + g_flat] + [_full_spec(p) for p in ps + cfs]),
        out_specs=([_tok_spec(plain(x), tile) for x in dx_like] + [_full_spec(plain(p)) for p in ps]
                   + [_tok_spec(plain(xs[i]), tile) for i in bf16_copies]), out_shape=out_shape,
        compiler_params=_cparams(("arbitrary",)), name=name)(*[_arr(a) for a in xs + halos + cts + g_flat + ps + cfs])
    if bf16_copies:
        return list(res[:ndx]), list(res[ndx:ndx + np_]), list(res[ndx + np_:])
    return list(res[:ndx]), list(res[ndx:])


def _rms(x, g, eps=RMS_EPS):
    return x * lax.rsqrt(jnp.mean(x * x, axis=-1, keepdims=True) + eps) * g


def f_rmsnorm(x, g):
    return (_rms(x, g),)


def f_rmsnorm_res(x, g):
    return _rms(x, g), x


def f_rmsnorm2(x, g1, g2):
    n = x * lax.rsqrt(jnp.mean(x * x, axis=-1, keepdims=True) + RMS_EPS)
    return n * g1, n * g2


def f_rmsnorm2_res(x, g1, g2):
    return f_rmsnorm2(x, g1, g2) + (x,)


def _sigmoid(x):
    return 1.0 / (1.0 + jnp.exp(-x))


def _softplus(x):
    return jnp.maximum(x, 0.0) + jnp.log(1.0 + jnp.exp(-jnp.abs(x)))


def f_rwkv_pre(pr, pk, pv, pl_, qr, qk, qv, ql, mu_r, mu_k, mu_v, mu_l, w0, w2, a0, a2, g2, k_k, k_a, seg, seg_t):
    xr = pr + (qr - pr) * mu_r
    xk = pk + (qk - pk) * mu_k
    xv = pv + (qv - pv) * mu_v
    xl = pl_ + (ql - pl_) * mu_l
    w_log = -_softplus(-(w0 + mm_nn(jnp.tanh(xl), w2))) - 0.5
    lw = -jnp.exp(w_log)
    a = _sigmoid(a0 + mm_nn(xl, a2))
    g = mm_nn(_sigmoid(xl), g2)
    kkr = xk * k_k
    inv = lax.rsqrt(jnp.maximum(mmh(kkr * kkr, seg), 1e-24))
    kk = kkr * mmh(inv, seg_t)
    k2 = xk * (1.0 + (a - 1.0) * k_a)
    return xr, lw, k2, xv, kk, kk * a, g


def f_rwkv_post(y, r, k2, v, g, lnx_w, lnx_b, r_k, seg, seg_t):
    inv_n = 1.0 / HEAD_DIM
    m = mmh(mmh(y, seg) * inv_n, seg_t)
    yc = y - m
    rstd = lax.rsqrt(mmh(yc * yc, seg) * inv_n + LNX_EPS)
    yn = yc * mmh(rstd, seg_t) * lnx_w + lnx_b
    bonus = mmh(mmh(r * k2 * r_k, seg), seg_t) * v
    return ((yn + bonus) * g,)


def _headnorm(z, g, seg, seg_t):
    ms = mmh(z * z, seg) * (1.0 / HEAD_DIM)
    return z * mmh(lax.rsqrt(ms + RMS_EPS), seg_t) * g


def f_headnorm(z, g, seg, seg_t):
    return (_headnorm(z, g, seg, seg_t),)


def f_qkprep(z, g, cos, sin, seg, seg_t, rot):
    zn = _headnorm(z, g, seg, seg_t)
    return (zn * cos + mmh(zn, rot) * sin,)


def _head_mask(width, h):
    lane = lax.broadcasted_iota(jnp.int32, (1, width), 1)
    return jnp.where((lane >> 6) == h, jnp.ones((), F32), 0.0)


def f_memattn(q, k, v, q_norm, seg, seg_t):
    qn = _headnorm(q, q_norm, seg, seg_t)
    out = jnp.zeros_like(q)
    for h in range(MEM_HEADS):
        m = _head_mask(MEM_WIDTH, h)
        s = mm_nt(qn * m, k) * (1.0 / math.sqrt(HEAD_DIM))
        s = s - jnp.max(s, axis=-1, keepdims=True)
        p = jnp.exp(s)
        p = p / jnp.sum(p, axis=-1, keepdims=True)
        out = out + mm_nn(p, v) * m
    return (out,)


def f_mix(o1, o2, o3, l1, l2, l3):
    mx = jnp.maximum(jnp.maximum(l1, l2), l3)
    e1, e2, e3 = jnp.exp(l1 - mx), jnp.exp(l2 - mx), jnp.exp(l3 - mx)
    return ((e1 * o1 + e2 * o2 + e3 * o3) / (e1 + e2 + e3),)


def _chunk_masks(L):
    t = lax.broadcasted_iota(jnp.int32, (L, L), 0)
    s = lax.broadcasted_iota(jnp.int32, (L, L), 1)
    return t, s


def _unit_lower_inverse(a):
    L = a.shape[-1]
    t, s = _chunk_masks(L)
    one = jnp.ones((), F32)
    blk = lambda sh: jnp.where((t >> sh) == (s >> sh), one, 0.0)
    n0 = a * blk(3)
    x = jnp.where(t == s, one, 0.0) - n0
    n2 = mmh(n0, n0)
    x = x + mmh(x, n2)
    x = x + mmh(x, mmh(n2, n2))
    for sh in (3, 4, 5):
        if (1 << sh) >= L:
            break
        off = a * (blk(sh + 1) - blk(sh))
        x = x - mmh(x, mmh(off, x))
    return x


@jax.custom_vjp
def _inverse_known(a, x):
    return x


def _inverse_known_fwd(a, x):
    return x, x


def _inverse_known_bwd(x, dx):
    return -mmh_nt(mmh_tn(x, dx), x), jnp.zeros_like(x)


_inverse_known.defvjp(_inverse_known_fwd, _inverse_known_bwd)


def _running_sum(x, reverse):
    L = x.shape[1]
    pos = lax.broadcasted_iota(jnp.int32, (1, L, 1), 1)
    step = 1
    while step < L:
        if reverse:
            x = x + jnp.where(pos < L - step, pltpu.roll(x, L - step, 1), 0.0)
        else:
            x = x + jnp.where(pos >= step, pltpu.roll(x, step, 1), 0.0)
        step *= 2
    return x


@jax.custom_vjp
def _cumsum_tokens(x):
    return _running_sum(x, False)


_cumsum_tokens.defvjp(lambda x: (_running_sum(x, False), None), lambda _, g: (_running_sum(g, True),))


def f_rwkv_chunk(s0, r, lw, k, v, kk, b, x_known=None):
    H, L, _ = r.shape
    t, s = _chunk_masks(L)
    one = jnp.ones((), F32)
    incl = jnp.where(t >= s, one, 0.0)
    strict = jnp.where(t > s, one, 0.0)
    cum = _cumsum_tokens(lw)
    w_in = jnp.exp(cum)
    w_ex = jnp.exp(cum - lw)
    w_inv = jnp.exp(-cum)
    rt, kkt, kt, bt = r * w_in, kk * w_ex, k * w_inv, b * w_inv
    a_b = mmh_nt(kkt, bt) * strict
    a_k = mmh_nt(kkt, kt) * strict
    m_k = mmh_nt(rt, kt) * incl
    m_b = mmh_nt(rt, bt) * incl
    x = _unit_lower_inverse(a_b) if x_known is None else _inverse_known(a_b, x_known)
    u = mmh(x, mmh_nt(kkt, s0) + mmh(a_k, v))
    y = mmh_nt(rt, s0) + mmh(m_k, v) - mmh(m_b, u)
    w_last = jnp.exp(jnp.sum(lw, axis=1, keepdims=True))
    s1 = (s0 + mmh_tn(v, kt) - mmh_tn(u, bt)) * w_last
    return y, s1, x


def _ex_split(ex, refs, n_in, n_out):
    n = ex.n
    ins, ex_in = refs[:n_in], refs[n_in:n_in + n]
    outs, ex_out = refs[n_in + n:n_in + n + n_out], refs[n_in + n + n_out:n_in + 2 * n + n_out]
    rest = refs[n_in + 2 * n + n_out:]
    return ins, outs, rest[:len(rest) - 3], (ex_in, ex_out) + tuple(rest[len(rest) - 3:])


def rwkv_scan_fwd(r, lw, k, v, kk, b, ex):
    H, T, N = r.shape
    groups = SCAN_GROUPS_FWD
    nc, hg = T // CHUNK, H // groups
    seq = pl.BlockSpec((hg, CHUNK, N), lambda g, c: (g, c, 0))

    def body(*refs):
        (r_ref, lw_ref, k_ref, v_ref, kk_ref, b_ref), (y_ref, hs_ref, xs_ref), (h_scr,), ex_refs = _ex_split(ex, refs, 6, 3)
        g, c = pl.program_id(0), pl.program_id(1)

        @pl.when(jnp.logical_and(g == 0, c == 0))
        def _():
            ex.start(*ex_refs)

        @pl.when(c == 0)
        def _():
            h_scr[...] = jnp.zeros_like(h_scr)

        h0 = h_scr[...]
        hs_ref[0] = h0
        y, h1, x = f_rwkv_chunk(h0, r_ref[...], lw_ref[...], k_ref[...], v_ref[...], kk_ref[...], b_ref[...])
        y_ref[...] = y
        xs_ref[0] = x
        h_scr[...] = h1

        @pl.when(jnp.logical_and(g == groups - 1, c == (3 * nc) // 4))
        def _():
            ex.forward(*ex_refs)

        @pl.when(jnp.logical_and(g == groups - 1, c == nc - 1))
        def _():
            ex.wait(*ex_refs)

    res = pl.pallas_call(
        body, grid=(groups, nc), in_specs=[seq] * 6 + [_ANY] * ex.n,
        out_specs=[seq, pl.BlockSpec((1, hg, N, N), lambda g, c: (c, g, 0, 0)),
                   pl.BlockSpec((1, hg, CHUNK, CHUNK), lambda g, c: (c, g, 0, 0))] + [_ANY] * ex.n,
        out_shape=[jax.ShapeDtypeStruct((H, T, N), F32), jax.ShapeDtypeStruct((nc, H, N, N), F32),
                   jax.ShapeDtypeStruct((nc, H, CHUNK, CHUNK), F32)] + ex.out_shape(),
        scratch_shapes=[pltpu.VMEM((hg, N, N), F32)] + ex.scratch(),
        compiler_params=_cparams(("arbitrary", "arbitrary")), name="rwkv_scan_fwd")(r, lw, k, v, kk, b, *ex.operands())
    return res[0], (res[1], res[2]), list(res[3:])


def rwkv_scan_bwd(r, lw, k, v, kk, b, saved, dy, ex):
    H, T, N = r.shape
    groups = SCAN_GROUPS_BWD
    nc, hg = T // CHUNK, H // groups
    seq = pl.BlockSpec((hg, CHUNK, N), lambda g, c: (g, nc - 1 - c, 0))
    state = pl.BlockSpec((1, hg, N, N), lambda g, c: (nc - 1 - c, g, 0, 0))

    def body(*refs):
        (r_ref, lw_ref, k_ref, v_ref, kk_ref, b_ref, hs_ref, xs_ref, dy_ref), outs, (dh_scr,), ex_refs = _ex_split(ex, refs, 9, 6)
        g, c = pl.program_id(0), pl.program_id(1)

        @pl.when(jnp.logical_and(g == 0, c == 0))
        def _():
            ex.start(*ex_refs)

        @pl.when(c == 0)
        def _():
            dh_scr[...] = jnp.zeros_like(dh_scr)

        x_known = xs_ref[0]
        _, vjp = jax.vjp(lambda *a: f_rwkv_chunk(*a, x_known=x_known)[:2], hs_ref[0], r_ref[...], lw_ref[...], k_ref[...],
                         v_ref[...], kk_ref[...], b_ref[...])
        d = vjp((dy_ref[...], dh_scr[...]))
        dh_scr[...] = d[0]
        for o_ref, dz in zip(outs, d[1:]):
            o_ref[...] = dz

        @pl.when(jnp.logical_and(g == groups - 1, c == nc - 1))
        def _():
            ex.forward(*ex_refs)
            ex.wait(*ex_refs)

    res = pl.pallas_call(
        body, grid=(groups, nc),
        in_specs=[seq] * 6 + [state, state, seq] + [_ANY] * ex.n,
        out_specs=[seq] * 6 + [_ANY] * ex.n, out_shape=[jax.ShapeDtypeStruct((H, T, N), F32)] * 6 + ex.out_shape(),
        scratch_shapes=[pltpu.VMEM((hg, N, N), F32)] + ex.scratch(),
        compiler_params=_cparams(("arbitrary", "arbitrary")), name="rwkv_scan_bwd")(r, lw, k, v, kk, b, *saved, dy, *ex.operands())
    return list(res[:6]), list(res[6:])


PAIR = 2 * HEAD_DIM


def _f_dilattn(has_prev, q, kc, kp, vc, vp):
    scale = 1.0 / math.sqrt(HEAD_DIM)
    i = lax.broadcasted_iota(jnp.int32, (DIL_BLOCK, DIL_BLOCK), 0)
    j = lax.broadcasted_iota(jnp.int32, (DIL_BLOCK, DIL_BLOCK), 1)
    o, l = jnp.zeros_like(q), jnp.zeros_like(q)
    for h in range(2):
        m = _head_mask(PAIR, h)
        sc = jnp.where(j <= i, mm_nt(q * m, kc) * scale, NEG_INF)
        sp = jnp.where(jnp.logical_and(i <= j, has_prev), mm_nt(q * m, kp) * scale, NEG_INF)
        mx = jnp.maximum(jnp.max(sc, axis=-1, keepdims=True), jnp.max(sp, axis=-1, keepdims=True))
        pc, pp = jnp.exp(sc - mx), jnp.exp(sp - mx)
        den = jnp.sum(pc, axis=-1, keepdims=True) + jnp.sum(pp, axis=-1, keepdims=True)
        o = o + (mm_nn(pc, vc) + mm_nn(pp, vp)) / den * m
        l = l + (mx + jnp.log(den)) * m
    return o, l


def _dil_specs(gi, d, q_cols, k_cols, v_cols, v_off):
    blk = (DIL_BLOCK, PAIR)
    at = lambda cols, off: (lambda r, p, n: (n, r * (cols // PAIR) + off + p))
    before = lambda cols, off: (lambda r, p, n: (jnp.maximum(n - 1, 0), r * (cols // PAIR) + off + p))
    pair0 = 2 * gi
    q = pl.BlockSpec(blk, at(q_cols, pair0))
    kc, kp = pl.BlockSpec(blk, at(k_cols, pair0)), pl.BlockSpec(blk, before(k_cols, pair0))
    vc, vp = pl.BlockSpec(blk, at(v_cols, v_off + pair0)), pl.BlockSpec(blk, before(v_cols, v_off + pair0))
    out = pl.BlockSpec(blk, at(4 * HEAD_DIM, 0))
    return q, kc, kp, vc, vp, out


def _dil_views(q, k, kv, d):
    T = q.shape[0]
    view = lambda z: z.reshape(T // d, d * z.shape[1])
    return view(q), view(k), view(kv), (d, 2, T // d // DIL_BLOCK)


def dil_fwd(q, k, kv, gi, d, name):
    T = q.shape[0]
    qv, kv_, vv, grid = _dil_views(q, k, kv, d)
    qs, kc, kp, vc, vp, out = _dil_specs(gi, d, DIL_WIDTH, DIL_WIDTH, 2 * DIL_WIDTH, DIL_WIDTH // PAIR)

    def body(q_ref, kc_ref, kp_ref, vc_ref, vp_ref, o_ref, l_ref):
        o, l = _f_dilattn(pl.program_id(2) > 0, q_ref[...], kc_ref[...], kp_ref[...], vc_ref[...], vp_ref[...])
        o_ref[...] = o
        l_ref[...] = l

    shape = jax.ShapeDtypeStruct((T // d, d * 4 * HEAD_DIM), F32)
    o, l = pl.pallas_call(
        body, grid=grid, in_specs=[qs, kc, kp, vc, vp], out_specs=[out, out], out_shape=[shape, shape],
        compiler_params=_cparams(("parallel", "parallel", "parallel")), name=name)(qv, kv_, kv_, vv, vv)
    return o.reshape(T, 4 * HEAD_DIM), l.reshape(T, 4 * HEAD_DIM)


def dil_bwd(q, k, kv, do, dl, gi, d, name):
    T = q.shape[0]
    qv, kv_, vv, grid = _dil_views(q, k, kv, d)
    qs, kc, kp, vc, vp, out = _dil_specs(gi, d, DIL_WIDTH, DIL_WIDTH, 2 * DIL_WIDTH, DIL_WIDTH // PAIR)
    view = lambda z: z.reshape(T // d, d * z.shape[1])

    def body(q_ref, kc_ref, kp_ref, vc_ref, vp_ref, do_ref, dl_ref, *outs):
        f = functools.partial(_f_dilattn, pl.program_id(2) > 0)
        _, vjp = jax.vjp(f, q_ref[...], kc_ref[...], kp_ref[...], vc_ref[...], vp_ref[...])
        for o_ref, g in zip(outs, vjp((do_ref[...], dl_ref[...]))):
            o_ref[...] = g

    shape = jax.ShapeDtypeStruct((T // d, d * 4 * HEAD_DIM), F32)
    dq, dkc, dkp, dvc, dvp = pl.pallas_call(
        body, grid=grid, in_specs=[qs, kc, kp, vc, vp, out, out], out_specs=[out] * 5, out_shape=[shape] * 5,
        compiler_params=_cparams(("parallel", "parallel", "parallel")), name=name)(qv, kv_, kv_, vv, vv, view(do), view(dl))

    def own_plus_next(c, p):
        return (c + jnp.concatenate([p[DIL_BLOCK:], jnp.zeros_like(p[:DIL_BLOCK])], axis=0)).reshape(T, 4 * HEAD_DIM)

    return dq.reshape(T, 4 * HEAD_DIM), own_plus_next(dkc, dkp), own_plus_next(dvc, dvp)


CONV_TILE = 128


def _conv3(u, h6, h7, w, b):
    rows = lax.broadcasted_iota(jnp.int32, (u.shape[0], 1), 0)
    s1 = jnp.where(rows == 0, h7, pltpu.roll(u, 1, 0))
    s2 = jnp.where(rows == 0, h6, jnp.where(rows == 1, h7, pltpu.roll(u, 2, 0)))
    return b + w[0:1] * s2 + w[1:2] * s1 + w[2:3] * u, s1, s2


def _conv_halves(u_ref, h_ref, cw_ref, cb_ref):
    F = D_FF
    first = pl.program_id(0) > 0
    res = []
    for lo in (0, F):
        h = h_ref[:, lo:lo + F]
        h6 = jnp.where(first, h[6:7], 0.0)
        h7 = jnp.where(first, h[7:8], 0.0)
        u = u_ref[:, lo:lo + F]
        res.append((u,) + _conv3(u, h6, h7, cw_ref[:, lo:lo + F], cb_ref[:, lo:lo + F]))
    return res


def _halo_before(C):
    return pl.BlockSpec((8, C), lambda i: (jnp.maximum(i * (CONV_TILE // 8) - 1, 0), 0))


def convgate_fwd(u, cw, cb, name):
    T, C = u.shape
    F = C // 2

    def body(u_ref, h_ref, cw_ref, cb_ref, z_ref):
        (_, cg, _, _), (_, cv, _, _) = _conv_halves(u_ref, h_ref, cw_ref, cb_ref)
        z_ref[...] = (cg * _sigmoid(cg) * cv).astype(BF16)

    return pl.pallas_call(
        body, grid=(T // CONV_TILE,),
        in_specs=[pl.BlockSpec((CONV_TILE, C), lambda i: (i, 0)), _halo_before(C), _full_spec(cw), _full_spec(cb)],
        out_specs=pl.BlockSpec((CONV_TILE, F), lambda i: (i, 0)), out_shape=jax.ShapeDtypeStruct((T, F), BF16),
        compiler_params=_cparams(("parallel",)), name=name)(u, u, cw, cb)


def convgate_bwd_c(u, cw, cb, dz, name):
    T, C = u.shape
    F = C // 2

    def body(u_ref, h_ref, cw_ref, cb_ref, dz_ref, dc_ref, dcw_ref, dcb_ref):
        (ug, cg, g1, g2), (uv, cv, v1, v2) = _conv_halves(u_ref, h_ref, cw_ref, cb_ref)
        dz = dz_ref[...]
        sg = _sigmoid(cg)
        dgate = dz * cv * sg * (1.0 + cg * (1.0 - sg))
        dval = dz * cg * sg
        dc_ref[:, :F] = dgate
        dc_ref[:, F:] = dval

        @pl.when(pl.program_id(0) == 0)
        def _():
            dcw_ref[...] = jnp.zeros_like(dcw_ref)
            dcb_ref[...] = jnp.zeros_like(dcb_ref)

        for lo, d, u0, s1, s2 in ((0, dgate, ug, g1, g2), (F, dval, uv, v1, v2)):
            dcb_ref[:, lo:lo + F] += jnp.sum(d, axis=0, keepdims=True)
            dcw_ref[0:1, lo:lo + F] += jnp.sum(d * s2, axis=0, keepdims=True)
            dcw_ref[1:2, lo:lo + F] += jnp.sum(d * s1, axis=0, keepdims=True)
            dcw_ref[2:3, lo:lo + F] += jnp.sum(d * u0, axis=0, keepdims=True)

    return pl.pallas_call(
        body, grid=(T // CONV_TILE,),
        in_specs=[pl.BlockSpec((CONV_TILE, C), lambda i: (i, 0)), _halo_before(C), _full_spec(cw), _full_spec(cb),
                  pl.BlockSpec((CONV_TILE, F), lambda i: (i, 0))],
        out_specs=[pl.BlockSpec((CONV_TILE, C), lambda i: (i, 0)), _full_spec(cw), _full_spec(cb)],
        out_shape=[jax.ShapeDtypeStruct((T, C), F32), jax.ShapeDtypeStruct(cw.shape, F32), jax.ShapeDtypeStruct(cb.shape, F32)],
        compiler_params=_cparams(("arbitrary",)), name=name)(u, u, cw, cb, dz)


def convgate_bwd_u(dc, cw, name):
    T, C = dc.shape
    n = T // CONV_TILE

    def body(dc_ref, nx_ref, cw_ref, du_ref):
        last = pl.program_id(0) < n - 1
        rows = lax.broadcasted_iota(jnp.int32, (CONV_TILE, 1), 0)
        for lo in (0, C // 2):
            sl = slice(lo, lo + C // 2)
            d = dc_ref[:, sl]
            n0 = jnp.where(last, nx_ref[0:1, sl], 0.0)
            n1 = jnp.where(last, nx_ref[1:2, sl], 0.0)
            up1 = jnp.where(rows == CONV_TILE - 1, n0, pltpu.roll(d, CONV_TILE - 1, 0))
            up2 = jnp.where(rows == CONV_TILE - 1, n1, jnp.where(rows == CONV_TILE - 2, n0, pltpu.roll(d, CONV_TILE - 2, 0)))
            du_ref[:, sl] = (cw_ref[2:3, sl] * d + cw_ref[1:2, sl] * up1 + cw_ref[0:1, sl] * up2).astype(BF16)

    nxt = pl.BlockSpec((8, C), lambda i: (jnp.minimum((i + 1) * (CONV_TILE // 8), T // 8 - 1), 0))
    return pl.pallas_call(
        body, grid=(n,), in_specs=[pl.BlockSpec((CONV_TILE, C), lambda i: (i, 0)), nxt, _full_spec(cw)],
        out_specs=pl.BlockSpec((CONV_TILE, C), lambda i: (i, 0)), out_shape=jax.ShapeDtypeStruct((T, C), BF16),
        compiler_params=_cparams(("parallel",)), name=name)(dc, dc, cw)


def loss_head(y, tgt):
    T, D = y.shape
    tile = 256

    def body(y_ref, t_ref, l_ref, d_ref, db_ref):
        d = y_ref[...] - t_ref[...]
        d_ref[...] = d * (1.0 / D)
        db_ref[...] = (d * (1.0 / D)).astype(BF16)

        @pl.when(pl.program_id(0) == 0)
        def _():
            l_ref[...] = jnp.zeros_like(l_ref)

        l_ref[...] += (0.5 / D) * jnp.sum(d * d)

    row = pl.BlockSpec((tile, D), lambda i: (i, 0))
    return pl.pallas_call(
        body, grid=(T // tile,), in_specs=[row, row], out_specs=[pl.BlockSpec((8, 128), lambda i: (0, 0)), row, row],
        out_shape=[jax.ShapeDtypeStruct((8, 128), F32), jax.ShapeDtypeStruct((T, D), F32), jax.ShapeDtypeStruct((T, D), BF16)],
        compiler_params=_cparams(("arbitrary",)), name="loss_head")(y, tgt)


def sum_parts(parts, name):
    S, R, C = parts.shape
    tile = _pick(R, (256, 128, 64, 32, 16, 8))

    def body(p_ref, o_ref):
        acc = p_ref[0]
        for s in range(1, S):
            acc = acc + p_ref[s]
        o_ref[...] = acc

    return pl.pallas_call(
        body, grid=(R // tile,), in_specs=[pl.BlockSpec((S, tile, C), lambda i: (0, i, 0))],
        out_specs=pl.BlockSpec((tile, C), lambda i: (i, 0)), out_shape=jax.ShapeDtypeStruct((R, C), F32),
        compiler_params=_cparams(("parallel",)), name=name)(parts)


def adamw(gparts, w, m, v, name):
    S, R, C = gparts.shape
    tile = _pick(R, (256, 128, 64, 32, 16, 8))
    c1 = 1.0 / (1.0 - ADAM_B1 ** ADAM_STEP)
    c2 = 1.0 / (1.0 - ADAM_B2 ** ADAM_STEP)

    def body(g_ref, w_ref, m_ref, v_ref, go_ref, d_ref, mo_ref, vo_ref):
        g = g_ref[0].astype(F32)
        for s in range(1, S):
            g = g + g_ref[s].astype(F32)
        m1 = ADAM_B1 * m_ref[...] + (1.0 - ADAM_B1) * g
        v1 = ADAM_B2 * v_ref[...] + (1.0 - ADAM_B2) * (g * g)
        go_ref[...] = g
        mo_ref[...] = m1
        vo_ref[...] = v1
        d_ref[...] = -ADAM_LR * ((m1 * c1) / (jnp.sqrt(v1 * c2) + ADAM_EPS) + ADAM_WD * w_ref[...])

    row = pl.BlockSpec((tile, C), lambda i: (i, 0))
    return pl.pallas_call(
        body, grid=(R // tile,), in_specs=[pl.BlockSpec((S, tile, C), lambda i: (0, i, 0)), row, row, row],
        out_specs=[row] * 4, out_shape=[jax.ShapeDtypeStruct((R, C), F32)] * 4,
        compiler_params=_cparams(("parallel",)), name=name)(gparts, w, m, v)


def _peers():
    x, y, c = lax.axis_index("x"), lax.axis_index("y"), lax.axis_index("c")
    peers = []
    for k in range(1, N_DEV):
        px = 1 - x if k & 4 else x
        py = 1 - y if k & 2 else y
        pc = 1 - c if k & 1 else c
        peers.append(((px, py, pc), 4 * px + 2 * py + pc))
    return 4 * x + 2 * y + c, peers


_ANY = pl.BlockSpec(memory_space=pl.ANY)


class Exchange:
    def __init__(self, gathers=(), scatters=()):
        self.gathers, self.scatters = list(gathers), list(scatters)
        self.n = len(self.gathers) + len(self.scatters)

    def operands(self):
        return self.gathers + self.scatters

    def out_shape(self):
        return ([jax.ShapeDtypeStruct((N_DEV,) + x.shape, x.dtype) for x in self.gathers]
                + [jax.ShapeDtypeStruct(x.shape, x.dtype) for x in self.scatters])

    def scratch(self):
        n = max(self.n, 1)
        return [pltpu.SemaphoreType.DMA((7 * n,)), pltpu.SemaphoreType.DMA((7 * n,)), pltpu.SemaphoreType.DMA((n,))]

    def _copies(self, in_refs, out_refs, send_sems, recv_sems, local_sems):
        me, peers = _peers()
        ng = len(self.gathers)
        local, sends, recvs = [], [], []
        for a in range(self.n):
            x, o = in_refs[a], out_refs[a]
            mine = x if a < ng else x.at[me]
            local.append(pltpu.make_async_copy(mine, o.at[me], local_sems.at[a]))
            s_a, r_a = {}, {}
            for k in range(1, N_DEV):
                peer, slot = peers[k - 1]
                sems = dict(send_sem=send_sems.at[7 * a + k - 1], recv_sem=recv_sems.at[7 * a + k - 1],
                            device_id_type=pl.DeviceIdType.MESH)
                if a >= ng:
                    s_a[k] = pltpu.make_async_remote_copy(src_ref=x.at[slot], dst_ref=o.at[me], device_id=peer, **sems)
                elif k in FORWARDED:
                    came = o.at[peers[k - 2][1]]
                    s_a[k] = pltpu.make_async_remote_copy(src_ref=came, dst_ref=came, device_id=peers[0][0], **sems)
                else:
                    s_a[k] = pltpu.make_async_remote_copy(src_ref=x, dst_ref=o.at[me], device_id=peer, **sems)
                r_a[k] = pltpu.make_async_remote_copy(src_ref=mine, dst_ref=o.at[slot], device_id=peer, **sems)
            sends.append(s_a)
            recvs.append(r_a)
        return local, sends, recvs

    def start(self, *refs):
        if self.n == 0:
            return
        local, sends, _ = self._copies(*refs)
        for a in range(self.n):
            local[a].start()
            for k in range(1, N_DEV):
                if a >= len(self.gathers) or k not in FORWARDED:
                    sends[a][k].start()

    def forward(self, *refs):
        if not self.gathers:
            return
        _, sends, recvs = self._copies(*refs)
        for a in range(len(self.gathers)):
            for k in FORWARDED:
                recvs[a][k - 1].wait_recv()
                sends[a][k].start()

    def wait(self, *refs):
        if self.n == 0:
            return
        local, sends, recvs = self._copies(*refs)
        for a in range(self.n):
            waited_early = [f - 1 for f in FORWARDED] if a < len(self.gathers) else []
            for k in range(1, N_DEV):
                if k not in waited_early:
                    recvs[a][k].wait_recv()
            for k in range(1, N_DEV):
                sends[a][k].wait_send()
            local[a].wait()


FORWARDED = (3, 5, 7)


def exchange(ex, name):
    n = ex.n

    def body(*refs):
        args = (refs[:n], refs[n:2 * n]) + tuple(refs[2 * n:])
        ex.start(*args)
        ex.forward(*args)
        ex.wait(*args)

    return pl.pallas_call(body, in_specs=[_ANY] * n, out_specs=[_ANY] * n, out_shape=ex.out_shape(),
                          scratch_shapes=ex.scratch(), name=name)(*ex.operands())


def _heads(z, h):
    return z.reshape(z.shape[0], h, HEAD_DIM).transpose(1, 0, 2)


def _unheads(z):
    return z.transpose(1, 0, 2).reshape(z.shape[1], z.shape[0] * HEAD_DIM)


def _shift_up(z):
    return jnp.concatenate([z[1:], jnp.zeros_like(z[:1])], axis=0)


def _segments(width):
    seg = np.zeros((width, 128), np.float32)
    seg[np.arange(width), np.arange(width) // HEAD_DIM] = 1.0
    return jnp.asarray(seg), jnp.asarray(seg.T)


def _rope_consts(T, heads):
    inv = ROPE_THETA ** (-jnp.arange(0, HEAD_DIM, 2, dtype=F32) / HEAD_DIM)
    ang = jnp.arange(T, dtype=F32)[:, None] * inv[None, :]
    cos, sin = jnp.cos(ang), jnp.sin(ang)
    rot = np.zeros((HEAD_DIM, HEAD_DIM), np.float32)
    half = HEAD_DIM // 2
    rot[np.arange(half) + half, np.arange(half)] = -1.0
    rot[np.arange(half), np.arange(half) + half] = 1.0
    return (jnp.tile(cos, (1, 2 * heads)), jnp.tile(sin, (1, 2 * heads)), jnp.asarray(np.kron(np.eye(heads, dtype=np.float32), rot)))


def _per_head(g, heads):
    return jnp.tile(g.reshape(1, HEAD_DIM), (1, heads))


def _sum_heads(g):
    return g.reshape(-1, HEAD_DIM).sum(axis=0, keepdims=True)


LORA_COLS = 256
RW_TILE = 128
ROW_TILE = 256


def _local_step(x0, memx, tgt, P, ex_weights=None, weights_done=None, ex_grads=None):
    T = x0.shape[0]
    P = dict(P)
    G = {}
    seg, seg_t = _segments(RWKV_WIDTH)
    mseg = (seg[:MEM_WIDTH], seg_t[:, :MEM_WIDTH])
    cos, sin, rot = _rope_consts(T, DIL_WIDTH // HEAD_DIM)
    row = lambda v: v.reshape(1, -1)

    def mem_fwd(i, q):
        memn = stage_fwd(f_rmsnorm, [memx], [P["mem_norm"][i:i + 1]], [], [], N_MEM, f"mem{i}_norm", [BF16])[0]
        kvm = matmul(memn, P["mem_w_kv"][i], "nn", f"mem{i}_kv")
        kn, qn = _per_head(P["mem_k_norm"][i], MEM_HEADS), _per_head(P["mem_q_norm"][i], MEM_HEADS)
        km = stage_fwd(f_headnorm, [Cols(kvm, MEM_WIDTH, 0)], [kn], [], mseg, N_MEM, f"mem{i}_knorm")[0]
        om = stage_fwd(f_memattn, [q], [km, Cols(kvm, MEM_WIDTH, 1), qn], [], mseg, ROW_TILE, f"mem{i}_attn")[0]
        return om, (memn, kvm, km, kn, qn, q)

    def mem_bwd(i, saved, dymem):
        memn, kvm, km, kn, qn, q = saved
        (dq,), (dkm, dvm, g_qn) = stage_bwd(f_memattn, [q], [km, Cols(kvm, MEM_WIDTH, 1), qn], [], mseg, [dymem], ROW_TILE,
                                            f"mem{i}_attn_bwd")
        (dkraw,), (g_kn,) = stage_bwd(f_headnorm, [Cols(kvm, MEM_WIDTH, 0)], [kn], [], mseg, [dkm], N_MEM, f"mem{i}_knorm_bwd")
        dkvm = jnp.concatenate([dkraw, dvm], axis=1).astype(BF16)
        g_w = matmul(memn, dkvm, "tn", f"mem{i}_kv_dw")
        dmemn = matmul(dkvm, P["mem_w_kv"][i], "nt", f"mem{i}_kv_dx")
        _, (g_mn,) = stage_bwd(f_rmsnorm, [memx], [P["mem_norm"][i:i + 1]], [], [], [dmemn], N_MEM, f"mem{i}_norm_bwd")
        return dq, g_mn, g_w, _sum_heads(g_qn), _sum_heads(g_kn)

    def ffn_fwd(i, xin):
        hn = stage_fwd(f_rmsnorm, [xin], [P["ffn_norm"][i:i + 1]], [], [], ROW_TILE, f"ffn{i}_norm", [BF16])[0]
        u = matmul(hn, P["ffn_w_up"][i], "nn", f"ffn{i}_up")
        z = convgate_fwd(u, P["ffn_conv_w"][i], P["ffn_conv_b"][i:i + 1], f"ffn{i}_conv")
        return matmul(z, P["ffn_w_down"][i], "nn", f"ffn{i}_down", residual=xin), (hn, u, z)

    def ffn_bwd(i, xin, saved, dxo, dxo_b):
        hn, u, z = saved
        dz = matmul(dxo_b, P["ffn_w_down"][i], "nt", f"ffn{i}_down_dx")
        g_down = matmul(z, dxo_b, "tn", f"ffn{i}_down_dw")
        dc, g_cw, g_cb = convgate_bwd_c(u, P["ffn_conv_w"][i], P["ffn_conv_b"][i:i + 1], dz, f"ffn{i}_conv_bwd_c")
        du = convgate_bwd_u(dc, P["ffn_conv_w"][i], f"ffn{i}_conv_bwd_u")
        dhn = matmul(du, P["ffn_w_up"][i], "nt", f"ffn{i}_up_dx")
        g_up = matmul(hn, du, "tn", f"ffn{i}_up_dw")
        (dxin,), (g_n,), (dxin_b,) = stage_bwd(f_rmsnorm_res, [xin], [P["ffn_norm"][i:i + 1]], [], [], [dhn, dxo], ROW_TILE,
                                               f"ffn{i}_norm_bwd", bf16_copies=(0,))
        return dxin, dxin_b, g_n, g_up, g_cw, g_cb, g_down

    h0 = stage_fwd(f_rmsnorm, [x0], [P["attn_norm"][0:1]], [], [], ROW_TILE, "l0_norm", [BF16])[0]
    p0 = matmul(h0, P["a_w_in"][0], "nn", "l0_in")
    lora0 = 3 * RWKV_WIDTH // LORA_COLS
    pre_xs = [Cols(p0, RWKV_WIDTH, 0), Cols(p0, RWKV_WIDTH, 1), Cols(p0, RWKV_WIDTH, 2), Cols(p0, LORA_COLS, lora0)]
    mu = [Cols(P["a_mu"], RWKV_WIDTH, 0), Cols(P["a_mu"], RWKV_WIDTH, 1), Cols(P["a_mu"], RWKV_WIDTH, 2),
          Cols(P["a_mu"], LORA_COLS, lora0)]
    lora_rows = lambda w, lo: jnp.pad(w, ((lo, LORA_COLS - lo - w.shape[0]), (0, 0)))
    pre_ps = mu + [P["a_w0"], lora_rows(P["a_w2"][0], 0), P["a_a0"], lora_rows(P["a_a2"][0], 64), lora_rows(P["a_g2"][0], 128),
                   P["a_k_k"], P["a_k_a"]]
    r, lw, k2, v, kk, b, g = stage_fwd(f_rwkv_pre, pre_xs, pre_ps, [], [seg, seg_t], RW_TILE, "l0_rwkv_pre", with_prev=True)
    hm = lambda z: _heads(z, RWKV_HEADS)
    scan_in = [hm(z) for z in (r, lw, k2, v, kk, b)]
    y_h, h_states, got = rwkv_scan_fwd(*scan_in, ex_weights or Exchange())
    if weights_done is not None:
        P.update(weights_done(got))
    y_s = _unheads(y_h)
    post_ps = [P["a_lnx_w"], P["a_lnx_b"], P["a_r_k"].reshape(1, RWKV_WIDTH)]
    ymix0 = stage_fwd(f_rwkv_post, [y_s, r, k2, v, g], post_ps, [], [seg, seg_t], RW_TILE, "l0_rwkv_post")[0]
    ymem0, mem0_saved = mem_fwd(0, Cols(p0, MEM_WIDTH, SHIFT_WIDTH // MEM_WIDTH))
    ycat0 = jnp.concatenate([ymix0, ymem0], axis=1).astype(BF16)
    x1 = matmul(ycat0, P["a_w_out"][0], "nn", "l0_out", residual=x0)
    x2, ffn0_saved = ffn_fwd(0, x1)

    hk, h1 = stage_fwd(f_rmsnorm2, [x2], [row(P["kv_norm"]), P["attn_norm"][1:2]], [], [], ROW_TILE, "l1_norm", [BF16, BF16])
    kvp = matmul(hk, P["kv_w"][0], "nn", "l1_kv")
    p1 = matmul(h1, P["b_w_in"][0], "nn", "l1_in")
    kraw, qraw = Cols(kvp, DIL_WIDTH, 0), Cols(p1, DIL_WIDTH, 0)
    kgain, qgain = _per_head(P["kv_k_norm"], DIL_WIDTH // HEAD_DIM), _per_head(P["b_q_norm"], DIL_WIDTH // HEAD_DIM)
    ksh = stage_fwd(f_qkprep, [kraw], [kgain], [cos, sin], [seg, seg_t, rot], ROW_TILE, "l1_kprep")[0]
    q = stage_fwd(f_qkprep, [qraw], [qgain], [cos, sin], [seg, seg_t, rot], ROW_TILE, "l1_qprep")[0]
    outs, lses = [], []
    for gi, (_, d) in enumerate(DIL_GROUPS):
        og, lg = dil_fwd(q, ksh, kvp, gi, d, f"l1_dil{gi}")
        outs.append(og)
        lses.append(lg)
    omix = stage_fwd(f_mix, outs + lses, [], [], [], ROW_TILE, "l1_mix")[0]
    ymem1, mem1_saved = mem_fwd(1, Cols(p1, MEM_WIDTH, DIL_WIDTH // MEM_WIDTH))
    ycat1 = jnp.concatenate([omix, ymem1], axis=1).astype(BF16)
    x3 = matmul(ycat1, P["b_w_out"][0], "nn", "l1_out", residual=x2)
    x4, ffn1_saved = ffn_fwd(1, x3)
    loss_part, dx4, dx4_b = loss_head(x4, tgt)

    dx3, dx3_b, gn1, gup1, gcw1, gcb1, gdown1 = ffn_bwd(1, x3, ffn1_saved, dx4, dx4_b)
    dycat1 = matmul(dx3_b, P["b_w_out"][0], "nt", "l1_out_dx")
    G["b_w_out"] = [matmul(ycat1, dx3_b, "tn", "l1_out_dw")]
    dqmem1, gmn1, gmw1, gmq1, gmk1 = mem_bwd(1, mem1_saved, Cols(dycat1, MEM_WIDTH, 1))
    dmix, _ = stage_bwd(f_mix, outs + lses, [], [], [], [Cols(dycat1, MEM_WIDTH, 0)], ROW_TILE, "l1_mix_bwd")
    dq, dk, dv = zip(*[dil_bwd(q, ksh, kvp, dmix[gi], dmix[3 + gi], gi, d, f"l1_dil{gi}_bwd")
                       for gi, (_, d) in enumerate(DIL_GROUPS)])
    dq, dk, dv = jnp.concatenate(dq, axis=1), jnp.concatenate(dk, axis=1), jnp.concatenate(dv, axis=1)
    (dqraw,), (g_bq,) = stage_bwd(f_qkprep, [qraw], [qgain], [cos, sin], [seg, seg_t, rot], [dq], ROW_TILE, "l1_qprep_bwd")
    (dkraw,), (g_kk,) = stage_bwd(f_qkprep, [kraw], [kgain], [cos, sin], [seg, seg_t, rot], [dk], ROW_TILE, "l1_kprep_bwd")
    g_bq, g_kk = _sum_heads(g_bq), _sum_heads(g_kk)
    dp1 = jnp.concatenate([dqraw, dqmem1], axis=1).astype(BF16)
    dkvp = jnp.concatenate([dkraw, dv], axis=1).astype(BF16)
    dh1 = matmul(dp1, P["b_w_in"][0], "nt", "l1_in_dx")
    G["b_w_in"] = [matmul(h1, dp1, "tn", "l1_in_dw")]
    dhk = matmul(dkvp, P["kv_w"][0], "nt", "l1_kv_dx")
    G["kv_w"] = [matmul(hk, dkvp, "tn", "l1_kv_dw")]
    (dx2,), (g_kvn, g_an1), (dx2_b,) = stage_bwd(f_rmsnorm2_res, [x2], [row(P["kv_norm"]), P["attn_norm"][1:2]], [], [],
                                                 [dhk, dh1, dx3], ROW_TILE, "l1_norm_bwd", bf16_copies=(0,))

    dx1, dx1_b, gn0, gup0, gcw0, gcb0, gdown0 = ffn_bwd(0, x1, ffn0_saved, dx2, dx2_b)
    dycat0 = matmul(dx1_b, P["a_w_out"][0], "nt", "l0_out_dx")
    G["a_w_out"] = [matmul(ycat0, dx1_b, "tn", "l0_out_dw")]
    dqmem0, gmn0, gmw0, gmq0, gmk0 = mem_bwd(0, mem0_saved, Cols(dycat0, MEM_WIDTH, RWKV_WIDTH // MEM_WIDTH))
    (dy_s, dr_a, dk_a, dv_a, dg), (g_lw, g_lb, g_rk) = stage_bwd(
        f_rwkv_post, [y_s, r, k2, v, g], post_ps, [], [seg, seg_t], [Cols(dycat0, RWKV_WIDTH, 0)], RW_TILE, "l0_rwkv_post_bwd")
    G["mem_w_kv"], G["ffn_w_up"], G["ffn_w_down"] = [gmw0, gmw1], [gup0, gup1], [gdown0, gdown1]
    d_scan, G["_exchanged"] = rwkv_scan_bwd(*scan_in, h_states, hm(dy_s), ex_grads(G) if ex_grads else Exchange())
    dr_b, dlw, dk_b, dv_b, dkk, db = [_unheads(z) for z in d_scan]
    dpre, gpre = stage_bwd(f_rwkv_pre, pre_xs, pre_ps, [], [seg, seg_t],
                           [[dr_a, dr_b], dlw, [dk_a, dk_b], [dv_a, dv_b], dkk, db, dg], RW_TILE, "l0_rwkv_pre_bwd", with_prev=True)
    dp_rw = jnp.concatenate(dpre[:4], axis=1) + _shift_up(jnp.concatenate(dpre[4:], axis=1))
    dp0 = jnp.concatenate([dp_rw, dqmem0], axis=1).astype(BF16)
    dh0 = matmul(dp0, P["a_w_in"][0], "nt", "l0_in_dx")
    G["a_w_in"] = [matmul(h0, dp0, "tn", "l0_in_dw")]
    (dx0,), (g_an0,) = stage_bwd(f_rmsnorm_res, [x0], [P["attn_norm"][0:1]], [], [], [dh0, dx1], ROW_TILE, "l0_norm_bwd")

    G["attn_norm"] = jnp.concatenate([g_an0, g_an1], axis=0)
    G["a_mu"] = jnp.concatenate(gpre[:4], axis=1)
    G["a_w0"], G["a_w2"], G["a_a0"], G["a_a2"], G["a_g2"] = gpre[4], gpre[5][None, :64], gpre[6], gpre[7][None, 64:128], gpre[8][None, 128:]
    G["a_k_k"], G["a_k_a"] = gpre[9], gpre[10]
    G["a_r_k"] = g_rk.reshape(1, RWKV_HEADS, HEAD_DIM)
    G["a_lnx_w"], G["a_lnx_b"] = g_lw, g_lb
    G["kv_norm"], G["kv_k_norm"], G["b_q_norm"] = g_kvn.reshape(-1), g_kk.reshape(-1), g_bq
    G["mem_norm"] = jnp.concatenate([gmn0, gmn1], axis=0)
    G["mem_w_kv"] = [gmw0, gmw1]
    G["mem_q_norm"] = jnp.concatenate([gmq0, gmq1], axis=0)
    G["mem_k_norm"] = jnp.concatenate([gmk0, gmk1], axis=0)
    G["ffn_norm"] = jnp.concatenate([gn0, gn1], axis=0)
    G["ffn_w_up"] = [gup0, gup1]
    G["ffn_conv_w"] = jnp.stack([gcw0, gcw1])
    G["ffn_conv_b"] = jnp.concatenate([gcb0, gcb1], axis=0)
    G["ffn_w_down"] = [gdown0, gdown1]
    return loss_part, dx0, G


PARAMS = (("attn_norm", None), ("a_w_in", 2), ("a_mu", 1), ("a_w0", 1), ("a_w2", 2), ("a_a0", 1), ("a_a2", 2), ("a_g2", 2),
          ("a_k_k", 1), ("a_k_a", 1), ("a_r_k", None), ("a_lnx_w", 1), ("a_lnx_b", 1), ("a_w_out", 1), ("kv_norm", None),
          ("kv_w", 1), ("kv_k_norm", None), ("b_w_in", 1), ("b_q_norm", None), ("b_w_out", 2), ("mem_norm", None),
          ("mem_w_kv", 1), ("mem_q_norm", None), ("mem_k_norm", None), ("ffn_norm", None), ("ffn_w_up", 2),
          ("ffn_conv_w", 2), ("ffn_conv_b", None), ("ffn_w_down", 1))
BIG = ("a_w_in", "a_w_out", "kv_w", "b_w_in", "b_w_out", "mem_w_kv", "ffn_w_up", "ffn_w_down")
AXIS = dict(PARAMS)
SMALL = tuple(n for n, _ in PARAMS if n not in BIG)
SMALL_SHARDED = tuple(n for n in SMALL if AXIS[n] is not None)
PACK_QUANTUM = 256 * 128


def _from_shards(xs, axis):
    full = jnp.moveaxis(xs, 0, axis)
    sh = full.shape
    return full.reshape(sh[:axis] + (sh[axis] * sh[axis + 1],) + sh[axis + 2:])


def _to_shards(g, axis):
    sh = g.shape
    return jnp.moveaxis(g.reshape(sh[:axis] + (N_DEV, sh[axis] // N_DEV) + sh[axis + 1:]), axis, 0)


def _pack(parts, lead=0):
    ld = parts[0].shape[:lead]
    flat = jnp.concatenate([p.reshape(ld + (-1,)) for p in parts], axis=-1)
    pad = (-flat.shape[-1]) % PACK_QUANTUM
    flat = jnp.pad(flat, [(0, 0)] * lead + [(0, pad)])
    return flat.reshape(ld + (-1, 128))


def _unpack(packed, shapes, lead=0):
    ld = packed.shape[:lead]
    flat = packed.reshape(ld + (-1,))
    out, off = [], 0
    for s in shapes:
        n = math.prod(s)
        out.append(flat[..., off:off + n].reshape(ld + tuple(s)))
        off += n
    return out


def kernel(x, mem, attn_norm, a_w_in, a_mu, a_w0, a_w2, a_a0, a_a2, a_g2, a_k_k, a_k_a, a_r_k, a_lnx_w, a_lnx_b, a_w_out, kv_norm, kv_w, kv_k_norm, b_w_in, b_q_norm, b_w_out, mem_norm, mem_w_kv, mem_q_norm, mem_k_norm, ffn_norm, ffn_w_up, ffn_conv_w, ffn_conv_b, ffn_w_down, loss_target, m_attn_norm, m_a_w_in, m_a_mu, m_a_w0, m_a_w2, m_a_a0, m_a_a2, m_a_g2, m_a_k_k, m_a_k_a, m_a_r_k, m_a_lnx_w, m_a_lnx_b, m_a_w_out, m_kv_norm, m_kv_w, m_kv_k_norm, m_b_w_in, m_b_q_norm, m_b_w_out, m_mem_norm, m_mem_w_kv, m_mem_q_norm, m_mem_k_norm, m_ffn_norm, m_ffn_w_up, m_ffn_conv_w, m_ffn_conv_b, m_ffn_w_down, v_attn_norm, v_a_w_in, v_a_mu, v_a_w0, v_a_w2, v_a_a0, v_a_a2, v_a_g2, v_a_k_k, v_a_k_a, v_a_r_k, v_a_lnx_w, v_a_lnx_b, v_a_w_out, v_kv_norm, v_kv_w, v_kv_k_norm, v_b_w_in, v_b_q_norm, v_b_w_out, v_mem_norm, v_mem_w_kv, v_mem_q_norm, v_mem_k_norm, v_ffn_norm, v_ffn_w_up, v_ffn_conv_w, v_ffn_conv_b, v_ffn_w_down):
    names = [n for n, _ in PARAMS]
    vals = (attn_norm, a_w_in, a_mu, a_w0, a_w2, a_a0, a_a2, a_g2, a_k_k, a_k_a, a_r_k, a_lnx_w, a_lnx_b, a_w_out, kv_norm, kv_w, kv_k_norm, b_w_in, b_q_norm, b_w_out, mem_norm, mem_w_kv, mem_q_norm, mem_k_norm, ffn_norm, ffn_w_up, ffn_conv_w, ffn_conv_b, ffn_w_down)
    m_vals = (m_attn_norm, m_a_w_in, m_a_mu, m_a_w0, m_a_w2, m_a_a0, m_a_a2, m_a_g2, m_a_k_k, m_a_k_a, m_a_r_k, m_a_lnx_w, m_a_lnx_b, m_a_w_out, m_kv_norm, m_kv_w, m_kv_k_norm, m_b_w_in, m_b_q_norm, m_b_w_out, m_mem_norm, m_mem_w_kv, m_mem_q_norm, m_mem_k_norm, m_ffn_norm, m_ffn_w_up, m_ffn_conv_w, m_ffn_conv_b, m_ffn_w_down)
    v_vals = (v_attn_norm, v_a_w_in, v_a_mu, v_a_w0, v_a_w2, v_a_a0, v_a_a2, v_a_g2, v_a_k_k, v_a_k_a, v_a_r_k, v_a_lnx_w, v_a_lnx_b, v_a_w_out, v_kv_norm, v_kv_w, v_kv_k_norm, v_b_w_in, v_b_q_norm, v_b_w_out, v_mem_norm, v_mem_w_kv, v_mem_q_norm, v_mem_k_norm, v_ffn_norm, v_ffn_w_up, v_ffn_conv_w, v_ffn_conv_b, v_ffn_w_down)
    W, M, V = dict(zip(names, vals)), dict(zip(names, m_vals)), dict(zip(names, v_vals))
    me = 4 * lax.axis_index("x") + 2 * lax.axis_index("y") + lax.axis_index("c")
    layers = lambda D, n: [D[n]] if D[n].ndim == 2 else [D[n][i] for i in range(D[n].shape[0])]
    ax2 = lambda n: AXIS[n] - (W[n].ndim - 2)
    first = [("a_w_in", 0)]
    later = [(n, i) for n in BIG if n != "a_w_in" for i in range(len(layers(W, n)))]

    small_shapes = [W[n].shape for n in SMALL_SHARDED]
    got_w, got_small = exchange(Exchange(gathers=[W["a_w_in"][0].astype(BF16), _pack([W[n] for n in SMALL_SHARDED])]),
                                "gather_first")
    P = {n: W[n] for n in SMALL}
    P["a_w_in"] = [_from_shards(got_w, ax2("a_w_in"))]
    for n, s in zip(SMALL_SHARDED, _unpack(got_small, small_shapes, lead=1)):
        P[n] = _from_shards(s, AXIS[n])
    ex_weights = Exchange(gathers=[layers(W, n)[i].astype(BF16) for n, i in later])

    def weights_done(got):
        out = {}
        for (n, _), g in zip(later, got):
            out.setdefault(n, []).append(_from_shards(g, ax2(n)))
        return out

    ex_grads = lambda G: Exchange(scatters=[_to_shards(G[n][i], ax2(n)) for n, i in later])
    loss_part, dx0, G = _local_step(x[0], mem[0], loss_target[0], P, ex_weights, weights_done, ex_grads)
    loss = lax.psum(loss_part[0, 0], ("x", "y", "c"))
    gparts = dict(zip(later, G.pop("_exchanged")))
    got_gsmall, gparts[first[0]] = exchange(
        Exchange(gathers=[_pack([G[n] for n in SMALL])], scatters=[_to_shards(G["a_w_in"][0], ax2("a_w_in"))]), "exchange_last")

    results = {}
    for n in BIG:
        per_layer = [adamw(gparts[(n, i)], w, m, v, f"adamw_{n}{i}")
                     for i, (w, m, v) in enumerate(zip(layers(W, n), layers(M, n), layers(V, n)))]
        results[n] = [r[0] if W[n].ndim == 2 else jnp.stack(r) for r in zip(*per_layer)]
    g_small = sum_parts(got_gsmall, "sum_small_grads")
    mine = []
    for n, g in zip(SMALL, _unpack(g_small, [G[n].shape for n in SMALL])):
        if AXIS[n] is not None:
            s = W[n].shape[AXIS[n]]
            g = lax.dynamic_slice_in_dim(g, me * s, s, axis=AXIS[n])
        mine.append(g)
    res = adamw(_pack(mine)[None], _pack([W[n] for n in SMALL]), _pack([M[n] for n in SMALL]), _pack([V[n] for n in SMALL]),
                "adamw_small")
    for n, parts in zip(SMALL, zip(*[_unpack(r, [W[n].shape for n in SMALL]) for r in res])):
        results[n] = list(parts)
    outs = [[results[n][j] for n in names] for j in range(4)]
    return (loss, dx0[None], *outs[0], *outs[1], *outs[2], *outs[3])
```

```python
import functools
import math

import jax
import jax.numpy as jnp
import numpy as np
from jax import lax
from jax.experimental import pallas as pl
from jax.experimental.pallas import tpu as pltpu

F32 = jnp.float32
BF16 = jnp.bfloat16
HI = lax.Precision.HIGHEST
H3 = lax.Precision.HIGH

N_DEV = 8
D_MODEL = 1024
HEAD_DIM = 64
N_MEM = 256
MEM_HEADS = 4
MEM_WIDTH = 256
RWKV_HEADS = 12
RWKV_WIDTH = 768
SHIFT_WIDTH = 2560
DIL_GROUPS = ((128, 1), (512, 4), (2048, 16))
DIL_BLOCK = 128
DIL_WIDTH = 768
D_FF = 2816
RMS_EPS = 1e-6
LNX_EPS = 64e-5
NEG_INF = -1e30
ROPE_THETA = 10000.0
ADAM_LR, ADAM_B1, ADAM_B2, ADAM_EPS, ADAM_WD, ADAM_STEP = 0.001, 0.9, 0.999, 1e-08, 0.01, 10

CHUNK = 64
SCAN_GROUPS_FWD, SCAN_GROUPS_BWD = 1, 1
MM_TILE_CAP = 1408
VMEM_LIMIT_V7X = 48 * 1024 * 1024


def _cparams(sem):
    return pltpu.CompilerParams(dimension_semantics=sem, vmem_limit_bytes=VMEM_LIMIT_V7X)


def _pick(n, cands):
    for c in cands:
        if n % c == 0:
            return c
    return n


def _tile(n, cap):
    if n <= cap:
        return n
    for d in range(cap - cap % 128, 0, -128):
        if n % d == 0:
            return d
    return n


def _dg(a, b, ca, cb, batch):
    dims = (((ca,), (cb,)), ((0,), (0,))) if batch else (((ca,), (cb,)), ((), ()))
    return lax.dot_general(a.astype(BF16), b.astype(BF16), dims, preferred_element_type=F32)


@jax.custom_vjp
def mm_nn(a, b):
    n = a.ndim
    return _dg(a, b, n - 1, n - 2, n == 3)


def _mm_nn_fwd(a, b):
    return mm_nn(a, b), (a, b)


def _mm_nn_bwd(res, g):
    a, b = res
    n = a.ndim
    return _dg(g, b, n - 1, n - 1, n == 3), _dg(a, g, n - 2, n - 2, n == 3)


mm_nn.defvjp(_mm_nn_fwd, _mm_nn_bwd)


@jax.custom_vjp
def mm_nt(a, b):
    n = a.ndim
    return _dg(a, b, n - 1, n - 1, n == 3)


def _mm_nt_fwd(a, b):
    return mm_nt(a, b), (a, b)


def _mm_nt_bwd(res, g):
    a, b = res
    n = a.ndim
    return _dg(g, b, n - 1, n - 2, n == 3), _dg(g, a, n - 2, n - 2, n == 3)


mm_nt.defvjp(_mm_nt_fwd, _mm_nt_bwd)


def mmh(a, b, precision=H3):
    n = a.ndim
    dims = (((n - 1,), (n - 2,)), ((0,), (0,))) if n == 3 else (((1,), (0,)), ((), ()))
    return lax.dot_general(a, b, dims, precision=precision, preferred_element_type=F32)


def mmh_nt(a, b):
    n = a.ndim
    dims = (((n - 1,), (n - 1,)), ((0,), (0,))) if n == 3 else (((1,), (1,)), ((), ()))
    return lax.dot_general(a, b, dims, precision=H3, preferred_element_type=F32)


def mmh_tn(a, b):
    n = a.ndim
    dims = (((n - 2,), (n - 2,)), ((0,), (0,))) if n == 3 else (((0,), (0,)), ((), ()))
    return lax.dot_general(a, b, dims, precision=H3, preferred_element_type=F32)


def matmul(a, b, mode, name, residual=None):
    out_dtype = BF16 if mode == "tn" else F32
    if mode == "nn":
        (M, K), (_, N) = a.shape, b.shape
    elif mode == "nt":
        (M, K), (N, _) = a.shape, b.shape
    else:
        (K, M), (_, N) = a.shape, b.shape
    tm = _tile(M, 2048 if mode == "nn" else MM_TILE_CAP)
    tn = _tile(N, 512 if mode == "nn" else MM_TILE_CAP)
    tk = _tile(K, MM_TILE_CAP if mode != "nt" else 512)
    nk = K // tk
    if mode == "nn":
        a_spec = pl.BlockSpec((tm, tk), lambda i, j, k: (i, k))
        b_spec = pl.BlockSpec((tk, tn), lambda i, j, k: (k, j))
        dims = (((1,), (0,)), ((), ()))
    elif mode == "nt":
        a_spec = pl.BlockSpec((tm, tk), lambda i, j, k: (i, k))
        b_spec = pl.BlockSpec((tn, tk), lambda i, j, k: (j, k))
        dims = (((1,), (1,)), ((), ()))
    else:
        a_spec = pl.BlockSpec((tk, tm), lambda i, j, k: (k, i))
        b_spec = pl.BlockSpec((tk, tn), lambda i, j, k: (k, j))
        dims = (((0,), (0,)), ((), ()))
    o_spec = pl.BlockSpec((tm, tn), lambda i, j, k: (i, j))
    has_res = residual is not None

    def body(*refs):
        if has_res:
            a_ref, b_ref, r_ref, o_ref, acc_ref = refs
        else:
            a_ref, b_ref, o_ref, acc_ref = refs
        k = pl.program_id(2)

        @pl.when(k == 0)
        def _():
            acc_ref[...] = jnp.zeros_like(acc_ref)

        acc_ref[...] += lax.dot_general(a_ref[...].astype(BF16), b_ref[...].astype(BF16), dims,
                                        preferred_element_type=F32)

        @pl.when(k == nk - 1)
        def _():
            if has_res:
                o_ref[...] = (acc_ref[...] + r_ref[...]).astype(out_dtype)
            else:
                o_ref[...] = acc_ref[...].astype(out_dtype)

    ins = [a, b] + ([residual] if has_res else [])
    in_specs = [a_spec, b_spec] + ([o_spec] if has_res else [])
    return pl.pallas_call(
        body, grid=(M // tm, N // tn, nk), in_specs=in_specs, out_specs=o_spec,
        out_shape=jax.ShapeDtypeStruct((M, N), out_dtype), scratch_shapes=[pltpu.VMEM((tm, tn), F32)],
        compiler_params=_cparams(("parallel", "parallel", "arbitrary")), name=name)(*ins)


class Cols:
    def __init__(self, arr, width, idx):
        self.arr, self.width, self.idx = arr, width, idx


def _arr(x):
    return x.arr if isinstance(x, Cols) else x


def _shape(x):
    return x.arr.shape[:-1] + (x.width,) if isinstance(x, Cols) else x.shape


def _col(x):
    return x.idx if isinstance(x, Cols) else 0


def _tok_spec(x, tile):
    shape, col = _shape(x), _col(x)
    return pl.BlockSpec(shape[:-2] + (tile, shape[-1]), lambda i: (0,) * (len(shape) - 2) + (i, col))


def _full_spec(x):
    shape, col = _shape(x), _col(x)
    return pl.BlockSpec(shape, lambda i: (0,) * (len(shape) - 1) + (col,))


def _halo_spec(x, tile):
    shape, col = _shape(x), _col(x)
    return pl.BlockSpec((8, shape[-1]), lambda i: (jnp.maximum(i * (tile // 8) - 1, 0), col))


def _blk(x, tile):
    shape = _shape(x)
    return jax.ShapeDtypeStruct(shape[:-2] + (tile, shape[-1]), _arr(x).dtype)


def _prev_rows(x, halo):
    rows = lax.broadcasted_iota(jnp.int32, (x.shape[0], 1), 0)
    before = jnp.where(pl.program_id(0) > 0, halo[7:8], 0.0)
    return jnp.where(rows == 0, before, pltpu.roll(x, 1, 0))


def stage_fwd(f, xs, ps, cts, cfs, tile, name, out_dtypes=None, with_prev=False):
    xs, ps, cts, cfs = list(xs), list(ps), list(cts), list(cfs)
    halos = xs if with_prev else []
    nx, nh, nct, np_ = len(xs), len(halos), len(cts), len(ps)
    T = _shape(xs[0])[-2]
    blk = [_blk(x, tile) for x in xs]
    out_avals = jax.eval_shape(f, *blk, *(blk if with_prev else []), *[_blk(p, _shape(p)[-2]) for p in ps],
                               *[_blk(c, tile) for c in cts], *[_blk(c, _shape(c)[-2]) for c in cfs])
    if out_dtypes is None:
        out_dtypes = [o.dtype for o in out_avals]
    out_shape = [jax.ShapeDtypeStruct(o.shape[:-2] + (T, o.shape[-1]), dt) for o, dt in zip(out_avals, out_dtypes)]
    n_in = nx + nh + nct + np_ + len(cfs)

    def body(*refs):
        vals = [r[...] for r in refs[:n_in]]
        xv, hv, rest = vals[:nx], vals[nx:nx + nh], vals[nx + nh:]
        ctv, pv, cfv = rest[:nct], rest[nct:nct + np_], rest[nct + np_:]
        prev = [_prev_rows(x, h) for x, h in zip(xv, hv)]
        res = f(*xv, *prev, *pv, *ctv, *cfv)
        for o_ref, r in zip(refs[n_in:], res):
            o_ref[...] = r.astype(o_ref.dtype)

    return pl.pallas_call(
        body, grid=(T // tile,),
        in_specs=([_tok_spec(x, tile) for x in xs] + [_halo_spec(x, tile) for x in halos] + [_tok_spec(c, tile) for c in cts]
                  + [_full_spec(p) for p in ps + cfs]),
        out_specs=[_tok_spec(o, tile) for o in out_shape], out_shape=out_shape,
        compiler_params=_cparams(("parallel",)), name=name)(*[_arr(a) for a in xs + halos + cts + ps + cfs])


def stage_bwd(f, xs, ps, cts, cfs, gs, tile, name, bf16_copies=(), with_prev=False):
    xs, ps, cts, cfs = list(xs), list(ps), list(cts), list(cfs)
    gs = [list(g) if isinstance(g, (list, tuple)) else [g] for g in gs]
    g_flat = [a for g in gs for a in g]
    halos = xs if with_prev else []
    nx, nh, nct, ng, np_ = len(xs), len(halos), len(cts), len(g_flat), len(ps)
    T = _shape(xs[0])[-2]
    dx_like = xs + halos
    out_shape = ([jax.ShapeDtypeStruct(_shape(x), F32) for x in dx_like] + [jax.ShapeDtypeStruct(_shape(p), F32) for p in ps]
                 + [jax.ShapeDtypeStruct(_shape(xs[i]), BF16) for i in bf16_copies])
    n_in = nx + nh + nct + ng + np_ + len(cfs)
    ndx = nx + nh

    def body(*refs):
        vals = [r[...] for r in refs[:n_in]]
        outs = refs[n_in:]
        xv, hv, rest = vals[:nx], vals[nx:nx + nh], vals[nx + nh:]
        ctv, gparts, pv, cfv = rest[:nct], rest[nct:nct + ng], rest[nct + ng:nct + ng + np_], rest[nct + ng + np_:]
        gv = []
        for g in gs:
            gv.append(functools.reduce(lambda a, b: a + b, gparts[:len(g)]))
            gparts = gparts[len(g):]
        prev = [_prev_rows(x, h) for x, h in zip(xv, hv)]
        _, vjp = jax.vjp(lambda *xp: f(*xp, *ctv, *cfv), *xv, *prev, *pv)
        d = vjp(tuple(gv))
        for o_ref, r in zip(outs[:ndx], d[:ndx]):
            o_ref[...] = r
        for o_ref, i in zip(outs[ndx + np_:], bf16_copies):
            o_ref[...] = d[i].astype(BF16)

        @pl.when(pl.program_id(0) == 0)
        def _():
            for o_ref in outs[ndx:ndx + np_]:
                o_ref[...] = jnp.zeros_like(o_ref)

        for o_ref, r in zip(outs[ndx:ndx + np_], d[ndx:]):
            o_ref[...] += r

    plain = lambda x: jax.ShapeDtypeStruct(_shape(x), F32)
    res = pl.pallas_call(
        body, grid=(T // tile,),
        in_specs=([_tok_spec(x, tile) for x in xs] + [_halo_spec(x, tile) for x in halos]
                  + [_tok_spec(c, tile) for c in cts + g_flat] + [_full_spec(p) for p in ps + cfs]),
        out_specs=([_tok_spec(plain(x), tile) for x in dx_like] + [_full_spec(plain(p)) for p in ps]
                   + [_tok_spec(plain(xs[i]), tile) for i in bf16_copies]), out_shape=out_shape,
        compiler_params=_cparams(("arbitrary",)), name=name)(*[_arr(a) for a in xs + halos + cts + g_flat + ps + cfs])
    if bf16_copies:
        return list(res[:ndx]), list(res[ndx:ndx + np_]), list(res[ndx + np_:])
    return list(res[:ndx]), list(res[ndx:])


def _rms(x, g, eps=RMS_EPS):
    return x * lax.rsqrt(jnp.mean(x * x, axis=-1, keepdims=True) + eps) * g


def f_rmsnorm(x, g):
    return (_rms(x, g),)


def f_rmsnorm_res(x, g):
    return _rms(x, g), x


def f_rmsnorm2(x, g1, g2):
    n = x * lax.rsqrt(jnp.mean(x * x, axis=-1, keepdims=True) + RMS_EPS)
    return n * g1, n * g2


def f_rmsnorm2_res(x, g1, g2):
    return f_rmsnorm2(x, g1, g2) + (x,)


def _sigmoid(x):
    return 1.0 / (1.0 + jnp.exp(-x))


def _softplus(x):
    return jnp.maximum(x, 0.0) + jnp.log(1.0 + jnp.exp(-jnp.abs(x)))


def f_rwkv_pre(pr, pk, pv, pl_, qr, qk, qv, ql, mu_r, mu_k, mu_v, mu_l, w0, w2, a0, a2, g2, k_k, k_a, seg, seg_t):
    xr = pr + (qr - pr) * mu_r
    xk = pk + (qk - pk) * mu_k
    xv = pv + (qv - pv) * mu_v
    xl = pl_ + (ql - pl_) * mu_l
    w_log = -_softplus(-(w0 + mm_nn(jnp.tanh(xl), w2))) - 0.5
    lw = -jnp.exp(w_log)
    a = _sigmoid(a0 + mm_nn(xl, a2))
    g = mm_nn(_sigmoid(xl), g2)
    kkr = xk * k_k
    inv = lax.rsqrt(jnp.maximum(mmh(kkr * kkr, seg), 1e-24))
    kk = kkr * mmh(inv, seg_t)
    k2 = xk * (1.0 + (a - 1.0) * k_a)
    return xr, lw, k2, xv, kk, kk * a, g


def f_rwkv_post(y, r, k2, v, g, lnx_w, lnx_b, r_k, seg, seg_t):
    inv_n = 1.0 / HEAD_DIM
    m = mmh(mmh(y, seg) * inv_n, seg_t)
    yc = y - m
    rstd = lax.rsqrt(mmh(yc * yc, seg) * inv_n + LNX_EPS)
    yn = yc * mmh(rstd, seg_t) * lnx_w + lnx_b
    bonus = mmh(mmh(r * k2 * r_k, seg), seg_t) * v
    return ((yn + bonus) * g,)


def _headnorm(z, g, seg, seg_t):
    ms = mmh(z * z, seg) * (1.0 / HEAD_DIM)
    return z * mmh(lax.rsqrt(ms + RMS_EPS), seg_t) * g


def f_headnorm(z, g, seg, seg_t):
    return (_headnorm(z, g, seg, seg_t),)


def f_qkprep(z, g, cos, sin, seg, seg_t, rot):
    zn = _headnorm(z, g, seg, seg_t)
    return (zn * cos + mmh(zn, rot) * sin,)


def _head_mask(width, h):
    lane = lax.broadcasted_iota(jnp.int32, (1, width), 1)
    return jnp.where((lane >> 6) == h, jnp.ones((), F32), 0.0)


def f_memattn(q, k, v, q_norm, seg, seg_t):
    qn = _headnorm(q, q_norm, seg, seg_t)
    out = jnp.zeros_like(q)
    for h in range(MEM_HEADS):
        m = _head_mask(MEM_WIDTH, h)
        s = mm_nt(qn * m, k) * (1.0 / math.sqrt(HEAD_DIM))
        s = s - jnp.max(s, axis=-1, keepdims=True)
        p = jnp.exp(s)
        p = p / jnp.sum(p, axis=-1, keepdims=True)
        out = out + mm_nn(p, v) * m
    return (out,)


def f_mix(o1, o2, o3, l1, l2, l3):
    mx = jnp.maximum(jnp.maximum(l1, l2), l3)
    e1, e2, e3 = jnp.exp(l1 - mx), jnp.exp(l2 - mx), jnp.exp(l3 - mx)
    return ((e1 * o1 + e2 * o2 + e3 * o3) / (e1 + e2 + e3),)


def _chunk_masks(L):
    t = lax.broadcasted_iota(jnp.int32, (L, L), 0)
    s = lax.broadcasted_iota(jnp.int32, (L, L), 1)
    return t, s


def _unit_lower_inverse(a):
    L = a.shape[-1]
    t, s = _chunk_masks(L)
    one = jnp.ones((), F32)
    blk = lambda sh: jnp.where((t >> sh) == (s >> sh), one, 0.0)
    n0 = a * blk(3)
    x = jnp.where(t == s, one, 0.0) - n0
    n2 = mmh(n0, n0)
    x = x + mmh(x, n2)
    x = x + mmh(x, mmh(n2, n2))
    for sh in (3, 4, 5):
        if (1 << sh) >= L:
            break
        off = a * (blk(sh + 1) - blk(sh))
        x = x - mmh(x, mmh(off, x))
    return x


@jax.custom_vjp
def _inverse_known(a, x):
    return x


def _inverse_known_fwd(a, x):
    return x, x


def _inverse_known_bwd(x, dx):
    return -mmh_nt(mmh_tn(x, dx), x), jnp.zeros_like(x)


_inverse_known.defvjp(_inverse_known_fwd, _inverse_known_bwd)


def _running_sum(x, reverse):
    L = x.shape[1]
    pos = lax.broadcasted_iota(jnp.int32, (1, L, 1), 1)
    step = 1
    while step < L:
        if reverse:
            x = x + jnp.where(pos < L - step, pltpu.roll(x, L - step, 1), 0.0)
        else:
            x = x + jnp.where(pos >= step, pltpu.roll(x, step, 1), 0.0)
        step *= 2
    return x


@jax.custom_vjp
def _cumsum_tokens(x):
    return _running_sum(x, False)


_cumsum_tokens.defvjp(lambda x: (_running_sum(x, False), None), lambda _, g: (_running_sum(g, True),))


def f_rwkv_chunk(s0, r, lw, k, v, kk, b, x_known=None):
    H, L, _ = r.shape
    t, s = _chunk_masks(L)
    one = jnp.ones((), F32)
    incl = jnp.where(t >= s, one, 0.0)
    strict = jnp.where(t > s, one, 0.0)
    cum = _cumsum_tokens(lw)
    w_in = jnp.exp(cum)
    w_ex = jnp.exp(cum - lw)
    w_inv = jnp.exp(-cum)
    rt, kkt, kt, bt = r * w_in, kk * w_ex, k * w_inv, b * w_inv
    a_b = mmh_nt(kkt, bt) * strict
    a_k = mmh_nt(kkt, kt) * strict
    m_k = mmh_nt(rt, kt) * incl
    m_b = mmh_nt(rt, bt) * incl
    x = _unit_lower_inverse(a_b) if x_known is None else _inverse_known(a_b, x_known)
    u = mmh(x, mmh_nt(kkt, s0) + mmh(a_k, v))
    y = mmh_nt(rt, s0) + mmh(m_k, v) - mmh(m_b, u)
    w_last = jnp.exp(jnp.sum(lw, axis=1, keepdims=True))
    s1 = (s0 + mmh_tn(v, kt) - mmh_tn(u, bt)) * w_last
    return y, s1, x


def _ex_split(ex, refs, n_in, n_out):
    n = ex.n
    ins, ex_in = refs[:n_in], refs[n_in:n_in + n]
    outs, ex_out = refs[n_in + n:n_in + n + n_out], refs[n_in + n + n_out:n_in + 2 * n + n_out]
    rest = refs[n_in + 2 * n + n_out:]
    return ins, outs, rest[:len(rest) - 3], (ex_in, ex_out) + tuple(rest[len(rest) - 3:])


def rwkv_scan_fwd(r, lw, k, v, kk, b, ex):
    H, T, N = r.shape
    groups = SCAN_GROUPS_FWD
    nc, hg = T // CHUNK, H // groups
    seq = pl.BlockSpec((hg, CHUNK, N), lambda g, c: (g, c, 0))

    def body(*refs):
        (r_ref, lw_ref, k_ref, v_ref, kk_ref, b_ref), (y_ref, hs_ref, xs_ref), (h_scr,), ex_refs = _ex_split(ex, refs, 6, 3)
        g, c = pl.program_id(0), pl.program_id(1)

        @pl.when(jnp.logical_and(g == 0, c == 0))
        def _():
            ex.start(*ex_refs)

        @pl.when(c == 0)
        def _():
            h_scr[...] = jnp.zeros_like(h_scr)

        h0 = h_scr[...]
        hs_ref[0] = h0
        y, h1, x = f_rwkv_chunk(h0, r_ref[...], lw_ref[...], k_ref[...], v_ref[...], kk_ref[...], b_ref[...])
        y_ref[...] = y
        xs_ref[0] = x
        h_scr[...] = h1

        @pl.when(jnp.logical_and(g == groups - 1, c == (3 * nc) // 4))
        def _():
            ex.forward(*ex_refs)

        @pl.when(jnp.logical_and(g == groups - 1, c == nc - 1))
        def _():
            ex.wait(*ex_refs)

    res = pl.pallas_call(
        body, grid=(groups, nc), in_specs=[seq] * 6 + [_ANY] * ex.n,
        out_specs=[seq, pl.BlockSpec((1, hg, N, N), lambda g, c: (c, g, 0, 0)),
                   pl.BlockSpec((1, hg, CHUNK, CHUNK), lambda g, c: (c, g, 0, 0))] + [_ANY] * ex.n,
        out_shape=[jax.ShapeDtypeStruct((H, T, N), F32), jax.ShapeDtypeStruct((nc, H, N, N), F32),
                   jax.ShapeDtypeStruct((nc, H, CHUNK, CHUNK), F32)] + ex.out_shape(),
        scratch_shapes=[pltpu.VMEM((hg, N, N), F32)] + ex.scratch(),
        compiler_params=_cparams(("arbitrary", "arbitrary")), name="rwkv_scan_fwd")(r, lw, k, v, kk, b, *ex.operands())
    return res[0], (res[1], res[2]), list(res[3:])


def rwkv_scan_bwd(r, lw, k, v, kk, b, saved, dy, ex):
    H, T, N = r.shape
    groups = SCAN_GROUPS_BWD
    nc, hg = T // CHUNK, H // groups
    seq = pl.BlockSpec((hg, CHUNK, N), lambda g, c: (g, nc - 1 - c, 0))
    state = pl.BlockSpec((1, hg, N, N), lambda g, c: (nc - 1 - c, g, 0, 0))

    def body(*refs):
        (r_ref, lw_ref, k_ref, v_ref, kk_ref, b_ref, hs_ref, xs_ref, dy_ref), outs, (dh_scr,), ex_refs = _ex_split(ex, refs, 9, 6)
        g, c = pl.program_id(0), pl.program_id(1)

        @pl.when(jnp.logical_and(g == 0, c == 0))
        def _():
            ex.start(*ex_refs)

        @pl.when(c == 0)
        def _():
            dh_scr[...] = jnp.zeros_like(dh_scr)

        x_known = xs_ref[0]
        _, vjp = jax.vjp(lambda *a: f_rwkv_chunk(*a, x_known=x_known)[:2], hs_ref[0], r_ref[...], lw_ref[...], k_ref[...],
                         v_ref[...], kk_ref[...], b_ref[...])
        d = vjp((dy_ref[...], dh_scr[...]))
        dh_scr[...] = d[0]
        for o_ref, dz in zip(outs, d[1:]):
            o_ref[...] = dz

        @pl.when(jnp.logical_and(g == groups - 1, c == nc - 1))
        def _():
            ex.forward(*ex_refs)
            ex.wait(*ex_refs)

    res = pl.pallas_call(
        body, grid=(groups, nc),
        in_specs=[seq] * 6 + [state, state, seq] + [_ANY] * ex.n,
        out_specs=[seq] * 6 + [_ANY] * ex.n, out_shape=[jax.ShapeDtypeStruct((H, T, N), F32)] * 6 + ex.out_shape(),
        scratch_shapes=[pltpu.VMEM((hg, N, N), F32)] + ex.scratch(),
        compiler_params=_cparams(("arbitrary", "arbitrary")), name="rwkv_scan_bwd")(r, lw, k, v, kk, b, *saved, dy, *ex.operands())
    return list(res[:6]), list(res[6:])


PAIR = 2 * HEAD_DIM


def _f_dilattn(has_prev, q, kc, kp, vc, vp):
    scale = 1.0 / math.sqrt(HEAD_DIM)
    i = lax.broadcasted_iota(jnp.int32, (DIL_BLOCK, DIL_BLOCK), 0)
    j = lax.broadcasted_iota(jnp.int32, (DIL_BLOCK, DIL_BLOCK), 1)
    o, l = jnp.zeros_like(q), jnp.zeros_like(q)
    for h in range(2):
        m = _head_mask(PAIR, h)
        sc = jnp.where(j <= i, mm_nt(q * m, kc) * scale, NEG_INF)
        sp = jnp.where(jnp.logical_and(i <= j, has_prev), mm_nt(q * m, kp) * scale, NEG_INF)
        mx = jnp.maximum(jnp.max(sc, axis=-1, keepdims=True), jnp.max(sp, axis=-1, keepdims=True))
        pc, pp = jnp.exp(sc - mx), jnp.exp(sp - mx)
        den = jnp.sum(pc, axis=-1, keepdims=True) + jnp.sum(pp, axis=-1, keepdims=True)
        o = o + (mm_nn(pc, vc) + mm_nn(pp, vp)) / den * m
        l = l + (mx + jnp.log(den)) * m
    return o, l


def _dil_specs(gi, d):
    blk = (DIL_BLOCK * d, PAIR)
    at = lambda off: (lambda p, n: (n, off + p))
    before = lambda off: (lambda p, n: (jnp.maximum(n - 1, 0), off + p))
    pair0 = 2 * gi
    v0 = DIL_WIDTH // PAIR + pair0
    q = pl.BlockSpec(blk, at(pair0))
    kc, kp = pl.BlockSpec(blk, at(pair0)), pl.BlockSpec(blk, before(pair0))
    vc, vp = pl.BlockSpec(blk, at(v0)), pl.BlockSpec(blk, before(v0))
    out = pl.BlockSpec(blk, at(0))
    return q, kc, kp, vc, vp, out


def _residue_rows(r, d):
    return pl.ds(r, DIL_BLOCK, stride=d) if d > 1 else pl.ds(0, DIL_BLOCK)


def dil_fwd(q, k, kv, gi, d, name):
    T = q.shape[0]
    qs, kc, kp, vc, vp, out = _dil_specs(gi, d)

    def body(q_ref, kc_ref, kp_ref, vc_ref, vp_ref, o_ref, l_ref):
        has_prev = pl.program_id(1) > 0

        def residue(r, carry):
            rows = _residue_rows(r, d)
            o, l = _f_dilattn(has_prev, q_ref[rows, :], kc_ref[rows, :], kp_ref[rows, :], vc_ref[rows, :], vp_ref[rows, :])
            o_ref[rows, :] = o
            l_ref[rows, :] = l
            return carry

        lax.fori_loop(0, d, residue, 0)

    shape = jax.ShapeDtypeStruct((T, 4 * HEAD_DIM), F32)
    return pl.pallas_call(
        body, grid=(2, T // (DIL_BLOCK * d)), in_specs=[qs, kc, kp, vc, vp], out_specs=[out, out], out_shape=[shape, shape],
        compiler_params=_cparams(("parallel", "parallel")), name=name)(q, k, k, kv, kv)


def dil_bwd(q, k, kv, do, dl, gi, d, name):
    T = q.shape[0]
    qs, kc, kp, vc, vp, out = _dil_specs(gi, d)

    def body(q_ref, kc_ref, kp_ref, vc_ref, vp_ref, do_ref, dl_ref, *outs):
        f = functools.partial(_f_dilattn, pl.program_id(1) > 0)

        def residue(r, carry):
            rows = _residue_rows(r, d)
            _, vjp = jax.vjp(f, q_ref[rows, :], kc_ref[rows, :], kp_ref[rows, :], vc_ref[rows, :], vp_ref[rows, :])
            for o_ref, g in zip(outs, vjp((do_ref[rows, :], dl_ref[rows, :]))):
                o_ref[rows, :] = g
            return carry

        lax.fori_loop(0, d, residue, 0)

    shape = jax.ShapeDtypeStruct((T, 4 * HEAD_DIM), F32)
    dq, dkc, dkp, dvc, dvp = pl.pallas_call(
        body, grid=(2, T // (DIL_BLOCK * d)), in_specs=[qs, kc, kp, vc, vp, out, out], out_specs=[out] * 5, out_shape=[shape] * 5,
        compiler_params=_cparams(("parallel", "parallel")), name=name)(q, k, k, kv, kv, do, dl)

    def own_plus_next(c, p):
        return c + jnp.concatenate([p[DIL_BLOCK * d:], jnp.zeros_like(p[:DIL_BLOCK * d])], axis=0)

    return dq, own_plus_next(dkc, dkp), own_plus_next(dvc, dvp)


CONV_TILE = 128


def _conv3(u, h6, h7, w, b):
    rows = lax.broadcasted_iota(jnp.int32, (u.shape[0], 1), 0)
    s1 = jnp.where(rows == 0, h7, pltpu.roll(u, 1, 0))
    s2 = jnp.where(rows == 0, h6, jnp.where(rows == 1, h7, pltpu.roll(u, 2, 0)))
    return b + w[0:1] * s2 + w[1:2] * s1 + w[2:3] * u, s1, s2


def _conv_halves(u_ref, h_ref, cw_ref, cb_ref):
    F = D_FF
    first = pl.program_id(0) > 0
    res = []
    for lo in (0, F):
        h = h_ref[:, lo:lo + F]
        h6 = jnp.where(first, h[6:7], 0.0)
        h7 = jnp.where(first, h[7:8], 0.0)
        u = u_ref[:, lo:lo + F]
        res.append((u,) + _conv3(u, h6, h7, cw_ref[:, lo:lo + F], cb_ref[:, lo:lo + F]))
    return res


def _halo_before(C):
    return pl.BlockSpec((8, C), lambda i: (jnp.maximum(i * (CONV_TILE // 8) - 1, 0), 0))


def convgate_fwd(u, cw, cb, name):
    T, C = u.shape
    F = C // 2

    def body(u_ref, h_ref, cw_ref, cb_ref, z_ref):
        (_, cg, _, _), (_, cv, _, _) = _conv_halves(u_ref, h_ref, cw_ref, cb_ref)
        z_ref[...] = (cg * _sigmoid(cg) * cv).astype(BF16)

    return pl.pallas_call(
        body, grid=(T // CONV_TILE,),
        in_specs=[pl.BlockSpec((CONV_TILE, C), lambda i: (i, 0)), _halo_before(C), _full_spec(cw), _full_spec(cb)],
        out_specs=pl.BlockSpec((CONV_TILE, F), lambda i: (i, 0)), out_shape=jax.ShapeDtypeStruct((T, F), BF16),
        compiler_params=_cparams(("parallel",)), name=name)(u, u, cw, cb)


def convgate_bwd_c(u, cw, cb, dz, name):
    T, C = u.shape
    F = C // 2

    def body(u_ref, h_ref, cw_ref, cb_ref, dz_ref, dc_ref, dcw_ref, dcb_ref):
        (ug, cg, g1, g2), (uv, cv, v1, v2) = _conv_halves(u_ref, h_ref, cw_ref, cb_ref)
        dz = dz_ref[...]
        sg = _sigmoid(cg)
        dgate = dz * cv * sg * (1.0 + cg * (1.0 - sg))
        dval = dz * cg * sg
        dc_ref[:, :F] = dgate
        dc_ref[:, F:] = dval

        @pl.when(pl.program_id(0) == 0)
        def _():
            dcw_ref[...] = jnp.zeros_like(dcw_ref)
            dcb_ref[...] = jnp.zeros_like(dcb_ref)

        for lo, d, u0, s1, s2 in ((0, dgate, ug, g1, g2), (F, dval, uv, v1, v2)):
            dcb_ref[:, lo:lo + F] += jnp.sum(d, axis=0, keepdims=True)
            dcw_ref[0:1, lo:lo + F] += jnp.sum(d * s2, axis=0, keepdims=True)
            dcw_ref[1:2, lo:lo + F] += jnp.sum(d * s1, axis=0, keepdims=True)
            dcw_ref[2:3, lo:lo + F] += jnp.sum(d * u0, axis=0, keepdims=True)

    return pl.pallas_call(
        body, grid=(T // CONV_TILE,),
        in_specs=[pl.BlockSpec((CONV_TILE, C), lambda i: (i, 0)), _halo_before(C), _full_spec(cw), _full_spec(cb),
                  pl.BlockSpec((CONV_TILE, F), lambda i: (i, 0))],
        out_specs=[pl.BlockSpec((CONV_TILE, C), lambda i: (i, 0)), _full_spec(cw), _full_spec(cb)],
        out_shape=[jax.ShapeDtypeStruct((T, C), F32), jax.ShapeDtypeStruct(cw.shape, F32), jax.ShapeDtypeStruct(cb.shape, F32)],
        compiler_params=_cparams(("arbitrary",)), name=name)(u, u, cw, cb, dz)


def convgate_bwd_u(dc, cw, name):
    T, C = dc.shape
    n = T // CONV_TILE

    def body(dc_ref, nx_ref, cw_ref, du_ref):
        last = pl.program_id(0) < n - 1
        rows = lax.broadcasted_iota(jnp.int32, (CONV_TILE, 1), 0)
        for lo in (0, C // 2):
            sl = slice(lo, lo + C // 2)
            d = dc_ref[:, sl]
            n0 = jnp.where(last, nx_ref[0:1, sl], 0.0)
            n1 = jnp.where(last, nx_ref[1:2, sl], 0.0)
            up1 = jnp.where(rows == CONV_TILE - 1, n0, pltpu.roll(d, CONV_TILE - 1, 0))
            up2 = jnp.where(rows == CONV_TILE - 1, n1, jnp.where(rows == CONV_TILE - 2, n0, pltpu.roll(d, CONV_TILE - 2, 0)))
            du_ref[:, sl] = (cw_ref[2:3, sl] * d + cw_ref[1:2, sl] * up1 + cw_ref[0:1, sl] * up2).astype(BF16)

    nxt = pl.BlockSpec((8, C), lambda i: (jnp.minimum((i + 1) * (CONV_TILE // 8), T // 8 - 1), 0))
    return pl.pallas_call(
        body, grid=(n,), in_specs=[pl.BlockSpec((CONV_TILE, C), lambda i: (i, 0)), nxt, _full_spec(cw)],
        out_specs=pl.BlockSpec((CONV_TILE, C), lambda i: (i, 0)), out_shape=jax.ShapeDtypeStruct((T, C), BF16),
        compiler_params=_cparams(("parallel",)), name=name)(dc, dc, cw)


def loss_head(y, tgt):
    T, D = y.shape
    tile = 256

    def body(y_ref, t_ref, l_ref, d_ref, db_ref):
        d = y_ref[...] - t_ref[...]
        d_ref[...] = d * (1.0 / D)
        db_ref[...] = (d * (1.0 / D)).astype(BF16)

        @pl.when(pl.program_id(0) == 0)
        def _():
            l_ref[...] = jnp.zeros_like(l_ref)

        l_ref[...] += (0.5 / D) * jnp.sum(d * d)

    row = pl.BlockSpec((tile, D), lambda i: (i, 0))
    return pl.pallas_call(
        body, grid=(T // tile,), in_specs=[row, row], out_specs=[pl.BlockSpec((8, 128), lambda i: (0, 0)), row, row],
        out_shape=[jax.ShapeDtypeStruct((8, 128), F32), jax.ShapeDtypeStruct((T, D), F32), jax.ShapeDtypeStruct((T, D), BF16)],
        compiler_params=_cparams(("arbitrary",)), name="loss_head")(y, tgt)


def sum_parts(parts, name):
    S, R, C = parts.shape
    tile = _pick(R, (256, 128, 64, 32, 16, 8))

    def body(p_ref, o_ref):
        acc = p_ref[0]
        for s in range(1, S):
            acc = acc + p_ref[s]
        o_ref[...] = acc

    return pl.pallas_call(
        body, grid=(R // tile,), in_specs=[pl.BlockSpec((S, tile, C), lambda i: (0, i, 0))],
        out_specs=pl.BlockSpec((tile, C), lambda i: (i, 0)), out_shape=jax.ShapeDtypeStruct((R, C), F32),
        compiler_params=_cparams(("parallel",)), name=name)(parts)


def adamw(gparts, w, m, v, name):
    S, R, C = gparts.shape
    tile = _pick(R, (256, 128, 64, 32, 16, 8))
    c1 = 1.0 / (1.0 - ADAM_B1 ** ADAM_STEP)
    c2 = 1.0 / (1.0 - ADAM_B2 ** ADAM_STEP)

    def body(g_ref, w_ref, m_ref, v_ref, go_ref, d_ref, mo_ref, vo_ref):
        g = g_ref[0].astype(F32)
        for s in range(1, S):
            g = g + g_ref[s].astype(F32)
        m1 = ADAM_B1 * m_ref[...] + (1.0 - ADAM_B1) * g
        v1 = ADAM_B2 * v_ref[...] + (1.0 - ADAM_B2) * (g * g)
        go_ref[...] = g
        mo_ref[...] = m1
        vo_ref[...] = v1
        d_ref[...] = -ADAM_LR * ((m1 * c1) / (jnp.sqrt(v1 * c2) + ADAM_EPS) + ADAM_WD * w_ref[...])

    row = pl.BlockSpec((tile, C), lambda i: (i, 0))
    return pl.pallas_call(
        body, grid=(R // tile,), in_specs=[pl.BlockSpec((S, tile, C), lambda i: (0, i, 0)), row, row, row],
        out_specs=[row] * 4, out_shape=[jax.ShapeDtypeStruct((R, C), F32)] * 4,
        compiler_params=_cparams(("parallel",)), name=name)(gparts, w, m, v)


def _peers():
    x, y, c = lax.axis_index("x"), lax.axis_index("y"), lax.axis_index("c")
    peers = []
    for k in range(1, N_DEV):
        px = 1 - x if k & 4 else x
        py = 1 - y if k & 2 else y
        pc = 1 - c if k & 1 else c
        peers.append(((px, py, pc), 4 * px + 2 * py + pc))
    return 4 * x + 2 * y + c, peers


_ANY = pl.BlockSpec(memory_space=pl.ANY)


class Exchange:
    def __init__(self, gathers=(), scatters=()):
        self.gathers, self.scatters = list(gathers), list(scatters)
        self.n = len(self.gathers) + len(self.scatters)

    def operands(self):
        return self.gathers + self.scatters

    def out_shape(self):
        return ([jax.ShapeDtypeStruct((N_DEV,) + x.shape, x.dtype) for x in self.gathers]
                + [jax.ShapeDtypeStruct(x.shape, x.dtype) for x in self.scatters])

    def scratch(self):
        n = max(self.n, 1)
        return [pltpu.SemaphoreType.DMA((7 * n,)), pltpu.SemaphoreType.DMA((7 * n,)), pltpu.SemaphoreType.DMA((n,))]

    def _copies(self, in_refs, out_refs, send_sems, recv_sems, local_sems):
        me, peers = _peers()
        ng = len(self.gathers)
        local, sends, recvs = [], [], []
        for a in range(self.n):
            x, o = in_refs[a], out_refs[a]
            mine = x if a < ng else x.at[me]
            local.append(pltpu.make_async_copy(mine, o.at[me], local_sems.at[a]))
            s_a, r_a = {}, {}
            for k in range(1, N_DEV):
                peer, slot = peers[k - 1]
                sems = dict(send_sem=send_sems.at[7 * a + k - 1], recv_sem=recv_sems.at[7 * a + k - 1],
                            device_id_type=pl.DeviceIdType.MESH)
                if a >= ng:
                    s_a[k] = pltpu.make_async_remote_copy(src_ref=x.at[slot], dst_ref=o.at[me], device_id=peer, **sems)
                elif k in FORWARDED:
                    came = o.at[peers[k - 2][1]]
                    s_a[k] = pltpu.make_async_remote_copy(src_ref=came, dst_ref=came, device_id=peers[0][0], **sems)
                else:
                    s_a[k] = pltpu.make_async_remote_copy(src_ref=x, dst_ref=o.at[me], device_id=peer, **sems)
                r_a[k] = pltpu.make_async_remote_copy(src_ref=mine, dst_ref=o.at[slot], device_id=peer, **sems)
            sends.append(s_a)
            recvs.append(r_a)
        return local, sends, recvs

    def start(self, *refs):
        if self.n == 0:
            return
        local, sends, _ = self._copies(*refs)
        for a in range(self.n):
            local[a].start()
            for k in range(1, N_DEV):
                if a >= len(self.gathers) or k not in FORWARDED:
                    sends[a][k].start()

    def forward(self, *refs):
        if not self.gathers:
            return
        _, sends, recvs = self._copies(*refs)
        for a in range(len(self.gathers)):
            for k in FORWARDED:
                recvs[a][k - 1].wait_recv()
                sends[a][k].start()

    def wait(self, *refs):
        if self.n == 0:
            return
        local, sends, recvs = self._copies(*refs)
        for a in range(self.n):
            waited_early = [f - 1 for f in FORWARDED] if a < len(self.gathers) else []
            for k in range(1, N_DEV):
                if k not in waited_early:
                    recvs[a][k].wait_recv()
            for k in range(1, N_DEV):
                sends[a][k].wait_send()
            local[a].wait()


FORWARDED = (3, 5, 7)


def exchange(ex, name):
    n = ex.n

    def body(*refs):
        args = (refs[:n], refs[n:2 * n]) + tuple(refs[2 * n:])
        ex.start(*args)
        ex.forward(*args)
        ex.wait(*args)

    return pl.pallas_call(body, in_specs=[_ANY] * n, out_specs=[_ANY] * n, out_shape=ex.out_shape(),
                          scratch_shapes=ex.scratch(), name=name)(*ex.operands())


def _heads(z, h):
    return z.reshape(z.shape[0], h, HEAD_DIM).transpose(1, 0, 2)


def _unheads(z):
    return z.transpose(1, 0, 2).reshape(z.shape[1], z.shape[0] * HEAD_DIM)


def _shift_up(z):
    return jnp.concatenate([z[1:], jnp.zeros_like(z[:1])], axis=0)


def _segments(width):
    seg = np.zeros((width, 128), np.float32)
    seg[np.arange(width), np.arange(width) // HEAD_DIM] = 1.0
    return jnp.asarray(seg), jnp.asarray(seg.T)


def _rope_consts(T, heads):
    inv = ROPE_THETA ** (-jnp.arange(0, HEAD_DIM, 2, dtype=F32) / HEAD_DIM)
    ang = jnp.arange(T, dtype=F32)[:, None] * inv[None, :]
    cos, sin = jnp.cos(ang), jnp.sin(ang)
    rot = np.zeros((HEAD_DIM, HEAD_DIM), np.float32)
    half = HEAD_DIM // 2
    rot[np.arange(half) + half, np.arange(half)] = -1.0
    rot[np.arange(half), np.arange(half) + half] = 1.0
    return (jnp.tile(cos, (1, 2 * heads)), jnp.tile(sin, (1, 2 * heads)), jnp.asarray(np.kron(np.eye(heads, dtype=np.float32), rot)))


def _per_head(g, heads):
    return jnp.tile(g.reshape(1, HEAD_DIM), (1, heads))


def _sum_heads(g):
    return g.reshape(-1, HEAD_DIM).sum(axis=0, keepdims=True)


LORA_COLS = 256
RW_TILE = 128
ROW_TILE = 256


def _local_step(x0, memx, tgt, P, ex_weights=None, weights_done=None, ex_grads=None):
    T = x0.shape[0]
    P = dict(P)
    G = {}
    seg, seg_t = _segments(RWKV_WIDTH)
    mseg = (seg[:MEM_WIDTH], seg_t[:, :MEM_WIDTH])
    cos, sin, rot = _rope_consts(T, DIL_WIDTH // HEAD_DIM)
    row = lambda v: v.reshape(1, -1)

    def mem_fwd(i, q):
        memn = stage_fwd(f_rmsnorm, [memx], [P["mem_norm"][i:i + 1]], [], [], N_MEM, f"mem{i}_norm", [BF16])[0]
        kvm = matmul(memn, P["mem_w_kv"][i], "nn", f"mem{i}_kv")
        kn, qn = _per_head(P["mem_k_norm"][i], MEM_HEADS), _per_head(P["mem_q_norm"][i], MEM_HEADS)
        km = stage_fwd(f_headnorm, [Cols(kvm, MEM_WIDTH, 0)], [kn], [], mseg, N_MEM, f"mem{i}_knorm")[0]
        om = stage_fwd(f_memattn, [q], [km, Cols(kvm, MEM_WIDTH, 1), qn], [], mseg, ROW_TILE, f"mem{i}_attn")[0]
        return om, (memn, kvm, km, kn, qn, q)

    def mem_bwd(i, saved, dymem):
        memn, kvm, km, kn, qn, q = saved
        (dq,), (dkm, dvm, g_qn) = stage_bwd(f_memattn, [q], [km, Cols(kvm, MEM_WIDTH, 1), qn], [], mseg, [dymem], ROW_TILE,
                                            f"mem{i}_attn_bwd")
        (dkraw,), (g_kn,) = stage_bwd(f_headnorm, [Cols(kvm, MEM_WIDTH, 0)], [kn], [], mseg, [dkm], N_MEM, f"mem{i}_knorm_bwd")
        dkvm = jnp.concatenate([dkraw, dvm], axis=1).astype(BF16)
        g_w = matmul(memn, dkvm, "tn", f"mem{i}_kv_dw")
        dmemn = matmul(dkvm, P["mem_w_kv"][i], "nt", f"mem{i}_kv_dx")
        _, (g_mn,) = stage_bwd(f_rmsnorm, [memx], [P["mem_norm"][i:i + 1]], [], [], [dmemn], N_MEM, f"mem{i}_norm_bwd")
        return dq, g_mn, g_w, _sum_heads(g_qn), _sum_heads(g_kn)

    def ffn_fwd(i, xin):
        hn = stage_fwd(f_rmsnorm, [xin], [P["ffn_norm"][i:i + 1]], [], [], ROW_TILE, f"ffn{i}_norm", [BF16])[0]
        u = matmul(hn, P["ffn_w_up"][i], "nn", f"ffn{i}_up")
        z = convgate_fwd(u, P["ffn_conv_w"][i], P["ffn_conv_b"][i:i + 1], f"ffn{i}_conv")
        return matmul(z, P["ffn_w_down"][i], "nn", f"ffn{i}_down", residual=xin), (hn, u, z)

    def ffn_bwd(i, xin, saved, dxo, dxo_b):
        hn, u, z = saved
        dz = matmul(dxo_b, P["ffn_w_down"][i], "nt", f"ffn{i}_down_dx")
        g_down = matmul(z, dxo_b, "tn", f"ffn{i}_down_dw")
        dc, g_cw, g_cb = convgate_bwd_c(u, P["ffn_conv_w"][i], P["ffn_conv_b"][i:i + 1], dz, f"ffn{i}_conv_bwd_c")
        du = convgate_bwd_u(dc, P["ffn_conv_w"][i], f"ffn{i}_conv_bwd_u")
        dhn = matmul(du, P["ffn_w_up"][i], "nt", f"ffn{i}_up_dx")
        g_up = matmul(hn, du, "tn", f"ffn{i}_up_dw")
        (dxin,), (g_n,), (dxin_b,) = stage_bwd(f_rmsnorm_res, [xin], [P["ffn_norm"][i:i + 1]], [], [], [dhn, dxo], ROW_TILE,
                                               f"ffn{i}_norm_bwd", bf16_copies=(0,))
        return dxin, dxin_b, g_n, g_up, g_cw, g_cb, g_down

    h0 = stage_fwd(f_rmsnorm, [x0], [P["attn_norm"][0:1]], [], [], ROW_TILE, "l0_norm", [BF16])[0]
    p0 = matmul(h0, P["a_w_in"][0], "nn", "l0_in")
    lora0 = 3 * RWKV_WIDTH // LORA_COLS
    pre_xs = [Cols(p0, RWKV_WIDTH, 0), Cols(p0, RWKV_WIDTH, 1), Cols(p0, RWKV_WIDTH, 2), Cols(p0, LORA_COLS, lora0)]
    mu = [Cols(P["a_mu"], RWKV_WIDTH, 0), Cols(P["a_mu"], RWKV_WIDTH, 1), Cols(P["a_mu"], RWKV_WIDTH, 2),
          Cols(P["a_mu"], LORA_COLS, lora0)]
    lora_rows = lambda w, lo: jnp.pad(w, ((lo, LORA_COLS - lo - w.shape[0]), (0, 0)))
    pre_ps = mu + [P["a_w0"], lora_rows(P["a_w2"][0], 0), P["a_a0"], lora_rows(P["a_a2"][0], 64), lora_rows(P["a_g2"][0], 128),
                   P["a_k_k"], P["a_k_a"]]
    r, lw, k2, v, kk, b, g = stage_fwd(f_rwkv_pre, pre_xs, pre_ps, [], [seg, seg_t], RW_TILE, "l0_rwkv_pre", with_prev=True)
    hm = lambda z: _heads(z, RWKV_HEADS)
    scan_in = [hm(z) for z in (r, lw, k2, v, kk, b)]
    y_h, h_states, got = rwkv_scan_fwd(*scan_in, ex_weights or Exchange())
    if weights_done is not None:
        P.update(weights_done(got))
    y_s = _unheads(y_h)
    post_ps = [P["a_lnx_w"], P["a_lnx_b"], P["a_r_k"].reshape(1, RWKV_WIDTH)]
    ymix0 = stage_fwd(f_rwkv_post, [y_s, r, k2, v, g], post_ps, [], [seg, seg_t], RW_TILE, "l0_rwkv_post")[0]
    ymem0, mem0_saved = mem_fwd(0, Cols(p0, MEM_WIDTH, SHIFT_WIDTH // MEM_WIDTH))
    ycat0 = jnp.concatenate([ymix0, ymem0], axis=1).astype(BF16)
    x1 = matmul(ycat0, P["a_w_out"][0], "nn", "l0_out", residual=x0)
    x2, ffn0_saved = ffn_fwd(0, x1)

    hk, h1 = stage_fwd(f_rmsnorm2, [x2], [row(P["kv_norm"]), P["attn_norm"][1:2]], [], [], ROW_TILE, "l1_norm", [BF16, BF16])
    kvp = matmul(hk, P["kv_w"][0], "nn", "l1_kv")
    p1 = matmul(h1, P["b_w_in"][0], "nn", "l1_in")
    kraw, qraw = Cols(kvp, DIL_WIDTH, 0), Cols(p1, DIL_WIDTH, 0)
    kgain, qgain = _per_head(P["kv_k_norm"], DIL_WIDTH // HEAD_DIM), _per_head(P["b_q_norm"], DIL_WIDTH // HEAD_DIM)
    ksh = stage_fwd(f_qkprep, [kraw], [kgain], [cos, sin], [seg, seg_t, rot], ROW_TILE, "l1_kprep")[0]
    q = stage_fwd(f_qkprep, [qraw], [qgain], [cos, sin], [seg, seg_t, rot], ROW_TILE, "l1_qprep")[0]
    outs, lses = [], []
    for gi, (_, d) in enumerate(DIL_GROUPS):
        og, lg = dil_fwd(q, ksh, kvp, gi, d, f"l1_dil{gi}")
        outs.append(og)
        lses.append(lg)
    omix = stage_fwd(f_mix, outs + lses, [], [], [], ROW_TILE, "l1_mix")[0]
    ymem1, mem1_saved = mem_fwd(1, Cols(p1, MEM_WIDTH, DIL_WIDTH // MEM_WIDTH))
    ycat1 = jnp.concatenate([omix, ymem1], axis=1).astype(BF16)
    x3 = matmul(ycat1, P["b_w_out"][0], "nn", "l1_out", residual=x2)
    x4, ffn1_saved = ffn_fwd(1, x3)
    loss_part, dx4, dx4_b = loss_head(x4, tgt)

    dx3, dx3_b, gn1, gup1, gcw1, gcb1, gdown1 = ffn_bwd(1, x3, ffn1_saved, dx4, dx4_b)
    dycat1 = matmul(dx3_b, P["b_w_out"][0], "nt", "l1_out_dx")
    G["b_w_out"] = [matmul(ycat1, dx3_b, "tn", "l1_out_dw")]
    dqmem1, gmn1, gmw1, gmq1, gmk1 = mem_bwd(1, mem1_saved, Cols(dycat1, MEM_WIDTH, 1))
    dmix, _ = stage_bwd(f_mix, outs + lses, [], [], [], [Cols(dycat1, MEM_WIDTH, 0)], ROW_TILE, "l1_mix_bwd")
    dq, dk, dv = zip(*[dil_bwd(q, ksh, kvp, dmix[gi], dmix[3 + gi], gi, d, f"l1_dil{gi}_bwd")
                       for gi, (_, d) in enumerate(DIL_GROUPS)])
    dq, dk, dv = jnp.concatenate(dq, axis=1), jnp.concatenate(dk, axis=1), jnp.concatenate(dv, axis=1)
    (dqraw,), (g_bq,) = stage_bwd(f_qkprep, [qraw], [qgain], [cos, sin], [seg, seg_t, rot], [dq], ROW_TILE, "l1_qprep_bwd")
    (dkraw,), (g_kk,) = stage_bwd(f_qkprep, [kraw], [kgain], [cos, sin], [seg, seg_t, rot], [dk], ROW_TILE, "l1_kprep_bwd")
    g_bq, g_kk = _sum_heads(g_bq), _sum_heads(g_kk)
    dp1 = jnp.concatenate([dqraw, dqmem1], axis=1).astype(BF16)
    dkvp = jnp.concatenate([dkraw, dv], axis=1).astype(BF16)
    dh1 = matmul(dp1, P["b_w_in"][0], "nt", "l1_in_dx")
    G["b_w_in"] = [matmul(h1, dp1, "tn", "l1_in_dw")]
    dhk = matmul(dkvp, P["kv_w"][0], "nt", "l1_kv_dx")
    G["kv_w"] = [matmul(hk, dkvp, "tn", "l1_kv_dw")]
    (dx2,), (g_kvn, g_an1), (dx2_b,) = stage_bwd(f_rmsnorm2_res, [x2], [row(P["kv_norm"]), P["attn_norm"][1:2]], [], [],
                                                 [dhk, dh1, dx3], ROW_TILE, "l1_norm_bwd", bf16_copies=(0,))

    dx1, dx1_b, gn0, gup0, gcw0, gcb0, gdown0 = ffn_bwd(0, x1, ffn0_saved, dx2, dx2_b)
    dycat0 = matmul(dx1_b, P["a_w_out"][0], "nt", "l0_out_dx")
    G["a_w_out"] = [matmul(ycat0, dx1_b, "tn", "l0_out_dw")]
    dqmem0, gmn0, gmw0, gmq0, gmk0 = mem_bwd(0, mem0_saved, Cols(dycat0, MEM_WIDTH, RWKV_WIDTH // MEM_WIDTH))
    (dy_s, dr_a, dk_a, dv_a, dg), (g_lw, g_lb, g_rk) = stage_bwd(
        f_rwkv_post, [y_s, r, k2, v, g], post_ps, [], [seg, seg_t], [Cols(dycat0, RWKV_WIDTH, 0)], RW_TILE, "l0_rwkv_post_bwd")
    G["mem_w_kv"], G["ffn_w_up"], G["ffn_w_down"] = [gmw0, gmw1], [gup0, gup1], [gdown0, gdown1]
    d_scan, G["_exchanged"] = rwkv_scan_bwd(*scan_in, h_states, hm(dy_s), ex_grads(G) if ex_grads else Exchange())
    dr_b, dlw, dk_b, dv_b, dkk, db = [_unheads(z) for z in d_scan]
    dpre, gpre = stage_bwd(f_rwkv_pre, pre_xs, pre_ps, [], [seg, seg_t],
                           [[dr_a, dr_b], dlw, [dk_a, dk_b], [dv_a, dv_b], dkk, db, dg], RW_TILE, "l0_rwkv_pre_bwd", with_prev=True)
    dp_rw = jnp.concatenate(dpre[:4], axis=1) + _shift_up(jnp.concatenate(dpre[4:], axis=1))
    dp0 = jnp.concatenate([dp_rw, dqmem0], axis=1).astype(BF16)
    dh0 = matmul(dp0, P["a_w_in"][0], "nt", "l0_in_dx")
    G["a_w_in"] = [matmul(h0, dp0, "tn", "l0_in_dw")]
    (dx0,), (g_an0,) = stage_bwd(f_rmsnorm_res, [x0], [P["attn_norm"][0:1]], [], [], [dh0, dx1], ROW_TILE, "l0_norm_bwd")

    G["attn_norm"] = jnp.concatenate([g_an0, g_an1], axis=0)
    G["a_mu"] = jnp.concatenate(gpre[:4], axis=1)
    G["a_w0"], G["a_w2"], G["a_a0"], G["a_a2"], G["a_g2"] = gpre[4], gpre[5][None, :64], gpre[6], gpre[7][None, 64:128], gpre[8][None, 128:]
    G["a_k_k"], G["a_k_a"] = gpre[9], gpre[10]
    G["a_r_k"] = g_rk.reshape(1, RWKV_HEADS, HEAD_DIM)
    G["a_lnx_w"], G["a_lnx_b"] = g_lw, g_lb
    G["kv_norm"], G["kv_k_norm"], G["b_q_norm"] = g_kvn.reshape(-1), g_kk.reshape(-1), g_bq
    G["mem_norm"] = jnp.concatenate([gmn0, gmn1], axis=0)
    G["mem_w_kv"] = [gmw0, gmw1]
    G["mem_q_norm"] = jnp.concatenate([gmq0, gmq1], axis=0)
    G["mem_k_norm"] = jnp.concatenate([gmk0, gmk1], axis=0)
    G["ffn_norm"] = jnp.concatenate([gn0, gn1], axis=0)
    G["ffn_w_up"] = [gup0, gup1]
    G["ffn_conv_w"] = jnp.stack([gcw0, gcw1])
    G["ffn_conv_b"] = jnp.concatenate([gcb0, gcb1], axis=0)
    G["ffn_w_down"] = [gdown0, gdown1]
    return loss_part, dx0, G


PARAMS = (("attn_norm", None), ("a_w_in", 2), ("a_mu", 1), ("a_w0", 1), ("a_w2", 2), ("a_a0", 1), ("a_a2", 2), ("a_g2", 2),
          ("a_k_k", 1), ("a_k_a", 1), ("a_r_k", None), ("a_lnx_w", 1), ("a_lnx_b", 1), ("a_w_out", 1), ("kv_norm", None),
          ("kv_w", 1), ("kv_k_norm", None), ("b_w_in", 1), ("b_q_norm", None), ("b_w_out", 2), ("mem_norm", None),
          ("mem_w_kv", 1), ("mem_q_norm", None), ("mem_k_norm", None), ("ffn_norm", None), ("ffn_w_up", 2),
          ("ffn_conv_w", 2), ("ffn_conv_b", None), ("ffn_w_down", 1))
BIG = ("a_w_in", "a_w_out", "kv_w", "b_w_in", "b_w_out", "mem_w_kv", "ffn_w_up", "ffn_w_down")
AXIS = dict(PARAMS)
SMALL = tuple(n for n, _ in PARAMS if n not in BIG)
SMALL_SHARDED = tuple(n for n in SMALL if AXIS[n] is not None)
PACK_QUANTUM = 256 * 128


def _from_shards(xs, axis):
    full = jnp.moveaxis(xs, 0, axis)
    sh = full.shape
    return full.reshape(sh[:axis] + (sh[axis] * sh[axis + 1],) + sh[axis + 2:])


def _to_shards(g, axis):
    sh = g.shape
    return jnp.moveaxis(g.reshape(sh[:axis] + (N_DEV, sh[axis] // N_DEV) + sh[axis + 1:]), axis, 0)


def _pack(parts, lead=0):
    ld = parts[0].shape[:lead]
    flat = jnp.concatenate([p.reshape(ld + (-1,)) for p in parts], axis=-1)
    pad = (-flat.shape[-1]) % PACK_QUANTUM
    flat = jnp.pad(flat, [(0, 0)] * lead + [(0, pad)])
    return flat.reshape(ld + (-1, 128))


def _unpack(packed, shapes, lead=0):
    ld = packed.shape[:lead]
    flat = packed.reshape(ld + (-1,))
    out, off = [], 0
    for s in shapes:
        n = math.prod(s)
        out.append(flat[..., off:off + n].reshape(ld + tuple(s)))
        off += n
    return out


def kernel(x, mem, attn_norm, a_w_in, a_mu, a_w0, a_w2, a_a0, a_a2, a_g2, a_k_k, a_k_a, a_r_k, a_lnx_w, a_lnx_b, a_w_out, kv_norm, kv_w, kv_k_norm, b_w_in, b_q_norm, b_w_out, mem_norm, mem_w_kv, mem_q_norm, mem_k_norm, ffn_norm, ffn_w_up, ffn_conv_w, ffn_conv_b, ffn_w_down, loss_target, m_attn_norm, m_a_w_in, m_a_mu, m_a_w0, m_a_w2, m_a_a0, m_a_a2, m_a_g2, m_a_k_k, m_a_k_a, m_a_r_k, m_a_lnx_w, m_a_lnx_b, m_a_w_out, m_kv_norm, m_kv_w, m_kv_k_norm, m_b_w_in, m_b_q_norm, m_b_w_out, m_mem_norm, m_mem_w_kv, m_mem_q_norm, m_mem_k_norm, m_ffn_norm, m_ffn_w_up, m_ffn_conv_w, m_ffn_conv_b, m_ffn_w_down, v_attn_norm, v_a_w_in, v_a_mu, v_a_w0, v_a_w2, v_a_a0, v_a_a2, v_a_g2, v_a_k_k, v_a_k_a, v_a_r_k, v_a_lnx_w, v_a_lnx_b, v_a_w_out, v_kv_norm, v_kv_w, v_kv_k_norm, v_b_w_in, v_b_q_norm, v_b_w_out, v_mem_norm, v_mem_w_kv, v_mem_q_norm, v_mem_k_norm, v_ffn_norm, v_ffn_w_up, v_ffn_conv_w, v_ffn_conv_b, v_ffn_w_down):
    names = [n for n, _ in PARAMS]
    vals = (attn_norm, a_w_in, a_mu, a_w0, a_w2, a_a0, a_a2, a_g2, a_k_k, a_k_a, a_r_k, a_lnx_w, a_lnx_b, a_w_out, kv_norm, kv_w, kv_k_norm, b_w_in, b_q_norm, b_w_out, mem_norm, mem_w_kv, mem_q_norm, mem_k_norm, ffn_norm, ffn_w_up, ffn_conv_w, ffn_conv_b, ffn_w_down)
    m_vals = (m_attn_norm, m_a_w_in, m_a_mu, m_a_w0, m_a_w2, m_a_a0, m_a_a2, m_a_g2, m_a_k_k, m_a_k_a, m_a_r_k, m_a_lnx_w, m_a_lnx_b, m_a_w_out, m_kv_norm, m_kv_w, m_kv_k_norm, m_b_w_in, m_b_q_norm, m_b_w_out, m_mem_norm, m_mem_w_kv, m_mem_q_norm, m_mem_k_norm, m_ffn_norm, m_ffn_w_up, m_ffn_conv_w, m_ffn_conv_b, m_ffn_w_down)
    v_vals = (v_attn_norm, v_a_w_in, v_a_mu, v_a_w0, v_a_w2, v_a_a0, v_a_a2, v_a_g2, v_a_k_k, v_a_k_a, v_a_r_k, v_a_lnx_w, v_a_lnx_b, v_a_w_out, v_kv_norm, v_kv_w, v_kv_k_norm, v_b_w_in, v_b_q_norm, v_b_w_out, v_mem_norm, v_mem_w_kv, v_mem_q_norm, v_mem_k_norm, v_ffn_norm, v_ffn_w_up, v_ffn_conv_w, v_ffn_conv_b, v_ffn_w_down)
    W, M, V = dict(zip(names, vals)), dict(zip(names, m_vals)), dict(zip(names, v_vals))
    me = 4 * lax.axis_index("x") + 2 * lax.axis_index("y") + lax.axis_index("c")
    layers = lambda D, n: [D[n]] if D[n].ndim == 2 else [D[n][i] for i in range(D[n].shape[0])]
    ax2 = lambda n: AXIS[n] - (W[n].ndim - 2)
    first = [("a_w_in", 0)]
    later = [(n, i) for n in BIG if n != "a_w_in" for i in range(len(layers(W, n)))]

    small_shapes = [W[n].shape for n in SMALL_SHARDED]
    got_w, got_small = exchange(Exchange(gathers=[W["a_w_in"][0].astype(BF16), _pack([W[n] for n in SMALL_SHARDED])]),
                                "gather_first")
    P = {n: W[n] for n in SMALL}
    P["a_w_in"] = [_from_shards(got_w, ax2("a_w_in"))]
    for n, s in zip(SMALL_SHARDED, _unpack(got_small, small_shapes, lead=1)):
        P[n] = _from_shards(s, AXIS[n])
    ex_weights = Exchange(gathers=[layers(W, n)[i].astype(BF16) for n, i in later])

    def weights_done(got):
        out = {}
        for (n, _), g in zip(later, got):
            out.setdefault(n, []).append(_from_shards(g, ax2(n)))
        return out

    ex_grads = lambda G: Exchange(scatters=[_to_shards(G[n][i], ax2(n)) for n, i in later])
    loss_part, dx0, G = _local_step(x[0], mem[0], loss_target[0], P, ex_weights, weights_done, ex_grads)
    loss = lax.psum(loss_part[0, 0], ("x", "y", "c"))
    gparts = dict(zip(later, G.pop("_exchanged")))
    got_gsmall, gparts[first[0]] = exchange(
        Exchange(gathers=[_pack([G[n] for n in SMALL])], scatters=[_to_shards(G["a_w_in"][0], ax2("a_w_in"))]), "exchange_last")

    results = {}
    for n in BIG:
        per_layer = [adamw(gparts[(n, i)], w, m, v, f"adamw_{n}{i}")
                     for i, (w, m, v) in enumerate(zip(layers(W, n), layers(M, n), layers(V, n)))]
        results[n] = [r[0] if W[n].ndim == 2 else jnp.stack(r) for r in zip(*per_layer)]
    g_small = sum_parts(got_gsmall, "sum_small_grads")
    mine = []
    for n, g in zip(SMALL, _unpack(g_small, [G[n].shape for n in SMALL])):
        if AXIS[n] is not None:
            s = W[n].shape[AXIS[n]]
            g = lax.dynamic_slice_in_dim(g, me * s, s, axis=AXIS[n])
        mine.append(g)
    res = adamw(_pack(mine)[None], _pack([W[n] for n in SMALL]), _pack([M[n] for n in SMALL]), _pack([V[n] for n in SMALL]),
                "adamw_small")
    for n, parts in zip(SMALL, zip(*[_unpack(r, [W[n].shape for n in SMALL]) for r in res])):
        results[n] = list(parts)
    outs = [[results[n][j] for n in names] for j in range(4)]
    return (loss, dx0[None], *outs[0], *outs[1], *outs[2], *outs[3])
```

```python
import functools
import math

import jax
import jax.numpy as jnp
import numpy as np
from jax import lax
from jax.experimental import pallas as pl
from jax.experimental.pallas import tpu as pltpu

F32 = jnp.float32
BF16 = jnp.bfloat16
HI = lax.Precision.HIGHEST
H3 = lax.Precision.HIGH

N_DEV = 8
D_MODEL = 1024
HEAD_DIM = 64
N_MEM = 256
MEM_HEADS = 4
MEM_WIDTH = 256
RWKV_HEADS = 12
RWKV_WIDTH = 768
SHIFT_WIDTH = 2560
DIL_GROUPS = ((128, 1), (512, 4), (2048, 16))
DIL_BLOCK = 128
DIL_WIDTH = 768
D_FF = 2816
RMS_EPS = 1e-6
LNX_EPS = 64e-5
NEG_INF = -1e30
ROPE_THETA = 10000.0
ADAM_LR, ADAM_B1, ADAM_B2, ADAM_EPS, ADAM_WD, ADAM_STEP = 0.001, 0.9, 0.999, 1e-08, 0.01, 10

CHUNK = 64
SCAN_GROUPS_FWD, SCAN_GROUPS_BWD = 1, 1
MM_TILE_CAP = 1408
VMEM_LIMIT_V7X = 48 * 1024 * 1024


def _cparams(sem):
    return pltpu.CompilerParams(dimension_semantics=sem, vmem_limit_bytes=VMEM_LIMIT_V7X)


def _pick(n, cands):
    for c in cands:
        if n % c == 0:
            return c
    return n


def _tile(n, cap):
    if n <= cap:
        return n
    for d in range(cap - cap % 128, 0, -128):
        if n % d == 0:
            return d
    return n


def _dg(a, b, ca, cb, batch):
    dims = (((ca,), (cb,)), ((0,), (0,))) if batch else (((ca,), (cb,)), ((), ()))
    return lax.dot_general(a.astype(BF16), b.astype(BF16), dims, preferred_element_type=F32)


@jax.custom_vjp
def mm_nn(a, b):
    n = a.ndim
    return _dg(a, b, n - 1, n - 2, n == 3)


def _mm_nn_fwd(a, b):
    return mm_nn(a, b), (a, b)


def _mm_nn_bwd(res, g):
    a, b = res
    n = a.ndim
    return _dg(g, b, n - 1, n - 1, n == 3), _dg(a, g, n - 2, n - 2, n == 3)


mm_nn.defvjp(_mm_nn_fwd, _mm_nn_bwd)


@jax.custom_vjp
def mm_nt(a, b):
    n = a.ndim
    return _dg(a, b, n - 1, n - 1, n == 3)


def _mm_nt_fwd(a, b):
    return mm_nt(a, b), (a, b)


def _mm_nt_bwd(res, g):
    a, b = res
    n = a.ndim
    return _dg(g, b, n - 1, n - 2, n == 3), _dg(g, a, n - 2, n - 2, n == 3)


mm_nt.defvjp(_mm_nt_fwd, _mm_nt_bwd)


def mmh(a, b, precision=H3):
    n = a.ndim
    dims = (((n - 1,), (n - 2,)), ((0,), (0,))) if n == 3 else (((1,), (0,)), ((), ()))
    return lax.dot_general(a, b, dims, precision=precision, preferred_element_type=F32)


def mmh_nt(a, b):
    n = a.ndim
    dims = (((n - 1,), (n - 1,)), ((0,), (0,))) if n == 3 else (((1,), (1,)), ((), ()))
    return lax.dot_general(a, b, dims, precision=H3, preferred_element_type=F32)


def mmh_tn(a, b):
    n = a.ndim
    dims = (((n - 2,), (n - 2,)), ((0,), (0,))) if n == 3 else (((0,), (0,)), ((), ()))
    return lax.dot_general(a, b, dims, precision=H3, preferred_element_type=F32)


def matmul(a, b, mode, name, residual=None):
    out_dtype = BF16 if mode == "tn" else F32
    if mode == "nn":
        (M, K), (_, N) = a.shape, b.shape
    elif mode == "nt":
        (M, K), (N, _) = a.shape, b.shape
    else:
        (K, M), (_, N) = a.shape, b.shape
    tm = _tile(M, 2048 if mode == "nn" else MM_TILE_CAP)
    tn = _tile(N, 512 if mode == "nn" else MM_TILE_CAP)
    tk = _tile(K, MM_TILE_CAP if mode != "nt" else 512)
    nk = K // tk
    if mode == "nn":
        a_spec = pl.BlockSpec((tm, tk), lambda i, j, k: (i, k))
        b_spec = pl.BlockSpec((tk, tn), lambda i, j, k: (k, j))
        dims = (((1,), (0,)), ((), ()))
    elif mode == "nt":
        a_spec = pl.BlockSpec((tm, tk), lambda i, j, k: (i, k))
        b_spec = pl.BlockSpec((tn, tk), lambda i, j, k: (j, k))
        dims = (((1,), (1,)), ((), ()))
    else:
        a_spec = pl.BlockSpec((tk, tm), lambda i, j, k: (k, i))
        b_spec = pl.BlockSpec((tk, tn), lambda i, j, k: (k, j))
        dims = (((0,), (0,)), ((), ()))
    o_spec = pl.BlockSpec((tm, tn), lambda i, j, k: (i, j))
    has_res = residual is not None

    def body(*refs):
        if has_res:
            a_ref, b_ref, r_ref, o_ref, acc_ref = refs
        else:
            a_ref, b_ref, o_ref, acc_ref = refs
        k = pl.program_id(2)

        @pl.when(k == 0)
        def _():
            acc_ref[...] = jnp.zeros_like(acc_ref)

        acc_ref[...] += lax.dot_general(a_ref[...].astype(BF16), b_ref[...].astype(BF16), dims,
                                        preferred_element_type=F32)

        @pl.when(k == nk - 1)
        def _():
            if has_res:
                o_ref[...] = (acc_ref[...] + r_ref[...]).astype(out_dtype)
            else:
                o_ref[...] = acc_ref[...].astype(out_dtype)

    ins = [a, b] + ([residual] if has_res else [])
    in_specs = [a_spec, b_spec] + ([o_spec] if has_res else [])
    return pl.pallas_call(
        body, grid=(M // tm, N // tn, nk), in_specs=in_specs, out_specs=o_spec,
        out_shape=jax.ShapeDtypeStruct((M, N), out_dtype), scratch_shapes=[pltpu.VMEM((tm, tn), F32)],
        compiler_params=_cparams(("parallel", "parallel", "arbitrary")), name=name)(*ins)


class Cols:
    def __init__(self, arr, width, idx):
        self.arr, self.width, self.idx = arr, width, idx


def _arr(x):
    return x.arr if isinstance(x, Cols) else x


def _shape(x):
    return x.arr.shape[:-1] + (x.width,) if isinstance(x, Cols) else x.shape


def _col(x):
    return x.idx if isinstance(x, Cols) else 0


def _tok_spec(x, tile):
    shape, col = _shape(x), _col(x)
    return pl.BlockSpec(shape[:-2] + (tile, shape[-1]), lambda i: (0,) * (len(shape) - 2) + (i, col))


def _full_spec(x):
    shape, col = _shape(x), _col(x)
    return pl.BlockSpec(shape, lambda i: (0,) * (len(shape) - 1) + (col,))


def _halo_spec(x, tile):
    shape, col = _shape(x), _col(x)
    return pl.BlockSpec((8, shape[-1]), lambda i: (jnp.maximum(i * (tile // 8) - 1, 0), col))


def _blk(x, tile):
    shape = _shape(x)
    return jax.ShapeDtypeStruct(shape[:-2] + (tile, shape[-1]), _arr(x).dtype)


def _prev_rows(x, halo):
    rows = lax.broadcasted_iota(jnp.int32, (x.shape[0], 1), 0)
    before = jnp.where(pl.program_id(0) > 0, halo[7:8], 0.0)
    return jnp.where(rows == 0, before, pltpu.roll(x, 1, 0))


def stage_fwd(f, xs, ps, cts, cfs, tile, name, out_dtypes=None, with_prev=False):
    xs, ps, cts, cfs = list(xs), list(ps), list(cts), list(cfs)
    halos = xs if with_prev else []
    nx, nh, nct, np_ = len(xs), len(halos), len(cts), len(ps)
    T = _shape(xs[0])[-2]
    blk = [_blk(x, tile) for x in xs]
    out_avals = jax.eval_shape(f, *blk, *(blk if with_prev else []), *[_blk(p, _shape(p)[-2]) for p in ps],
                               *[_blk(c, tile) for c in cts], *[_blk(c, _shape(c)[-2]) for c in cfs])
    if out_dtypes is None:
        out_dtypes = [o.dtype for o in out_avals]
    out_shape = [jax.ShapeDtypeStruct(o.shape[:-2] + (T, o.shape[-1]), dt) for o, dt in zip(out_avals, out_dtypes)]
    n_in = nx + nh + nct + np_ + len(cfs)

    def body(*refs):
        vals = [r[...] for r in refs[:n_in]]
        xv, hv, rest = vals[:nx], vals[nx:nx + nh], vals[nx + nh:]
        ctv, pv, cfv = rest[:nct], rest[nct:nct + np_], rest[nct + np_:]
        prev = [_prev_rows(x, h) for x, h in zip(xv, hv)]
        res = f(*xv, *prev, *pv, *ctv, *cfv)
        for o_ref, r in zip(refs[n_in:], res):
            o_ref[...] = r.astype(o_ref.dtype)

    return pl.pallas_call(
        body, grid=(T // tile,),
        in_specs=([_tok_spec(x, tile) for x in xs] + [_halo_spec(x, tile) for x in halos] + [_tok_spec(c, tile) for c in cts]
                  + [_full_spec(p) for p in ps + cfs]),
        out_specs=[_tok_spec(o, tile) for o in out_shape], out_shape=out_shape,
        compiler_params=_cparams(("parallel",)), name=name)(*[_arr(a) for a in xs + halos + cts + ps + cfs])


def stage_bwd(f, xs, ps, cts, cfs, gs, tile, name, bf16_copies=(), with_prev=False):
    xs, ps, cts, cfs = list(xs), list(ps), list(cts), list(cfs)
    gs = [list(g) if isinstance(g, (list, tuple)) else [g] for g in gs]
    g_flat = [a for g in gs for a in g]
    halos = xs if with_prev else []
    nx, nh, nct, ng, np_ = len(xs), len(halos), len(cts), len(g_flat), len(ps)
    T = _shape(xs[0])[-2]
    dx_like = xs + halos
    out_shape = ([jax.ShapeDtypeStruct(_shape(x), F32) for x in dx_like] + [jax.ShapeDtypeStruct(_shape(p), F32) for p in ps]
                 + [jax.ShapeDtypeStruct(_shape(xs[i]), BF16) for i in bf16_copies])
    n_in = nx + nh + nct + ng + np_ + len(cfs)
    ndx = nx + nh

    def body(*refs):
        vals = [r[...] for r in refs[:n_in]]
        outs = refs[n_in:]
        xv, hv, rest = vals[:nx], vals[nx:nx + nh], vals[nx + nh:]
        ctv, gparts, pv, cfv = rest[:nct], rest[nct:nct + ng], rest[nct + ng:nct + ng + np_], rest[nct + ng + np_:]
        gv = []
        for g in gs:
            gv.append(functools.reduce(lambda a, b: a + b, gparts[:len(g)]))
            gparts = gparts[len(g):]
        prev = [_prev_rows(x, h) for x, h in zip(xv, hv)]
        _, vjp = jax.vjp(lambda *xp: f(*xp, *ctv, *cfv), *xv, *prev, *pv)
        d = vjp(tuple(gv))
        for o_ref, r in zip(outs[:ndx], d[:ndx]):
            o_ref[...] = r
        for o_ref, i in zip(outs[ndx + np_:], bf16_copies):
            o_ref[...] = d[i].astype(BF16)

        @pl.when(pl.program_id(0) == 0)
        def _():
            for o_ref in outs[ndx:ndx + np_]:
                o_ref[...] = jnp.zeros_like(o_ref)

        for o_ref, r in zip(outs[ndx:ndx + np_], d[ndx:]):
            o_ref[...] += r

    plain = lambda x: jax.ShapeDtypeStruct(_shape(x), F32)
    res = pl.pallas_call(
        body, grid=(T // tile,),
        in_specs=([_tok_spec(x, tile) for x in xs] + [_halo_spec(x, tile) for x in halos]
                  + [_tok_spec(c, tile) for c in cts + g_flat] + [_full_spec(p) for p in ps + cfs]),
        out_specs=([_tok_spec(plain(x), tile) for x in dx_like] + [_full_spec(plain(p)) for p in ps]
                   + [_tok_spec(plain(xs[i]), tile) for i in bf16_copies]), out_shape=out_shape,
        compiler_params=_cparams(("arbitrary",)), name=name)(*[_arr(a) for a in xs + halos + cts + g_flat + ps + cfs])
    if bf16_copies:
        return list(res[:ndx]), list(res[ndx:ndx + np_]), list(res[ndx + np_:])
    return list(res[:ndx]), list(res[ndx:])


def _rms(x, g, eps=RMS_EPS):
    return x * lax.rsqrt(jnp.mean(x * x, axis=-1, keepdims=True) + eps) * g


def f_rmsnorm(x, g):
    return (_rms(x, g),)


def f_rmsnorm_res(x, g):
    return _rms(x, g), x


def f_rmsnorm2(x, g1, g2):
    n = x * lax.rsqrt(jnp.mean(x * x, axis=-1, keepdims=True) + RMS_EPS)
    return n * g1, n * g2


def f_rmsnorm2_res(x, g1, g2):
    return f_rmsnorm2(x, g1, g2) + (x,)


def _sigmoid(x):
    return 1.0 / (1.0 + jnp.exp(-x))


def _softplus(x):
    return jnp.maximum(x, 0.0) + jnp.log(1.0 + jnp.exp(-jnp.abs(x)))


def f_rwkv_pre(pr, pk, pv, pl_, qr, qk, qv, ql, mu_r, mu_k, mu_v, mu_l, w0, w2, a0, a2, g2, k_k, k_a, seg, seg_t):
    xr = pr + (qr - pr) * mu_r
    xk = pk + (qk - pk) * mu_k
    xv = pv + (qv - pv) * mu_v
    xl = pl_ + (ql - pl_) * mu_l
    w_log = -_softplus(-(w0 + mm_nn(jnp.tanh(xl), w2))) - 0.5
    lw = -jnp.exp(w_log)
    a = _sigmoid(a0 + mm_nn(xl, a2))
    g = mm_nn(_sigmoid(xl), g2)
    kkr = xk * k_k
    inv = lax.rsqrt(jnp.maximum(mmh(kkr * kkr, seg), 1e-24))
    kk = kkr * mmh(inv, seg_t)
    k2 = xk * (1.0 + (a - 1.0) * k_a)
    return xr, lw, k2, xv, kk, kk * a, g


def f_rwkv_post(y, r, k2, v, g, lnx_w, lnx_b, r_k, seg, seg_t):
    inv_n = 1.0 / HEAD_DIM
    m = mmh(mmh(y, seg) * inv_n, seg_t)
    yc = y - m
    rstd = lax.rsqrt(mmh(yc * yc, seg) * inv_n + LNX_EPS)
    yn = yc * mmh(rstd, seg_t) * lnx_w + lnx_b
    bonus = mmh(mmh(r * k2 * r_k, seg), seg_t) * v
    return ((yn + bonus) * g,)


def _headnorm(z, g, seg, seg_t):
    ms = mmh(z * z, seg) * (1.0 / HEAD_DIM)
    return z * mmh(lax.rsqrt(ms + RMS_EPS), seg_t) * g


def f_headnorm(z, g, seg, seg_t):
    return (_headnorm(z, g, seg, seg_t),)


def f_qkprep(z, g, cos, sin, seg, seg_t, rot):
    zn = _headnorm(z, g, seg, seg_t)
    return (zn * cos + mmh(zn, rot) * sin,)


def _head_mask(width, h):
    lane = lax.broadcasted_iota(jnp.int32, (1, width), 1)
    return jnp.where((lane >> 6) == h, jnp.ones((), F32), 0.0)


def f_memattn(q, k, v, q_norm, seg, seg_t):
    qn = _headnorm(q, q_norm, seg, seg_t)
    out = jnp.zeros_like(q)
    for h in range(MEM_HEADS):
        m = _head_mask(MEM_WIDTH, h)
        s = mm_nt(qn * m, k) * (1.0 / math.sqrt(HEAD_DIM))
        s = s - jnp.max(s, axis=-1, keepdims=True)
        p = jnp.exp(s)
        p = p / jnp.sum(p, axis=-1, keepdims=True)
        out = out + mm_nn(p, v) * m
    return (out,)


def f_mix(o1, o2, o3, l1, l2, l3):
    mx = jnp.maximum(jnp.maximum(l1, l2), l3)
    e1, e2, e3 = jnp.exp(l1 - mx), jnp.exp(l2 - mx), jnp.exp(l3 - mx)
    return ((e1 * o1 + e2 * o2 + e3 * o3) / (e1 + e2 + e3),)


def _chunk_masks(L):
    t = lax.broadcasted_iota(jnp.int32, (L, L), 0)
    s = lax.broadcasted_iota(jnp.int32, (L, L), 1)
    return t, s


def _unit_lower_inverse(a):
    L = a.shape[-1]
    t, s = _chunk_masks(L)
    one = jnp.ones((), F32)
    blk = lambda sh: jnp.where((t >> sh) == (s >> sh), one, 0.0)
    n0 = a * blk(3)
    x = jnp.where(t == s, one, 0.0) - n0
    n2 = mmh(n0, n0)
    x = x + mmh(x, n2)
    x = x + mmh(x, mmh(n2, n2))
    for sh in (3, 4, 5):
        if (1 << sh) >= L:
            break
        off = a * (blk(sh + 1) - blk(sh))
        x = x - mmh(x, mmh(off, x))
    return x


@jax.custom_vjp
def _inverse_known(a, x):
    return x


def _inverse_known_fwd(a, x):
    return x, x


def _inverse_known_bwd(x, dx):
    return -mmh_nt(mmh_tn(x, dx), x), jnp.zeros_like(x)


_inverse_known.defvjp(_inverse_known_fwd, _inverse_known_bwd)


def _running_sum(x, reverse):
    L = x.shape[1]
    pos = lax.broadcasted_iota(jnp.int32, (1, L, 1), 1)
    step = 1
    while step < L:
        if reverse:
            x = x + jnp.where(pos < L - step, pltpu.roll(x, L - step, 1), 0.0)
        else:
            x = x + jnp.where(pos >= step, pltpu.roll(x, step, 1), 0.0)
        step *= 2
    return x


@jax.custom_vjp
def _cumsum_tokens(x):
    return _running_sum(x, False)


_cumsum_tokens.defvjp(lambda x: (_running_sum(x, False), None), lambda _, g: (_running_sum(g, True),))


def f_rwkv_chunk(s0, r, lw, k, v, kk, b, x_known=None):
    H, L, _ = r.shape
    t, s = _chunk_masks(L)
    one = jnp.ones((), F32)
    incl = jnp.where(t >= s, one, 0.0)
    strict = jnp.where(t > s, one, 0.0)
    cum = _cumsum_tokens(lw)
    w_in = jnp.exp(cum)
    w_ex = jnp.exp(cum - lw)
    w_inv = jnp.exp(-cum)
    rt, kkt, kt, bt = r * w_in, kk * w_ex, k * w_inv, b * w_inv
    a_b = mmh_nt(kkt, bt) * strict
    a_k = mmh_nt(kkt, kt) * strict
    m_k = mmh_nt(rt, kt) * incl
    m_b = mmh_nt(rt, bt) * incl
    x = _unit_lower_inverse(a_b) if x_known is None else _inverse_known(a_b, x_known)
    u = mmh(x, mmh_nt(kkt, s0) + mmh(a_k, v))
    y = mmh_nt(rt, s0) + mmh(m_k, v) - mmh(m_b, u)
    w_last = jnp.exp(jnp.sum(lw, axis=1, keepdims=True))
    s1 = (s0 + mmh_tn(v, kt) - mmh_tn(u, bt)) * w_last
    return y, s1, x


def _ex_split(ex, refs, n_in, n_out):
    n = ex.n
    ins, ex_in = refs[:n_in], refs[n_in:n_in + n]
    outs, ex_out = refs[n_in + n:n_in + n + n_out], refs[n_in + n + n_out:n_in + 2 * n + n_out]
    rest = refs[n_in + 2 * n + n_out:]
    return ins, outs, rest[:len(rest) - 3], (ex_in, ex_out) + tuple(rest[len(rest) - 3:])


def _split_heads(x):
    return jnp.stack([x[:, h * HEAD_DIM:(h + 1) * HEAD_DIM] for h in range(x.shape[1] // HEAD_DIM)], axis=0)


def _merge_heads(x):
    return jnp.concatenate([x[h] for h in range(x.shape[0])], axis=1)


def rwkv_scan_fwd(r, lw, k, v, kk, b, ex):
    T, N = r.shape[0], HEAD_DIM
    H = r.shape[1] // N
    groups = SCAN_GROUPS_FWD
    nc, hg = T // CHUNK, H // groups
    seq = pl.BlockSpec((CHUNK, hg * N), lambda g, c: (c, g))

    def body(*refs):
        (r_ref, lw_ref, k_ref, v_ref, kk_ref, b_ref), (y_ref, hs_ref, xs_ref), (h_scr,), ex_refs = _ex_split(ex, refs, 6, 3)
        g, c = pl.program_id(0), pl.program_id(1)

        @pl.when(jnp.logical_and(g == 0, c == 0))
        def _():
            ex.start(*ex_refs)

        @pl.when(c == 0)
        def _():
            h_scr[...] = jnp.zeros_like(h_scr)

        h0 = h_scr[...]
        hs_ref[0] = h0
        y, h1, x = f_rwkv_chunk(h0, *[_split_heads(z[...]) for z in (r_ref, lw_ref, k_ref, v_ref, kk_ref, b_ref)])
        y_ref[...] = _merge_heads(y)
        xs_ref[0] = x
        h_scr[...] = h1

        @pl.when(jnp.logical_and(g == groups - 1, c == (3 * nc) // 4))
        def _():
            ex.forward(*ex_refs)

        @pl.when(jnp.logical_and(g == groups - 1, c == nc - 1))
        def _():
            ex.wait(*ex_refs)

    res = pl.pallas_call(
        body, grid=(groups, nc), in_specs=[seq] * 6 + [_ANY] * ex.n,
        out_specs=[seq, pl.BlockSpec((1, hg, N, N), lambda g, c: (c, g, 0, 0)),
                   pl.BlockSpec((1, hg, CHUNK, CHUNK), lambda g, c: (c, g, 0, 0))] + [_ANY] * ex.n,
        out_shape=[jax.ShapeDtypeStruct((T, H * N), F32), jax.ShapeDtypeStruct((nc, H, N, N), F32),
                   jax.ShapeDtypeStruct((nc, H, CHUNK, CHUNK), F32)] + ex.out_shape(),
        scratch_shapes=[pltpu.VMEM((hg, N, N), F32)] + ex.scratch(),
        compiler_params=_cparams(("arbitrary", "arbitrary")), name="rwkv_scan_fwd")(r, lw, k, v, kk, b, *ex.operands())
    return res[0], (res[1], res[2]), list(res[3:])


def rwkv_scan_bwd(r, lw, k, v, kk, b, saved, dy, ex):
    T, N = r.shape[0], HEAD_DIM
    H = r.shape[1] // N
    groups = SCAN_GROUPS_BWD
    nc, hg = T // CHUNK, H // groups
    seq = pl.BlockSpec((CHUNK, hg * N), lambda g, c: (nc - 1 - c, g))
    state = pl.BlockSpec((1, hg, N, N), lambda g, c: (nc - 1 - c, g, 0, 0))

    def body(*refs):
        (r_ref, lw_ref, k_ref, v_ref, kk_ref, b_ref, hs_ref, xs_ref, dy_ref), outs, (dh_scr,), ex_refs = _ex_split(ex, refs, 9, 6)
        g, c = pl.program_id(0), pl.program_id(1)

        @pl.when(jnp.logical_and(g == 0, c == 0))
        def _():
            ex.start(*ex_refs)

        @pl.when(c == 0)
        def _():
            dh_scr[...] = jnp.zeros_like(dh_scr)

        x_known = xs_ref[0]
        _, vjp = jax.vjp(lambda *a: f_rwkv_chunk(*a, x_known=x_known)[:2], hs_ref[0],
                         *[_split_heads(z[...]) for z in (r_ref, lw_ref, k_ref, v_ref, kk_ref, b_ref)])
        d = vjp((_split_heads(dy_ref[...]), dh_scr[...]))
        dh_scr[...] = d[0]
        for o_ref, dz in zip(outs, d[1:]):
            o_ref[...] = _merge_heads(dz)

        @pl.when(jnp.logical_and(g == groups - 1, c == nc - 1))
        def _():
            ex.forward(*ex_refs)
            ex.wait(*ex_refs)

    res = pl.pallas_call(
        body, grid=(groups, nc),
        in_specs=[seq] * 6 + [state, state, seq] + [_ANY] * ex.n,
        out_specs=[seq] * 6 + [_ANY] * ex.n, out_shape=[jax.ShapeDtypeStruct((T, H * N), F32)] * 6 + ex.out_shape(),
        scratch_shapes=[pltpu.VMEM((hg, N, N), F32)] + ex.scratch(),
        compiler_params=_cparams(("arbitrary", "arbitrary")), name="rwkv_scan_bwd")(r, lw, k, v, kk, b, *saved, dy, *ex.operands())
    return list(res[:6]), list(res[6:])


PAIR = 2 * HEAD_DIM


def _f_dilattn(has_prev, q, kc, kp, vc, vp):
    scale = 1.0 / math.sqrt(HEAD_DIM)
    i = lax.broadcasted_iota(jnp.int32, (DIL_BLOCK, DIL_BLOCK), 0)
    j = lax.broadcasted_iota(jnp.int32, (DIL_BLOCK, DIL_BLOCK), 1)
    o, l = jnp.zeros_like(q), jnp.zeros_like(q)
    for h in range(2):
        m = _head_mask(PAIR, h)
        sc = jnp.where(j <= i, mm_nt(q * m, kc) * scale, NEG_INF)
        sp = jnp.where(jnp.logical_and(i <= j, has_prev), mm_nt(q * m, kp) * scale, NEG_INF)
        mx = jnp.maximum(jnp.max(sc, axis=-1, keepdims=True), jnp.max(sp, axis=-1, keepdims=True))
        pc, pp = jnp.exp(sc - mx), jnp.exp(sp - mx)
        den = jnp.sum(pc, axis=-1, keepdims=True) + jnp.sum(pp, axis=-1, keepdims=True)
        o = o + (mm_nn(pc, vc) + mm_nn(pp, vp)) / den * m
        l = l + (mx + jnp.log(den)) * m
    return o, l


def _dil_specs(gi, d):
    blk = (DIL_BLOCK * d, PAIR)
    at = lambda off: (lambda p, n: (n, off + p))
    before = lambda off: (lambda p, n: (jnp.maximum(n - 1, 0), off + p))
    pair0 = 2 * gi
    v0 = DIL_WIDTH // PAIR + pair0
    q = pl.BlockSpec(blk, at(pair0))
    kc, kp = pl.BlockSpec(blk, at(pair0)), pl.BlockSpec(blk, before(pair0))
    vc, vp = pl.BlockSpec(blk, at(v0)), pl.BlockSpec(blk, before(v0))
    out = pl.BlockSpec(blk, at(0))
    return q, kc, kp, vc, vp, out


def _residue_rows(r, d):
    return pl.ds(r, DIL_BLOCK, stride=d) if d > 1 else pl.ds(0, DIL_BLOCK)


def dil_fwd(q, k, kv, gi, d, name):
    T = q.shape[0]
    qs, kc, kp, vc, vp, out = _dil_specs(gi, d)

    def body(q_ref, kc_ref, kp_ref, vc_ref, vp_ref, o_ref, l_ref):
        has_prev = pl.program_id(1) > 0

        def residue(r, carry):
            rows = _residue_rows(r, d)
            o, l = _f_dilattn(has_prev, q_ref[rows, :], kc_ref[rows, :], kp_ref[rows, :], vc_ref[rows, :], vp_ref[rows, :])
            o_ref[rows, :] = o
            l_ref[rows, :] = l
            return carry

        lax.fori_loop(0, d, residue, 0, unroll=min(d, 2))

    shape = jax.ShapeDtypeStruct((T, 4 * HEAD_DIM), F32)
    return pl.pallas_call(
        body, grid=(2, T // (DIL_BLOCK * d)), in_specs=[qs, kc, kp, vc, vp], out_specs=[out, out], out_shape=[shape, shape],
        compiler_params=_cparams(("parallel", "parallel")), name=name)(q, k, k, kv, kv)


def dil_bwd(q, k, kv, do, dl, gi, d, name):
    T = q.shape[0]
    qs, kc, kp, vc, vp, out = _dil_specs(gi, d)

    def body(q_ref, kc_ref, kp_ref, vc_ref, vp_ref, do_ref, dl_ref, *outs):
        f = functools.partial(_f_dilattn, pl.program_id(1) > 0)

        def residue(r, carry):
            rows = _residue_rows(r, d)
            _, vjp = jax.vjp(f, q_ref[rows, :], kc_ref[rows, :], kp_ref[rows, :], vc_ref[rows, :], vp_ref[rows, :])
            for o_ref, g in zip(outs, vjp((do_ref[rows, :], dl_ref[rows, :]))):
                o_ref[rows, :] = g
            return carry

        lax.fori_loop(0, d, residue, 0, unroll=min(d, 2))

    shape = jax.ShapeDtypeStruct((T, 4 * HEAD_DIM), F32)
    dq, dkc, dkp, dvc, dvp = pl.pallas_call(
        body, grid=(2, T // (DIL_BLOCK * d)), in_specs=[qs, kc, kp, vc, vp, out, out], out_specs=[out] * 5, out_shape=[shape] * 5,
        compiler_params=_cparams(("parallel", "parallel")), name=name)(q, k, k, kv, kv, do, dl)

    def own_plus_next(c, p):
        return c + jnp.concatenate([p[DIL_BLOCK * d:], jnp.zeros_like(p[:DIL_BLOCK * d])], axis=0)

    return dq, own_plus_next(dkc, dkp), own_plus_next(dvc, dvp)


CONV_TILE = 128


def _conv3(u, h6, h7, w, b):
    rows = lax.broadcasted_iota(jnp.int32, (u.shape[0], 1), 0)
    s1 = jnp.where(rows == 0, h7, pltpu.roll(u, 1, 0))
    s2 = jnp.where(rows == 0, h6, jnp.where(rows == 1, h7, pltpu.roll(u, 2, 0)))
    return b + w[0:1] * s2 + w[1:2] * s1 + w[2:3] * u, s1, s2


def _conv_halves(u_ref, h_ref, cw_ref, cb_ref):
    F = D_FF
    first = pl.program_id(0) > 0
    res = []
    for lo in (0, F):
        h = h_ref[:, lo:lo + F]
        h6 = jnp.where(first, h[6:7], 0.0)
        h7 = jnp.where(first, h[7:8], 0.0)
        u = u_ref[:, lo:lo + F]
        res.append((u,) + _conv3(u, h6, h7, cw_ref[:, lo:lo + F], cb_ref[:, lo:lo + F]))
    return res


def _halo_before(C):
    return pl.BlockSpec((8, C), lambda i: (jnp.maximum(i * (CONV_TILE // 8) - 1, 0), 0))


def convgate_fwd(u, cw, cb, name):
    T, C = u.shape
    F = C // 2

    def body(u_ref, h_ref, cw_ref, cb_ref, z_ref):
        (_, cg, _, _), (_, cv, _, _) = _conv_halves(u_ref, h_ref, cw_ref, cb_ref)
        z_ref[...] = (cg * _sigmoid(cg) * cv).astype(BF16)

    return pl.pallas_call(
        body, grid=(T // CONV_TILE,),
        in_specs=[pl.BlockSpec((CONV_TILE, C), lambda i: (i, 0)), _halo_before(C), _full_spec(cw), _full_spec(cb)],
        out_specs=pl.BlockSpec((CONV_TILE, F), lambda i: (i, 0)), out_shape=jax.ShapeDtypeStruct((T, F), BF16),
        compiler_params=_cparams(("parallel",)), name=name)(u, u, cw, cb)


def convgate_bwd_c(u, cw, cb, dz, name):
    T, C = u.shape
    F = C // 2

    def body(u_ref, h_ref, cw_ref, cb_ref, dz_ref, dc_ref, dcw_ref, dcb_ref):
        (ug, cg, g1, g2), (uv, cv, v1, v2) = _conv_halves(u_ref, h_ref, cw_ref, cb_ref)
        dz = dz_ref[...]
        sg = _sigmoid(cg)
        dgate = dz * cv * sg * (1.0 + cg * (1.0 - sg))
        dval = dz * cg * sg
        dc_ref[:, :F] = dgate
        dc_ref[:, F:] = dval

        @pl.when(pl.program_id(0) == 0)
        def _():
            dcw_ref[...] = jnp.zeros_like(dcw_ref)
            dcb_ref[...] = jnp.zeros_like(dcb_ref)

        for lo, d, u0, s1, s2 in ((0, dgate, ug, g1, g2), (F, dval, uv, v1, v2)):
            dcb_ref[:, lo:lo + F] += jnp.sum(d, axis=0, keepdims=True)
            dcw_ref[0:1, lo:lo + F] += jnp.sum(d * s2, axis=0, keepdims=True)
            dcw_ref[1:2, lo:lo + F] += jnp.sum(d * s1, axis=0, keepdims=True)
            dcw_ref[2:3, lo:lo + F] += jnp.sum(d * u0, axis=0, keepdims=True)

    return pl.pallas_call(
        body, grid=(T // CONV_TILE,),
        in_specs=[pl.BlockSpec((CONV_TILE, C), lambda i: (i, 0)), _halo_before(C), _full_spec(cw), _full_spec(cb),
                  pl.BlockSpec((CONV_TILE, F), lambda i: (i, 0))],
        out_specs=[pl.BlockSpec((CONV_TILE, C), lambda i: (i, 0)), _full_spec(cw), _full_spec(cb)],
        out_shape=[jax.ShapeDtypeStruct((T, C), F32), jax.ShapeDtypeStruct(cw.shape, F32), jax.ShapeDtypeStruct(cb.shape, F32)],
        compiler_params=_cparams(("arbitrary",)), name=name)(u, u, cw, cb, dz)


def convgate_bwd_u(dc, cw, name):
    T, C = dc.shape
    n = T // CONV_TILE

    def body(dc_ref, nx_ref, cw_ref, du_ref):
        last = pl.program_id(0) < n - 1
        rows = lax.broadcasted_iota(jnp.int32, (CONV_TILE, 1), 0)
        for lo in (0, C // 2):
            sl = slice(lo, lo + C // 2)
            d = dc_ref[:, sl]
            n0 = jnp.where(last, nx_ref[0:1, sl], 0.0)
            n1 = jnp.where(last, nx_ref[1:2, sl], 0.0)
            up1 = jnp.where(rows == CONV_TILE - 1, n0, pltpu.roll(d, CONV_TILE - 1, 0))
            up2 = jnp.where(rows == CONV_TILE - 1, n1, jnp.where(rows == CONV_TILE - 2, n0, pltpu.roll(d, CONV_TILE - 2, 0)))
            du_ref[:, sl] = (cw_ref[2:3, sl] * d + cw_ref[1:2, sl] * up1 + cw_ref[0:1, sl] * up2).astype(BF16)

    nxt = pl.BlockSpec((8, C), lambda i: (jnp.minimum((i + 1) * (CONV_TILE // 8), T // 8 - 1), 0))
    return pl.pallas_call(
        body, grid=(n,), in_specs=[pl.BlockSpec((CONV_TILE, C), lambda i: (i, 0)), nxt, _full_spec(cw)],
        out_specs=pl.BlockSpec((CONV_TILE, C), lambda i: (i, 0)), out_shape=jax.ShapeDtypeStruct((T, C), BF16),
        compiler_params=_cparams(("parallel",)), name=name)(dc, dc, cw)


def loss_head(y, tgt):
    T, D = y.shape
    tile = 256

    def body(y_ref, t_ref, l_ref, d_ref, db_ref):
        d = y_ref[...] - t_ref[...]
        d_ref[...] = d * (1.0 / D)
        db_ref[...] = (d * (1.0 / D)).astype(BF16)

        @pl.when(pl.program_id(0) == 0)
        def _():
            l_ref[...] = jnp.zeros_like(l_ref)

        l_ref[...] += (0.5 / D) * jnp.sum(d * d)

    row = pl.BlockSpec((tile, D), lambda i: (i, 0))
    return pl.pallas_call(
        body, grid=(T // tile,), in_specs=[row, row], out_specs=[pl.BlockSpec((8, 128), lambda i: (0, 0)), row, row],
        out_shape=[jax.ShapeDtypeStruct((8, 128), F32), jax.ShapeDtypeStruct((T, D), F32), jax.ShapeDtypeStruct((T, D), BF16)],
        compiler_params=_cparams(("arbitrary",)), name="loss_head")(y, tgt)


def sum_parts(parts, name):
    S, R, C = parts.shape
    tile = _pick(R, (256, 128, 64, 32, 16, 8))

    def body(p_ref, o_ref):
        acc = p_ref[0]
        for s in range(1, S):
            acc = acc + p_ref[s]
        o_ref[...] = acc

    return pl.pallas_call(
        body, grid=(R // tile,), in_specs=[pl.BlockSpec((S, tile, C), lambda i: (0, i, 0))],
        out_specs=pl.BlockSpec((tile, C), lambda i: (i, 0)), out_shape=jax.ShapeDtypeStruct((R, C), F32),
        compiler_params=_cparams(("parallel",)), name=name)(parts)


def adamw(gparts, w, m, v, name):
    S, R, C = gparts.shape
    tile = _pick(R, (256, 128, 64, 32, 16, 8))
    c1 = 1.0 / (1.0 - ADAM_B1 ** ADAM_STEP)
    c2 = 1.0 / (1.0 - ADAM_B2 ** ADAM_STEP)

    def body(g_ref, w_ref, m_ref, v_ref, go_ref, d_ref, mo_ref, vo_ref):
        g = g_ref[0].astype(F32)
        for s in range(1, S):
            g = g + g_ref[s].astype(F32)
        m1 = ADAM_B1 * m_ref[...] + (1.0 - ADAM_B1) * g
        v1 = ADAM_B2 * v_ref[...] + (1.0 - ADAM_B2) * (g * g)
        go_ref[...] = g
        mo_ref[...] = m1
        vo_ref[...] = v1
        d_ref[...] = -ADAM_LR * ((m1 * c1) / (jnp.sqrt(v1 * c2) + ADAM_EPS) + ADAM_WD * w_ref[...])

    row = pl.BlockSpec((tile, C), lambda i: (i, 0))
    return pl.pallas_call(
        body, grid=(R // tile,), in_specs=[pl.BlockSpec((S, tile, C), lambda i: (0, i, 0)), row, row, row],
        out_specs=[row] * 4, out_shape=[jax.ShapeDtypeStruct((R, C), F32)] * 4,
        compiler_params=_cparams(("parallel",)), name=name)(gparts, w, m, v)


def _peers():
    x, y, c = lax.axis_index("x"), lax.axis_index("y"), lax.axis_index("c")
    peers = []
    for k in range(1, N_DEV):
        px = 1 - x if k & 4 else x
        py = 1 - y if k & 2 else y
        pc = 1 - c if k & 1 else c
        peers.append(((px, py, pc), 4 * px + 2 * py + pc))
    return 4 * x + 2 * y + c, peers


_ANY = pl.BlockSpec(memory_space=pl.ANY)


class Exchange:
    def __init__(self, gathers=(), scatters=()):
        self.gathers, self.scatters = list(gathers), list(scatters)
        self.n = len(self.gathers) + len(self.scatters)

    def operands(self):
        return self.gathers + self.scatters

    def out_shape(self):
        return ([jax.ShapeDtypeStruct((N_DEV,) + x.shape, x.dtype) for x in self.gathers]
                + [jax.ShapeDtypeStruct(x.shape, x.dtype) for x in self.scatters])

    def scratch(self):
        n = max(self.n, 1)
        return [pltpu.SemaphoreType.DMA((7 * n,)), pltpu.SemaphoreType.DMA((7 * n,)), pltpu.SemaphoreType.DMA((n,))]

    def _copies(self, in_refs, out_refs, send_sems, recv_sems, local_sems):
        me, peers = _peers()
        ng = len(self.gathers)
        local, sends, recvs = [], [], []
        for a in range(self.n):
            x, o = in_refs[a], out_refs[a]
            mine = x if a < ng else x.at[me]
            local.append(pltpu.make_async_copy(mine, o.at[me], local_sems.at[a]))
            s_a, r_a = {}, {}
            for k in range(1, N_DEV):
                peer, slot = peers[k - 1]
                sems = dict(send_sem=send_sems.at[7 * a + k - 1], recv_sem=recv_sems.at[7 * a + k - 1],
                            device_id_type=pl.DeviceIdType.MESH)
                if a >= ng:
                    s_a[k] = pltpu.make_async_remote_copy(src_ref=x.at[slot], dst_ref=o.at[me], device_id=peer, **sems)
                elif k in FORWARDED:
                    came = o.at[peers[k - 2][1]]
                    s_a[k] = pltpu.make_async_remote_copy(src_ref=came, dst_ref=came, device_id=peers[0][0], **sems)
                else:
                    s_a[k] = pltpu.make_async_remote_copy(src_ref=x, dst_ref=o.at[me], device_id=peer, **sems)
                r_a[k] = pltpu.make_async_remote_copy(src_ref=mine, dst_ref=o.at[slot], device_id=peer, **sems)
            sends.append(s_a)
            recvs.append(r_a)
        return local, sends, recvs

    def start(self, *refs):
        if self.n == 0:
            return
        local, sends, _ = self._copies(*refs)
        for a in range(self.n):
            local[a].start()
            for k in range(1, N_DEV):
                if a >= len(self.gathers) or k not in FORWARDED:
                    sends[a][k].start()

    def forward(self, *refs):
        if not self.gathers:
            return
        _, sends, recvs = self._copies(*refs)
        for a in range(len(self.gathers)):
            for k in FORWARDED:
                recvs[a][k - 1].wait_recv()
                sends[a][k].start()

    def wait(self, *refs):
        if self.n == 0:
            return
        local, sends, recvs = self._copies(*refs)
        for a in range(self.n):
            waited_early = [f - 1 for f in FORWARDED] if a < len(self.gathers) else []
            for k in range(1, N_DEV):
                if k not in waited_early:
                    recvs[a][k].wait_recv()
            for k in range(1, N_DEV):
                sends[a][k].wait_send()
            local[a].wait()


FORWARDED = (3, 5, 7)


def exchange(ex, name):
    n = ex.n

    def body(*refs):
        args = (refs[:n], refs[n:2 * n]) + tuple(refs[2 * n:])
        ex.start(*args)
        ex.forward(*args)
        ex.wait(*args)

    return pl.pallas_call(body, in_specs=[_ANY] * n, out_specs=[_ANY] * n, out_shape=ex.out_shape(),
                          scratch_shapes=ex.scratch(), name=name)(*ex.operands())


def _heads(z, h):
    return z.reshape(z.shape[0], h, HEAD_DIM).transpose(1, 0, 2)


def _unheads(z):
    return z.transpose(1, 0, 2).reshape(z.shape[1], z.shape[0] * HEAD_DIM)


def _shift_up(z):
    return jnp.concatenate([z[1:], jnp.zeros_like(z[:1])], axis=0)


def _segments(width):
    seg = np.zeros((width, 128), np.float32)
    seg[np.arange(width), np.arange(width) // HEAD_DIM] = 1.0
    return jnp.asarray(seg), jnp.asarray(seg.T)


def _rope_consts(T, heads):
    inv = ROPE_THETA ** (-jnp.arange(0, HEAD_DIM, 2, dtype=F32) / HEAD_DIM)
    ang = jnp.arange(T, dtype=F32)[:, None] * inv[None, :]
    cos, sin = jnp.cos(ang), jnp.sin(ang)
    rot = np.zeros((HEAD_DIM, HEAD_DIM), np.float32)
    half = HEAD_DIM // 2
    rot[np.arange(half) + half, np.arange(half)] = -1.0
    rot[np.arange(half), np.arange(half) + half] = 1.0
    return (jnp.tile(cos, (1, 2 * heads)), jnp.tile(sin, (1, 2 * heads)), jnp.asarray(np.kron(np.eye(heads, dtype=np.float32), rot)))


def _per_head(g, heads):
    return jnp.tile(g.reshape(1, HEAD_DIM), (1, heads))


def _sum_heads(g):
    return g.reshape(-1, HEAD_DIM).sum(axis=0, keepdims=True)


LORA_COLS = 256
RW_TILE = 128
ROW_TILE = 256


def _local_step(x0, memx, tgt, P, ex_weights=None, weights_done=None, ex_grads=None):
    T = x0.shape[0]
    P = dict(P)
    G = {}
    seg, seg_t = _segments(RWKV_WIDTH)
    mseg = (seg[:MEM_WIDTH], seg_t[:, :MEM_WIDTH])
    cos, sin, rot = _rope_consts(T, DIL_WIDTH // HEAD_DIM)
    row = lambda v: v.reshape(1, -1)

    def mem_fwd(i, q):
        memn = stage_fwd(f_rmsnorm, [memx], [P["mem_norm"][i:i + 1]], [], [], N_MEM, f"mem{i}_norm", [BF16])[0]
        kvm = matmul(memn, P["mem_w_kv"][i], "nn", f"mem{i}_kv")
        kn, qn = _per_head(P["mem_k_norm"][i], MEM_HEADS), _per_head(P["mem_q_norm"][i], MEM_HEADS)
        km = stage_fwd(f_headnorm, [Cols(kvm, MEM_WIDTH, 0)], [kn], [], mseg, N_MEM, f"mem{i}_knorm")[0]
        om = stage_fwd(f_memattn, [q], [km, Cols(kvm, MEM_WIDTH, 1), qn], [], mseg, ROW_TILE, f"mem{i}_attn")[0]
        return om, (memn, kvm, km, kn, qn, q)

    def mem_bwd(i, saved, dymem):
        memn, kvm, km, kn, qn, q = saved
        (dq,), (dkm, dvm, g_qn) = stage_bwd(f_memattn, [q], [km, Cols(kvm, MEM_WIDTH, 1), qn], [], mseg, [dymem], ROW_TILE,
                                            f"mem{i}_attn_bwd")
        (dkraw,), (g_kn,) = stage_bwd(f_headnorm, [Cols(kvm, MEM_WIDTH, 0)], [kn], [], mseg, [dkm], N_MEM, f"mem{i}_knorm_bwd")
        dkvm = jnp.concatenate([dkraw, dvm], axis=1).astype(BF16)
        g_w = matmul(memn, dkvm, "tn", f"mem{i}_kv_dw")
        dmemn = matmul(dkvm, P["mem_w_kv"][i], "nt", f"mem{i}_kv_dx")
        _, (g_mn,) = stage_bwd(f_rmsnorm, [memx], [P["mem_norm"][i:i + 1]], [], [], [dmemn], N_MEM, f"mem{i}_norm_bwd")
        return dq, g_mn, g_w, _sum_heads(g_qn), _sum_heads(g_kn)

    def ffn_fwd(i, xin):
        hn = stage_fwd(f_rmsnorm, [xin], [P["ffn_norm"][i:i + 1]], [], [], ROW_TILE, f"ffn{i}_norm", [BF16])[0]
        u = matmul(hn, P["ffn_w_up"][i], "nn", f"ffn{i}_up")
        z = convgate_fwd(u, P["ffn_conv_w"][i], P["ffn_conv_b"][i:i + 1], f"ffn{i}_conv")
        return matmul(z, P["ffn_w_down"][i], "nn", f"ffn{i}_down", residual=xin), (hn, u, z)

    def ffn_bwd(i, xin, saved, dxo, dxo_b):
        hn, u, z = saved
        dz = matmul(dxo_b, P["ffn_w_down"][i], "nt", f"ffn{i}_down_dx")
        g_down = matmul(z, dxo_b, "tn", f"ffn{i}_down_dw")
        dc, g_cw, g_cb = convgate_bwd_c(u, P["ffn_conv_w"][i], P["ffn_conv_b"][i:i + 1], dz, f"ffn{i}_conv_bwd_c")
        du = convgate_bwd_u(dc, P["ffn_conv_w"][i], f"ffn{i}_conv_bwd_u")
        dhn = matmul(du, P["ffn_w_up"][i], "nt", f"ffn{i}_up_dx")
        g_up = matmul(hn, du, "tn", f"ffn{i}_up_dw")
        (dxin,), (g_n,), (dxin_b,) = stage_bwd(f_rmsnorm_res, [xin], [P["ffn_norm"][i:i + 1]], [], [], [dhn, dxo], ROW_TILE,
                                               f"ffn{i}_norm_bwd", bf16_copies=(0,))
        return dxin, dxin_b, g_n, g_up, g_cw, g_cb, g_down

    h0 = stage_fwd(f_rmsnorm, [x0], [P["attn_norm"][0:1]], [], [], ROW_TILE, "l0_norm", [BF16])[0]
    p0 = matmul(h0, P["a_w_in"][0], "nn", "l0_in")
    lora0 = 3 * RWKV_WIDTH // LORA_COLS
    pre_xs = [Cols(p0, RWKV_WIDTH, 0), Cols(p0, RWKV_WIDTH, 1), Cols(p0, RWKV_WIDTH, 2), Cols(p0, LORA_COLS, lora0)]
    mu = [Cols(P["a_mu"], RWKV_WIDTH, 0), Cols(P["a_mu"], RWKV_WIDTH, 1), Cols(P["a_mu"], RWKV_WIDTH, 2),
          Cols(P["a_mu"], LORA_COLS, lora0)]
    lora_rows = lambda w, lo: jnp.pad(w, ((lo, LORA_COLS - lo - w.shape[0]), (0, 0)))
    pre_ps = mu + [P["a_w0"], lora_rows(P["a_w2"][0], 0), P["a_a0"], lora_rows(P["a_a2"][0], 64), lora_rows(P["a_g2"][0], 128),
                   P["a_k_k"], P["a_k_a"]]
    r, lw, k2, v, kk, b, g = stage_fwd(f_rwkv_pre, pre_xs, pre_ps, [], [seg, seg_t], RW_TILE, "l0_rwkv_pre", with_prev=True)
    scan_in = [r, lw, k2, v, kk, b]
    y_h, h_states, got = rwkv_scan_fwd(*scan_in, ex_weights or Exchange())
    if weights_done is not None:
        P.update(weights_done(got))
    y_s = y_h
    post_ps = [P["a_lnx_w"], P["a_lnx_b"], P["a_r_k"].reshape(1, RWKV_WIDTH)]
    ymix0 = stage_fwd(f_rwkv_post, [y_s, r, k2, v, g], post_ps, [], [seg, seg_t], RW_TILE, "l0_rwkv_post")[0]
    ymem0, mem0_saved = mem_fwd(0, Cols(p0, MEM_WIDTH, SHIFT_WIDTH // MEM_WIDTH))
    ycat0 = jnp.concatenate([ymix0, ymem0], axis=1).astype(BF16)
    x1 = matmul(ycat0, P["a_w_out"][0], "nn", "l0_out", residual=x0)
    x2, ffn0_saved = ffn_fwd(0, x1)

    hk, h1 = stage_fwd(f_rmsnorm2, [x2], [row(P["kv_norm"]), P["attn_norm"][1:2]], [], [], ROW_TILE, "l1_norm", [BF16, BF16])
    kvp = matmul(hk, P["kv_w"][0], "nn", "l1_kv")
    p1 = matmul(h1, P["b_w_in"][0], "nn", "l1_in")
    kraw, qraw = Cols(kvp, DIL_WIDTH, 0), Cols(p1, DIL_WIDTH, 0)
    kgain, qgain = _per_head(P["kv_k_norm"], DIL_WIDTH // HEAD_DIM), _per_head(P["b_q_norm"], DIL_WIDTH // HEAD_DIM)
    ksh = stage_fwd(f_qkprep, [kraw], [kgain], [cos, sin], [seg, seg_t, rot], ROW_TILE, "l1_kprep")[0]
    q = stage_fwd(f_qkprep, [qraw], [qgain], [cos, sin], [seg, seg_t, rot], ROW_TILE, "l1_qprep")[0]
    outs, lses = [], []
    for gi, (_, d) in enumerate(DIL_GROUPS):
        og, lg = dil_fwd(q, ksh, kvp, gi, d, f"l1_dil{gi}")
        outs.append(og)
        lses.append(lg)
    omix = stage_fwd(f_mix, outs + lses, [], [], [], ROW_TILE, "l1_mix")[0]
    ymem1, mem1_saved = mem_fwd(1, Cols(p1, MEM_WIDTH, DIL_WIDTH // MEM_WIDTH))
    ycat1 = jnp.concatenate([omix, ymem1], axis=1).astype(BF16)
    x3 = matmul(ycat1, P["b_w_out"][0], "nn", "l1_out", residual=x2)
    x4, ffn1_saved = ffn_fwd(1, x3)
    loss_part, dx4, dx4_b = loss_head(x4, tgt)

    dx3, dx3_b, gn1, gup1, gcw1, gcb1, gdown1 = ffn_bwd(1, x3, ffn1_saved, dx4, dx4_b)
    dycat1 = matmul(dx3_b, P["b_w_out"][0], "nt", "l1_out_dx")
    G["b_w_out"] = [matmul(ycat1, dx3_b, "tn", "l1_out_dw")]
    dqmem1, gmn1, gmw1, gmq1, gmk1 = mem_bwd(1, mem1_saved, Cols(dycat1, MEM_WIDTH, 1))
    dmix, _ = stage_bwd(f_mix, outs + lses, [], [], [], [Cols(dycat1, MEM_WIDTH, 0)], ROW_TILE, "l1_mix_bwd")
    dq, dk, dv = zip(*[dil_bwd(q, ksh, kvp, dmix[gi], dmix[3 + gi], gi, d, f"l1_dil{gi}_bwd")
                       for gi, (_, d) in enumerate(DIL_GROUPS)])
    dq, dk, dv = jnp.concatenate(dq, axis=1), jnp.concatenate(dk, axis=1), jnp.concatenate(dv, axis=1)
    (dqraw,), (g_bq,) = stage_bwd(f_qkprep, [qraw], [qgain], [cos, sin], [seg, seg_t, rot], [dq], ROW_TILE, "l1_qprep_bwd")
    (dkraw,), (g_kk,) = stage_bwd(f_qkprep, [kraw], [kgain], [cos, sin], [seg, seg_t, rot], [dk], ROW_TILE, "l1_kprep_bwd")
    g_bq, g_kk = _sum_heads(g_bq), _sum_heads(g_kk)
    dp1 = jnp.concatenate([dqraw, dqmem1], axis=1).astype(BF16)
    dkvp = jnp.concatenate([dkraw, dv], axis=1).astype(BF16)
    dh1 = matmul(dp1, P["b_w_in"][0], "nt", "l1_in_dx")
    G["b_w_in"] = [matmul(h1, dp1, "tn", "l1_in_dw")]
    dhk = matmul(dkvp, P["kv_w"][0], "nt", "l1_kv_dx")
    G["kv_w"] = [matmul(hk, dkvp, "tn", "l1_kv_dw")]
    (dx2,), (g_kvn, g_an1), (dx2_b,) = stage_bwd(f_rmsnorm2_res, [x2], [row(P["kv_norm"]), P["attn_norm"][1:2]], [], [],
                                                 [dhk, dh1, dx3], ROW_TILE, "l1_norm_bwd", bf16_copies=(0,))

    dx1, dx1_b, gn0, gup0, gcw0, gcb0, gdown0 = ffn_bwd(0, x1, ffn0_saved, dx2, dx2_b)
    dycat0 = matmul(dx1_b, P["a_w_out"][0], "nt", "l0_out_dx")
    G["a_w_out"] = [matmul(ycat0, dx1_b, "tn", "l0_out_dw")]
    dqmem0, gmn0, gmw0, gmq0, gmk0 = mem_bwd(0, mem0_saved, Cols(dycat0, MEM_WIDTH, RWKV_WIDTH // MEM_WIDTH))
    (dy_s, dr_a, dk_a, dv_a, dg), (g_lw, g_lb, g_rk) = stage_bwd(
        f_rwkv_post, [y_s, r, k2, v, g], post_ps, [], [seg, seg_t], [Cols(dycat0, RWKV_WIDTH, 0)], RW_TILE, "l0_rwkv_post_bwd")
    G["mem_w_kv"], G["ffn_w_up"], G["ffn_w_down"] = [gmw0, gmw1], [gup0, gup1], [gdown0, gdown1]
    (dr_b, dlw, dk_b, dv_b, dkk, db), G["_exchanged"] = rwkv_scan_bwd(*scan_in, h_states, dy_s,
                                                                      ex_grads(G) if ex_grads else Exchange())
    dpre, gpre = stage_bwd(f_rwkv_pre, pre_xs, pre_ps, [], [seg, seg_t],
                           [[dr_a, dr_b], dlw, [dk_a, dk_b], [dv_a, dv_b], dkk, db, dg], RW_TILE, "l0_rwkv_pre_bwd", with_prev=True)
    dp_rw = jnp.concatenate(dpre[:4], axis=1) + _shift_up(jnp.concatenate(dpre[4:], axis=1))
    dp0 = jnp.concatenate([dp_rw, dqmem0], axis=1).astype(BF16)
    dh0 = matmul(dp0, P["a_w_in"][0], "nt", "l0_in_dx")
    G["a_w_in"] = [matmul(h0, dp0, "tn", "l0_in_dw")]
    (dx0,), (g_an0,) = stage_bwd(f_rmsnorm_res, [x0], [P["attn_norm"][0:1]], [], [], [dh0, dx1], ROW_TILE, "l0_norm_bwd")

    G["attn_norm"] = jnp.concatenate([g_an0, g_an1], axis=0)
    G["a_mu"] = jnp.concatenate(gpre[:4], axis=1)
    G["a_w0"], G["a_w2"], G["a_a0"], G["a_a2"], G["a_g2"] = gpre[4], gpre[5][None, :64], gpre[6], gpre[7][None, 64:128], gpre[8][None, 128:]
    G["a_k_k"], G["a_k_a"] = gpre[9], gpre[10]
    G["a_r_k"] = g_rk.reshape(1, RWKV_HEADS, HEAD_DIM)
    G["a_lnx_w"], G["a_lnx_b"] = g_lw, g_lb
    G["kv_norm"], G["kv_k_norm"], G["b_q_norm"] = g_kvn.reshape(-1), g_kk.reshape(-1), g_bq
    G["mem_norm"] = jnp.concatenate([gmn0, gmn1], axis=0)
    G["mem_w_kv"] = [gmw0, gmw1]
    G["mem_q_norm"] = jnp.concatenate([gmq0, gmq1], axis=0)
    G["mem_k_norm"] = jnp.concatenate([gmk0, gmk1], axis=0)
    G["ffn_norm"] = jnp.concatenate([gn0, gn1], axis=0)
    G["ffn_w_up"] = [gup0, gup1]
    G["ffn_conv_w"] = jnp.stack([gcw0, gcw1])
    G["ffn_conv_b"] = jnp.concatenate([gcb0, gcb1], axis=0)
    G["ffn_w_down"] = [gdown0, gdown1]
    return loss_part, dx0, G


PARAMS = (("attn_norm", None), ("a_w_in", 2), ("a_mu", 1), ("a_w0", 1), ("a_w2", 2), ("a_a0", 1), ("a_a2", 2), ("a_g2", 2),
          ("a_k_k", 1), ("a_k_a", 1), ("a_r_k", None), ("a_lnx_w", 1), ("a_lnx_b", 1), ("a_w_out", 1), ("kv_norm", None),
          ("kv_w", 1), ("kv_k_norm", None), ("b_w_in", 1), ("b_q_norm", None), ("b_w_out", 2), ("mem_norm", None),
          ("mem_w_kv", 1), ("mem_q_norm", None), ("mem_k_norm", None), ("ffn_norm", None), ("ffn_w_up", 2),
          ("ffn_conv_w", 2), ("ffn_conv_b", None), ("ffn_w_down", 1))
BIG = ("a_w_in", "a_w_out", "kv_w", "b_w_in", "b_w_out", "mem_w_kv", "ffn_w_up", "ffn_w_down")
AXIS = dict(PARAMS)
SMALL = tuple(n for n, _ in PARAMS if n not in BIG)
SMALL_SHARDED = tuple(n for n in SMALL if AXIS[n] is not None)
PACK_QUANTUM = 256 * 128


def _from_shards(xs, axis):
    full = jnp.moveaxis(xs, 0, axis)
    sh = full.shape
    return full.reshape(sh[:axis] + (sh[axis] * sh[axis + 1],) + sh[axis + 2:])


def _to_shards(g, axis):
    sh = g.shape
    return jnp.moveaxis(g.reshape(sh[:axis] + (N_DEV, sh[axis] // N_DEV) + sh[axis + 1:]), axis, 0)


def _pack(parts, lead=0):
    ld = parts[0].shape[:lead]
    flat = jnp.concatenate([p.reshape(ld + (-1,)) for p in parts], axis=-1)
    pad = (-flat.shape[-1]) % PACK_QUANTUM
    flat = jnp.pad(flat, [(0, 0)] * lead + [(0, pad)])
    return flat.reshape(ld + (-1, 128))


def _unpack(packed, shapes, lead=0):
    ld = packed.shape[:lead]
    flat = packed.reshape(ld + (-1,))
    out, off = [], 0
    for s in shapes:
        n = math.prod(s)
        out.append(flat[..., off:off + n].reshape(ld + tuple(s)))
        off += n
    return out


def kernel(x, mem, attn_norm, a_w_in, a_mu, a_w0, a_w2, a_a0, a_a2, a_g2, a_k_k, a_k_a, a_r_k, a_lnx_w, a_lnx_b, a_w_out, kv_norm, kv_w, kv_k_norm, b_w_in, b_q_norm, b_w_out, mem_norm, mem_w_kv, mem_q_norm, mem_k_norm, ffn_norm, ffn_w_up, ffn_conv_w, ffn_conv_b, ffn_w_down, loss_target, m_attn_norm, m_a_w_in, m_a_mu, m_a_w0, m_a_w2, m_a_a0, m_a_a2, m_a_g2, m_a_k_k, m_a_k_a, m_a_r_k, m_a_lnx_w, m_a_lnx_b, m_a_w_out, m_kv_norm, m_kv_w, m_kv_k_norm, m_b_w_in, m_b_q_norm, m_b_w_out, m_mem_norm, m_mem_w_kv, m_mem_q_norm, m_mem_k_norm, m_ffn_norm, m_ffn_w_up, m_ffn_conv_w, m_ffn_conv_b, m_ffn_w_down, v_attn_norm, v_a_w_in, v_a_mu, v_a_w0, v_a_w2, v_a_a0, v_a_a2, v_a_g2, v_a_k_k, v_a_k_a, v_a_r_k, v_a_lnx_w, v_a_lnx_b, v_a_w_out, v_kv_norm, v_kv_w, v_kv_k_norm, v_b_w_in, v_b_q_norm, v_b_w_out, v_mem_norm, v_mem_w_kv, v_mem_q_norm, v_mem_k_norm, v_ffn_norm, v_ffn_w_up, v_ffn_conv_w, v_ffn_conv_b, v_ffn_w_down):
    names = [n for n, _ in PARAMS]
    vals = (attn_norm, a_w_in, a_mu, a_w0, a_w2, a_a0, a_a2, a_g2, a_k_k, a_k_a, a_r_k, a_lnx_w, a_lnx_b, a_w_out, kv_norm, kv_w, kv_k_norm, b_w_in, b_q_norm, b_w_out, mem_norm, mem_w_kv, mem_q_norm, mem_k_norm, ffn_norm, ffn_w_up, ffn_conv_w, ffn_conv_b, ffn_w_down)
    m_vals = (m_attn_norm, m_a_w_in, m_a_mu, m_a_w0, m_a_w2, m_a_a0, m_a_a2, m_a_g2, m_a_k_k, m_a_k_a, m_a_r_k, m_a_lnx_w, m_a_lnx_b, m_a_w_out, m_kv_norm, m_kv_w, m_kv_k_norm, m_b_w_in, m_b_q_norm, m_b_w_out, m_mem_norm, m_mem_w_kv, m_mem_q_norm, m_mem_k_norm, m_ffn_norm, m_ffn_w_up, m_ffn_conv_w, m_ffn_conv_b, m_ffn_w_down)
    v_vals = (v_attn_norm, v_a_w_in, v_a_mu, v_a_w0, v_a_w2, v_a_a0, v_a_a2, v_a_g2, v_a_k_k, v_a_k_a, v_a_r_k, v_a_lnx_w, v_a_lnx_b, v_a_w_out, v_kv_norm, v_kv_w, v_kv_k_norm, v_b_w_in, v_b_q_norm, v_b_w_out, v_mem_norm, v_mem_w_kv, v_mem_q_norm, v_mem_k_norm, v_ffn_norm, v_ffn_w_up, v_ffn_conv_w, v_ffn_conv_b, v_ffn_w_down)
    W, M, V = dict(zip(names, vals)), dict(zip(names, m_vals)), dict(zip(names, v_vals))
    me = 4 * lax.axis_index("x") + 2 * lax.axis_index("y") + lax.axis_index("c")
    layers = lambda D, n: [D[n]] if D[n].ndim == 2 else [D[n][i] for i in range(D[n].shape[0])]
    ax2 = lambda n: AXIS[n] - (W[n].ndim - 2)
    first = [("a_w_in", 0)]
    later = [(n, i) for n in BIG if n != "a_w_in" for i in range(len(layers(W, n)))]

    small_shapes = [W[n].shape for n in SMALL_SHARDED]
    got_w, got_small = exchange(Exchange(gathers=[W["a_w_in"][0].astype(BF16), _pack([W[n] for n in SMALL_SHARDED])]),
                                "gather_first")
    P = {n: W[n] for n in SMALL}
    P["a_w_in"] = [_from_shards(got_w, ax2("a_w_in"))]
    for n, s in zip(SMALL_SHARDED, _unpack(got_small, small_shapes, lead=1)):
        P[n] = _from_shards(s, AXIS[n])
    ex_weights = Exchange(gathers=[layers(W, n)[i].astype(BF16) for n, i in later])

    def weights_done(got):
        out = {}
        for (n, _), g in zip(later, got):
            out.setdefault(n, []).append(_from_shards(g, ax2(n)))
        return out

    slots = lambda G, n: jnp.stack([_to_shards(g, ax2(n)) for g in G[n]], axis=1)
    later_names = [n for n in BIG if n != "a_w_in"]
    ex_grads = lambda G: Exchange(scatters=[slots(G, n) for n in later_names])
    loss_part, dx0, G = _local_step(x[0], mem[0], loss_target[0], P, ex_weights, weights_done, ex_grads)
    loss = lax.psum(loss_part[0, 0], ("x", "y", "c"))
    gparts = dict(zip(later_names, G.pop("_exchanged")))
    got_gsmall, gparts["a_w_in"] = exchange(
        Exchange(gathers=[_pack([G[n] for n in SMALL])], scatters=[slots(G, "a_w_in")]), "exchange_last")

    results = {}
    for n in BIG:
        rows = lambda z: z.reshape((-1,) + z.shape[-1:])
        res = adamw(gparts[n].reshape((N_DEV, -1) + gparts[n].shape[-1:]), rows(W[n]), rows(M[n]), rows(V[n]), f"adamw_{n}")
        results[n] = [r.reshape(W[n].shape) for r in res]
    g_small = sum_parts(got_gsmall, "sum_small_grads")
    mine = []
    for n, g in zip(SMALL, _unpack(g_small, [G[n].shape for n in SMALL])):
        if AXIS[n] is not None:
            s = W[n].shape[AXIS[n]]
            g = lax.dynamic_slice_in_dim(g, me * s, s, axis=AXIS[n])
        mine.append(g)
    res = adamw(_pack(mine)[None], _pack([W[n] for n in SMALL]), _pack([M[n] for n in SMALL]), _pack([V[n] for n in SMALL]),
                "adamw_small")
    for n, parts in zip(SMALL, zip(*[_unpack(r, [W[n].shape for n in SMALL]) for r in res])):
        results[n] = list(parts)
    outs = [[results[n][j] for n in names] for j in range(4)]
    return (loss, dx0[None], *outs[0], *outs[1], *outs[2], *outs[3])
```

```python
import functools
import math

import jax
import jax.numpy as jnp
import numpy as np
from jax import lax
from jax.experimental import pallas as pl
from jax.experimental.pallas import tpu as pltpu

F32 = jnp.float32
BF16 = jnp.bfloat16
HI = lax.Precision.HIGHEST
H3 = lax.Precision.HIGH

N_DEV = 8
D_MODEL = 1024
HEAD_DIM = 64
N_MEM = 256
MEM_HEADS = 4
MEM_WIDTH = 256
RWKV_HEADS = 12
RWKV_WIDTH = 768
SHIFT_WIDTH = 2560
DIL_GROUPS = ((128, 1), (512, 4), (2048, 16))
DIL_BLOCK = 128
DIL_WIDTH = 768
D_FF = 2816
RMS_EPS = 1e-6
LNX_EPS = 64e-5
NEG_INF = -1e30
ROPE_THETA = 10000.0
ADAM_LR, ADAM_B1, ADAM_B2, ADAM_EPS, ADAM_WD, ADAM_STEP = 0.001, 0.9, 0.999, 1e-08, 0.01, 10

CHUNK = 64
SCAN_GROUPS_FWD, SCAN_GROUPS_BWD = 1, 1
MM_TILE_CAP = 1408
VMEM_LIMIT_V7X = 48 * 1024 * 1024


def _cparams(sem):
    return pltpu.CompilerParams(dimension_semantics=sem, vmem_limit_bytes=VMEM_LIMIT_V7X)


def _pick(n, cands):
    for c in cands:
        if n % c == 0:
            return c
    return n


def _tile(n, cap):
    if n <= cap:
        return n
    for d in range(cap - cap % 128, 0, -128):
        if n % d == 0:
            return d
    return n


def _dg(a, b, ca, cb, batch):
    dims = (((ca,), (cb,)), ((0,), (0,))) if batch else (((ca,), (cb,)), ((), ()))
    return lax.dot_general(a.astype(BF16), b.astype(BF16), dims, preferred_element_type=F32)


@jax.custom_vjp
def mm_nn(a, b):
    n = a.ndim
    return _dg(a, b, n - 1, n - 2, n == 3)


def _mm_nn_fwd(a, b):
    return mm_nn(a, b), (a, b)


def _mm_nn_bwd(res, g):
    a, b = res
    n = a.ndim
    return _dg(g, b, n - 1, n - 1, n == 3), _dg(a, g, n - 2, n - 2, n == 3)


mm_nn.defvjp(_mm_nn_fwd, _mm_nn_bwd)


@jax.custom_vjp
def mm_nt(a, b):
    n = a.ndim
    return _dg(a, b, n - 1, n - 1, n == 3)


def _mm_nt_fwd(a, b):
    return mm_nt(a, b), (a, b)


def _mm_nt_bwd(res, g):
    a, b = res
    n = a.ndim
    return _dg(g, b, n - 1, n - 2, n == 3), _dg(g, a, n - 2, n - 2, n == 3)


mm_nt.defvjp(_mm_nt_fwd, _mm_nt_bwd)


def mmh(a, b, precision=H3):
    n = a.ndim
    dims = (((n - 1,), (n - 2,)), ((0,), (0,))) if n == 3 else (((1,), (0,)), ((), ()))
    return lax.dot_general(a, b, dims, precision=precision, preferred_element_type=F32)


def mmh_nt(a, b):
    n = a.ndim
    dims = (((n - 1,), (n - 1,)), ((0,), (0,))) if n == 3 else (((1,), (1,)), ((), ()))
    return lax.dot_general(a, b, dims, precision=H3, preferred_element_type=F32)


def mmh_tn(a, b):
    n = a.ndim
    dims = (((n - 2,), (n - 2,)), ((0,), (0,))) if n == 3 else (((0,), (0,)), ((), ()))
    return lax.dot_general(a, b, dims, precision=H3, preferred_element_type=F32)


def matmul(a, b, mode, name, residual=None):
    out_dtype = BF16 if mode == "tn" else F32
    if mode == "nn":
        (M, K), (_, N) = a.shape, b.shape
    elif mode == "nt":
        (M, K), (N, _) = a.shape, b.shape
    else:
        (K, M), (_, N) = a.shape, b.shape
    tm = _tile(M, 2048 if mode == "nn" else MM_TILE_CAP)
    tn = _tile(N, 512 if mode == "nn" else MM_TILE_CAP)
    tk = _tile(K, MM_TILE_CAP if mode != "nt" else 512)
    nk = K // tk
    if mode == "nn":
        a_spec = pl.BlockSpec((tm, tk), lambda i, j, k: (i, k))
        b_spec = pl.BlockSpec((tk, tn), lambda i, j, k: (k, j))
        dims = (((1,), (0,)), ((), ()))
    elif mode == "nt":
        a_spec = pl.BlockSpec((tm, tk), lambda i, j, k: (i, k))
        b_spec = pl.BlockSpec((tn, tk), lambda i, j, k: (j, k))
        dims = (((1,), (1,)), ((), ()))
    else:
        a_spec = pl.BlockSpec((tk, tm), lambda i, j, k: (k, i))
        b_spec = pl.BlockSpec((tk, tn), lambda i, j, k: (k, j))
        dims = (((0,), (0,)), ((), ()))
    o_spec = pl.BlockSpec((tm, tn), lambda i, j, k: (i, j))
    has_res = residual is not None

    def body(*refs):
        if has_res:
            a_ref, b_ref, r_ref, o_ref, acc_ref = refs
        else:
            a_ref, b_ref, o_ref, acc_ref = refs
        k = pl.program_id(2)

        @pl.when(k == 0)
        def _():
            acc_ref[...] = jnp.zeros_like(acc_ref)

        acc_ref[...] += lax.dot_general(a_ref[...].astype(BF16), b_ref[...].astype(BF16), dims,
                                        preferred_element_type=F32)

        @pl.when(k == nk - 1)
        def _():
            if has_res:
                o_ref[...] = (acc_ref[...] + r_ref[...]).astype(out_dtype)
            else:
                o_ref[...] = acc_ref[...].astype(out_dtype)

    ins = [a, b] + ([residual] if has_res else [])
    in_specs = [a_spec, b_spec] + ([o_spec] if has_res else [])
    return pl.pallas_call(
        body, grid=(M // tm, N // tn, nk), in_specs=in_specs, out_specs=o_spec,
        out_shape=jax.ShapeDtypeStruct((M, N), out_dtype), scratch_shapes=[pltpu.VMEM((tm, tn), F32)],
        compiler_params=_cparams(("parallel", "parallel", "arbitrary")), name=name)(*ins)


class Cols:
    def __init__(self, arr, width, idx):
        self.arr, self.width, self.idx = arr, width, idx


def _arr(x):
    return x.arr if isinstance(x, Cols) else x


def _shape(x):
    return x.arr.shape[:-1] + (x.width,) if isinstance(x, Cols) else x.shape


def _col(x):
    return x.idx if isinstance(x, Cols) else 0


def _tok_spec(x, tile):
    shape, col = _shape(x), _col(x)
    return pl.BlockSpec(shape[:-2] + (tile, shape[-1]), lambda i: (0,) * (len(shape) - 2) + (i, col))


def _full_spec(x):
    shape, col = _shape(x), _col(x)
    return pl.BlockSpec(shape, lambda i: (0,) * (len(shape) - 1) + (col,))


def _halo_spec(x, tile):
    shape, col = _shape(x), _col(x)
    return pl.BlockSpec((8, shape[-1]), lambda i: (jnp.maximum(i * (tile // 8) - 1, 0), col))


def _blk(x, tile):
    shape = _shape(x)
    return jax.ShapeDtypeStruct(shape[:-2] + (tile, shape[-1]), _arr(x).dtype)


def _prev_rows(x, halo):
    rows = lax.broadcasted_iota(jnp.int32, (x.shape[0], 1), 0)
    before = jnp.where(pl.program_id(0) > 0, halo[7:8], 0.0)
    return jnp.where(rows == 0, before, pltpu.roll(x, 1, 0))


def stage_fwd(f, xs, ps, cts, cfs, tile, name, out_dtypes=None, with_prev=False):
    xs, ps, cts, cfs = list(xs), list(ps), list(cts), list(cfs)
    halos = xs if with_prev else []
    nx, nh, nct, np_ = len(xs), len(halos), len(cts), len(ps)
    T = _shape(xs[0])[-2]
    blk = [_blk(x, tile) for x in xs]
    out_avals = jax.eval_shape(f, *blk, *(blk if with_prev else []), *[_blk(p, _shape(p)[-2]) for p in ps],
                               *[_blk(c, tile) for c in cts], *[_blk(c, _shape(c)[-2]) for c in cfs])
    if out_dtypes is None:
        out_dtypes = [o.dtype for o in out_avals]
    out_shape = [jax.ShapeDtypeStruct(o.shape[:-2] + (T, o.shape[-1]), dt) for o, dt in zip(out_avals, out_dtypes)]
    n_in = nx + nh + nct + np_ + len(cfs)

    def body(*refs):
        vals = [r[...] for r in refs[:n_in]]
        xv, hv, rest = vals[:nx], vals[nx:nx + nh], vals[nx + nh:]
        ctv, pv, cfv = rest[:nct], rest[nct:nct + np_], rest[nct + np_:]
        prev = [_prev_rows(x, h) for x, h in zip(xv, hv)]
        res = f(*xv, *prev, *pv, *ctv, *cfv)
        for o_ref, r in zip(refs[n_in:], res):
            o_ref[...] = r.astype(o_ref.dtype)

    return pl.pallas_call(
        body, grid=(T // tile,),
        in_specs=([_tok_spec(x, tile) for x in xs] + [_halo_spec(x, tile) for x in halos] + [_tok_spec(c, tile) for c in cts]
                  + [_full_spec(p) for p in ps + cfs]),
        out_specs=[_tok_spec(o, tile) for o in out_shape], out_shape=out_shape,
        compiler_params=_cparams(("parallel",)), name=name)(*[_arr(a) for a in xs + halos + cts + ps + cfs])


def stage_bwd(f, xs, ps, cts, cfs, gs, tile, name, bf16_copies=(), with_prev=False):
    xs, ps, cts, cfs = list(xs), list(ps), list(cts), list(cfs)
    gs = [list(g) if isinstance(g, (list, tuple)) else [g] for g in gs]
    g_flat = [a for g in gs for a in g]
    halos = xs if with_prev else []
    nx, nh, nct, ng, np_ = len(xs), len(halos), len(cts), len(g_flat), len(ps)
    T = _shape(xs[0])[-2]
    dx_like = xs + halos
    out_shape = ([jax.ShapeDtypeStruct(_shape(x), F32) for x in dx_like] + [jax.ShapeDtypeStruct(_shape(p), F32) for p in ps]
                 + [jax.ShapeDtypeStruct(_shape(xs[i]), BF16) for i in bf16_copies])
    n_in = nx + nh + nct + ng + np_ + len(cfs)
    ndx = nx + nh

    def body(*refs):
        vals = [r[...] for r in refs[:n_in]]
        outs = refs[n_in:]
        xv, hv, rest = vals[:nx], vals[nx:nx + nh], vals[nx + nh:]
        ctv, gparts, pv, cfv = rest[:nct], rest[nct:nct + ng], rest[nct + ng:nct + ng + np_], rest[nct + ng + np_:]
        gv = []
        for g in gs:
            gv.append(functools.reduce(lambda a, b: a + b, gparts[:len(g)]))
            gparts = gparts[len(g):]
        prev = [_prev_rows(x, h) for x, h in zip(xv, hv)]
        _, vjp = jax.vjp(lambda *xp: f(*xp, *ctv, *cfv), *xv, *prev, *pv)
        d = vjp(tuple(gv))
        for o_ref, r in zip(outs[:ndx], d[:ndx]):
            o_ref[...] = r
        for o_ref, i in zip(outs[ndx + np_:], bf16_copies):
            o_ref[...] = d[i].astype(BF16)

        @pl.when(pl.program_id(0) == 0)
        def _():
            for o_ref in outs[ndx:ndx + np_]:
                o_ref[...] = jnp.zeros_like(o_ref)

        for o_ref, r in zip(outs[ndx:ndx + np_], d[ndx:]):
            o_ref[...] += r

    plain = lambda x: jax.ShapeDtypeStruct(_shape(x), F32)
    res = pl.pallas_call(
        body, grid=(T // tile,),
        in_specs=([_tok_spec(x, tile) for x in xs] + [_halo_spec(x, tile) for x in halos]
                  + [_tok_spec(c, tile) for c in cts + g_flat] + [_full_spec(p) for p in ps + cfs]),
        out_specs=([_tok_spec(plain(x), tile) for x in dx_like] + [_full_spec(plain(p)) for p in ps]
                   + [_tok_spec(plain(xs[i]), tile) for i in bf16_copies]), out_shape=out_shape,
        compiler_params=_cparams(("arbitrary",)), name=name)(*[_arr(a) for a in xs + halos + cts + g_flat + ps + cfs])
    if bf16_copies:
        return list(res[:ndx]), list(res[ndx:ndx + np_]), list(res[ndx + np_:])
    return list(res[:ndx]), list(res[ndx:])


def _rms(x, g, eps=RMS_EPS):
    return x * lax.rsqrt(jnp.mean(x * x, axis=-1, keepdims=True) + eps) * g


def f_rmsnorm(x, g):
    return (_rms(x, g),)


def f_rmsnorm_res(x, g):
    return _rms(x, g), x


def f_rmsnorm2(x, g1, g2):
    n = x * lax.rsqrt(jnp.mean(x * x, axis=-1, keepdims=True) + RMS_EPS)
    return n * g1, n * g2


def f_rmsnorm2_res(x, g1, g2):
    return f_rmsnorm2(x, g1, g2) + (x,)


def _sigmoid(x):
    return 1.0 / (1.0 + jnp.exp(-x))


def _softplus(x):
    return jnp.maximum(x, 0.0) + jnp.log(1.0 + jnp.exp(-jnp.abs(x)))


def f_rwkv_pre(pr, pk, pv, pl_, qr, qk, qv, ql, mu_r, mu_k, mu_v, mu_l, w0, w2, a0, a2, g2, k_k, k_a, seg, seg_t):
    xr = pr + (qr - pr) * mu_r
    xk = pk + (qk - pk) * mu_k
    xv = pv + (qv - pv) * mu_v
    xl = pl_ + (ql - pl_) * mu_l
    w_log = -_softplus(-(w0 + mm_nn(jnp.tanh(xl), w2))) - 0.5
    lw = -jnp.exp(w_log)
    a = _sigmoid(a0 + mm_nn(xl, a2))
    g = mm_nn(_sigmoid(xl), g2)
    kkr = xk * k_k
    inv = lax.rsqrt(jnp.maximum(mmh(kkr * kkr, seg), 1e-24))
    kk = kkr * mmh(inv, seg_t)
    k2 = xk * (1.0 + (a - 1.0) * k_a)
    return xr, lw, k2, xv, kk, kk * a, g


def f_rwkv_post(y, r, k2, v, g, lnx_w, lnx_b, r_k, seg, seg_t):
    inv_n = 1.0 / HEAD_DIM
    m = mmh(mmh(y, seg) * inv_n, seg_t)
    yc = y - m
    rstd = lax.rsqrt(mmh(yc * yc, seg) * inv_n + LNX_EPS)
    yn = yc * mmh(rstd, seg_t) * lnx_w + lnx_b
    bonus = mmh(mmh(r * k2 * r_k, seg), seg_t) * v
    return ((yn + bonus) * g,)


def _headnorm(z, g, seg, seg_t):
    ms = mmh(z * z, seg) * (1.0 / HEAD_DIM)
    return z * mmh(lax.rsqrt(ms + RMS_EPS), seg_t) * g


def f_headnorm(z, g, seg, seg_t):
    return (_headnorm(z, g, seg, seg_t),)


def _rot_half(z):
    w = z.shape[1]
    half = HEAD_DIM // 2
    lane = lax.broadcasted_iota(jnp.int32, (1, w), 1)
    return jnp.where((lane & (HEAD_DIM - 1)) < half, -pltpu.roll(z, w - half, 1), pltpu.roll(z, half, 1))


@jax.custom_vjp
def _rotate_half(z):
    return _rot_half(z)


_rotate_half.defvjp(lambda z: (_rot_half(z), None), lambda _, g: (-_rot_half(g),))


def f_qkprep(z, g, cos, sin, seg, seg_t):
    zn = _headnorm(z, g, seg, seg_t)
    return (zn * cos + _rotate_half(zn) * sin,)


def _head_mask(width, h):
    lane = lax.broadcasted_iota(jnp.int32, (1, width), 1)
    return jnp.where((lane >> 6) == h, jnp.ones((), F32), 0.0)


def f_memattn(q, k, v, q_norm, seg, seg_t):
    qn = _headnorm(q, q_norm, seg, seg_t)
    out = jnp.zeros_like(q)
    for h in range(MEM_HEADS):
        m = _head_mask(MEM_WIDTH, h)
        s = mm_nt(qn * m, k) * (1.0 / math.sqrt(HEAD_DIM))
        s = s - jnp.max(s, axis=-1, keepdims=True)
        p = jnp.exp(s)
        p = p / jnp.sum(p, axis=-1, keepdims=True)
        out = out + mm_nn(p, v) * m
    return (out,)


def f_mix(o1, o2, o3, l1, l2, l3):
    mx = jnp.maximum(jnp.maximum(l1, l2), l3)
    e1, e2, e3 = jnp.exp(l1 - mx), jnp.exp(l2 - mx), jnp.exp(l3 - mx)
    return ((e1 * o1 + e2 * o2 + e3 * o3) / (e1 + e2 + e3),)


def _chunk_masks(L):
    t = lax.broadcasted_iota(jnp.int32, (L, L), 0)
    s = lax.broadcasted_iota(jnp.int32, (L, L), 1)
    return t, s


def _unit_lower_inverse(a):
    L = a.shape[-1]
    t, s = _chunk_masks(L)
    one = jnp.ones((), F32)
    blk = lambda sh: jnp.where((t >> sh) == (s >> sh), one, 0.0)
    n0 = a * blk(3)
    x = jnp.where(t == s, one, 0.0) - n0
    n2 = mmh(n0, n0)
    x = x + mmh(x, n2)
    x = x + mmh(x, mmh(n2, n2))
    for sh in (3, 4, 5):
        if (1 << sh) >= L:
            break
        off = a * (blk(sh + 1) - blk(sh))
        x = x - mmh(x, mmh(off, x))
    return x


@jax.custom_vjp
def _inverse_known(a, x):
    return x


def _inverse_known_fwd(a, x):
    return x, x


def _inverse_known_bwd(x, dx):
    return -mmh_nt(mmh_tn(x, dx), x), jnp.zeros_like(x)


_inverse_known.defvjp(_inverse_known_fwd, _inverse_known_bwd)


def _running_sum(x, reverse):
    L = x.shape[1]
    pos = lax.broadcasted_iota(jnp.int32, (1, L, 1), 1)
    step = 1
    while step < L:
        if reverse:
            x = x + jnp.where(pos < L - step, pltpu.roll(x, L - step, 1), 0.0)
        else:
            x = x + jnp.where(pos >= step, pltpu.roll(x, step, 1), 0.0)
        step *= 2
    return x


@jax.custom_vjp
def _cumsum_tokens(x):
    return _running_sum(x, False)


_cumsum_tokens.defvjp(lambda x: (_running_sum(x, False), None), lambda _, g: (_running_sum(g, True),))


def f_rwkv_chunk(s0, r, lw, k, v, kk, b, x_known=None):
    H, L, _ = r.shape
    t, s = _chunk_masks(L)
    one = jnp.ones((), F32)
    incl = jnp.where(t >= s, one, 0.0)
    strict = jnp.where(t > s, one, 0.0)
    cum = _cumsum_tokens(lw)
    w_in = jnp.exp(cum)
    w_ex = jnp.exp(cum - lw)
    w_inv = jnp.exp(-cum)
    rt, kkt, kt, bt = r * w_in, kk * w_ex, k * w_inv, b * w_inv
    a_b = mmh_nt(kkt, bt) * strict
    a_k = mmh_nt(kkt, kt) * strict
    m_k = mmh_nt(rt, kt) * incl
    m_b = mmh_nt(rt, bt) * incl
    x = _unit_lower_inverse(a_b) if x_known is None else _inverse_known(a_b, x_known)
    u = mmh(x, mmh_nt(kkt, s0) + mmh(a_k, v))
    y = mmh_nt(rt, s0) + mmh(m_k, v) - mmh(m_b, u)
    w_last = jnp.exp(jnp.sum(lw, axis=1, keepdims=True))
    s1 = (s0 + mmh_tn(v, kt) - mmh_tn(u, bt)) * w_last
    return y, s1, x


def _ex_split(ex, refs, n_in, n_out):
    n = ex.n
    ins, ex_in = refs[:n_in], refs[n_in:n_in + n]
    outs, ex_out = refs[n_in + n:n_in + n + n_out], refs[n_in + n + n_out:n_in + 2 * n + n_out]
    rest = refs[n_in + 2 * n + n_out:]
    return ins, outs, rest[:len(rest) - 3], (ex_in, ex_out) + tuple(rest[len(rest) - 3:])


def _split_heads(x):
    return jnp.stack([x[:, h * HEAD_DIM:(h + 1) * HEAD_DIM] for h in range(x.shape[1] // HEAD_DIM)], axis=0)


def _merge_heads(x):
    return jnp.concatenate([x[h] for h in range(x.shape[0])], axis=1)


def rwkv_scan_fwd(r, lw, k, v, kk, b, ex):
    T, N = r.shape[0], HEAD_DIM
    H = r.shape[1] // N
    groups = SCAN_GROUPS_FWD
    nc, hg = T // CHUNK, H // groups
    seq = pl.BlockSpec((CHUNK, hg * N), lambda g, c: (c, g))

    def body(*refs):
        (r_ref, lw_ref, k_ref, v_ref, kk_ref, b_ref), (y_ref, hs_ref, xs_ref), (h_scr,), ex_refs = _ex_split(ex, refs, 6, 3)
        g, c = pl.program_id(0), pl.program_id(1)

        @pl.when(jnp.logical_and(g == 0, c == 0))
        def _():
            ex.start(*ex_refs)

        @pl.when(c == 0)
        def _():
            h_scr[...] = jnp.zeros_like(h_scr)

        h0 = h_scr[...]
        hs_ref[0] = h0
        y, h1, x = f_rwkv_chunk(h0, *[_split_heads(z[...]) for z in (r_ref, lw_ref, k_ref, v_ref, kk_ref, b_ref)])
        y_ref[...] = _merge_heads(y)
        xs_ref[0] = x
        h_scr[...] = h1

        @pl.when(jnp.logical_and(g == groups - 1, c == (3 * nc) // 4))
        def _():
            ex.forward(*ex_refs)

        @pl.when(jnp.logical_and(g == groups - 1, c == nc - 1))
        def _():
            ex.wait(*ex_refs)

    res = pl.pallas_call(
        body, grid=(groups, nc), in_specs=[seq] * 6 + [_ANY] * ex.n,
        out_specs=[seq, pl.BlockSpec((1, hg, N, N), lambda g, c: (c, g, 0, 0)),
                   pl.BlockSpec((1, hg, CHUNK, CHUNK), lambda g, c: (c, g, 0, 0))] + [_ANY] * ex.n,
        out_shape=[jax.ShapeDtypeStruct((T, H * N), F32), jax.ShapeDtypeStruct((nc, H, N, N), F32),
                   jax.ShapeDtypeStruct((nc, H, CHUNK, CHUNK), F32)] + ex.out_shape(),
        scratch_shapes=[pltpu.VMEM((hg, N, N), F32)] + ex.scratch(),
        compiler_params=_cparams(("arbitrary", "arbitrary")), name="rwkv_scan_fwd")(r, lw, k, v, kk, b, *ex.operands())
    return res[0], (res[1], res[2]), list(res[3:])


def rwkv_scan_bwd(r, lw, k, v, kk, b, saved, dy, ex):
    T, N = r.shape[0], HEAD_DIM
    H = r.shape[1] // N
    groups = SCAN_GROUPS_BWD
    nc, hg = T // CHUNK, H // groups
    seq = pl.BlockSpec((CHUNK, hg * N), lambda g, c: (nc - 1 - c, g))
    state = pl.BlockSpec((1, hg, N, N), lambda g, c: (nc - 1 - c, g, 0, 0))

    def body(*refs):
        (r_ref, lw_ref, k_ref, v_ref, kk_ref, b_ref, hs_ref, xs_ref, dy_ref), outs, (dh_scr,), ex_refs = _ex_split(ex, refs, 9, 6)
        g, c = pl.program_id(0), pl.program_id(1)

        @pl.when(jnp.logical_and(g == 0, c == 0))
        def _():
            ex.start(*ex_refs)

        @pl.when(c == 0)
        def _():
            dh_scr[...] = jnp.zeros_like(dh_scr)

        x_known = xs_ref[0]
        _, vjp = jax.vjp(lambda *a: f_rwkv_chunk(*a, x_known=x_known)[:2], hs_ref[0],
                         *[_split_heads(z[...]) for z in (r_ref, lw_ref, k_ref, v_ref, kk_ref, b_ref)])
        d = vjp((_split_heads(dy_ref[...]), dh_scr[...]))
        dh_scr[...] = d[0]
        for o_ref, dz in zip(outs, d[1:]):
            o_ref[...] = _merge_heads(dz)

        @pl.when(jnp.logical_and(g == groups - 1, c == nc - 1))
        def _():
            ex.forward(*ex_refs)
            ex.wait(*ex_refs)

    res = pl.pallas_call(
        body, grid=(groups, nc),
        in_specs=[seq] * 6 + [state, state, seq] + [_ANY] * ex.n,
        out_specs=[seq] * 6 + [_ANY] * ex.n, out_shape=[jax.ShapeDtypeStruct((T, H * N), F32)] * 6 + ex.out_shape(),
        scratch_shapes=[pltpu.VMEM((hg, N, N), F32)] + ex.scratch(),
        compiler_params=_cparams(("arbitrary", "arbitrary")), name="rwkv_scan_bwd")(r, lw, k, v, kk, b, *saved, dy, *ex.operands())
    return list(res[:6]), list(res[6:])


PAIR = 2 * HEAD_DIM


def _f_dilattn(has_prev, q, kc, kp, vc, vp):
    scale = 1.0 / math.sqrt(HEAD_DIM)
    i = lax.broadcasted_iota(jnp.int32, (DIL_BLOCK, DIL_BLOCK), 0)
    j = lax.broadcasted_iota(jnp.int32, (DIL_BLOCK, DIL_BLOCK), 1)
    o, l = jnp.zeros_like(q), jnp.zeros_like(q)
    for h in range(2):
        m = _head_mask(PAIR, h)
        sc = jnp.where(j <= i, mm_nt(q * m, kc) * scale, NEG_INF)
        sp = jnp.where(jnp.logical_and(i <= j, has_prev), mm_nt(q * m, kp) * scale, NEG_INF)
        mx = jnp.maximum(jnp.max(sc, axis=-1, keepdims=True), jnp.max(sp, axis=-1, keepdims=True))
        pc, pp = jnp.exp(sc - mx), jnp.exp(sp - mx)
        den = jnp.sum(pc, axis=-1, keepdims=True) + jnp.sum(pp, axis=-1, keepdims=True)
        o = o + (mm_nn(pc, vc) + mm_nn(pp, vp)) / den * m
        l = l + (mx + jnp.log(den)) * m
    return o, l


def _dil_specs(gi, d):
    blk = (DIL_BLOCK * d, PAIR)
    at = lambda off: (lambda p, n: (n, off + p))
    before = lambda off: (lambda p, n: (jnp.maximum(n - 1, 0), off + p))
    pair0 = 2 * gi
    v0 = DIL_WIDTH // PAIR + pair0
    q = pl.BlockSpec(blk, at(pair0))
    kc, kp = pl.BlockSpec(blk, at(pair0)), pl.BlockSpec(blk, before(pair0))
    vc, vp = pl.BlockSpec(blk, at(v0)), pl.BlockSpec(blk, before(v0))
    out = pl.BlockSpec(blk, at(0))
    return q, kc, kp, vc, vp, out


def _residue_rows(r, d):
    return pl.ds(r, DIL_BLOCK, stride=d) if d > 1 else pl.ds(0, DIL_BLOCK)


def dil_fwd(q, k, kv, gi, d, name):
    T = q.shape[0]
    qs, kc, kp, vc, vp, out = _dil_specs(gi, d)

    def body(q_ref, kc_ref, kp_ref, vc_ref, vp_ref, o_ref, l_ref):
        has_prev = pl.program_id(1) > 0

        def residue(r, carry):
            rows = _residue_rows(r, d)
            o, l = _f_dilattn(has_prev, q_ref[rows, :], kc_ref[rows, :], kp_ref[rows, :], vc_ref[rows, :], vp_ref[rows, :])
            o_ref[rows, :] = o
            l_ref[rows, :] = l
            return carry

        lax.fori_loop(0, d, residue, 0, unroll=min(d, 2))

    shape = jax.ShapeDtypeStruct((T, 4 * HEAD_DIM), F32)
    return pl.pallas_call(
        body, grid=(2, T // (DIL_BLOCK * d)), in_specs=[qs, kc, kp, vc, vp], out_specs=[out, out], out_shape=[shape, shape],
        compiler_params=_cparams(("parallel", "parallel")), name=name)(q, k, k, kv, kv)


def dil_bwd(q, k, kv, do, dl, gi, d, name):
    T = q.shape[0]
    qs, kc, kp, vc, vp, out = _dil_specs(gi, d)

    def body(q_ref, kc_ref, kp_ref, vc_ref, vp_ref, do_ref, dl_ref, *outs):
        f = functools.partial(_f_dilattn, pl.program_id(1) > 0)

        def residue(r, carry):
            rows = _residue_rows(r, d)
            _, vjp = jax.vjp(f, q_ref[rows, :], kc_ref[rows, :], kp_ref[rows, :], vc_ref[rows, :], vp_ref[rows, :])
            for o_ref, g in zip(outs, vjp((do_ref[rows, :], dl_ref[rows, :]))):
                o_ref[rows, :] = g
            return carry

        lax.fori_loop(0, d, residue, 0, unroll=min(d, 2))

    shape = jax.ShapeDtypeStruct((T, 4 * HEAD_DIM), F32)
    dq, dkc, dkp, dvc, dvp = pl.pallas_call(
        body, grid=(2, T // (DIL_BLOCK * d)), in_specs=[qs, kc, kp, vc, vp, out, out], out_specs=[out] * 5, out_shape=[shape] * 5,
        compiler_params=_cparams(("parallel", "parallel")), name=name)(q, k, k, kv, kv, do, dl)

    def own_plus_next(c, p):
        return c + jnp.concatenate([p[DIL_BLOCK * d:], jnp.zeros_like(p[:DIL_BLOCK * d])], axis=0)

    return dq, own_plus_next(dkc, dkp), own_plus_next(dvc, dvp)


CONV_TILE = 128


def _conv3(u, h6, h7, w, b):
    rows = lax.broadcasted_iota(jnp.int32, (u.shape[0], 1), 0)
    s1 = jnp.where(rows == 0, h7, pltpu.roll(u, 1, 0))
    s2 = jnp.where(rows == 0, h6, jnp.where(rows == 1, h7, pltpu.roll(u, 2, 0)))
    return b + w[0:1] * s2 + w[1:2] * s1 + w[2:3] * u, s1, s2


def _conv_halves(u_ref, h_ref, cw_ref, cb_ref):
    F = D_FF
    first = pl.program_id(0) > 0
    res = []
    for lo in (0, F):
        h = h_ref[:, lo:lo + F]
        h6 = jnp.where(first, h[6:7], 0.0)
        h7 = jnp.where(first, h[7:8], 0.0)
        u = u_ref[:, lo:lo + F]
        res.append((u,) + _conv3(u, h6, h7, cw_ref[:, lo:lo + F], cb_ref[:, lo:lo + F]))
    return res


def _halo_before(C):
    return pl.BlockSpec((8, C), lambda i: (jnp.maximum(i * (CONV_TILE // 8) - 1, 0), 0))


def convgate_fwd(u, cw, cb, name):
    T, C = u.shape
    F = C // 2

    def body(u_ref, h_ref, cw_ref, cb_ref, z_ref):
        (_, cg, _, _), (_, cv, _, _) = _conv_halves(u_ref, h_ref, cw_ref, cb_ref)
        z_ref[...] = (cg * _sigmoid(cg) * cv).astype(BF16)

    return pl.pallas_call(
        body, grid=(T // CONV_TILE,),
        in_specs=[pl.BlockSpec((CONV_TILE, C), lambda i: (i, 0)), _halo_before(C), _full_spec(cw), _full_spec(cb)],
        out_specs=pl.BlockSpec((CONV_TILE, F), lambda i: (i, 0)), out_shape=jax.ShapeDtypeStruct((T, F), BF16),
        compiler_params=_cparams(("parallel",)), name=name)(u, u, cw, cb)


def convgate_bwd(u, cw, cb, dz, name):
    T, C = u.shape
    F = C // 2
    n = T // CONV_TILE
    E = CONV_TILE + 8

    def body(u_ref, hb_ref, ha_ref, cw_ref, cb_ref, dz_ref, dza_ref, du_ref, dcw_ref, dcb_ref):
        i = pl.program_id(0)
        dze = jnp.concatenate([dz_ref[...], jnp.where(i < n - 1, dza_ref[...], 0.0)], axis=0)

        @pl.when(i == 0)
        def _():
            dcw_ref[...] = jnp.zeros_like(dcw_ref)
            dcb_ref[...] = jnp.zeros_like(dcb_ref)

        halves = []
        for lo in (0, F):
            sl = slice(lo, lo + F)
            hb = hb_ref[:, sl]
            ue = jnp.concatenate([u_ref[:, sl], ha_ref[:, sl]], axis=0)
            c, s1, s2 = _conv3(ue, jnp.where(i > 0, hb[6:7], 0.0), jnp.where(i > 0, hb[7:8], 0.0), cw_ref[:, sl], cb_ref[:, sl])
            halves.append((sl, ue, c, s1, s2))
        (_, _, cg, _, _), (_, _, cv, _, _) = halves
        sg = _sigmoid(cg)
        dcs = (dze * cv * sg * (1.0 + cg * (1.0 - sg)), dze * cg * sg)
        for (sl, ue, _, s1, s2), dc in zip(halves, dcs):
            own = lambda z: z[:CONV_TILE]
            dcb_ref[:, sl] += jnp.sum(own(dc), axis=0, keepdims=True)
            dcw_ref[0:1, sl] += jnp.sum(own(dc * s2), axis=0, keepdims=True)
            dcw_ref[1:2, sl] += jnp.sum(own(dc * s1), axis=0, keepdims=True)
            dcw_ref[2:3, sl] += jnp.sum(own(dc * ue), axis=0, keepdims=True)
            du = cw_ref[2:3, sl] * dc + cw_ref[1:2, sl] * pltpu.roll(dc, E - 1, 0) + cw_ref[0:1, sl] * pltpu.roll(dc, E - 2, 0)
            du_ref[:, sl] = own(du).astype(BF16)

    after = lambda w: pl.BlockSpec((8, w), lambda i: (jnp.minimum((i + 1) * (CONV_TILE // 8), T // 8 - 1), 0))
    return pl.pallas_call(
        body, grid=(n,),
        in_specs=[pl.BlockSpec((CONV_TILE, C), lambda i: (i, 0)), _halo_before(C), after(C), _full_spec(cw), _full_spec(cb),
                  pl.BlockSpec((CONV_TILE, F), lambda i: (i, 0)), after(F)],
        out_specs=[pl.BlockSpec((CONV_TILE, C), lambda i: (i, 0)), _full_spec(cw), _full_spec(cb)],
        out_shape=[jax.ShapeDtypeStruct((T, C), BF16), jax.ShapeDtypeStruct(cw.shape, F32), jax.ShapeDtypeStruct(cb.shape, F32)],
        compiler_params=_cparams(("arbitrary",)), name=name)(u, u, u, cw, cb, dz, dz)


def loss_head(y, tgt):
    T, D = y.shape
    tile = 256

    def body(y_ref, t_ref, l_ref, d_ref, db_ref):
        d = y_ref[...] - t_ref[...]
        d_ref[...] = d * (1.0 / D)
        db_ref[...] = (d * (1.0 / D)).astype(BF16)

        @pl.when(pl.program_id(0) == 0)
        def _():
            l_ref[...] = jnp.zeros_like(l_ref)

        l_ref[...] += (0.5 / D) * jnp.sum(d * d)

    row = pl.BlockSpec((tile, D), lambda i: (i, 0))
    return pl.pallas_call(
        body, grid=(T // tile,), in_specs=[row, row], out_specs=[pl.BlockSpec((8, 128), lambda i: (0, 0)), row, row],
        out_shape=[jax.ShapeDtypeStruct((8, 128), F32), jax.ShapeDtypeStruct((T, D), F32), jax.ShapeDtypeStruct((T, D), BF16)],
        compiler_params=_cparams(("arbitrary",)), name="loss_head")(y, tgt)


def sum_parts(parts, name):
    S, R, C = parts.shape
    tile = _pick(R, (256, 128, 64, 32, 16, 8))

    def body(p_ref, o_ref):
        acc = p_ref[0]
        for s in range(1, S):
            acc = acc + p_ref[s]
        o_ref[...] = acc

    return pl.pallas_call(
        body, grid=(R // tile,), in_specs=[pl.BlockSpec((S, tile, C), lambda i: (0, i, 0))],
        out_specs=pl.BlockSpec((tile, C), lambda i: (i, 0)), out_shape=jax.ShapeDtypeStruct((R, C), F32),
        compiler_params=_cparams(("parallel",)), name=name)(parts)


def adamw(gparts, w, m, v, name):
    S, R, C = gparts.shape
    tile = _pick(R, (256, 128, 64, 32, 16, 8))
    c1 = 1.0 / (1.0 - ADAM_B1 ** ADAM_STEP)
    c2 = 1.0 / (1.0 - ADAM_B2 ** ADAM_STEP)

    def body(g_ref, w_ref, m_ref, v_ref, go_ref, d_ref, mo_ref, vo_ref):
        g = g_ref[0].astype(F32)
        for s in range(1, S):
            g = g + g_ref[s].astype(F32)
        m1 = ADAM_B1 * m_ref[...] + (1.0 - ADAM_B1) * g
        v1 = ADAM_B2 * v_ref[...] + (1.0 - ADAM_B2) * (g * g)
        go_ref[...] = g
        mo_ref[...] = m1
        vo_ref[...] = v1
        d_ref[...] = -ADAM_LR * ((m1 * c1) / (jnp.sqrt(v1 * c2) + ADAM_EPS) + ADAM_WD * w_ref[...])

    row = pl.BlockSpec((tile, C), lambda i: (i, 0))
    return pl.pallas_call(
        body, grid=(R // tile,), in_specs=[pl.BlockSpec((S, tile, C), lambda i: (0, i, 0)), row, row, row],
        out_specs=[row] * 4, out_shape=[jax.ShapeDtypeStruct((R, C), F32)] * 4,
        compiler_params=_cparams(("parallel",)), name=name)(gparts, w, m, v)


def _peers():
    x, y, c = lax.axis_index("x"), lax.axis_index("y"), lax.axis_index("c")
    peers = []
    for k in range(1, N_DEV):
        px = 1 - x if k & 4 else x
        py = 1 - y if k & 2 else y
        pc = 1 - c if k & 1 else c
        peers.append(((px, py, pc), 4 * px + 2 * py + pc))
    return 4 * x + 2 * y + c, peers


_ANY = pl.BlockSpec(memory_space=pl.ANY)


class Exchange:
    def __init__(self, gathers=(), scatters=()):
        self.gathers, self.scatters = list(gathers), list(scatters)
        self.n = len(self.gathers) + len(self.scatters)

    def operands(self):
        return self.gathers + self.scatters

    def out_shape(self):
        return ([jax.ShapeDtypeStruct((N_DEV,) + x.shape, x.dtype) for x in self.gathers]
                + [jax.ShapeDtypeStruct(x.shape, x.dtype) for x in self.scatters])

    def scratch(self):
        n = max(self.n, 1)
        return [pltpu.SemaphoreType.DMA((7 * n,)), pltpu.SemaphoreType.DMA((7 * n,)), pltpu.SemaphoreType.DMA((n,))]

    def _copies(self, in_refs, out_refs, send_sems, recv_sems, local_sems):
        me, peers = _peers()
        ng = len(self.gathers)
        local, sends, recvs = [], [], []
        for a in range(self.n):
            x, o = in_refs[a], out_refs[a]
            mine = x if a < ng else x.at[me]
            local.append(pltpu.make_async_copy(mine, o.at[me], local_sems.at[a]))
            s_a, r_a = {}, {}
            for k in range(1, N_DEV):
                peer, slot = peers[k - 1]
                sems = dict(send_sem=send_sems.at[7 * a + k - 1], recv_sem=recv_sems.at[7 * a + k - 1],
                            device_id_type=pl.DeviceIdType.MESH)
                if a >= ng:
                    s_a[k] = pltpu.make_async_remote_copy(src_ref=x.at[slot], dst_ref=o.at[me], device_id=peer, **sems)
                elif k in FORWARDED:
                    came = o.at[peers[k - 2][1]]
                    s_a[k] = pltpu.make_async_remote_copy(src_ref=came, dst_ref=came, device_id=peers[0][0], **sems)
                else:
                    s_a[k] = pltpu.make_async_remote_copy(src_ref=x, dst_ref=o.at[me], device_id=peer, **sems)
                r_a[k] = pltpu.make_async_remote_copy(src_ref=mine, dst_ref=o.at[slot], device_id=peer, **sems)
            sends.append(s_a)
            recvs.append(r_a)
        return local, sends, recvs

    def start(self, *refs):
        if self.n == 0:
            return
        local, sends, _ = self._copies(*refs)
        for a in range(self.n):
            local[a].start()
            for k in range(1, N_DEV):
                if a >= len(self.gathers) or k not in FORWARDED:
                    sends[a][k].start()

    def forward(self, *refs):
        if not self.gathers:
            return
        _, sends, recvs = self._copies(*refs)
        for a in range(len(self.gathers)):
            for k in FORWARDED:
                recvs[a][k - 1].wait_recv()
                sends[a][k].start()

    def wait(self, *refs):
        if self.n == 0:
            return
        local, sends, recvs = self._copies(*refs)
        for a in range(self.n):
            waited_early = [f - 1 for f in FORWARDED] if a < len(self.gathers) else []
            for k in range(1, N_DEV):
                if k not in waited_early:
                    recvs[a][k].wait_recv()
            for k in range(1, N_DEV):
                sends[a][k].wait_send()
            local[a].wait()


FORWARDED = (3, 5, 7)


def exchange(ex, name):
    n = ex.n

    def body(*refs):
        args = (refs[:n], refs[n:2 * n]) + tuple(refs[2 * n:])
        ex.start(*args)
        ex.forward(*args)
        ex.wait(*args)

    return pl.pallas_call(body, in_specs=[_ANY] * n, out_specs=[_ANY] * n, out_shape=ex.out_shape(),
                          scratch_shapes=ex.scratch(), name=name)(*ex.operands())


def _heads(z, h):
    return z.reshape(z.shape[0], h, HEAD_DIM).transpose(1, 0, 2)


def _unheads(z):
    return z.transpose(1, 0, 2).reshape(z.shape[1], z.shape[0] * HEAD_DIM)


def _shift_up(z):
    return jnp.concatenate([z[1:], jnp.zeros_like(z[:1])], axis=0)


def _segments(width):
    seg = np.zeros((width, 128), np.float32)
    seg[np.arange(width), np.arange(width) // HEAD_DIM] = 1.0
    return jnp.asarray(seg), jnp.asarray(seg.T)


def _rope_consts(T, heads):
    inv = ROPE_THETA ** (-jnp.arange(0, HEAD_DIM, 2, dtype=F32) / HEAD_DIM)
    ang = jnp.arange(T, dtype=F32)[:, None] * inv[None, :]
    return jnp.tile(jnp.cos(ang), (1, 2 * heads)), jnp.tile(jnp.sin(ang), (1, 2 * heads))


def _per_head(g, heads):
    return jnp.tile(g.reshape(1, HEAD_DIM), (1, heads))


def _sum_heads(g):
    return g.reshape(-1, HEAD_DIM).sum(axis=0, keepdims=True)


LORA_COLS = 256
RW_TILE = 128
ROW_TILE = 256


def _local_step(x0, memx, tgt, P, ex_weights=None, weights_done=None, ex_grads=None):
    T = x0.shape[0]
    P = dict(P)
    G = {}
    seg, seg_t = _segments(RWKV_WIDTH)
    mseg = (seg[:MEM_WIDTH], seg_t[:, :MEM_WIDTH])
    cos, sin = _rope_consts(T, DIL_WIDTH // HEAD_DIM)
    row = lambda v: v.reshape(1, -1)

    def mem_fwd(i, q):
        memn = stage_fwd(f_rmsnorm, [memx], [P["mem_norm"][i:i + 1]], [], [], N_MEM, f"mem{i}_norm", [BF16])[0]
        kvm = matmul(memn, P["mem_w_kv"][i], "nn", f"mem{i}_kv")
        kn, qn = _per_head(P["mem_k_norm"][i], MEM_HEADS), _per_head(P["mem_q_norm"][i], MEM_HEADS)
        km = stage_fwd(f_headnorm, [Cols(kvm, MEM_WIDTH, 0)], [kn], [], mseg, N_MEM, f"mem{i}_knorm")[0]
        om = stage_fwd(f_memattn, [q], [km, Cols(kvm, MEM_WIDTH, 1), qn], [], mseg, ROW_TILE, f"mem{i}_attn")[0]
        return om, (memn, kvm, km, kn, qn, q)

    def mem_bwd(i, saved, dymem):
        memn, kvm, km, kn, qn, q = saved
        (dq,), (dkm, dvm, g_qn) = stage_bwd(f_memattn, [q], [km, Cols(kvm, MEM_WIDTH, 1), qn], [], mseg, [dymem], ROW_TILE,
                                            f"mem{i}_attn_bwd")
        (dkraw,), (g_kn,) = stage_bwd(f_headnorm, [Cols(kvm, MEM_WIDTH, 0)], [kn], [], mseg, [dkm], N_MEM, f"mem{i}_knorm_bwd")
        dkvm = jnp.concatenate([dkraw, dvm], axis=1).astype(BF16)
        g_w = matmul(memn, dkvm, "tn", f"mem{i}_kv_dw")
        dmemn = matmul(dkvm, P["mem_w_kv"][i], "nt", f"mem{i}_kv_dx")
        _, (g_mn,) = stage_bwd(f_rmsnorm, [memx], [P["mem_norm"][i:i + 1]], [], [], [dmemn], N_MEM, f"mem{i}_norm_bwd")
        return dq, g_mn, g_w, _sum_heads(g_qn), _sum_heads(g_kn)

    def ffn_fwd(i, xin):
        hn = stage_fwd(f_rmsnorm, [xin], [P["ffn_norm"][i:i + 1]], [], [], ROW_TILE, f"ffn{i}_norm", [BF16])[0]
        u = matmul(hn, P["ffn_w_up"][i], "nn", f"ffn{i}_up")
        z = convgate_fwd(u, P["ffn_conv_w"][i], P["ffn_conv_b"][i:i + 1], f"ffn{i}_conv")
        return matmul(z, P["ffn_w_down"][i], "nn", f"ffn{i}_down", residual=xin), (hn, u, z)

    def ffn_bwd(i, xin, saved, dxo, dxo_b):
        hn, u, z = saved
        dz = matmul(dxo_b, P["ffn_w_down"][i], "nt", f"ffn{i}_down_dx")
        g_down = matmul(z, dxo_b, "tn", f"ffn{i}_down_dw")
        du, g_cw, g_cb = convgate_bwd(u, P["ffn_conv_w"][i], P["ffn_conv_b"][i:i + 1], dz, f"ffn{i}_conv_bwd")
        dhn = matmul(du, P["ffn_w_up"][i], "nt", f"ffn{i}_up_dx")
        g_up = matmul(hn, du, "tn", f"ffn{i}_up_dw")
        (dxin,), (g_n,), (dxin_b,) = stage_bwd(f_rmsnorm_res, [xin], [P["ffn_norm"][i:i + 1]], [], [], [dhn, dxo], ROW_TILE,
                                               f"ffn{i}_norm_bwd", bf16_copies=(0,))
        return dxin, dxin_b, g_n, g_up, g_cw, g_cb, g_down

    h0 = stage_fwd(f_rmsnorm, [x0], [P["attn_norm"][0:1]], [], [], ROW_TILE, "l0_norm", [BF16])[0]
    p0 = matmul(h0, P["a_w_in"][0], "nn", "l0_in")
    lora0 = 3 * RWKV_WIDTH // LORA_COLS
    pre_xs = [Cols(p0, RWKV_WIDTH, 0), Cols(p0, RWKV_WIDTH, 1), Cols(p0, RWKV_WIDTH, 2), Cols(p0, LORA_COLS, lora0)]
    mu = [Cols(P["a_mu"], RWKV_WIDTH, 0), Cols(P["a_mu"], RWKV_WIDTH, 1), Cols(P["a_mu"], RWKV_WIDTH, 2),
          Cols(P["a_mu"], LORA_COLS, lora0)]
    lora_rows = lambda w, lo: jnp.pad(w, ((lo, LORA_COLS - lo - w.shape[0]), (0, 0)))
    pre_ps = mu + [P["a_w0"], lora_rows(P["a_w2"][0], 0), P["a_a0"], lora_rows(P["a_a2"][0], 64), lora_rows(P["a_g2"][0], 128),
                   P["a_k_k"], P["a_k_a"]]
    r, lw, k2, v, kk, b, g = stage_fwd(f_rwkv_pre, pre_xs, pre_ps, [], [seg, seg_t], RW_TILE, "l0_rwkv_pre", with_prev=True)
    scan_in = [r, lw, k2, v, kk, b]
    y_h, h_states, got = rwkv_scan_fwd(*scan_in, ex_weights or Exchange())
    if weights_done is not None:
        P.update(weights_done(got))
    y_s = y_h
    post_ps = [P["a_lnx_w"], P["a_lnx_b"], P["a_r_k"].reshape(1, RWKV_WIDTH)]
    ymix0 = stage_fwd(f_rwkv_post, [y_s, r, k2, v, g], post_ps, [], [seg, seg_t], RW_TILE, "l0_rwkv_post")[0]
    ymem0, mem0_saved = mem_fwd(0, Cols(p0, MEM_WIDTH, SHIFT_WIDTH // MEM_WIDTH))
    ycat0 = jnp.concatenate([ymix0, ymem0], axis=1).astype(BF16)
    x1 = matmul(ycat0, P["a_w_out"][0], "nn", "l0_out", residual=x0)
    x2, ffn0_saved = ffn_fwd(0, x1)

    hk, h1 = stage_fwd(f_rmsnorm2, [x2], [row(P["kv_norm"]), P["attn_norm"][1:2]], [], [], ROW_TILE, "l1_norm", [BF16, BF16])
    kvp = matmul(hk, P["kv_w"][0], "nn", "l1_kv")
    p1 = matmul(h1, P["b_w_in"][0], "nn", "l1_in")
    kraw, qraw = Cols(kvp, DIL_WIDTH, 0), Cols(p1, DIL_WIDTH, 0)
    kgain, qgain = _per_head(P["kv_k_norm"], DIL_WIDTH // HEAD_DIM), _per_head(P["b_q_norm"], DIL_WIDTH // HEAD_DIM)
    ksh = stage_fwd(f_qkprep, [kraw], [kgain], [cos, sin], [seg, seg_t], ROW_TILE, "l1_kprep")[0]
    q = stage_fwd(f_qkprep, [qraw], [qgain], [cos, sin], [seg, seg_t], ROW_TILE, "l1_qprep")[0]
    outs, lses = [], []
    for gi, (_, d) in enumerate(DIL_GROUPS):
        og, lg = dil_fwd(q, ksh, kvp, gi, d, f"l1_dil{gi}")
        outs.append(og)
        lses.append(lg)
    omix = stage_fwd(f_mix, outs + lses, [], [], [], ROW_TILE, "l1_mix")[0]
    ymem1, mem1_saved = mem_fwd(1, Cols(p1, MEM_WIDTH, DIL_WIDTH // MEM_WIDTH))
    ycat1 = jnp.concatenate([omix, ymem1], axis=1).astype(BF16)
    x3 = matmul(ycat1, P["b_w_out"][0], "nn", "l1_out", residual=x2)
    x4, ffn1_saved = ffn_fwd(1, x3)
    loss_part, dx4, dx4_b = loss_head(x4, tgt)

    dx3, dx3_b, gn1, gup1, gcw1, gcb1, gdown1 = ffn_bwd(1, x3, ffn1_saved, dx4, dx4_b)
    dycat1 = matmul(dx3_b, P["b_w_out"][0], "nt", "l1_out_dx")
    G["b_w_out"] = [matmul(ycat1, dx3_b, "tn", "l1_out_dw")]
    dqmem1, gmn1, gmw1, gmq1, gmk1 = mem_bwd(1, mem1_saved, Cols(dycat1, MEM_WIDTH, 1))
    dmix, _ = stage_bwd(f_mix, outs + lses, [], [], [], [Cols(dycat1, MEM_WIDTH, 0)], ROW_TILE, "l1_mix_bwd")
    dq, dk, dv = zip(*[dil_bwd(q, ksh, kvp, dmix[gi], dmix[3 + gi], gi, d, f"l1_dil{gi}_bwd")
                       for gi, (_, d) in enumerate(DIL_GROUPS)])
    dq, dk, dv = jnp.concatenate(dq, axis=1), jnp.concatenate(dk, axis=1), jnp.concatenate(dv, axis=1)
    (dqraw,), (g_bq,) = stage_bwd(f_qkprep, [qraw], [qgain], [cos, sin], [seg, seg_t], [dq], ROW_TILE, "l1_qprep_bwd")
    (dkraw,), (g_kk,) = stage_bwd(f_qkprep, [kraw], [kgain], [cos, sin], [seg, seg_t], [dk], ROW_TILE, "l1_kprep_bwd")
    g_bq, g_kk = _sum_heads(g_bq), _sum_heads(g_kk)
    dp1 = jnp.concatenate([dqraw, dqmem1], axis=1).astype(BF16)
    dkvp = jnp.concatenate([dkraw, dv], axis=1).astype(BF16)
    dh1 = matmul(dp1, P["b_w_in"][0], "nt", "l1_in_dx")
    G["b_w_in"] = [matmul(h1, dp1, "tn", "l1_in_dw")]
    dhk = matmul(dkvp, P["kv_w"][0], "nt", "l1_kv_dx")
    G["kv_w"] = [matmul(hk, dkvp, "tn", "l1_kv_dw")]
    (dx2,), (g_kvn, g_an1), (dx2_b,) = stage_bwd(f_rmsnorm2_res, [x2], [row(P["kv_norm"]), P["attn_norm"][1:2]], [], [],
                                                 [dhk, dh1, dx3], ROW_TILE, "l1_norm_bwd", bf16_copies=(0,))

    dx1, dx1_b, gn0, gup0, gcw0, gcb0, gdown0 = ffn_bwd(0, x1, ffn0_saved, dx2, dx2_b)
    dycat0 = matmul(dx1_b, P["a_w_out"][0], "nt", "l0_out_dx")
    G["a_w_out"] = [matmul(ycat0, dx1_b, "tn", "l0_out_dw")]
    dqmem0, gmn0, gmw0, gmq0, gmk0 = mem_bwd(0, mem0_saved, Cols(dycat0, MEM_WIDTH, RWKV_WIDTH // MEM_WIDTH))
    (dy_s, dr_a, dk_a, dv_a, dg), (g_lw, g_lb, g_rk) = stage_bwd(
        f_rwkv_post, [y_s, r, k2, v, g], post_ps, [], [seg, seg_t], [Cols(dycat0, RWKV_WIDTH, 0)], RW_TILE, "l0_rwkv_post_bwd")
    G["mem_w_kv"], G["ffn_w_up"], G["ffn_w_down"] = [gmw0, gmw1], [gup0, gup1], [gdown0, gdown1]
    (dr_b, dlw, dk_b, dv_b, dkk, db), G["_exchanged"] = rwkv_scan_bwd(*scan_in, h_states, dy_s,
                                                                      ex_grads(G) if ex_grads else Exchange())
    dpre, gpre = stage_bwd(f_rwkv_pre, pre_xs, pre_ps, [], [seg, seg_t],
                           [[dr_a, dr_b], dlw, [dk_a, dk_b], [dv_a, dv_b], dkk, db, dg], RW_TILE, "l0_rwkv_pre_bwd", with_prev=True)
    dp_rw = jnp.concatenate(dpre[:4], axis=1) + _shift_up(jnp.concatenate(dpre[4:], axis=1))
    dp0 = jnp.concatenate([dp_rw, dqmem0], axis=1).astype(BF16)
    dh0 = matmul(dp0, P["a_w_in"][0], "nt", "l0_in_dx")
    G["a_w_in"] = [matmul(h0, dp0, "tn", "l0_in_dw")]
    (dx0,), (g_an0,) = stage_bwd(f_rmsnorm_res, [x0], [P["attn_norm"][0:1]], [], [], [dh0, dx1], ROW_TILE, "l0_norm_bwd")

    G["attn_norm"] = jnp.concatenate([g_an0, g_an1], axis=0)
    G["a_mu"] = jnp.concatenate(gpre[:4], axis=1)
    G["a_w0"], G["a_w2"], G["a_a0"], G["a_a2"], G["a_g2"] = gpre[4], gpre[5][None, :64], gpre[6], gpre[7][None, 64:128], gpre[8][None, 128:]
    G["a_k_k"], G["a_k_a"] = gpre[9], gpre[10]
    G["a_r_k"] = g_rk.reshape(1, RWKV_HEADS, HEAD_DIM)
    G["a_lnx_w"], G["a_lnx_b"] = g_lw, g_lb
    G["kv_norm"], G["kv_k_norm"], G["b_q_norm"] = g_kvn.reshape(-1), g_kk.reshape(-1), g_bq
    G["mem_norm"] = jnp.concatenate([gmn0, gmn1], axis=0)
    G["mem_w_kv"] = [gmw0, gmw1]
    G["mem_q_norm"] = jnp.concatenate([gmq0, gmq1], axis=0)
    G["mem_k_norm"] = jnp.concatenate([gmk0, gmk1], axis=0)
    G["ffn_norm"] = jnp.concatenate([gn0, gn1], axis=0)
    G["ffn_w_up"] = [gup0, gup1]
    G["ffn_conv_w"] = jnp.stack([gcw0, gcw1])
    G["ffn_conv_b"] = jnp.concatenate([gcb0, gcb1], axis=0)
    G["ffn_w_down"] = [gdown0, gdown1]
    return loss_part, dx0, G


PARAMS = (("attn_norm", None), ("a_w_in", 2), ("a_mu", 1), ("a_w0", 1), ("a_w2", 2), ("a_a0", 1), ("a_a2", 2), ("a_g2", 2),
          ("a_k_k", 1), ("a_k_a", 1), ("a_r_k", None), ("a_lnx_w", 1), ("a_lnx_b", 1), ("a_w_out", 1), ("kv_norm", None),
          ("kv_w", 1), ("kv_k_norm", None), ("b_w_in", 1), ("b_q_norm", None), ("b_w_out", 2), ("mem_norm", None),
          ("mem_w_kv", 1), ("mem_q_norm", None), ("mem_k_norm", None), ("ffn_norm", None), ("ffn_w_up", 2),
          ("ffn_conv_w", 2), ("ffn_conv_b", None), ("ffn_w_down", 1))
BIG = ("a_w_in", "a_w_out", "kv_w", "b_w_in", "b_w_out", "mem_w_kv", "ffn_w_up", "ffn_w_down")
AXIS = dict(PARAMS)
SMALL = tuple(n for n, _ in PARAMS if n not in BIG)
SMALL_SHARDED = tuple(n for n in SMALL if AXIS[n] is not None)
PACK_QUANTUM = 256 * 128


def _from_shards(xs, axis):
    full = jnp.moveaxis(xs, 0, axis)
    sh = full.shape
    return full.reshape(sh[:axis] + (sh[axis] * sh[axis + 1],) + sh[axis + 2:])


def _to_shards(g, axis):
    sh = g.shape
    return jnp.moveaxis(g.reshape(sh[:axis] + (N_DEV, sh[axis] // N_DEV) + sh[axis + 1:]), axis, 0)


def _pack(parts, lead=0):
    ld = parts[0].shape[:lead]
    flat = jnp.concatenate([p.reshape(ld + (-1,)) for p in parts], axis=-1)
    pad = (-flat.shape[-1]) % PACK_QUANTUM
    flat = jnp.pad(flat, [(0, 0)] * lead + [(0, pad)])
    return flat.reshape(ld + (-1, 128))


def _unpack(packed, shapes, lead=0):
    ld = packed.shape[:lead]
    flat = packed.reshape(ld + (-1,))
    out, off = [], 0
    for s in shapes:
        n = math.prod(s)
        out.append(flat[..., off:off + n].reshape(ld + tuple(s)))
        off += n
    return out


def kernel(x, mem, attn_norm, a_w_in, a_mu, a_w0, a_w2, a_a0, a_a2, a_g2, a_k_k, a_k_a, a_r_k, a_lnx_w, a_lnx_b, a_w_out, kv_norm, kv_w, kv_k_norm, b_w_in, b_q_norm, b_w_out, mem_norm, mem_w_kv, mem_q_norm, mem_k_norm, ffn_norm, ffn_w_up, ffn_conv_w, ffn_conv_b, ffn_w_down, loss_target, m_attn_norm, m_a_w_in, m_a_mu, m_a_w0, m_a_w2, m_a_a0, m_a_a2, m_a_g2, m_a_k_k, m_a_k_a, m_a_r_k, m_a_lnx_w, m_a_lnx_b, m_a_w_out, m_kv_norm, m_kv_w, m_kv_k_norm, m_b_w_in, m_b_q_norm, m_b_w_out, m_mem_norm, m_mem_w_kv, m_mem_q_norm, m_mem_k_norm, m_ffn_norm, m_ffn_w_up, m_ffn_conv_w, m_ffn_conv_b, m_ffn_w_down, v_attn_norm, v_a_w_in, v_a_mu, v_a_w0, v_a_w2, v_a_a0, v_a_a2, v_a_g2, v_a_k_k, v_a_k_a, v_a_r_k, v_a_lnx_w, v_a_lnx_b, v_a_w_out, v_kv_norm, v_kv_w, v_kv_k_norm, v_b_w_in, v_b_q_norm, v_b_w_out, v_mem_norm, v_mem_w_kv, v_mem_q_norm, v_mem_k_norm, v_ffn_norm, v_ffn_w_up, v_ffn_conv_w, v_ffn_conv_b, v_ffn_w_down):
    names = [n for n, _ in PARAMS]
    vals = (attn_norm, a_w_in, a_mu, a_w0, a_w2, a_a0, a_a2, a_g2, a_k_k, a_k_a, a_r_k, a_lnx_w, a_lnx_b, a_w_out, kv_norm, kv_w, kv_k_norm, b_w_in, b_q_norm, b_w_out, mem_norm, mem_w_kv, mem_q_norm, mem_k_norm, ffn_norm, ffn_w_up, ffn_conv_w, ffn_conv_b, ffn_w_down)
    m_vals = (m_attn_norm, m_a_w_in, m_a_mu, m_a_w0, m_a_w2, m_a_a0, m_a_a2, m_a_g2, m_a_k_k, m_a_k_a, m_a_r_k, m_a_lnx_w, m_a_lnx_b, m_a_w_out, m_kv_norm, m_kv_w, m_kv_k_norm, m_b_w_in, m_b_q_norm, m_b_w_out, m_mem_norm, m_mem_w_kv, m_mem_q_norm, m_mem_k_norm, m_ffn_norm, m_ffn_w_up, m_ffn_conv_w, m_ffn_conv_b, m_ffn_w_down)
    v_vals = (v_attn_norm, v_a_w_in, v_a_mu, v_a_w0, v_a_w2, v_a_a0, v_a_a2, v_a_g2, v_a_k_k, v_a_k_a, v_a_r_k, v_a_lnx_w, v_a_lnx_b, v_a_w_out, v_kv_norm, v_kv_w, v_kv_k_norm, v_b_w_in, v_b_q_norm, v_b_w_out, v_mem_norm, v_mem_w_kv, v_mem_q_norm, v_mem_k_norm, v_ffn_norm, v_ffn_w_up, v_ffn_conv_w, v_ffn_conv_b, v_ffn_w_down)
    W, M, V = dict(zip(names, vals)), dict(zip(names, m_vals)), dict(zip(names, v_vals))
    me = 4 * lax.axis_index("x") + 2 * lax.axis_index("y") + lax.axis_index("c")
    layers = lambda D, n: [D[n]] if D[n].ndim == 2 else [D[n][i] for i in range(D[n].shape[0])]
    ax2 = lambda n: AXIS[n] - (W[n].ndim - 2)
    first = [("a_w_in", 0)]
    later = [(n, i) for n in BIG if n != "a_w_in" for i in range(len(layers(W, n)))]

    small_shapes = [W[n].shape for n in SMALL_SHARDED]
    got_w, got_small = exchange(Exchange(gathers=[W["a_w_in"][0].astype(BF16), _pack([W[n] for n in SMALL_SHARDED])]),
                                "gather_first")
    P = {n: W[n] for n in SMALL}
    P["a_w_in"] = [_from_shards(got_w, ax2("a_w_in"))]
    for n, s in zip(SMALL_SHARDED, _unpack(got_small, small_shapes, lead=1)):
        P[n] = _from_shards(s, AXIS[n])
    ex_weights = Exchange(gathers=[layers(W, n)[i].astype(BF16) for n, i in later])

    def weights_done(got):
        out = {}
        for (n, _), g in zip(later, got):
            out.setdefault(n, []).append(_from_shards(g, ax2(n)))
        return out

    slots = lambda G, n: jnp.stack([_to_shards(g, ax2(n)) for g in G[n]], axis=1)
    later_names = [n for n in BIG if n != "a_w_in"]
    ex_grads = lambda G: Exchange(scatters=[slots(G, n) for n in later_names])
    loss_part, dx0, G = _local_step(x[0], mem[0], loss_target[0], P, ex_weights, weights_done, ex_grads)
    loss = lax.psum(loss_part[0, 0], ("x", "y", "c"))
    gparts = dict(zip(later_names, G.pop("_exchanged")))
    got_gsmall, gparts["a_w_in"] = exchange(
        Exchange(gathers=[_pack([G[n] for n in SMALL])], scatters=[slots(G, "a_w_in")]), "exchange_last")

    results = {}
    for n in BIG:
        rows = lambda z: z.reshape((-1,) + z.shape[-1:])
        res = adamw(gparts[n].reshape((N_DEV, -1) + gparts[n].shape[-1:]), rows(W[n]), rows(M[n]), rows(V[n]), f"adamw_{n}")
        results[n] = [r.reshape(W[n].shape) for r in res]
    g_small = sum_parts(got_gsmall, "sum_small_grads")
    mine = []
    for n, g in zip(SMALL, _unpack(g_small, [G[n].shape for n in SMALL])):
        if AXIS[n] is not None:
            s = W[n].shape[AXIS[n]]
            g = lax.dynamic_slice_in_dim(g, me * s, s, axis=AXIS[n])
        mine.append(g)
    res = adamw(_pack(mine)[None], _pack([W[n] for n in SMALL]), _pack([M[n] for n in SMALL]), _pack([V[n] for n in SMALL]),
                "adamw_small")
    for n, parts in zip(SMALL, zip(*[_unpack(r, [W[n].shape for n in SMALL]) for r in res])):
        results[n] = list(parts)
    outs = [[results[n][j] for n in names] for j in range(4)]
    return (loss, dx0[None], *outs[0], *outs[1], *outs[2], *outs[3])
```

```python
import functools
import math

import jax
import jax.numpy as jnp
import numpy as np
from jax import lax
from jax.experimental import pallas as pl
from jax.experimental.pallas import tpu as pltpu

F32 = jnp.float32
BF16 = jnp.bfloat16
HI = lax.Precision.HIGHEST
H3 = lax.Precision.HIGH

N_DEV = 8
D_MODEL = 1024
HEAD_DIM = 64
N_MEM = 256
MEM_HEADS = 4
MEM_WIDTH = 256
RWKV_HEADS = 12
RWKV_WIDTH = 768
SHIFT_WIDTH = 2560
DIL_GROUPS = ((128, 1), (512, 4), (2048, 16))
DIL_BLOCK = 128
DIL_WIDTH = 768
D_FF = 2816
RMS_EPS = 1e-6
LNX_EPS = 64e-5
NEG_INF = -1e30
ROPE_THETA = 10000.0
ADAM_LR, ADAM_B1, ADAM_B2, ADAM_EPS, ADAM_WD, ADAM_STEP = 0.001, 0.9, 0.999, 1e-08, 0.01, 10

CHUNK = 64
SCAN_GROUPS_FWD, SCAN_GROUPS_BWD = 1, 1
MM_TILE_CAP = 1408
VMEM_LIMIT_V7X = 48 * 1024 * 1024


def _cparams(sem):
    return pltpu.CompilerParams(dimension_semantics=sem, vmem_limit_bytes=VMEM_LIMIT_V7X)


def _pick(n, cands):
    for c in cands:
        if n % c == 0:
            return c
    return n


def _tile(n, cap):
    if n <= cap:
        return n
    for d in range(cap - cap % 128, 0, -128):
        if n % d == 0:
            return d
    return n


def _dg(a, b, ca, cb, batch):
    dims = (((ca,), (cb,)), ((0,), (0,))) if batch else (((ca,), (cb,)), ((), ()))
    return lax.dot_general(a.astype(BF16), b.astype(BF16), dims, preferred_element_type=F32)


@jax.custom_vjp
def mm_nn(a, b):
    n = a.ndim
    return _dg(a, b, n - 1, n - 2, n == 3)


def _mm_nn_fwd(a, b):
    return mm_nn(a, b), (a, b)


def _mm_nn_bwd(res, g):
    a, b = res
    n = a.ndim
    return _dg(g, b, n - 1, n - 1, n == 3), _dg(a, g, n - 2, n - 2, n == 3)


mm_nn.defvjp(_mm_nn_fwd, _mm_nn_bwd)


@jax.custom_vjp
def mm_nt(a, b):
    n = a.ndim
    return _dg(a, b, n - 1, n - 1, n == 3)


def _mm_nt_fwd(a, b):
    return mm_nt(a, b), (a, b)


def _mm_nt_bwd(res, g):
    a, b = res
    n = a.ndim
    return _dg(g, b, n - 1, n - 2, n == 3), _dg(g, a, n - 2, n - 2, n == 3)


mm_nt.defvjp(_mm_nt_fwd, _mm_nt_bwd)


def mmh(a, b, precision=H3):
    n = a.ndim
    dims = (((n - 1,), (n - 2,)), ((0,), (0,))) if n == 3 else (((1,), (0,)), ((), ()))
    return lax.dot_general(a, b, dims, precision=precision, preferred_element_type=F32)


def mmh_nt(a, b):
    n = a.ndim
    dims = (((n - 1,), (n - 1,)), ((0,), (0,))) if n == 3 else (((1,), (1,)), ((), ()))
    return lax.dot_general(a, b, dims, precision=H3, preferred_element_type=F32)


def mmh_tn(a, b):
    n = a.ndim
    dims = (((n - 2,), (n - 2,)), ((0,), (0,))) if n == 3 else (((0,), (0,)), ((), ()))
    return lax.dot_general(a, b, dims, precision=H3, preferred_element_type=F32)


def matmul(a, b, mode, name, residual=None):
    out_dtype = BF16 if mode == "tn" else F32
    if mode == "nn":
        (M, K), (_, N) = a.shape, b.shape
    elif mode == "nt":
        (M, K), (N, _) = a.shape, b.shape
    else:
        (K, M), (_, N) = a.shape, b.shape
    tm = _tile(M, 2048 if mode == "nn" else MM_TILE_CAP)
    tn = _tile(N, 512 if mode == "nn" else MM_TILE_CAP)
    tk = _tile(K, MM_TILE_CAP)
    nk = K // tk
    if mode == "nn":
        a_spec = pl.BlockSpec((tm, tk), lambda i, j, k: (i, k))
        b_spec = pl.BlockSpec((tk, tn), lambda i, j, k: (k, j))
        dims = (((1,), (0,)), ((), ()))
    elif mode == "nt":
        a_spec = pl.BlockSpec((tm, tk), lambda i, j, k: (i, k))
        b_spec = pl.BlockSpec((tn, tk), lambda i, j, k: (j, k))
        dims = (((1,), (1,)), ((), ()))
    else:
        a_spec = pl.BlockSpec((tk, tm), lambda i, j, k: (k, i))
        b_spec = pl.BlockSpec((tk, tn), lambda i, j, k: (k, j))
        dims = (((0,), (0,)), ((), ()))
    o_spec = pl.BlockSpec((tm, tn), lambda i, j, k: (i, j))
    has_res = residual is not None

    def body(*refs):
        if has_res:
            a_ref, b_ref, r_ref, o_ref, acc_ref = refs
        else:
            a_ref, b_ref, o_ref, acc_ref = refs
        k = pl.program_id(2)

        @pl.when(k == 0)
        def _():
            acc_ref[...] = jnp.zeros_like(acc_ref)

        acc_ref[...] += lax.dot_general(a_ref[...].astype(BF16), b_ref[...].astype(BF16), dims,
                                        preferred_element_type=F32)

        @pl.when(k == nk - 1)
        def _():
            if has_res:
                o_ref[...] = (acc_ref[...] + r_ref[...]).astype(out_dtype)
            else:
                o_ref[...] = acc_ref[...].astype(out_dtype)

    ins = [a, b] + ([residual] if has_res else [])
    in_specs = [a_spec, b_spec] + ([o_spec] if has_res else [])
    return pl.pallas_call(
        body, grid=(M // tm, N // tn, nk), in_specs=in_specs, out_specs=o_spec,
        out_shape=jax.ShapeDtypeStruct((M, N), out_dtype), scratch_shapes=[pltpu.VMEM((tm, tn), F32)],
        compiler_params=_cparams(("parallel", "parallel", "arbitrary")), name=name)(*ins)


class Cols:
    def __init__(self, arr, width, idx):
        self.arr, self.width, self.idx = arr, width, idx


def _arr(x):
    return x.arr if isinstance(x, Cols) else x


def _shape(x):
    return x.arr.shape[:-1] + (x.width,) if isinstance(x, Cols) else x.shape


def _col(x):
    return x.idx if isinstance(x, Cols) else 0


def _tok_spec(x, tile):
    shape, col = _shape(x), _col(x)
    return pl.BlockSpec(shape[:-2] + (tile, shape[-1]), lambda i: (0,) * (len(shape) - 2) + (i, col))


def _full_spec(x):
    shape, col = _shape(x), _col(x)
    return pl.BlockSpec(shape, lambda i: (0,) * (len(shape) - 1) + (col,))


def _halo_spec(x, tile):
    shape, col = _shape(x), _col(x)
    return pl.BlockSpec((8, shape[-1]), lambda i: (jnp.maximum(i * (tile // 8) - 1, 0), col))


def _blk(x, tile):
    shape = _shape(x)
    return jax.ShapeDtypeStruct(shape[:-2] + (tile, shape[-1]), _arr(x).dtype)


def _prev_rows(x, halo):
    rows = lax.broadcasted_iota(jnp.int32, (x.shape[0], 1), 0)
    before = jnp.where(pl.program_id(0) > 0, halo[7:8], 0.0)
    return jnp.where(rows == 0, before, pltpu.roll(x, 1, 0))


def stage_fwd(f, xs, ps, cts, cfs, tile, name, out_dtypes=None, with_prev=False):
    xs, ps, cts, cfs = list(xs), list(ps), list(cts), list(cfs)
    halos = xs if with_prev else []
    nx, nh, nct, np_ = len(xs), len(halos), len(cts), len(ps)
    T = _shape(xs[0])[-2]
    blk = [_blk(x, tile) for x in xs]
    out_avals = jax.eval_shape(f, *blk, *(blk if with_prev else []), *[_blk(p, _shape(p)[-2]) for p in ps],
                               *[_blk(c, tile) for c in cts], *[_blk(c, _shape(c)[-2]) for c in cfs])
    if out_dtypes is None:
        out_dtypes = [o.dtype for o in out_avals]
    out_shape = [jax.ShapeDtypeStruct(o.shape[:-2] + (T, o.shape[-1]), dt) for o, dt in zip(out_avals, out_dtypes)]
    n_in = nx + nh + nct + np_ + len(cfs)

    def body(*refs):
        vals = [r[...] for r in refs[:n_in]]
        xv, hv, rest = vals[:nx], vals[nx:nx + nh], vals[nx + nh:]
        ctv, pv, cfv = rest[:nct], rest[nct:nct + np_], rest[nct + np_:]
        prev = [_prev_rows(x, h) for x, h in zip(xv, hv)]
        res = f(*xv, *prev, *pv, *ctv, *cfv)
        for o_ref, r in zip(refs[n_in:], res):
            o_ref[...] = r.astype(o_ref.dtype)

    return pl.pallas_call(
        body, grid=(T // tile,),
        in_specs=([_tok_spec(x, tile) for x in xs] + [_halo_spec(x, tile) for x in halos] + [_tok_spec(c, tile) for c in cts]
                  + [_full_spec(p) for p in ps + cfs]),
        out_specs=[_tok_spec(o, tile) for o in out_shape], out_shape=out_shape,
        compiler_params=_cparams(("parallel",)), name=name)(*[_arr(a) for a in xs + halos + cts + ps + cfs])


def stage_bwd(f, xs, ps, cts, cfs, gs, tile, name, bf16_copies=(), with_prev=False):
    xs, ps, cts, cfs = list(xs), list(ps), list(cts), list(cfs)
    gs = [list(g) if isinstance(g, (list, tuple)) else [g] for g in gs]
    g_flat = [a for g in gs for a in g]
    halos = xs if with_prev else []
    nx, nh, nct, ng, np_ = len(xs), len(halos), len(cts), len(g_flat), len(ps)
    T = _shape(xs[0])[-2]
    dx_like = xs + halos
    out_shape = ([jax.ShapeDtypeStruct(_shape(x), F32) for x in dx_like] + [jax.ShapeDtypeStruct(_shape(p), F32) for p in ps]
                 + [jax.ShapeDtypeStruct(_shape(xs[i]), BF16) for i in bf16_copies])
    n_in = nx + nh + nct + ng + np_ + len(cfs)
    ndx = nx + nh

    def body(*refs):
        vals = [r[...] for r in refs[:n_in]]
        outs = refs[n_in:]
        xv, hv, rest = vals[:nx], vals[nx:nx + nh], vals[nx + nh:]
        ctv, gparts, pv, cfv = rest[:nct], rest[nct:nct + ng], rest[nct + ng:nct + ng + np_], rest[nct + ng + np_:]
        gv = []
        for g in gs:
            gv.append(functools.reduce(lambda a, b: a + b, gparts[:len(g)]))
            gparts = gparts[len(g):]
        prev = [_prev_rows(x, h) for x, h in zip(xv, hv)]
        _, vjp = jax.vjp(lambda *xp: f(*xp, *ctv, *cfv), *xv, *prev, *pv)
        d = vjp(tuple(gv))
        for o_ref, r in zip(outs[:ndx], d[:ndx]):
            o_ref[...] = r
        for o_ref, i in zip(outs[ndx + np_:], bf16_copies):
            o_ref[...] = d[i].astype(BF16)

        @pl.when(pl.program_id(0) == 0)
        def _():
            for o_ref in outs[ndx:ndx + np_]:
                o_ref[...] = jnp.zeros_like(o_ref)

        for o_ref, r in zip(outs[ndx:ndx + np_], d[ndx:]):
            o_ref[...] += r

    plain = lambda x: jax.ShapeDtypeStruct(_shape(x), F32)
    res = pl.pallas_call(
        body, grid=(T // tile,),
        in_specs=([_tok_spec(x, tile) for x in xs] + [_halo_spec(x, tile) for x in halos]
                  + [_tok_spec(c, tile) for c in cts + g_flat] + [_full_spec(p) for p in ps + cfs]),
        out_specs=([_tok_spec(plain(x), tile) for x in dx_like] + [_full_spec(plain(p)) for p in ps]
                   + [_tok_spec(plain(xs[i]), tile) for i in bf16_copies]), out_shape=out_shape,
        compiler_params=_cparams(("arbitrary",)), name=name)(*[_arr(a) for a in xs + halos + cts + g_flat + ps + cfs])
    if bf16_copies:
        return list(res[:ndx]), list(res[ndx:ndx + np_]), list(res[ndx + np_:])
    return list(res[:ndx]), list(res[ndx:])


def _rms(x, g, eps=RMS_EPS):
    return x * lax.rsqrt(jnp.mean(x * x, axis=-1, keepdims=True) + eps) * g


def f_rmsnorm(x, g):
    return (_rms(x, g),)


def f_rmsnorm_res(x, g):
    return _rms(x, g), x


def f_rmsnorm2(x, g1, g2):
    n = x * lax.rsqrt(jnp.mean(x * x, axis=-1, keepdims=True) + RMS_EPS)
    return n * g1, n * g2


def f_rmsnorm2_res(x, g1, g2):
    return f_rmsnorm2(x, g1, g2) + (x,)


def _sigmoid(x):
    return 1.0 / (1.0 + jnp.exp(-x))


def _softplus(x):
    return jnp.maximum(x, 0.0) + jnp.log(1.0 + jnp.exp(-jnp.abs(x)))


def f_rwkv_pre(pr, pk, pv, pl_, qr, qk, qv, ql, mu_r, mu_k, mu_v, mu_l, w0, w2, a0, a2, g2, k_k, k_a, seg, seg_t):
    xr = pr + (qr - pr) * mu_r
    xk = pk + (qk - pk) * mu_k
    xv = pv + (qv - pv) * mu_v
    xl = pl_ + (ql - pl_) * mu_l
    w_log = -_softplus(-(w0 + mm_nn(jnp.tanh(xl), w2))) - 0.5
    lw = -jnp.exp(w_log)
    a = _sigmoid(a0 + mm_nn(xl, a2))
    g = mm_nn(_sigmoid(xl), g2)
    kkr = xk * k_k
    inv = lax.rsqrt(jnp.maximum(mmh(kkr * kkr, seg), 1e-24))
    kk = kkr * mmh(inv, seg_t)
    k2 = xk * (1.0 + (a - 1.0) * k_a)
    return xr, lw, k2, xv, kk, kk * a, g


def f_rwkv_post(y, r, k2, v, g, lnx_w, lnx_b, r_k, seg, seg_t):
    inv_n = 1.0 / HEAD_DIM
    m = mmh(mmh(y, seg) * inv_n, seg_t)
    yc = y - m
    rstd = lax.rsqrt(mmh(yc * yc, seg) * inv_n + LNX_EPS)
    yn = yc * mmh(rstd, seg_t) * lnx_w + lnx_b
    bonus = mmh(mmh(r * k2 * r_k, seg), seg_t) * v
    return ((yn + bonus) * g,)


def _headnorm(z, g, seg, seg_t):
    ms = mmh(z * z, seg) * (1.0 / HEAD_DIM)
    return z * mmh(lax.rsqrt(ms + RMS_EPS), seg_t) * g


def f_headnorm(z, g, seg, seg_t):
    return (_headnorm(z, g, seg, seg_t),)


def _rot_half(z):
    w = z.shape[1]
    half = HEAD_DIM // 2
    lane = lax.broadcasted_iota(jnp.int32, (1, w), 1)
    return jnp.where((lane & (HEAD_DIM - 1)) < half, -pltpu.roll(z, w - half, 1), pltpu.roll(z, half, 1))


@jax.custom_vjp
def _rotate_half(z):
    return _rot_half(z)


_rotate_half.defvjp(lambda z: (_rot_half(z), None), lambda _, g: (-_rot_half(g),))


def f_qkprep(z, g, cos, sin, seg, seg_t):
    zn = _headnorm(z, g, seg, seg_t)
    return (zn * cos + _rotate_half(zn) * sin,)


def _head_mask(width, h):
    lane = lax.broadcasted_iota(jnp.int32, (1, width), 1)
    return jnp.where((lane >> 6) == h, jnp.ones((), F32), 0.0)


def f_memattn(q, k, v, q_norm, seg, seg_t):
    qn = _headnorm(q, q_norm, seg, seg_t)
    out = jnp.zeros_like(q)
    for h in range(MEM_HEADS):
        m = _head_mask(MEM_WIDTH, h)
        s = mm_nt(qn * m, k) * (1.0 / math.sqrt(HEAD_DIM))
        s = s - jnp.max(s, axis=-1, keepdims=True)
        p = jnp.exp(s)
        p = p / jnp.sum(p, axis=-1, keepdims=True)
        out = out + mm_nn(p, v) * m
    return (out,)


def f_mix(o1, o2, o3, l1, l2, l3):
    mx = jnp.maximum(jnp.maximum(l1, l2), l3)
    e1, e2, e3 = jnp.exp(l1 - mx), jnp.exp(l2 - mx), jnp.exp(l3 - mx)
    return ((e1 * o1 + e2 * o2 + e3 * o3) / (e1 + e2 + e3),)


def _chunk_masks(L):
    t = lax.broadcasted_iota(jnp.int32, (L, L), 0)
    s = lax.broadcasted_iota(jnp.int32, (L, L), 1)
    return t, s


def _unit_lower_inverse(a):
    L = a.shape[-1]
    t, s = _chunk_masks(L)
    one = jnp.ones((), F32)
    blk = lambda sh: jnp.where((t >> sh) == (s >> sh), one, 0.0)
    n0 = a * blk(3)
    x = jnp.where(t == s, one, 0.0) - n0
    n2 = mmh(n0, n0)
    x = x + mmh(x, n2)
    x = x + mmh(x, mmh(n2, n2))
    for sh in (3, 4, 5):
        if (1 << sh) >= L:
            break
        off = a * (blk(sh + 1) - blk(sh))
        x = x - mmh(x, mmh(off, x))
    return x


@jax.custom_vjp
def _inverse_known(a, x):
    return x


def _inverse_known_fwd(a, x):
    return x, x


def _inverse_known_bwd(x, dx):
    return -mmh_nt(mmh_tn(x, dx), x), jnp.zeros_like(x)


_inverse_known.defvjp(_inverse_known_fwd, _inverse_known_bwd)


def _running_sum(x, reverse):
    L = x.shape[1]
    pos = lax.broadcasted_iota(jnp.int32, (1, L, 1), 1)
    step = 1
    while step < L:
        if reverse:
            x = x + jnp.where(pos < L - step, pltpu.roll(x, L - step, 1), 0.0)
        else:
            x = x + jnp.where(pos >= step, pltpu.roll(x, step, 1), 0.0)
        step *= 2
    return x


@jax.custom_vjp
def _cumsum_tokens(x):
    return _running_sum(x, False)


_cumsum_tokens.defvjp(lambda x: (_running_sum(x, False), None), lambda _, g: (_running_sum(g, True),))


def f_rwkv_chunk(s0, r, lw, k, v, kk, b, x_known=None):
    H, L, _ = r.shape
    t, s = _chunk_masks(L)
    one = jnp.ones((), F32)
    incl = jnp.where(t >= s, one, 0.0)
    strict = jnp.where(t > s, one, 0.0)
    cum = _cumsum_tokens(lw)
    w_in = jnp.exp(cum)
    w_ex = jnp.exp(cum - lw)
    w_inv = jnp.exp(-cum)
    rt, kkt, kt, bt = r * w_in, kk * w_ex, k * w_inv, b * w_inv
    a_b = mmh_nt(kkt, bt) * strict
    a_k = mmh_nt(kkt, kt) * strict
    m_k = mmh_nt(rt, kt) * incl
    m_b = mmh_nt(rt, bt) * incl
    x = _unit_lower_inverse(a_b) if x_known is None else _inverse_known(a_b, x_known)
    u = mmh(x, mmh_nt(kkt, s0) + mmh(a_k, v))
    y = mmh_nt(rt, s0) + mmh(m_k, v) - mmh(m_b, u)
    w_last = jnp.exp(jnp.sum(lw, axis=1, keepdims=True))
    s1 = (s0 + mmh_tn(v, kt) - mmh_tn(u, bt)) * w_last
    return y, s1, x


def _ex_split(ex, refs, n_in, n_out):
    n = ex.n
    ins, ex_in = refs[:n_in], refs[n_in:n_in + n]
    outs, ex_out = refs[n_in + n:n_in + n + n_out], refs[n_in + n + n_out:n_in + 2 * n + n_out]
    rest = refs[n_in + 2 * n + n_out:]
    return ins, outs, rest[:len(rest) - 3], (ex_in, ex_out) + tuple(rest[len(rest) - 3:])


def _split_heads(x):
    return jnp.stack([x[:, h * HEAD_DIM:(h + 1) * HEAD_DIM] for h in range(x.shape[1] // HEAD_DIM)], axis=0)


def _merge_heads(x):
    return jnp.concatenate([x[h] for h in range(x.shape[0])], axis=1)


def rwkv_scan_fwd(r, lw, k, v, kk, b, ex):
    T, N = r.shape[0], HEAD_DIM
    H = r.shape[1] // N
    groups = SCAN_GROUPS_FWD
    nc, hg = T // CHUNK, H // groups
    seq = pl.BlockSpec((CHUNK, hg * N), lambda g, c: (c, g))

    def body(*refs):
        (r_ref, lw_ref, k_ref, v_ref, kk_ref, b_ref), (y_ref, hs_ref, xs_ref), (h_scr,), ex_refs = _ex_split(ex, refs, 6, 3)
        g, c = pl.program_id(0), pl.program_id(1)

        @pl.when(jnp.logical_and(g == 0, c == 0))
        def _():
            ex.start(*ex_refs)

        @pl.when(c == 0)
        def _():
            h_scr[...] = jnp.zeros_like(h_scr)

        h0 = h_scr[...]
        hs_ref[0] = h0
        y, h1, x = f_rwkv_chunk(h0, *[_split_heads(z[...]) for z in (r_ref, lw_ref, k_ref, v_ref, kk_ref, b_ref)])
        y_ref[...] = _merge_heads(y)
        xs_ref[0] = x
        h_scr[...] = h1

        @pl.when(jnp.logical_and(g == groups - 1, c == (3 * nc) // 4))
        def _():
            ex.forward(*ex_refs)

        @pl.when(jnp.logical_and(g == groups - 1, c == nc - 1))
        def _():
            ex.wait(*ex_refs)

    res = pl.pallas_call(
        body, grid=(groups, nc), in_specs=[seq] * 6 + [_ANY] * ex.n,
        out_specs=[seq, pl.BlockSpec((1, hg, N, N), lambda g, c: (c, g, 0, 0)),
                   pl.BlockSpec((1, hg, CHUNK, CHUNK), lambda g, c: (c, g, 0, 0))] + [_ANY] * ex.n,
        out_shape=[jax.ShapeDtypeStruct((T, H * N), F32), jax.ShapeDtypeStruct((nc, H, N, N), F32),
                   jax.ShapeDtypeStruct((nc, H, CHUNK, CHUNK), F32)] + ex.out_shape(),
        scratch_shapes=[pltpu.VMEM((hg, N, N), F32)] + ex.scratch(),
        compiler_params=_cparams(("arbitrary", "arbitrary")), name="rwkv_scan_fwd")(r, lw, k, v, kk, b, *ex.operands())
    return res[0], (res[1], res[2]), list(res[3:])


def rwkv_scan_bwd(r, lw, k, v, kk, b, saved, dy, ex):
    T, N = r.shape[0], HEAD_DIM
    H = r.shape[1] // N
    groups = SCAN_GROUPS_BWD
    nc, hg = T // CHUNK, H // groups
    seq = pl.BlockSpec((CHUNK, hg * N), lambda g, c: (nc - 1 - c, g))
    state = pl.BlockSpec((1, hg, N, N), lambda g, c: (nc - 1 - c, g, 0, 0))

    def body(*refs):
        (r_ref, lw_ref, k_ref, v_ref, kk_ref, b_ref, hs_ref, xs_ref, dy_ref), outs, (dh_scr,), ex_refs = _ex_split(ex, refs, 9, 6)
        g, c = pl.program_id(0), pl.program_id(1)

        @pl.when(jnp.logical_and(g == 0, c == 0))
        def _():
            ex.start(*ex_refs)

        @pl.when(c == 0)
        def _():
            dh_scr[...] = jnp.zeros_like(dh_scr)

        x_known = xs_ref[0]
        _, vjp = jax.vjp(lambda *a: f_rwkv_chunk(*a, x_known=x_known)[:2], hs_ref[0],
                         *[_split_heads(z[...]) for z in (r_ref, lw_ref, k_ref, v_ref, kk_ref, b_ref)])
        d = vjp((_split_heads(dy_ref[...]), dh_scr[...]))
        dh_scr[...] = d[0]
        for o_ref, dz in zip(outs, d[1:]):
            o_ref[...] = _merge_heads(dz)

        @pl.when(jnp.logical_and(g == groups - 1, c == nc - 1))
        def _():
            ex.forward(*ex_refs)
            ex.wait(*ex_refs)

    res = pl.pallas_call(
        body, grid=(groups, nc),
        in_specs=[seq] * 6 + [state, state, seq] + [_ANY] * ex.n,
        out_specs=[seq] * 6 + [_ANY] * ex.n, out_shape=[jax.ShapeDtypeStruct((T, H * N), F32)] * 6 + ex.out_shape(),
        scratch_shapes=[pltpu.VMEM((hg, N, N), F32)] + ex.scratch(),
        compiler_params=_cparams(("arbitrary", "arbitrary")), name="rwkv_scan_bwd")(r, lw, k, v, kk, b, *saved, dy, *ex.operands())
    return list(res[:6]), list(res[6:])


GROUP_COLS = 4 * HEAD_DIM


def _f_dilattn(has_prev, q, kc, kp, vc, vp):
    scale = 1.0 / math.sqrt(HEAD_DIM)
    i = lax.broadcasted_iota(jnp.int32, (DIL_BLOCK, DIL_BLOCK), 0)
    j = lax.broadcasted_iota(jnp.int32, (DIL_BLOCK, DIL_BLOCK), 1)
    o, l = jnp.zeros_like(q), jnp.zeros_like(q)
    for h in range(q.shape[1] // HEAD_DIM):
        m = _head_mask(q.shape[1], h)
        sc = jnp.where(j <= i, mm_nt(q * m, kc) * scale, NEG_INF)
        sp = jnp.where(jnp.logical_and(i <= j, has_prev), mm_nt(q * m, kp) * scale, NEG_INF)
        mx = jnp.maximum(jnp.max(sc, axis=-1, keepdims=True), jnp.max(sp, axis=-1, keepdims=True))
        pc, pp = jnp.exp(sc - mx), jnp.exp(sp - mx)
        den = jnp.sum(pc, axis=-1, keepdims=True) + jnp.sum(pp, axis=-1, keepdims=True)
        o = o + (mm_nn(pc, vc) + mm_nn(pp, vp)) / den * m
        l = l + (mx + jnp.log(den)) * m
    return o, l


def _dil_specs(gi, d):
    parts = 1 if d == 1 else 2
    blk = (DIL_BLOCK * d, GROUP_COLS // parts)
    at = lambda col: (lambda p, n: (n, col * parts + p))
    before = lambda col: (lambda p, n: (jnp.maximum(n - 1, 0), col * parts + p))
    v0 = DIL_WIDTH // GROUP_COLS + gi
    q = pl.BlockSpec(blk, at(gi))
    kc, kp = pl.BlockSpec(blk, at(gi)), pl.BlockSpec(blk, before(gi))
    vc, vp = pl.BlockSpec(blk, at(v0)), pl.BlockSpec(blk, before(v0))
    out = pl.BlockSpec(blk, at(0))
    return (q, kc, kp, vc, vp, out), parts


def _residue_rows(r, d):
    return pl.ds(r, DIL_BLOCK, stride=d) if d > 1 else pl.ds(0, DIL_BLOCK)


def dil_fwd(q, k, kv, gi, d, name):
    T = q.shape[0]
    (qs, kc, kp, vc, vp, out), parts = _dil_specs(gi, d)

    def body(q_ref, kc_ref, kp_ref, vc_ref, vp_ref, o_ref, l_ref):
        has_prev = pl.program_id(1) > 0

        def residue(r, carry):
            rows = _residue_rows(r, d)
            o, l = _f_dilattn(has_prev, q_ref[rows, :], kc_ref[rows, :], kp_ref[rows, :], vc_ref[rows, :], vp_ref[rows, :])
            o_ref[rows, :] = o
            l_ref[rows, :] = l
            return carry

        lax.fori_loop(0, d, residue, 0, unroll=min(d, 2))

    shape = jax.ShapeDtypeStruct((T, 4 * HEAD_DIM), F32)
    return pl.pallas_call(
        body, grid=(parts, T // (DIL_BLOCK * d)), in_specs=[qs, kc, kp, vc, vp], out_specs=[out, out], out_shape=[shape, shape],
        compiler_params=_cparams(("parallel", "parallel")), name=name)(q, k, k, kv, kv)


def dil_bwd(q, k, kv, do, dl, gi, d, name):
    T = q.shape[0]
    (qs, kc, kp, vc, vp, out), parts = _dil_specs(gi, d)

    def body(q_ref, kc_ref, kp_ref, vc_ref, vp_ref, do_ref, dl_ref, *outs):
        f = functools.partial(_f_dilattn, pl.program_id(1) > 0)

        def residue(r, carry):
            rows = _residue_rows(r, d)
            _, vjp = jax.vjp(f, q_ref[rows, :], kc_ref[rows, :], kp_ref[rows, :], vc_ref[rows, :], vp_ref[rows, :])
            for o_ref, g in zip(outs, vjp((do_ref[rows, :], dl_ref[rows, :]))):
                o_ref[rows, :] = g
            return carry

        lax.fori_loop(0, d, residue, 0, unroll=min(d, 2))

    shape = jax.ShapeDtypeStruct((T, 4 * HEAD_DIM), F32)
    dq, dkc, dkp, dvc, dvp = pl.pallas_call(
        body, grid=(parts, T // (DIL_BLOCK * d)), in_specs=[qs, kc, kp, vc, vp, out, out], out_specs=[out] * 5, out_shape=[shape] * 5,
        compiler_params=_cparams(("parallel", "parallel")), name=name)(q, k, k, kv, kv, do, dl)

    def own_plus_next(c, p):
        return c + jnp.concatenate([p[DIL_BLOCK * d:], jnp.zeros_like(p[:DIL_BLOCK * d])], axis=0)

    return dq, own_plus_next(dkc, dkp), own_plus_next(dvc, dvp)


CONV_TILE = 128


def _conv3(u, h6, h7, w, b):
    rows = lax.broadcasted_iota(jnp.int32, (u.shape[0], 1), 0)
    s1 = jnp.where(rows == 0, h7, pltpu.roll(u, 1, 0))
    s2 = jnp.where(rows == 0, h6, jnp.where(rows == 1, h7, pltpu.roll(u, 2, 0)))
    return b + w[0:1] * s2 + w[1:2] * s1 + w[2:3] * u, s1, s2


def _conv_halves(u_ref, h_ref, cw_ref, cb_ref):
    F = D_FF
    first = pl.program_id(0) > 0
    res = []
    for lo in (0, F):
        h = h_ref[:, lo:lo + F]
        h6 = jnp.where(first, h[6:7], 0.0)
        h7 = jnp.where(first, h[7:8], 0.0)
        u = u_ref[:, lo:lo + F]
        res.append((u,) + _conv3(u, h6, h7, cw_ref[:, lo:lo + F], cb_ref[:, lo:lo + F]))
    return res


def _halo_before(C):
    return pl.BlockSpec((8, C), lambda i: (jnp.maximum(i * (CONV_TILE // 8) - 1, 0), 0))


def convgate_fwd(u, cw, cb, name):
    T, C = u.shape
    F = C // 2

    def body(u_ref, h_ref, cw_ref, cb_ref, z_ref):
        (_, cg, _, _), (_, cv, _, _) = _conv_halves(u_ref, h_ref, cw_ref, cb_ref)
        z_ref[...] = (cg * _sigmoid(cg) * cv).astype(BF16)

    return pl.pallas_call(
        body, grid=(T // CONV_TILE,),
        in_specs=[pl.BlockSpec((CONV_TILE, C), lambda i: (i, 0)), _halo_before(C), _full_spec(cw), _full_spec(cb)],
        out_specs=pl.BlockSpec((CONV_TILE, F), lambda i: (i, 0)), out_shape=jax.ShapeDtypeStruct((T, F), BF16),
        compiler_params=_cparams(("parallel",)), name=name)(u, u, cw, cb)


def convgate_bwd(u, cw, cb, dz, name):
    T, C = u.shape
    F = C // 2
    n = T // CONV_TILE
    E = CONV_TILE + 8

    def body(u_ref, hb_ref, ha_ref, cw_ref, cb_ref, dz_ref, dza_ref, du_ref, dcw_ref, dcb_ref):
        i = pl.program_id(0)
        dze = jnp.concatenate([dz_ref[...], jnp.where(i < n - 1, dza_ref[...], 0.0)], axis=0)

        @pl.when(i == 0)
        def _():
            dcw_ref[...] = jnp.zeros_like(dcw_ref)
            dcb_ref[...] = jnp.zeros_like(dcb_ref)

        halves = []
        for lo in (0, F):
            sl = slice(lo, lo + F)
            hb = hb_ref[:, sl]
            ue = jnp.concatenate([u_ref[:, sl], ha_ref[:, sl]], axis=0)
            c, s1, s2 = _conv3(ue, jnp.where(i > 0, hb[6:7], 0.0), jnp.where(i > 0, hb[7:8], 0.0), cw_ref[:, sl], cb_ref[:, sl])
            halves.append((sl, ue, c, s1, s2))
        (_, _, cg, _, _), (_, _, cv, _, _) = halves
        sg = _sigmoid(cg)
        dcs = (dze * cv * sg * (1.0 + cg * (1.0 - sg)), dze * cg * sg)
        for (sl, ue, _, s1, s2), dc in zip(halves, dcs):
            own = lambda z: z[:CONV_TILE]
            dcb_ref[:, sl] += jnp.sum(own(dc), axis=0, keepdims=True)
            dcw_ref[0:1, sl] += jnp.sum(own(dc * s2), axis=0, keepdims=True)
            dcw_ref[1:2, sl] += jnp.sum(own(dc * s1), axis=0, keepdims=True)
            dcw_ref[2:3, sl] += jnp.sum(own(dc * ue), axis=0, keepdims=True)
            du = cw_ref[2:3, sl] * dc + cw_ref[1:2, sl] * pltpu.roll(dc, E - 1, 0) + cw_ref[0:1, sl] * pltpu.roll(dc, E - 2, 0)
            du_ref[:, sl] = own(du).astype(BF16)

    after = lambda w: pl.BlockSpec((8, w), lambda i: (jnp.minimum((i + 1) * (CONV_TILE // 8), T // 8 - 1), 0))
    return pl.pallas_call(
        body, grid=(n,),
        in_specs=[pl.BlockSpec((CONV_TILE, C), lambda i: (i, 0)), _halo_before(C), after(C), _full_spec(cw), _full_spec(cb),
                  pl.BlockSpec((CONV_TILE, F), lambda i: (i, 0)), after(F)],
        out_specs=[pl.BlockSpec((CONV_TILE, C), lambda i: (i, 0)), _full_spec(cw), _full_spec(cb)],
        out_shape=[jax.ShapeDtypeStruct((T, C), BF16), jax.ShapeDtypeStruct(cw.shape, F32), jax.ShapeDtypeStruct(cb.shape, F32)],
        compiler_params=_cparams(("arbitrary",)), name=name)(u, u, u, cw, cb, dz, dz)


def loss_head(y, tgt):
    T, D = y.shape
    tile = 256

    def body(y_ref, t_ref, l_ref, d_ref, db_ref):
        d = y_ref[...] - t_ref[...]
        d_ref[...] = d * (1.0 / D)
        db_ref[...] = (d * (1.0 / D)).astype(BF16)

        @pl.when(pl.program_id(0) == 0)
        def _():
            l_ref[...] = jnp.zeros_like(l_ref)

        l_ref[...] += (0.5 / D) * jnp.sum(d * d)

    row = pl.BlockSpec((tile, D), lambda i: (i, 0))
    return pl.pallas_call(
        body, grid=(T // tile,), in_specs=[row, row], out_specs=[pl.BlockSpec((8, 128), lambda i: (0, 0)), row, row],
        out_shape=[jax.ShapeDtypeStruct((8, 128), F32), jax.ShapeDtypeStruct((T, D), F32), jax.ShapeDtypeStruct((T, D), BF16)],
        compiler_params=_cparams(("arbitrary",)), name="loss_head")(y, tgt)


def sum_parts(parts, name):
    S, R, C = parts.shape
    tile = _pick(R, (256, 128, 64, 32, 16, 8))

    def body(p_ref, o_ref):
        acc = p_ref[0]
        for s in range(1, S):
            acc = acc + p_ref[s]
        o_ref[...] = acc

    return pl.pallas_call(
        body, grid=(R // tile,), in_specs=[pl.BlockSpec((S, tile, C), lambda i: (0, i, 0))],
        out_specs=pl.BlockSpec((tile, C), lambda i: (i, 0)), out_shape=jax.ShapeDtypeStruct((R, C), F32),
        compiler_params=_cparams(("parallel",)), name=name)(parts)


def adamw(gparts, w, m, v, name):
    S, R, C = gparts.shape
    tile = _pick(R, (256, 128, 64, 32, 16, 8))
    c1 = 1.0 / (1.0 - ADAM_B1 ** ADAM_STEP)
    c2 = 1.0 / (1.0 - ADAM_B2 ** ADAM_STEP)

    def body(g_ref, w_ref, m_ref, v_ref, go_ref, d_ref, mo_ref, vo_ref):
        g = g_ref[0].astype(F32)
        for s in range(1, S):
            g = g + g_ref[s].astype(F32)
        m1 = ADAM_B1 * m_ref[...] + (1.0 - ADAM_B1) * g
        v1 = ADAM_B2 * v_ref[...] + (1.0 - ADAM_B2) * (g * g)
        go_ref[...] = g
        mo_ref[...] = m1
        vo_ref[...] = v1
        d_ref[...] = -ADAM_LR * ((m1 * c1) / (jnp.sqrt(v1 * c2) + ADAM_EPS) + ADAM_WD * w_ref[...])

    row = pl.BlockSpec((tile, C), lambda i: (i, 0))
    return pl.pallas_call(
        body, grid=(R // tile,), in_specs=[pl.BlockSpec((S, tile, C), lambda i: (0, i, 0)), row, row, row],
        out_specs=[row] * 4, out_shape=[jax.ShapeDtypeStruct((R, C), F32)] * 4,
        compiler_params=_cparams(("parallel",)), name=name)(gparts, w, m, v)


def _peers():
    x, y, c = lax.axis_index("x"), lax.axis_index("y"), lax.axis_index("c")
    peers = []
    for k in range(1, N_DEV):
        px = 1 - x if k & 4 else x
        py = 1 - y if k & 2 else y
        pc = 1 - c if k & 1 else c
        peers.append(((px, py, pc), 4 * px + 2 * py + pc))
    return 4 * x + 2 * y + c, peers


_ANY = pl.BlockSpec(memory_space=pl.ANY)


class Exchange:
    def __init__(self, gathers=(), scatters=()):
        self.gathers, self.scatters = list(gathers), list(scatters)
        self.n = len(self.gathers) + len(self.scatters)

    def operands(self):
        return self.gathers + self.scatters

    def out_shape(self):
        return ([jax.ShapeDtypeStruct((N_DEV,) + x.shape, x.dtype) for x in self.gathers]
                + [jax.ShapeDtypeStruct(x.shape, x.dtype) for x in self.scatters])

    def scratch(self):
        n = max(self.n, 1)
        return [pltpu.SemaphoreType.DMA((7 * n,)), pltpu.SemaphoreType.DMA((7 * n,)), pltpu.SemaphoreType.DMA((n,))]

    def _copies(self, in_refs, out_refs, send_sems, recv_sems, local_sems):
        me, peers = _peers()
        ng = len(self.gathers)
        local, sends, recvs = [], [], []
        for a in range(self.n):
            x, o = in_refs[a], out_refs[a]
            mine = x if a < ng else x.at[me]
            local.append(pltpu.make_async_copy(mine, o.at[me], local_sems.at[a]))
            s_a, r_a = {}, {}
            for k in range(1, N_DEV):
                peer, slot = peers[k - 1]
                sems = dict(send_sem=send_sems.at[7 * a + k - 1], recv_sem=recv_sems.at[7 * a + k - 1],
                            device_id_type=pl.DeviceIdType.MESH)
                if a >= ng:
                    s_a[k] = pltpu.make_async_remote_copy(src_ref=x.at[slot], dst_ref=o.at[me], device_id=peer, **sems)
                elif k in FORWARDED:
                    came = o.at[peers[k - 2][1]]
                    s_a[k] = pltpu.make_async_remote_copy(src_ref=came, dst_ref=came, device_id=peers[0][0], **sems)
                else:
                    s_a[k] = pltpu.make_async_remote_copy(src_ref=x, dst_ref=o.at[me], device_id=peer, **sems)
                r_a[k] = pltpu.make_async_remote_copy(src_ref=mine, dst_ref=o.at[slot], device_id=peer, **sems)
            sends.append(s_a)
            recvs.append(r_a)
        return local, sends, recvs

    def start(self, *refs):
        if self.n == 0:
            return
        local, sends, _ = self._copies(*refs)
        for a in range(self.n):
            local[a].start()
            for k in range(1, N_DEV):
                if a >= len(self.gathers) or k not in FORWARDED:
                    sends[a][k].start()

    def forward(self, *refs):
        if not self.gathers:
            return
        _, sends, recvs = self._copies(*refs)
        for a in range(len(self.gathers)):
            for k in FORWARDED:
                recvs[a][k - 1].wait_recv()
                sends[a][k].start()

    def wait(self, *refs):
        if self.n == 0:
            return
        local, sends, recvs = self._copies(*refs)
        for a in range(self.n):
            waited_early = [f - 1 for f in FORWARDED] if a < len(self.gathers) else []
            for k in range(1, N_DEV):
                if k not in waited_early:
                    recvs[a][k].wait_recv()
            for k in range(1, N_DEV):
                sends[a][k].wait_send()
            local[a].wait()


FORWARDED = (3, 5, 7)


def exchange(ex, name):
    n = ex.n

    def body(*refs):
        args = (refs[:n], refs[n:2 * n]) + tuple(refs[2 * n:])
        ex.start(*args)
        ex.forward(*args)
        ex.wait(*args)

    return pl.pallas_call(body, in_specs=[_ANY] * n, out_specs=[_ANY] * n, out_shape=ex.out_shape(),
                          scratch_shapes=ex.scratch(), name=name)(*ex.operands())


def _heads(z, h):
    return z.reshape(z.shape[0], h, HEAD_DIM).transpose(1, 0, 2)


def _unheads(z):
    return z.transpose(1, 0, 2).reshape(z.shape[1], z.shape[0] * HEAD_DIM)


def _shift_up(z):
    return jnp.concatenate([z[1:], jnp.zeros_like(z[:1])], axis=0)


def _segments(width):
    seg = np.zeros((width, 128), np.float32)
    seg[np.arange(width), np.arange(width) // HEAD_DIM] = 1.0
    return jnp.asarray(seg), jnp.asarray(seg.T)


def _rope_consts(T, heads):
    inv = ROPE_THETA ** (-jnp.arange(0, HEAD_DIM, 2, dtype=F32) / HEAD_DIM)
    ang = jnp.arange(T, dtype=F32)[:, None] * inv[None, :]
    return jnp.tile(jnp.cos(ang), (1, 2 * heads)), jnp.tile(jnp.sin(ang), (1, 2 * heads))


def _per_head(g, heads):
    return jnp.tile(g.reshape(1, HEAD_DIM), (1, heads))


def _sum_heads(g):
    return g.reshape(-1, HEAD_DIM).sum(axis=0, keepdims=True)


LORA_COLS = 256
RW_TILE = 128
ROW_TILE = 256


def _local_step(x0, memx, tgt, P, ex_weights=None, weights_done=None, ex_grads=None):
    T = x0.shape[0]
    P = dict(P)
    G = {}
    seg, seg_t = _segments(RWKV_WIDTH)
    mseg = (seg[:MEM_WIDTH], seg_t[:, :MEM_WIDTH])
    cos, sin = _rope_consts(T, DIL_WIDTH // HEAD_DIM)
    row = lambda v: v.reshape(1, -1)

    def mem_fwd(i, q):
        memn = stage_fwd(f_rmsnorm, [memx], [P["mem_norm"][i:i + 1]], [], [], N_MEM, f"mem{i}_norm", [BF16])[0]
        kvm = matmul(memn, P["mem_w_kv"][i], "nn", f"mem{i}_kv")
        kn, qn = _per_head(P["mem_k_norm"][i], MEM_HEADS), _per_head(P["mem_q_norm"][i], MEM_HEADS)
        km = stage_fwd(f_headnorm, [Cols(kvm, MEM_WIDTH, 0)], [kn], [], mseg, N_MEM, f"mem{i}_knorm")[0]
        om = stage_fwd(f_memattn, [q], [km, Cols(kvm, MEM_WIDTH, 1), qn], [], mseg, ROW_TILE, f"mem{i}_attn")[0]
        return om, (memn, kvm, km, kn, qn, q)

    def mem_bwd(i, saved, dymem):
        memn, kvm, km, kn, qn, q = saved
        (dq,), (dkm, dvm, g_qn) = stage_bwd(f_memattn, [q], [km, Cols(kvm, MEM_WIDTH, 1), qn], [], mseg, [dymem], ROW_TILE,
                                            f"mem{i}_attn_bwd")
        (dkraw,), (g_kn,) = stage_bwd(f_headnorm, [Cols(kvm, MEM_WIDTH, 0)], [kn], [], mseg, [dkm], N_MEM, f"mem{i}_knorm_bwd")
        dkvm = jnp.concatenate([dkraw, dvm], axis=1).astype(BF16)
        g_w = matmul(memn, dkvm, "tn", f"mem{i}_kv_dw")
        dmemn = matmul(dkvm, P["mem_w_kv"][i], "nt", f"mem{i}_kv_dx")
        _, (g_mn,) = stage_bwd(f_rmsnorm, [memx], [P["mem_norm"][i:i + 1]], [], [], [dmemn], N_MEM, f"mem{i}_norm_bwd")
        return dq, g_mn, g_w, _sum_heads(g_qn), _sum_heads(g_kn)

    def ffn_fwd(i, xin):
        hn = stage_fwd(f_rmsnorm, [xin], [P["ffn_norm"][i:i + 1]], [], [], ROW_TILE, f"ffn{i}_norm", [BF16])[0]
        u = matmul(hn, P["ffn_w_up"][i], "nn", f"ffn{i}_up")
        z = convgate_fwd(u, P["ffn_conv_w"][i], P["ffn_conv_b"][i:i + 1], f"ffn{i}_conv")
        return matmul(z, P["ffn_w_down"][i], "nn", f"ffn{i}_down", residual=xin), (hn, u, z)

    def ffn_bwd(i, xin, saved, dxo, dxo_b):
        hn, u, z = saved
        dz = matmul(dxo_b, P["ffn_w_down"][i], "nt", f"ffn{i}_down_dx")
        g_down = matmul(z, dxo_b, "tn", f"ffn{i}_down_dw")
        du, g_cw, g_cb = convgate_bwd(u, P["ffn_conv_w"][i], P["ffn_conv_b"][i:i + 1], dz, f"ffn{i}_conv_bwd")
        dhn = matmul(du, P["ffn_w_up"][i], "nt", f"ffn{i}_up_dx")
        g_up = matmul(hn, du, "tn", f"ffn{i}_up_dw")
        (dxin,), (g_n,), (dxin_b,) = stage_bwd(f_rmsnorm_res, [xin], [P["ffn_norm"][i:i + 1]], [], [], [dhn, dxo], ROW_TILE,
                                               f"ffn{i}_norm_bwd", bf16_copies=(0,))
        return dxin, dxin_b, g_n, g_up, g_cw, g_cb, g_down

    h0 = stage_fwd(f_rmsnorm, [x0], [P["attn_norm"][0:1]], [], [], ROW_TILE, "l0_norm", [BF16])[0]
    p0 = matmul(h0, P["a_w_in"][0], "nn", "l0_in")
    lora0 = 3 * RWKV_WIDTH // LORA_COLS
    pre_xs = [Cols(p0, RWKV_WIDTH, 0), Cols(p0, RWKV_WIDTH, 1), Cols(p0, RWKV_WIDTH, 2), Cols(p0, LORA_COLS, lora0)]
    mu = [Cols(P["a_mu"], RWKV_WIDTH, 0), Cols(P["a_mu"], RWKV_WIDTH, 1), Cols(P["a_mu"], RWKV_WIDTH, 2),
          Cols(P["a_mu"], LORA_COLS, lora0)]
    lora_rows = lambda w, lo: jnp.pad(w, ((lo, LORA_COLS - lo - w.shape[0]), (0, 0)))
    pre_ps = mu + [P["a_w0"], lora_rows(P["a_w2"][0], 0), P["a_a0"], lora_rows(P["a_a2"][0], 64), lora_rows(P["a_g2"][0], 128),
                   P["a_k_k"], P["a_k_a"]]
    r, lw, k2, v, kk, b, g = stage_fwd(f_rwkv_pre, pre_xs, pre_ps, [], [seg, seg_t], RW_TILE, "l0_rwkv_pre", with_prev=True)
    scan_in = [r, lw, k2, v, kk, b]
    y_h, h_states, got = rwkv_scan_fwd(*scan_in, ex_weights or Exchange())
    if weights_done is not None:
        P.update(weights_done(got))
    y_s = y_h
    post_ps = [P["a_lnx_w"], P["a_lnx_b"], P["a_r_k"].reshape(1, RWKV_WIDTH)]
    ymix0 = stage_fwd(f_rwkv_post, [y_s, r, k2, v, g], post_ps, [], [seg, seg_t], RW_TILE, "l0_rwkv_post")[0]
    ymem0, mem0_saved = mem_fwd(0, Cols(p0, MEM_WIDTH, SHIFT_WIDTH // MEM_WIDTH))
    ycat0 = jnp.concatenate([ymix0, ymem0], axis=1).astype(BF16)
    x1 = matmul(ycat0, P["a_w_out"][0], "nn", "l0_out", residual=x0)
    x2, ffn0_saved = ffn_fwd(0, x1)

    hk, h1 = stage_fwd(f_rmsnorm2, [x2], [row(P["kv_norm"]), P["attn_norm"][1:2]], [], [], ROW_TILE, "l1_norm", [BF16, BF16])
    kvp = matmul(hk, P["kv_w"][0], "nn", "l1_kv")
    p1 = matmul(h1, P["b_w_in"][0], "nn", "l1_in")
    kraw, qraw = Cols(kvp, DIL_WIDTH, 0), Cols(p1, DIL_WIDTH, 0)
    kgain, qgain = _per_head(P["kv_k_norm"], DIL_WIDTH // HEAD_DIM), _per_head(P["b_q_norm"], DIL_WIDTH // HEAD_DIM)
    ksh = stage_fwd(f_qkprep, [kraw], [kgain], [cos, sin], [seg, seg_t], ROW_TILE, "l1_kprep")[0]
    q = stage_fwd(f_qkprep, [qraw], [qgain], [cos, sin], [seg, seg_t], ROW_TILE, "l1_qprep")[0]
    outs, lses = [], []
    for gi, (_, d) in enumerate(DIL_GROUPS):
        og, lg = dil_fwd(q, ksh, kvp, gi, d, f"l1_dil{gi}")
        outs.append(og)
        lses.append(lg)
    omix = stage_fwd(f_mix, outs + lses, [], [], [], ROW_TILE, "l1_mix")[0]
    ymem1, mem1_saved = mem_fwd(1, Cols(p1, MEM_WIDTH, DIL_WIDTH // MEM_WIDTH))
    ycat1 = jnp.concatenate([omix, ymem1], axis=1).astype(BF16)
    x3 = matmul(ycat1, P["b_w_out"][0], "nn", "l1_out", residual=x2)
    x4, ffn1_saved = ffn_fwd(1, x3)
    loss_part, dx4, dx4_b = loss_head(x4, tgt)

    dx3, dx3_b, gn1, gup1, gcw1, gcb1, gdown1 = ffn_bwd(1, x3, ffn1_saved, dx4, dx4_b)
    dycat1 = matmul(dx3_b, P["b_w_out"][0], "nt", "l1_out_dx")
    G["b_w_out"] = [matmul(ycat1, dx3_b, "tn", "l1_out_dw")]
    dqmem1, gmn1, gmw1, gmq1, gmk1 = mem_bwd(1, mem1_saved, Cols(dycat1, MEM_WIDTH, 1))
    dmix, _ = stage_bwd(f_mix, outs + lses, [], [], [], [Cols(dycat1, MEM_WIDTH, 0)], ROW_TILE, "l1_mix_bwd")
    dq, dk, dv = zip(*[dil_bwd(q, ksh, kvp, dmix[gi], dmix[3 + gi], gi, d, f"l1_dil{gi}_bwd")
                       for gi, (_, d) in enumerate(DIL_GROUPS)])
    dq, dk, dv = jnp.concatenate(dq, axis=1), jnp.concatenate(dk, axis=1), jnp.concatenate(dv, axis=1)
    (dqraw,), (g_bq,) = stage_bwd(f_qkprep, [qraw], [qgain], [cos, sin], [seg, seg_t], [dq], ROW_TILE, "l1_qprep_bwd")
    (dkraw,), (g_kk,) = stage_bwd(f_qkprep, [kraw], [kgain], [cos, sin], [seg, seg_t], [dk], ROW_TILE, "l1_kprep_bwd")
    g_bq, g_kk = _sum_heads(g_bq), _sum_heads(g_kk)
    dp1 = jnp.concatenate([dqraw, dqmem1], axis=1).astype(BF16)
    dkvp = jnp.concatenate([dkraw, dv], axis=1).astype(BF16)
    dh1 = matmul(dp1, P["b_w_in"][0], "nt", "l1_in_dx")
    G["b_w_in"] = [matmul(h1, dp1, "tn", "l1_in_dw")]
    dhk = matmul(dkvp, P["kv_w"][0], "nt", "l1_kv_dx")
    G["kv_w"] = [matmul(hk, dkvp, "tn", "l1_kv_dw")]
    (dx2,), (g_kvn, g_an1), (dx2_b,) = stage_bwd(f_rmsnorm2_res, [x2], [row(P["kv_norm"]), P["attn_norm"][1:2]], [], [],
                                                 [dhk, dh1, dx3], ROW_TILE, "l1_norm_bwd", bf16_copies=(0,))

    dx1, dx1_b, gn0, gup0, gcw0, gcb0, gdown0 = ffn_bwd(0, x1, ffn0_saved, dx2, dx2_b)
    dycat0 = matmul(dx1_b, P["a_w_out"][0], "nt", "l0_out_dx")
    G["a_w_out"] = [matmul(ycat0, dx1_b, "tn", "l0_out_dw")]
    dqmem0, gmn0, gmw0, gmq0, gmk0 = mem_bwd(0, mem0_saved, Cols(dycat0, MEM_WIDTH, RWKV_WIDTH // MEM_WIDTH))
    (dy_s, dr_a, dk_a, dv_a, dg), (g_lw, g_lb, g_rk) = stage_bwd(
        f_rwkv_post, [y_s, r, k2, v, g], post_ps, [], [seg, seg_t], [Cols(dycat0, RWKV_WIDTH, 0)], RW_TILE, "l0_rwkv_post_bwd")
    G["mem_w_kv"], G["ffn_w_up"], G["ffn_w_down"] = [gmw0, gmw1], [gup0, gup1], [gdown0, gdown1]
    (dr_b, dlw, dk_b, dv_b, dkk, db), G["_exchanged"] = rwkv_scan_bwd(*scan_in, h_states, dy_s,
                                                                      ex_grads(G) if ex_grads else Exchange())
    dpre, gpre = stage_bwd(f_rwkv_pre, pre_xs, pre_ps, [], [seg, seg_t],
                           [[dr_a, dr_b], dlw, [dk_a, dk_b], [dv_a, dv_b], dkk, db, dg], RW_TILE, "l0_rwkv_pre_bwd", with_prev=True)
    dp_rw = jnp.concatenate(dpre[:4], axis=1) + _shift_up(jnp.concatenate(dpre[4:], axis=1))
    dp0 = jnp.concatenate([dp_rw, dqmem0], axis=1).astype(BF16)
    dh0 = matmul(dp0, P["a_w_in"][0], "nt", "l0_in_dx")
    G["a_w_in"] = [matmul(h0, dp0, "tn", "l0_in_dw")]
    (dx0,), (g_an0,) = stage_bwd(f_rmsnorm_res, [x0], [P["attn_norm"][0:1]], [], [], [dh0, dx1], ROW_TILE, "l0_norm_bwd")

    G["attn_norm"] = jnp.concatenate([g_an0, g_an1], axis=0)
    G["a_mu"] = jnp.concatenate(gpre[:4], axis=1)
    G["a_w0"], G["a_w2"], G["a_a0"], G["a_a2"], G["a_g2"] = gpre[4], gpre[5][None, :64], gpre[6], gpre[7][None, 64:128], gpre[8][None, 128:]
    G["a_k_k"], G["a_k_a"] = gpre[9], gpre[10]
    G["a_r_k"] = g_rk.reshape(1, RWKV_HEADS, HEAD_DIM)
    G["a_lnx_w"], G["a_lnx_b"] = g_lw, g_lb
    G["kv_norm"], G["kv_k_norm"], G["b_q_norm"] = g_kvn.reshape(-1), g_kk.reshape(-1), g_bq
    G["mem_norm"] = jnp.concatenate([gmn0, gmn1], axis=0)
    G["mem_w_kv"] = [gmw0, gmw1]
    G["mem_q_norm"] = jnp.concatenate([gmq0, gmq1], axis=0)
    G["mem_k_norm"] = jnp.concatenate([gmk0, gmk1], axis=0)
    G["ffn_norm"] = jnp.concatenate([gn0, gn1], axis=0)
    G["ffn_w_up"] = [gup0, gup1]
    G["ffn_conv_w"] = jnp.stack([gcw0, gcw1])
    G["ffn_conv_b"] = jnp.concatenate([gcb0, gcb1], axis=0)
    G["ffn_w_down"] = [gdown0, gdown1]
    return loss_part, dx0, G


PARAMS = (("attn_norm", None), ("a_w_in", 2), ("a_mu", 1), ("a_w0", 1), ("a_w2", 2), ("a_a0", 1), ("a_a2", 2), ("a_g2", 2),
          ("a_k_k", 1), ("a_k_a", 1), ("a_r_k", None), ("a_lnx_w", 1), ("a_lnx_b", 1), ("a_w_out", 1), ("kv_norm", None),
          ("kv_w", 1), ("kv_k_norm", None), ("b_w_in", 1), ("b_q_norm", None), ("b_w_out", 2), ("mem_norm", None),
          ("mem_w_kv", 1), ("mem_q_norm", None), ("mem_k_norm", None), ("ffn_norm", None), ("ffn_w_up", 2),
          ("ffn_conv_w", 2), ("ffn_conv_b", None), ("ffn_w_down", 1))
BIG = ("a_w_in", "a_w_out", "kv_w", "b_w_in", "b_w_out", "mem_w_kv", "ffn_w_up", "ffn_w_down")
AXIS = dict(PARAMS)
SMALL = tuple(n for n, _ in PARAMS if n not in BIG)
SMALL_SHARDED = tuple(n for n in SMALL if AXIS[n] is not None)
PACK_QUANTUM = 256 * 128


def _from_shards(xs, axis):
    full = jnp.moveaxis(xs, 0, axis)
    sh = full.shape
    return full.reshape(sh[:axis] + (sh[axis] * sh[axis + 1],) + sh[axis + 2:])


def _to_shards(g, axis):
    sh = g.shape
    return jnp.moveaxis(g.reshape(sh[:axis] + (N_DEV, sh[axis] // N_DEV) + sh[axis + 1:]), axis, 0)


def _pack(parts, lead=0):
    ld = parts[0].shape[:lead]
    flat = jnp.concatenate([p.reshape(ld + (-1,)) for p in parts], axis=-1)
    pad = (-flat.shape[-1]) % PACK_QUANTUM
    flat = jnp.pad(flat, [(0, 0)] * lead + [(0, pad)])
    return flat.reshape(ld + (-1, 128))


def _unpack(packed, shapes, lead=0):
    ld = packed.shape[:lead]
    flat = packed.reshape(ld + (-1,))
    out, off = [], 0
    for s in shapes:
        n = math.prod(s)
        out.append(flat[..., off:off + n].reshape(ld + tuple(s)))
        off += n
    return out


def kernel(x, mem, attn_norm, a_w_in, a_mu, a_w0, a_w2, a_a0, a_a2, a_g2, a_k_k, a_k_a, a_r_k, a_lnx_w, a_lnx_b, a_w_out, kv_norm, kv_w, kv_k_norm, b_w_in, b_q_norm, b_w_out, mem_norm, mem_w_kv, mem_q_norm, mem_k_norm, ffn_norm, ffn_w_up, ffn_conv_w, ffn_conv_b, ffn_w_down, loss_target, m_attn_norm, m_a_w_in, m_a_mu, m_a_w0, m_a_w2, m_a_a0, m_a_a2, m_a_g2, m_a_k_k, m_a_k_a, m_a_r_k, m_a_lnx_w, m_a_lnx_b, m_a_w_out, m_kv_norm, m_kv_w, m_kv_k_norm, m_b_w_in, m_b_q_norm, m_b_w_out, m_mem_norm, m_mem_w_kv, m_mem_q_norm, m_mem_k_norm, m_ffn_norm, m_ffn_w_up, m_ffn_conv_w, m_ffn_conv_b, m_ffn_w_down, v_attn_norm, v_a_w_in, v_a_mu, v_a_w0, v_a_w2, v_a_a0, v_a_a2, v_a_g2, v_a_k_k, v_a_k_a, v_a_r_k, v_a_lnx_w, v_a_lnx_b, v_a_w_out, v_kv_norm, v_kv_w, v_kv_k_norm, v_b_w_in, v_b_q_norm, v_b_w_out, v_mem_norm, v_mem_w_kv, v_mem_q_norm, v_mem_k_norm, v_ffn_norm, v_ffn_w_up, v_ffn_conv_w, v_ffn_conv_b, v_ffn_w_down):
    names = [n for n, _ in PARAMS]
    vals = (attn_norm, a_w_in, a_mu, a_w0, a_w2, a_a0, a_a2, a_g2, a_k_k, a_k_a, a_r_k, a_lnx_w, a_lnx_b, a_w_out, kv_norm, kv_w, kv_k_norm, b_w_in, b_q_norm, b_w_out, mem_norm, mem_w_kv, mem_q_norm, mem_k_norm, ffn_norm, ffn_w_up, ffn_conv_w, ffn_conv_b, ffn_w_down)
    m_vals = (m_attn_norm, m_a_w_in, m_a_mu, m_a_w0, m_a_w2, m_a_a0, m_a_a2, m_a_g2, m_a_k_k, m_a_k_a, m_a_r_k, m_a_lnx_w, m_a_lnx_b, m_a_w_out, m_kv_norm, m_kv_w, m_kv_k_norm, m_b_w_in, m_b_q_norm, m_b_w_out, m_mem_norm, m_mem_w_kv, m_mem_q_norm, m_mem_k_norm, m_ffn_norm, m_ffn_w_up, m_ffn_conv_w, m_ffn_conv_b, m_ffn_w_down)
    v_vals = (v_attn_norm, v_a_w_in, v_a_mu, v_a_w0, v_a_w2, v_a_a0, v_a_a2, v_a_g2, v_a_k_k, v_a_k_a, v_a_r_k, v_a_lnx_w, v_a_lnx_b, v_a_w_out, v_kv_norm, v_kv_w, v_kv_k_norm, v_b_w_in, v_b_q_norm, v_b_w_out, v_mem_norm, v_mem_w_kv, v_mem_q_norm, v_mem_k_norm, v_ffn_norm, v_ffn_w_up, v_ffn_conv_w, v_ffn_conv_b, v_ffn_w_down)
    W, M, V = dict(zip(names, vals)), dict(zip(names, m_vals)), dict(zip(names, v_vals))
    me = 4 * lax.axis_index("x") + 2 * lax.axis_index("y") + lax.axis_index("c")
    layers = lambda D, n: [D[n]] if D[n].ndim == 2 else [D[n][i] for i in range(D[n].shape[0])]
    ax2 = lambda n: AXIS[n] - (W[n].ndim - 2)
    first = [("a_w_in", 0)]
    later = [(n, i) for n in BIG if n != "a_w_in" for i in range(len(layers(W, n)))]

    small_shapes = [W[n].shape for n in SMALL_SHARDED]
    got_w, got_small = exchange(Exchange(gathers=[W["a_w_in"][0].astype(BF16), _pack([W[n] for n in SMALL_SHARDED])]),
                                "gather_first")
    P = {n: W[n] for n in SMALL}
    P["a_w_in"] = [_from_shards(got_w, ax2("a_w_in"))]
    for n, s in zip(SMALL_SHARDED, _unpack(got_small, small_shapes, lead=1)):
        P[n] = _from_shards(s, AXIS[n])
    ex_weights = Exchange(gathers=[layers(W, n)[i].astype(BF16) for n, i in later])

    def weights_done(got):
        out = {}
        for (n, _), g in zip(later, got):
            out.setdefault(n, []).append(_from_shards(g, ax2(n)))
        return out

    slots = lambda G, n: jnp.stack([_to_shards(g, ax2(n)) for g in G[n]], axis=1)
    later_names = [n for n in BIG if n != "a_w_in"]
    ex_grads = lambda G: Exchange(scatters=[slots(G, n) for n in later_names])
    loss_part, dx0, G = _local_step(x[0], mem[0], loss_target[0], P, ex_weights, weights_done, ex_grads)
    loss = lax.psum(loss_part[0, 0], ("x", "y", "c"))
    gparts = dict(zip(later_names, G.pop("_exchanged")))
    got_gsmall, gparts["a_w_in"] = exchange(
        Exchange(gathers=[_pack([G[n] for n in SMALL])], scatters=[slots(G, "a_w_in")]), "exchange_last")

    results = {}
    for n in BIG:
        rows = lambda z: z.reshape((-1,) + z.shape[-1:])
        res = adamw(gparts[n].reshape((N_DEV, -1) + gparts[n].shape[-1:]), rows(W[n]), rows(M[n]), rows(V[n]), f"adamw_{n}")
        results[n] = [r.reshape(W[n].shape) for r in res]
    g_small = sum_parts(got_gsmall, "sum_small_grads")
    mine = []
    for n, g in zip(SMALL, _unpack(g_small, [G[n].shape for n in SMALL])):
        if AXIS[n] is not None:
            s = W[n].shape[AXIS[n]]
            g = lax.dynamic_slice_in_dim(g, me * s, s, axis=AXIS[n])
        mine.append(g)
    res = adamw(_pack(mine)[None], _pack([W[n] for n in SMALL]), _pack([M[n] for n in SMALL]), _pack([V[n] for n in SMALL]),
                "adamw_small")
    for n, parts in zip(SMALL, zip(*[_unpack(r, [W[n].shape for n in SMALL]) for r in res])):
        results[n] = list(parts)
    outs = [[results[n][j] for n in names] for j in range(4)]
    return (loss, dx0[None], *outs[0], *outs[1], *outs[2], *outs[3])
```

```python
import functools
import math

import jax
import jax.numpy as jnp
import numpy as np
from jax import lax
from jax.experimental import pallas as pl
from jax.experimental.pallas import tpu as pltpu

F32 = jnp.float32
BF16 = jnp.bfloat16
HI = lax.Precision.HIGHEST
H3 = lax.Precision.HIGH

N_DEV = 8
D_MODEL = 1024
HEAD_DIM = 64
N_MEM = 256
MEM_HEADS = 4
MEM_WIDTH = 256
RWKV_HEADS = 12
RWKV_WIDTH = 768
SHIFT_WIDTH = 2560
DIL_GROUPS = ((128, 1), (512, 4), (2048, 16))
DIL_BLOCK = 128
DIL_WIDTH = 768
D_FF = 2816
RMS_EPS = 1e-6
LNX_EPS = 64e-5
NEG_INF = -1e30
ROPE_THETA = 10000.0
ADAM_LR, ADAM_B1, ADAM_B2, ADAM_EPS, ADAM_WD, ADAM_STEP = 0.001, 0.9, 0.999, 1e-08, 0.01, 10

CHUNK = 64
SCAN_GROUPS_FWD, SCAN_GROUPS_BWD = 1, 1
MM_TILE_CAP = 1408
VMEM_LIMIT_V7X = 48 * 1024 * 1024


def _cparams(sem):
    return pltpu.CompilerParams(dimension_semantics=sem, vmem_limit_bytes=VMEM_LIMIT_V7X)


def _pick(n, cands):
    for c in cands:
        if n % c == 0:
            return c
    return n


def _tile(n, cap):
    if n <= cap:
        return n
    for d in range(cap - cap % 128, 0, -128):
        if n % d == 0:
            return d
    return n


def _dg(a, b, ca, cb, batch):
    dims = (((ca,), (cb,)), ((0,), (0,))) if batch else (((ca,), (cb,)), ((), ()))
    return lax.dot_general(a.astype(BF16), b.astype(BF16), dims, preferred_element_type=F32)


@jax.custom_vjp
def mm_nn(a, b):
    n = a.ndim
    return _dg(a, b, n - 1, n - 2, n == 3)


def _mm_nn_fwd(a, b):
    return mm_nn(a, b), (a, b)


def _mm_nn_bwd(res, g):
    a, b = res
    n = a.ndim
    return _dg(g, b, n - 1, n - 1, n == 3), _dg(a, g, n - 2, n - 2, n == 3)


mm_nn.defvjp(_mm_nn_fwd, _mm_nn_bwd)


@jax.custom_vjp
def mm_nt(a, b):
    n = a.ndim
    return _dg(a, b, n - 1, n - 1, n == 3)


def _mm_nt_fwd(a, b):
    return mm_nt(a, b), (a, b)


def _mm_nt_bwd(res, g):
    a, b = res
    n = a.ndim
    return _dg(g, b, n - 1, n - 2, n == 3), _dg(g, a, n - 2, n - 2, n == 3)


mm_nt.defvjp(_mm_nt_fwd, _mm_nt_bwd)


def mmh(a, b, precision=H3):
    n = a.ndim
    dims = (((n - 1,), (n - 2,)), ((0,), (0,))) if n == 3 else (((1,), (0,)), ((), ()))
    return lax.dot_general(a, b, dims, precision=precision, preferred_element_type=F32)


def mmh_nt(a, b):
    n = a.ndim
    dims = (((n - 1,), (n - 1,)), ((0,), (0,))) if n == 3 else (((1,), (1,)), ((), ()))
    return lax.dot_general(a, b, dims, precision=H3, preferred_element_type=F32)


def mmh_tn(a, b):
    n = a.ndim
    dims = (((n - 2,), (n - 2,)), ((0,), (0,))) if n == 3 else (((0,), (0,)), ((), ()))
    return lax.dot_general(a, b, dims, precision=H3, preferred_element_type=F32)


def matmul(a, b, mode, name, residual=None):
    out_dtype = BF16 if mode == "tn" else F32
    if mode == "nn":
        (M, K), (_, N) = a.shape, b.shape
    elif mode == "nt":
        (M, K), (N, _) = a.shape, b.shape
    else:
        (K, M), (_, N) = a.shape, b.shape
    tm = _tile(M, 2048 if mode == "nn" else MM_TILE_CAP)
    tn = _tile(N, 512 if mode == "nn" else MM_TILE_CAP)
    tk = _tile(K, MM_TILE_CAP)
    nk = K // tk
    if mode == "nn":
        a_spec = pl.BlockSpec((tm, tk), lambda i, j, k: (i, k))
        b_spec = pl.BlockSpec((tk, tn), lambda i, j, k: (k, j))
        dims = (((1,), (0,)), ((), ()))
    elif mode == "nt":
        a_spec = pl.BlockSpec((tm, tk), lambda i, j, k: (i, k))
        b_spec = pl.BlockSpec((tn, tk), lambda i, j, k: (j, k))
        dims = (((1,), (1,)), ((), ()))
    else:
        a_spec = pl.BlockSpec((tk, tm), lambda i, j, k: (k, i))
        b_spec = pl.BlockSpec((tk, tn), lambda i, j, k: (k, j))
        dims = (((0,), (0,)), ((), ()))
    o_spec = pl.BlockSpec((tm, tn), lambda i, j, k: (i, j))
    has_res = residual is not None

    def body(*refs):
        if has_res:
            a_ref, b_ref, r_ref, o_ref, acc_ref = refs
        else:
            a_ref, b_ref, o_ref, acc_ref = refs
        k = pl.program_id(2)

        @pl.when(k == 0)
        def _():
            acc_ref[...] = jnp.zeros_like(acc_ref)

        acc_ref[...] += lax.dot_general(a_ref[...].astype(BF16), b_ref[...].astype(BF16), dims,
                                        preferred_element_type=F32)

        @pl.when(k == nk - 1)
        def _():
            if has_res:
                o_ref[...] = (acc_ref[...] + r_ref[...]).astype(out_dtype)
            else:
                o_ref[...] = acc_ref[...].astype(out_dtype)

    ins = [a, b] + ([residual] if has_res else [])
    in_specs = [a_spec, b_spec] + ([o_spec] if has_res else [])
    return pl.pallas_call(
        body, grid=(M // tm, N // tn, nk), in_specs=in_specs, out_specs=o_spec,
        out_shape=jax.ShapeDtypeStruct((M, N), out_dtype), scratch_shapes=[pltpu.VMEM((tm, tn), F32)],
        compiler_params=_cparams(("parallel", "parallel", "arbitrary")), name=name)(*ins)


class Cols:
    def __init__(self, arr, width, idx):
        self.arr, self.width, self.idx = arr, width, idx


def _arr(x):
    return x.arr if isinstance(x, Cols) else x


def _shape(x):
    return x.arr.shape[:-1] + (x.width,) if isinstance(x, Cols) else x.shape


def _col(x):
    return x.idx if isinstance(x, Cols) else 0


def _tok_spec(x, tile):
    shape, col = _shape(x), _col(x)
    return pl.BlockSpec(shape[:-2] + (tile, shape[-1]), lambda i: (0,) * (len(shape) - 2) + (i, col))


def _full_spec(x):
    shape, col = _shape(x), _col(x)
    return pl.BlockSpec(shape, lambda i: (0,) * (len(shape) - 1) + (col,))


def _halo_spec(x, tile):
    shape, col = _shape(x), _col(x)
    return pl.BlockSpec((8, shape[-1]), lambda i: (jnp.maximum(i * (tile // 8) - 1, 0), col))


def _blk(x, tile):
    shape = _shape(x)
    return jax.ShapeDtypeStruct(shape[:-2] + (tile, shape[-1]), _arr(x).dtype)


def _prev_rows(x, halo):
    rows = lax.broadcasted_iota(jnp.int32, (x.shape[0], 1), 0)
    before = jnp.where(pl.program_id(0) > 0, halo[7:8], 0.0)
    return jnp.where(rows == 0, before, pltpu.roll(x, 1, 0))


def stage_fwd(f, xs, ps, cts, cfs, tile, name, out_dtypes=None, with_prev=False):
    xs, ps, cts, cfs = list(xs), list(ps), list(cts), list(cfs)
    halos = xs if with_prev else []
    nx, nh, nct, np_ = len(xs), len(halos), len(cts), len(ps)
    T = _shape(xs[0])[-2]
    blk = [_blk(x, tile) for x in xs]
    out_avals = jax.eval_shape(f, *blk, *(blk if with_prev else []), *[_blk(p, _shape(p)[-2]) for p in ps],
                               *[_blk(c, tile) for c in cts], *[_blk(c, _shape(c)[-2]) for c in cfs])
    if out_dtypes is None:
        out_dtypes = [o.dtype for o in out_avals]
    out_shape = [jax.ShapeDtypeStruct(o.shape[:-2] + (T, o.shape[-1]), dt) for o, dt in zip(out_avals, out_dtypes)]
    n_in = nx + nh + nct + np_ + len(cfs)

    def body(*refs):
        vals = [r[...] for r in refs[:n_in]]
        xv, hv, rest = vals[:nx], vals[nx:nx + nh], vals[nx + nh:]
        ctv, pv, cfv = rest[:nct], rest[nct:nct + np_], rest[nct + np_:]
        prev = [_prev_rows(x, h) for x, h in zip(xv, hv)]
        res = f(*xv, *prev, *pv, *ctv, *cfv)
        for o_ref, r in zip(refs[n_in:], res):
            o_ref[...] = r.astype(o_ref.dtype)

    return pl.pallas_call(
        body, grid=(T // tile,),
        in_specs=([_tok_spec(x, tile) for x in xs] + [_halo_spec(x, tile) for x in halos] + [_tok_spec(c, tile) for c in cts]
                  + [_full_spec(p) for p in ps + cfs]),
        out_specs=[_tok_spec(o, tile) for o in out_shape], out_shape=out_shape,
        compiler_params=_cparams(("parallel",)), name=name)(*[_arr(a) for a in xs + halos + cts + ps + cfs])


def stage_bwd(f, xs, ps, cts, cfs, gs, tile, name, bf16_copies=(), with_prev=False):
    xs, ps, cts, cfs = list(xs), list(ps), list(cts), list(cfs)
    gs = [list(g) if isinstance(g, (list, tuple)) else [g] for g in gs]
    g_flat = [a for g in gs for a in g]
    halos = xs if with_prev else []
    nx, nh, nct, ng, np_ = len(xs), len(halos), len(cts), len(g_flat), len(ps)
    T = _shape(xs[0])[-2]
    dx_like = xs + halos
    out_shape = ([jax.ShapeDtypeStruct(_shape(x), F32) for x in dx_like] + [jax.ShapeDtypeStruct(_shape(p), F32) for p in ps]
                 + [jax.ShapeDtypeStruct(_shape(xs[i]), BF16) for i in bf16_copies])
    n_in = nx + nh + nct + ng + np_ + len(cfs)
    ndx = nx + nh

    def body(*refs):
        vals = [r[...] for r in refs[:n_in]]
        outs = refs[n_in:]
        xv, hv, rest = vals[:nx], vals[nx:nx + nh], vals[nx + nh:]
        ctv, gparts, pv, cfv = rest[:nct], rest[nct:nct + ng], rest[nct + ng:nct + ng + np_], rest[nct + ng + np_:]
        gv = []
        for g in gs:
            gv.append(functools.reduce(lambda a, b: a + b, gparts[:len(g)]))
            gparts = gparts[len(g):]
        prev = [_prev_rows(x, h) for x, h in zip(xv, hv)]
        _, vjp = jax.vjp(lambda *xp: f(*xp, *ctv, *cfv), *xv, *prev, *pv)
        d = vjp(tuple(gv))
        for o_ref, r in zip(outs[:ndx], d[:ndx]):
            o_ref[...] = r
        for o_ref, i in zip(outs[ndx + np_:], bf16_copies):
            o_ref[...] = d[i].astype(BF16)

        @pl.when(pl.program_id(0) == 0)
        def _():
            for o_ref in outs[ndx:ndx + np_]:
                o_ref[...] = jnp.zeros_like(o_ref)

        for o_ref, r in zip(outs[ndx:ndx + np_], d[ndx:]):
            o_ref[...] += r

    plain = lambda x: jax.ShapeDtypeStruct(_shape(x), F32)
    res = pl.pallas_call(
        body, grid=(T // tile,),
        in_specs=([_tok_spec(x, tile) for x in xs] + [_halo_spec(x, tile) for x in halos]
                  + [_tok_spec(c, tile) for c in cts + g_flat] + [_full_spec(p) for p in ps + cfs]),
        out_specs=([_tok_spec(plain(x), tile) for x in dx_like] + [_full_spec(plain(p)) for p in ps]
                   + [_tok_spec(plain(xs[i]), tile) for i in bf16_copies]), out_shape=out_shape,
        compiler_params=_cparams(("arbitrary",)), name=name)(*[_arr(a) for a in xs + halos + cts + g_flat + ps + cfs])
    if bf16_copies:
        return list(res[:ndx]), list(res[ndx:ndx + np_]), list(res[ndx + np_:])
    return list(res[:ndx]), list(res[ndx:])


def _rms(x, g, eps=RMS_EPS):
    return x * lax.rsqrt(jnp.mean(x * x, axis=-1, keepdims=True) + eps) * g


def f_rmsnorm(x, g):
    return (_rms(x, g),)


def f_rmsnorm_res(x, g):
    return _rms(x, g), x


def f_rmsnorm2(x, g1, g2):
    n = x * lax.rsqrt(jnp.mean(x * x, axis=-1, keepdims=True) + RMS_EPS)
    return n * g1, n * g2


def f_rmsnorm2_res(x, g1, g2):
    return f_rmsnorm2(x, g1, g2) + (x,)


def _sigmoid(x):
    return 1.0 / (1.0 + jnp.exp(-x))


def _softplus(x):
    return jnp.maximum(x, 0.0) + jnp.log(1.0 + jnp.exp(-jnp.abs(x)))


def f_rwkv_pre(pr, pk, pv, pl_, qr, qk, qv, ql, mu_r, mu_k, mu_v, mu_l, w0, w2, a0, a2, g2, k_k, k_a, seg, seg_t):
    xr = pr + (qr - pr) * mu_r
    xk = pk + (qk - pk) * mu_k
    xv = pv + (qv - pv) * mu_v
    xl = pl_ + (ql - pl_) * mu_l
    w_log = -_softplus(-(w0 + mm_nn(jnp.tanh(xl), w2))) - 0.5
    lw = -jnp.exp(w_log)
    a = _sigmoid(a0 + mm_nn(xl, a2))
    g = mm_nn(_sigmoid(xl), g2)
    kkr = xk * k_k
    inv = lax.rsqrt(jnp.maximum(mmh(kkr * kkr, seg), 1e-24))
    kk = kkr * mmh(inv, seg_t)
    k2 = xk * (1.0 + (a - 1.0) * k_a)
    return xr, lw, k2, xv, kk, kk * a, g


def f_rwkv_post(y, r, k2, v, g, lnx_w, lnx_b, r_k, seg, seg_t):
    inv_n = 1.0 / HEAD_DIM
    m = mmh(mmh(y, seg) * inv_n, seg_t)
    yc = y - m
    rstd = lax.rsqrt(mmh(yc * yc, seg) * inv_n + LNX_EPS)
    yn = yc * mmh(rstd, seg_t) * lnx_w + lnx_b
    bonus = mmh(mmh(r * k2 * r_k, seg), seg_t) * v
    return ((yn + bonus) * g,)


def _headnorm(z, g, seg, seg_t):
    ms = mmh(z * z, seg) * (1.0 / HEAD_DIM)
    return z * mmh(lax.rsqrt(ms + RMS_EPS), seg_t) * g


def f_headnorm(z, g, seg, seg_t):
    return (_headnorm(z, g, seg, seg_t),)


def _rot_half(z):
    w = z.shape[1]
    half = HEAD_DIM // 2
    lane = lax.broadcasted_iota(jnp.int32, (1, w), 1)
    return jnp.where((lane & (HEAD_DIM - 1)) < half, -pltpu.roll(z, w - half, 1), pltpu.roll(z, half, 1))


@jax.custom_vjp
def _rotate_half(z):
    return _rot_half(z)


_rotate_half.defvjp(lambda z: (_rot_half(z), None), lambda _, g: (-_rot_half(g),))


def f_qkprep(z, g, cos, sin, seg, seg_t):
    zn = _headnorm(z, g, seg, seg_t)
    return (zn * cos + _rotate_half(zn) * sin,)


def _head_mask(width, h):
    lane = lax.broadcasted_iota(jnp.int32, (1, width), 1)
    return jnp.where((lane >> 6) == h, jnp.ones((), F32), 0.0)


def f_memattn(q, k, v, q_norm, seg, seg_t):
    qn = _headnorm(q, q_norm, seg, seg_t)
    out = jnp.zeros_like(q)
    for h in range(MEM_HEADS):
        m = _head_mask(MEM_WIDTH, h)
        s = mm_nt(qn * m, k) * (1.0 / math.sqrt(HEAD_DIM))
        s = s - jnp.max(s, axis=-1, keepdims=True)
        p = jnp.exp(s)
        p = p / jnp.sum(p, axis=-1, keepdims=True)
        out = out + mm_nn(p, v) * m
    return (out,)


def f_mix(o1, o2, o3, l1, l2, l3):
    mx = jnp.maximum(jnp.maximum(l1, l2), l3)
    e1, e2, e3 = jnp.exp(l1 - mx), jnp.exp(l2 - mx), jnp.exp(l3 - mx)
    return ((e1 * o1 + e2 * o2 + e3 * o3) / (e1 + e2 + e3),)


def _chunk_masks(L):
    t = lax.broadcasted_iota(jnp.int32, (L, L), 0)
    s = lax.broadcasted_iota(jnp.int32, (L, L), 1)
    return t, s


def _unit_lower_inverse(a):
    L = a.shape[-1]
    t, s = _chunk_masks(L)
    one = jnp.ones((), F32)
    blk = lambda sh: jnp.where((t >> sh) == (s >> sh), one, 0.0)
    n0 = a * blk(3)
    x = jnp.where(t == s, one, 0.0) - n0
    n2 = mmh(n0, n0)
    x = x + mmh(x, n2)
    x = x + mmh(x, mmh(n2, n2))
    for sh in (3, 4, 5):
        if (1 << sh) >= L:
            break
        off = a * (blk(sh + 1) - blk(sh))
        x = x - mmh(x, mmh(off, x))
    return x


@jax.custom_vjp
def _inverse_known(a, x):
    return x


def _inverse_known_fwd(a, x):
    return x, x


def _inverse_known_bwd(x, dx):
    return -mmh_nt(mmh_tn(x, dx), x), jnp.zeros_like(x)


_inverse_known.defvjp(_inverse_known_fwd, _inverse_known_bwd)


def _running_sum(x, reverse):
    L = x.shape[1]
    pos = lax.broadcasted_iota(jnp.int32, (1, L, 1), 1)
    step = 1
    while step < L:
        if reverse:
            x = x + jnp.where(pos < L - step, pltpu.roll(x, L - step, 1), 0.0)
        else:
            x = x + jnp.where(pos >= step, pltpu.roll(x, step, 1), 0.0)
        step *= 2
    return x


@jax.custom_vjp
def _cumsum_tokens(x):
    return _running_sum(x, False)


_cumsum_tokens.defvjp(lambda x: (_running_sum(x, False), None), lambda _, g: (_running_sum(g, True),))


def f_rwkv_chunk(s0, r, lw, k, v, kk, b, x_known=None):
    H, L, _ = r.shape
    t, s = _chunk_masks(L)
    one = jnp.ones((), F32)
    incl = jnp.where(t >= s, one, 0.0)
    strict = jnp.where(t > s, one, 0.0)
    cum = _cumsum_tokens(lw)
    w_in = jnp.exp(cum)
    w_ex = jnp.exp(cum - lw)
    w_inv = jnp.exp(-cum)
    rt, kkt, kt, bt = r * w_in, kk * w_ex, k * w_inv, b * w_inv
    a_b = mmh_nt(kkt, bt) * strict
    a_k = mmh_nt(kkt, kt) * strict
    m_k = mmh_nt(rt, kt) * incl
    m_b = mmh_nt(rt, bt) * incl
    x = _unit_lower_inverse(a_b) if x_known is None else _inverse_known(a_b, x_known)
    u = mmh(x, mmh_nt(kkt, s0) + mmh(a_k, v))
    y = mmh_nt(rt, s0) + mmh(m_k, v) - mmh(m_b, u)
    w_last = jnp.exp(jnp.sum(lw, axis=1, keepdims=True))
    s1 = (s0 + mmh_tn(v, kt) - mmh_tn(u, bt)) * w_last
    return y, s1, x


def _ex_split(ex, refs, n_in, n_out):
    n = ex.n
    ins, ex_in = refs[:n_in], refs[n_in:n_in + n]
    outs, ex_out = refs[n_in + n:n_in + n + n_out], refs[n_in + n + n_out:n_in + 2 * n + n_out]
    rest = refs[n_in + 2 * n + n_out:]
    return ins, outs, rest[:len(rest) - 3], (ex_in, ex_out) + tuple(rest[len(rest) - 3:])


def _split_heads(x):
    return jnp.stack([x[:, h * HEAD_DIM:(h + 1) * HEAD_DIM] for h in range(x.shape[1] // HEAD_DIM)], axis=0)


def _merge_heads(x):
    return jnp.concatenate([x[h] for h in range(x.shape[0])], axis=1)


def rwkv_scan_fwd(r, lw, k, v, kk, b, ex):
    T, N = r.shape[0], HEAD_DIM
    H = r.shape[1] // N
    groups = SCAN_GROUPS_FWD
    nc, hg = T // CHUNK, H // groups
    seq = pl.BlockSpec((CHUNK, hg * N), lambda g, c: (c, g))

    def body(*refs):
        (r_ref, lw_ref, k_ref, v_ref, kk_ref, b_ref), (y_ref, hs_ref, xs_ref), (h_scr,), ex_refs = _ex_split(ex, refs, 6, 3)
        g, c = pl.program_id(0), pl.program_id(1)

        @pl.when(jnp.logical_and(g == 0, c == 0))
        def _():
            ex.start(*ex_refs)

        @pl.when(c == 0)
        def _():
            h_scr[...] = jnp.zeros_like(h_scr)

        h0 = h_scr[...]
        hs_ref[0] = h0
        y, h1, x = f_rwkv_chunk(h0, *[_split_heads(z[...]) for z in (r_ref, lw_ref, k_ref, v_ref, kk_ref, b_ref)])
        y_ref[...] = _merge_heads(y)
        xs_ref[0] = x
        h_scr[...] = h1

        @pl.when(jnp.logical_and(g == groups - 1, c == (3 * nc) // 4))
        def _():
            ex.forward(*ex_refs)

        @pl.when(jnp.logical_and(g == groups - 1, c == nc - 1))
        def _():
            ex.wait(*ex_refs)

    res = pl.pallas_call(
        body, grid=(groups, nc), in_specs=[seq] * 6 + [_ANY] * ex.n,
        out_specs=[seq, pl.BlockSpec((1, hg, N, N), lambda g, c: (c, g, 0, 0)),
                   pl.BlockSpec((1, hg, CHUNK, CHUNK), lambda g, c: (c, g, 0, 0))] + [_ANY] * ex.n,
        out_shape=[jax.ShapeDtypeStruct((T, H * N), F32), jax.ShapeDtypeStruct((nc, H, N, N), F32),
                   jax.ShapeDtypeStruct((nc, H, CHUNK, CHUNK), F32)] + ex.out_shape(),
        scratch_shapes=[pltpu.VMEM((hg, N, N), F32)] + ex.scratch(),
        compiler_params=_cparams(("arbitrary", "arbitrary")), name="rwkv_scan_fwd")(r, lw, k, v, kk, b, *ex.operands())
    return res[0], (res[1], res[2]), list(res[3:])


def rwkv_scan_bwd(r, lw, k, v, kk, b, saved, dy, ex):
    T, N = r.shape[0], HEAD_DIM
    H = r.shape[1] // N
    groups = SCAN_GROUPS_BWD
    nc, hg = T // CHUNK, H // groups
    seq = pl.BlockSpec((CHUNK, hg * N), lambda g, c: (nc - 1 - c, g))
    state = pl.BlockSpec((1, hg, N, N), lambda g, c: (nc - 1 - c, g, 0, 0))

    def body(*refs):
        (r_ref, lw_ref, k_ref, v_ref, kk_ref, b_ref, hs_ref, xs_ref, dy_ref), outs, (dh_scr,), ex_refs = _ex_split(ex, refs, 9, 6)
        g, c = pl.program_id(0), pl.program_id(1)

        @pl.when(jnp.logical_and(g == 0, c == 0))
        def _():
            ex.start(*ex_refs)

        @pl.when(c == 0)
        def _():
            dh_scr[...] = jnp.zeros_like(dh_scr)

        x_known = xs_ref[0]
        _, vjp = jax.vjp(lambda *a: f_rwkv_chunk(*a, x_known=x_known)[:2], hs_ref[0],
                         *[_split_heads(z[...]) for z in (r_ref, lw_ref, k_ref, v_ref, kk_ref, b_ref)])
        d = vjp((_split_heads(dy_ref[...]), dh_scr[...]))
        dh_scr[...] = d[0]
        for o_ref, dz in zip(outs, d[1:]):
            o_ref[...] = _merge_heads(dz)

        @pl.when(jnp.logical_and(g == groups - 1, c == nc - 1))
        def _():
            ex.forward(*ex_refs)
            ex.wait(*ex_refs)

    res = pl.pallas_call(
        body, grid=(groups, nc),
        in_specs=[seq] * 6 + [state, state, seq] + [_ANY] * ex.n,
        out_specs=[seq] * 6 + [_ANY] * ex.n, out_shape=[jax.ShapeDtypeStruct((T, H * N), F32)] * 6 + ex.out_shape(),
        scratch_shapes=[pltpu.VMEM((hg, N, N), F32)] + ex.scratch(),
        compiler_params=_cparams(("arbitrary", "arbitrary")), name="rwkv_scan_bwd")(r, lw, k, v, kk, b, *saved, dy, *ex.operands())
    return list(res[:6]), list(res[6:])


GROUP_COLS = 4 * HEAD_DIM


def _f_dilattn(has_prev, q, kc, kp, vc, vp):
    scale = 1.0 / math.sqrt(HEAD_DIM)
    i = lax.broadcasted_iota(jnp.int32, (DIL_BLOCK, DIL_BLOCK), 0)
    j = lax.broadcasted_iota(jnp.int32, (DIL_BLOCK, DIL_BLOCK), 1)
    o, l = jnp.zeros_like(q), jnp.zeros_like(q)
    for h in range(q.shape[1] // HEAD_DIM):
        m = _head_mask(q.shape[1], h)
        sc = jnp.where(j <= i, mm_nt(q * m, kc) * scale, NEG_INF)
        sp = jnp.where(jnp.logical_and(i <= j, has_prev), mm_nt(q * m, kp) * scale, NEG_INF)
        mx = jnp.maximum(jnp.max(sc, axis=-1, keepdims=True), jnp.max(sp, axis=-1, keepdims=True))
        pc, pp = jnp.exp(sc - mx), jnp.exp(sp - mx)
        den = jnp.sum(pc, axis=-1, keepdims=True) + jnp.sum(pp, axis=-1, keepdims=True)
        o = o + (mm_nn(pc, vc) + mm_nn(pp, vp)) / den * m
        l = l + (mx + jnp.log(den)) * m
    return o, l


def _dil_specs(gi, d):
    parts = 1 if d == 1 else 2
    blk = (DIL_BLOCK * d, GROUP_COLS // parts)
    at = lambda col: (lambda p, n: (n, col * parts + p))
    before = lambda col: (lambda p, n: (jnp.maximum(n - 1, 0), col * parts + p))
    v0 = DIL_WIDTH // GROUP_COLS + gi
    q = pl.BlockSpec(blk, at(gi))
    kc, kp = pl.BlockSpec(blk, at(gi)), pl.BlockSpec(blk, before(gi))
    vc, vp = pl.BlockSpec(blk, at(v0)), pl.BlockSpec(blk, before(v0))
    out = pl.BlockSpec(blk, at(0))
    together = min(d, 2)
    return (q, kc, kp, vc, vp, out), parts, together


def _residue_rows(r, d):
    return pl.ds(r, DIL_BLOCK, stride=d) if d > 1 else pl.ds(0, DIL_BLOCK)


def dil_fwd(q, k, kv, gi, d, name):
    T = q.shape[0]
    (qs, kc, kp, vc, vp, out), parts, together = _dil_specs(gi, d)

    def body(q_ref, kc_ref, kp_ref, vc_ref, vp_ref, o_ref, l_ref):
        has_prev = pl.program_id(1) > 0

        def residues(it, carry):
            rows = [_residue_rows(it * together + a, d) for a in range(together)]
            ins = [[ref[rw, :] for ref in (q_ref, kc_ref, kp_ref, vc_ref, vp_ref)] for rw in rows]
            res = [_f_dilattn(has_prev, *x) for x in ins]
            for rw, (o, l) in zip(rows, res):
                o_ref[rw, :] = o
                l_ref[rw, :] = l
            return carry

        lax.fori_loop(0, d // together, residues, 0)

    shape = jax.ShapeDtypeStruct((T, 4 * HEAD_DIM), F32)
    return pl.pallas_call(
        body, grid=(parts, T // (DIL_BLOCK * d)), in_specs=[qs, kc, kp, vc, vp], out_specs=[out, out], out_shape=[shape, shape],
        compiler_params=_cparams(("parallel", "parallel")), name=name)(q, k, k, kv, kv)


def dil_bwd(q, k, kv, do, dl, gi, d, name):
    T = q.shape[0]
    (qs, kc, kp, vc, vp, out), parts, together = _dil_specs(gi, d)

    def body(q_ref, kc_ref, kp_ref, vc_ref, vp_ref, do_ref, dl_ref, *outs):
        f = functools.partial(_f_dilattn, pl.program_id(1) > 0)

        def residues(it, carry):
            rows = [_residue_rows(it * together + a, d) for a in range(together)]
            ins = [[ref[rw, :] for ref in (q_ref, kc_ref, kp_ref, vc_ref, vp_ref, do_ref, dl_ref)] for rw in rows]
            res = [jax.vjp(f, *x[:5])[1]((x[5], x[6])) for x in ins]
            for rw, gs in zip(rows, res):
                for o_ref, g in zip(outs, gs):
                    o_ref[rw, :] = g
            return carry

        lax.fori_loop(0, d // together, residues, 0)

    shape = jax.ShapeDtypeStruct((T, 4 * HEAD_DIM), F32)
    dq, dkc, dkp, dvc, dvp = pl.pallas_call(
        body, grid=(parts, T // (DIL_BLOCK * d)), in_specs=[qs, kc, kp, vc, vp, out, out], out_specs=[out] * 5, out_shape=[shape] * 5,
        compiler_params=_cparams(("parallel", "parallel")), name=name)(q, k, k, kv, kv, do, dl)

    def own_plus_next(c, p):
        return c + jnp.concatenate([p[DIL_BLOCK * d:], jnp.zeros_like(p[:DIL_BLOCK * d])], axis=0)

    return dq, own_plus_next(dkc, dkp), own_plus_next(dvc, dvp)


CONV_TILE = 128


def _conv3(u, h6, h7, w, b):
    rows = lax.broadcasted_iota(jnp.int32, (u.shape[0], 1), 0)
    s1 = jnp.where(rows == 0, h7, pltpu.roll(u, 1, 0))
    s2 = jnp.where(rows == 0, h6, jnp.where(rows == 1, h7, pltpu.roll(u, 2, 0)))
    return b + w[0:1] * s2 + w[1:2] * s1 + w[2:3] * u, s1, s2


def _conv_halves(u_ref, h_ref, cw_ref, cb_ref):
    F = D_FF
    first = pl.program_id(0) > 0
    res = []
    for lo in (0, F):
        h = h_ref[:, lo:lo + F]
        h6 = jnp.where(first, h[6:7], 0.0)
        h7 = jnp.where(first, h[7:8], 0.0)
        u = u_ref[:, lo:lo + F]
        res.append((u,) + _conv3(u, h6, h7, cw_ref[:, lo:lo + F], cb_ref[:, lo:lo + F]))
    return res


def _halo_before(C):
    return pl.BlockSpec((8, C), lambda i: (jnp.maximum(i * (CONV_TILE // 8) - 1, 0), 0))


def convgate_fwd(u, cw, cb, name):
    T, C = u.shape
    F = C // 2

    def body(u_ref, h_ref, cw_ref, cb_ref, z_ref):
        (_, cg, _, _), (_, cv, _, _) = _conv_halves(u_ref, h_ref, cw_ref, cb_ref)
        z_ref[...] = (cg * _sigmoid(cg) * cv).astype(BF16)

    return pl.pallas_call(
        body, grid=(T // CONV_TILE,),
        in_specs=[pl.BlockSpec((CONV_TILE, C), lambda i: (i, 0)), _halo_before(C), _full_spec(cw), _full_spec(cb)],
        out_specs=pl.BlockSpec((CONV_TILE, F), lambda i: (i, 0)), out_shape=jax.ShapeDtypeStruct((T, F), BF16),
        compiler_params=_cparams(("parallel",)), name=name)(u, u, cw, cb)


def convgate_bwd(u, cw, cb, dz, name):
    T, C = u.shape
    F = C // 2
    n = T // CONV_TILE
    E = CONV_TILE + 8

    def body(u_ref, hb_ref, ha_ref, cw_ref, cb_ref, dz_ref, dza_ref, du_ref, dcw_ref, dcb_ref):
        i = pl.program_id(0)
        dze = jnp.concatenate([dz_ref[...], jnp.where(i < n - 1, dza_ref[...], 0.0)], axis=0)

        @pl.when(i == 0)
        def _():
            dcw_ref[...] = jnp.zeros_like(dcw_ref)
            dcb_ref[...] = jnp.zeros_like(dcb_ref)

        halves = []
        for lo in (0, F):
            sl = slice(lo, lo + F)
            hb = hb_ref[:, sl]
            ue = jnp.concatenate([u_ref[:, sl], ha_ref[:, sl]], axis=0)
            c, s1, s2 = _conv3(ue, jnp.where(i > 0, hb[6:7], 0.0), jnp.where(i > 0, hb[7:8], 0.0), cw_ref[:, sl], cb_ref[:, sl])
            halves.append((sl, ue, c, s1, s2))
        (_, _, cg, _, _), (_, _, cv, _, _) = halves
        sg = _sigmoid(cg)
        dcs = (dze * cv * sg * (1.0 + cg * (1.0 - sg)), dze * cg * sg)
        for (sl, ue, _, s1, s2), dc in zip(halves, dcs):
            own = lambda z: z[:CONV_TILE]
            dcb_ref[:, sl] += jnp.sum(own(dc), axis=0, keepdims=True)
            dcw_ref[0:1, sl] += jnp.sum(own(dc * s2), axis=0, keepdims=True)
            dcw_ref[1:2, sl] += jnp.sum(own(dc * s1), axis=0, keepdims=True)
            dcw_ref[2:3, sl] += jnp.sum(own(dc * ue), axis=0, keepdims=True)
            du = cw_ref[2:3, sl] * dc + cw_ref[1:2, sl] * pltpu.roll(dc, E - 1, 0) + cw_ref[0:1, sl] * pltpu.roll(dc, E - 2, 0)
            du_ref[:, sl] = own(du).astype(BF16)

    after = lambda w: pl.BlockSpec((8, w), lambda i: (jnp.minimum((i + 1) * (CONV_TILE // 8), T // 8 - 1), 0))
    return pl.pallas_call(
        body, grid=(n,),
        in_specs=[pl.BlockSpec((CONV_TILE, C), lambda i: (i, 0)), _halo_before(C), after(C), _full_spec(cw), _full_spec(cb),
                  pl.BlockSpec((CONV_TILE, F), lambda i: (i, 0)), after(F)],
        out_specs=[pl.BlockSpec((CONV_TILE, C), lambda i: (i, 0)), _full_spec(cw), _full_spec(cb)],
        out_shape=[jax.ShapeDtypeStruct((T, C), BF16), jax.ShapeDtypeStruct(cw.shape, F32), jax.ShapeDtypeStruct(cb.shape, F32)],
        compiler_params=_cparams(("arbitrary",)), name=name)(u, u, u, cw, cb, dz, dz)


def loss_head(y, tgt):
    T, D = y.shape
    tile = 256

    def body(y_ref, t_ref, l_ref, d_ref, db_ref):
        d = y_ref[...] - t_ref[...]
        d_ref[...] = d * (1.0 / D)
        db_ref[...] = (d * (1.0 / D)).astype(BF16)

        @pl.when(pl.program_id(0) == 0)
        def _():
            l_ref[...] = jnp.zeros_like(l_ref)

        l_ref[...] += (0.5 / D) * jnp.sum(d * d)

    row = pl.BlockSpec((tile, D), lambda i: (i, 0))
    return pl.pallas_call(
        body, grid=(T // tile,), in_specs=[row, row], out_specs=[pl.BlockSpec((8, 128), lambda i: (0, 0)), row, row],
        out_shape=[jax.ShapeDtypeStruct((8, 128), F32), jax.ShapeDtypeStruct((T, D), F32), jax.ShapeDtypeStruct((T, D), BF16)],
        compiler_params=_cparams(("arbitrary",)), name="loss_head")(y, tgt)


def sum_parts(parts, name):
    S, R, C = parts.shape
    tile = _pick(R, (256, 128, 64, 32, 16, 8))

    def body(p_ref, o_ref):
        acc = p_ref[0]
        for s in range(1, S):
            acc = acc + p_ref[s]
        o_ref[...] = acc

    return pl.pallas_call(
        body, grid=(R // tile,), in_specs=[pl.BlockSpec((S, tile, C), lambda i: (0, i, 0))],
        out_specs=pl.BlockSpec((tile, C), lambda i: (i, 0)), out_shape=jax.ShapeDtypeStruct((R, C), F32),
        compiler_params=_cparams(("parallel",)), name=name)(parts)


def adamw(gparts, w, m, v, name):
    S, R, C = gparts.shape
    tile = _pick(R, (256, 128, 64, 32, 16, 8))
    c1 = 1.0 / (1.0 - ADAM_B1 ** ADAM_STEP)
    c2 = 1.0 / (1.0 - ADAM_B2 ** ADAM_STEP)

    def body(g_ref, w_ref, m_ref, v_ref, go_ref, d_ref, mo_ref, vo_ref):
        g = g_ref[0].astype(F32)
        for s in range(1, S):
            g = g + g_ref[s].astype(F32)
        m1 = ADAM_B1 * m_ref[...] + (1.0 - ADAM_B1) * g
        v1 = ADAM_B2 * v_ref[...] + (1.0 - ADAM_B2) * (g * g)
        go_ref[...] = g
        mo_ref[...] = m1
        vo_ref[...] = v1
        d_ref[...] = -ADAM_LR * ((m1 * c1) / (jnp.sqrt(v1 * c2) + ADAM_EPS) + ADAM_WD * w_ref[...])

    row = pl.BlockSpec((tile, C), lambda i: (i, 0))
    return pl.pallas_call(
        body, grid=(R // tile,), in_specs=[pl.BlockSpec((S, tile, C), lambda i: (0, i, 0)), row, row, row],
        out_specs=[row] * 4, out_shape=[jax.ShapeDtypeStruct((R, C), F32)] * 4,
        compiler_params=_cparams(("parallel",)), name=name)(gparts, w, m, v)


def _peers():
    x, y, c = lax.axis_index("x"), lax.axis_index("y"), lax.axis_index("c")
    peers = []
    for k in range(1, N_DEV):
        px = 1 - x if k & 4 else x
        py = 1 - y if k & 2 else y
        pc = 1 - c if k & 1 else c
        peers.append(((px, py, pc), 4 * px + 2 * py + pc))
    return 4 * x + 2 * y + c, peers


_ANY = pl.BlockSpec(memory_space=pl.ANY)


class Exchange:
    def __init__(self, gathers=(), scatters=()):
        self.gathers, self.scatters = list(gathers), list(scatters)
        self.n = len(self.gathers) + len(self.scatters)

    def operands(self):
        return self.gathers + self.scatters

    def out_shape(self):
        return ([jax.ShapeDtypeStruct((N_DEV,) + x.shape, x.dtype) for x in self.gathers]
                + [jax.ShapeDtypeStruct(x.shape, x.dtype) for x in self.scatters])

    def scratch(self):
        n = max(self.n, 1)
        return [pltpu.SemaphoreType.DMA((7 * n,)), pltpu.SemaphoreType.DMA((7 * n,)), pltpu.SemaphoreType.DMA((n,))]

    def _copies(self, in_refs, out_refs, send_sems, recv_sems, local_sems):
        me, peers = _peers()
        ng = len(self.gathers)
        local, sends, recvs = [], [], []
        for a in range(self.n):
            x, o = in_refs[a], out_refs[a]
            mine = x if a < ng else x.at[me]
            local.append(pltpu.make_async_copy(mine, o.at[me], local_sems.at[a]))
            s_a, r_a = {}, {}
            for k in range(1, N_DEV):
                peer, slot = peers[k - 1]
                sems = dict(send_sem=send_sems.at[7 * a + k - 1], recv_sem=recv_sems.at[7 * a + k - 1],
                            device_id_type=pl.DeviceIdType.MESH)
                if a >= ng:
                    s_a[k] = pltpu.make_async_remote_copy(src_ref=x.at[slot], dst_ref=o.at[me], device_id=peer, **sems)
                elif k in FORWARDED:
                    came = o.at[peers[k - 2][1]]
                    s_a[k] = pltpu.make_async_remote_copy(src_ref=came, dst_ref=came, device_id=peers[0][0], **sems)
                else:
                    s_a[k] = pltpu.make_async_remote_copy(src_ref=x, dst_ref=o.at[me], device_id=peer, **sems)
                r_a[k] = pltpu.make_async_remote_copy(src_ref=mine, dst_ref=o.at[slot], device_id=peer, **sems)
            sends.append(s_a)
            recvs.append(r_a)
        return local, sends, recvs

    def start(self, *refs):
        if self.n == 0:
            return
        local, sends, _ = self._copies(*refs)
        for a in range(self.n):
            local[a].start()
            for k in range(1, N_DEV):
                if a >= len(self.gathers) or k not in FORWARDED:
                    sends[a][k].start()

    def forward(self, *refs):
        if not self.gathers:
            return
        _, sends, recvs = self._copies(*refs)
        for a in range(len(self.gathers)):
            for k in FORWARDED:
                recvs[a][k - 1].wait_recv()
                sends[a][k].start()

    def wait(self, *refs):
        if self.n == 0:
            return
        local, sends, recvs = self._copies(*refs)
        for a in range(self.n):
            waited_early = [f - 1 for f in FORWARDED] if a < len(self.gathers) else []
            for k in range(1, N_DEV):
                if k not in waited_early:
                    recvs[a][k].wait_recv()
            for k in range(1, N_DEV):
                sends[a][k].wait_send()
            local[a].wait()


FORWARDED = (3, 5, 7)


def exchange(ex, name):
    n = ex.n

    def body(*refs):
        args = (refs[:n], refs[n:2 * n]) + tuple(refs[2 * n:])
        ex.start(*args)
        ex.forward(*args)
        ex.wait(*args)

    return pl.pallas_call(body, in_specs=[_ANY] * n, out_specs=[_ANY] * n, out_shape=ex.out_shape(),
                          scratch_shapes=ex.scratch(), name=name)(*ex.operands())


def _heads(z, h):
    return z.reshape(z.shape[0], h, HEAD_DIM).transpose(1, 0, 2)


def _unheads(z):
    return z.transpose(1, 0, 2).reshape(z.shape[1], z.shape[0] * HEAD_DIM)


def _shift_up(z):
    return jnp.concatenate([z[1:], jnp.zeros_like(z[:1])], axis=0)


def _segments(width):
    seg = np.zeros((width, 128), np.float32)
    seg[np.arange(width), np.arange(width) // HEAD_DIM] = 1.0
    return jnp.asarray(seg), jnp.asarray(seg.T)


def _rope_consts(T, heads):
    inv = ROPE_THETA ** (-jnp.arange(0, HEAD_DIM, 2, dtype=F32) / HEAD_DIM)
    ang = jnp.arange(T, dtype=F32)[:, None] * inv[None, :]
    return jnp.tile(jnp.cos(ang), (1, 2 * heads)), jnp.tile(jnp.sin(ang), (1, 2 * heads))


def _per_head(g, heads):
    return jnp.tile(g.reshape(1, HEAD_DIM), (1, heads))


def _sum_heads(g):
    return g.reshape(-1, HEAD_DIM).sum(axis=0, keepdims=True)


LORA_COLS = 256
RW_TILE = 128
ROW_TILE = 256


def _local_step(x0, memx, tgt, P, ex_weights=None, weights_done=None, ex_grads=None):
    T = x0.shape[0]
    P = dict(P)
    G = {}
    seg, seg_t = _segments(RWKV_WIDTH)
    mseg = (seg[:MEM_WIDTH], seg_t[:, :MEM_WIDTH])
    cos, sin = _rope_consts(T, DIL_WIDTH // HEAD_DIM)
    row = lambda v: v.reshape(1, -1)

    def mem_fwd(i, q):
        memn = stage_fwd(f_rmsnorm, [memx], [P["mem_norm"][i:i + 1]], [], [], N_MEM, f"mem{i}_norm", [BF16])[0]
        kvm = matmul(memn, P["mem_w_kv"][i], "nn", f"mem{i}_kv")
        kn, qn = _per_head(P["mem_k_norm"][i], MEM_HEADS), _per_head(P["mem_q_norm"][i], MEM_HEADS)
        km = stage_fwd(f_headnorm, [Cols(kvm, MEM_WIDTH, 0)], [kn], [], mseg, N_MEM, f"mem{i}_knorm")[0]
        om = stage_fwd(f_memattn, [q], [km, Cols(kvm, MEM_WIDTH, 1), qn], [], mseg, ROW_TILE, f"mem{i}_attn")[0]
        return om, (memn, kvm, km, kn, qn, q)

    def mem_bwd(i, saved, dymem):
        memn, kvm, km, kn, qn, q = saved
        (dq,), (dkm, dvm, g_qn) = stage_bwd(f_memattn, [q], [km, Cols(kvm, MEM_WIDTH, 1), qn], [], mseg, [dymem], ROW_TILE,
                                            f"mem{i}_attn_bwd")
        (dkraw,), (g_kn,) = stage_bwd(f_headnorm, [Cols(kvm, MEM_WIDTH, 0)], [kn], [], mseg, [dkm], N_MEM, f"mem{i}_knorm_bwd")
        dkvm = jnp.concatenate([dkraw, dvm], axis=1).astype(BF16)
        g_w = matmul(memn, dkvm, "tn", f"mem{i}_kv_dw")
        dmemn = matmul(dkvm, P["mem_w_kv"][i], "nt", f"mem{i}_kv_dx")
        _, (g_mn,) = stage_bwd(f_rmsnorm, [memx], [P["mem_norm"][i:i + 1]], [], [], [dmemn], N_MEM, f"mem{i}_norm_bwd")
        return dq, g_mn, g_w, _sum_heads(g_qn), _sum_heads(g_kn)

    def ffn_fwd(i, xin):
        hn = stage_fwd(f_rmsnorm, [xin], [P["ffn_norm"][i:i + 1]], [], [], ROW_TILE, f"ffn{i}_norm", [BF16])[0]
        u = matmul(hn, P["ffn_w_up"][i], "nn", f"ffn{i}_up")
        z = convgate_fwd(u, P["ffn_conv_w"][i], P["ffn_conv_b"][i:i + 1], f"ffn{i}_conv")
        return matmul(z, P["ffn_w_down"][i], "nn", f"ffn{i}_down", residual=xin), (hn, u, z)

    def ffn_bwd(i, xin, saved, dxo, dxo_b):
        hn, u, z = saved
        dz = matmul(dxo_b, P["ffn_w_down"][i], "nt", f"ffn{i}_down_dx")
        g_down = matmul(z, dxo_b, "tn", f"ffn{i}_down_dw")
        du, g_cw, g_cb = convgate_bwd(u, P["ffn_conv_w"][i], P["ffn_conv_b"][i:i + 1], dz, f"ffn{i}_conv_bwd")
        dhn = matmul(du, P["ffn_w_up"][i], "nt", f"ffn{i}_up_dx")
        g_up = matmul(hn, du, "tn", f"ffn{i}_up_dw")
        (dxin,), (g_n,), (dxin_b,) = stage_bwd(f_rmsnorm_res, [xin], [P["ffn_norm"][i:i + 1]], [], [], [dhn, dxo], ROW_TILE,
                                               f"ffn{i}_norm_bwd", bf16_copies=(0,))
        return dxin, dxin_b, g_n, g_up, g_cw, g_cb, g_down

    h0 = stage_fwd(f_rmsnorm, [x0], [P["attn_norm"][0:1]], [], [], ROW_TILE, "l0_norm", [BF16])[0]
    p0 = matmul(h0, P["a_w_in"][0], "nn", "l0_in")
    lora0 = 3 * RWKV_WIDTH // LORA_COLS
    pre_xs = [Cols(p0, RWKV_WIDTH, 0), Cols(p0, RWKV_WIDTH, 1), Cols(p0, RWKV_WIDTH, 2), Cols(p0, LORA_COLS, lora0)]
    mu = [Cols(P["a_mu"], RWKV_WIDTH, 0), Cols(P["a_mu"], RWKV_WIDTH, 1), Cols(P["a_mu"], RWKV_WIDTH, 2),
          Cols(P["a_mu"], LORA_COLS, lora0)]
    lora_rows = lambda w, lo: jnp.pad(w, ((lo, LORA_COLS - lo - w.shape[0]), (0, 0)))
    pre_ps = mu + [P["a_w0"], lora_rows(P["a_w2"][0], 0), P["a_a0"], lora_rows(P["a_a2"][0], 64), lora_rows(P["a_g2"][0], 128),
                   P["a_k_k"], P["a_k_a"]]
    r, lw, k2, v, kk, b, g = stage_fwd(f_rwkv_pre, pre_xs, pre_ps, [], [seg, seg_t], RW_TILE, "l0_rwkv_pre", with_prev=True)
    scan_in = [r, lw, k2, v, kk, b]
    y_h, h_states, got = rwkv_scan_fwd(*scan_in, ex_weights or Exchange())
    if weights_done is not None:
        P.update(weights_done(got))
    y_s = y_h
    post_ps = [P["a_lnx_w"], P["a_lnx_b"], P["a_r_k"].reshape(1, RWKV_WIDTH)]
    ymix0 = stage_fwd(f_rwkv_post, [y_s, r, k2, v, g], post_ps, [], [seg, seg_t], RW_TILE, "l0_rwkv_post")[0]
    ymem0, mem0_saved = mem_fwd(0, Cols(p0, MEM_WIDTH, SHIFT_WIDTH // MEM_WIDTH))
    ycat0 = jnp.concatenate([ymix0, ymem0], axis=1).astype(BF16)
    x1 = matmul(ycat0, P["a_w_out"][0], "nn", "l0_out", residual=x0)
    x2, ffn0_saved = ffn_fwd(0, x1)

    hk, h1 = stage_fwd(f_rmsnorm2, [x2], [row(P["kv_norm"]), P["attn_norm"][1:2]], [], [], ROW_TILE, "l1_norm", [BF16, BF16])
    kvp = matmul(hk, P["kv_w"][0], "nn", "l1_kv")
    p1 = matmul(h1, P["b_w_in"][0], "nn", "l1_in")
    kraw, qraw = Cols(kvp, DIL_WIDTH, 0), Cols(p1, DIL_WIDTH, 0)
    kgain, qgain = _per_head(P["kv_k_norm"], DIL_WIDTH // HEAD_DIM), _per_head(P["b_q_norm"], DIL_WIDTH // HEAD_DIM)
    ksh = stage_fwd(f_qkprep, [kraw], [kgain], [cos, sin], [seg, seg_t], ROW_TILE, "l1_kprep")[0]
    q = stage_fwd(f_qkprep, [qraw], [qgain], [cos, sin], [seg, seg_t], ROW_TILE, "l1_qprep")[0]
    outs, lses = [], []
    for gi, (_, d) in enumerate(DIL_GROUPS):
        og, lg = dil_fwd(q, ksh, kvp, gi, d, f"l1_dil{gi}")
        outs.append(og)
        lses.append(lg)
    omix = stage_fwd(f_mix, outs + lses, [], [], [], ROW_TILE, "l1_mix")[0]
    ymem1, mem1_saved = mem_fwd(1, Cols(p1, MEM_WIDTH, DIL_WIDTH // MEM_WIDTH))
    ycat1 = jnp.concatenate([omix, ymem1], axis=1).astype(BF16)
    x3 = matmul(ycat1, P["b_w_out"][0], "nn", "l1_out", residual=x2)
    x4, ffn1_saved = ffn_fwd(1, x3)
    loss_part, dx4, dx4_b = loss_head(x4, tgt)

    dx3, dx3_b, gn1, gup1, gcw1, gcb1, gdown1 = ffn_bwd(1, x3, ffn1_saved, dx4, dx4_b)
    dycat1 = matmul(dx3_b, P["b_w_out"][0], "nt", "l1_out_dx")
    G["b_w_out"] = [matmul(ycat1, dx3_b, "tn", "l1_out_dw")]
    dqmem1, gmn1, gmw1, gmq1, gmk1 = mem_bwd(1, mem1_saved, Cols(dycat1, MEM_WIDTH, 1))
    dmix, _ = stage_bwd(f_mix, outs + lses, [], [], [], [Cols(dycat1, MEM_WIDTH, 0)], ROW_TILE, "l1_mix_bwd")
    dq, dk, dv = zip(*[dil_bwd(q, ksh, kvp, dmix[gi], dmix[3 + gi], gi, d, f"l1_dil{gi}_bwd")
                       for gi, (_, d) in enumerate(DIL_GROUPS)])
    dq, dk, dv = jnp.concatenate(dq, axis=1), jnp.concatenate(dk, axis=1), jnp.concatenate(dv, axis=1)
    (dqraw,), (g_bq,) = stage_bwd(f_qkprep, [qraw], [qgain], [cos, sin], [seg, seg_t], [dq], ROW_TILE, "l1_qprep_bwd")
    (dkraw,), (g_kk,) = stage_bwd(f_qkprep, [kraw], [kgain], [cos, sin], [seg, seg_t], [dk], ROW_TILE, "l1_kprep_bwd")
    g_bq, g_kk = _sum_heads(g_bq), _sum_heads(g_kk)
    dp1 = jnp.concatenate([dqraw, dqmem1], axis=1).astype(BF16)
    dkvp = jnp.concatenate([dkraw, dv], axis=1).astype(BF16)
    dh1 = matmul(dp1, P["b_w_in"][0], "nt", "l1_in_dx")
    G["b_w_in"] = [matmul(h1, dp1, "tn", "l1_in_dw")]
    dhk = matmul(dkvp, P["kv_w"][0], "nt", "l1_kv_dx")
    G["kv_w"] = [matmul(hk, dkvp, "tn", "l1_kv_dw")]
    (dx2,), (g_kvn, g_an1), (dx2_b,) = stage_bwd(f_rmsnorm2_res, [x2], [row(P["kv_norm"]), P["attn_norm"][1:2]], [], [],
                                                 [dhk, dh1, dx3], ROW_TILE, "l1_norm_bwd", bf16_copies=(0,))

    dx1, dx1_b, gn0, gup0, gcw0, gcb0, gdown0 = ffn_bwd(0, x1, ffn0_saved, dx2, dx2_b)
    dycat0 = matmul(dx1_b, P["a_w_out"][0], "nt", "l0_out_dx")
    G["a_w_out"] = [matmul(ycat0, dx1_b, "tn", "l0_out_dw")]
    dqmem0, gmn0, gmw0, gmq0, gmk0 = mem_bwd(0, mem0_saved, Cols(dycat0, MEM_WIDTH, RWKV_WIDTH // MEM_WIDTH))
    (dy_s, dr_a, dk_a, dv_a, dg), (g_lw, g_lb, g_rk) = stage_bwd(
        f_rwkv_post, [y_s, r, k2, v, g], post_ps, [], [seg, seg_t], [Cols(dycat0, RWKV_WIDTH, 0)], RW_TILE, "l0_rwkv_post_bwd")
    G["mem_w_kv"], G["ffn_w_up"], G["ffn_w_down"] = [gmw0, gmw1], [gup0, gup1], [gdown0, gdown1]
    (dr_b, dlw, dk_b, dv_b, dkk, db), G["_exchanged"] = rwkv_scan_bwd(*scan_in, h_states, dy_s,
                                                                      ex_grads(G) if ex_grads else Exchange())
    dpre, gpre = stage_bwd(f_rwkv_pre, pre_xs, pre_ps, [], [seg, seg_t],
                           [[dr_a, dr_b], dlw, [dk_a, dk_b], [dv_a, dv_b], dkk, db, dg], RW_TILE, "l0_rwkv_pre_bwd", with_prev=True)
    dp_rw = jnp.concatenate(dpre[:4], axis=1) + _shift_up(jnp.concatenate(dpre[4:], axis=1))
    dp0 = jnp.concatenate([dp_rw, dqmem0], axis=1).astype(BF16)
    dh0 = matmul(dp0, P["a_w_in"][0], "nt", "l0_in_dx")
    G["a_w_in"] = [matmul(h0, dp0, "tn", "l0_in_dw")]
    (dx0,), (g_an0,) = stage_bwd(f_rmsnorm_res, [x0], [P["attn_norm"][0:1]], [], [], [dh0, dx1], ROW_TILE, "l0_norm_bwd")

    G["attn_norm"] = jnp.concatenate([g_an0, g_an1], axis=0)
    G["a_mu"] = jnp.concatenate(gpre[:4], axis=1)
    G["a_w0"], G["a_w2"], G["a_a0"], G["a_a2"], G["a_g2"] = gpre[4], gpre[5][None, :64], gpre[6], gpre[7][None, 64:128], gpre[8][None, 128:]
    G["a_k_k"], G["a_k_a"] = gpre[9], gpre[10]
    G["a_r_k"] = g_rk.reshape(1, RWKV_HEADS, HEAD_DIM)
    G["a_lnx_w"], G["a_lnx_b"] = g_lw, g_lb
    G["kv_norm"], G["kv_k_norm"], G["b_q_norm"] = g_kvn.reshape(-1), g_kk.reshape(-1), g_bq
    G["mem_norm"] = jnp.concatenate([gmn0, gmn1], axis=0)
    G["mem_w_kv"] = [gmw0, gmw1]
    G["mem_q_norm"] = jnp.concatenate([gmq0, gmq1], axis=0)
    G["mem_k_norm"] = jnp.concatenate([gmk0, gmk1], axis=0)
    G["ffn_norm"] = jnp.concatenate([gn0, gn1], axis=0)
    G["ffn_w_up"] = [gup0, gup1]
    G["ffn_conv_w"] = jnp.stack([gcw0, gcw1])
    G["ffn_conv_b"] = jnp.concatenate([gcb0, gcb1], axis=0)
    G["ffn_w_down"] = [gdown0, gdown1]
    return loss_part, dx0, G


PARAMS = (("attn_norm", None), ("a_w_in", 2), ("a_mu", 1), ("a_w0", 1), ("a_w2", 2), ("a_a0", 1), ("a_a2", 2), ("a_g2", 2),
          ("a_k_k", 1), ("a_k_a", 1), ("a_r_k", None), ("a_lnx_w", 1), ("a_lnx_b", 1), ("a_w_out", 1), ("kv_norm", None),
          ("kv_w", 1), ("kv_k_norm", None), ("b_w_in", 1), ("b_q_norm", None), ("b_w_out", 2), ("mem_norm", None),
          ("mem_w_kv", 1), ("mem_q_norm", None), ("mem_k_norm", None), ("ffn_norm", None), ("ffn_w_up", 2),
          ("ffn_conv_w", 2), ("ffn_conv_b", None), ("ffn_w_down", 1))
BIG = ("a_w_in", "a_w_out", "kv_w", "b_w_in", "b_w_out", "mem_w_kv", "ffn_w_up", "ffn_w_down")
AXIS = dict(PARAMS)
SMALL = tuple(n for n, _ in PARAMS if n not in BIG)
SMALL_SHARDED = tuple(n for n in SMALL if AXIS[n] is not None)
PACK_QUANTUM = 256 * 128


def _from_shards(xs, axis):
    full = jnp.moveaxis(xs, 0, axis)
    sh = full.shape
    return full.reshape(sh[:axis] + (sh[axis] * sh[axis + 1],) + sh[axis + 2:])


def _to_shards(g, axis):
    sh = g.shape
    return jnp.moveaxis(g.reshape(sh[:axis] + (N_DEV, sh[axis] // N_DEV) + sh[axis + 1:]), axis, 0)


def _pack(parts, lead=0):
    ld = parts[0].shape[:lead]
    flat = jnp.concatenate([p.reshape(ld + (-1,)) for p in parts], axis=-1)
    pad = (-flat.shape[-1]) % PACK_QUANTUM
    flat = jnp.pad(flat, [(0, 0)] * lead + [(0, pad)])
    return flat.reshape(ld + (-1, 128))


def _unpack(packed, shapes, lead=0):
    ld = packed.shape[:lead]
    flat = packed.reshape(ld + (-1,))
    out, off = [], 0
    for s in shapes:
        n = math.prod(s)
        out.append(flat[..., off:off + n].reshape(ld + tuple(s)))
        off += n
    return out


def kernel(x, mem, attn_norm, a_w_in, a_mu, a_w0, a_w2, a_a0, a_a2, a_g2, a_k_k, a_k_a, a_r_k, a_lnx_w, a_lnx_b, a_w_out, kv_norm, kv_w, kv_k_norm, b_w_in, b_q_norm, b_w_out, mem_norm, mem_w_kv, mem_q_norm, mem_k_norm, ffn_norm, ffn_w_up, ffn_conv_w, ffn_conv_b, ffn_w_down, loss_target, m_attn_norm, m_a_w_in, m_a_mu, m_a_w0, m_a_w2, m_a_a0, m_a_a2, m_a_g2, m_a_k_k, m_a_k_a, m_a_r_k, m_a_lnx_w, m_a_lnx_b, m_a_w_out, m_kv_norm, m_kv_w, m_kv_k_norm, m_b_w_in, m_b_q_norm, m_b_w_out, m_mem_norm, m_mem_w_kv, m_mem_q_norm, m_mem_k_norm, m_ffn_norm, m_ffn_w_up, m_ffn_conv_w, m_ffn_conv_b, m_ffn_w_down, v_attn_norm, v_a_w_in, v_a_mu, v_a_w0, v_a_w2, v_a_a0, v_a_a2, v_a_g2, v_a_k_k, v_a_k_a, v_a_r_k, v_a_lnx_w, v_a_lnx_b, v_a_w_out, v_kv_norm, v_kv_w, v_kv_k_norm, v_b_w_in, v_b_q_norm, v_b_w_out, v_mem_norm, v_mem_w_kv, v_mem_q_norm, v_mem_k_norm, v_ffn_norm, v_ffn_w_up, v_ffn_conv_w, v_ffn_conv_b, v_ffn_w_down):
    names = [n for n, _ in PARAMS]
    vals = (attn_norm, a_w_in, a_mu, a_w0, a_w2, a_a0, a_a2, a_g2, a_k_k, a_k_a, a_r_k, a_lnx_w, a_lnx_b, a_w_out, kv_norm, kv_w, kv_k_norm, b_w_in, b_q_norm, b_w_out, mem_norm, mem_w_kv, mem_q_norm, mem_k_norm, ffn_norm, ffn_w_up, ffn_conv_w, ffn_conv_b, ffn_w_down)
    m_vals = (m_attn_norm, m_a_w_in, m_a_mu, m_a_w0, m_a_w2, m_a_a0, m_a_a2, m_a_g2, m_a_k_k, m_a_k_a, m_a_r_k, m_a_lnx_w, m_a_lnx_b, m_a_w_out, m_kv_norm, m_kv_w, m_kv_k_norm, m_b_w_in, m_b_q_norm, m_b_w_out, m_mem_norm, m_mem_w_kv, m_mem_q_norm, m_mem_k_norm, m_ffn_norm, m_ffn_w_up, m_ffn_conv_w, m_ffn_conv_b, m_ffn_w_down)
    v_vals = (v_attn_norm, v_a_w_in, v_a_mu, v_a_w0, v_a_w2, v_a_a0, v_a_a2, v_a_g2, v_a_k_k, v_a_k_a, v_a_r_k, v_a_lnx_w, v_a_lnx_b, v_a_w_out, v_kv_norm, v_kv_w, v_kv_k_norm, v_b_w_in, v_b_q_norm, v_b_w_out, v_mem_norm, v_mem_w_kv, v_mem_q_norm, v_mem_k_norm, v_ffn_norm, v_ffn_w_up, v_ffn_conv_w, v_ffn_conv_b, v_ffn_w_down)
    W, M, V = dict(zip(names, vals)), dict(zip(names, m_vals)), dict(zip(names, v_vals))
    me = 4 * lax.axis_index("x") + 2 * lax.axis_index("y") + lax.axis_index("c")
    layers = lambda D, n: [D[n]] if D[n].ndim == 2 else [D[n][i] for i in range(D[n].shape[0])]
    ax2 = lambda n: AXIS[n] - (W[n].ndim - 2)
    first = [("a_w_in", 0)]
    later = [(n, i) for n in BIG if n != "a_w_in" for i in range(len(layers(W, n)))]

    small_shapes = [W[n].shape for n in SMALL_SHARDED]
    got_w, got_small = exchange(Exchange(gathers=[W["a_w_in"][0].astype(BF16), _pack([W[n] for n in SMALL_SHARDED])]),
                                "gather_first")
    P = {n: W[n] for n in SMALL}
    P["a_w_in"] = [_from_shards(got_w, ax2("a_w_in"))]
    for n, s in zip(SMALL_SHARDED, _unpack(got_small, small_shapes, lead=1)):
        P[n] = _from_shards(s, AXIS[n])
    ex_weights = Exchange(gathers=[layers(W, n)[i].astype(BF16) for n, i in later])

    def weights_done(got):
        out = {}
        for (n, _), g in zip(later, got):
            out.setdefault(n, []).append(_from_shards(g, ax2(n)))
        return out

    slots = lambda G, n: jnp.stack([_to_shards(g, ax2(n)) for g in G[n]], axis=1)
    later_names = [n for n in BIG if n != "a_w_in"]
    ex_grads = lambda G: Exchange(scatters=[slots(G, n) for n in later_names])
    loss_part, dx0, G = _local_step(x[0], mem[0], loss_target[0], P, ex_weights, weights_done, ex_grads)
    loss = lax.psum(loss_part[0, 0], ("x", "y", "c"))
    gparts = dict(zip(later_names, G.pop("_exchanged")))
    got_gsmall, gparts["a_w_in"] = exchange(
        Exchange(gathers=[_pack([G[n] for n in SMALL])], scatters=[slots(G, "a_w_in")]), "exchange_last")

    results = {}
    for n in BIG:
        rows = lambda z: z.reshape((-1,) + z.shape[-1:])
        res = adamw(gparts[n].reshape((N_DEV, -1) + gparts[n].shape[-1:]), rows(W[n]), rows(M[n]), rows(V[n]), f"adamw_{n}")
        results[n] = [r.reshape(W[n].shape) for r in res]
    g_small = sum_parts(got_gsmall, "sum_small_grads")
    mine = []
    for n, g in zip(SMALL, _unpack(g_small, [G[n].shape for n in SMALL])):
        if AXIS[n] is not None:
            s = W[n].shape[AXIS[n]]
            g = lax.dynamic_slice_in_dim(g, me * s, s, axis=AXIS[n])
        mine.append(g)
    res = adamw(_pack(mine)[None], _pack([W[n] for n in SMALL]), _pack([M[n] for n in SMALL]), _pack([V[n] for n in SMALL]),
                "adamw_small")
    for n, parts in zip(SMALL, zip(*[_unpack(r, [W[n].shape for n in SMALL]) for r in res])):
        results[n] = list(parts)
    outs = [[results[n][j] for n in names] for j in range(4)]
    return (loss, dx0[None], *outs[0], *outs[1], *outs[2], *outs[3])
```

```python
import functools
import math

import jax
import jax.numpy as jnp
import numpy as np
from jax import lax
from jax.experimental import pallas as pl
from jax.experimental.pallas import tpu as pltpu

F32 = jnp.float32
BF16 = jnp.bfloat16
HI = lax.Precision.HIGHEST
H3 = lax.Precision.HIGH

N_DEV = 8
D_MODEL = 1024
HEAD_DIM = 64
N_MEM = 256
MEM_HEADS = 4
MEM_WIDTH = 256
RWKV_HEADS = 12
RWKV_WIDTH = 768
SHIFT_WIDTH = 2560
DIL_GROUPS = ((128, 1), (512, 4), (2048, 16))
DIL_BLOCK = 128
DIL_WIDTH = 768
D_FF = 2816
RMS_EPS = 1e-6
LNX_EPS = 64e-5
NEG_INF = -1e30
ROPE_THETA = 10000.0
ADAM_LR, ADAM_B1, ADAM_B2, ADAM_EPS, ADAM_WD, ADAM_STEP = 0.001, 0.9, 0.999, 1e-08, 0.01, 10

CHUNK = 64
SCAN_GROUPS_FWD, SCAN_GROUPS_BWD = 1, 1
MM_TILE_CAP = 1408
VMEM_LIMIT_V7X = 48 * 1024 * 1024


def _cparams(sem):
    return pltpu.CompilerParams(dimension_semantics=sem, vmem_limit_bytes=VMEM_LIMIT_V7X)


def _pick(n, cands):
    for c in cands:
        if n % c == 0:
            return c
    return n


def _tile(n, cap):
    if n <= cap:
        return n
    for d in range(cap - cap % 128, 0, -128):
        if n % d == 0:
            return d
    return n


def _dg(a, b, ca, cb, batch):
    dims = (((ca,), (cb,)), ((0,), (0,))) if batch else (((ca,), (cb,)), ((), ()))
    return lax.dot_general(a.astype(BF16), b.astype(BF16), dims, preferred_element_type=F32)


@jax.custom_vjp
def mm_nn(a, b):
    n = a.ndim
    return _dg(a, b, n - 1, n - 2, n == 3)


def _mm_nn_fwd(a, b):
    return mm_nn(a, b), (a, b)


def _mm_nn_bwd(res, g):
    a, b = res
    n = a.ndim
    return _dg(g, b, n - 1, n - 1, n == 3), _dg(a, g, n - 2, n - 2, n == 3)


mm_nn.defvjp(_mm_nn_fwd, _mm_nn_bwd)


@jax.custom_vjp
def mm_nt(a, b):
    n = a.ndim
    return _dg(a, b, n - 1, n - 1, n == 3)


def _mm_nt_fwd(a, b):
    return mm_nt(a, b), (a, b)


def _mm_nt_bwd(res, g):
    a, b = res
    n = a.ndim
    return _dg(g, b, n - 1, n - 2, n == 3), _dg(g, a, n - 2, n - 2, n == 3)


mm_nt.defvjp(_mm_nt_fwd, _mm_nt_bwd)


def mmh(a, b, precision=H3):
    n = a.ndim
    dims = (((n - 1,), (n - 2,)), ((0,), (0,))) if n == 3 else (((1,), (0,)), ((), ()))
    return lax.dot_general(a, b, dims, precision=precision, preferred_element_type=F32)


def mmh_nt(a, b):
    n = a.ndim
    dims = (((n - 1,), (n - 1,)), ((0,), (0,))) if n == 3 else (((1,), (1,)), ((), ()))
    return lax.dot_general(a, b, dims, precision=H3, preferred_element_type=F32)


def mmh_tn(a, b):
    n = a.ndim
    dims = (((n - 2,), (n - 2,)), ((0,), (0,))) if n == 3 else (((0,), (0,)), ((), ()))
    return lax.dot_general(a, b, dims, precision=H3, preferred_element_type=F32)


def matmul(a, b, mode, name, residual=None):
    out_dtype = BF16 if mode == "tn" else F32
    if mode == "nn":
        (M, K), (_, N) = a.shape, b.shape
    elif mode == "nt":
        (M, K), (N, _) = a.shape, b.shape
    else:
        (K, M), (_, N) = a.shape, b.shape
    tm = _tile(M, 2048 if mode == "nn" else MM_TILE_CAP)
    tn = _tile(N, 512 if mode == "nn" else MM_TILE_CAP)
    tk = _tile(K, MM_TILE_CAP)
    nk = K // tk
    if mode == "nn":
        a_spec = pl.BlockSpec((tm, tk), lambda i, j, k: (i, k))
        b_spec = pl.BlockSpec((tk, tn), lambda i, j, k: (k, j))
        dims = (((1,), (0,)), ((), ()))
    elif mode == "nt":
        a_spec = pl.BlockSpec((tm, tk), lambda i, j, k: (i, k))
        b_spec = pl.BlockSpec((tn, tk), lambda i, j, k: (j, k))
        dims = (((1,), (1,)), ((), ()))
    else:
        a_spec = pl.BlockSpec((tk, tm), lambda i, j, k: (k, i))
        b_spec = pl.BlockSpec((tk, tn), lambda i, j, k: (k, j))
        dims = (((0,), (0,)), ((), ()))
    o_spec = pl.BlockSpec((tm, tn), lambda i, j, k: (i, j))
    has_res = residual is not None

    def body(*refs):
        if has_res:
            a_ref, b_ref, r_ref, o_ref, acc_ref = refs
        else:
            a_ref, b_ref, o_ref, acc_ref = refs
        k = pl.program_id(2)

        @pl.when(k == 0)
        def _():
            acc_ref[...] = jnp.zeros_like(acc_ref)

        acc_ref[...] += lax.dot_general(a_ref[...].astype(BF16), b_ref[...].astype(BF16), dims,
                                        preferred_element_type=F32)

        @pl.when(k == nk - 1)
        def _():
            if has_res:
                o_ref[...] = (acc_ref[...] + r_ref[...]).astype(out_dtype)
            else:
                o_ref[...] = acc_ref[...].astype(out_dtype)

    ins = [a, b] + ([residual] if has_res else [])
    in_specs = [a_spec, b_spec] + ([o_spec] if has_res else [])
    return pl.pallas_call(
        body, grid=(M // tm, N // tn, nk), in_specs=in_specs, out_specs=o_spec,
        out_shape=jax.ShapeDtypeStruct((M, N), out_dtype), scratch_shapes=[pltpu.VMEM((tm, tn), F32)],
        compiler_params=_cparams(("parallel", "parallel", "arbitrary")), name=name)(*ins)


class Cols:
    def __init__(self, arr, width, idx):
        self.arr, self.width, self.idx = arr, width, idx


def _arr(x):
    return x.arr if isinstance(x, Cols) else x


def _shape(x):
    return x.arr.shape[:-1] + (x.width,) if isinstance(x, Cols) else x.shape


def _col(x):
    return x.idx if isinstance(x, Cols) else 0


def _tok_spec(x, tile):
    shape, col = _shape(x), _col(x)
    return pl.BlockSpec(shape[:-2] + (tile, shape[-1]), lambda i: (0,) * (len(shape) - 2) + (i, col))


def _full_spec(x):
    shape, col = _shape(x), _col(x)
    return pl.BlockSpec(shape, lambda i: (0,) * (len(shape) - 1) + (col,))


def _halo_spec(x, tile):
    shape, col = _shape(x), _col(x)
    return pl.BlockSpec((8, shape[-1]), lambda i: (jnp.maximum(i * (tile // 8) - 1, 0), col))


def _blk(x, tile):
    shape = _shape(x)
    return jax.ShapeDtypeStruct(shape[:-2] + (tile, shape[-1]), _arr(x).dtype)


def _prev_rows(x, halo):
    rows = lax.broadcasted_iota(jnp.int32, (x.shape[0], 1), 0)
    before = jnp.where(pl.program_id(0) > 0, halo[7:8], 0.0)
    return jnp.where(rows == 0, before, pltpu.roll(x, 1, 0))


def stage_fwd(f, xs, ps, cts, cfs, tile, name, out_dtypes=None, with_prev=False):
    xs, ps, cts, cfs = list(xs), list(ps), list(cts), list(cfs)
    halos = xs if with_prev else []
    nx, nh, nct, np_ = len(xs), len(halos), len(cts), len(ps)
    T = _shape(xs[0])[-2]
    blk = [_blk(x, tile) for x in xs]
    out_avals = jax.eval_shape(f, *blk, *(blk if with_prev else []), *[_blk(p, _shape(p)[-2]) for p in ps],
                               *[_blk(c, tile) for c in cts], *[_blk(c, _shape(c)[-2]) for c in cfs])
    if out_dtypes is None:
        out_dtypes = [o.dtype for o in out_avals]
    out_shape = [jax.ShapeDtypeStruct(o.shape[:-2] + (T, o.shape[-1]), dt) for o, dt in zip(out_avals, out_dtypes)]
    n_in = nx + nh + nct + np_ + len(cfs)

    def body(*refs):
        vals = [r[...] for r in refs[:n_in]]
        xv, hv, rest = vals[:nx], vals[nx:nx + nh], vals[nx + nh:]
        ctv, pv, cfv = rest[:nct], rest[nct:nct + np_], rest[nct + np_:]
        prev = [_prev_rows(x, h) for x, h in zip(xv, hv)]
        res = f(*xv, *prev, *pv, *ctv, *cfv)
        for o_ref, r in zip(refs[n_in:], res):
            o_ref[...] = r.astype(o_ref.dtype)

    return pl.pallas_call(
        body, grid=(T // tile,),
        in_specs=([_tok_spec(x, tile) for x in xs] + [_halo_spec(x, tile) for x in halos] + [_tok_spec(c, tile) for c in cts]
                  + [_full_spec(p) for p in ps + cfs]),
        out_specs=[_tok_spec(o, tile) for o in out_shape], out_shape=out_shape,
        compiler_params=_cparams(("parallel",)), name=name)(*[_arr(a) for a in xs + halos + cts + ps + cfs])


def stage_bwd(f, xs, ps, cts, cfs, gs, tile, name, bf16_copies=(), with_prev=False):
    xs, ps, cts, cfs = list(xs), list(ps), list(cts), list(cfs)
    gs = [list(g) if isinstance(g, (list, tuple)) else [g] for g in gs]
    g_flat = [a for g in gs for a in g]
    halos = xs if with_prev else []
    nx, nh, nct, ng, np_ = len(xs), len(halos), len(cts), len(g_flat), len(ps)
    T = _shape(xs[0])[-2]
    dx_like = xs + halos
    out_shape = ([jax.ShapeDtypeStruct(_shape(x), F32) for x in dx_like] + [jax.ShapeDtypeStruct(_shape(p), F32) for p in ps]
                 + [jax.ShapeDtypeStruct(_shape(xs[i]), BF16) for i in bf16_copies])
    n_in = nx + nh + nct + ng + np_ + len(cfs)
    ndx = nx + nh

    def body(*refs):
        vals = [r[...] for r in refs[:n_in]]
        outs = refs[n_in:]
        xv, hv, rest = vals[:nx], vals[nx:nx + nh], vals[nx + nh:]
        ctv, gparts, pv, cfv = rest[:nct], rest[nct:nct + ng], rest[nct + ng:nct + ng + np_], rest[nct + ng + np_:]
        gv = []
        for g in gs:
            gv.append(functools.reduce(lambda a, b: a + b, gparts[:len(g)]))
            gparts = gparts[len(g):]
        prev = [_prev_rows(x, h) for x, h in zip(xv, hv)]
        _, vjp = jax.vjp(lambda *xp: f(*xp, *ctv, *cfv), *xv, *prev, *pv)
        d = vjp(tuple(gv))
        for o_ref, r in zip(outs[:ndx], d[:ndx]):
            o_ref[...] = r
        for o_ref, i in zip(outs[ndx + np_:], bf16_copies):
            o_ref[...] = d[i].astype(BF16)

        @pl.when(pl.program_id(0) == 0)
        def _():
            for o_ref in outs[ndx:ndx + np_]:
                o_ref[...] = jnp.zeros_like(o_ref)

        for o_ref, r in zip(outs[ndx:ndx + np_], d[ndx:]):
            o_ref[...] += r

    plain = lambda x: jax.ShapeDtypeStruct(_shape(x), F32)
    res = pl.pallas_call(
        body, grid=(T // tile,),
        in_specs=([_tok_spec(x, tile) for x in xs] + [_halo_spec(x, tile) for x in halos]
                  + [_tok_spec(c, tile) for c in cts + g_flat] + [_full_spec(p) for p in ps + cfs]),
        out_specs=([_tok_spec(plain(x), tile) for x in dx_like] + [_full_spec(plain(p)) for p in ps]
                   + [_tok_spec(plain(xs[i]), tile) for i in bf16_copies]), out_shape=out_shape,
        compiler_params=_cparams(("arbitrary",)), name=name)(*[_arr(a) for a in xs + halos + cts + g_flat + ps + cfs])
    if bf16_copies:
        return list(res[:ndx]), list(res[ndx:ndx + np_]), list(res[ndx + np_:])
    return list(res[:ndx]), list(res[ndx:])


def _rms(x, g, eps=RMS_EPS):
    return x * lax.rsqrt(jnp.mean(x * x, axis=-1, keepdims=True) + eps) * g


def f_rmsnorm(x, g):
    return (_rms(x, g),)


def f_rmsnorm_res(x, g):
    return _rms(x, g), x


def f_rmsnorm2(x, g1, g2):
    n = x * lax.rsqrt(jnp.mean(x * x, axis=-1, keepdims=True) + RMS_EPS)
    return n * g1, n * g2


def f_rmsnorm2_res(x, g1, g2):
    return f_rmsnorm2(x, g1, g2) + (x,)


def _sigmoid(x):
    return 1.0 / (1.0 + jnp.exp(-x))


def _softplus(x):
    return jnp.maximum(x, 0.0) + jnp.log(1.0 + jnp.exp(-jnp.abs(x)))


def f_rwkv_pre(pr, pk, pv, pl_, qr, qk, qv, ql, mu_r, mu_k, mu_v, mu_l, w0, w2, a0, a2, g2, k_k, k_a, seg, seg_t):
    xr = pr + (qr - pr) * mu_r
    xk = pk + (qk - pk) * mu_k
    xv = pv + (qv - pv) * mu_v
    xl = pl_ + (ql - pl_) * mu_l
    w_log = -_softplus(-(w0 + mm_nn(jnp.tanh(xl), w2))) - 0.5
    lw = -jnp.exp(w_log)
    a = _sigmoid(a0 + mm_nn(xl, a2))
    g = mm_nn(_sigmoid(xl), g2)
    kkr = xk * k_k
    inv = lax.rsqrt(jnp.maximum(mmh(kkr * kkr, seg), 1e-24))
    kk = kkr * mmh(inv, seg_t)
    k2 = xk * (1.0 + (a - 1.0) * k_a)
    return xr, lw, k2, xv, kk, kk * a, g


def f_rwkv_post(y, r, k2, v, g, lnx_w, lnx_b, r_k, seg, seg_t):
    inv_n = 1.0 / HEAD_DIM
    m = mmh(mmh(y, seg) * inv_n, seg_t)
    yc = y - m
    rstd = lax.rsqrt(mmh(yc * yc, seg) * inv_n + LNX_EPS)
    yn = yc * mmh(rstd, seg_t) * lnx_w + lnx_b
    bonus = mmh(mmh(r * k2 * r_k, seg), seg_t) * v
    return ((yn + bonus) * g,)


def _headnorm(z, g, seg, seg_t):
    ms = mmh(z * z, seg) * (1.0 / HEAD_DIM)
    return z * mmh(lax.rsqrt(ms + RMS_EPS), seg_t) * g


def f_headnorm(z, g, seg, seg_t):
    return (_headnorm(z, g, seg, seg_t),)


def _rot_half(z):
    w = z.shape[1]
    half = HEAD_DIM // 2
    lane = lax.broadcasted_iota(jnp.int32, (1, w), 1)
    return jnp.where((lane & (HEAD_DIM - 1)) < half, -pltpu.roll(z, w - half, 1), pltpu.roll(z, half, 1))


@jax.custom_vjp
def _rotate_half(z):
    return _rot_half(z)


_rotate_half.defvjp(lambda z: (_rot_half(z), None), lambda _, g: (-_rot_half(g),))


def f_qkprep(z, g, cos, sin, seg, seg_t):
    zn = _headnorm(z, g, seg, seg_t)
    return (zn * cos + _rotate_half(zn) * sin,)


def _head_mask(width, h):
    lane = lax.broadcasted_iota(jnp.int32, (1, width), 1)
    return jnp.where((lane >> 6) == h, jnp.ones((), F32), 0.0)


def f_memattn(q, k, v, q_norm, seg, seg_t):
    qn = _headnorm(q, q_norm, seg, seg_t)
    out = jnp.zeros_like(q)
    for h in range(MEM_HEADS):
        m = _head_mask(MEM_WIDTH, h)
        s = mm_nt(qn * m, k) * (1.0 / math.sqrt(HEAD_DIM))
        s = s - jnp.max(s, axis=-1, keepdims=True)
        p = jnp.exp(s)
        p = p / jnp.sum(p, axis=-1, keepdims=True)
        out = out + mm_nn(p, v) * m
    return (out,)


def f_mix(o1, o2, o3, l1, l2, l3):
    mx = jnp.maximum(jnp.maximum(l1, l2), l3)
    e1, e2, e3 = jnp.exp(l1 - mx), jnp.exp(l2 - mx), jnp.exp(l3 - mx)
    return ((e1 * o1 + e2 * o2 + e3 * o3) / (e1 + e2 + e3),)


def _chunk_masks(L):
    t = lax.broadcasted_iota(jnp.int32, (L, L), 0)
    s = lax.broadcasted_iota(jnp.int32, (L, L), 1)
    return t, s


def _unit_lower_inverse(a):
    L = a.shape[-1]
    t, s = _chunk_masks(L)
    one = jnp.ones((), F32)
    blk = lambda sh: jnp.where((t >> sh) == (s >> sh), one, 0.0)
    n0 = a * blk(3)
    x = jnp.where(t == s, one, 0.0) - n0
    n2 = mmh(n0, n0)
    x = x + mmh(x, n2)
    x = x + mmh(x, mmh(n2, n2))
    for sh in (3, 4, 5):
        if (1 << sh) >= L:
            break
        off = a * (blk(sh + 1) - blk(sh))
        x = x - mmh(x, mmh(off, x))
    return x


@jax.custom_vjp
def _inverse_known(a, x):
    return x


def _inverse_known_fwd(a, x):
    return x, x


def _inverse_known_bwd(x, dx):
    return -mmh_nt(mmh_tn(x, dx), x), jnp.zeros_like(x)


_inverse_known.defvjp(_inverse_known_fwd, _inverse_known_bwd)


def _running_sum(x, reverse):
    L = x.shape[1]
    pos = lax.broadcasted_iota(jnp.int32, (1, L, 1), 1)
    step = 1
    while step < L:
        if reverse:
            x = x + jnp.where(pos < L - step, pltpu.roll(x, L - step, 1), 0.0)
        else:
            x = x + jnp.where(pos >= step, pltpu.roll(x, step, 1), 0.0)
        step *= 2
    return x


@jax.custom_vjp
def _cumsum_tokens(x):
    return _running_sum(x, False)


_cumsum_tokens.defvjp(lambda x: (_running_sum(x, False), None), lambda _, g: (_running_sum(g, True),))


def f_rwkv_chunk(s0, r, lw, k, v, kk, b, x_known=None):
    H, L, _ = r.shape
    t, s = _chunk_masks(L)
    one = jnp.ones((), F32)
    incl = jnp.where(t >= s, one, 0.0)
    strict = jnp.where(t > s, one, 0.0)
    cum = _cumsum_tokens(lw)
    w_in = jnp.exp(cum)
    w_ex = jnp.exp(cum - lw)
    w_inv = jnp.exp(-cum)
    rt, kkt, kt, bt = r * w_in, kk * w_ex, k * w_inv, b * w_inv
    a_b = mmh_nt(kkt, bt) * strict
    a_k = mmh_nt(kkt, kt) * strict
    m_k = mmh_nt(rt, kt) * incl
    m_b = mmh_nt(rt, bt) * incl
    x = _unit_lower_inverse(a_b) if x_known is None else _inverse_known(a_b, x_known)
    u = mmh(x, mmh_nt(kkt, s0) + mmh(a_k, v))
    y = mmh_nt(rt, s0) + mmh(m_k, v) - mmh(m_b, u)
    w_last = jnp.exp(jnp.sum(lw, axis=1, keepdims=True))
    s1 = (s0 + mmh_tn(v, kt) - mmh_tn(u, bt)) * w_last
    return y, s1, x


def _ex_split(ex, refs, n_in, n_out):
    n = ex.n
    ins, ex_in = refs[:n_in], refs[n_in:n_in + n]
    outs, ex_out = refs[n_in + n:n_in + n + n_out], refs[n_in + n + n_out:n_in + 2 * n + n_out]
    rest = refs[n_in + 2 * n + n_out:]
    return ins, outs, rest[:len(rest) - 3], (ex_in, ex_out) + tuple(rest[len(rest) - 3:])


def _split_heads(x):
    return jnp.stack([x[:, h * HEAD_DIM:(h + 1) * HEAD_DIM] for h in range(x.shape[1] // HEAD_DIM)], axis=0)


def _merge_heads(x):
    return jnp.concatenate([x[h] for h in range(x.shape[0])], axis=1)


def rwkv_scan_fwd(r, lw, k, v, kk, b, ex):
    T, N = r.shape[0], HEAD_DIM
    H = r.shape[1] // N
    groups = SCAN_GROUPS_FWD
    nc, hg = T // CHUNK, H // groups
    seq = pl.BlockSpec((CHUNK, hg * N), lambda g, c: (c, g))

    def body(*refs):
        (r_ref, lw_ref, k_ref, v_ref, kk_ref, b_ref), (y_ref, hs_ref, xs_ref), (h_scr,), ex_refs = _ex_split(ex, refs, 6, 3)
        g, c = pl.program_id(0), pl.program_id(1)

        @pl.when(jnp.logical_and(g == 0, c == 0))
        def _():
            ex.start(*ex_refs)

        @pl.when(c == 0)
        def _():
            h_scr[...] = jnp.zeros_like(h_scr)

        h0 = h_scr[...]
        hs_ref[0] = h0
        y, h1, x = f_rwkv_chunk(h0, *[_split_heads(z[...]) for z in (r_ref, lw_ref, k_ref, v_ref, kk_ref, b_ref)])
        y_ref[...] = _merge_heads(y)
        xs_ref[0] = x
        h_scr[...] = h1

        @pl.when(jnp.logical_and(g == groups - 1, c == (3 * nc) // 4))
        def _():
            ex.forward(*ex_refs)

        @pl.when(jnp.logical_and(g == groups - 1, c == nc - 1))
        def _():
            ex.wait(*ex_refs)

    res = pl.pallas_call(
        body, grid=(groups, nc), in_specs=[seq] * 6 + [_ANY] * ex.n,
        out_specs=[seq, pl.BlockSpec((1, hg, N, N), lambda g, c: (c, g, 0, 0)),
                   pl.BlockSpec((1, hg, CHUNK, CHUNK), lambda g, c: (c, g, 0, 0))] + [_ANY] * ex.n,
        out_shape=[jax.ShapeDtypeStruct((T, H * N), F32), jax.ShapeDtypeStruct((nc, H, N, N), F32),
                   jax.ShapeDtypeStruct((nc, H, CHUNK, CHUNK), F32)] + ex.out_shape(),
        scratch_shapes=[pltpu.VMEM((hg, N, N), F32)] + ex.scratch(),
        compiler_params=_cparams(("arbitrary", "arbitrary")), name="rwkv_scan_fwd")(r, lw, k, v, kk, b, *ex.operands())
    return res[0], (res[1], res[2]), list(res[3:])


def rwkv_scan_bwd(r, lw, k, v, kk, b, saved, dy, ex):
    T, N = r.shape[0], HEAD_DIM
    H = r.shape[1] // N
    groups = SCAN_GROUPS_BWD
    nc, hg = T // CHUNK, H // groups
    seq = pl.BlockSpec((CHUNK, hg * N), lambda g, c: (nc - 1 - c, g))
    state = pl.BlockSpec((1, hg, N, N), lambda g, c: (nc - 1 - c, g, 0, 0))

    def body(*refs):
        (r_ref, lw_ref, k_ref, v_ref, kk_ref, b_ref, hs_ref, xs_ref, dy_ref), outs, (dh_scr,), ex_refs = _ex_split(ex, refs, 9, 6)
        g, c = pl.program_id(0), pl.program_id(1)

        @pl.when(jnp.logical_and(g == 0, c == 0))
        def _():
            ex.start(*ex_refs)

        @pl.when(c == 0)
        def _():
            dh_scr[...] = jnp.zeros_like(dh_scr)

        x_known = xs_ref[0]
        _, vjp = jax.vjp(lambda *a: f_rwkv_chunk(*a, x_known=x_known)[:2], hs_ref[0],
                         *[_split_heads(z[...]) for z in (r_ref, lw_ref, k_ref, v_ref, kk_ref, b_ref)])
        d = vjp((_split_heads(dy_ref[...]), dh_scr[...]))
        dh_scr[...] = d[0]
        for o_ref, dz in zip(outs, d[1:]):
            o_ref[...] = _merge_heads(dz)

        @pl.when(jnp.logical_and(g == groups - 1, c == nc - 1))
        def _():
            ex.forward(*ex_refs)
            ex.wait(*ex_refs)

    res = pl.pallas_call(
        body, grid=(groups, nc),
        in_specs=[seq] * 6 + [state, state, seq] + [_ANY] * ex.n,
        out_specs=[seq] * 6 + [_ANY] * ex.n, out_shape=[jax.ShapeDtypeStruct((T, H * N), F32)] * 6 + ex.out_shape(),
        scratch_shapes=[pltpu.VMEM((hg, N, N), F32)] + ex.scratch(),
        compiler_params=_cparams(("arbitrary", "arbitrary")), name="rwkv_scan_bwd")(r, lw, k, v, kk, b, *saved, dy, *ex.operands())
    return list(res[:6]), list(res[6:])


GROUP_COLS = 4 * HEAD_DIM


def _f_dilattn(has_prev, q, kc, kp, vc, vp):
    scale = 1.0 / math.sqrt(HEAD_DIM)
    i = lax.broadcasted_iota(jnp.int32, (DIL_BLOCK, DIL_BLOCK), 0)
    j = lax.broadcasted_iota(jnp.int32, (DIL_BLOCK, DIL_BLOCK), 1)
    o, l = jnp.zeros_like(q), jnp.zeros_like(q)
    for h in range(q.shape[1] // HEAD_DIM):
        m = _head_mask(q.shape[1], h)
        sc = jnp.where(j <= i, mm_nt(q * m, kc) * scale, NEG_INF)
        sp = jnp.where(jnp.logical_and(i <= j, has_prev), mm_nt(q * m, kp) * scale, NEG_INF)
        mx = jnp.maximum(jnp.max(sc, axis=-1, keepdims=True), jnp.max(sp, axis=-1, keepdims=True))
        pc, pp = jnp.exp(sc - mx), jnp.exp(sp - mx)
        den = jnp.sum(pc, axis=-1, keepdims=True) + jnp.sum(pp, axis=-1, keepdims=True)
        o = o + (mm_nn(pc, vc) + mm_nn(pp, vp)) / den * m
        l = l + (mx + jnp.log(den)) * m
    return o, l


def _dil_specs(gi, d):
    parts = 1 if d == 1 else 2
    blk = (DIL_BLOCK * d, GROUP_COLS // parts)
    at = lambda col: (lambda p, n: (n, col * parts + p))
    before = lambda col: (lambda p, n: (jnp.maximum(n - 1, 0), col * parts + p))
    v0 = DIL_WIDTH // GROUP_COLS + gi
    q = pl.BlockSpec(blk, at(gi))
    kc, kp = pl.BlockSpec(blk, at(gi)), pl.BlockSpec(blk, before(gi))
    vc, vp = pl.BlockSpec(blk, at(v0)), pl.BlockSpec(blk, before(v0))
    out = pl.BlockSpec(blk, at(0))
    together = min(d, 2)
    return (q, kc, kp, vc, vp, out), parts, together


def _residue_rows(r, d):
    return pl.ds(r, DIL_BLOCK, stride=d) if d > 1 else pl.ds(0, DIL_BLOCK)


def dil_fwd(q, k, kv, gi, d, name):
    T = q.shape[0]
    (qs, kc, kp, vc, vp, out), parts, together = _dil_specs(gi, d)

    def body(q_ref, kc_ref, kp_ref, vc_ref, vp_ref, o_ref, l_ref):
        has_prev = pl.program_id(1) > 0

        def residues(it, carry):
            rows = [_residue_rows(it * together + a, d) for a in range(together)]
            ins = [[ref[rw, :] for ref in (q_ref, kc_ref, kp_ref, vc_ref, vp_ref)] for rw in rows]
            res = [_f_dilattn(has_prev, *x) for x in ins]
            for rw, (o, l) in zip(rows, res):
                o_ref[rw, :] = o
                l_ref[rw, :] = l
            return carry

        lax.fori_loop(0, d // together, residues, 0)

    shape = jax.ShapeDtypeStruct((T, 4 * HEAD_DIM), F32)
    return pl.pallas_call(
        body, grid=(parts, T // (DIL_BLOCK * d)), in_specs=[qs, kc, kp, vc, vp], out_specs=[out, out], out_shape=[shape, shape],
        compiler_params=_cparams(("parallel", "parallel")), name=name)(q, k, k, kv, kv)


def dil_bwd(q, k, kv, do, dl, gi, d, name):
    T = q.shape[0]
    (qs, kc, kp, vc, vp, out), parts, together = _dil_specs(gi, d)

    def body(q_ref, kc_ref, kp_ref, vc_ref, vp_ref, do_ref, dl_ref, *outs):
        f = functools.partial(_f_dilattn, pl.program_id(1) > 0)

        def residues(it, carry):
            rows = [_residue_rows(it * together + a, d) for a in range(together)]
            ins = [[ref[rw, :] for ref in (q_ref, kc_ref, kp_ref, vc_ref, vp_ref, do_ref, dl_ref)] for rw in rows]
            res = [jax.vjp(f, *x[:5])[1]((x[5], x[6])) for x in ins]
            for rw, gs in zip(rows, res):
                for o_ref, g in zip(outs, gs):
                    o_ref[rw, :] = g
            return carry

        lax.fori_loop(0, d // together, residues, 0)

    shape = jax.ShapeDtypeStruct((T, 4 * HEAD_DIM), F32)
    dq, dkc, dkp, dvc, dvp = pl.pallas_call(
        body, grid=(parts, T // (DIL_BLOCK * d)), in_specs=[qs, kc, kp, vc, vp, out, out], out_specs=[out] * 5, out_shape=[shape] * 5,
        compiler_params=_cparams(("parallel", "parallel")), name=name)(q, k, k, kv, kv, do, dl)

    def own_plus_next(c, p):
        return c + jnp.concatenate([p[DIL_BLOCK * d:], jnp.zeros_like(p[:DIL_BLOCK * d])], axis=0)

    return dq, own_plus_next(dkc, dkp), own_plus_next(dvc, dvp)


CONV_TILE = 128


def _conv3(u, h6, h7, w, b):
    rows = lax.broadcasted_iota(jnp.int32, (u.shape[0], 1), 0)
    s1 = jnp.where(rows == 0, h7, pltpu.roll(u, 1, 0))
    s2 = jnp.where(rows == 0, h6, jnp.where(rows == 1, h7, pltpu.roll(u, 2, 0)))
    return b + w[0:1] * s2 + w[1:2] * s1 + w[2:3] * u, s1, s2


def _conv_halves(u_ref, h_ref, cw_ref, cb_ref):
    F = D_FF
    first = pl.program_id(0) > 0
    res = []
    for lo in (0, F):
        h = h_ref[:, lo:lo + F]
        h6 = jnp.where(first, h[6:7], 0.0)
        h7 = jnp.where(first, h[7:8], 0.0)
        u = u_ref[:, lo:lo + F]
        res.append((u,) + _conv3(u, h6, h7, cw_ref[:, lo:lo + F], cb_ref[:, lo:lo + F]))
    return res


def _halo_before(C):
    return pl.BlockSpec((8, C), lambda i: (jnp.maximum(i * (CONV_TILE // 8) - 1, 0), 0))


def convgate_fwd(u, cw, cb, name):
    T, C = u.shape
    F = C // 2

    def body(u_ref, h_ref, cw_ref, cb_ref, z_ref):
        (_, cg, _, _), (_, cv, _, _) = _conv_halves(u_ref, h_ref, cw_ref, cb_ref)
        z_ref[...] = (cg * _sigmoid(cg) * cv).astype(BF16)

    return pl.pallas_call(
        body, grid=(T // CONV_TILE,),
        in_specs=[pl.BlockSpec((CONV_TILE, C), lambda i: (i, 0)), _halo_before(C), _full_spec(cw), _full_spec(cb)],
        out_specs=pl.BlockSpec((CONV_TILE, F), lambda i: (i, 0)), out_shape=jax.ShapeDtypeStruct((T, F), BF16),
        compiler_params=_cparams(("parallel",)), name=name)(u, u, cw, cb)


def convgate_bwd(u, cw, cb, dz, name):
    T, C = u.shape
    F = C // 2
    n = T // CONV_TILE
    E = CONV_TILE + 8

    def body(u_ref, hb_ref, ha_ref, cw_ref, cb_ref, dz_ref, dza_ref, du_ref, dcw_ref, dcb_ref):
        i = pl.program_id(0)
        dze = jnp.concatenate([dz_ref[...], jnp.where(i < n - 1, dza_ref[...], 0.0)], axis=0)

        @pl.when(i == 0)
        def _():
            dcw_ref[...] = jnp.zeros_like(dcw_ref)
            dcb_ref[...] = jnp.zeros_like(dcb_ref)

        halves = []
        for lo in (0, F):
            sl = slice(lo, lo + F)
            hb = hb_ref[:, sl]
            ue = jnp.concatenate([u_ref[:, sl], ha_ref[:, sl]], axis=0)
            c, s1, s2 = _conv3(ue, jnp.where(i > 0, hb[6:7], 0.0), jnp.where(i > 0, hb[7:8], 0.0), cw_ref[:, sl], cb_ref[:, sl])
            halves.append((sl, ue, c, s1, s2))
        (_, _, cg, _, _), (_, _, cv, _, _) = halves
        sg = _sigmoid(cg)
        dcs = (dze * cv * sg * (1.0 + cg * (1.0 - sg)), dze * cg * sg)
        for (sl, ue, _, s1, s2), dc in zip(halves, dcs):
            own = lambda z: z[:CONV_TILE]
            dcb_ref[:, sl] += jnp.sum(own(dc), axis=0, keepdims=True)
            dcw_ref[0:1, sl] += jnp.sum(own(dc * s2), axis=0, keepdims=True)
            dcw_ref[1:2, sl] += jnp.sum(own(dc * s1), axis=0, keepdims=True)
            dcw_ref[2:3, sl] += jnp.sum(own(dc * ue), axis=0, keepdims=True)
            du = cw_ref[2:3, sl] * dc + cw_ref[1:2, sl] * pltpu.roll(dc, E - 1, 0) + cw_ref[0:1, sl] * pltpu.roll(dc, E - 2, 0)
            du_ref[:, sl] = own(du).astype(BF16)

    after = lambda w: pl.BlockSpec((8, w), lambda i: (jnp.minimum((i + 1) * (CONV_TILE // 8), T // 8 - 1), 0))
    return pl.pallas_call(
        body, grid=(n,),
        in_specs=[pl.BlockSpec((CONV_TILE, C), lambda i: (i, 0)), _halo_before(C), after(C), _full_spec(cw), _full_spec(cb),
                  pl.BlockSpec((CONV_TILE, F), lambda i: (i, 0)), after(F)],
        out_specs=[pl.BlockSpec((CONV_TILE, C), lambda i: (i, 0)), _full_spec(cw), _full_spec(cb)],
        out_shape=[jax.ShapeDtypeStruct((T, C), BF16), jax.ShapeDtypeStruct(cw.shape, F32), jax.ShapeDtypeStruct(cb.shape, F32)],
        compiler_params=_cparams(("arbitrary",)), name=name)(u, u, u, cw, cb, dz, dz)


def loss_head(y, tgt):
    T, D = y.shape
    tile = 256

    def body(y_ref, t_ref, l_ref, d_ref, db_ref):
        d = y_ref[...] - t_ref[...]
        d_ref[...] = d * (1.0 / D)
        db_ref[...] = (d * (1.0 / D)).astype(BF16)

        @pl.when(pl.program_id(0) == 0)
        def _():
            l_ref[...] = jnp.zeros_like(l_ref)

        l_ref[...] += (0.5 / D) * jnp.sum(d * d)

    row = pl.BlockSpec((tile, D), lambda i: (i, 0))
    return pl.pallas_call(
        body, grid=(T // tile,), in_specs=[row, row], out_specs=[pl.BlockSpec((8, 128), lambda i: (0, 0)), row, row],
        out_shape=[jax.ShapeDtypeStruct((8, 128), F32), jax.ShapeDtypeStruct((T, D), F32), jax.ShapeDtypeStruct((T, D), BF16)],
        compiler_params=_cparams(("arbitrary",)), name="loss_head")(y, tgt)


def sum_parts(parts, name):
    S, R, C = parts.shape
    tile = _pick(R, (256, 128, 64, 32, 16, 8))

    def body(p_ref, o_ref):
        acc = p_ref[0]
        for s in range(1, S):
            acc = acc + p_ref[s]
        o_ref[...] = acc

    return pl.pallas_call(
        body, grid=(R // tile,), in_specs=[pl.BlockSpec((S, tile, C), lambda i: (0, i, 0))],
        out_specs=pl.BlockSpec((tile, C), lambda i: (i, 0)), out_shape=jax.ShapeDtypeStruct((R, C), F32),
        compiler_params=_cparams(("parallel",)), name=name)(parts)


def adamw(gparts, w, m, v, name):
    S, R, C = gparts.shape
    tile = _pick(R, (256, 128, 64, 32, 16, 8))
    c1 = 1.0 / (1.0 - ADAM_B1 ** ADAM_STEP)
    c2 = 1.0 / (1.0 - ADAM_B2 ** ADAM_STEP)

    def body(g_ref, w_ref, m_ref, v_ref, go_ref, d_ref, mo_ref, vo_ref):
        g = g_ref[0].astype(F32)
        for s in range(1, S):
            g = g + g_ref[s].astype(F32)
        m1 = ADAM_B1 * m_ref[...] + (1.0 - ADAM_B1) * g
        v1 = ADAM_B2 * v_ref[...] + (1.0 - ADAM_B2) * (g * g)
        go_ref[...] = g
        mo_ref[...] = m1
        vo_ref[...] = v1
        d_ref[...] = -ADAM_LR * ((m1 * c1) / (jnp.sqrt(v1 * c2) + ADAM_EPS) + ADAM_WD * w_ref[...])

    row = pl.BlockSpec((tile, C), lambda i: (i, 0))
    return pl.pallas_call(
        body, grid=(R // tile,), in_specs=[pl.BlockSpec((S, tile, C), lambda i: (0, i, 0)), row, row, row],
        out_specs=[row] * 4, out_shape=[jax.ShapeDtypeStruct((R, C), F32)] * 4,
        compiler_params=_cparams(("parallel",)), name=name)(gparts, w, m, v)


def _peers():
    x, y, c = lax.axis_index("x"), lax.axis_index("y"), lax.axis_index("c")
    peers = []
    for k in range(1, N_DEV):
        px = 1 - x if k & 4 else x
        py = 1 - y if k & 2 else y
        pc = 1 - c if k & 1 else c
        peers.append(((px, py, pc), 4 * px + 2 * py + pc))
    return 4 * x + 2 * y + c, peers


_ANY = pl.BlockSpec(memory_space=pl.ANY)


class Exchange:
    def __init__(self, gathers=(), scatters=()):
        self.gathers, self.scatters = list(gathers), list(scatters)
        self.n = len(self.gathers) + len(self.scatters)

    def operands(self):
        return self.gathers + self.scatters

    def out_shape(self):
        return ([jax.ShapeDtypeStruct((N_DEV,) + x.shape, x.dtype) for x in self.gathers]
                + [jax.ShapeDtypeStruct(x.shape, x.dtype) for x in self.scatters])

    def scratch(self):
        n = max(self.n, 1)
        return [pltpu.SemaphoreType.DMA((7 * n,)), pltpu.SemaphoreType.DMA((7 * n,)), pltpu.SemaphoreType.DMA((n,))]

    def _copies(self, in_refs, out_refs, send_sems, recv_sems, local_sems):
        me, peers = _peers()
        ng = len(self.gathers)
        local, sends, recvs = [], [], []
        for a in range(self.n):
            x, o = in_refs[a], out_refs[a]
            mine = x if a < ng else x.at[me]
            local.append(pltpu.make_async_copy(mine, o.at[me], local_sems.at[a]))
            s_a, r_a = {}, {}
            for k in range(1, N_DEV):
                peer, slot = peers[k - 1]
                sems = dict(send_sem=send_sems.at[7 * a + k - 1], recv_sem=recv_sems.at[7 * a + k - 1],
                            device_id_type=pl.DeviceIdType.MESH)
                if a >= ng:
                    s_a[k] = pltpu.make_async_remote_copy(src_ref=x.at[slot], dst_ref=o.at[me], device_id=peer, **sems)
                elif k in FORWARDED:
                    came = o.at[peers[k - 2][1]]
                    s_a[k] = pltpu.make_async_remote_copy(src_ref=came, dst_ref=came, device_id=peers[0][0], **sems)
                else:
                    s_a[k] = pltpu.make_async_remote_copy(src_ref=x, dst_ref=o.at[me], device_id=peer, **sems)
                r_a[k] = pltpu.make_async_remote_copy(src_ref=mine, dst_ref=o.at[slot], device_id=peer, **sems)
            sends.append(s_a)
            recvs.append(r_a)
        return local, sends, recvs

    def start(self, *refs):
        if self.n == 0:
            return
        local, sends, _ = self._copies(*refs)
        for a in range(self.n):
            local[a].start()
            for k in range(1, N_DEV):
                if a >= len(self.gathers) or k not in FORWARDED:
                    sends[a][k].start()

    def forward(self, *refs):
        if not self.gathers:
            return
        _, sends, recvs = self._copies(*refs)
        for a in range(len(self.gathers)):
            for k in FORWARDED:
                recvs[a][k - 1].wait_recv()
                sends[a][k].start()

    def wait(self, *refs):
        if self.n == 0:
            return
        local, sends, recvs = self._copies(*refs)
        for a in range(self.n):
            waited_early = [f - 1 for f in FORWARDED] if a < len(self.gathers) else []
            for k in range(1, N_DEV):
                if k not in waited_early:
                    recvs[a][k].wait_recv()
            for k in range(1, N_DEV):
                sends[a][k].wait_send()
            local[a].wait()


FORWARDED = (3, 5, 7)


def exchange(ex, name):
    n = ex.n

    def body(*refs):
        args = (refs[:n], refs[n:2 * n]) + tuple(refs[2 * n:])
        ex.start(*args)
        ex.forward(*args)
        ex.wait(*args)

    return pl.pallas_call(body, in_specs=[_ANY] * n, out_specs=[_ANY] * n, out_shape=ex.out_shape(),
                          scratch_shapes=ex.scratch(), name=name)(*ex.operands())


def _heads(z, h):
    return z.reshape(z.shape[0], h, HEAD_DIM).transpose(1, 0, 2)


def _unheads(z):
    return z.transpose(1, 0, 2).reshape(z.shape[1], z.shape[0] * HEAD_DIM)


def _shift_up(z):
    return jnp.concatenate([z[1:], jnp.zeros_like(z[:1])], axis=0)


def _segments(width):
    seg = np.zeros((width, 128), np.float32)
    seg[np.arange(width), np.arange(width) // HEAD_DIM] = 1.0
    return jnp.asarray(seg), jnp.asarray(seg.T)


def _rope_consts(T, heads):
    inv = ROPE_THETA ** (-jnp.arange(0, HEAD_DIM, 2, dtype=F32) / HEAD_DIM)
    ang = jnp.arange(T, dtype=F32)[:, None] * inv[None, :]
    return jnp.tile(jnp.cos(ang), (1, 2 * heads)), jnp.tile(jnp.sin(ang), (1, 2 * heads))


def _per_head(g, heads):
    return jnp.tile(g.reshape(1, HEAD_DIM), (1, heads))


def _sum_heads(g):
    return g.reshape(-1, HEAD_DIM).sum(axis=0, keepdims=True)


LORA_COLS = 256
RW_TILE = 256
ROW_TILE = 512


def _local_step(x0, memx, tgt, P, ex_weights=None, weights_done=None, ex_grads=None):
    T = x0.shape[0]
    P = dict(P)
    G = {}
    seg, seg_t = _segments(RWKV_WIDTH)
    mseg = (seg[:MEM_WIDTH], seg_t[:, :MEM_WIDTH])
    cos, sin = _rope_consts(T, DIL_WIDTH // HEAD_DIM)
    row = lambda v: v.reshape(1, -1)

    def mem_fwd(i, q):
        memn = stage_fwd(f_rmsnorm, [memx], [P["mem_norm"][i:i + 1]], [], [], N_MEM, f"mem{i}_norm", [BF16])[0]
        kvm = matmul(memn, P["mem_w_kv"][i], "nn", f"mem{i}_kv")
        kn, qn = _per_head(P["mem_k_norm"][i], MEM_HEADS), _per_head(P["mem_q_norm"][i], MEM_HEADS)
        km = stage_fwd(f_headnorm, [Cols(kvm, MEM_WIDTH, 0)], [kn], [], mseg, N_MEM, f"mem{i}_knorm")[0]
        om = stage_fwd(f_memattn, [q], [km, Cols(kvm, MEM_WIDTH, 1), qn], [], mseg, ROW_TILE, f"mem{i}_attn")[0]
        return om, (memn, kvm, km, kn, qn, q)

    def mem_bwd(i, saved, dymem):
        memn, kvm, km, kn, qn, q = saved
        (dq,), (dkm, dvm, g_qn) = stage_bwd(f_memattn, [q], [km, Cols(kvm, MEM_WIDTH, 1), qn], [], mseg, [dymem], ROW_TILE,
                                            f"mem{i}_attn_bwd")
        (dkraw,), (g_kn,) = stage_bwd(f_headnorm, [Cols(kvm, MEM_WIDTH, 0)], [kn], [], mseg, [dkm], N_MEM, f"mem{i}_knorm_bwd")
        dkvm = jnp.concatenate([dkraw, dvm], axis=1).astype(BF16)
        g_w = matmul(memn, dkvm, "tn", f"mem{i}_kv_dw")
        dmemn = matmul(dkvm, P["mem_w_kv"][i], "nt", f"mem{i}_kv_dx")
        _, (g_mn,) = stage_bwd(f_rmsnorm, [memx], [P["mem_norm"][i:i + 1]], [], [], [dmemn], N_MEM, f"mem{i}_norm_bwd")
        return dq, g_mn, g_w, _sum_heads(g_qn), _sum_heads(g_kn)

    def ffn_fwd(i, xin):
        hn = stage_fwd(f_rmsnorm, [xin], [P["ffn_norm"][i:i + 1]], [], [], ROW_TILE, f"ffn{i}_norm", [BF16])[0]
        u = matmul(hn, P["ffn_w_up"][i], "nn", f"ffn{i}_up")
        z = convgate_fwd(u, P["ffn_conv_w"][i], P["ffn_conv_b"][i:i + 1], f"ffn{i}_conv")
        return matmul(z, P["ffn_w_down"][i], "nn", f"ffn{i}_down", residual=xin), (hn, u, z)

    def ffn_bwd(i, xin, saved, dxo, dxo_b):
        hn, u, z = saved
        dz = matmul(dxo_b, P["ffn_w_down"][i], "nt", f"ffn{i}_down_dx")
        g_down = matmul(z, dxo_b, "tn", f"ffn{i}_down_dw")
        du, g_cw, g_cb = convgate_bwd(u, P["ffn_conv_w"][i], P["ffn_conv_b"][i:i + 1], dz, f"ffn{i}_conv_bwd")
        dhn = matmul(du, P["ffn_w_up"][i], "nt", f"ffn{i}_up_dx")
        g_up = matmul(hn, du, "tn", f"ffn{i}_up_dw")
        (dxin,), (g_n,), (dxin_b,) = stage_bwd(f_rmsnorm_res, [xin], [P["ffn_norm"][i:i + 1]], [], [], [dhn, dxo], ROW_TILE,
                                               f"ffn{i}_norm_bwd", bf16_copies=(0,))
        return dxin, dxin_b, g_n, g_up, g_cw, g_cb, g_down

    h0 = stage_fwd(f_rmsnorm, [x0], [P["attn_norm"][0:1]], [], [], ROW_TILE, "l0_norm", [BF16])[0]
    p0 = matmul(h0, P["a_w_in"][0], "nn", "l0_in")
    lora0 = 3 * RWKV_WIDTH // LORA_COLS
    pre_xs = [Cols(p0, RWKV_WIDTH, 0), Cols(p0, RWKV_WIDTH, 1), Cols(p0, RWKV_WIDTH, 2), Cols(p0, LORA_COLS, lora0)]
    mu = [Cols(P["a_mu"], RWKV_WIDTH, 0), Cols(P["a_mu"], RWKV_WIDTH, 1), Cols(P["a_mu"], RWKV_WIDTH, 2),
          Cols(P["a_mu"], LORA_COLS, lora0)]
    lora_rows = lambda w, lo: jnp.pad(w, ((lo, LORA_COLS - lo - w.shape[0]), (0, 0)))
    pre_ps = mu + [P["a_w0"], lora_rows(P["a_w2"][0], 0), P["a_a0"], lora_rows(P["a_a2"][0], 64), lora_rows(P["a_g2"][0], 128),
                   P["a_k_k"], P["a_k_a"]]
    r, lw, k2, v, kk, b, g = stage_fwd(f_rwkv_pre, pre_xs, pre_ps, [], [seg, seg_t], RW_TILE, "l0_rwkv_pre", with_prev=True)
    scan_in = [r, lw, k2, v, kk, b]
    y_h, h_states, got = rwkv_scan_fwd(*scan_in, ex_weights or Exchange())
    if weights_done is not None:
        P.update(weights_done(got))
    y_s = y_h
    post_ps = [P["a_lnx_w"], P["a_lnx_b"], P["a_r_k"].reshape(1, RWKV_WIDTH)]
    ymix0 = stage_fwd(f_rwkv_post, [y_s, r, k2, v, g], post_ps, [], [seg, seg_t], RW_TILE, "l0_rwkv_post")[0]
    ymem0, mem0_saved = mem_fwd(0, Cols(p0, MEM_WIDTH, SHIFT_WIDTH // MEM_WIDTH))
    ycat0 = jnp.concatenate([ymix0, ymem0], axis=1).astype(BF16)
    x1 = matmul(ycat0, P["a_w_out"][0], "nn", "l0_out", residual=x0)
    x2, ffn0_saved = ffn_fwd(0, x1)

    hk, h1 = stage_fwd(f_rmsnorm2, [x2], [row(P["kv_norm"]), P["attn_norm"][1:2]], [], [], ROW_TILE, "l1_norm", [BF16, BF16])
    kvp = matmul(hk, P["kv_w"][0], "nn", "l1_kv")
    p1 = matmul(h1, P["b_w_in"][0], "nn", "l1_in")
    kraw, qraw = Cols(kvp, DIL_WIDTH, 0), Cols(p1, DIL_WIDTH, 0)
    kgain, qgain = _per_head(P["kv_k_norm"], DIL_WIDTH // HEAD_DIM), _per_head(P["b_q_norm"], DIL_WIDTH // HEAD_DIM)
    ksh = stage_fwd(f_qkprep, [kraw], [kgain], [cos, sin], [seg, seg_t], ROW_TILE, "l1_kprep")[0]
    q = stage_fwd(f_qkprep, [qraw], [qgain], [cos, sin], [seg, seg_t], ROW_TILE, "l1_qprep")[0]
    outs, lses = [], []
    for gi, (_, d) in enumerate(DIL_GROUPS):
        og, lg = dil_fwd(q, ksh, kvp, gi, d, f"l1_dil{gi}")
        outs.append(og)
        lses.append(lg)
    omix = stage_fwd(f_mix, outs + lses, [], [], [], ROW_TILE, "l1_mix")[0]
    ymem1, mem1_saved = mem_fwd(1, Cols(p1, MEM_WIDTH, DIL_WIDTH // MEM_WIDTH))
    ycat1 = jnp.concatenate([omix, ymem1], axis=1).astype(BF16)
    x3 = matmul(ycat1, P["b_w_out"][0], "nn", "l1_out", residual=x2)
    x4, ffn1_saved = ffn_fwd(1, x3)
    loss_part, dx4, dx4_b = loss_head(x4, tgt)

    dx3, dx3_b, gn1, gup1, gcw1, gcb1, gdown1 = ffn_bwd(1, x3, ffn1_saved, dx4, dx4_b)
    dycat1 = matmul(dx3_b, P["b_w_out"][0], "nt", "l1_out_dx")
    G["b_w_out"] = [matmul(ycat1, dx3_b, "tn", "l1_out_dw")]
    dqmem1, gmn1, gmw1, gmq1, gmk1 = mem_bwd(1, mem1_saved, Cols(dycat1, MEM_WIDTH, 1))
    dmix, _ = stage_bwd(f_mix, outs + lses, [], [], [], [Cols(dycat1, MEM_WIDTH, 0)], ROW_TILE, "l1_mix_bwd")
    dq, dk, dv = zip(*[dil_bwd(q, ksh, kvp, dmix[gi], dmix[3 + gi], gi, d, f"l1_dil{gi}_bwd")
                       for gi, (_, d) in enumerate(DIL_GROUPS)])
    dq, dk, dv = jnp.concatenate(dq, axis=1), jnp.concatenate(dk, axis=1), jnp.concatenate(dv, axis=1)
    (dqraw,), (g_bq,) = stage_bwd(f_qkprep, [qraw], [qgain], [cos, sin], [seg, seg_t], [dq], ROW_TILE, "l1_qprep_bwd")
    (dkraw,), (g_kk,) = stage_bwd(f_qkprep, [kraw], [kgain], [cos, sin], [seg, seg_t], [dk], ROW_TILE, "l1_kprep_bwd")
    g_bq, g_kk = _sum_heads(g_bq), _sum_heads(g_kk)
    dp1 = jnp.concatenate([dqraw, dqmem1], axis=1).astype(BF16)
    dkvp = jnp.concatenate([dkraw, dv], axis=1).astype(BF16)
    dh1 = matmul(dp1, P["b_w_in"][0], "nt", "l1_in_dx")
    G["b_w_in"] = [matmul(h1, dp1, "tn", "l1_in_dw")]
    dhk = matmul(dkvp, P["kv_w"][0], "nt", "l1_kv_dx")
    G["kv_w"] = [matmul(hk, dkvp, "tn", "l1_kv_dw")]
    (dx2,), (g_kvn, g_an1), (dx2_b,) = stage_bwd(f_rmsnorm2_res, [x2], [row(P["kv_norm"]), P["attn_norm"][1:2]], [], [],
                                                 [dhk, dh1, dx3], ROW_TILE, "l1_norm_bwd", bf16_copies=(0,))

    dx1, dx1_b, gn0, gup0, gcw0, gcb0, gdown0 = ffn_bwd(0, x1, ffn0_saved, dx2, dx2_b)
    dycat0 = matmul(dx1_b, P["a_w_out"][0], "nt", "l0_out_dx")
    G["a_w_out"] = [matmul(ycat0, dx1_b, "tn", "l0_out_dw")]
    dqmem0, gmn0, gmw0, gmq0, gmk0 = mem_bwd(0, mem0_saved, Cols(dycat0, MEM_WIDTH, RWKV_WIDTH // MEM_WIDTH))
    (dy_s, dr_a, dk_a, dv_a, dg), (g_lw, g_lb, g_rk) = stage_bwd(
        f_rwkv_post, [y_s, r, k2, v, g], post_ps, [], [seg, seg_t], [Cols(dycat0, RWKV_WIDTH, 0)], RW_TILE, "l0_rwkv_post_bwd")
    G["mem_w_kv"], G["ffn_w_up"], G["ffn_w_down"] = [gmw0, gmw1], [gup0, gup1], [gdown0, gdown1]
    (dr_b, dlw, dk_b, dv_b, dkk, db), G["_exchanged"] = rwkv_scan_bwd(*scan_in, h_states, dy_s,
                                                                      ex_grads(G) if ex_grads else Exchange())
    dpre, gpre = stage_bwd(f_rwkv_pre, pre_xs, pre_ps, [], [seg, seg_t],
                           [[dr_a, dr_b], dlw, [dk_a, dk_b], [dv_a, dv_b], dkk, db, dg], RW_TILE, "l0_rwkv_pre_bwd", with_prev=True)
    dp_rw = jnp.concatenate(dpre[:4], axis=1) + _shift_up(jnp.concatenate(dpre[4:], axis=1))
    dp0 = jnp.concatenate([dp_rw, dqmem0], axis=1).astype(BF16)
    dh0 = matmul(dp0, P["a_w_in"][0], "nt", "l0_in_dx")
    G["a_w_in"] = [matmul(h0, dp0, "tn", "l0_in_dw")]
    (dx0,), (g_an0,) = stage_bwd(f_rmsnorm_res, [x0], [P["attn_norm"][0:1]], [], [], [dh0, dx1], ROW_TILE, "l0_norm_bwd")

    G["attn_norm"] = jnp.concatenate([g_an0, g_an1], axis=0)
    G["a_mu"] = jnp.concatenate(gpre[:4], axis=1)
    G["a_w0"], G["a_w2"], G["a_a0"], G["a_a2"], G["a_g2"] = gpre[4], gpre[5][None, :64], gpre[6], gpre[7][None, 64:128], gpre[8][None, 128:]
    G["a_k_k"], G["a_k_a"] = gpre[9], gpre[10]
    G["a_r_k"] = g_rk.reshape(1, RWKV_HEADS, HEAD_DIM)
    G["a_lnx_w"], G["a_lnx_b"] = g_lw, g_lb
    G["kv_norm"], G["kv_k_norm"], G["b_q_norm"] = g_kvn.reshape(-1), g_kk.reshape(-1), g_bq
    G["mem_norm"] = jnp.concatenate([gmn0, gmn1], axis=0)
    G["mem_w_kv"] = [gmw0, gmw1]
    G["mem_q_norm"] = jnp.concatenate([gmq0, gmq1], axis=0)
    G["mem_k_norm"] = jnp.concatenate([gmk0, gmk1], axis=0)
    G["ffn_norm"] = jnp.concatenate([gn0, gn1], axis=0)
    G["ffn_w_up"] = [gup0, gup1]
    G["ffn_conv_w"] = jnp.stack([gcw0, gcw1])
    G["ffn_conv_b"] = jnp.concatenate([gcb0, gcb1], axis=0)
    G["ffn_w_down"] = [gdown0, gdown1]
    return loss_part, dx0, G


PARAMS = (("attn_norm", None), ("a_w_in", 2), ("a_mu", 1), ("a_w0", 1), ("a_w2", 2), ("a_a0", 1), ("a_a2", 2), ("a_g2", 2),
          ("a_k_k", 1), ("a_k_a", 1), ("a_r_k", None), ("a_lnx_w", 1), ("a_lnx_b", 1), ("a_w_out", 1), ("kv_norm", None),
          ("kv_w", 1), ("kv_k_norm", None), ("b_w_in", 1), ("b_q_norm", None), ("b_w_out", 2), ("mem_norm", None),
          ("mem_w_kv", 1), ("mem_q_norm", None), ("mem_k_norm", None), ("ffn_norm", None), ("ffn_w_up", 2),
          ("ffn_conv_w", 2), ("ffn_conv_b", None), ("ffn_w_down", 1))
BIG = ("a_w_in", "a_w_out", "kv_w", "b_w_in", "b_w_out", "mem_w_kv", "ffn_w_up", "ffn_w_down")
AXIS = dict(PARAMS)
SMALL = tuple(n for n, _ in PARAMS if n not in BIG)
SMALL_SHARDED = tuple(n for n in SMALL if AXIS[n] is not None)
PACK_QUANTUM = 256 * 128


def _from_shards(xs, axis):
    full = jnp.moveaxis(xs, 0, axis)
    sh = full.shape
    return full.reshape(sh[:axis] + (sh[axis] * sh[axis + 1],) + sh[axis + 2:])


def _to_shards(g, axis):
    sh = g.shape
    return jnp.moveaxis(g.reshape(sh[:axis] + (N_DEV, sh[axis] // N_DEV) + sh[axis + 1:]), axis, 0)


def _pack(parts, lead=0):
    ld = parts[0].shape[:lead]
    flat = jnp.concatenate([p.reshape(ld + (-1,)) for p in parts], axis=-1)
    pad = (-flat.shape[-1]) % PACK_QUANTUM
    flat = jnp.pad(flat, [(0, 0)] * lead + [(0, pad)])
    return flat.reshape(ld + (-1, 128))


def _unpack(packed, shapes, lead=0):
    ld = packed.shape[:lead]
    flat = packed.reshape(ld + (-1,))
    out, off = [], 0
    for s in shapes:
        n = math.prod(s)
        out.append(flat[..., off:off + n].reshape(ld + tuple(s)))
        off += n
    return out


def kernel(x, mem, attn_norm, a_w_in, a_mu, a_w0, a_w2, a_a0, a_a2, a_g2, a_k_k, a_k_a, a_r_k, a_lnx_w, a_lnx_b, a_w_out, kv_norm, kv_w, kv_k_norm, b_w_in, b_q_norm, b_w_out, mem_norm, mem_w_kv, mem_q_norm, mem_k_norm, ffn_norm, ffn_w_up, ffn_conv_w, ffn_conv_b, ffn_w_down, loss_target, m_attn_norm, m_a_w_in, m_a_mu, m_a_w0, m_a_w2, m_a_a0, m_a_a2, m_a_g2, m_a_k_k, m_a_k_a, m_a_r_k, m_a_lnx_w, m_a_lnx_b, m_a_w_out, m_kv_norm, m_kv_w, m_kv_k_norm, m_b_w_in, m_b_q_norm, m_b_w_out, m_mem_norm, m_mem_w_kv, m_mem_q_norm, m_mem_k_norm, m_ffn_norm, m_ffn_w_up, m_ffn_conv_w, m_ffn_conv_b, m_ffn_w_down, v_attn_norm, v_a_w_in, v_a_mu, v_a_w0, v_a_w2, v_a_a0, v_a_a2, v_a_g2, v_a_k_k, v_a_k_a, v_a_r_k, v_a_lnx_w, v_a_lnx_b, v_a_w_out, v_kv_norm, v_kv_w, v_kv_k_norm, v_b_w_in, v_b_q_norm, v_b_w_out, v_mem_norm, v_mem_w_kv, v_mem_q_norm, v_mem_k_norm, v_ffn_norm, v_ffn_w_up, v_ffn_conv_w, v_ffn_conv_b, v_ffn_w_down):
    names = [n for n, _ in PARAMS]
    vals = (attn_norm, a_w_in, a_mu, a_w0, a_w2, a_a0, a_a2, a_g2, a_k_k, a_k_a, a_r_k, a_lnx_w, a_lnx_b, a_w_out, kv_norm, kv_w, kv_k_norm, b_w_in, b_q_norm, b_w_out, mem_norm, mem_w_kv, mem_q_norm, mem_k_norm, ffn_norm, ffn_w_up, ffn_conv_w, ffn_conv_b, ffn_w_down)
    m_vals = (m_attn_norm, m_a_w_in, m_a_mu, m_a_w0, m_a_w2, m_a_a0, m_a_a2, m_a_g2, m_a_k_k, m_a_k_a, m_a_r_k, m_a_lnx_w, m_a_lnx_b, m_a_w_out, m_kv_norm, m_kv_w, m_kv_k_norm, m_b_w_in, m_b_q_norm, m_b_w_out, m_mem_norm, m_mem_w_kv, m_mem_q_norm, m_mem_k_norm, m_ffn_norm, m_ffn_w_up, m_ffn_conv_w, m_ffn_conv_b, m_ffn_w_down)
    v_vals = (v_attn_norm, v_a_w_in, v_a_mu, v_a_w0, v_a_w2, v_a_a0, v_a_a2, v_a_g2, v_a_k_k, v_a_k_a, v_a_r_k, v_a_lnx_w, v_a_lnx_b, v_a_w_out, v_kv_norm, v_kv_w, v_kv_k_norm, v_b_w_in, v_b_q_norm, v_b_w_out, v_mem_norm, v_mem_w_kv, v_mem_q_norm, v_mem_k_norm, v_ffn_norm, v_ffn_w_up, v_ffn_conv_w, v_ffn_conv_b, v_ffn_w_down)
    W, M, V = dict(zip(names, vals)), dict(zip(names, m_vals)), dict(zip(names, v_vals))
    me = 4 * lax.axis_index("x") + 2 * lax.axis_index("y") + lax.axis_index("c")
    layers = lambda D, n: [D[n]] if D[n].ndim == 2 else [D[n][i] for i in range(D[n].shape[0])]
    ax2 = lambda n: AXIS[n] - (W[n].ndim - 2)
    first = [("a_w_in", 0)]
    later = [(n, i) for n in BIG if n != "a_w_in" for i in range(len(layers(W, n)))]

    small_shapes = [W[n].shape for n in SMALL_SHARDED]
    got_w, got_small = exchange(Exchange(gathers=[W["a_w_in"][0].astype(BF16), _pack([W[n] for n in SMALL_SHARDED])]),
                                "gather_first")
    P = {n: W[n] for n in SMALL}
    P["a_w_in"] = [_from_shards(got_w, ax2("a_w_in"))]
    for n, s in zip(SMALL_SHARDED, _unpack(got_small, small_shapes, lead=1)):
        P[n] = _from_shards(s, AXIS[n])
    ex_weights = Exchange(gathers=[layers(W, n)[i].astype(BF16) for n, i in later])

    def weights_done(got):
        out = {}
        for (n, _), g in zip(later, got):
            out.setdefault(n, []).append(_from_shards(g, ax2(n)))
        return out

    slots = lambda G, n: jnp.stack([_to_shards(g, ax2(n)) for g in G[n]], axis=1)
    later_names = [n for n in BIG if n != "a_w_in"]
    ex_grads = lambda G: Exchange(scatters=[slots(G, n) for n in later_names])
    loss_part, dx0, G = _local_step(x[0], mem[0], loss_target[0], P, ex_weights, weights_done, ex_grads)
    loss = lax.psum(loss_part[0, 0], ("x", "y", "c"))
    gparts = dict(zip(later_names, G.pop("_exchanged")))
    got_gsmall, gparts["a_w_in"] = exchange(
        Exchange(gathers=[_pack([G[n] for n in SMALL])], scatters=[slots(G, "a_w_in")]), "exchange_last")

    results = {}
    for n in BIG:
        rows = lambda z: z.reshape((-1,) + z.shape[-1:])
        res = adamw(gparts[n].reshape((N_DEV, -1) + gparts[n].shape[-1:]), rows(W[n]), rows(M[n]), rows(V[n]), f"adamw_{n}")
        results[n] = [r.reshape(W[n].shape) for r in res]
    g_small = sum_parts(got_gsmall, "sum_small_grads")
    mine = []
    for n, g in zip(SMALL, _unpack(g_small, [G[n].shape for n in SMALL])):
        if AXIS[n] is not None:
            s = W[n].shape[AXIS[n]]
            g = lax.dynamic_slice_in_dim(g, me * s, s, axis=AXIS[n])
        mine.append(g)
    res = adamw(_pack(mine)[None], _pack([W[n] for n in SMALL]), _pack([M[n] for n in SMALL]), _pack([V[n] for n in SMALL]),
                "adamw_small")
    for n, parts in zip(SMALL, zip(*[_unpack(r, [W[n].shape for n in SMALL]) for r in res])):
        results[n] = list(parts)
    outs = [[results[n][j] for n in names] for j in range(4)]
    return (loss, dx0[None], *outs[0], *outs[1], *outs[2], *outs[3])
```

```python
import functools
import math

import jax
import jax.numpy as jnp
import numpy as np
from jax import lax
from jax.experimental import pallas as pl
from jax.experimental.pallas import tpu as pltpu

F32 = jnp.float32
BF16 = jnp.bfloat16
HI = lax.Precision.HIGHEST
H3 = lax.Precision.HIGH

N_DEV = 8
D_MODEL = 1024
HEAD_DIM = 64
N_MEM = 256
MEM_HEADS = 4
MEM_WIDTH = 256
RWKV_HEADS = 12
RWKV_WIDTH = 768
SHIFT_WIDTH = 2560
DIL_GROUPS = ((128, 1), (512, 4), (2048, 16))
DIL_BLOCK = 128
DIL_WIDTH = 768
D_FF = 2816
RMS_EPS = 1e-6
LNX_EPS = 64e-5
NEG_INF = -1e30
ROPE_THETA = 10000.0
ADAM_LR, ADAM_B1, ADAM_B2, ADAM_EPS, ADAM_WD, ADAM_STEP = 0.001, 0.9, 0.999, 1e-08, 0.01, 10

CHUNK = 64
SCAN_GROUPS_FWD, SCAN_GROUPS_BWD = 1, 1
MM_TILE_CAP = 1408
VMEM_LIMIT_V7X = 48 * 1024 * 1024


def _cparams(sem):
    return pltpu.CompilerParams(dimension_semantics=sem, vmem_limit_bytes=VMEM_LIMIT_V7X)


def _pick(n, cands):
    for c in cands:
        if n % c == 0:
            return c
    return n


def _tile(n, cap):
    if n <= cap:
        return n
    for d in range(cap - cap % 128, 0, -128):
        if n % d == 0:
            return d
    return n


def _dg(a, b, ca, cb, batch):
    dims = (((ca,), (cb,)), ((0,), (0,))) if batch else (((ca,), (cb,)), ((), ()))
    return lax.dot_general(a.astype(BF16), b.astype(BF16), dims, preferred_element_type=F32)


@jax.custom_vjp
def mm_nn(a, b):
    n = a.ndim
    return _dg(a, b, n - 1, n - 2, n == 3)


def _mm_nn_fwd(a, b):
    return mm_nn(a, b), (a, b)


def _mm_nn_bwd(res, g):
    a, b = res
    n = a.ndim
    return _dg(g, b, n - 1, n - 1, n == 3), _dg(a, g, n - 2, n - 2, n == 3)


mm_nn.defvjp(_mm_nn_fwd, _mm_nn_bwd)


@jax.custom_vjp
def mm_nt(a, b):
    n = a.ndim
    return _dg(a, b, n - 1, n - 1, n == 3)


def _mm_nt_fwd(a, b):
    return mm_nt(a, b), (a, b)


def _mm_nt_bwd(res, g):
    a, b = res
    n = a.ndim
    return _dg(g, b, n - 1, n - 2, n == 3), _dg(g, a, n - 2, n - 2, n == 3)


mm_nt.defvjp(_mm_nt_fwd, _mm_nt_bwd)


def mmh(a, b, precision=H3):
    n = a.ndim
    dims = (((n - 1,), (n - 2,)), ((0,), (0,))) if n == 3 else (((1,), (0,)), ((), ()))
    return lax.dot_general(a, b, dims, precision=precision, preferred_element_type=F32)


def mmh_nt(a, b):
    n = a.ndim
    dims = (((n - 1,), (n - 1,)), ((0,), (0,))) if n == 3 else (((1,), (1,)), ((), ()))
    return lax.dot_general(a, b, dims, precision=H3, preferred_element_type=F32)


def mmh_tn(a, b):
    n = a.ndim
    dims = (((n - 2,), (n - 2,)), ((0,), (0,))) if n == 3 else (((0,), (0,)), ((), ()))
    return lax.dot_general(a, b, dims, precision=H3, preferred_element_type=F32)


def matmul(a, b, mode, name, residual=None):
    out_dtype = BF16 if mode == "tn" else F32
    if mode == "nn":
        (M, K), (_, N) = a.shape, b.shape
    elif mode == "nt":
        (M, K), (N, _) = a.shape, b.shape
    else:
        (K, M), (_, N) = a.shape, b.shape
    tm = _tile(M, 2048 if mode == "nn" else MM_TILE_CAP)
    tn = _tile(N, 512 if mode == "nn" else MM_TILE_CAP)
    tk = _tile(K, MM_TILE_CAP)
    nk = K // tk
    if mode == "nn":
        a_spec = pl.BlockSpec((tm, tk), lambda i, j, k: (i, k))
        b_spec = pl.BlockSpec((tk, tn), lambda i, j, k: (k, j))
        dims = (((1,), (0,)), ((), ()))
    elif mode == "nt":
        a_spec = pl.BlockSpec((tm, tk), lambda i, j, k: (i, k))
        b_spec = pl.BlockSpec((tn, tk), lambda i, j, k: (j, k))
        dims = (((1,), (1,)), ((), ()))
    else:
        a_spec = pl.BlockSpec((tk, tm), lambda i, j, k: (k, i))
        b_spec = pl.BlockSpec((tk, tn), lambda i, j, k: (k, j))
        dims = (((0,), (0,)), ((), ()))
    o_spec = pl.BlockSpec((tm, tn), lambda i, j, k: (i, j))
    has_res = residual is not None

    def body(*refs):
        if has_res:
            a_ref, b_ref, r_ref, o_ref, acc_ref = refs
        else:
            a_ref, b_ref, o_ref, acc_ref = refs
        k = pl.program_id(2)

        @pl.when(k == 0)
        def _():
            acc_ref[...] = jnp.zeros_like(acc_ref)

        acc_ref[...] += lax.dot_general(a_ref[...].astype(BF16), b_ref[...].astype(BF16), dims,
                                        preferred_element_type=F32)

        @pl.when(k == nk - 1)
        def _():
            if has_res:
                o_ref[...] = (acc_ref[...] + r_ref[...]).astype(out_dtype)
            else:
                o_ref[...] = acc_ref[...].astype(out_dtype)

    ins = [a, b] + ([residual] if has_res else [])
    in_specs = [a_spec, b_spec] + ([o_spec] if has_res else [])
    return pl.pallas_call(
        body, grid=(M // tm, N // tn, nk), in_specs=in_specs, out_specs=o_spec,
        out_shape=jax.ShapeDtypeStruct((M, N), out_dtype), scratch_shapes=[pltpu.VMEM((tm, tn), F32)],
        compiler_params=_cparams(("parallel", "parallel", "arbitrary")), name=name)(*ins)


class Cols:
    def __init__(self, arr, width, idx):
        self.arr, self.width, self.idx = arr, width, idx


def _arr(x):
    return x.arr if isinstance(x, Cols) else x


def _shape(x):
    return x.arr.shape[:-1] + (x.width,) if isinstance(x, Cols) else x.shape


def _col(x):
    return x.idx if isinstance(x, Cols) else 0


def _tok_spec(x, tile):
    shape, col = _shape(x), _col(x)
    return pl.BlockSpec(shape[:-2] + (tile, shape[-1]), lambda i: (0,) * (len(shape) - 2) + (i, col))


def _full_spec(x):
    shape, col = _shape(x), _col(x)
    return pl.BlockSpec(shape, lambda i: (0,) * (len(shape) - 1) + (col,))


def _halo_spec(x, tile):
    shape, col = _shape(x), _col(x)
    return pl.BlockSpec((8, shape[-1]), lambda i: (jnp.maximum(i * (tile // 8) - 1, 0), col))


def _blk(x, tile):
    shape = _shape(x)
    return jax.ShapeDtypeStruct(shape[:-2] + (tile, shape[-1]), _arr(x).dtype)


def _prev_rows(x, halo):
    rows = lax.broadcasted_iota(jnp.int32, (x.shape[0], 1), 0)
    before = jnp.where(pl.program_id(0) > 0, halo[7:8], 0.0)
    return jnp.where(rows == 0, before, pltpu.roll(x, 1, 0))


def stage_fwd(f, xs, ps, cts, cfs, tile, name, out_dtypes=None, with_prev=False):
    xs, ps, cts, cfs = list(xs), list(ps), list(cts), list(cfs)
    halos = xs if with_prev else []
    nx, nh, nct, np_ = len(xs), len(halos), len(cts), len(ps)
    T = _shape(xs[0])[-2]
    blk = [_blk(x, tile) for x in xs]
    out_avals = jax.eval_shape(f, *blk, *(blk if with_prev else []), *[_blk(p, _shape(p)[-2]) for p in ps],
                               *[_blk(c, tile) for c in cts], *[_blk(c, _shape(c)[-2]) for c in cfs])
    if out_dtypes is None:
        out_dtypes = [o.dtype for o in out_avals]
    out_shape = [jax.ShapeDtypeStruct(o.shape[:-2] + (T, o.shape[-1]), dt) for o, dt in zip(out_avals, out_dtypes)]
    n_in = nx + nh + nct + np_ + len(cfs)

    def body(*refs):
        vals = [r[...] for r in refs[:n_in]]
        xv, hv, rest = vals[:nx], vals[nx:nx + nh], vals[nx + nh:]
        ctv, pv, cfv = rest[:nct], rest[nct:nct + np_], rest[nct + np_:]
        prev = [_prev_rows(x, h) for x, h in zip(xv, hv)]
        res = f(*xv, *prev, *pv, *ctv, *cfv)
        for o_ref, r in zip(refs[n_in:], res):
            o_ref[...] = r.astype(o_ref.dtype)

    return pl.pallas_call(
        body, grid=(T // tile,),
        in_specs=([_tok_spec(x, tile) for x in xs] + [_halo_spec(x, tile) for x in halos] + [_tok_spec(c, tile) for c in cts]
                  + [_full_spec(p) for p in ps + cfs]),
        out_specs=[_tok_spec(o, tile) for o in out_shape], out_shape=out_shape,
        compiler_params=_cparams(("parallel",)), name=name)(*[_arr(a) for a in xs + halos + cts + ps + cfs])


def stage_bwd(f, xs, ps, cts, cfs, gs, tile, name, bf16_copies=(), with_prev=False):
    xs, ps, cts, cfs = list(xs), list(ps), list(cts), list(cfs)
    gs = [list(g) if isinstance(g, (list, tuple)) else [g] for g in gs]
    g_flat = [a for g in gs for a in g]
    halos = xs if with_prev else []
    nx, nh, nct, ng, np_ = len(xs), len(halos), len(cts), len(g_flat), len(ps)
    T = _shape(xs[0])[-2]
    dx_like = xs + halos
    out_shape = ([jax.ShapeDtypeStruct(_shape(x), F32) for x in dx_like] + [jax.ShapeDtypeStruct(_shape(p), F32) for p in ps]
                 + [jax.ShapeDtypeStruct(_shape(xs[i]), BF16) for i in bf16_copies])
    n_in = nx + nh + nct + ng + np_ + len(cfs)
    ndx = nx + nh

    def body(*refs):
        vals = [r[...] for r in refs[:n_in]]
        outs = refs[n_in:]
        xv, hv, rest = vals[:nx], vals[nx:nx + nh], vals[nx + nh:]
        ctv, gparts, pv, cfv = rest[:nct], rest[nct:nct + ng], rest[nct + ng:nct + ng + np_], rest[nct + ng + np_:]
        gv = []
        for g in gs:
            gv.append(functools.reduce(lambda a, b: a + b, gparts[:len(g)]))
            gparts = gparts[len(g):]
        prev = [_prev_rows(x, h) for x, h in zip(xv, hv)]
        _, vjp = jax.vjp(lambda *xp: f(*xp, *ctv, *cfv), *xv, *prev, *pv)
        d = vjp(tuple(gv))
        for o_ref, r in zip(outs[:ndx], d[:ndx]):
            o_ref[...] = r
        for o_ref, i in zip(outs[ndx + np_:], bf16_copies):
            o_ref[...] = d[i].astype(BF16)

        @pl.when(pl.program_id(0) == 0)
        def _():
            for o_ref in outs[ndx:ndx + np_]:
                o_ref[...] = jnp.zeros_like(o_ref)

        for o_ref, r in zip(outs[ndx:ndx + np_], d[ndx:]):
            o_ref[...] += r

    plain = lambda x: jax.ShapeDtypeStruct(_shape(x), F32)
    res = pl.pallas_call(
        body, grid=(T // tile,),
        in_specs=([_tok_spec(x, tile) for x in xs] + [_halo_spec(x, tile) for x in halos]
                  + [_tok_spec(c, tile) for c in cts + g_flat] + [_full_spec(p) for p in ps + cfs]),
        out_specs=([_tok_spec(plain(x), tile) for x in dx_like] + [_full_spec(plain(p)) for p in ps]
                   + [_tok_spec(plain(xs[i]), tile) for i in bf16_copies]), out_shape=out_shape,
        compiler_params=_cparams(("arbitrary",)), name=name)(*[_arr(a) for a in xs + halos + cts + g_flat + ps + cfs])
    if bf16_copies:
        return list(res[:ndx]), list(res[ndx:ndx + np_]), list(res[ndx + np_:])
    return list(res[:ndx]), list(res[ndx:])


def _rms(x, g, eps=RMS_EPS):
    return x * lax.rsqrt(jnp.mean(x * x, axis=-1, keepdims=True) + eps) * g


def f_rmsnorm(x, g):
    return (_rms(x, g),)


def f_rmsnorm_res(x, g):
    return _rms(x, g), x


def f_rmsnorm2(x, g1, g2):
    n = x * lax.rsqrt(jnp.mean(x * x, axis=-1, keepdims=True) + RMS_EPS)
    return n * g1, n * g2


def f_rmsnorm2_res(x, g1, g2):
    return f_rmsnorm2(x, g1, g2) + (x,)


def _sigmoid(x):
    return 1.0 / (1.0 + jnp.exp(-x))


def _softplus(x):
    return jnp.maximum(x, 0.0) + jnp.log(1.0 + jnp.exp(-jnp.abs(x)))


def f_rwkv_pre(pr, pk, pv, pl_, qr, qk, qv, ql, mu_r, mu_k, mu_v, mu_l, w0, w2, a0, a2, g2, k_k, k_a, seg, seg_t):
    xr = pr + (qr - pr) * mu_r
    xk = pk + (qk - pk) * mu_k
    xv = pv + (qv - pv) * mu_v
    xl = pl_ + (ql - pl_) * mu_l
    w_log = -_softplus(-(w0 + mm_nn(jnp.tanh(xl), w2))) - 0.5
    lw = -jnp.exp(w_log)
    a = _sigmoid(a0 + mm_nn(xl, a2))
    g = mm_nn(_sigmoid(xl), g2)
    kkr = xk * k_k
    inv = lax.rsqrt(jnp.maximum(mmh(kkr * kkr, seg), 1e-24))
    kk = kkr * mmh(inv, seg_t)
    k2 = xk * (1.0 + (a - 1.0) * k_a)
    return xr, lw, k2, xv, kk, kk * a, g


def f_rwkv_post(y, r, k2, v, g, lnx_w, lnx_b, r_k, seg, seg_t):
    inv_n = 1.0 / HEAD_DIM
    m = mmh(mmh(y, seg) * inv_n, seg_t)
    yc = y - m
    rstd = lax.rsqrt(mmh(yc * yc, seg) * inv_n + LNX_EPS)
    yn = yc * mmh(rstd, seg_t) * lnx_w + lnx_b
    bonus = mmh(mmh(r * k2 * r_k, seg), seg_t) * v
    return ((yn + bonus) * g,)


def _headnorm(z, g, seg, seg_t):
    ms = mmh(z * z, seg) * (1.0 / HEAD_DIM)
    return z * mmh(lax.rsqrt(ms + RMS_EPS), seg_t) * g


def f_headnorm(z, g, seg, seg_t):
    return (_headnorm(z, g, seg, seg_t),)


def _rot_half(z):
    w = z.shape[1]
    half = HEAD_DIM // 2
    lane = lax.broadcasted_iota(jnp.int32, (1, w), 1)
    return jnp.where((lane & (HEAD_DIM - 1)) < half, -pltpu.roll(z, w - half, 1), pltpu.roll(z, half, 1))


@jax.custom_vjp
def _rotate_half(z):
    return _rot_half(z)


_rotate_half.defvjp(lambda z: (_rot_half(z), None), lambda _, g: (-_rot_half(g),))


def f_qkprep(z, g, cos, sin, seg, seg_t):
    zn = _headnorm(z, g, seg, seg_t)
    return (zn * cos + _rotate_half(zn) * sin,)


def _head_mask(width, h):
    lane = lax.broadcasted_iota(jnp.int32, (1, width), 1)
    return jnp.where((lane >> 6) == h, jnp.ones((), F32), 0.0)


def f_memattn(q, k, v, q_norm, seg, seg_t):
    qn = _headnorm(q, q_norm, seg, seg_t)
    out = jnp.zeros_like(q)
    for h in range(MEM_HEADS):
        m = _head_mask(MEM_WIDTH, h)
        s = mm_nt(qn * m, k) * (1.0 / math.sqrt(HEAD_DIM))
        s = s - jnp.max(s, axis=-1, keepdims=True)
        p = jnp.exp(s)
        p = p / jnp.sum(p, axis=-1, keepdims=True)
        out = out + mm_nn(p, v) * m
    return (out,)


def f_mix(o1, o2, o3, l1, l2, l3):
    mx = jnp.maximum(jnp.maximum(l1, l2), l3)
    e1, e2, e3 = jnp.exp(l1 - mx), jnp.exp(l2 - mx), jnp.exp(l3 - mx)
    return ((e1 * o1 + e2 * o2 + e3 * o3) / (e1 + e2 + e3),)


def _chunk_masks(L):
    t = lax.broadcasted_iota(jnp.int32, (L, L), 0)
    s = lax.broadcasted_iota(jnp.int32, (L, L), 1)
    return t, s


def _unit_lower_inverse(a):
    L = a.shape[-1]
    t, s = _chunk_masks(L)
    one = jnp.ones((), F32)
    blk = lambda sh: jnp.where((t >> sh) == (s >> sh), one, 0.0)
    n0 = a * blk(3)
    x = jnp.where(t == s, one, 0.0) - n0
    n2 = mmh(n0, n0)
    x = x + mmh(x, n2)
    x = x + mmh(x, mmh(n2, n2))
    for sh in (3, 4, 5):
        if (1 << sh) >= L:
            break
        off = a * (blk(sh + 1) - blk(sh))
        x = x - mmh(x, mmh(off, x))
    return x


@jax.custom_vjp
def _inverse_known(a, x):
    return x


def _inverse_known_fwd(a, x):
    return x, x


def _inverse_known_bwd(x, dx):
    return -mmh_nt(mmh_tn(x, dx), x), jnp.zeros_like(x)


_inverse_known.defvjp(_inverse_known_fwd, _inverse_known_bwd)


def _running_sum(x, reverse):
    L = x.shape[1]
    pos = lax.broadcasted_iota(jnp.int32, (1, L, 1), 1)
    step = 1
    while step < L:
        if reverse:
            x = x + jnp.where(pos < L - step, pltpu.roll(x, L - step, 1), 0.0)
        else:
            x = x + jnp.where(pos >= step, pltpu.roll(x, step, 1), 0.0)
        step *= 2
    return x


@jax.custom_vjp
def _cumsum_tokens(x):
    return _running_sum(x, False)


_cumsum_tokens.defvjp(lambda x: (_running_sum(x, False), None), lambda _, g: (_running_sum(g, True),))


def f_rwkv_chunk(s0, r, lw, k, v, kk, b, x_known=None):
    H, L, _ = r.shape
    t, s = _chunk_masks(L)
    one = jnp.ones((), F32)
    incl = jnp.where(t >= s, one, 0.0)
    strict = jnp.where(t > s, one, 0.0)
    cum = _cumsum_tokens(lw)
    w_in = jnp.exp(cum)
    w_ex = jnp.exp(cum - lw)
    w_inv = jnp.exp(-cum)
    rt, kkt, kt, bt = r * w_in, kk * w_ex, k * w_inv, b * w_inv
    a_b = mmh_nt(kkt, bt) * strict
    a_k = mmh_nt(kkt, kt) * strict
    m_k = mmh_nt(rt, kt) * incl
    m_b = mmh_nt(rt, bt) * incl
    x = _unit_lower_inverse(a_b) if x_known is None else _inverse_known(a_b, x_known)
    u = mmh(x, mmh_nt(kkt, s0) + mmh(a_k, v))
    y = mmh_nt(rt, s0) + mmh(m_k, v) - mmh(m_b, u)
    w_last = jnp.exp(jnp.sum(lw, axis=1, keepdims=True))
    s1 = (s0 + mmh_tn(v, kt) - mmh_tn(u, bt)) * w_last
    return y, s1, x


def _ex_split(ex, refs, n_in, n_out):
    n = ex.n
    ins, ex_in = refs[:n_in], refs[n_in:n_in + n]
    outs, ex_out = refs[n_in + n:n_in + n + n_out], refs[n_in + n + n_out:n_in + 2 * n + n_out]
    rest = refs[n_in + 2 * n + n_out:]
    return ins, outs, rest[:len(rest) - 3], (ex_in, ex_out) + tuple(rest[len(rest) - 3:])


def _split_heads(x):
    return jnp.stack([x[:, h * HEAD_DIM:(h + 1) * HEAD_DIM] for h in range(x.shape[1] // HEAD_DIM)], axis=0)


def _merge_heads(x):
    return jnp.concatenate([x[h] for h in range(x.shape[0])], axis=1)


def rwkv_scan_fwd(r, lw, k, v, kk, b, ex):
    T, N = r.shape[0], HEAD_DIM
    H = r.shape[1] // N
    groups = SCAN_GROUPS_FWD
    nc, hg = T // CHUNK, H // groups
    seq = pl.BlockSpec((CHUNK, hg * N), lambda g, c: (c, g))

    def body(*refs):
        (r_ref, lw_ref, k_ref, v_ref, kk_ref, b_ref), (y_ref, hs_ref, xs_ref), (h_scr,), ex_refs = _ex_split(ex, refs, 6, 3)
        g, c = pl.program_id(0), pl.program_id(1)

        @pl.when(jnp.logical_and(g == 0, c == 0))
        def _():
            ex.start(*ex_refs)

        @pl.when(c == 0)
        def _():
            h_scr[...] = jnp.zeros_like(h_scr)

        h0 = h_scr[...]
        hs_ref[0] = h0
        y, h1, x = f_rwkv_chunk(h0, *[_split_heads(z[...]) for z in (r_ref, lw_ref, k_ref, v_ref, kk_ref, b_ref)])
        y_ref[...] = _merge_heads(y)
        xs_ref[0] = x
        h_scr[...] = h1

        @pl.when(jnp.logical_and(g == groups - 1, c == (3 * nc) // 4))
        def _():
            ex.forward(*ex_refs)

        @pl.when(jnp.logical_and(g == groups - 1, c == nc - 1))
        def _():
            ex.wait(*ex_refs)

    res = pl.pallas_call(
        body, grid=(groups, nc), in_specs=[seq] * 6 + [_ANY] * ex.n,
        out_specs=[seq, pl.BlockSpec((1, hg, N, N), lambda g, c: (c, g, 0, 0)),
                   pl.BlockSpec((1, hg, CHUNK, CHUNK), lambda g, c: (c, g, 0, 0))] + [_ANY] * ex.n,
        out_shape=[jax.ShapeDtypeStruct((T, H * N), F32), jax.ShapeDtypeStruct((nc, H, N, N), F32),
                   jax.ShapeDtypeStruct((nc, H, CHUNK, CHUNK), F32)] + ex.out_shape(),
        scratch_shapes=[pltpu.VMEM((hg, N, N), F32)] + ex.scratch(),
        compiler_params=_cparams(("arbitrary", "arbitrary")), name="rwkv_scan_fwd")(r, lw, k, v, kk, b, *ex.operands())
    return res[0], (res[1], res[2]), list(res[3:])


def rwkv_scan_bwd(r, lw, k, v, kk, b, saved, dy, ex):
    T, N = r.shape[0], HEAD_DIM
    H = r.shape[1] // N
    groups = SCAN_GROUPS_BWD
    nc, hg = T // CHUNK, H // groups
    seq = pl.BlockSpec((CHUNK, hg * N), lambda g, c: (nc - 1 - c, g))
    state = pl.BlockSpec((1, hg, N, N), lambda g, c: (nc - 1 - c, g, 0, 0))

    def body(*refs):
        (r_ref, lw_ref, k_ref, v_ref, kk_ref, b_ref, hs_ref, xs_ref, dy_ref), outs, (dh_scr,), ex_refs = _ex_split(ex, refs, 9, 6)
        g, c = pl.program_id(0), pl.program_id(1)

        @pl.when(jnp.logical_and(g == 0, c == 0))
        def _():
            ex.start(*ex_refs)

        @pl.when(c == 0)
        def _():
            dh_scr[...] = jnp.zeros_like(dh_scr)

        x_known = xs_ref[0]
        _, vjp = jax.vjp(lambda *a: f_rwkv_chunk(*a, x_known=x_known)[:2], hs_ref[0],
                         *[_split_heads(z[...]) for z in (r_ref, lw_ref, k_ref, v_ref, kk_ref, b_ref)])
        d = vjp((_split_heads(dy_ref[...]), dh_scr[...]))
        dh_scr[...] = d[0]
        for o_ref, dz in zip(outs, d[1:]):
            o_ref[...] = _merge_heads(dz)

        @pl.when(jnp.logical_and(g == groups - 1, c == nc - 1))
        def _():
            ex.forward(*ex_refs)
            ex.wait(*ex_refs)

    res = pl.pallas_call(
        body, grid=(groups, nc),
        in_specs=[seq] * 6 + [state, state, seq] + [_ANY] * ex.n,
        out_specs=[seq] * 6 + [_ANY] * ex.n, out_shape=[jax.ShapeDtypeStruct((T, H * N), F32)] * 6 + ex.out_shape(),
        scratch_shapes=[pltpu.VMEM((hg, N, N), F32)] + ex.scratch(),
        compiler_params=_cparams(("arbitrary", "arbitrary")), name="rwkv_scan_bwd")(r, lw, k, v, kk, b, *saved, dy, *ex.operands())
    return list(res[:6]), list(res[6:])


GROUP_COLS = 4 * HEAD_DIM


def _f_dilattn(has_prev, q, kc, kp, vc, vp):
    scale = 1.0 / math.sqrt(HEAD_DIM)
    i = lax.broadcasted_iota(jnp.int32, (DIL_BLOCK, DIL_BLOCK), 0)
    j = lax.broadcasted_iota(jnp.int32, (DIL_BLOCK, DIL_BLOCK), 1)
    o, l = jnp.zeros_like(q), jnp.zeros_like(q)
    for h in range(q.shape[1] // HEAD_DIM):
        m = _head_mask(q.shape[1], h)
        sc = jnp.where(j <= i, mm_nt(q * m, kc) * scale, NEG_INF)
        sp = jnp.where(jnp.logical_and(i <= j, has_prev), mm_nt(q * m, kp) * scale, NEG_INF)
        mx = jnp.maximum(jnp.max(sc, axis=-1, keepdims=True), jnp.max(sp, axis=-1, keepdims=True))
        pc, pp = jnp.exp(sc - mx), jnp.exp(sp - mx)
        den = jnp.sum(pc, axis=-1, keepdims=True) + jnp.sum(pp, axis=-1, keepdims=True)
        o = o + (mm_nn(pc, vc) + mm_nn(pp, vp)) / den * m
        l = l + (mx + jnp.log(den)) * m
    return o, l


def _dil_specs(gi, d):
    parts = 1 if d == 1 else 2
    blk = (DIL_BLOCK * d, GROUP_COLS // parts)
    at = lambda col: (lambda p, n: (n, col * parts + p))
    before = lambda col: (lambda p, n: (jnp.maximum(n - 1, 0), col * parts + p))
    v0 = DIL_WIDTH // GROUP_COLS + gi
    q = pl.BlockSpec(blk, at(gi))
    kc, kp = pl.BlockSpec(blk, at(gi)), pl.BlockSpec(blk, before(gi))
    vc, vp = pl.BlockSpec(blk, at(v0)), pl.BlockSpec(blk, before(v0))
    out = pl.BlockSpec(blk, at(0))
    together = min(d, 2)
    return (q, kc, kp, vc, vp, out), parts, together


def _residue_rows(r, d):
    return pl.ds(r, DIL_BLOCK, stride=d) if d > 1 else pl.ds(0, DIL_BLOCK)


def dil_fwd(q, k, kv, gi, d, name):
    T = q.shape[0]
    (qs, kc, kp, vc, vp, out), parts, together = _dil_specs(gi, d)

    def body(q_ref, kc_ref, kp_ref, vc_ref, vp_ref, o_ref, l_ref):
        has_prev = pl.program_id(1) > 0

        def residues(it, carry):
            rows = [_residue_rows(it * together + a, d) for a in range(together)]
            ins = [[ref[rw, :] for ref in (q_ref, kc_ref, kp_ref, vc_ref, vp_ref)] for rw in rows]
            res = [_f_dilattn(has_prev, *x) for x in ins]
            for rw, (o, l) in zip(rows, res):
                o_ref[rw, :] = o
                l_ref[rw, :] = l
            return carry

        lax.fori_loop(0, d // together, residues, 0)

    shape = jax.ShapeDtypeStruct((T, 4 * HEAD_DIM), F32)
    return pl.pallas_call(
        body, grid=(parts, T // (DIL_BLOCK * d)), in_specs=[qs, kc, kp, vc, vp], out_specs=[out, out], out_shape=[shape, shape],
        compiler_params=_cparams(("parallel", "parallel")), name=name)(q, k, k, kv, kv)


def dil_bwd(q, k, kv, do, dl, gi, d, name):
    T = q.shape[0]
    (qs, kc, kp, vc, vp, out), parts, together = _dil_specs(gi, d)

    def body(q_ref, kc_ref, kp_ref, vc_ref, vp_ref, do_ref, dl_ref, *outs):
        f = functools.partial(_f_dilattn, pl.program_id(1) > 0)

        def residues(it, carry):
            rows = [_residue_rows(it * together + a, d) for a in range(together)]
            ins = [[ref[rw, :] for ref in (q_ref, kc_ref, kp_ref, vc_ref, vp_ref, do_ref, dl_ref)] for rw in rows]
            res = [jax.vjp(f, *x[:5])[1]((x[5], x[6])) for x in ins]
            for rw, gs in zip(rows, res):
                for o_ref, g in zip(outs, gs):
                    o_ref[rw, :] = g
            return carry

        lax.fori_loop(0, d // together, residues, 0)

    shape = jax.ShapeDtypeStruct((T, 4 * HEAD_DIM), F32)
    dq, dkc, dkp, dvc, dvp = pl.pallas_call(
        body, grid=(parts, T // (DIL_BLOCK * d)), in_specs=[qs, kc, kp, vc, vp, out, out], out_specs=[out] * 5, out_shape=[shape] * 5,
        compiler_params=_cparams(("parallel", "parallel")), name=name)(q, k, k, kv, kv, do, dl)

    def own_plus_next(c, p):
        return c + jnp.concatenate([p[DIL_BLOCK * d:], jnp.zeros_like(p[:DIL_BLOCK * d])], axis=0)

    return dq, own_plus_next(dkc, dkp), own_plus_next(dvc, dvp)


CONV_TILE = 256


def _conv3(u, h6, h7, w, b):
    rows = lax.broadcasted_iota(jnp.int32, (u.shape[0], 1), 0)
    s1 = jnp.where(rows == 0, h7, pltpu.roll(u, 1, 0))
    s2 = jnp.where(rows == 0, h6, jnp.where(rows == 1, h7, pltpu.roll(u, 2, 0)))
    return b + w[0:1] * s2 + w[1:2] * s1 + w[2:3] * u, s1, s2


def _conv_halves(u_ref, h_ref, cw_ref, cb_ref):
    F = D_FF
    first = pl.program_id(0) > 0
    res = []
    for lo in (0, F):
        h = h_ref[:, lo:lo + F]
        h6 = jnp.where(first, h[6:7], 0.0)
        h7 = jnp.where(first, h[7:8], 0.0)
        u = u_ref[:, lo:lo + F]
        res.append((u,) + _conv3(u, h6, h7, cw_ref[:, lo:lo + F], cb_ref[:, lo:lo + F]))
    return res


def _halo_before(C):
    return pl.BlockSpec((8, C), lambda i: (jnp.maximum(i * (CONV_TILE // 8) - 1, 0), 0))


def convgate_fwd(u, cw, cb, name):
    T, C = u.shape
    F = C // 2

    def body(u_ref, h_ref, cw_ref, cb_ref, z_ref):
        (_, cg, _, _), (_, cv, _, _) = _conv_halves(u_ref, h_ref, cw_ref, cb_ref)
        z_ref[...] = (cg * _sigmoid(cg) * cv).astype(BF16)

    return pl.pallas_call(
        body, grid=(T // CONV_TILE,),
        in_specs=[pl.BlockSpec((CONV_TILE, C), lambda i: (i, 0)), _halo_before(C), _full_spec(cw), _full_spec(cb)],
        out_specs=pl.BlockSpec((CONV_TILE, F), lambda i: (i, 0)), out_shape=jax.ShapeDtypeStruct((T, F), BF16),
        compiler_params=_cparams(("parallel",)), name=name)(u, u, cw, cb)


def convgate_bwd(u, cw, cb, dz, name):
    T, C = u.shape
    F = C // 2
    n = T // CONV_TILE
    E = CONV_TILE + 8

    def body(u_ref, hb_ref, ha_ref, cw_ref, cb_ref, dz_ref, dza_ref, du_ref, dcw_ref, dcb_ref):
        i = pl.program_id(0)
        dze = jnp.concatenate([dz_ref[...], jnp.where(i < n - 1, dza_ref[...], 0.0)], axis=0)

        @pl.when(i == 0)
        def _():
            dcw_ref[...] = jnp.zeros_like(dcw_ref)
            dcb_ref[...] = jnp.zeros_like(dcb_ref)

        halves = []
        for lo in (0, F):
            sl = slice(lo, lo + F)
            hb = hb_ref[:, sl]
            ue = jnp.concatenate([u_ref[:, sl], ha_ref[:, sl]], axis=0)
            c, s1, s2 = _conv3(ue, jnp.where(i > 0, hb[6:7], 0.0), jnp.where(i > 0, hb[7:8], 0.0), cw_ref[:, sl], cb_ref[:, sl])
            halves.append((sl, ue, c, s1, s2))
        (_, _, cg, _, _), (_, _, cv, _, _) = halves
        sg = _sigmoid(cg)
        dcs = (dze * cv * sg * (1.0 + cg * (1.0 - sg)), dze * cg * sg)
        for (sl, ue, _, s1, s2), dc in zip(halves, dcs):
            own = lambda z: z[:CONV_TILE]
            dcb_ref[:, sl] += jnp.sum(own(dc), axis=0, keepdims=True)
            dcw_ref[0:1, sl] += jnp.sum(own(dc * s2), axis=0, keepdims=True)
            dcw_ref[1:2, sl] += jnp.sum(own(dc * s1), axis=0, keepdims=True)
            dcw_ref[2:3, sl] += jnp.sum(own(dc * ue), axis=0, keepdims=True)
            du = cw_ref[2:3, sl] * dc + cw_ref[1:2, sl] * pltpu.roll(dc, E - 1, 0) + cw_ref[0:1, sl] * pltpu.roll(dc, E - 2, 0)
            du_ref[:, sl] = own(du).astype(BF16)

    after = lambda w: pl.BlockSpec((8, w), lambda i: (jnp.minimum((i + 1) * (CONV_TILE // 8), T // 8 - 1), 0))
    return pl.pallas_call(
        body, grid=(n,),
        in_specs=[pl.BlockSpec((CONV_TILE, C), lambda i: (i, 0)), _halo_before(C), after(C), _full_spec(cw), _full_spec(cb),
                  pl.BlockSpec((CONV_TILE, F), lambda i: (i, 0)), after(F)],
        out_specs=[pl.BlockSpec((CONV_TILE, C), lambda i: (i, 0)), _full_spec(cw), _full_spec(cb)],
        out_shape=[jax.ShapeDtypeStruct((T, C), BF16), jax.ShapeDtypeStruct(cw.shape, F32), jax.ShapeDtypeStruct(cb.shape, F32)],
        compiler_params=_cparams(("arbitrary",)), name=name)(u, u, u, cw, cb, dz, dz)


def loss_head(y, tgt):
    T, D = y.shape
    tile = ROW_TILE

    def body(y_ref, t_ref, l_ref, d_ref, db_ref):
        d = y_ref[...] - t_ref[...]
        d_ref[...] = d * (1.0 / D)
        db_ref[...] = (d * (1.0 / D)).astype(BF16)

        @pl.when(pl.program_id(0) == 0)
        def _():
            l_ref[...] = jnp.zeros_like(l_ref)

        l_ref[...] += (0.5 / D) * jnp.sum(d * d)

    row = pl.BlockSpec((tile, D), lambda i: (i, 0))
    return pl.pallas_call(
        body, grid=(T // tile,), in_specs=[row, row], out_specs=[pl.BlockSpec((8, 128), lambda i: (0, 0)), row, row],
        out_shape=[jax.ShapeDtypeStruct((8, 128), F32), jax.ShapeDtypeStruct((T, D), F32), jax.ShapeDtypeStruct((T, D), BF16)],
        compiler_params=_cparams(("arbitrary",)), name="loss_head")(y, tgt)


def sum_parts(parts, name):
    S, R, C = parts.shape
    tile = _pick(R, (256, 128, 64, 32, 16, 8))

    def body(p_ref, o_ref):
        acc = p_ref[0]
        for s in range(1, S):
            acc = acc + p_ref[s]
        o_ref[...] = acc

    return pl.pallas_call(
        body, grid=(R // tile,), in_specs=[pl.BlockSpec((S, tile, C), lambda i: (0, i, 0))],
        out_specs=pl.BlockSpec((tile, C), lambda i: (i, 0)), out_shape=jax.ShapeDtypeStruct((R, C), F32),
        compiler_params=_cparams(("parallel",)), name=name)(parts)


def adamw(gparts, w, m, v, name):
    S, R, C = gparts.shape
    tile = _pick(R, (256, 128, 64, 32, 16, 8))
    c1 = 1.0 / (1.0 - ADAM_B1 ** ADAM_STEP)
    c2 = 1.0 / (1.0 - ADAM_B2 ** ADAM_STEP)

    def body(g_ref, w_ref, m_ref, v_ref, go_ref, d_ref, mo_ref, vo_ref):
        g = g_ref[0].astype(F32)
        for s in range(1, S):
            g = g + g_ref[s].astype(F32)
        m1 = ADAM_B1 * m_ref[...] + (1.0 - ADAM_B1) * g
        v1 = ADAM_B2 * v_ref[...] + (1.0 - ADAM_B2) * (g * g)
        go_ref[...] = g
        mo_ref[...] = m1
        vo_ref[...] = v1
        d_ref[...] = -ADAM_LR * ((m1 * c1) / (jnp.sqrt(v1 * c2) + ADAM_EPS) + ADAM_WD * w_ref[...])

    row = pl.BlockSpec((tile, C), lambda i: (i, 0))
    return pl.pallas_call(
        body, grid=(R // tile,), in_specs=[pl.BlockSpec((S, tile, C), lambda i: (0, i, 0)), row, row, row],
        out_specs=[row] * 4, out_shape=[jax.ShapeDtypeStruct((R, C), F32)] * 4,
        compiler_params=_cparams(("parallel",)), name=name)(gparts, w, m, v)


def _peers():
    x, y, c = lax.axis_index("x"), lax.axis_index("y"), lax.axis_index("c")
    peers = []
    for k in range(1, N_DEV):
        px = 1 - x if k & 4 else x
        py = 1 - y if k & 2 else y
        pc = 1 - c if k & 1 else c
        peers.append(((px, py, pc), 4 * px + 2 * py + pc))
    return 4 * x + 2 * y + c, peers


_ANY = pl.BlockSpec(memory_space=pl.ANY)


class Exchange:
    def __init__(self, gathers=(), scatters=()):
        self.gathers, self.scatters = list(gathers), list(scatters)
        self.n = len(self.gathers) + len(self.scatters)

    def operands(self):
        return self.gathers + self.scatters

    def out_shape(self):
        return ([jax.ShapeDtypeStruct((N_DEV,) + x.shape, x.dtype) for x in self.gathers]
                + [jax.ShapeDtypeStruct(x.shape, x.dtype) for x in self.scatters])

    def scratch(self):
        n = max(self.n, 1)
        return [pltpu.SemaphoreType.DMA((7 * n,)), pltpu.SemaphoreType.DMA((7 * n,)), pltpu.SemaphoreType.DMA((n,))]

    def _copies(self, in_refs, out_refs, send_sems, recv_sems, local_sems):
        me, peers = _peers()
        ng = len(self.gathers)
        local, sends, recvs = [], [], []
        for a in range(self.n):
            x, o = in_refs[a], out_refs[a]
            mine = x if a < ng else x.at[me]
            local.append(pltpu.make_async_copy(mine, o.at[me], local_sems.at[a]))
            s_a, r_a = {}, {}
            for k in range(1, N_DEV):
                peer, slot = peers[k - 1]
                sems = dict(send_sem=send_sems.at[7 * a + k - 1], recv_sem=recv_sems.at[7 * a + k - 1],
                            device_id_type=pl.DeviceIdType.MESH)
                if a >= ng:
                    s_a[k] = pltpu.make_async_remote_copy(src_ref=x.at[slot], dst_ref=o.at[me], device_id=peer, **sems)
                elif k in FORWARDED:
                    came = o.at[peers[k - 2][1]]
                    s_a[k] = pltpu.make_async_remote_copy(src_ref=came, dst_ref=came, device_id=peers[0][0], **sems)
                else:
                    s_a[k] = pltpu.make_async_remote_copy(src_ref=x, dst_ref=o.at[me], device_id=peer, **sems)
                r_a[k] = pltpu.make_async_remote_copy(src_ref=mine, dst_ref=o.at[slot], device_id=peer, **sems)
            sends.append(s_a)
            recvs.append(r_a)
        return local, sends, recvs

    def start(self, *refs):
        if self.n == 0:
            return
        local, sends, _ = self._copies(*refs)
        for a in range(self.n):
            local[a].start()
            for k in range(1, N_DEV):
                if a >= len(self.gathers) or k not in FORWARDED:
                    sends[a][k].start()

    def forward(self, *refs):
        if not self.gathers:
            return
        _, sends, recvs = self._copies(*refs)
        for a in range(len(self.gathers)):
            for k in FORWARDED:
                recvs[a][k - 1].wait_recv()
                sends[a][k].start()

    def wait(self, *refs):
        if self.n == 0:
            return
        local, sends, recvs = self._copies(*refs)
        for a in range(self.n):
            waited_early = [f - 1 for f in FORWARDED] if a < len(self.gathers) else []
            for k in range(1, N_DEV):
                if k not in waited_early:
                    recvs[a][k].wait_recv()
            for k in range(1, N_DEV):
                sends[a][k].wait_send()
            local[a].wait()


FORWARDED = (3, 5, 7)


def exchange(ex, name):
    n = ex.n

    def body(*refs):
        args = (refs[:n], refs[n:2 * n]) + tuple(refs[2 * n:])
        ex.start(*args)
        ex.forward(*args)
        ex.wait(*args)

    return pl.pallas_call(body, in_specs=[_ANY] * n, out_specs=[_ANY] * n, out_shape=ex.out_shape(),
                          scratch_shapes=ex.scratch(), name=name)(*ex.operands())


def _heads(z, h):
    return z.reshape(z.shape[0], h, HEAD_DIM).transpose(1, 0, 2)


def _unheads(z):
    return z.transpose(1, 0, 2).reshape(z.shape[1], z.shape[0] * HEAD_DIM)


def _shift_up(z):
    return jnp.concatenate([z[1:], jnp.zeros_like(z[:1])], axis=0)


def _segments(width):
    seg = np.zeros((width, 128), np.float32)
    seg[np.arange(width), np.arange(width) // HEAD_DIM] = 1.0
    return jnp.asarray(seg), jnp.asarray(seg.T)


def _rope_consts(T, heads):
    inv = ROPE_THETA ** (-jnp.arange(0, HEAD_DIM, 2, dtype=F32) / HEAD_DIM)
    ang = jnp.arange(T, dtype=F32)[:, None] * inv[None, :]
    return jnp.tile(jnp.cos(ang), (1, 2 * heads)), jnp.tile(jnp.sin(ang), (1, 2 * heads))


def _per_head(g, heads):
    return jnp.tile(g.reshape(1, HEAD_DIM), (1, heads))


def _sum_heads(g):
    return g.reshape(-1, HEAD_DIM).sum(axis=0, keepdims=True)


LORA_COLS = 256
RW_TILE = 256
ROW_TILE = 512


def _local_step(x0, memx, tgt, P, ex_weights=None, weights_done=None, ex_grads=None):
    T = x0.shape[0]
    P = dict(P)
    G = {}
    seg, seg_t = _segments(RWKV_WIDTH)
    mseg = (seg[:MEM_WIDTH], seg_t[:, :MEM_WIDTH])
    cos, sin = _rope_consts(T, DIL_WIDTH // HEAD_DIM)
    row = lambda v: v.reshape(1, -1)

    def mem_fwd(i, q):
        memn = stage_fwd(f_rmsnorm, [memx], [P["mem_norm"][i:i + 1]], [], [], N_MEM, f"mem{i}_norm", [BF16])[0]
        kvm = matmul(memn, P["mem_w_kv"][i], "nn", f"mem{i}_kv")
        kn, qn = _per_head(P["mem_k_norm"][i], MEM_HEADS), _per_head(P["mem_q_norm"][i], MEM_HEADS)
        km = stage_fwd(f_headnorm, [Cols(kvm, MEM_WIDTH, 0)], [kn], [], mseg, N_MEM, f"mem{i}_knorm")[0]
        om = stage_fwd(f_memattn, [q], [km, Cols(kvm, MEM_WIDTH, 1), qn], [], mseg, ROW_TILE, f"mem{i}_attn")[0]
        return om, (memn, kvm, km, kn, qn, q)

    def mem_bwd(i, saved, dymem):
        memn, kvm, km, kn, qn, q = saved
        (dq,), (dkm, dvm, g_qn) = stage_bwd(f_memattn, [q], [km, Cols(kvm, MEM_WIDTH, 1), qn], [], mseg, [dymem], ROW_TILE,
                                            f"mem{i}_attn_bwd")
        (dkraw,), (g_kn,) = stage_bwd(f_headnorm, [Cols(kvm, MEM_WIDTH, 0)], [kn], [], mseg, [dkm], N_MEM, f"mem{i}_knorm_bwd")
        dkvm = jnp.concatenate([dkraw, dvm], axis=1).astype(BF16)
        g_w = matmul(memn, dkvm, "tn", f"mem{i}_kv_dw")
        dmemn = matmul(dkvm, P["mem_w_kv"][i], "nt", f"mem{i}_kv_dx")
        _, (g_mn,) = stage_bwd(f_rmsnorm, [memx], [P["mem_norm"][i:i + 1]], [], [], [dmemn], N_MEM, f"mem{i}_norm_bwd")
        return dq, g_mn, g_w, _sum_heads(g_qn), _sum_heads(g_kn)

    def ffn_fwd(i, xin):
        hn = stage_fwd(f_rmsnorm, [xin], [P["ffn_norm"][i:i + 1]], [], [], ROW_TILE, f"ffn{i}_norm", [BF16])[0]
        u = matmul(hn, P["ffn_w_up"][i], "nn", f"ffn{i}_up")
        z = convgate_fwd(u, P["ffn_conv_w"][i], P["ffn_conv_b"][i:i + 1], f"ffn{i}_conv")
        return matmul(z, P["ffn_w_down"][i], "nn", f"ffn{i}_down", residual=xin), (hn, u, z)

    def ffn_bwd(i, xin, saved, dxo, dxo_b):
        hn, u, z = saved
        dz = matmul(dxo_b, P["ffn_w_down"][i], "nt", f"ffn{i}_down_dx")
        g_down = matmul(z, dxo_b, "tn", f"ffn{i}_down_dw")
        du, g_cw, g_cb = convgate_bwd(u, P["ffn_conv_w"][i], P["ffn_conv_b"][i:i + 1], dz, f"ffn{i}_conv_bwd")
        dhn = matmul(du, P["ffn_w_up"][i], "nt", f"ffn{i}_up_dx")
        g_up = matmul(hn, du, "tn", f"ffn{i}_up_dw")
        (dxin,), (g_n,), (dxin_b,) = stage_bwd(f_rmsnorm_res, [xin], [P["ffn_norm"][i:i + 1]], [], [], [dhn, dxo], ROW_TILE,
                                               f"ffn{i}_norm_bwd", bf16_copies=(0,))
        return dxin, dxin_b, g_n, g_up, g_cw, g_cb, g_down

    h0 = stage_fwd(f_rmsnorm, [x0], [P["attn_norm"][0:1]], [], [], ROW_TILE, "l0_norm", [BF16])[0]
    p0 = matmul(h0, P["a_w_in"][0], "nn", "l0_in")
    lora0 = 3 * RWKV_WIDTH // LORA_COLS
    pre_xs = [Cols(p0, RWKV_WIDTH, 0), Cols(p0, RWKV_WIDTH, 1), Cols(p0, RWKV_WIDTH, 2), Cols(p0, LORA_COLS, lora0)]
    mu = [Cols(P["a_mu"], RWKV_WIDTH, 0), Cols(P["a_mu"], RWKV_WIDTH, 1), Cols(P["a_mu"], RWKV_WIDTH, 2),
          Cols(P["a_mu"], LORA_COLS, lora0)]
    lora_rows = lambda w, lo: jnp.pad(w, ((lo, LORA_COLS - lo - w.shape[0]), (0, 0)))
    pre_ps = mu + [P["a_w0"], lora_rows(P["a_w2"][0], 0), P["a_a0"], lora_rows(P["a_a2"][0], 64), lora_rows(P["a_g2"][0], 128),
                   P["a_k_k"], P["a_k_a"]]
    r, lw, k2, v, kk, b, g = stage_fwd(f_rwkv_pre, pre_xs, pre_ps, [], [seg, seg_t], RW_TILE, "l0_rwkv_pre", with_prev=True)
    scan_in = [r, lw, k2, v, kk, b]
    y_h, h_states, got = rwkv_scan_fwd(*scan_in, ex_weights or Exchange())
    if weights_done is not None:
        P.update(weights_done(got))
    y_s = y_h
    post_ps = [P["a_lnx_w"], P["a_lnx_b"], P["a_r_k"].reshape(1, RWKV_WIDTH)]
    ymix0 = stage_fwd(f_rwkv_post, [y_s, r, k2, v, g], post_ps, [], [seg, seg_t], RW_TILE, "l0_rwkv_post")[0]
    ymem0, mem0_saved = mem_fwd(0, Cols(p0, MEM_WIDTH, SHIFT_WIDTH // MEM_WIDTH))
    ycat0 = jnp.concatenate([ymix0, ymem0], axis=1).astype(BF16)
    x1 = matmul(ycat0, P["a_w_out"][0], "nn", "l0_out", residual=x0)
    x2, ffn0_saved = ffn_fwd(0, x1)

    hk, h1 = stage_fwd(f_rmsnorm2, [x2], [row(P["kv_norm"]), P["attn_norm"][1:2]], [], [], ROW_TILE, "l1_norm", [BF16, BF16])
    kvp = matmul(hk, P["kv_w"][0], "nn", "l1_kv")
    p1 = matmul(h1, P["b_w_in"][0], "nn", "l1_in")
    kraw, qraw = Cols(kvp, DIL_WIDTH, 0), Cols(p1, DIL_WIDTH, 0)
    kgain, qgain = _per_head(P["kv_k_norm"], DIL_WIDTH // HEAD_DIM), _per_head(P["b_q_norm"], DIL_WIDTH // HEAD_DIM)
    ksh = stage_fwd(f_qkprep, [kraw], [kgain], [cos, sin], [seg, seg_t], ROW_TILE, "l1_kprep")[0]
    q = stage_fwd(f_qkprep, [qraw], [qgain], [cos, sin], [seg, seg_t], ROW_TILE, "l1_qprep")[0]
    outs, lses = [], []
    for gi, (_, d) in enumerate(DIL_GROUPS):
        og, lg = dil_fwd(q, ksh, kvp, gi, d, f"l1_dil{gi}")
        outs.append(og)
        lses.append(lg)
    omix = stage_fwd(f_mix, outs + lses, [], [], [], ROW_TILE, "l1_mix")[0]
    ymem1, mem1_saved = mem_fwd(1, Cols(p1, MEM_WIDTH, DIL_WIDTH // MEM_WIDTH))
    ycat1 = jnp.concatenate([omix, ymem1], axis=1).astype(BF16)
    x3 = matmul(ycat1, P["b_w_out"][0], "nn", "l1_out", residual=x2)
    x4, ffn1_saved = ffn_fwd(1, x3)
    loss_part, dx4, dx4_b = loss_head(x4, tgt)

    dx3, dx3_b, gn1, gup1, gcw1, gcb1, gdown1 = ffn_bwd(1, x3, ffn1_saved, dx4, dx4_b)
    dycat1 = matmul(dx3_b, P["b_w_out"][0], "nt", "l1_out_dx")
    G["b_w_out"] = [matmul(ycat1, dx3_b, "tn", "l1_out_dw")]
    dqmem1, gmn1, gmw1, gmq1, gmk1 = mem_bwd(1, mem1_saved, Cols(dycat1, MEM_WIDTH, 1))
    dmix, _ = stage_bwd(f_mix, outs + lses, [], [], [], [Cols(dycat1, MEM_WIDTH, 0)], ROW_TILE, "l1_mix_bwd")
    dq, dk, dv = zip(*[dil_bwd(q, ksh, kvp, dmix[gi], dmix[3 + gi], gi, d, f"l1_dil{gi}_bwd")
                       for gi, (_, d) in enumerate(DIL_GROUPS)])
    dq, dk, dv = jnp.concatenate(dq, axis=1), jnp.concatenate(dk, axis=1), jnp.concatenate(dv, axis=1)
    (dqraw,), (g_bq,) = stage_bwd(f_qkprep, [qraw], [qgain], [cos, sin], [seg, seg_t], [dq], ROW_TILE, "l1_qprep_bwd")
    (dkraw,), (g_kk,) = stage_bwd(f_qkprep, [kraw], [kgain], [cos, sin], [seg, seg_t], [dk], ROW_TILE, "l1_kprep_bwd")
    g_bq, g_kk = _sum_heads(g_bq), _sum_heads(g_kk)
    dp1 = jnp.concatenate([dqraw, dqmem1], axis=1).astype(BF16)
    dkvp = jnp.concatenate([dkraw, dv], axis=1).astype(BF16)
    dh1 = matmul(dp1, P["b_w_in"][0], "nt", "l1_in_dx")
    G["b_w_in"] = [matmul(h1, dp1, "tn", "l1_in_dw")]
    dhk = matmul(dkvp, P["kv_w"][0], "nt", "l1_kv_dx")
    G["kv_w"] = [matmul(hk, dkvp, "tn", "l1_kv_dw")]
    (dx2,), (g_kvn, g_an1), (dx2_b,) = stage_bwd(f_rmsnorm2_res, [x2], [row(P["kv_norm"]), P["attn_norm"][1:2]], [], [],
                                                 [dhk, dh1, dx3], ROW_TILE, "l1_norm_bwd", bf16_copies=(0,))

    dx1, dx1_b, gn0, gup0, gcw0, gcb0, gdown0 = ffn_bwd(0, x1, ffn0_saved, dx2, dx2_b)
    dycat0 = matmul(dx1_b, P["a_w_out"][0], "nt", "l0_out_dx")
    G["a_w_out"] = [matmul(ycat0, dx1_b, "tn", "l0_out_dw")]
    dqmem0, gmn0, gmw0, gmq0, gmk0 = mem_bwd(0, mem0_saved, Cols(dycat0, MEM_WIDTH, RWKV_WIDTH // MEM_WIDTH))
    (dy_s, dr_a, dk_a, dv_a, dg), (g_lw, g_lb, g_rk) = stage_bwd(
        f_rwkv_post, [y_s, r, k2, v, g], post_ps, [], [seg, seg_t], [Cols(dycat0, RWKV_WIDTH, 0)], RW_TILE, "l0_rwkv_post_bwd")
    G["mem_w_kv"], G["ffn_w_up"], G["ffn_w_down"] = [gmw0, gmw1], [gup0, gup1], [gdown0, gdown1]
    (dr_b, dlw, dk_b, dv_b, dkk, db), G["_exchanged"] = rwkv_scan_bwd(*scan_in, h_states, dy_s,
                                                                      ex_grads(G) if ex_grads else Exchange())
    dpre, gpre = stage_bwd(f_rwkv_pre, pre_xs, pre_ps, [], [seg, seg_t],
                           [[dr_a, dr_b], dlw, [dk_a, dk_b], [dv_a, dv_b], dkk, db, dg], RW_TILE, "l0_rwkv_pre_bwd", with_prev=True)
    dp_rw = jnp.concatenate(dpre[:4], axis=1) + _shift_up(jnp.concatenate(dpre[4:], axis=1))
    dp0 = jnp.concatenate([dp_rw, dqmem0], axis=1).astype(BF16)
    dh0 = matmul(dp0, P["a_w_in"][0], "nt", "l0_in_dx")
    G["a_w_in"] = [matmul(h0, dp0, "tn", "l0_in_dw")]
    (dx0,), (g_an0,) = stage_bwd(f_rmsnorm_res, [x0], [P["attn_norm"][0:1]], [], [], [dh0, dx1], ROW_TILE, "l0_norm_bwd")

    G["attn_norm"] = jnp.concatenate([g_an0, g_an1], axis=0)
    G["a_mu"] = jnp.concatenate(gpre[:4], axis=1)
    G["a_w0"], G["a_w2"], G["a_a0"], G["a_a2"], G["a_g2"] = gpre[4], gpre[5][None, :64], gpre[6], gpre[7][None, 64:128], gpre[8][None, 128:]
    G["a_k_k"], G["a_k_a"] = gpre[9], gpre[10]
    G["a_r_k"] = g_rk.reshape(1, RWKV_HEADS, HEAD_DIM)
    G["a_lnx_w"], G["a_lnx_b"] = g_lw, g_lb
    G["kv_norm"], G["kv_k_norm"], G["b_q_norm"] = g_kvn.reshape(-1), g_kk.reshape(-1), g_bq
    G["mem_norm"] = jnp.concatenate([gmn0, gmn1], axis=0)
    G["mem_w_kv"] = [gmw0, gmw1]
    G["mem_q_norm"] = jnp.concatenate([gmq0, gmq1], axis=0)
    G["mem_k_norm"] = jnp.concatenate([gmk0, gmk1], axis=0)
    G["ffn_norm"] = jnp.concatenate([gn0, gn1], axis=0)
    G["ffn_w_up"] = [gup0, gup1]
    G["ffn_conv_w"] = jnp.stack([gcw0, gcw1])
    G["ffn_conv_b"] = jnp.concatenate([gcb0, gcb1], axis=0)
    G["ffn_w_down"] = [gdown0, gdown1]
    return loss_part, dx0, G


PARAMS = (("attn_norm", None), ("a_w_in", 2), ("a_mu", 1), ("a_w0", 1), ("a_w2", 2), ("a_a0", 1), ("a_a2", 2), ("a_g2", 2),
          ("a_k_k", 1), ("a_k_a", 1), ("a_r_k", None), ("a_lnx_w", 1), ("a_lnx_b", 1), ("a_w_out", 1), ("kv_norm", None),
          ("kv_w", 1), ("kv_k_norm", None), ("b_w_in", 1), ("b_q_norm", None), ("b_w_out", 2), ("mem_norm", None),
          ("mem_w_kv", 1), ("mem_q_norm", None), ("mem_k_norm", None), ("ffn_norm", None), ("ffn_w_up", 2),
          ("ffn_conv_w", 2), ("ffn_conv_b", None), ("ffn_w_down", 1))
BIG = ("a_w_in", "a_w_out", "kv_w", "b_w_in", "b_w_out", "mem_w_kv", "ffn_w_up", "ffn_w_down")
AXIS = dict(PARAMS)
SMALL = tuple(n for n, _ in PARAMS if n not in BIG)
SMALL_SHARDED = tuple(n for n in SMALL if AXIS[n] is not None)
PACK_QUANTUM = 256 * 128


def _from_shards(xs, axis):
    full = jnp.moveaxis(xs, 0, axis)
    sh = full.shape
    return full.reshape(sh[:axis] + (sh[axis] * sh[axis + 1],) + sh[axis + 2:])


def _to_shards(g, axis):
    sh = g.shape
    return jnp.moveaxis(g.reshape(sh[:axis] + (N_DEV, sh[axis] // N_DEV) + sh[axis + 1:]), axis, 0)


def _pack(parts, lead=0):
    ld = parts[0].shape[:lead]
    flat = jnp.concatenate([p.reshape(ld + (-1,)) for p in parts], axis=-1)
    pad = (-flat.shape[-1]) % PACK_QUANTUM
    flat = jnp.pad(flat, [(0, 0)] * lead + [(0, pad)])
    return flat.reshape(ld + (-1, 128))


def _unpack(packed, shapes, lead=0):
    ld = packed.shape[:lead]
    flat = packed.reshape(ld + (-1,))
    out, off = [], 0
    for s in shapes:
        n = math.prod(s)
        out.append(flat[..., off:off + n].reshape(ld + tuple(s)))
        off += n
    return out


def kernel(x, mem, attn_norm, a_w_in, a_mu, a_w0, a_w2, a_a0, a_a2, a_g2, a_k_k, a_k_a, a_r_k, a_lnx_w, a_lnx_b, a_w_out, kv_norm, kv_w, kv_k_norm, b_w_in, b_q_norm, b_w_out, mem_norm, mem_w_kv, mem_q_norm, mem_k_norm, ffn_norm, ffn_w_up, ffn_conv_w, ffn_conv_b, ffn_w_down, loss_target, m_attn_norm, m_a_w_in, m_a_mu, m_a_w0, m_a_w2, m_a_a0, m_a_a2, m_a_g2, m_a_k_k, m_a_k_a, m_a_r_k, m_a_lnx_w, m_a_lnx_b, m_a_w_out, m_kv_norm, m_kv_w, m_kv_k_norm, m_b_w_in, m_b_q_norm, m_b_w_out, m_mem_norm, m_mem_w_kv, m_mem_q_norm, m_mem_k_norm, m_ffn_norm, m_ffn_w_up, m_ffn_conv_w, m_ffn_conv_b, m_ffn_w_down, v_attn_norm, v_a_w_in, v_a_mu, v_a_w0, v_a_w2, v_a_a0, v_a_a2, v_a_g2, v_a_k_k, v_a_k_a, v_a_r_k, v_a_lnx_w, v_a_lnx_b, v_a_w_out, v_kv_norm, v_kv_w, v_kv_k_norm, v_b_w_in, v_b_q_norm, v_b_w_out, v_mem_norm, v_mem_w_kv, v_mem_q_norm, v_mem_k_norm, v_ffn_norm, v_ffn_w_up, v_ffn_conv_w, v_ffn_conv_b, v_ffn_w_down):
    names = [n for n, _ in PARAMS]
    vals = (attn_norm, a_w_in, a_mu, a_w0, a_w2, a_a0, a_a2, a_g2, a_k_k, a_k_a, a_r_k, a_lnx_w, a_lnx_b, a_w_out, kv_norm, kv_w, kv_k_norm, b_w_in, b_q_norm, b_w_out, mem_norm, mem_w_kv, mem_q_norm, mem_k_norm, ffn_norm, ffn_w_up, ffn_conv_w, ffn_conv_b, ffn_w_down)
    m_vals = (m_attn_norm, m_a_w_in, m_a_mu, m_a_w0, m_a_w2, m_a_a0, m_a_a2, m_a_g2, m_a_k_k, m_a_k_a, m_a_r_k, m_a_lnx_w, m_a_lnx_b, m_a_w_out, m_kv_norm, m_kv_w, m_kv_k_norm, m_b_w_in, m_b_q_norm, m_b_w_out, m_mem_norm, m_mem_w_kv, m_mem_q_norm, m_mem_k_norm, m_ffn_norm, m_ffn_w_up, m_ffn_conv_w, m_ffn_conv_b, m_ffn_w_down)
    v_vals = (v_attn_norm, v_a_w_in, v_a_mu, v_a_w0, v_a_w2, v_a_a0, v_a_a2, v_a_g2, v_a_k_k, v_a_k_a, v_a_r_k, v_a_lnx_w, v_a_lnx_b, v_a_w_out, v_kv_norm, v_kv_w, v_kv_k_norm, v_b_w_in, v_b_q_norm, v_b_w_out, v_mem_norm, v_mem_w_kv, v_mem_q_norm, v_mem_k_norm, v_ffn_norm, v_ffn_w_up, v_ffn_conv_w, v_ffn_conv_b, v_ffn_w_down)
    W, M, V = dict(zip(names, vals)), dict(zip(names, m_vals)), dict(zip(names, v_vals))
    me = 4 * lax.axis_index("x") + 2 * lax.axis_index("y") + lax.axis_index("c")
    layers = lambda D, n: [D[n]] if D[n].ndim == 2 else [D[n][i] for i in range(D[n].shape[0])]
    ax2 = lambda n: AXIS[n] - (W[n].ndim - 2)
    first = [("a_w_in", 0)]
    later = [(n, i) for n in BIG if n != "a_w_in" for i in range(len(layers(W, n)))]

    small_shapes = [W[n].shape for n in SMALL_SHARDED]
    got_w, got_small = exchange(Exchange(gathers=[W["a_w_in"][0].astype(BF16), _pack([W[n] for n in SMALL_SHARDED])]),
                                "gather_first")
    P = {n: W[n] for n in SMALL}
    P["a_w_in"] = [_from_shards(got_w, ax2("a_w_in"))]
    for n, s in zip(SMALL_SHARDED, _unpack(got_small, small_shapes, lead=1)):
        P[n] = _from_shards(s, AXIS[n])
    ex_weights = Exchange(gathers=[layers(W, n)[i].astype(BF16) for n, i in later])

    def weights_done(got):
        out = {}
        for (n, _), g in zip(later, got):
            out.setdefault(n, []).append(_from_shards(g, ax2(n)))
        return out

    slots = lambda G, n: jnp.stack([_to_shards(g, ax2(n)) for g in G[n]], axis=1)
    later_names = [n for n in BIG if n != "a_w_in"]
    ex_grads = lambda G: Exchange(scatters=[slots(G, n) for n in later_names])
    loss_part, dx0, G = _local_step(x[0], mem[0], loss_target[0], P, ex_weights, weights_done, ex_grads)
    loss = lax.psum(loss_part[0, 0], ("x", "y", "c"))
    gparts = dict(zip(later_names, G.pop("_exchanged")))
    got_gsmall, gparts["a_w_in"] = exchange(
        Exchange(gathers=[_pack([G[n] for n in SMALL])], scatters=[slots(G, "a_w_in")]), "exchange_last")

    results = {}
    for n in BIG:
        rows = lambda z: z.reshape((-1,) + z.shape[-1:])
        res = adamw(gparts[n].reshape((N_DEV, -1) + gparts[n].shape[-1:]), rows(W[n]), rows(M[n]), rows(V[n]), f"adamw_{n}")
        results[n] = [r.reshape(W[n].shape) for r in res]
    g_small = sum_parts(got_gsmall, "sum_small_grads")
    mine = []
    for n, g in zip(SMALL, _unpack(g_small, [G[n].shape for n in SMALL])):
        if AXIS[n] is not None:
            s = W[n].shape[AXIS[n]]
            g = lax.dynamic_slice_in_dim(g, me * s, s, axis=AXIS[n])
        mine.append(g)
    res = adamw(_pack(mine)[None], _pack([W[n] for n in SMALL]), _pack([M[n] for n in SMALL]), _pack([V[n] for n in SMALL]),
                "adamw_small")
    for n, parts in zip(SMALL, zip(*[_unpack(r, [W[n].shape for n in SMALL]) for r in res])):
        results[n] = list(parts)
    outs = [[results[n][j] for n in names] for j in range(4)]
    return (loss, dx0[None], *outs[0], *outs[1], *outs[2], *outs[3])
```

```python
import functools
import math

import jax
import jax.numpy as jnp
import numpy as np
from jax import lax
from jax.experimental import pallas as pl
from jax.experimental.pallas import tpu as pltpu

F32 = jnp.float32
BF16 = jnp.bfloat16
HI = lax.Precision.HIGHEST
H3 = lax.Precision.HIGH

N_DEV = 8
D_MODEL = 1024
HEAD_DIM = 64
N_MEM = 256
MEM_HEADS = 4
MEM_WIDTH = 256
RWKV_HEADS = 12
RWKV_WIDTH = 768
SHIFT_WIDTH = 2560
DIL_GROUPS = ((128, 1), (512, 4), (2048, 16))
DIL_BLOCK = 128
DIL_WIDTH = 768
D_FF = 2816
RMS_EPS = 1e-6
LNX_EPS = 64e-5
NEG_INF = -1e30
ROPE_THETA = 10000.0
ADAM_LR, ADAM_B1, ADAM_B2, ADAM_EPS, ADAM_WD, ADAM_STEP = 0.001, 0.9, 0.999, 1e-08, 0.01, 10

CHUNK = 64
SCAN_GROUPS_FWD, SCAN_GROUPS_BWD = 1, 1
MM_TILE_CAP = 1408
VMEM_LIMIT_V7X = 48 * 1024 * 1024


def _cparams(sem):
    return pltpu.CompilerParams(dimension_semantics=sem, vmem_limit_bytes=VMEM_LIMIT_V7X)


def _pick(n, cands):
    for c in cands:
        if n % c == 0:
            return c
    return n


def _tile(n, cap):
    if n <= cap:
        return n
    for d in range(cap - cap % 128, 0, -128):
        if n % d == 0:
            return d
    return n


def _dg(a, b, ca, cb, batch):
    dims = (((ca,), (cb,)), ((0,), (0,))) if batch else (((ca,), (cb,)), ((), ()))
    return lax.dot_general(a.astype(BF16), b.astype(BF16), dims, preferred_element_type=F32)


@jax.custom_vjp
def mm_nn(a, b):
    n = a.ndim
    return _dg(a, b, n - 1, n - 2, n == 3)


def _mm_nn_fwd(a, b):
    return mm_nn(a, b), (a, b)


def _mm_nn_bwd(res, g):
    a, b = res
    n = a.ndim
    return _dg(g, b, n - 1, n - 1, n == 3), _dg(a, g, n - 2, n - 2, n == 3)


mm_nn.defvjp(_mm_nn_fwd, _mm_nn_bwd)


@jax.custom_vjp
def mm_nt(a, b):
    n = a.ndim
    return _dg(a, b, n - 1, n - 1, n == 3)


def _mm_nt_fwd(a, b):
    return mm_nt(a, b), (a, b)


def _mm_nt_bwd(res, g):
    a, b = res
    n = a.ndim
    return _dg(g, b, n - 1, n - 2, n == 3), _dg(g, a, n - 2, n - 2, n == 3)


mm_nt.defvjp(_mm_nt_fwd, _mm_nt_bwd)


def mmh(a, b, precision=H3):
    n = a.ndim
    dims = (((n - 1,), (n - 2,)), ((0,), (0,))) if n == 3 else (((1,), (0,)), ((), ()))
    return lax.dot_general(a, b, dims, precision=precision, preferred_element_type=F32)


def mmh_nt(a, b):
    n = a.ndim
    dims = (((n - 1,), (n - 1,)), ((0,), (0,))) if n == 3 else (((1,), (1,)), ((), ()))
    return lax.dot_general(a, b, dims, precision=H3, preferred_element_type=F32)


def mmh_tn(a, b):
    n = a.ndim
    dims = (((n - 2,), (n - 2,)), ((0,), (0,))) if n == 3 else (((0,), (0,)), ((), ()))
    return lax.dot_general(a, b, dims, precision=H3, preferred_element_type=F32)


def matmul(a, b, mode, name, residual=None):
    out_dtype = BF16 if mode == "tn" else F32
    if mode == "nn":
        (M, K), (_, N) = a.shape, b.shape
    elif mode == "nt":
        (M, K), (N, _) = a.shape, b.shape
    else:
        (K, M), (_, N) = a.shape, b.shape
    tm = _tile(M, 2048 if mode == "nn" else MM_TILE_CAP)
    tn = _tile(N, 512 if mode == "nn" else MM_TILE_CAP)
    tk = _tile(K, MM_TILE_CAP)
    nk = K // tk
    if mode == "nn":
        a_spec = pl.BlockSpec((tm, tk), lambda i, j, k: (i, k))
        b_spec = pl.BlockSpec((tk, tn), lambda i, j, k: (k, j))
        dims = (((1,), (0,)), ((), ()))
    elif mode == "nt":
        a_spec = pl.BlockSpec((tm, tk), lambda i, j, k: (i, k))
        b_spec = pl.BlockSpec((tn, tk), lambda i, j, k: (j, k))
        dims = (((1,), (1,)), ((), ()))
    else:
        a_spec = pl.BlockSpec((tk, tm), lambda i, j, k: (k, i))
        b_spec = pl.BlockSpec((tk, tn), lambda i, j, k: (k, j))
        dims = (((0,), (0,)), ((), ()))
    o_spec = pl.BlockSpec((tm, tn), lambda i, j, k: (i, j))
    has_res = residual is not None

    def body(*refs):
        if has_res:
            a_ref, b_ref, r_ref, o_ref, acc_ref = refs
        else:
            a_ref, b_ref, o_ref, acc_ref = refs
        k = pl.program_id(2)

        @pl.when(k == 0)
        def _():
            acc_ref[...] = jnp.zeros_like(acc_ref)

        acc_ref[...] += lax.dot_general(a_ref[...].astype(BF16), b_ref[...].astype(BF16), dims,
                                        preferred_element_type=F32)

        @pl.when(k == nk - 1)
        def _():
            if has_res:
                o_ref[...] = (acc_ref[...] + r_ref[...]).astype(out_dtype)
            else:
                o_ref[...] = acc_ref[...].astype(out_dtype)

    ins = [a, b] + ([residual] if has_res else [])
    in_specs = [a_spec, b_spec] + ([o_spec] if has_res else [])
    return pl.pallas_call(
        body, grid=(M // tm, N // tn, nk), in_specs=in_specs, out_specs=o_spec,
        out_shape=jax.ShapeDtypeStruct((M, N), out_dtype), scratch_shapes=[pltpu.VMEM((tm, tn), F32)],
        compiler_params=_cparams(("parallel", "parallel", "arbitrary")), name=name)(*ins)


class Cols:
    def __init__(self, arr, width, idx):
        self.arr, self.width, self.idx = arr, width, idx


def _arr(x):
    return x.arr if isinstance(x, Cols) else x


def _shape(x):
    return x.arr.shape[:-1] + (x.width,) if isinstance(x, Cols) else x.shape


def _col(x):
    return x.idx if isinstance(x, Cols) else 0


def _tok_spec(x, tile):
    shape, col = _shape(x), _col(x)
    return pl.BlockSpec(shape[:-2] + (tile, shape[-1]), lambda i: (0,) * (len(shape) - 2) + (i, col))


def _full_spec(x):
    shape, col = _shape(x), _col(x)
    return pl.BlockSpec(shape, lambda i: (0,) * (len(shape) - 1) + (col,))


def _halo_spec(x, tile):
    shape, col = _shape(x), _col(x)
    return pl.BlockSpec((8, shape[-1]), lambda i: (jnp.maximum(i * (tile // 8) - 1, 0), col))


def _blk(x, tile):
    shape = _shape(x)
    return jax.ShapeDtypeStruct(shape[:-2] + (tile, shape[-1]), _arr(x).dtype)


def _prev_rows(x, halo):
    rows = lax.broadcasted_iota(jnp.int32, (x.shape[0], 1), 0)
    before = jnp.where(pl.program_id(0) > 0, halo[7:8], 0.0)
    return jnp.where(rows == 0, before, pltpu.roll(x, 1, 0))


def stage_fwd(f, xs, ps, cts, cfs, tile, name, out_dtypes=None, with_prev=False):
    xs, ps, cts, cfs = list(xs), list(ps), list(cts), list(cfs)
    halos = xs if with_prev else []
    nx, nh, nct, np_ = len(xs), len(halos), len(cts), len(ps)
    T = _shape(xs[0])[-2]
    blk = [_blk(x, tile) for x in xs]
    out_avals = jax.eval_shape(f, *blk, *(blk if with_prev else []), *[_blk(p, _shape(p)[-2]) for p in ps],
                               *[_blk(c, tile) for c in cts], *[_blk(c, _shape(c)[-2]) for c in cfs])
    if out_dtypes is None:
        out_dtypes = [o.dtype for o in out_avals]
    out_shape = [jax.ShapeDtypeStruct(o.shape[:-2] + (T, o.shape[-1]), dt) for o, dt in zip(out_avals, out_dtypes)]
    n_in = nx + nh + nct + np_ + len(cfs)

    def body(*refs):
        vals = [r[...] for r in refs[:n_in]]
        xv, hv, rest = vals[:nx], vals[nx:nx + nh], vals[nx + nh:]
        ctv, pv, cfv = rest[:nct], rest[nct:nct + np_], rest[nct + np_:]
        prev = [_prev_rows(x, h) for x, h in zip(xv, hv)]
        res = f(*xv, *prev, *pv, *ctv, *cfv)
        for o_ref, r in zip(refs[n_in:], res):
            o_ref[...] = r.astype(o_ref.dtype)

    return pl.pallas_call(
        body, grid=(T // tile,),
        in_specs=([_tok_spec(x, tile) for x in xs] + [_halo_spec(x, tile) for x in halos] + [_tok_spec(c, tile) for c in cts]
                  + [_full_spec(p) for p in ps + cfs]),
        out_specs=[_tok_spec(o, tile) for o in out_shape], out_shape=out_shape,
        compiler_params=_cparams(("parallel",)), name=name)(*[_arr(a) for a in xs + halos + cts + ps + cfs])


def stage_bwd(f, xs, ps, cts, cfs, gs, tile, name, bf16_copies=(), with_prev=False):
    xs, ps, cts, cfs = list(xs), list(ps), list(cts), list(cfs)
    gs = [list(g) if isinstance(g, (list, tuple)) else [g] for g in gs]
    g_flat = [a for g in gs for a in g]
    halos = xs if with_prev else []
    nx, nh, nct, ng, np_ = len(xs), len(halos), len(cts), len(g_flat), len(ps)
    T = _shape(xs[0])[-2]
    dx_like = xs + halos
    out_shape = ([jax.ShapeDtypeStruct(_shape(x), F32) for x in dx_like] + [jax.ShapeDtypeStruct(_shape(p), F32) for p in ps]
                 + [jax.ShapeDtypeStruct(_shape(xs[i]), BF16) for i in bf16_copies])
    n_in = nx + nh + nct + ng + np_ + len(cfs)
    ndx = nx + nh

    def body(*refs):
        vals = [r[...] for r in refs[:n_in]]
        outs = refs[n_in:]
        xv, hv, rest = vals[:nx], vals[nx:nx + nh], vals[nx + nh:]
        ctv, gparts, pv, cfv = rest[:nct], rest[nct:nct + ng], rest[nct + ng:nct + ng + np_], rest[nct + ng + np_:]
        gv = []
        for g in gs:
            gv.append(functools.reduce(lambda a, b: a + b, gparts[:len(g)]))
            gparts = gparts[len(g):]
        prev = [_prev_rows(x, h) for x, h in zip(xv, hv)]
        _, vjp = jax.vjp(lambda *xp: f(*xp, *ctv, *cfv), *xv, *prev, *pv)
        d = vjp(tuple(gv))
        for o_ref, r in zip(outs[:ndx], d[:ndx]):
            o_ref[...] = r
        for o_ref, i in zip(outs[ndx + np_:], bf16_copies):
            o_ref[...] = d[i].astype(BF16)

        @pl.when(pl.program_id(0) == 0)
        def _():
            for o_ref in outs[ndx:ndx + np_]:
                o_ref[...] = jnp.zeros_like(o_ref)

        for o_ref, r in zip(outs[ndx:ndx + np_], d[ndx:]):
            o_ref[...] += r

    plain = lambda x: jax.ShapeDtypeStruct(_shape(x), F32)
    res = pl.pallas_call(
        body, grid=(T // tile,),
        in_specs=([_tok_spec(x, tile) for x in xs] + [_halo_spec(x, tile) for x in halos]
                  + [_tok_spec(c, tile) for c in cts + g_flat] + [_full_spec(p) for p in ps + cfs]),
        out_specs=([_tok_spec(plain(x), tile) for x in dx_like] + [_full_spec(plain(p)) for p in ps]
                   + [_tok_spec(plain(xs[i]), tile) for i in bf16_copies]), out_shape=out_shape,
        compiler_params=_cparams(("arbitrary",)), name=name)(*[_arr(a) for a in xs + halos + cts + g_flat + ps + cfs])
    if bf16_copies:
        return list(res[:ndx]), list(res[ndx:ndx + np_]), list(res[ndx + np_:])
    return list(res[:ndx]), list(res[ndx:])


def _rms(x, g, eps=RMS_EPS):
    return x * lax.rsqrt(jnp.mean(x * x, axis=-1, keepdims=True) + eps) * g


def f_rmsnorm(x, g):
    return (_rms(x, g),)


def f_rmsnorm_res(x, g):
    return _rms(x, g), x


def f_rmsnorm2(x, g1, g2):
    n = x * lax.rsqrt(jnp.mean(x * x, axis=-1, keepdims=True) + RMS_EPS)
    return n * g1, n * g2


def f_rmsnorm2_res(x, g1, g2):
    return f_rmsnorm2(x, g1, g2) + (x,)


def _sigmoid(x):
    return 1.0 / (1.0 + jnp.exp(-x))


def _softplus(x):
    return jnp.maximum(x, 0.0) + jnp.log(1.0 + jnp.exp(-jnp.abs(x)))


def f_rwkv_pre(pr, pk, pv, pl_, qr, qk, qv, ql, mu_r, mu_k, mu_v, mu_l, w0, w2, a0, a2, g2, k_k, k_a, seg, seg_t):
    xr = pr + (qr - pr) * mu_r
    xk = pk + (qk - pk) * mu_k
    xv = pv + (qv - pv) * mu_v
    xl = pl_ + (ql - pl_) * mu_l
    w_log = -_softplus(-(w0 + mm_nn(jnp.tanh(xl), w2))) - 0.5
    lw = -jnp.exp(w_log)
    a = _sigmoid(a0 + mm_nn(xl, a2))
    g = mm_nn(_sigmoid(xl), g2)
    kkr = xk * k_k
    inv = lax.rsqrt(jnp.maximum(mmh(kkr * kkr, seg), 1e-24))
    kk = kkr * mmh(inv, seg_t)
    k2 = xk * (1.0 + (a - 1.0) * k_a)
    return xr, lw, k2, xv, kk, kk * a, g


def f_rwkv_post(y, r, k2, v, g, lnx_w, lnx_b, r_k, seg, seg_t):
    inv_n = 1.0 / HEAD_DIM
    m = mmh(mmh(y, seg) * inv_n, seg_t)
    yc = y - m
    rstd = lax.rsqrt(mmh(yc * yc, seg) * inv_n + LNX_EPS)
    yn = yc * mmh(rstd, seg_t) * lnx_w + lnx_b
    bonus = mmh(mmh(r * k2 * r_k, seg), seg_t) * v
    return ((yn + bonus) * g,)


def _headnorm(z, g, seg, seg_t):
    ms = mmh(z * z, seg) * (1.0 / HEAD_DIM)
    return z * mmh(lax.rsqrt(ms + RMS_EPS), seg_t) * g


def f_headnorm(z, g, seg, seg_t):
    return (_headnorm(z, g, seg, seg_t),)


def _rot_half(z):
    w = z.shape[1]
    half = HEAD_DIM // 2
    lane = lax.broadcasted_iota(jnp.int32, (1, w), 1)
    return jnp.where((lane & (HEAD_DIM - 1)) < half, -pltpu.roll(z, w - half, 1), pltpu.roll(z, half, 1))


@jax.custom_vjp
def _rotate_half(z):
    return _rot_half(z)


_rotate_half.defvjp(lambda z: (_rot_half(z), None), lambda _, g: (-_rot_half(g),))


def f_qkprep(z, g, cos, sin, seg, seg_t):
    zn = _headnorm(z, g, seg, seg_t)
    return (zn * cos + _rotate_half(zn) * sin,)


def _head_mask(width, h):
    lane = lax.broadcasted_iota(jnp.int32, (1, width), 1)
    return jnp.where((lane >> 6) == h, jnp.ones((), F32), 0.0)


def f_memattn(q, k, v, q_norm, seg, seg_t):
    qn = _headnorm(q, q_norm, seg, seg_t)
    out = jnp.zeros_like(q)
    for h in range(MEM_HEADS):
        m = _head_mask(MEM_WIDTH, h)
        s = mm_nt(qn * m, k) * (1.0 / math.sqrt(HEAD_DIM))
        s = s - jnp.max(s, axis=-1, keepdims=True)
        p = jnp.exp(s)
        p = p / jnp.sum(p, axis=-1, keepdims=True)
        out = out + mm_nn(p, v) * m
    return (out,)


def f_mix(o1, o2, o3, l1, l2, l3):
    mx = jnp.maximum(jnp.maximum(l1, l2), l3)
    e1, e2, e3 = jnp.exp(l1 - mx), jnp.exp(l2 - mx), jnp.exp(l3 - mx)
    return ((e1 * o1 + e2 * o2 + e3 * o3) / (e1 + e2 + e3),)


def _chunk_masks(L):
    t = lax.broadcasted_iota(jnp.int32, (L, L), 0)
    s = lax.broadcasted_iota(jnp.int32, (L, L), 1)
    return t, s


def _unit_lower_inverse(a):
    L = a.shape[-1]
    t, s = _chunk_masks(L)
    one = jnp.ones((), F32)
    blk = lambda sh: jnp.where((t >> sh) == (s >> sh), one, 0.0)
    n0 = a * blk(3)
    x = jnp.where(t == s, one, 0.0) - n0
    n2 = mmh(n0, n0)
    x = x + mmh(x, n2)
    x = x + mmh(x, mmh(n2, n2))
    for sh in (3, 4, 5):
        if (1 << sh) >= L:
            break
        off = a * (blk(sh + 1) - blk(sh))
        x = x - mmh(x, mmh(off, x))
    return x


@jax.custom_vjp
def _inverse_known(a, x):
    return x


def _inverse_known_fwd(a, x):
    return x, x


def _inverse_known_bwd(x, dx):
    return -mmh_nt(mmh_tn(x, dx), x), jnp.zeros_like(x)


_inverse_known.defvjp(_inverse_known_fwd, _inverse_known_bwd)


def _running_sum(x, reverse):
    L = x.shape[1]
    pos = lax.broadcasted_iota(jnp.int32, (1, L, 1), 1)
    step = 1
    while step < L:
        if reverse:
            x = x + jnp.where(pos < L - step, pltpu.roll(x, L - step, 1), 0.0)
        else:
            x = x + jnp.where(pos >= step, pltpu.roll(x, step, 1), 0.0)
        step *= 2
    return x


@jax.custom_vjp
def _cumsum_tokens(x):
    return _running_sum(x, False)


_cumsum_tokens.defvjp(lambda x: (_running_sum(x, False), None), lambda _, g: (_running_sum(g, True),))


def f_rwkv_chunk(s0, r, lw, k, v, kk, b, x_known=None):
    H, L, _ = r.shape
    t, s = _chunk_masks(L)
    one = jnp.ones((), F32)
    incl = jnp.where(t >= s, one, 0.0)
    strict = jnp.where(t > s, one, 0.0)
    cum = _cumsum_tokens(lw)
    w_in = jnp.exp(cum)
    w_ex = jnp.exp(cum - lw)
    w_inv = jnp.exp(-cum)
    rt, kkt, kt, bt = r * w_in, kk * w_ex, k * w_inv, b * w_inv
    a_b = mmh_nt(kkt, bt) * strict
    a_k = mmh_nt(kkt, kt) * strict
    m_k = mmh_nt(rt, kt) * incl
    m_b = mmh_nt(rt, bt) * incl
    x = _unit_lower_inverse(a_b) if x_known is None else _inverse_known(a_b, x_known)
    u = mmh(x, mmh_nt(kkt, s0) + mmh(a_k, v))
    y = mmh_nt(rt, s0) + mmh(m_k, v) - mmh(m_b, u)
    w_last = jnp.exp(jnp.sum(lw, axis=1, keepdims=True))
    s1 = (s0 + mmh_tn(v, kt) - mmh_tn(u, bt)) * w_last
    return y, s1, x


def _ex_split(ex, refs, n_in, n_out):
    n = ex.n
    ins, ex_in = refs[:n_in], refs[n_in:n_in + n]
    outs, ex_out = refs[n_in + n:n_in + n + n_out], refs[n_in + n + n_out:n_in + 2 * n + n_out]
    rest = refs[n_in + 2 * n + n_out:]
    return ins, outs, rest[:len(rest) - 3], (ex_in, ex_out) + tuple(rest[len(rest) - 3:])


def _split_heads(x):
    return jnp.stack([x[:, h * HEAD_DIM:(h + 1) * HEAD_DIM] for h in range(x.shape[1] // HEAD_DIM)], axis=0)


def _merge_heads(x):
    return jnp.concatenate([x[h] for h in range(x.shape[0])], axis=1)


def rwkv_scan_fwd(r, lw, k, v, kk, b, ex):
    T, N = r.shape[0], HEAD_DIM
    H = r.shape[1] // N
    groups = SCAN_GROUPS_FWD
    nc, hg = T // CHUNK, H // groups
    seq = pl.BlockSpec((CHUNK, hg * N), lambda g, c: (c, g))

    def body(*refs):
        (r_ref, lw_ref, k_ref, v_ref, kk_ref, b_ref), (y_ref, hs_ref, xs_ref), (h_scr,), ex_refs = _ex_split(ex, refs, 6, 3)
        g, c = pl.program_id(0), pl.program_id(1)

        @pl.when(jnp.logical_and(g == 0, c == 0))
        def _():
            ex.start(*ex_refs)

        @pl.when(c == 0)
        def _():
            h_scr[...] = jnp.zeros_like(h_scr)

        h0 = h_scr[...]
        hs_ref[0] = h0
        y, h1, x = f_rwkv_chunk(h0, *[_split_heads(z[...]) for z in (r_ref, lw_ref, k_ref, v_ref, kk_ref, b_ref)])
        y_ref[...] = _merge_heads(y)
        xs_ref[0] = x
        h_scr[...] = h1

        @pl.when(jnp.logical_and(g == groups - 1, c == (3 * nc) // 4))
        def _():
            ex.forward(*ex_refs)

        @pl.when(jnp.logical_and(g == groups - 1, c == nc - 1))
        def _():
            ex.wait(*ex_refs)

    res = pl.pallas_call(
        body, grid=(groups, nc), in_specs=[seq] * 6 + [_ANY] * ex.n,
        out_specs=[seq, pl.BlockSpec((1, hg, N, N), lambda g, c: (c, g, 0, 0)),
                   pl.BlockSpec((1, hg, CHUNK, CHUNK), lambda g, c: (c, g, 0, 0))] + [_ANY] * ex.n,
        out_shape=[jax.ShapeDtypeStruct((T, H * N), F32), jax.ShapeDtypeStruct((nc, H, N, N), F32),
                   jax.ShapeDtypeStruct((nc, H, CHUNK, CHUNK), F32)] + ex.out_shape(),
        scratch_shapes=[pltpu.VMEM((hg, N, N), F32)] + ex.scratch(),
        compiler_params=_cparams(("arbitrary", "arbitrary")), name="rwkv_scan_fwd")(r, lw, k, v, kk, b, *ex.operands())
    return res[0], (res[1], res[2]), list(res[3:])


def rwkv_scan_bwd(r, lw, k, v, kk, b, saved, dy, ex):
    T, N = r.shape[0], HEAD_DIM
    H = r.shape[1] // N
    groups = SCAN_GROUPS_BWD
    nc, hg = T // CHUNK, H // groups
    seq = pl.BlockSpec((CHUNK, hg * N), lambda g, c: (nc - 1 - c, g))
    state = pl.BlockSpec((1, hg, N, N), lambda g, c: (nc - 1 - c, g, 0, 0))

    def body(*refs):
        (r_ref, lw_ref, k_ref, v_ref, kk_ref, b_ref, hs_ref, xs_ref, dy_ref), outs, (dh_scr,), ex_refs = _ex_split(ex, refs, 9, 6)
        g, c = pl.program_id(0), pl.program_id(1)

        @pl.when(jnp.logical_and(g == 0, c == 0))
        def _():
            ex.start(*ex_refs)

        @pl.when(c == 0)
        def _():
            dh_scr[...] = jnp.zeros_like(dh_scr)

        x_known = xs_ref[0]
        _, vjp = jax.vjp(lambda *a: f_rwkv_chunk(*a, x_known=x_known)[:2], hs_ref[0],
                         *[_split_heads(z[...]) for z in (r_ref, lw_ref, k_ref, v_ref, kk_ref, b_ref)])
        d = vjp((_split_heads(dy_ref[...]), dh_scr[...]))
        dh_scr[...] = d[0]
        for o_ref, dz in zip(outs, d[1:]):
            o_ref[...] = _merge_heads(dz)

        @pl.when(jnp.logical_and(g == groups - 1, c == nc - 1))
        def _():
            ex.forward(*ex_refs)
            ex.wait(*ex_refs)

    res = pl.pallas_call(
        body, grid=(groups, nc),
        in_specs=[seq] * 6 + [state, state, seq] + [_ANY] * ex.n,
        out_specs=[seq] * 6 + [_ANY] * ex.n, out_shape=[jax.ShapeDtypeStruct((T, H * N), F32)] * 6 + ex.out_shape(),
        scratch_shapes=[pltpu.VMEM((hg, N, N), F32)] + ex.scratch(),
        compiler_params=_cparams(("arbitrary", "arbitrary")), name="rwkv_scan_bwd")(r, lw, k, v, kk, b, *saved, dy, *ex.operands())
    return list(res[:6]), list(res[6:])


GROUP_COLS = 4 * HEAD_DIM


def _f_dilattn(has_prev, q, kc, kp, vc, vp):
    scale = 1.0 / math.sqrt(HEAD_DIM)
    i = lax.broadcasted_iota(jnp.int32, (DIL_BLOCK, DIL_BLOCK), 0)
    j = lax.broadcasted_iota(jnp.int32, (DIL_BLOCK, DIL_BLOCK), 1)
    o, l = jnp.zeros_like(q), jnp.zeros_like(q)
    for h in range(q.shape[1] // HEAD_DIM):
        m = _head_mask(q.shape[1], h)
        sc = jnp.where(j <= i, mm_nt(q * m, kc) * scale, NEG_INF)
        sp = jnp.where(jnp.logical_and(i <= j, has_prev), mm_nt(q * m, kp) * scale, NEG_INF)
        mx = jnp.maximum(jnp.max(sc, axis=-1, keepdims=True), jnp.max(sp, axis=-1, keepdims=True))
        pc, pp = jnp.exp(sc - mx), jnp.exp(sp - mx)
        den = jnp.sum(pc, axis=-1, keepdims=True) + jnp.sum(pp, axis=-1, keepdims=True)
        o = o + (mm_nn(pc, vc) + mm_nn(pp, vp)) / den * m
        l = l + (mx + jnp.log(den)) * m
    return o, l


def _dil_specs(gi, d):
    parts = 1 if d == 1 else 2
    blk = (DIL_BLOCK * d, GROUP_COLS // parts)
    at = lambda col: (lambda p, n: (n, col * parts + p))
    before = lambda col: (lambda p, n: (jnp.maximum(n - 1, 0), col * parts + p))
    v0 = DIL_WIDTH // GROUP_COLS + gi
    q = pl.BlockSpec(blk, at(gi))
    kc, kp = pl.BlockSpec(blk, at(gi)), pl.BlockSpec(blk, before(gi))
    vc, vp = pl.BlockSpec(blk, at(v0)), pl.BlockSpec(blk, before(v0))
    out = pl.BlockSpec(blk, at(0))
    together = min(d, 2)
    return (q, kc, kp, vc, vp, out), parts, together


def _residue_rows(r, d):
    return pl.ds(r, DIL_BLOCK, stride=d) if d > 1 else pl.ds(0, DIL_BLOCK)


def dil_fwd(q, k, kv, gi, d, name):
    T = q.shape[0]
    (qs, kc, kp, vc, vp, out), parts, together = _dil_specs(gi, d)

    def body(q_ref, kc_ref, kp_ref, vc_ref, vp_ref, o_ref, l_ref):
        has_prev = pl.program_id(1) > 0

        def residues(it, carry):
            rows = [_residue_rows(it * together + a, d) for a in range(together)]
            ins = [[ref[rw, :] for ref in (q_ref, kc_ref, kp_ref, vc_ref, vp_ref)] for rw in rows]
            res = [_f_dilattn(has_prev, *x) for x in ins]
            for rw, (o, l) in zip(rows, res):
                o_ref[rw, :] = o
                l_ref[rw, :] = l
            return carry

        lax.fori_loop(0, d // together, residues, 0)

    shape = jax.ShapeDtypeStruct((T, 4 * HEAD_DIM), F32)
    return pl.pallas_call(
        body, grid=(parts, T // (DIL_BLOCK * d)), in_specs=[qs, kc, kp, vc, vp], out_specs=[out, out], out_shape=[shape, shape],
        compiler_params=_cparams(("parallel", "parallel")), name=name)(q, k, k, kv, kv)


def dil_bwd(q, k, kv, do, dl, gi, d, name):
    T = q.shape[0]
    (qs, kc, kp, vc, vp, out), parts, together = _dil_specs(gi, d)

    def body(q_ref, kc_ref, kp_ref, vc_ref, vp_ref, do_ref, dl_ref, *outs):
        f = functools.partial(_f_dilattn, pl.program_id(1) > 0)

        def residues(it, carry):
            rows = [_residue_rows(it * together + a, d) for a in range(together)]
            ins = [[ref[rw, :] for ref in (q_ref, kc_ref, kp_ref, vc_ref, vp_ref, do_ref, dl_ref)] for rw in rows]
            res = [jax.vjp(f, *x[:5])[1]((x[5], x[6])) for x in ins]
            for rw, gs in zip(rows, res):
                for o_ref, g in zip(outs, gs):
                    o_ref[rw, :] = g
            return carry

        lax.fori_loop(0, d // together, residues, 0)

    shape = jax.ShapeDtypeStruct((T, 4 * HEAD_DIM), F32)
    dq, dkc, dkp, dvc, dvp = pl.pallas_call(
        body, grid=(parts, T // (DIL_BLOCK * d)), in_specs=[qs, kc, kp, vc, vp, out, out], out_specs=[out] * 5, out_shape=[shape] * 5,
        compiler_params=_cparams(("parallel", "parallel")), name=name)(q, k, k, kv, kv, do, dl)

    def own_plus_next(c, p):
        return c + jnp.concatenate([p[DIL_BLOCK * d:], jnp.zeros_like(p[:DIL_BLOCK * d])], axis=0)

    return dq, own_plus_next(dkc, dkp), own_plus_next(dvc, dvp)


CONV_TILE = 256


def _conv3(before, u, w, b):
    ue = jnp.concatenate([before, u], axis=0)
    s1, s2 = pltpu.roll(ue, 1, 0)[8:], pltpu.roll(ue, 2, 0)[8:]
    return b + w[0:1] * s2 + w[1:2] * s1 + w[2:3] * u, s1, s2


def _conv_halves(u_ref, h_ref, cw_ref, cb_ref):
    F = D_FF
    res = []
    for lo in (0, F):
        before = jnp.where(pl.program_id(0) > 0, h_ref[:, lo:lo + F], 0.0)
        u = u_ref[:, lo:lo + F]
        res.append((u,) + _conv3(before, u, cw_ref[:, lo:lo + F], cb_ref[:, lo:lo + F]))
    return res


def _halo_before(C):
    return pl.BlockSpec((8, C), lambda i: (jnp.maximum(i * (CONV_TILE // 8) - 1, 0), 0))


def convgate_fwd(u, cw, cb, name):
    T, C = u.shape
    F = C // 2

    def body(u_ref, h_ref, cw_ref, cb_ref, z_ref):
        (_, cg, _, _), (_, cv, _, _) = _conv_halves(u_ref, h_ref, cw_ref, cb_ref)
        z_ref[...] = (cg * _sigmoid(cg) * cv).astype(BF16)

    return pl.pallas_call(
        body, grid=(T // CONV_TILE,),
        in_specs=[pl.BlockSpec((CONV_TILE, C), lambda i: (i, 0)), _halo_before(C), _full_spec(cw), _full_spec(cb)],
        out_specs=pl.BlockSpec((CONV_TILE, F), lambda i: (i, 0)), out_shape=jax.ShapeDtypeStruct((T, F), BF16),
        compiler_params=_cparams(("parallel",)), name=name)(u, u, cw, cb)


def convgate_bwd(u, cw, cb, dz, name):
    T, C = u.shape
    F = C // 2
    n = T // CONV_TILE
    E = CONV_TILE + 8

    def body(u_ref, hb_ref, ha_ref, cw_ref, cb_ref, dz_ref, dza_ref, du_ref, dcw_ref, dcb_ref):
        i = pl.program_id(0)
        dze = jnp.concatenate([dz_ref[...], jnp.where(i < n - 1, dza_ref[...], 0.0)], axis=0)

        @pl.when(i == 0)
        def _():
            dcw_ref[...] = jnp.zeros_like(dcw_ref)
            dcb_ref[...] = jnp.zeros_like(dcb_ref)

        halves = []
        for lo in (0, F):
            sl = slice(lo, lo + F)
            ue = jnp.concatenate([u_ref[:, sl], ha_ref[:, sl]], axis=0)
            c, s1, s2 = _conv3(jnp.where(i > 0, hb_ref[:, sl], 0.0), ue, cw_ref[:, sl], cb_ref[:, sl])
            halves.append((sl, ue, c, s1, s2))
        (_, _, cg, _, _), (_, _, cv, _, _) = halves
        sg = _sigmoid(cg)
        dcs = (dze * cv * sg * (1.0 + cg * (1.0 - sg)), dze * cg * sg)
        for (sl, ue, _, s1, s2), dc in zip(halves, dcs):
            own = lambda z: z[:CONV_TILE]
            dcb_ref[:, sl] += jnp.sum(own(dc), axis=0, keepdims=True)
            dcw_ref[0:1, sl] += jnp.sum(own(dc * s2), axis=0, keepdims=True)
            dcw_ref[1:2, sl] += jnp.sum(own(dc * s1), axis=0, keepdims=True)
            dcw_ref[2:3, sl] += jnp.sum(own(dc * ue), axis=0, keepdims=True)
            du = cw_ref[2:3, sl] * dc + cw_ref[1:2, sl] * pltpu.roll(dc, E - 1, 0) + cw_ref[0:1, sl] * pltpu.roll(dc, E - 2, 0)
            du_ref[:, sl] = own(du).astype(BF16)

    after = lambda w: pl.BlockSpec((8, w), lambda i: (jnp.minimum((i + 1) * (CONV_TILE // 8), T // 8 - 1), 0))
    return pl.pallas_call(
        body, grid=(n,),
        in_specs=[pl.BlockSpec((CONV_TILE, C), lambda i: (i, 0)), _halo_before(C), after(C), _full_spec(cw), _full_spec(cb),
                  pl.BlockSpec((CONV_TILE, F), lambda i: (i, 0)), after(F)],
        out_specs=[pl.BlockSpec((CONV_TILE, C), lambda i: (i, 0)), _full_spec(cw), _full_spec(cb)],
        out_shape=[jax.ShapeDtypeStruct((T, C), BF16), jax.ShapeDtypeStruct(cw.shape, F32), jax.ShapeDtypeStruct(cb.shape, F32)],
        compiler_params=_cparams(("arbitrary",)), name=name)(u, u, u, cw, cb, dz, dz)


def loss_head(y, tgt):
    T, D = y.shape
    tile = ROW_TILE

    def body(y_ref, t_ref, l_ref, d_ref, db_ref):
        d = y_ref[...] - t_ref[...]
        d_ref[...] = d * (1.0 / D)
        db_ref[...] = (d * (1.0 / D)).astype(BF16)

        @pl.when(pl.program_id(0) == 0)
        def _():
            l_ref[...] = jnp.zeros_like(l_ref)

        l_ref[...] += (0.5 / D) * jnp.sum(d * d)

    row = pl.BlockSpec((tile, D), lambda i: (i, 0))
    return pl.pallas_call(
        body, grid=(T // tile,), in_specs=[row, row], out_specs=[pl.BlockSpec((8, 128), lambda i: (0, 0)), row, row],
        out_shape=[jax.ShapeDtypeStruct((8, 128), F32), jax.ShapeDtypeStruct((T, D), F32), jax.ShapeDtypeStruct((T, D), BF16)],
        compiler_params=_cparams(("arbitrary",)), name="loss_head")(y, tgt)


def sum_parts(parts, name):
    S, R, C = parts.shape
    tile = _pick(R, (256, 128, 64, 32, 16, 8))

    def body(p_ref, o_ref):
        acc = p_ref[0]
        for s in range(1, S):
            acc = acc + p_ref[s]
        o_ref[...] = acc

    return pl.pallas_call(
        body, grid=(R // tile,), in_specs=[pl.BlockSpec((S, tile, C), lambda i: (0, i, 0))],
        out_specs=pl.BlockSpec((tile, C), lambda i: (i, 0)), out_shape=jax.ShapeDtypeStruct((R, C), F32),
        compiler_params=_cparams(("parallel",)), name=name)(parts)


def adamw(gparts, w, m, v, name):
    S, R, C = gparts.shape
    tile = _pick(R, (256, 128, 64, 32, 16, 8))
    c1 = 1.0 / (1.0 - ADAM_B1 ** ADAM_STEP)
    c2 = 1.0 / (1.0 - ADAM_B2 ** ADAM_STEP)

    def body(g_ref, w_ref, m_ref, v_ref, go_ref, d_ref, mo_ref, vo_ref):
        g = g_ref[0].astype(F32)
        for s in range(1, S):
            g = g + g_ref[s].astype(F32)
        m1 = ADAM_B1 * m_ref[...] + (1.0 - ADAM_B1) * g
        v1 = ADAM_B2 * v_ref[...] + (1.0 - ADAM_B2) * (g * g)
        go_ref[...] = g
        mo_ref[...] = m1
        vo_ref[...] = v1
        d_ref[...] = -ADAM_LR * ((m1 * c1) / (jnp.sqrt(v1 * c2) + ADAM_EPS) + ADAM_WD * w_ref[...])

    row = pl.BlockSpec((tile, C), lambda i: (i, 0))
    return pl.pallas_call(
        body, grid=(R // tile,), in_specs=[pl.BlockSpec((S, tile, C), lambda i: (0, i, 0)), row, row, row],
        out_specs=[row] * 4, out_shape=[jax.ShapeDtypeStruct((R, C), F32)] * 4,
        compiler_params=_cparams(("parallel",)), name=name)(gparts, w, m, v)


def _peers():
    x, y, c = lax.axis_index("x"), lax.axis_index("y"), lax.axis_index("c")
    peers = []
    for k in range(1, N_DEV):
        px = 1 - x if k & 4 else x
        py = 1 - y if k & 2 else y
        pc = 1 - c if k & 1 else c
        peers.append(((px, py, pc), 4 * px + 2 * py + pc))
    return 4 * x + 2 * y + c, peers


_ANY = pl.BlockSpec(memory_space=pl.ANY)


class Exchange:
    def __init__(self, gathers=(), scatters=()):
        self.gathers, self.scatters = list(gathers), list(scatters)
        self.n = len(self.gathers) + len(self.scatters)

    def operands(self):
        return self.gathers + self.scatters

    def out_shape(self):
        return ([jax.ShapeDtypeStruct((N_DEV,) + x.shape, x.dtype) for x in self.gathers]
                + [jax.ShapeDtypeStruct(x.shape, x.dtype) for x in self.scatters])

    def scratch(self):
        n = max(self.n, 1)
        return [pltpu.SemaphoreType.DMA((7 * n,)), pltpu.SemaphoreType.DMA((7 * n,)), pltpu.SemaphoreType.DMA((n,))]

    def _copies(self, in_refs, out_refs, send_sems, recv_sems, local_sems):
        me, peers = _peers()
        ng = len(self.gathers)
        local, sends, recvs = [], [], []
        for a in range(self.n):
            x, o = in_refs[a], out_refs[a]
            mine = x if a < ng else x.at[me]
            local.append(pltpu.make_async_copy(mine, o.at[me], local_sems.at[a]))
            s_a, r_a = {}, {}
            for k in range(1, N_DEV):
                peer, slot = peers[k - 1]
                sems = dict(send_sem=send_sems.at[7 * a + k - 1], recv_sem=recv_sems.at[7 * a + k - 1],
                            device_id_type=pl.DeviceIdType.MESH)
                if a >= ng:
                    s_a[k] = pltpu.make_async_remote_copy(src_ref=x.at[slot], dst_ref=o.at[me], device_id=peer, **sems)
                elif k in FORWARDED:
                    came = o.at[peers[k - 2][1]]
                    s_a[k] = pltpu.make_async_remote_copy(src_ref=came, dst_ref=came, device_id=peers[0][0], **sems)
                else:
                    s_a[k] = pltpu.make_async_remote_copy(src_ref=x, dst_ref=o.at[me], device_id=peer, **sems)
                r_a[k] = pltpu.make_async_remote_copy(src_ref=mine, dst_ref=o.at[slot], device_id=peer, **sems)
            sends.append(s_a)
            recvs.append(r_a)
        return local, sends, recvs

    def start(self, *refs):
        if self.n == 0:
            return
        local, sends, _ = self._copies(*refs)
        for a in range(self.n):
            local[a].start()
            for k in range(1, N_DEV):
                if a >= len(self.gathers) or k not in FORWARDED:
                    sends[a][k].start()

    def forward(self, *refs):
        if not self.gathers:
            return
        _, sends, recvs = self._copies(*refs)
        for a in range(len(self.gathers)):
            for k in FORWARDED:
                recvs[a][k - 1].wait_recv()
                sends[a][k].start()

    def wait(self, *refs):
        if self.n == 0:
            return
        local, sends, recvs = self._copies(*refs)
        for a in range(self.n):
            waited_early = [f - 1 for f in FORWARDED] if a < len(self.gathers) else []
            for k in range(1, N_DEV):
                if k not in waited_early:
                    recvs[a][k].wait_recv()
            for k in range(1, N_DEV):
                sends[a][k].wait_send()
            local[a].wait()


FORWARDED = (3, 5, 7)


def exchange(ex, name):
    n = ex.n

    def body(*refs):
        args = (refs[:n], refs[n:2 * n]) + tuple(refs[2 * n:])
        ex.start(*args)
        ex.forward(*args)
        ex.wait(*args)

    return pl.pallas_call(body, in_specs=[_ANY] * n, out_specs=[_ANY] * n, out_shape=ex.out_shape(),
                          scratch_shapes=ex.scratch(), name=name)(*ex.operands())


def _heads(z, h):
    return z.reshape(z.shape[0], h, HEAD_DIM).transpose(1, 0, 2)


def _unheads(z):
    return z.transpose(1, 0, 2).reshape(z.shape[1], z.shape[0] * HEAD_DIM)


def _shift_up(z):
    return jnp.concatenate([z[1:], jnp.zeros_like(z[:1])], axis=0)


def _segments(width):
    seg = np.zeros((width, 128), np.float32)
    seg[np.arange(width), np.arange(width) // HEAD_DIM] = 1.0
    return jnp.asarray(seg), jnp.asarray(seg.T)


def _rope_consts(T, heads):
    inv = ROPE_THETA ** (-jnp.arange(0, HEAD_DIM, 2, dtype=F32) / HEAD_DIM)
    ang = jnp.arange(T, dtype=F32)[:, None] * inv[None, :]
    return jnp.tile(jnp.cos(ang), (1, 2 * heads)), jnp.tile(jnp.sin(ang), (1, 2 * heads))


def _per_head(g, heads):
    return jnp.tile(g.reshape(1, HEAD_DIM), (1, heads))


def _sum_heads(g):
    return g.reshape(-1, HEAD_DIM).sum(axis=0, keepdims=True)


LORA_COLS = 256
RW_TILE = 256
ROW_TILE = 512


def _local_step(x0, memx, tgt, P, ex_weights=None, weights_done=None, ex_grads=None):
    T = x0.shape[0]
    P = dict(P)
    G = {}
    seg, seg_t = _segments(RWKV_WIDTH)
    mseg = (seg[:MEM_WIDTH], seg_t[:, :MEM_WIDTH])
    cos, sin = _rope_consts(T, DIL_WIDTH // HEAD_DIM)
    row = lambda v: v.reshape(1, -1)

    def mem_fwd(i, q):
        memn = stage_fwd(f_rmsnorm, [memx], [P["mem_norm"][i:i + 1]], [], [], N_MEM, f"mem{i}_norm", [BF16])[0]
        kvm = matmul(memn, P["mem_w_kv"][i], "nn", f"mem{i}_kv")
        kn, qn = _per_head(P["mem_k_norm"][i], MEM_HEADS), _per_head(P["mem_q_norm"][i], MEM_HEADS)
        km = stage_fwd(f_headnorm, [Cols(kvm, MEM_WIDTH, 0)], [kn], [], mseg, N_MEM, f"mem{i}_knorm")[0]
        om = stage_fwd(f_memattn, [q], [km, Cols(kvm, MEM_WIDTH, 1), qn], [], mseg, ROW_TILE, f"mem{i}_attn")[0]
        return om, (memn, kvm, km, kn, qn, q)

    def mem_bwd(i, saved, dymem):
        memn, kvm, km, kn, qn, q = saved
        (dq,), (dkm, dvm, g_qn) = stage_bwd(f_memattn, [q], [km, Cols(kvm, MEM_WIDTH, 1), qn], [], mseg, [dymem], ROW_TILE,
                                            f"mem{i}_attn_bwd")
        (dkraw,), (g_kn,) = stage_bwd(f_headnorm, [Cols(kvm, MEM_WIDTH, 0)], [kn], [], mseg, [dkm], N_MEM, f"mem{i}_knorm_bwd")
        dkvm = jnp.concatenate([dkraw, dvm], axis=1).astype(BF16)
        g_w = matmul(memn, dkvm, "tn", f"mem{i}_kv_dw")
        dmemn = matmul(dkvm, P["mem_w_kv"][i], "nt", f"mem{i}_kv_dx")
        _, (g_mn,) = stage_bwd(f_rmsnorm, [memx], [P["mem_norm"][i:i + 1]], [], [], [dmemn], N_MEM, f"mem{i}_norm_bwd")
        return dq, g_mn, g_w, _sum_heads(g_qn), _sum_heads(g_kn)

    def ffn_fwd(i, xin):
        hn = stage_fwd(f_rmsnorm, [xin], [P["ffn_norm"][i:i + 1]], [], [], ROW_TILE, f"ffn{i}_norm", [BF16])[0]
        u = matmul(hn, P["ffn_w_up"][i], "nn", f"ffn{i}_up")
        z = convgate_fwd(u, P["ffn_conv_w"][i], P["ffn_conv_b"][i:i + 1], f"ffn{i}_conv")
        return matmul(z, P["ffn_w_down"][i], "nn", f"ffn{i}_down", residual=xin), (hn, u, z)

    def ffn_bwd(i, xin, saved, dxo, dxo_b):
        hn, u, z = saved
        dz = matmul(dxo_b, P["ffn_w_down"][i], "nt", f"ffn{i}_down_dx")
        g_down = matmul(z, dxo_b, "tn", f"ffn{i}_down_dw")
        du, g_cw, g_cb = convgate_bwd(u, P["ffn_conv_w"][i], P["ffn_conv_b"][i:i + 1], dz, f"ffn{i}_conv_bwd")
        dhn = matmul(du, P["ffn_w_up"][i], "nt", f"ffn{i}_up_dx")
        g_up = matmul(hn, du, "tn", f"ffn{i}_up_dw")
        (dxin,), (g_n,), (dxin_b,) = stage_bwd(f_rmsnorm_res, [xin], [P["ffn_norm"][i:i + 1]], [], [], [dhn, dxo], ROW_TILE,
                                               f"ffn{i}_norm_bwd", bf16_copies=(0,))
        return dxin, dxin_b, g_n, g_up, g_cw, g_cb, g_down

    h0 = stage_fwd(f_rmsnorm, [x0], [P["attn_norm"][0:1]], [], [], ROW_TILE, "l0_norm", [BF16])[0]
    p0 = matmul(h0, P["a_w_in"][0], "nn", "l0_in")
    lora0 = 3 * RWKV_WIDTH // LORA_COLS
    pre_xs = [Cols(p0, RWKV_WIDTH, 0), Cols(p0, RWKV_WIDTH, 1), Cols(p0, RWKV_WIDTH, 2), Cols(p0, LORA_COLS, lora0)]
    mu = [Cols(P["a_mu"], RWKV_WIDTH, 0), Cols(P["a_mu"], RWKV_WIDTH, 1), Cols(P["a_mu"], RWKV_WIDTH, 2),
          Cols(P["a_mu"], LORA_COLS, lora0)]
    lora_rows = lambda w, lo: jnp.pad(w, ((lo, LORA_COLS - lo - w.shape[0]), (0, 0)))
    pre_ps = mu + [P["a_w0"], lora_rows(P["a_w2"][0], 0), P["a_a0"], lora_rows(P["a_a2"][0], 64), lora_rows(P["a_g2"][0], 128),
                   P["a_k_k"], P["a_k_a"]]
    r, lw, k2, v, kk, b, g = stage_fwd(f_rwkv_pre, pre_xs, pre_ps, [], [seg, seg_t], RW_TILE, "l0_rwkv_pre", with_prev=True)
    scan_in = [r, lw, k2, v, kk, b]
    y_h, h_states, got = rwkv_scan_fwd(*scan_in, ex_weights or Exchange())
    if weights_done is not None:
        P.update(weights_done(got))
    y_s = y_h
    post_ps = [P["a_lnx_w"], P["a_lnx_b"], P["a_r_k"].reshape(1, RWKV_WIDTH)]
    ymix0 = stage_fwd(f_rwkv_post, [y_s, r, k2, v, g], post_ps, [], [seg, seg_t], RW_TILE, "l0_rwkv_post")[0]
    ymem0, mem0_saved = mem_fwd(0, Cols(p0, MEM_WIDTH, SHIFT_WIDTH // MEM_WIDTH))
    ycat0 = jnp.concatenate([ymix0, ymem0], axis=1).astype(BF16)
    x1 = matmul(ycat0, P["a_w_out"][0], "nn", "l0_out", residual=x0)
    x2, ffn0_saved = ffn_fwd(0, x1)

    hk, h1 = stage_fwd(f_rmsnorm2, [x2], [row(P["kv_norm"]), P["attn_norm"][1:2]], [], [], ROW_TILE, "l1_norm", [BF16, BF16])
    kvp = matmul(hk, P["kv_w"][0], "nn", "l1_kv")
    p1 = matmul(h1, P["b_w_in"][0], "nn", "l1_in")
    kraw, qraw = Cols(kvp, DIL_WIDTH, 0), Cols(p1, DIL_WIDTH, 0)
    kgain, qgain = _per_head(P["kv_k_norm"], DIL_WIDTH // HEAD_DIM), _per_head(P["b_q_norm"], DIL_WIDTH // HEAD_DIM)
    ksh = stage_fwd(f_qkprep, [kraw], [kgain], [cos, sin], [seg, seg_t], ROW_TILE, "l1_kprep")[0]
    q = stage_fwd(f_qkprep, [qraw], [qgain], [cos, sin], [seg, seg_t], ROW_TILE, "l1_qprep")[0]
    outs, lses = [], []
    for gi, (_, d) in enumerate(DIL_GROUPS):
        og, lg = dil_fwd(q, ksh, kvp, gi, d, f"l1_dil{gi}")
        outs.append(og)
        lses.append(lg)
    omix = stage_fwd(f_mix, outs + lses, [], [], [], ROW_TILE, "l1_mix")[0]
    ymem1, mem1_saved = mem_fwd(1, Cols(p1, MEM_WIDTH, DIL_WIDTH // MEM_WIDTH))
    ycat1 = jnp.concatenate([omix, ymem1], axis=1).astype(BF16)
    x3 = matmul(ycat1, P["b_w_out"][0], "nn", "l1_out", residual=x2)
    x4, ffn1_saved = ffn_fwd(1, x3)
    loss_part, dx4, dx4_b = loss_head(x4, tgt)

    dx3, dx3_b, gn1, gup1, gcw1, gcb1, gdown1 = ffn_bwd(1, x3, ffn1_saved, dx4, dx4_b)
    dycat1 = matmul(dx3_b, P["b_w_out"][0], "nt", "l1_out_dx")
    G["b_w_out"] = [matmul(ycat1, dx3_b, "tn", "l1_out_dw")]
    dqmem1, gmn1, gmw1, gmq1, gmk1 = mem_bwd(1, mem1_saved, Cols(dycat1, MEM_WIDTH, 1))
    dmix, _ = stage_bwd(f_mix, outs + lses, [], [], [], [Cols(dycat1, MEM_WIDTH, 0)], ROW_TILE, "l1_mix_bwd")
    dq, dk, dv = zip(*[dil_bwd(q, ksh, kvp, dmix[gi], dmix[3 + gi], gi, d, f"l1_dil{gi}_bwd")
                       for gi, (_, d) in enumerate(DIL_GROUPS)])
    dq, dk, dv = jnp.concatenate(dq, axis=1), jnp.concatenate(dk, axis=1), jnp.concatenate(dv, axis=1)
    (dqraw,), (g_bq,) = stage_bwd(f_qkprep, [qraw], [qgain], [cos, sin], [seg, seg_t], [dq], ROW_TILE, "l1_qprep_bwd")
    (dkraw,), (g_kk,) = stage_bwd(f_qkprep, [kraw], [kgain], [cos, sin], [seg, seg_t], [dk], ROW_TILE, "l1_kprep_bwd")
    g_bq, g_kk = _sum_heads(g_bq), _sum_heads(g_kk)
    dp1 = jnp.concatenate([dqraw, dqmem1], axis=1).astype(BF16)
    dkvp = jnp.concatenate([dkraw, dv], axis=1).astype(BF16)
    dh1 = matmul(dp1, P["b_w_in"][0], "nt", "l1_in_dx")
    G["b_w_in"] = [matmul(h1, dp1, "tn", "l1_in_dw")]
    dhk = matmul(dkvp, P["kv_w"][0], "nt", "l1_kv_dx")
    G["kv_w"] = [matmul(hk, dkvp, "tn", "l1_kv_dw")]
    (dx2,), (g_kvn, g_an1), (dx2_b,) = stage_bwd(f_rmsnorm2_res, [x2], [row(P["kv_norm"]), P["attn_norm"][1:2]], [], [],
                                                 [dhk, dh1, dx3], ROW_TILE, "l1_norm_bwd", bf16_copies=(0,))

    dx1, dx1_b, gn0, gup0, gcw0, gcb0, gdown0 = ffn_bwd(0, x1, ffn0_saved, dx2, dx2_b)
    dycat0 = matmul(dx1_b, P["a_w_out"][0], "nt", "l0_out_dx")
    G["a_w_out"] = [matmul(ycat0, dx1_b, "tn", "l0_out_dw")]
    dqmem0, gmn0, gmw0, gmq0, gmk0 = mem_bwd(0, mem0_saved, Cols(dycat0, MEM_WIDTH, RWKV_WIDTH // MEM_WIDTH))
    (dy_s, dr_a, dk_a, dv_a, dg), (g_lw, g_lb, g_rk) = stage_bwd(
        f_rwkv_post, [y_s, r, k2, v, g], post_ps, [], [seg, seg_t], [Cols(dycat0, RWKV_WIDTH, 0)], RW_TILE, "l0_rwkv_post_bwd")
    G["mem_w_kv"], G["ffn_w_up"], G["ffn_w_down"] = [gmw0, gmw1], [gup0, gup1], [gdown0, gdown1]
    (dr_b, dlw, dk_b, dv_b, dkk, db), G["_exchanged"] = rwkv_scan_bwd(*scan_in, h_states, dy_s,
                                                                      ex_grads(G) if ex_grads else Exchange())
    dpre, gpre = stage_bwd(f_rwkv_pre, pre_xs, pre_ps, [], [seg, seg_t],
                           [[dr_a, dr_b], dlw, [dk_a, dk_b], [dv_a, dv_b], dkk, db, dg], RW_TILE, "l0_rwkv_pre_bwd", with_prev=True)
    dp_rw = jnp.concatenate(dpre[:4], axis=1) + _shift_up(jnp.concatenate(dpre[4:], axis=1))
    dp0 = jnp.concatenate([dp_rw, dqmem0], axis=1).astype(BF16)
    dh0 = matmul(dp0, P["a_w_in"][0], "nt", "l0_in_dx")
    G["a_w_in"] = [matmul(h0, dp0, "tn", "l0_in_dw")]
    (dx0,), (g_an0,) = stage_bwd(f_rmsnorm_res, [x0], [P["attn_norm"][0:1]], [], [], [dh0, dx1], ROW_TILE, "l0_norm_bwd")

    G["attn_norm"] = jnp.concatenate([g_an0, g_an1], axis=0)
    G["a_mu"] = jnp.concatenate(gpre[:4], axis=1)
    G["a_w0"], G["a_w2"], G["a_a0"], G["a_a2"], G["a_g2"] = gpre[4], gpre[5][None, :64], gpre[6], gpre[7][None, 64:128], gpre[8][None, 128:]
    G["a_k_k"], G["a_k_a"] = gpre[9], gpre[10]
    G["a_r_k"] = g_rk.reshape(1, RWKV_HEADS, HEAD_DIM)
    G["a_lnx_w"], G["a_lnx_b"] = g_lw, g_lb
    G["kv_norm"], G["kv_k_norm"], G["b_q_norm"] = g_kvn.reshape(-1), g_kk.reshape(-1), g_bq
    G["mem_norm"] = jnp.concatenate([gmn0, gmn1], axis=0)
    G["mem_w_kv"] = [gmw0, gmw1]
    G["mem_q_norm"] = jnp.concatenate([gmq0, gmq1], axis=0)
    G["mem_k_norm"] = jnp.concatenate([gmk0, gmk1], axis=0)
    G["ffn_norm"] = jnp.concatenate([gn0, gn1], axis=0)
    G["ffn_w_up"] = [gup0, gup1]
    G["ffn_conv_w"] = jnp.stack([gcw0, gcw1])
    G["ffn_conv_b"] = jnp.concatenate([gcb0, gcb1], axis=0)
    G["ffn_w_down"] = [gdown0, gdown1]
    return loss_part, dx0, G


PARAMS = (("attn_norm", None), ("a_w_in", 2), ("a_mu", 1), ("a_w0", 1), ("a_w2", 2), ("a_a0", 1), ("a_a2", 2), ("a_g2", 2),
          ("a_k_k", 1), ("a_k_a", 1), ("a_r_k", None), ("a_lnx_w", 1), ("a_lnx_b", 1), ("a_w_out", 1), ("kv_norm", None),
          ("kv_w", 1), ("kv_k_norm", None), ("b_w_in", 1), ("b_q_norm", None), ("b_w_out", 2), ("mem_norm", None),
          ("mem_w_kv", 1), ("mem_q_norm", None), ("mem_k_norm", None), ("ffn_norm", None), ("ffn_w_up", 2),
          ("ffn_conv_w", 2), ("ffn_conv_b", None), ("ffn_w_down", 1))
BIG = ("a_w_in", "a_w_out", "kv_w", "b_w_in", "b_w_out", "mem_w_kv", "ffn_w_up", "ffn_w_down")
AXIS = dict(PARAMS)
SMALL = tuple(n for n, _ in PARAMS if n not in BIG)
SMALL_SHARDED = tuple(n for n in SMALL if AXIS[n] is not None)
PACK_QUANTUM = 256 * 128


def _from_shards(xs, axis):
    full = jnp.moveaxis(xs, 0, axis)
    sh = full.shape
    return full.reshape(sh[:axis] + (sh[axis] * sh[axis + 1],) + sh[axis + 2:])


def _to_shards(g, axis):
    sh = g.shape
    return jnp.moveaxis(g.reshape(sh[:axis] + (N_DEV, sh[axis] // N_DEV) + sh[axis + 1:]), axis, 0)


def _pack(parts, lead=0):
    ld = parts[0].shape[:lead]
    flat = jnp.concatenate([p.reshape(ld + (-1,)) for p in parts], axis=-1)
    pad = (-flat.shape[-1]) % PACK_QUANTUM
    flat = jnp.pad(flat, [(0, 0)] * lead + [(0, pad)])
    return flat.reshape(ld + (-1, 128))


def _unpack(packed, shapes, lead=0):
    ld = packed.shape[:lead]
    flat = packed.reshape(ld + (-1,))
    out, off = [], 0
    for s in shapes:
        n = math.prod(s)
        out.append(flat[..., off:off + n].reshape(ld + tuple(s)))
        off += n
    return out


def kernel(x, mem, attn_norm, a_w_in, a_mu, a_w0, a_w2, a_a0, a_a2, a_g2, a_k_k, a_k_a, a_r_k, a_lnx_w, a_lnx_b, a_w_out, kv_norm, kv_w, kv_k_norm, b_w_in, b_q_norm, b_w_out, mem_norm, mem_w_kv, mem_q_norm, mem_k_norm, ffn_norm, ffn_w_up, ffn_conv_w, ffn_conv_b, ffn_w_down, loss_target, m_attn_norm, m_a_w_in, m_a_mu, m_a_w0, m_a_w2, m_a_a0, m_a_a2, m_a_g2, m_a_k_k, m_a_k_a, m_a_r_k, m_a_lnx_w, m_a_lnx_b, m_a_w_out, m_kv_norm, m_kv_w, m_kv_k_norm, m_b_w_in, m_b_q_norm, m_b_w_out, m_mem_norm, m_mem_w_kv, m_mem_q_norm, m_mem_k_norm, m_ffn_norm, m_ffn_w_up, m_ffn_conv_w, m_ffn_conv_b, m_ffn_w_down, v_attn_norm, v_a_w_in, v_a_mu, v_a_w0, v_a_w2, v_a_a0, v_a_a2, v_a_g2, v_a_k_k, v_a_k_a, v_a_r_k, v_a_lnx_w, v_a_lnx_b, v_a_w_out, v_kv_norm, v_kv_w, v_kv_k_norm, v_b_w_in, v_b_q_norm, v_b_w_out, v_mem_norm, v_mem_w_kv, v_mem_q_norm, v_mem_k_norm, v_ffn_norm, v_ffn_w_up, v_ffn_conv_w, v_ffn_conv_b, v_ffn_w_down):
    names = [n for n, _ in PARAMS]
    vals = (attn_norm, a_w_in, a_mu, a_w0, a_w2, a_a0, a_a2, a_g2, a_k_k, a_k_a, a_r_k, a_lnx_w, a_lnx_b, a_w_out, kv_norm, kv_w, kv_k_norm, b_w_in, b_q_norm, b_w_out, mem_norm, mem_w_kv, mem_q_norm, mem_k_norm, ffn_norm, ffn_w_up, ffn_conv_w, ffn_conv_b, ffn_w_down)
    m_vals = (m_attn_norm, m_a_w_in, m_a_mu, m_a_w0, m_a_w2, m_a_a0, m_a_a2, m_a_g2, m_a_k_k, m_a_k_a, m_a_r_k, m_a_lnx_w, m_a_lnx_b, m_a_w_out, m_kv_norm, m_kv_w, m_kv_k_norm, m_b_w_in, m_b_q_norm, m_b_w_out, m_mem_norm, m_mem_w_kv, m_mem_q_norm, m_mem_k_norm, m_ffn_norm, m_ffn_w_up, m_ffn_conv_w, m_ffn_conv_b, m_ffn_w_down)
    v_vals = (v_attn_norm, v_a_w_in, v_a_mu, v_a_w0, v_a_w2, v_a_a0, v_a_a2, v_a_g2, v_a_k_k, v_a_k_a, v_a_r_k, v_a_lnx_w, v_a_lnx_b, v_a_w_out, v_kv_norm, v_kv_w, v_kv_k_norm, v_b_w_in, v_b_q_norm, v_b_w_out, v_mem_norm, v_mem_w_kv, v_mem_q_norm, v_mem_k_norm, v_ffn_norm, v_ffn_w_up, v_ffn_conv_w, v_ffn_conv_b, v_ffn_w_down)
    W, M, V = dict(zip(names, vals)), dict(zip(names, m_vals)), dict(zip(names, v_vals))
    me = 4 * lax.axis_index("x") + 2 * lax.axis_index("y") + lax.axis_index("c")
    layers = lambda D, n: [D[n]] if D[n].ndim == 2 else [D[n][i] for i in range(D[n].shape[0])]
    ax2 = lambda n: AXIS[n] - (W[n].ndim - 2)
    first = [("a_w_in", 0)]
    later = [(n, i) for n in BIG if n != "a_w_in" for i in range(len(layers(W, n)))]

    small_shapes = [W[n].shape for n in SMALL_SHARDED]
    got_w, got_small = exchange(Exchange(gathers=[W["a_w_in"][0].astype(BF16), _pack([W[n] for n in SMALL_SHARDED])]),
                                "gather_first")
    P = {n: W[n] for n in SMALL}
    P["a_w_in"] = [_from_shards(got_w, ax2("a_w_in"))]
    for n, s in zip(SMALL_SHARDED, _unpack(got_small, small_shapes, lead=1)):
        P[n] = _from_shards(s, AXIS[n])
    ex_weights = Exchange(gathers=[layers(W, n)[i].astype(BF16) for n, i in later])

    def weights_done(got):
        out = {}
        for (n, _), g in zip(later, got):
            out.setdefault(n, []).append(_from_shards(g, ax2(n)))
        return out

    slots = lambda G, n: jnp.stack([_to_shards(g, ax2(n)) for g in G[n]], axis=1)
    later_names = [n for n in BIG if n != "a_w_in"]
    ex_grads = lambda G: Exchange(scatters=[slots(G, n) for n in later_names])
    loss_part, dx0, G = _local_step(x[0], mem[0], loss_target[0], P, ex_weights, weights_done, ex_grads)
    loss = lax.psum(loss_part[0, 0], ("x", "y", "c"))
    gparts = dict(zip(later_names, G.pop("_exchanged")))
    got_gsmall, gparts["a_w_in"] = exchange(
        Exchange(gathers=[_pack([G[n] for n in SMALL])], scatters=[slots(G, "a_w_in")]), "exchange_last")

    results = {}
    for n in BIG:
        rows = lambda z: z.reshape((-1,) + z.shape[-1:])
        res = adamw(gparts[n].reshape((N_DEV, -1) + gparts[n].shape[-1:]), rows(W[n]), rows(M[n]), rows(V[n]), f"adamw_{n}")
        results[n] = [r.reshape(W[n].shape) for r in res]
    g_small = sum_parts(got_gsmall, "sum_small_grads")
    mine = []
    for n, g in zip(SMALL, _unpack(g_small, [G[n].shape for n in SMALL])):
        if AXIS[n] is not None:
            s = W[n].shape[AXIS[n]]
            g = lax.dynamic_slice_in_dim(g, me * s, s, axis=AXIS[n])
        mine.append(g)
    res = adamw(_pack(mine)[None], _pack([W[n] for n in SMALL]), _pack([M[n] for n in SMALL]), _pack([V[n] for n in SMALL]),
                "adamw_small")
    for n, parts in zip(SMALL, zip(*[_unpack(r, [W[n].shape for n in SMALL]) for r in res])):
        results[n] = list(parts)
    outs = [[results[n][j] for n in names] for j in range(4)]
    return (loss, dx0[None], *outs[0], *outs[1], *outs[2], *outs[3])
```

```python
import functools
import math

import jax
import jax.numpy as jnp
import numpy as np
from jax import lax
from jax.experimental import pallas as pl
from jax.experimental.pallas import tpu as pltpu

F32 = jnp.float32
BF16 = jnp.bfloat16
HI = lax.Precision.HIGHEST
H3 = lax.Precision.HIGH

N_DEV = 8
D_MODEL = 1024
HEAD_DIM = 64
N_MEM = 256
MEM_HEADS = 4
MEM_WIDTH = 256
RWKV_HEADS = 12
RWKV_WIDTH = 768
SHIFT_WIDTH = 2560
DIL_GROUPS = ((128, 1), (512, 4), (2048, 16))
DIL_BLOCK = 128
DIL_WIDTH = 768
D_FF = 2816
RMS_EPS = 1e-6
LNX_EPS = 64e-5
NEG_INF = -1e30
ROPE_THETA = 10000.0
ADAM_LR, ADAM_B1, ADAM_B2, ADAM_EPS, ADAM_WD, ADAM_STEP = 0.001, 0.9, 0.999, 1e-08, 0.01, 10

CHUNK = 64
SCAN_GROUPS_FWD, SCAN_GROUPS_BWD = 1, 1
MM_TILE_CAP = 1408
VMEM_LIMIT_V7X = 48 * 1024 * 1024


def _cparams(sem):
    return pltpu.CompilerParams(dimension_semantics=sem, vmem_limit_bytes=VMEM_LIMIT_V7X)


def _pick(n, cands):
    for c in cands:
        if n % c == 0:
            return c
    return n


def _tile(n, cap):
    if n <= cap:
        return n
    for d in range(cap - cap % 128, 0, -128):
        if n % d == 0:
            return d
    return n


def _dg(a, b, ca, cb, batch):
    dims = (((ca,), (cb,)), ((0,), (0,))) if batch else (((ca,), (cb,)), ((), ()))
    return lax.dot_general(a.astype(BF16), b.astype(BF16), dims, preferred_element_type=F32)


@jax.custom_vjp
def mm_nn(a, b):
    n = a.ndim
    return _dg(a, b, n - 1, n - 2, n == 3)


def _mm_nn_fwd(a, b):
    return mm_nn(a, b), (a, b)


def _mm_nn_bwd(res, g):
    a, b = res
    n = a.ndim
    return _dg(g, b, n - 1, n - 1, n == 3), _dg(a, g, n - 2, n - 2, n == 3)


mm_nn.defvjp(_mm_nn_fwd, _mm_nn_bwd)


@jax.custom_vjp
def mm_nt(a, b):
    n = a.ndim
    return _dg(a, b, n - 1, n - 1, n == 3)


def _mm_nt_fwd(a, b):
    return mm_nt(a, b), (a, b)


def _mm_nt_bwd(res, g):
    a, b = res
    n = a.ndim
    return _dg(g, b, n - 1, n - 2, n == 3), _dg(g, a, n - 2, n - 2, n == 3)


mm_nt.defvjp(_mm_nt_fwd, _mm_nt_bwd)


def mmh(a, b, precision=H3):
    n = a.ndim
    dims = (((n - 1,), (n - 2,)), ((0,), (0,))) if n == 3 else (((1,), (0,)), ((), ()))
    return lax.dot_general(a, b, dims, precision=precision, preferred_element_type=F32)


def mmh_nt(a, b):
    n = a.ndim
    dims = (((n - 1,), (n - 1,)), ((0,), (0,))) if n == 3 else (((1,), (1,)), ((), ()))
    return lax.dot_general(a, b, dims, precision=H3, preferred_element_type=F32)


def mmh_tn(a, b):
    n = a.ndim
    dims = (((n - 2,), (n - 2,)), ((0,), (0,))) if n == 3 else (((0,), (0,)), ((), ()))
    return lax.dot_general(a, b, dims, precision=H3, preferred_element_type=F32)


def matmul(a, b, mode, name, residual=None):
    out_dtype = BF16 if mode == "tn" else F32
    if mode == "nn":
        (M, K), (_, N) = a.shape, b.shape
    elif mode == "nt":
        (M, K), (N, _) = a.shape, b.shape
    else:
        (K, M), (_, N) = a.shape, b.shape
    tm = _tile(M, 2048 if mode == "nn" else MM_TILE_CAP)
    tn = _tile(N, 512 if mode == "nn" else MM_TILE_CAP)
    tk = _tile(K, MM_TILE_CAP)
    nk = K // tk
    if mode == "nn":
        a_spec = pl.BlockSpec((tm, tk), lambda i, j, k: (i, k))
        b_spec = pl.BlockSpec((tk, tn), lambda i, j, k: (k, j))
        dims = (((1,), (0,)), ((), ()))
    elif mode == "nt":
        a_spec = pl.BlockSpec((tm, tk), lambda i, j, k: (i, k))
        b_spec = pl.BlockSpec((tn, tk), lambda i, j, k: (j, k))
        dims = (((1,), (1,)), ((), ()))
    else:
        a_spec = pl.BlockSpec((tk, tm), lambda i, j, k: (k, i))
        b_spec = pl.BlockSpec((tk, tn), lambda i, j, k: (k, j))
        dims = (((0,), (0,)), ((), ()))
    o_spec = pl.BlockSpec((tm, tn), lambda i, j, k: (i, j))
    has_res = residual is not None

    def body(*refs):
        if has_res:
            a_ref, b_ref, r_ref, o_ref, acc_ref = refs
        else:
            a_ref, b_ref, o_ref, acc_ref = refs
        k = pl.program_id(2)

        @pl.when(k == 0)
        def _():
            acc_ref[...] = jnp.zeros_like(acc_ref)

        acc_ref[...] += lax.dot_general(a_ref[...].astype(BF16), b_ref[...].astype(BF16), dims,
                                        preferred_element_type=F32)

        @pl.when(k == nk - 1)
        def _():
            if has_res:
                o_ref[...] = (acc_ref[...] + r_ref[...]).astype(out_dtype)
            else:
                o_ref[...] = acc_ref[...].astype(out_dtype)

    ins = [a, b] + ([residual] if has_res else [])
    in_specs = [a_spec, b_spec] + ([o_spec] if has_res else [])
    return pl.pallas_call(
        body, grid=(M // tm, N // tn, nk), in_specs=in_specs, out_specs=o_spec,
        out_shape=jax.ShapeDtypeStruct((M, N), out_dtype), scratch_shapes=[pltpu.VMEM((tm, tn), F32)],
        compiler_params=_cparams(("parallel", "parallel", "arbitrary")), name=name)(*ins)


class Cols:
    def __init__(self, arr, width, idx):
        self.arr, self.width, self.idx = arr, width, idx


def _arr(x):
    return x.arr if isinstance(x, Cols) else x


def _shape(x):
    return x.arr.shape[:-1] + (x.width,) if isinstance(x, Cols) else x.shape


def _col(x):
    return x.idx if isinstance(x, Cols) else 0


def _tok_spec(x, tile):
    shape, col = _shape(x), _col(x)
    return pl.BlockSpec(shape[:-2] + (tile, shape[-1]), lambda i: (0,) * (len(shape) - 2) + (i, col))


def _full_spec(x):
    shape, col = _shape(x), _col(x)
    return pl.BlockSpec(shape, lambda i: (0,) * (len(shape) - 1) + (col,))


def _halo_spec(x, tile):
    shape, col = _shape(x), _col(x)
    return pl.BlockSpec((8, shape[-1]), lambda i: (jnp.maximum(i * (tile // 8) - 1, 0), col))


def _blk(x, tile):
    shape = _shape(x)
    return jax.ShapeDtypeStruct(shape[:-2] + (tile, shape[-1]), _arr(x).dtype)


def _prev_rows(x, halo):
    rows = lax.broadcasted_iota(jnp.int32, (x.shape[0], 1), 0)
    before = jnp.where(pl.program_id(0) > 0, halo[7:8], 0.0)
    return jnp.where(rows == 0, before, pltpu.roll(x, 1, 0))


def stage_fwd(f, xs, ps, cts, cfs, tile, name, out_dtypes=None, with_prev=False):
    xs, ps, cts, cfs = list(xs), list(ps), list(cts), list(cfs)
    halos = xs if with_prev else []
    nx, nh, nct, np_ = len(xs), len(halos), len(cts), len(ps)
    T = _shape(xs[0])[-2]
    blk = [_blk(x, tile) for x in xs]
    out_avals = jax.eval_shape(f, *blk, *(blk if with_prev else []), *[_blk(p, _shape(p)[-2]) for p in ps],
                               *[_blk(c, tile) for c in cts], *[_blk(c, _shape(c)[-2]) for c in cfs])
    if out_dtypes is None:
        out_dtypes = [o.dtype for o in out_avals]
    out_shape = [jax.ShapeDtypeStruct(o.shape[:-2] + (T, o.shape[-1]), dt) for o, dt in zip(out_avals, out_dtypes)]
    n_in = nx + nh + nct + np_ + len(cfs)

    def body(*refs):
        vals = [r[...] for r in refs[:n_in]]
        xv, hv, rest = vals[:nx], vals[nx:nx + nh], vals[nx + nh:]
        ctv, pv, cfv = rest[:nct], rest[nct:nct + np_], rest[nct + np_:]
        prev = [_prev_rows(x, h) for x, h in zip(xv, hv)]
        res = f(*xv, *prev, *pv, *ctv, *cfv)
        for o_ref, r in zip(refs[n_in:], res):
            o_ref[...] = r.astype(o_ref.dtype)

    return pl.pallas_call(
        body, grid=(T // tile,),
        in_specs=([_tok_spec(x, tile) for x in xs] + [_halo_spec(x, tile) for x in halos] + [_tok_spec(c, tile) for c in cts]
                  + [_full_spec(p) for p in ps + cfs]),
        out_specs=[_tok_spec(o, tile) for o in out_shape], out_shape=out_shape,
        compiler_params=_cparams(("parallel",)), name=name)(*[_arr(a) for a in xs + halos + cts + ps + cfs])


def stage_bwd(f, xs, ps, cts, cfs, gs, tile, name, bf16_copies=(), with_prev=False):
    xs, ps, cts, cfs = list(xs), list(ps), list(cts), list(cfs)
    gs = [list(g) if isinstance(g, (list, tuple)) else [g] for g in gs]
    g_flat = [a for g in gs for a in g]
    halos = xs if with_prev else []
    nx, nh, nct, ng, np_ = len(xs), len(halos), len(cts), len(g_flat), len(ps)
    T = _shape(xs[0])[-2]
    dx_like = xs + halos
    out_shape = ([jax.ShapeDtypeStruct(_shape(x), F32) for x in dx_like] + [jax.ShapeDtypeStruct(_shape(p), F32) for p in ps]
                 + [jax.ShapeDtypeStruct(_shape(xs[i]), BF16) for i in bf16_copies])
    n_in = nx + nh + nct + ng + np_ + len(cfs)
    ndx = nx + nh

    def body(*refs):
        vals = [r[...] for r in refs[:n_in]]
        outs = refs[n_in:]
        xv, hv, rest = vals[:nx], vals[nx:nx + nh], vals[nx + nh:]
        ctv, gparts, pv, cfv = rest[:nct], rest[nct:nct + ng], rest[nct + ng:nct + ng + np_], rest[nct + ng + np_:]
        gv = []
        for g in gs:
            gv.append(functools.reduce(lambda a, b: a + b, gparts[:len(g)]))
            gparts = gparts[len(g):]
        prev = [_prev_rows(x, h) for x, h in zip(xv, hv)]
        _, vjp = jax.vjp(lambda *xp: f(*xp, *ctv, *cfv), *xv, *prev, *pv)
        d = vjp(tuple(gv))
        for o_ref, r in zip(outs[:ndx], d[:ndx]):
            o_ref[...] = r
        for o_ref, i in zip(outs[ndx + np_:], bf16_copies):
            o_ref[...] = d[i].astype(BF16)

        @pl.when(pl.program_id(0) == 0)
        def _():
            for o_ref in outs[ndx:ndx + np_]:
                o_ref[...] = jnp.zeros_like(o_ref)

        for o_ref, r in zip(outs[ndx:ndx + np_], d[ndx:]):
            o_ref[...] += r

    plain = lambda x: jax.ShapeDtypeStruct(_shape(x), F32)
    res = pl.pallas_call(
        body, grid=(T // tile,),
        in_specs=([_tok_spec(x, tile) for x in xs] + [_halo_spec(x, tile) for x in halos]
                  + [_tok_spec(c, tile) for c in cts + g_flat] + [_full_spec(p) for p in ps + cfs]),
        out_specs=([_tok_spec(plain(x), tile) for x in dx_like] + [_full_spec(plain(p)) for p in ps]
                   + [_tok_spec(plain(xs[i]), tile) for i in bf16_copies]), out_shape=out_shape,
        compiler_params=_cparams(("arbitrary",)), name=name)(*[_arr(a) for a in xs + halos + cts + g_flat + ps + cfs])
    if bf16_copies:
        return list(res[:ndx]), list(res[ndx:ndx + np_]), list(res[ndx + np_:])
    return list(res[:ndx]), list(res[ndx:])


def _rms(x, g, eps=RMS_EPS):
    return x * lax.rsqrt(jnp.mean(x * x, axis=-1, keepdims=True) + eps) * g


def f_rmsnorm(x, g):
    return (_rms(x, g),)


def f_rmsnorm_res(x, g):
    return _rms(x, g), x


def f_rmsnorm2(x, g1, g2):
    n = x * lax.rsqrt(jnp.mean(x * x, axis=-1, keepdims=True) + RMS_EPS)
    return n * g1, n * g2


def f_rmsnorm2_res(x, g1, g2):
    return f_rmsnorm2(x, g1, g2) + (x,)


def _sigmoid(x):
    return 1.0 / (1.0 + jnp.exp(-x))


def _softplus(x):
    return jnp.maximum(x, 0.0) + jnp.log(1.0 + jnp.exp(-jnp.abs(x)))


def f_rwkv_pre(pr, pk, pv, pl_, qr, qk, qv, ql, mu_r, mu_k, mu_v, mu_l, w0, w2, a0, a2, g2, k_k, k_a, seg, seg_t):
    xr = pr + (qr - pr) * mu_r
    xk = pk + (qk - pk) * mu_k
    xv = pv + (qv - pv) * mu_v
    xl = pl_ + (ql - pl_) * mu_l
    w_log = -_softplus(-(w0 + mm_nn(jnp.tanh(xl), w2))) - 0.5
    lw = -jnp.exp(w_log)
    a = _sigmoid(a0 + mm_nn(xl, a2))
    g = mm_nn(_sigmoid(xl), g2)
    kkr = xk * k_k
    inv = lax.rsqrt(jnp.maximum(mmh(kkr * kkr, seg), 1e-24))
    kk = kkr * mmh(inv, seg_t)
    k2 = xk * (1.0 + (a - 1.0) * k_a)
    return xr, lw, k2, xv, kk, kk * a, g


def f_rwkv_post(y, r, k2, v, g, lnx_w, lnx_b, r_k, seg, seg_t):
    inv_n = 1.0 / HEAD_DIM
    m = mmh(mmh(y, seg) * inv_n, seg_t)
    yc = y - m
    rstd = lax.rsqrt(mmh(yc * yc, seg) * inv_n + LNX_EPS)
    yn = yc * mmh(rstd, seg_t) * lnx_w + lnx_b
    bonus = mmh(mmh(r * k2 * r_k, seg), seg_t) * v
    return ((yn + bonus) * g,)


def _headnorm(z, g, seg, seg_t):
    ms = mmh(z * z, seg) * (1.0 / HEAD_DIM)
    return z * mmh(lax.rsqrt(ms + RMS_EPS), seg_t) * g


def f_headnorm(z, g, seg, seg_t):
    return (_headnorm(z, g, seg, seg_t),)


def _rot_half(z):
    w = z.shape[1]
    half = HEAD_DIM // 2
    lane = lax.broadcasted_iota(jnp.int32, (1, w), 1)
    return jnp.where((lane & (HEAD_DIM - 1)) < half, -pltpu.roll(z, w - half, 1), pltpu.roll(z, half, 1))


@jax.custom_vjp
def _rotate_half(z):
    return _rot_half(z)


_rotate_half.defvjp(lambda z: (_rot_half(z), None), lambda _, g: (-_rot_half(g),))


def f_qkprep(z, g, cos, sin, seg, seg_t):
    zn = _headnorm(z, g, seg, seg_t)
    pairs = z.shape[1] // cos.shape[1]
    return (zn * jnp.tile(cos, (1, pairs)) + _rotate_half(zn) * jnp.tile(sin, (1, pairs)),)


def _head_mask(width, h):
    lane = lax.broadcasted_iota(jnp.int32, (1, width), 1)
    return jnp.where((lane >> 6) == h, jnp.ones((), F32), 0.0)


def f_memattn(q, k, v, q_norm, seg, seg_t):
    qn = _headnorm(q, q_norm, seg, seg_t)
    out = jnp.zeros_like(q)
    for h in range(MEM_HEADS):
        m = _head_mask(MEM_WIDTH, h)
        s = mm_nt(qn * m, k) * (1.0 / math.sqrt(HEAD_DIM))
        s = s - jnp.max(s, axis=-1, keepdims=True)
        p = jnp.exp(s)
        p = p / jnp.sum(p, axis=-1, keepdims=True)
        out = out + mm_nn(p, v) * m
    return (out,)


def f_mix(o1, o2, o3, l1, l2, l3):
    mx = jnp.maximum(jnp.maximum(l1, l2), l3)
    e1, e2, e3 = jnp.exp(l1 - mx), jnp.exp(l2 - mx), jnp.exp(l3 - mx)
    return ((e1 * o1 + e2 * o2 + e3 * o3) / (e1 + e2 + e3),)


def _chunk_masks(L):
    t = lax.broadcasted_iota(jnp.int32, (L, L), 0)
    s = lax.broadcasted_iota(jnp.int32, (L, L), 1)
    return t, s


def _unit_lower_inverse(a):
    L = a.shape[-1]
    t, s = _chunk_masks(L)
    one = jnp.ones((), F32)
    blk = lambda sh: jnp.where((t >> sh) == (s >> sh), one, 0.0)
    n0 = a * blk(3)
    x = jnp.where(t == s, one, 0.0) - n0
    n2 = mmh(n0, n0)
    x = x + mmh(x, n2)
    x = x + mmh(x, mmh(n2, n2))
    for sh in (3, 4, 5):
        if (1 << sh) >= L:
            break
        off = a * (blk(sh + 1) - blk(sh))
        x = x - mmh(x, mmh(off, x))
    return x


@jax.custom_vjp
def _inverse_known(a, x):
    return x


def _inverse_known_fwd(a, x):
    return x, x


def _inverse_known_bwd(x, dx):
    return -mmh_nt(mmh_tn(x, dx), x), jnp.zeros_like(x)


_inverse_known.defvjp(_inverse_known_fwd, _inverse_known_bwd)


def _running_sum(x, reverse):
    L = x.shape[1]
    pos = lax.broadcasted_iota(jnp.int32, (1, L, 1), 1)
    step = 1
    while step < L:
        if reverse:
            x = x + jnp.where(pos < L - step, pltpu.roll(x, L - step, 1), 0.0)
        else:
            x = x + jnp.where(pos >= step, pltpu.roll(x, step, 1), 0.0)
        step *= 2
    return x


@jax.custom_vjp
def _cumsum_tokens(x):
    return _running_sum(x, False)


_cumsum_tokens.defvjp(lambda x: (_running_sum(x, False), None), lambda _, g: (_running_sum(g, True),))


def f_rwkv_chunk(s0, r, lw, k, v, kk, b, x_known=None):
    H, L, _ = r.shape
    t, s = _chunk_masks(L)
    one = jnp.ones((), F32)
    incl = jnp.where(t >= s, one, 0.0)
    strict = jnp.where(t > s, one, 0.0)
    cum = _cumsum_tokens(lw)
    w_in = jnp.exp(cum)
    w_ex = jnp.exp(cum - lw)
    w_inv = jnp.exp(-cum)
    rt, kkt, kt, bt = r * w_in, kk * w_ex, k * w_inv, b * w_inv
    a_b = mmh_nt(kkt, bt) * strict
    a_k = mmh_nt(kkt, kt) * strict
    m_k = mmh_nt(rt, kt) * incl
    m_b = mmh_nt(rt, bt) * incl
    x = _unit_lower_inverse(a_b) if x_known is None else _inverse_known(a_b, x_known)
    u = mmh(x, mmh_nt(kkt, s0) + mmh(a_k, v))
    y = mmh_nt(rt, s0) + mmh(m_k, v) - mmh(m_b, u)
    w_last = jnp.exp(jnp.sum(lw, axis=1, keepdims=True))
    s1 = (s0 + mmh_tn(v, kt) - mmh_tn(u, bt)) * w_last
    return y, s1, x


def _ex_split(ex, refs, n_in, n_out):
    n = ex.n
    ins, ex_in = refs[:n_in], refs[n_in:n_in + n]
    outs, ex_out = refs[n_in + n:n_in + n + n_out], refs[n_in + n + n_out:n_in + 2 * n + n_out]
    rest = refs[n_in + 2 * n + n_out:]
    return ins, outs, rest[:len(rest) - 3], (ex_in, ex_out) + tuple(rest[len(rest) - 3:])


def _split_heads(x):
    return jnp.stack([x[:, h * HEAD_DIM:(h + 1) * HEAD_DIM] for h in range(x.shape[1] // HEAD_DIM)], axis=0)


def _merge_heads(x):
    return jnp.concatenate([x[h] for h in range(x.shape[0])], axis=1)


def rwkv_scan_fwd(r, lw, k, v, kk, b, ex):
    T, N = r.shape[0], HEAD_DIM
    H = r.shape[1] // N
    groups = SCAN_GROUPS_FWD
    nc, hg = T // CHUNK, H // groups
    seq = pl.BlockSpec((CHUNK, hg * N), lambda g, c: (c, g))

    def body(*refs):
        (r_ref, lw_ref, k_ref, v_ref, kk_ref, b_ref), (y_ref, hs_ref, xs_ref), (h_scr,), ex_refs = _ex_split(ex, refs, 6, 3)
        g, c = pl.program_id(0), pl.program_id(1)

        @pl.when(jnp.logical_and(g == 0, c == 0))
        def _():
            ex.start(*ex_refs)

        @pl.when(c == 0)
        def _():
            h_scr[...] = jnp.zeros_like(h_scr)

        h0 = h_scr[...]
        hs_ref[0] = h0
        y, h1, x = f_rwkv_chunk(h0, *[_split_heads(z[...]) for z in (r_ref, lw_ref, k_ref, v_ref, kk_ref, b_ref)])
        y_ref[...] = _merge_heads(y)
        xs_ref[0] = x
        h_scr[...] = h1

        @pl.when(jnp.logical_and(g == groups - 1, c == (3 * nc) // 4))
        def _():
            ex.forward(*ex_refs)

        @pl.when(jnp.logical_and(g == groups - 1, c == nc - 1))
        def _():
            ex.wait(*ex_refs)

    res = pl.pallas_call(
        body, grid=(groups, nc), in_specs=[seq] * 6 + [_ANY] * ex.n,
        out_specs=[seq, pl.BlockSpec((1, hg, N, N), lambda g, c: (c, g, 0, 0)),
                   pl.BlockSpec((1, hg, CHUNK, CHUNK), lambda g, c: (c, g, 0, 0))] + [_ANY] * ex.n,
        out_shape=[jax.ShapeDtypeStruct((T, H * N), F32), jax.ShapeDtypeStruct((nc, H, N, N), F32),
                   jax.ShapeDtypeStruct((nc, H, CHUNK, CHUNK), F32)] + ex.out_shape(),
        scratch_shapes=[pltpu.VMEM((hg, N, N), F32)] + ex.scratch(),
        compiler_params=_cparams(("arbitrary", "arbitrary")), name="rwkv_scan_fwd")(r, lw, k, v, kk, b, *ex.operands())
    return res[0], (res[1], res[2]), list(res[3:])


def rwkv_scan_bwd(r, lw, k, v, kk, b, saved, dy, ex):
    T, N = r.shape[0], HEAD_DIM
    H = r.shape[1] // N
    groups = SCAN_GROUPS_BWD
    nc, hg = T // CHUNK, H // groups
    seq = pl.BlockSpec((CHUNK, hg * N), lambda g, c: (nc - 1 - c, g))
    state = pl.BlockSpec((1, hg, N, N), lambda g, c: (nc - 1 - c, g, 0, 0))

    def body(*refs):
        (r_ref, lw_ref, k_ref, v_ref, kk_ref, b_ref, hs_ref, xs_ref, dy_ref), outs, (dh_scr,), ex_refs = _ex_split(ex, refs, 9, 6)
        g, c = pl.program_id(0), pl.program_id(1)

        @pl.when(jnp.logical_and(g == 0, c == 0))
        def _():
            ex.start(*ex_refs)

        @pl.when(c == 0)
        def _():
            dh_scr[...] = jnp.zeros_like(dh_scr)

        x_known = xs_ref[0]
        _, vjp = jax.vjp(lambda *a: f_rwkv_chunk(*a, x_known=x_known)[:2], hs_ref[0],
                         *[_split_heads(z[...]) for z in (r_ref, lw_ref, k_ref, v_ref, kk_ref, b_ref)])
        d = vjp((_split_heads(dy_ref[...]), dh_scr[...]))
        dh_scr[...] = d[0]
        for o_ref, dz in zip(outs, d[1:]):
            o_ref[...] = _merge_heads(dz)

        @pl.when(jnp.logical_and(g == groups - 1, c == nc - 1))
        def _():
            ex.forward(*ex_refs)
            ex.wait(*ex_refs)

    res = pl.pallas_call(
        body, grid=(groups, nc),
        in_specs=[seq] * 6 + [state, state, seq] + [_ANY] * ex.n,
        out_specs=[seq] * 6 + [_ANY] * ex.n, out_shape=[jax.ShapeDtypeStruct((T, H * N), F32)] * 6 + ex.out_shape(),
        scratch_shapes=[pltpu.VMEM((hg, N, N), F32)] + ex.scratch(),
        compiler_params=_cparams(("arbitrary", "arbitrary")), name="rwkv_scan_bwd")(r, lw, k, v, kk, b, *saved, dy, *ex.operands())
    return list(res[:6]), list(res[6:])


GROUP_COLS = 4 * HEAD_DIM


def _f_dilattn(has_prev, q, kc, kp, vc, vp):
    scale = 1.0 / math.sqrt(HEAD_DIM)
    i = lax.broadcasted_iota(jnp.int32, (DIL_BLOCK, DIL_BLOCK), 0)
    j = lax.broadcasted_iota(jnp.int32, (DIL_BLOCK, DIL_BLOCK), 1)
    o, l = jnp.zeros_like(q), jnp.zeros_like(q)
    for h in range(q.shape[1] // HEAD_DIM):
        m = _head_mask(q.shape[1], h)
        sc = jnp.where(j <= i, mm_nt(q * m, kc) * scale, NEG_INF)
        sp = jnp.where(jnp.logical_and(i <= j, has_prev), mm_nt(q * m, kp) * scale, NEG_INF)
        mx = jnp.maximum(jnp.max(sc, axis=-1, keepdims=True), jnp.max(sp, axis=-1, keepdims=True))
        pc, pp = jnp.exp(sc - mx), jnp.exp(sp - mx)
        den = jnp.sum(pc, axis=-1, keepdims=True) + jnp.sum(pp, axis=-1, keepdims=True)
        o = o + (mm_nn(pc, vc) + mm_nn(pp, vp)) / den * m
        l = l + (mx + jnp.log(den)) * m
    return o, l


def _dil_specs(gi, d):
    parts = 1 if d == 1 else 2
    blk = (DIL_BLOCK * d, GROUP_COLS // parts)
    at = lambda col: (lambda p, n: (n, col * parts + p))
    before = lambda col: (lambda p, n: (jnp.maximum(n - 1, 0), col * parts + p))
    v0 = DIL_WIDTH // GROUP_COLS + gi
    q = pl.BlockSpec(blk, at(gi))
    kc, kp = pl.BlockSpec(blk, at(gi)), pl.BlockSpec(blk, before(gi))
    vc, vp = pl.BlockSpec(blk, at(v0)), pl.BlockSpec(blk, before(v0))
    out = pl.BlockSpec(blk, at(0))
    together = min(d, 2)
    return (q, kc, kp, vc, vp, out), parts, together


def _residue_rows(r, d):
    return pl.ds(r, DIL_BLOCK, stride=d) if d > 1 else pl.ds(0, DIL_BLOCK)


def dil_fwd(q, k, kv, gi, d, name):
    T = q.shape[0]
    (qs, kc, kp, vc, vp, out), parts, together = _dil_specs(gi, d)

    def body(q_ref, kc_ref, kp_ref, vc_ref, vp_ref, o_ref, l_ref):
        has_prev = pl.program_id(1) > 0

        def residues(it, carry):
            rows = [_residue_rows(it * together + a, d) for a in range(together)]
            ins = [[ref[rw, :] for ref in (q_ref, kc_ref, kp_ref, vc_ref, vp_ref)] for rw in rows]
            res = [_f_dilattn(has_prev, *x) for x in ins]
            for rw, (o, l) in zip(rows, res):
                o_ref[rw, :] = o
                l_ref[rw, :] = l
            return carry

        lax.fori_loop(0, d // together, residues, 0)

    shape = jax.ShapeDtypeStruct((T, 4 * HEAD_DIM), F32)
    return pl.pallas_call(
        body, grid=(parts, T // (DIL_BLOCK * d)), in_specs=[qs, kc, kp, vc, vp], out_specs=[out, out], out_shape=[shape, shape],
        compiler_params=_cparams(("parallel", "parallel")), name=name)(q, k, k, kv, kv)


def dil_bwd(q, k, kv, do, dl, gi, d, name):
    T = q.shape[0]
    (qs, kc, kp, vc, vp, out), parts, together = _dil_specs(gi, d)

    def body(q_ref, kc_ref, kp_ref, vc_ref, vp_ref, do_ref, dl_ref, *outs):
        f = functools.partial(_f_dilattn, pl.program_id(1) > 0)

        def residues(it, carry):
            rows = [_residue_rows(it * together + a, d) for a in range(together)]
            ins = [[ref[rw, :] for ref in (q_ref, kc_ref, kp_ref, vc_ref, vp_ref, do_ref, dl_ref)] for rw in rows]
            res = [jax.vjp(f, *x[:5])[1]((x[5], x[6])) for x in ins]
            for rw, gs in zip(rows, res):
                for o_ref, g in zip(outs, gs):
                    o_ref[rw, :] = g
            return carry

        lax.fori_loop(0, d // together, residues, 0)

    shape = jax.ShapeDtypeStruct((T, 4 * HEAD_DIM), F32)
    dq, dkc, dkp, dvc, dvp = pl.pallas_call(
        body, grid=(parts, T // (DIL_BLOCK * d)), in_specs=[qs, kc, kp, vc, vp, out, out], out_specs=[out] * 5, out_shape=[shape] * 5,
        compiler_params=_cparams(("parallel", "parallel")), name=name)(q, k, k, kv, kv, do, dl)

    def own_plus_next(c, p):
        return c + jnp.concatenate([p[DIL_BLOCK * d:], jnp.zeros_like(p[:DIL_BLOCK * d])], axis=0)

    return dq, own_plus_next(dkc, dkp), own_plus_next(dvc, dvp)


CONV_TILE = 256


def _conv3(before, u, w, b):
    ue = jnp.concatenate([before, u], axis=0)
    s1, s2 = pltpu.roll(ue, 1, 0)[8:], pltpu.roll(ue, 2, 0)[8:]
    return b + w[0:1] * s2 + w[1:2] * s1 + w[2:3] * u, s1, s2


def _conv_halves(u_ref, h_ref, cw_ref, cb_ref):
    F = D_FF
    res = []
    for lo in (0, F):
        before = jnp.where(pl.program_id(0) > 0, h_ref[:, lo:lo + F], 0.0)
        u = u_ref[:, lo:lo + F]
        res.append((u,) + _conv3(before, u, cw_ref[:, lo:lo + F], cb_ref[:, lo:lo + F]))
    return res


def _halo_before(C):
    return pl.BlockSpec((8, C), lambda i: (jnp.maximum(i * (CONV_TILE // 8) - 1, 0), 0))


def convgate_fwd(u, cw, cb, name):
    T, C = u.shape
    F = C // 2

    def body(u_ref, h_ref, cw_ref, cb_ref, z_ref):
        (_, cg, _, _), (_, cv, _, _) = _conv_halves(u_ref, h_ref, cw_ref, cb_ref)
        z_ref[...] = (cg * _sigmoid(cg) * cv).astype(BF16)

    return pl.pallas_call(
        body, grid=(T // CONV_TILE,),
        in_specs=[pl.BlockSpec((CONV_TILE, C), lambda i: (i, 0)), _halo_before(C), _full_spec(cw), _full_spec(cb)],
        out_specs=pl.BlockSpec((CONV_TILE, F), lambda i: (i, 0)), out_shape=jax.ShapeDtypeStruct((T, F), BF16),
        compiler_params=_cparams(("parallel",)), name=name)(u, u, cw, cb)


def convgate_bwd(u, cw, cb, dz, name):
    T, C = u.shape
    F = C // 2
    n = T // CONV_TILE
    E = CONV_TILE + 8

    def body(u_ref, hb_ref, ha_ref, cw_ref, cb_ref, dz_ref, dza_ref, du_ref, dcw_ref, dcb_ref):
        i = pl.program_id(0)
        dze = jnp.concatenate([dz_ref[...], jnp.where(i < n - 1, dza_ref[...], 0.0)], axis=0)

        @pl.when(i == 0)
        def _():
            dcw_ref[...] = jnp.zeros_like(dcw_ref)
            dcb_ref[...] = jnp.zeros_like(dcb_ref)

        halves = []
        for lo in (0, F):
            sl = slice(lo, lo + F)
            ue = jnp.concatenate([u_ref[:, sl], ha_ref[:, sl]], axis=0)
            c, s1, s2 = _conv3(jnp.where(i > 0, hb_ref[:, sl], 0.0), ue, cw_ref[:, sl], cb_ref[:, sl])
            halves.append((sl, ue, c, s1, s2))
        (_, _, cg, _, _), (_, _, cv, _, _) = halves
        sg = _sigmoid(cg)
        dcs = (dze * cv * sg * (1.0 + cg * (1.0 - sg)), dze * cg * sg)
        for (sl, ue, _, s1, s2), dc in zip(halves, dcs):
            own = lambda z: z[:CONV_TILE]
            dcb_ref[:, sl] += jnp.sum(own(dc), axis=0, keepdims=True)
            dcw_ref[0:1, sl] += jnp.sum(own(dc * s2), axis=0, keepdims=True)
            dcw_ref[1:2, sl] += jnp.sum(own(dc * s1), axis=0, keepdims=True)
            dcw_ref[2:3, sl] += jnp.sum(own(dc * ue), axis=0, keepdims=True)
            du = cw_ref[2:3, sl] * dc + cw_ref[1:2, sl] * pltpu.roll(dc, E - 1, 0) + cw_ref[0:1, sl] * pltpu.roll(dc, E - 2, 0)
            du_ref[:, sl] = own(du).astype(BF16)

    after = lambda w: pl.BlockSpec((8, w), lambda i: (jnp.minimum((i + 1) * (CONV_TILE // 8), T // 8 - 1), 0))
    return pl.pallas_call(
        body, grid=(n,),
        in_specs=[pl.BlockSpec((CONV_TILE, C), lambda i: (i, 0)), _halo_before(C), after(C), _full_spec(cw), _full_spec(cb),
                  pl.BlockSpec((CONV_TILE, F), lambda i: (i, 0)), after(F)],
        out_specs=[pl.BlockSpec((CONV_TILE, C), lambda i: (i, 0)), _full_spec(cw), _full_spec(cb)],
        out_shape=[jax.ShapeDtypeStruct((T, C), BF16), jax.ShapeDtypeStruct(cw.shape, F32), jax.ShapeDtypeStruct(cb.shape, F32)],
        compiler_params=_cparams(("arbitrary",)), name=name)(u, u, u, cw, cb, dz, dz)


def loss_head(y, tgt):
    T, D = y.shape
    tile = ROW_TILE

    def body(y_ref, t_ref, l_ref, d_ref, db_ref):
        d = y_ref[...] - t_ref[...]
        d_ref[...] = d * (1.0 / D)
        db_ref[...] = (d * (1.0 / D)).astype(BF16)

        @pl.when(pl.program_id(0) == 0)
        def _():
            l_ref[...] = jnp.zeros_like(l_ref)

        l_ref[...] += (0.5 / D) * jnp.sum(d * d)

    row = pl.BlockSpec((tile, D), lambda i: (i, 0))
    return pl.pallas_call(
        body, grid=(T // tile,), in_specs=[row, row], out_specs=[pl.BlockSpec((8, 128), lambda i: (0, 0)), row, row],
        out_shape=[jax.ShapeDtypeStruct((8, 128), F32), jax.ShapeDtypeStruct((T, D), F32), jax.ShapeDtypeStruct((T, D), BF16)],
        compiler_params=_cparams(("arbitrary",)), name="loss_head")(y, tgt)


def sum_parts(parts, name):
    S, R, C = parts.shape
    tile = _pick(R, (256, 128, 64, 32, 16, 8))

    def body(p_ref, o_ref):
        acc = p_ref[0]
        for s in range(1, S):
            acc = acc + p_ref[s]
        o_ref[...] = acc

    return pl.pallas_call(
        body, grid=(R // tile,), in_specs=[pl.BlockSpec((S, tile, C), lambda i: (0, i, 0))],
        out_specs=pl.BlockSpec((tile, C), lambda i: (i, 0)), out_shape=jax.ShapeDtypeStruct((R, C), F32),
        compiler_params=_cparams(("parallel",)), name=name)(parts)


def adamw(gparts, w, m, v, name):
    S, R, C = gparts.shape
    tile = _pick(R, (256, 128, 64, 32, 16, 8))
    c1 = 1.0 / (1.0 - ADAM_B1 ** ADAM_STEP)
    c2 = 1.0 / (1.0 - ADAM_B2 ** ADAM_STEP)

    def body(g_ref, w_ref, m_ref, v_ref, go_ref, d_ref, mo_ref, vo_ref):
        g = g_ref[0].astype(F32)
        for s in range(1, S):
            g = g + g_ref[s].astype(F32)
        m1 = ADAM_B1 * m_ref[...] + (1.0 - ADAM_B1) * g
        v1 = ADAM_B2 * v_ref[...] + (1.0 - ADAM_B2) * (g * g)
        go_ref[...] = g
        mo_ref[...] = m1
        vo_ref[...] = v1
        d_ref[...] = -ADAM_LR * ((m1 * c1) / (jnp.sqrt(v1 * c2) + ADAM_EPS) + ADAM_WD * w_ref[...])

    row = pl.BlockSpec((tile, C), lambda i: (i, 0))
    return pl.pallas_call(
        body, grid=(R // tile,), in_specs=[pl.BlockSpec((S, tile, C), lambda i: (0, i, 0)), row, row, row],
        out_specs=[row] * 4, out_shape=[jax.ShapeDtypeStruct((R, C), F32)] * 4,
        compiler_params=_cparams(("parallel",)), name=name)(gparts, w, m, v)


def _peers():
    x, y, c = lax.axis_index("x"), lax.axis_index("y"), lax.axis_index("c")
    peers = []
    for k in range(1, N_DEV):
        px = 1 - x if k & 4 else x
        py = 1 - y if k & 2 else y
        pc = 1 - c if k & 1 else c
        peers.append(((px, py, pc), 4 * px + 2 * py + pc))
    return 4 * x + 2 * y + c, peers


_ANY = pl.BlockSpec(memory_space=pl.ANY)


class Exchange:
    def __init__(self, gathers=(), scatters=()):
        self.gathers, self.scatters = list(gathers), list(scatters)
        self.n = len(self.gathers) + len(self.scatters)

    def operands(self):
        return self.gathers + self.scatters

    def out_shape(self):
        return ([jax.ShapeDtypeStruct((N_DEV,) + x.shape, x.dtype) for x in self.gathers]
                + [jax.ShapeDtypeStruct(x.shape, x.dtype) for x in self.scatters])

    def scratch(self):
        n = max(self.n, 1)
        return [pltpu.SemaphoreType.DMA((7 * n,)), pltpu.SemaphoreType.DMA((7 * n,)), pltpu.SemaphoreType.DMA((n,))]

    def _copies(self, in_refs, out_refs, send_sems, recv_sems, local_sems):
        me, peers = _peers()
        ng = len(self.gathers)
        local, sends, recvs = [], [], []
        for a in range(self.n):
            x, o = in_refs[a], out_refs[a]
            mine = x if a < ng else x.at[me]
            local.append(pltpu.make_async_copy(mine, o.at[me], local_sems.at[a]))
            s_a, r_a = {}, {}
            for k in range(1, N_DEV):
                peer, slot = peers[k - 1]
                sems = dict(send_sem=send_sems.at[7 * a + k - 1], recv_sem=recv_sems.at[7 * a + k - 1],
                            device_id_type=pl.DeviceIdType.MESH)
                if a >= ng:
                    s_a[k] = pltpu.make_async_remote_copy(src_ref=x.at[slot], dst_ref=o.at[me], device_id=peer, **sems)
                elif k in FORWARDED:
                    came = o.at[peers[k - 2][1]]
                    s_a[k] = pltpu.make_async_remote_copy(src_ref=came, dst_ref=came, device_id=peers[0][0], **sems)
                else:
                    s_a[k] = pltpu.make_async_remote_copy(src_ref=x, dst_ref=o.at[me], device_id=peer, **sems)
                r_a[k] = pltpu.make_async_remote_copy(src_ref=mine, dst_ref=o.at[slot], device_id=peer, **sems)
            sends.append(s_a)
            recvs.append(r_a)
        return local, sends, recvs

    def start(self, *refs):
        if self.n == 0:
            return
        local, sends, _ = self._copies(*refs)
        for a in range(self.n):
            local[a].start()
            for k in range(1, N_DEV):
                if a >= len(self.gathers) or k not in FORWARDED:
                    sends[a][k].start()

    def forward(self, *refs):
        if not self.gathers:
            return
        _, sends, recvs = self._copies(*refs)
        for a in range(len(self.gathers)):
            for k in FORWARDED:
                recvs[a][k - 1].wait_recv()
                sends[a][k].start()

    def wait(self, *refs):
        if self.n == 0:
            return
        local, sends, recvs = self._copies(*refs)
        for a in range(self.n):
            waited_early = [f - 1 for f in FORWARDED] if a < len(self.gathers) else []
            for k in range(1, N_DEV):
                if k not in waited_early:
                    recvs[a][k].wait_recv()
            for k in range(1, N_DEV):
                sends[a][k].wait_send()
            local[a].wait()


FORWARDED = (3, 5, 7)


def exchange(ex, name):
    n = ex.n

    def body(*refs):
        args = (refs[:n], refs[n:2 * n]) + tuple(refs[2 * n:])
        ex.start(*args)
        ex.forward(*args)
        ex.wait(*args)

    return pl.pallas_call(body, in_specs=[_ANY] * n, out_specs=[_ANY] * n, out_shape=ex.out_shape(),
                          scratch_shapes=ex.scratch(), name=name)(*ex.operands())


def _shift_up(z):
    return jnp.concatenate([z[1:], jnp.zeros_like(z[:1])], axis=0)


def _segments(width):
    seg = np.zeros((width, 128), np.float32)
    seg[np.arange(width), np.arange(width) // HEAD_DIM] = 1.0
    return jnp.asarray(seg), jnp.asarray(seg.T)


def _rope_consts(T):
    inv = ROPE_THETA ** (-jnp.arange(0, HEAD_DIM, 2, dtype=F32) / HEAD_DIM)
    ang = jnp.arange(T, dtype=F32)[:, None] * inv[None, :]
    return jnp.tile(jnp.cos(ang), (1, 4)), jnp.tile(jnp.sin(ang), (1, 4))


def _per_head(g, heads):
    return jnp.tile(g.reshape(1, HEAD_DIM), (1, heads))


def _sum_heads(g):
    return g.reshape(-1, HEAD_DIM).sum(axis=0, keepdims=True)


LORA_COLS = 256
RW_TILE = 256
ROW_TILE = 512


def _local_step(x0, memx, tgt, P, ex_weights=None, weights_done=None, ex_grads=None):
    T = x0.shape[0]
    P = dict(P)
    G = {}
    seg, seg_t = _segments(RWKV_WIDTH)
    mseg = (seg[:MEM_WIDTH], seg_t[:, :MEM_WIDTH])
    cos, sin = _rope_consts(T)
    row = lambda v: v.reshape(1, -1)

    def mem_fwd(i, q):
        memn = stage_fwd(f_rmsnorm, [memx], [P["mem_norm"][i:i + 1]], [], [], N_MEM, f"mem{i}_norm", [BF16])[0]
        kvm = matmul(memn, P["mem_w_kv"][i], "nn", f"mem{i}_kv")
        kn, qn = _per_head(P["mem_k_norm"][i], MEM_HEADS), _per_head(P["mem_q_norm"][i], MEM_HEADS)
        km = stage_fwd(f_headnorm, [Cols(kvm, MEM_WIDTH, 0)], [kn], [], mseg, N_MEM, f"mem{i}_knorm")[0]
        om = stage_fwd(f_memattn, [q], [km, Cols(kvm, MEM_WIDTH, 1), qn], [], mseg, ROW_TILE, f"mem{i}_attn")[0]
        return om, (memn, kvm, km, kn, qn, q)

    def mem_bwd(i, saved, dymem):
        memn, kvm, km, kn, qn, q = saved
        (dq,), (dkm, dvm, g_qn) = stage_bwd(f_memattn, [q], [km, Cols(kvm, MEM_WIDTH, 1), qn], [], mseg, [dymem], ROW_TILE,
                                            f"mem{i}_attn_bwd")
        (dkraw,), (g_kn,) = stage_bwd(f_headnorm, [Cols(kvm, MEM_WIDTH, 0)], [kn], [], mseg, [dkm], N_MEM, f"mem{i}_knorm_bwd")
        dkvm = jnp.concatenate([dkraw, dvm], axis=1).astype(BF16)
        g_w = matmul(memn, dkvm, "tn", f"mem{i}_kv_dw")
        dmemn = matmul(dkvm, P["mem_w_kv"][i], "nt", f"mem{i}_kv_dx")
        _, (g_mn,) = stage_bwd(f_rmsnorm, [memx], [P["mem_norm"][i:i + 1]], [], [], [dmemn], N_MEM, f"mem{i}_norm_bwd")
        return dq, g_mn, g_w, _sum_heads(g_qn), _sum_heads(g_kn)

    def ffn_fwd(i, xin):
        hn = stage_fwd(f_rmsnorm, [xin], [P["ffn_norm"][i:i + 1]], [], [], ROW_TILE, f"ffn{i}_norm", [BF16])[0]
        u = matmul(hn, P["ffn_w_up"][i], "nn", f"ffn{i}_up")
        z = convgate_fwd(u, P["ffn_conv_w"][i], P["ffn_conv_b"][i:i + 1], f"ffn{i}_conv")
        return matmul(z, P["ffn_w_down"][i], "nn", f"ffn{i}_down", residual=xin), (hn, u, z)

    def ffn_bwd(i, xin, saved, dxo, dxo_b):
        hn, u, z = saved
        dz = matmul(dxo_b, P["ffn_w_down"][i], "nt", f"ffn{i}_down_dx")
        g_down = matmul(z, dxo_b, "tn", f"ffn{i}_down_dw")
        du, g_cw, g_cb = convgate_bwd(u, P["ffn_conv_w"][i], P["ffn_conv_b"][i:i + 1], dz, f"ffn{i}_conv_bwd")
        dhn = matmul(du, P["ffn_w_up"][i], "nt", f"ffn{i}_up_dx")
        g_up = matmul(hn, du, "tn", f"ffn{i}_up_dw")
        (dxin,), (g_n,), (dxin_b,) = stage_bwd(f_rmsnorm_res, [xin], [P["ffn_norm"][i:i + 1]], [], [], [dhn, dxo], ROW_TILE,
                                               f"ffn{i}_norm_bwd", bf16_copies=(0,))
        return dxin, dxin_b, g_n, g_up, g_cw, g_cb, g_down

    h0 = stage_fwd(f_rmsnorm, [x0], [P["attn_norm"][0:1]], [], [], ROW_TILE, "l0_norm", [BF16])[0]
    p0 = matmul(h0, P["a_w_in"][0], "nn", "l0_in")
    lora0 = 3 * RWKV_WIDTH // LORA_COLS
    pre_xs = [Cols(p0, RWKV_WIDTH, 0), Cols(p0, RWKV_WIDTH, 1), Cols(p0, RWKV_WIDTH, 2), Cols(p0, LORA_COLS, lora0)]
    mu = [Cols(P["a_mu"], RWKV_WIDTH, 0), Cols(P["a_mu"], RWKV_WIDTH, 1), Cols(P["a_mu"], RWKV_WIDTH, 2),
          Cols(P["a_mu"], LORA_COLS, lora0)]
    lora_rows = lambda w, lo: jnp.pad(w, ((lo, LORA_COLS - lo - w.shape[0]), (0, 0)))
    pre_ps = mu + [P["a_w0"], lora_rows(P["a_w2"][0], 0), P["a_a0"], lora_rows(P["a_a2"][0], 64), lora_rows(P["a_g2"][0], 128),
                   P["a_k_k"], P["a_k_a"]]
    r, lw, k2, v, kk, b, g = stage_fwd(f_rwkv_pre, pre_xs, pre_ps, [], [seg, seg_t], RW_TILE, "l0_rwkv_pre", with_prev=True)
    scan_in = [r, lw, k2, v, kk, b]
    y_h, h_states, got = rwkv_scan_fwd(*scan_in, ex_weights or Exchange())
    if weights_done is not None:
        P.update(weights_done(got))
    y_s = y_h
    post_ps = [P["a_lnx_w"], P["a_lnx_b"], P["a_r_k"].reshape(1, RWKV_WIDTH)]
    ymix0 = stage_fwd(f_rwkv_post, [y_s, r, k2, v, g], post_ps, [], [seg, seg_t], RW_TILE, "l0_rwkv_post")[0]
    ymem0, mem0_saved = mem_fwd(0, Cols(p0, MEM_WIDTH, SHIFT_WIDTH // MEM_WIDTH))
    ycat0 = jnp.concatenate([ymix0, ymem0], axis=1).astype(BF16)
    x1 = matmul(ycat0, P["a_w_out"][0], "nn", "l0_out", residual=x0)
    x2, ffn0_saved = ffn_fwd(0, x1)

    hk, h1 = stage_fwd(f_rmsnorm2, [x2], [row(P["kv_norm"]), P["attn_norm"][1:2]], [], [], ROW_TILE, "l1_norm", [BF16, BF16])
    kvp = matmul(hk, P["kv_w"][0], "nn", "l1_kv")
    p1 = matmul(h1, P["b_w_in"][0], "nn", "l1_in")
    kraw, qraw = Cols(kvp, DIL_WIDTH, 0), Cols(p1, DIL_WIDTH, 0)
    kgain, qgain = _per_head(P["kv_k_norm"], DIL_WIDTH // HEAD_DIM), _per_head(P["b_q_norm"], DIL_WIDTH // HEAD_DIM)
    ksh = stage_fwd(f_qkprep, [kraw], [kgain], [cos, sin], [seg, seg_t], ROW_TILE, "l1_kprep")[0]
    q = stage_fwd(f_qkprep, [qraw], [qgain], [cos, sin], [seg, seg_t], ROW_TILE, "l1_qprep")[0]
    outs, lses = [], []
    for gi, (_, d) in enumerate(DIL_GROUPS):
        og, lg = dil_fwd(q, ksh, kvp, gi, d, f"l1_dil{gi}")
        outs.append(og)
        lses.append(lg)
    omix = stage_fwd(f_mix, outs + lses, [], [], [], ROW_TILE, "l1_mix")[0]
    ymem1, mem1_saved = mem_fwd(1, Cols(p1, MEM_WIDTH, DIL_WIDTH // MEM_WIDTH))
    ycat1 = jnp.concatenate([omix, ymem1], axis=1).astype(BF16)
    x3 = matmul(ycat1, P["b_w_out"][0], "nn", "l1_out", residual=x2)
    x4, ffn1_saved = ffn_fwd(1, x3)
    loss_part, dx4, dx4_b = loss_head(x4, tgt)

    dx3, dx3_b, gn1, gup1, gcw1, gcb1, gdown1 = ffn_bwd(1, x3, ffn1_saved, dx4, dx4_b)
    dycat1 = matmul(dx3_b, P["b_w_out"][0], "nt", "l1_out_dx")
    G["b_w_out"] = [matmul(ycat1, dx3_b, "tn", "l1_out_dw")]
    dqmem1, gmn1, gmw1, gmq1, gmk1 = mem_bwd(1, mem1_saved, Cols(dycat1, MEM_WIDTH, 1))
    dmix, _ = stage_bwd(f_mix, outs + lses, [], [], [], [Cols(dycat1, MEM_WIDTH, 0)], ROW_TILE, "l1_mix_bwd")
    dq, dk, dv = zip(*[dil_bwd(q, ksh, kvp, dmix[gi], dmix[3 + gi], gi, d, f"l1_dil{gi}_bwd")
                       for gi, (_, d) in enumerate(DIL_GROUPS)])
    dq, dk, dv = jnp.concatenate(dq, axis=1), jnp.concatenate(dk, axis=1), jnp.concatenate(dv, axis=1)
    (dqraw,), (g_bq,) = stage_bwd(f_qkprep, [qraw], [qgain], [cos, sin], [seg, seg_t], [dq], ROW_TILE, "l1_qprep_bwd")
    (dkraw,), (g_kk,) = stage_bwd(f_qkprep, [kraw], [kgain], [cos, sin], [seg, seg_t], [dk], ROW_TILE, "l1_kprep_bwd")
    g_bq, g_kk = _sum_heads(g_bq), _sum_heads(g_kk)
    dp1 = jnp.concatenate([dqraw, dqmem1], axis=1).astype(BF16)
    dkvp = jnp.concatenate([dkraw, dv], axis=1).astype(BF16)
    dh1 = matmul(dp1, P["b_w_in"][0], "nt", "l1_in_dx")
    G["b_w_in"] = [matmul(h1, dp1, "tn", "l1_in_dw")]
    dhk = matmul(dkvp, P["kv_w"][0], "nt", "l1_kv_dx")
    G["kv_w"] = [matmul(hk, dkvp, "tn", "l1_kv_dw")]
    (dx2,), (g_kvn, g_an1), (dx2_b,) = stage_bwd(f_rmsnorm2_res, [x2], [row(P["kv_norm"]), P["attn_norm"][1:2]], [], [],
                                                 [dhk, dh1, dx3], ROW_TILE, "l1_norm_bwd", bf16_copies=(0,))

    dx1, dx1_b, gn0, gup0, gcw0, gcb0, gdown0 = ffn_bwd(0, x1, ffn0_saved, dx2, dx2_b)
    dycat0 = matmul(dx1_b, P["a_w_out"][0], "nt", "l0_out_dx")
    G["a_w_out"] = [matmul(ycat0, dx1_b, "tn", "l0_out_dw")]
    dqmem0, gmn0, gmw0, gmq0, gmk0 = mem_bwd(0, mem0_saved, Cols(dycat0, MEM_WIDTH, RWKV_WIDTH // MEM_WIDTH))
    (dy_s, dr_a, dk_a, dv_a, dg), (g_lw, g_lb, g_rk) = stage_bwd(
        f_rwkv_post, [y_s, r, k2, v, g], post_ps, [], [seg, seg_t], [Cols(dycat0, RWKV_WIDTH, 0)], RW_TILE, "l0_rwkv_post_bwd")
    G["mem_w_kv"], G["ffn_w_up"], G["ffn_w_down"] = [gmw0, gmw1], [gup0, gup1], [gdown0, gdown1]
    (dr_b, dlw, dk_b, dv_b, dkk, db), G["_exchanged"] = rwkv_scan_bwd(*scan_in, h_states, dy_s,
                                                                      ex_grads(G) if ex_grads else Exchange())
    dpre, gpre = stage_bwd(f_rwkv_pre, pre_xs, pre_ps, [], [seg, seg_t],
                           [[dr_a, dr_b], dlw, [dk_a, dk_b], [dv_a, dv_b], dkk, db, dg], RW_TILE, "l0_rwkv_pre_bwd", with_prev=True)
    dp_rw = jnp.concatenate(dpre[:4], axis=1) + _shift_up(jnp.concatenate(dpre[4:], axis=1))
    dp0 = jnp.concatenate([dp_rw, dqmem0], axis=1).astype(BF16)
    dh0 = matmul(dp0, P["a_w_in"][0], "nt", "l0_in_dx")
    G["a_w_in"] = [matmul(h0, dp0, "tn", "l0_in_dw")]
    (dx0,), (g_an0,) = stage_bwd(f_rmsnorm_res, [x0], [P["attn_norm"][0:1]], [], [], [dh0, dx1], ROW_TILE, "l0_norm_bwd")

    G["attn_norm"] = jnp.concatenate([g_an0, g_an1], axis=0)
    G["a_mu"] = jnp.concatenate(gpre[:4], axis=1)
    G["a_w0"], G["a_w2"], G["a_a0"], G["a_a2"], G["a_g2"] = gpre[4], gpre[5][None, :64], gpre[6], gpre[7][None, 64:128], gpre[8][None, 128:]
    G["a_k_k"], G["a_k_a"] = gpre[9], gpre[10]
    G["a_r_k"] = g_rk.reshape(1, RWKV_HEADS, HEAD_DIM)
    G["a_lnx_w"], G["a_lnx_b"] = g_lw, g_lb
    G["kv_norm"], G["kv_k_norm"], G["b_q_norm"] = g_kvn.reshape(-1), g_kk.reshape(-1), g_bq
    G["mem_norm"] = jnp.concatenate([gmn0, gmn1], axis=0)
    G["mem_w_kv"] = [gmw0, gmw1]
    G["mem_q_norm"] = jnp.concatenate([gmq0, gmq1], axis=0)
    G["mem_k_norm"] = jnp.concatenate([gmk0, gmk1], axis=0)
    G["ffn_norm"] = jnp.concatenate([gn0, gn1], axis=0)
    G["ffn_w_up"] = [gup0, gup1]
    G["ffn_conv_w"] = jnp.stack([gcw0, gcw1])
    G["ffn_conv_b"] = jnp.concatenate([gcb0, gcb1], axis=0)
    G["ffn_w_down"] = [gdown0, gdown1]
    return loss_part, dx0, G


PARAMS = (("attn_norm", None), ("a_w_in", 2), ("a_mu", 1), ("a_w0", 1), ("a_w2", 2), ("a_a0", 1), ("a_a2", 2), ("a_g2", 2),
          ("a_k_k", 1), ("a_k_a", 1), ("a_r_k", None), ("a_lnx_w", 1), ("a_lnx_b", 1), ("a_w_out", 1), ("kv_norm", None),
          ("kv_w", 1), ("kv_k_norm", None), ("b_w_in", 1), ("b_q_norm", None), ("b_w_out", 2), ("mem_norm", None),
          ("mem_w_kv", 1), ("mem_q_norm", None), ("mem_k_norm", None), ("ffn_norm", None), ("ffn_w_up", 2),
          ("ffn_conv_w", 2), ("ffn_conv_b", None), ("ffn_w_down", 1))
BIG = ("a_w_in", "a_w_out", "kv_w", "b_w_in", "b_w_out", "mem_w_kv", "ffn_w_up", "ffn_w_down")
AXIS = dict(PARAMS)
SMALL = tuple(n for n, _ in PARAMS if n not in BIG)
SMALL_SHARDED = tuple(n for n in SMALL if AXIS[n] is not None)
PACK_QUANTUM = 256 * 128


def _from_shards(xs, axis):
    full = jnp.moveaxis(xs, 0, axis)
    sh = full.shape
    return full.reshape(sh[:axis] + (sh[axis] * sh[axis + 1],) + sh[axis + 2:])


def _to_shards(g, axis):
    sh = g.shape
    return jnp.moveaxis(g.reshape(sh[:axis] + (N_DEV, sh[axis] // N_DEV) + sh[axis + 1:]), axis, 0)


def _pack(parts, lead=0):
    ld = parts[0].shape[:lead]
    flat = jnp.concatenate([p.reshape(ld + (-1,)) for p in parts], axis=-1)
    pad = (-flat.shape[-1]) % PACK_QUANTUM
    flat = jnp.pad(flat, [(0, 0)] * lead + [(0, pad)])
    return flat.reshape(ld + (-1, 128))


def _unpack(packed, shapes, lead=0):
    ld = packed.shape[:lead]
    flat = packed.reshape(ld + (-1,))
    out, off = [], 0
    for s in shapes:
        n = math.prod(s)
        out.append(flat[..., off:off + n].reshape(ld + tuple(s)))
        off += n
    return out


def kernel(x, mem, attn_norm, a_w_in, a_mu, a_w0, a_w2, a_a0, a_a2, a_g2, a_k_k, a_k_a, a_r_k, a_lnx_w, a_lnx_b, a_w_out, kv_norm, kv_w, kv_k_norm, b_w_in, b_q_norm, b_w_out, mem_norm, mem_w_kv, mem_q_norm, mem_k_norm, ffn_norm, ffn_w_up, ffn_conv_w, ffn_conv_b, ffn_w_down, loss_target, m_attn_norm, m_a_w_in, m_a_mu, m_a_w0, m_a_w2, m_a_a0, m_a_a2, m_a_g2, m_a_k_k, m_a_k_a, m_a_r_k, m_a_lnx_w, m_a_lnx_b, m_a_w_out, m_kv_norm, m_kv_w, m_kv_k_norm, m_b_w_in, m_b_q_norm, m_b_w_out, m_mem_norm, m_mem_w_kv, m_mem_q_norm, m_mem_k_norm, m_ffn_norm, m_ffn_w_up, m_ffn_conv_w, m_ffn_conv_b, m_ffn_w_down, v_attn_norm, v_a_w_in, v_a_mu, v_a_w0, v_a_w2, v_a_a0, v_a_a2, v_a_g2, v_a_k_k, v_a_k_a, v_a_r_k, v_a_lnx_w, v_a_lnx_b, v_a_w_out, v_kv_norm, v_kv_w, v_kv_k_norm, v_b_w_in, v_b_q_norm, v_b_w_out, v_mem_norm, v_mem_w_kv, v_mem_q_norm, v_mem_k_norm, v_ffn_norm, v_ffn_w_up, v_ffn_conv_w, v_ffn_conv_b, v_ffn_w_down):
    names = [n for n, _ in PARAMS]
    vals = (attn_norm, a_w_in, a_mu, a_w0, a_w2, a_a0, a_a2, a_g2, a_k_k, a_k_a, a_r_k, a_lnx_w, a_lnx_b, a_w_out, kv_norm, kv_w, kv_k_norm, b_w_in, b_q_norm, b_w_out, mem_norm, mem_w_kv, mem_q_norm, mem_k_norm, ffn_norm, ffn_w_up, ffn_conv_w, ffn_conv_b, ffn_w_down)
    m_vals = (m_attn_norm, m_a_w_in, m_a_mu, m_a_w0, m_a_w2, m_a_a0, m_a_a2, m_a_g2, m_a_k_k, m_a_k_a, m_a_r_k, m_a_lnx_w, m_a_lnx_b, m_a_w_out, m_kv_norm, m_kv_w, m_kv_k_norm, m_b_w_in, m_b_q_norm, m_b_w_out, m_mem_norm, m_mem_w_kv, m_mem_q_norm, m_mem_k_norm, m_ffn_norm, m_ffn_w_up, m_ffn_conv_w, m_ffn_conv_b, m_ffn_w_down)
    v_vals = (v_attn_norm, v_a_w_in, v_a_mu, v_a_w0, v_a_w2, v_a_a0, v_a_a2, v_a_g2, v_a_k_k, v_a_k_a, v_a_r_k, v_a_lnx_w, v_a_lnx_b, v_a_w_out, v_kv_norm, v_kv_w, v_kv_k_norm, v_b_w_in, v_b_q_norm, v_b_w_out, v_mem_norm, v_mem_w_kv, v_mem_q_norm, v_mem_k_norm, v_ffn_norm, v_ffn_w_up, v_ffn_conv_w, v_ffn_conv_b, v_ffn_w_down)
    W, M, V = dict(zip(names, vals)), dict(zip(names, m_vals)), dict(zip(names, v_vals))
    layers = lambda D, n: [D[n]] if D[n].ndim == 2 else [D[n][i] for i in range(D[n].shape[0])]
    ax2 = lambda n: AXIS[n] - (W[n].ndim - 2)
    later =[(n, i) for n in BIG if n != "a_w_in" for i in range(len(layers(W, n)))]

    small_shapes = [W[n].shape for n in SMALL_SHARDED]
    got_w, got_small = exchange(Exchange(gathers=[W["a_w_in"][0].astype(BF16), _pack([W[n] for n in SMALL_SHARDED])]),
                                "gather_first")
    P = {n: W[n] for n in SMALL}
    P["a_w_in"] = [_from_shards(got_w, ax2("a_w_in"))]
    for n, s in zip(SMALL_SHARDED, _unpack(got_small, small_shapes, lead=1)):
        P[n] = _from_shards(s, AXIS[n])
    ex_weights = Exchange(gathers=[layers(W, n)[i].astype(BF16) for n, i in later])

    def weights_done(got):
        out = {}
        for (n, _), g in zip(later, got):
            out.setdefault(n, []).append(_from_shards(g, ax2(n)))
        return out

    slots = lambda G, n: jnp.stack([_to_shards(g, ax2(n)) for g in G[n]], axis=1)
    later_names = [n for n in BIG if n != "a_w_in"]
    ex_grads = lambda G: Exchange(scatters=[slots(G, n) for n in later_names])
    loss_part, dx0, G = _local_step(x[0], mem[0], loss_target[0], P, ex_weights, weights_done, ex_grads)
    loss = lax.psum(loss_part[0, 0], ("x", "y", "c"))
    gparts = dict(zip(later_names, G.pop("_exchanged")))
    replicated = [n for n in SMALL if AXIS[n] is None]
    small_slots = _pack([_to_shards(G[n], AXIS[n]) for n in SMALL_SHARDED], lead=1)
    got_rep, gparts["a_w_in"], got_sharded = exchange(
        Exchange(gathers=[_pack([G[n] for n in replicated])], scatters=[slots(G, "a_w_in"), small_slots]), "exchange_last")

    results = {}
    for n in BIG:
        rows = lambda z: z.reshape((-1,) + z.shape[-1:])
        res = adamw(gparts[n].reshape((N_DEV, -1) + gparts[n].shape[-1:]), rows(W[n]), rows(M[n]), rows(V[n]), f"adamw_{n}")
        results[n] = [r.reshape(W[n].shape) for r in res]
    g_mine = dict(zip(replicated, _unpack(sum_parts(got_rep, "sum_replicated_grads"), [W[n].shape for n in replicated])))
    g_mine.update(zip(SMALL_SHARDED, _unpack(sum_parts(got_sharded, "sum_small_sharded_grads"), [W[n].shape for n in SMALL_SHARDED])))
    res = adamw(_pack([g_mine[n] for n in SMALL])[None], _pack([W[n] for n in SMALL]), _pack([M[n] for n in SMALL]),
                _pack([V[n] for n in SMALL]), "adamw_small")
    for n, parts in zip(SMALL, zip(*[_unpack(r, [W[n].shape for n in SMALL]) for r in res])):
        results[n] = list(parts)
    outs = [[results[n][j] for n in names] for j in range(4)]
    return (loss, dx0[None], *outs[0], *outs[1], *outs[2], *outs[3])
```

```python
import functools
import math

import jax
import jax.numpy as jnp
import numpy as np
from jax import lax
from jax.experimental import pallas as pl
from jax.experimental.pallas import tpu as pltpu

F32 = jnp.float32
BF16 = jnp.bfloat16
HI = lax.Precision.HIGHEST
H3 = lax.Precision.HIGH

N_DEV = 8
D_MODEL = 1024
HEAD_DIM = 64
N_MEM = 256
MEM_HEADS = 4
MEM_WIDTH = 256
RWKV_HEADS = 12
RWKV_WIDTH = 768
SHIFT_WIDTH = 2560
DIL_GROUPS = ((128, 1), (512, 4), (2048, 16))
DIL_BLOCK = 128
DIL_WIDTH = 768
D_FF = 2816
RMS_EPS = 1e-6
LNX_EPS = 64e-5
NEG_INF = -1e30
ROPE_THETA = 10000.0
ADAM_LR, ADAM_B1, ADAM_B2, ADAM_EPS, ADAM_WD, ADAM_STEP = 0.001, 0.9, 0.999, 1e-08, 0.01, 10

CHUNK = 64
SCAN_GROUPS_FWD, SCAN_GROUPS_BWD = 1, 1
MM_TILE_CAP = 1408
VMEM_LIMIT_V7X = 48 * 1024 * 1024


def _cparams(sem):
    return pltpu.CompilerParams(dimension_semantics=sem, vmem_limit_bytes=VMEM_LIMIT_V7X)


def _pick(n, cands):
    for c in cands:
        if n % c == 0:
            return c
    return n


def _tile(n, cap):
    if n <= cap:
        return n
    for d in range(cap - cap % 128, 0, -128):
        if n % d == 0:
            return d
    return n


def _dg(a, b, ca, cb, batch):
    dims = (((ca,), (cb,)), ((0,), (0,))) if batch else (((ca,), (cb,)), ((), ()))
    return lax.dot_general(a.astype(BF16), b.astype(BF16), dims, preferred_element_type=F32)


@jax.custom_vjp
def mm_nn(a, b):
    n = a.ndim
    return _dg(a, b, n - 1, n - 2, n == 3)


def _mm_nn_fwd(a, b):
    return mm_nn(a, b), (a, b)


def _mm_nn_bwd(res, g):
    a, b = res
    n = a.ndim
    return _dg(g, b, n - 1, n - 1, n == 3), _dg(a, g, n - 2, n - 2, n == 3)


mm_nn.defvjp(_mm_nn_fwd, _mm_nn_bwd)


@jax.custom_vjp
def mm_nt(a, b):
    n = a.ndim
    return _dg(a, b, n - 1, n - 1, n == 3)


def _mm_nt_fwd(a, b):
    return mm_nt(a, b), (a, b)


def _mm_nt_bwd(res, g):
    a, b = res
    n = a.ndim
    return _dg(g, b, n - 1, n - 2, n == 3), _dg(g, a, n - 2, n - 2, n == 3)


mm_nt.defvjp(_mm_nt_fwd, _mm_nt_bwd)


def mmh(a, b, precision=H3):
    n = a.ndim
    dims = (((n - 1,), (n - 2,)), ((0,), (0,))) if n == 3 else (((1,), (0,)), ((), ()))
    return lax.dot_general(a, b, dims, precision=precision, preferred_element_type=F32)


def mmh_nt(a, b):
    n = a.ndim
    dims = (((n - 1,), (n - 1,)), ((0,), (0,))) if n == 3 else (((1,), (1,)), ((), ()))
    return lax.dot_general(a, b, dims, precision=H3, preferred_element_type=F32)


def mmh_tn(a, b):
    n = a.ndim
    dims = (((n - 2,), (n - 2,)), ((0,), (0,))) if n == 3 else (((0,), (0,)), ((), ()))
    return lax.dot_general(a, b, dims, precision=H3, preferred_element_type=F32)


def matmul(a, b, mode, name, residual=None):
    out_dtype = BF16 if mode == "tn" else F32
    if mode == "nn":
        (M, K), (_, N) = a.shape, b.shape
    elif mode == "nt":
        (M, K), (N, _) = a.shape, b.shape
    else:
        (K, M), (_, N) = a.shape, b.shape
    tm = _tile(M, 2048 if mode == "nn" else MM_TILE_CAP)
    tn = _tile(N, 512 if mode == "nn" else MM_TILE_CAP)
    tk = _tile(K, MM_TILE_CAP)
    nk = K // tk
    if mode == "nn":
        a_spec = pl.BlockSpec((tm, tk), lambda i, j, k: (i, k))
        b_spec = pl.BlockSpec((tk, tn), lambda i, j, k: (k, j))
        dims = (((1,), (0,)), ((), ()))
    elif mode == "nt":
        a_spec = pl.BlockSpec((tm, tk), lambda i, j, k: (i, k))
        b_spec = pl.BlockSpec((tn, tk), lambda i, j, k: (j, k))
        dims = (((1,), (1,)), ((), ()))
    else:
        a_spec = pl.BlockSpec((tk, tm), lambda i, j, k: (k, i))
        b_spec = pl.BlockSpec((tk, tn), lambda i, j, k: (k, j))
        dims = (((0,), (0,)), ((), ()))
    o_spec = pl.BlockSpec((tm, tn), lambda i, j, k: (i, j))
    has_res = residual is not None

    def body(*refs):
        if has_res:
            a_ref, b_ref, r_ref, o_ref, acc_ref = refs
        else:
            a_ref, b_ref, o_ref, acc_ref = refs
        k = pl.program_id(2)

        @pl.when(k == 0)
        def _():
            acc_ref[...] = jnp.zeros_like(acc_ref)

        acc_ref[...] += lax.dot_general(a_ref[...].astype(BF16), b_ref[...].astype(BF16), dims,
                                        preferred_element_type=F32)

        @pl.when(k == nk - 1)
        def _():
            if has_res:
                o_ref[...] = (acc_ref[...] + r_ref[...]).astype(out_dtype)
            else:
                o_ref[...] = acc_ref[...].astype(out_dtype)

    ins = [a, b] + ([residual] if has_res else [])
    in_specs = [a_spec, b_spec] + ([o_spec] if has_res else [])
    return pl.pallas_call(
        body, grid=(M // tm, N // tn, nk), in_specs=in_specs, out_specs=o_spec,
        out_shape=jax.ShapeDtypeStruct((M, N), out_dtype), scratch_shapes=[pltpu.VMEM((tm, tn), F32)],
        compiler_params=_cparams(("parallel", "parallel", "arbitrary")), name=name)(*ins)


class Cols:
    def __init__(self, arr, width, idx):
        self.arr, self.width, self.idx = arr, width, idx


def _arr(x):
    return x.arr if isinstance(x, Cols) else x


def _shape(x):
    return x.arr.shape[:-1] + (x.width,) if isinstance(x, Cols) else x.shape


def _col(x):
    return x.idx if isinstance(x, Cols) else 0


def _tok_spec(x, tile):
    shape, col = _shape(x), _col(x)
    return pl.BlockSpec(shape[:-2] + (tile, shape[-1]), lambda i: (0,) * (len(shape) - 2) + (i, col))


def _full_spec(x):
    shape, col = _shape(x), _col(x)
    return pl.BlockSpec(shape, lambda i: (0,) * (len(shape) - 1) + (col,))


def _halo_spec(x, tile):
    shape, col = _shape(x), _col(x)
    return pl.BlockSpec((8, shape[-1]), lambda i: (jnp.maximum(i * (tile // 8) - 1, 0), col))


def _blk(x, tile):
    shape = _shape(x)
    return jax.ShapeDtypeStruct(shape[:-2] + (tile, shape[-1]), _arr(x).dtype)


def _prev_rows(x, halo):
    rows = lax.broadcasted_iota(jnp.int32, (x.shape[0], 1), 0)
    before = jnp.where(pl.program_id(0) > 0, halo[7:8], 0.0)
    return jnp.where(rows == 0, before, pltpu.roll(x, 1, 0))


class Into:
    def __init__(self, buf, total, idx):
        self.buf, self.total, self.idx = buf, total, idx

    def place(self, shape, tile):
        idx = self.idx
        return shape.update(shape=(shape.shape[0], self.total)), pl.BlockSpec((tile, shape.shape[1]), lambda i: (i, idx))

    def operand(self, n_in, n_out_index):
        if self.buf is None:
            return [], [], {}
        return [self.buf], [_ANY], {n_in: n_out_index}


def stage_fwd(f, xs, ps, cts, cfs, tile, name, out_dtypes=None, with_prev=False, into=None):
    xs, ps, cts, cfs = list(xs), list(ps), list(cts), list(cfs)
    halos = xs if with_prev else []
    nx, nh, nct, np_ = len(xs), len(halos), len(cts), len(ps)
    T = _shape(xs[0])[-2]
    blk = [_blk(x, tile) for x in xs]
    out_avals = jax.eval_shape(f, *blk, *(blk if with_prev else []), *[_blk(p, _shape(p)[-2]) for p in ps],
                               *[_blk(c, tile) for c in cts], *[_blk(c, _shape(c)[-2]) for c in cfs])
    if out_dtypes is None:
        out_dtypes = [o.dtype for o in out_avals]
    out_shape = [jax.ShapeDtypeStruct(o.shape[:-2] + (T, o.shape[-1]), dt) for o, dt in zip(out_avals, out_dtypes)]
    out_specs = [_tok_spec(o, tile) for o in out_shape]
    n_in = nx + nh + nct + np_ + len(cfs)
    extra, extra_specs, alias = [], [], {}
    if into is not None:
        out_shape[0], out_specs[0] = into.place(out_shape[0], tile)
        extra, extra_specs, alias = into.operand(n_in, 0)

    def body(*refs):
        vals = [r[...] for r in refs[:n_in]]
        xv, hv, rest = vals[:nx], vals[nx:nx + nh], vals[nx + nh:]
        ctv, pv, cfv = rest[:nct], rest[nct:nct + np_], rest[nct + np_:]
        prev = [_prev_rows(x, h) for x, h in zip(xv, hv)]
        res = f(*xv, *prev, *pv, *ctv, *cfv)
        for o_ref, r in zip(refs[n_in + len(extra):], res):
            o_ref[...] = r.astype(o_ref.dtype)

    return pl.pallas_call(
        body, grid=(T // tile,),
        in_specs=([_tok_spec(x, tile) for x in xs] + [_halo_spec(x, tile) for x in halos] + [_tok_spec(c, tile) for c in cts]
                  + [_full_spec(p) for p in ps + cfs] + extra_specs),
        out_specs=out_specs, out_shape=out_shape, input_output_aliases=alias,
        compiler_params=_cparams(("parallel",)), name=name)(*[_arr(a) for a in xs + halos + cts + ps + cfs], *extra)


def stage_bwd(f, xs, ps, cts, cfs, gs, tile, name, bf16_copies=(), with_prev=False):
    xs, ps, cts, cfs = list(xs), list(ps), list(cts), list(cfs)
    gs = [list(g) if isinstance(g, (list, tuple)) else [g] for g in gs]
    g_flat = [a for g in gs for a in g]
    halos = xs if with_prev else []
    nx, nh, nct, ng, np_ = len(xs), len(halos), len(cts), len(g_flat), len(ps)
    T = _shape(xs[0])[-2]
    dx_like = xs + halos
    out_shape = ([jax.ShapeDtypeStruct(_shape(x), F32) for x in dx_like] + [jax.ShapeDtypeStruct(_shape(p), F32) for p in ps]
                 + [jax.ShapeDtypeStruct(_shape(xs[i]), BF16) for i in bf16_copies])
    n_in = nx + nh + nct + ng + np_ + len(cfs)
    ndx = nx + nh

    def body(*refs):
        vals = [r[...] for r in refs[:n_in]]
        outs = refs[n_in:]
        xv, hv, rest = vals[:nx], vals[nx:nx + nh], vals[nx + nh:]
        ctv, gparts, pv, cfv = rest[:nct], rest[nct:nct + ng], rest[nct + ng:nct + ng + np_], rest[nct + ng + np_:]
        gv = []
        for g in gs:
            gv.append(functools.reduce(lambda a, b: a + b, gparts[:len(g)]))
            gparts = gparts[len(g):]
        prev = [_prev_rows(x, h) for x, h in zip(xv, hv)]
        _, vjp = jax.vjp(lambda *xp: f(*xp, *ctv, *cfv), *xv, *prev, *pv)
        d = vjp(tuple(gv))
        for o_ref, r in zip(outs[:ndx], d[:ndx]):
            o_ref[...] = r
        for o_ref, i in zip(outs[ndx + np_:], bf16_copies):
            o_ref[...] = d[i].astype(BF16)

        @pl.when(pl.program_id(0) == 0)
        def _():
            for o_ref in outs[ndx:ndx + np_]:
                o_ref[...] = jnp.zeros_like(o_ref)

        for o_ref, r in zip(outs[ndx:ndx + np_], d[ndx:]):
            o_ref[...] += r

    plain = lambda x: jax.ShapeDtypeStruct(_shape(x), F32)
    res = pl.pallas_call(
        body, grid=(T // tile,),
        in_specs=([_tok_spec(x, tile) for x in xs] + [_halo_spec(x, tile) for x in halos]
                  + [_tok_spec(c, tile) for c in cts + g_flat] + [_full_spec(p) for p in ps + cfs]),
        out_specs=([_tok_spec(plain(x), tile) for x in dx_like] + [_full_spec(plain(p)) for p in ps]
                   + [_tok_spec(plain(xs[i]), tile) for i in bf16_copies]), out_shape=out_shape,
        compiler_params=_cparams(("arbitrary",)), name=name)(*[_arr(a) for a in xs + halos + cts + g_flat + ps + cfs])
    if bf16_copies:
        return list(res[:ndx]), list(res[ndx:ndx + np_]), list(res[ndx + np_:])
    return list(res[:ndx]), list(res[ndx:])


def _rms(x, g, eps=RMS_EPS):
    return x * lax.rsqrt(jnp.mean(x * x, axis=-1, keepdims=True) + eps) * g


def f_rmsnorm(x, g):
    return (_rms(x, g),)


def f_rmsnorm_res(x, g):
    return _rms(x, g), x


def f_rmsnorm2(x, g1, g2):
    n = x * lax.rsqrt(jnp.mean(x * x, axis=-1, keepdims=True) + RMS_EPS)
    return n * g1, n * g2


def f_rmsnorm2_res(x, g1, g2):
    return f_rmsnorm2(x, g1, g2) + (x,)


def _sigmoid(x):
    return 1.0 / (1.0 + jnp.exp(-x))


def _softplus(x):
    return jnp.maximum(x, 0.0) + jnp.log(1.0 + jnp.exp(-jnp.abs(x)))


def f_rwkv_pre(pr, pk, pv, pl_, qr, qk, qv, ql, mu_r, mu_k, mu_v, mu_l, w0, w2, a0, a2, g2, k_k, k_a, seg, seg_t):
    xr = pr + (qr - pr) * mu_r
    xk = pk + (qk - pk) * mu_k
    xv = pv + (qv - pv) * mu_v
    xl = pl_ + (ql - pl_) * mu_l
    w_log = -_softplus(-(w0 + mm_nn(jnp.tanh(xl), w2))) - 0.5
    lw = -jnp.exp(w_log)
    a = _sigmoid(a0 + mm_nn(xl, a2))
    g = mm_nn(_sigmoid(xl), g2)
    kkr = xk * k_k
    inv = lax.rsqrt(jnp.maximum(mmh(kkr * kkr, seg), 1e-24))
    kk = kkr * mmh(inv, seg_t)
    k2 = xk * (1.0 + (a - 1.0) * k_a)
    return xr, lw, k2, xv, kk, kk * a, g


def f_rwkv_post(y, r, k2, v, g, lnx_w, lnx_b, r_k, seg, seg_t):
    inv_n = 1.0 / HEAD_DIM
    m = mmh(mmh(y, seg) * inv_n, seg_t)
    yc = y - m
    rstd = lax.rsqrt(mmh(yc * yc, seg) * inv_n + LNX_EPS)
    yn = yc * mmh(rstd, seg_t) * lnx_w + lnx_b
    bonus = mmh(mmh(r * k2 * r_k, seg), seg_t) * v
    return ((yn + bonus) * g,)


def _headnorm(z, g, seg, seg_t):
    ms = mmh(z * z, seg) * (1.0 / HEAD_DIM)
    return z * mmh(lax.rsqrt(ms + RMS_EPS), seg_t) * g


def f_headnorm(z, g, seg, seg_t):
    return (_headnorm(z, g, seg, seg_t),)


def _rot_half(z):
    w = z.shape[1]
    half = HEAD_DIM // 2
    lane = lax.broadcasted_iota(jnp.int32, (1, w), 1)
    return jnp.where((lane & (HEAD_DIM - 1)) < half, -pltpu.roll(z, w - half, 1), pltpu.roll(z, half, 1))


@jax.custom_vjp
def _rotate_half(z):
    return _rot_half(z)


_rotate_half.defvjp(lambda z: (_rot_half(z), None), lambda _, g: (-_rot_half(g),))


def f_qkprep(z, g, cos, sin, seg, seg_t):
    zn = _headnorm(z, g, seg, seg_t)
    pairs = z.shape[1] // cos.shape[1]
    return (zn * jnp.tile(cos, (1, pairs)) + _rotate_half(zn) * jnp.tile(sin, (1, pairs)),)


def _head_mask(width, h):
    lane = lax.broadcasted_iota(jnp.int32, (1, width), 1)
    return jnp.where((lane >> 6) == h, jnp.ones((), F32), 0.0)


def f_memattn(q, k, v, q_norm, seg, seg_t):
    qn = _headnorm(q, q_norm, seg, seg_t)
    out = jnp.zeros_like(q)
    for h in range(MEM_HEADS):
        m = _head_mask(MEM_WIDTH, h)
        s = mm_nt(qn * m, k) * (1.0 / math.sqrt(HEAD_DIM))
        s = s - jnp.max(s, axis=-1, keepdims=True)
        p = jnp.exp(s)
        p = p / jnp.sum(p, axis=-1, keepdims=True)
        out = out + mm_nn(p, v) * m
    return (out,)


def f_mix(o1, o2, o3, l1, l2, l3):
    mx = jnp.maximum(jnp.maximum(l1, l2), l3)
    e1, e2, e3 = jnp.exp(l1 - mx), jnp.exp(l2 - mx), jnp.exp(l3 - mx)
    return ((e1 * o1 + e2 * o2 + e3 * o3) / (e1 + e2 + e3),)


def _chunk_masks(L):
    t = lax.broadcasted_iota(jnp.int32, (L, L), 0)
    s = lax.broadcasted_iota(jnp.int32, (L, L), 1)
    return t, s


def _unit_lower_inverse(a):
    L = a.shape[-1]
    t, s = _chunk_masks(L)
    one = jnp.ones((), F32)
    blk = lambda sh: jnp.where((t >> sh) == (s >> sh), one, 0.0)
    n0 = a * blk(3)
    x = jnp.where(t == s, one, 0.0) - n0
    n2 = mmh(n0, n0)
    x = x + mmh(x, n2)
    x = x + mmh(x, mmh(n2, n2))
    for sh in (3, 4, 5):
        if (1 << sh) >= L:
            break
        off = a * (blk(sh + 1) - blk(sh))
        x = x - mmh(x, mmh(off, x))
    return x


@jax.custom_vjp
def _inverse_known(a, x):
    return x


def _inverse_known_fwd(a, x):
    return x, x


def _inverse_known_bwd(x, dx):
    return -mmh_nt(mmh_tn(x, dx), x), jnp.zeros_like(x)


_inverse_known.defvjp(_inverse_known_fwd, _inverse_known_bwd)


def _running_sum(x, reverse):
    L = x.shape[1]
    pos = lax.broadcasted_iota(jnp.int32, (1, L, 1), 1)
    step = 1
    while step < L:
        if reverse:
            x = x + jnp.where(pos < L - step, pltpu.roll(x, L - step, 1), 0.0)
        else:
            x = x + jnp.where(pos >= step, pltpu.roll(x, step, 1), 0.0)
        step *= 2
    return x


@jax.custom_vjp
def _cumsum_tokens(x):
    return _running_sum(x, False)


_cumsum_tokens.defvjp(lambda x: (_running_sum(x, False), None), lambda _, g: (_running_sum(g, True),))


def f_rwkv_chunk(s0, r, lw, k, v, kk, b, x_known=None):
    H, L, _ = r.shape
    t, s = _chunk_masks(L)
    one = jnp.ones((), F32)
    incl = jnp.where(t >= s, one, 0.0)
    strict = jnp.where(t > s, one, 0.0)
    cum = _cumsum_tokens(lw)
    w_in = jnp.exp(cum)
    w_ex = jnp.exp(cum - lw)
    w_inv = jnp.exp(-cum)
    rt, kkt, kt, bt = r * w_in, kk * w_ex, k * w_inv, b * w_inv
    a_b = mmh_nt(kkt, bt) * strict
    a_k = mmh_nt(kkt, kt) * strict
    m_k = mmh_nt(rt, kt) * incl
    m_b = mmh_nt(rt, bt) * incl
    x = _unit_lower_inverse(a_b) if x_known is None else _inverse_known(a_b, x_known)
    u = mmh(x, mmh_nt(kkt, s0) + mmh(a_k, v))
    y = mmh_nt(rt, s0) + mmh(m_k, v) - mmh(m_b, u)
    w_last = jnp.exp(jnp.sum(lw, axis=1, keepdims=True))
    s1 = (s0 + mmh_tn(v, kt) - mmh_tn(u, bt)) * w_last
    return y, s1, x


def _ex_split(ex, refs, n_in, n_out):
    n = ex.n
    ins, ex_in = refs[:n_in], refs[n_in:n_in + n]
    outs, ex_out = refs[n_in + n:n_in + n + n_out], refs[n_in + n + n_out:n_in + 2 * n + n_out]
    rest = refs[n_in + 2 * n + n_out:]
    return ins, outs, rest[:len(rest) - 3], (ex_in, ex_out) + tuple(rest[len(rest) - 3:])


def _split_heads(x):
    return jnp.stack([x[:, h * HEAD_DIM:(h + 1) * HEAD_DIM] for h in range(x.shape[1] // HEAD_DIM)], axis=0)


def _merge_heads(x):
    return jnp.concatenate([x[h] for h in range(x.shape[0])], axis=1)


def rwkv_scan_fwd(r, lw, k, v, kk, b, ex):
    T, N = r.shape[0], HEAD_DIM
    H = r.shape[1] // N
    groups = SCAN_GROUPS_FWD
    nc, hg = T // CHUNK, H // groups
    seq = pl.BlockSpec((CHUNK, hg * N), lambda g, c: (c, g))

    def body(*refs):
        (r_ref, lw_ref, k_ref, v_ref, kk_ref, b_ref), (y_ref, hs_ref, xs_ref), (h_scr,), ex_refs = _ex_split(ex, refs, 6, 3)
        g, c = pl.program_id(0), pl.program_id(1)

        @pl.when(jnp.logical_and(g == 0, c == 0))
        def _():
            ex.start(*ex_refs)

        @pl.when(c == 0)
        def _():
            h_scr[...] = jnp.zeros_like(h_scr)

        h0 = h_scr[...]
        hs_ref[0] = h0
        y, h1, x = f_rwkv_chunk(h0, *[_split_heads(z[...]) for z in (r_ref, lw_ref, k_ref, v_ref, kk_ref, b_ref)])
        y_ref[...] = _merge_heads(y)
        xs_ref[0] = x
        h_scr[...] = h1

        @pl.when(jnp.logical_and(g == groups - 1, c == (3 * nc) // 4))
        def _():
            ex.forward(*ex_refs)

        @pl.when(jnp.logical_and(g == groups - 1, c == nc - 1))
        def _():
            ex.wait(*ex_refs)

    res = pl.pallas_call(
        body, grid=(groups, nc), in_specs=[seq] * 6 + [_ANY] * ex.n,
        out_specs=[seq, pl.BlockSpec((1, hg, N, N), lambda g, c: (c, g, 0, 0)),
                   pl.BlockSpec((1, hg, CHUNK, CHUNK), lambda g, c: (c, g, 0, 0))] + [_ANY] * ex.n,
        out_shape=[jax.ShapeDtypeStruct((T, H * N), F32), jax.ShapeDtypeStruct((nc, H, N, N), F32),
                   jax.ShapeDtypeStruct((nc, H, CHUNK, CHUNK), F32)] + ex.out_shape(),
        scratch_shapes=[pltpu.VMEM((hg, N, N), F32)] + ex.scratch(),
        compiler_params=_cparams(("arbitrary", "arbitrary")), name="rwkv_scan_fwd")(r, lw, k, v, kk, b, *ex.operands())
    return res[0], (res[1], res[2]), list(res[3:])


def rwkv_scan_bwd(r, lw, k, v, kk, b, saved, dy, ex):
    T, N = r.shape[0], HEAD_DIM
    H = r.shape[1] // N
    groups = SCAN_GROUPS_BWD
    nc, hg = T // CHUNK, H // groups
    seq = pl.BlockSpec((CHUNK, hg * N), lambda g, c: (nc - 1 - c, g))
    state = pl.BlockSpec((1, hg, N, N), lambda g, c: (nc - 1 - c, g, 0, 0))

    def body(*refs):
        (r_ref, lw_ref, k_ref, v_ref, kk_ref, b_ref, hs_ref, xs_ref, dy_ref), outs, (dh_scr,), ex_refs = _ex_split(ex, refs, 9, 6)
        g, c = pl.program_id(0), pl.program_id(1)

        @pl.when(jnp.logical_and(g == 0, c == 0))
        def _():
            ex.start(*ex_refs)

        @pl.when(c == 0)
        def _():
            dh_scr[...] = jnp.zeros_like(dh_scr)

        x_known = xs_ref[0]
        _, vjp = jax.vjp(lambda *a: f_rwkv_chunk(*a, x_known=x_known)[:2], hs_ref[0],
                         *[_split_heads(z[...]) for z in (r_ref, lw_ref, k_ref, v_ref, kk_ref, b_ref)])
        d = vjp((_split_heads(dy_ref[...]), dh_scr[...]))
        dh_scr[...] = d[0]
        for o_ref, dz in zip(outs, d[1:]):
            o_ref[...] = _merge_heads(dz)

        @pl.when(jnp.logical_and(g == groups - 1, c == nc - 1))
        def _():
            ex.forward(*ex_refs)
            ex.wait(*ex_refs)

    res = pl.pallas_call(
        body, grid=(groups, nc),
        in_specs=[seq] * 6 + [state, state, seq] + [_ANY] * ex.n,
        out_specs=[seq] * 6 + [_ANY] * ex.n, out_shape=[jax.ShapeDtypeStruct((T, H * N), F32)] * 6 + ex.out_shape(),
        scratch_shapes=[pltpu.VMEM((hg, N, N), F32)] + ex.scratch(),
        compiler_params=_cparams(("arbitrary", "arbitrary")), name="rwkv_scan_bwd")(r, lw, k, v, kk, b, *saved, dy, *ex.operands())
    return list(res[:6]), list(res[6:])


GROUP_COLS = 4 * HEAD_DIM


def _f_dilattn(has_prev, q, kc, kp, vc, vp):
    scale = 1.0 / math.sqrt(HEAD_DIM)
    i = lax.broadcasted_iota(jnp.int32, (DIL_BLOCK, DIL_BLOCK), 0)
    j = lax.broadcasted_iota(jnp.int32, (DIL_BLOCK, DIL_BLOCK), 1)
    o, l = jnp.zeros_like(q), jnp.zeros_like(q)
    for h in range(q.shape[1] // HEAD_DIM):
        m = _head_mask(q.shape[1], h)
        sc = jnp.where(j <= i, mm_nt(q * m, kc) * scale, NEG_INF)
        sp = jnp.where(jnp.logical_and(i <= j, has_prev), mm_nt(q * m, kp) * scale, NEG_INF)
        mx = jnp.maximum(jnp.max(sc, axis=-1, keepdims=True), jnp.max(sp, axis=-1, keepdims=True))
        pc, pp = jnp.exp(sc - mx), jnp.exp(sp - mx)
        den = jnp.sum(pc, axis=-1, keepdims=True) + jnp.sum(pp, axis=-1, keepdims=True)
        o = o + (mm_nn(pc, vc) + mm_nn(pp, vp)) / den * m
        l = l + (mx + jnp.log(den)) * m
    return o, l


def _dil_specs(gi, d):
    parts = 1 if d == 1 else 2
    blk = (DIL_BLOCK * d, GROUP_COLS // parts)
    at = lambda col: (lambda p, n: (n, col * parts + p))
    before = lambda col: (lambda p, n: (jnp.maximum(n - 1, 0), col * parts + p))
    v0 = DIL_WIDTH // GROUP_COLS + gi
    q = pl.BlockSpec(blk, at(gi))
    kc, kp = pl.BlockSpec(blk, at(gi)), pl.BlockSpec(blk, before(gi))
    vc, vp = pl.BlockSpec(blk, at(v0)), pl.BlockSpec(blk, before(v0))
    out = pl.BlockSpec(blk, at(0))
    together = min(d, 2)
    return (q, kc, kp, vc, vp, out), parts, together


def _residue_rows(r, d):
    return pl.ds(r, DIL_BLOCK, stride=d) if d > 1 else pl.ds(0, DIL_BLOCK)


def dil_fwd(q, k, kv, gi, d, name):
    T = q.shape[0]
    (qs, kc, kp, vc, vp, out), parts, together = _dil_specs(gi, d)

    def body(q_ref, kc_ref, kp_ref, vc_ref, vp_ref, o_ref, l_ref):
        has_prev = pl.program_id(1) > 0

        def residues(it, carry):
            rows = [_residue_rows(it * together + a, d) for a in range(together)]
            ins = [[ref[rw, :] for ref in (q_ref, kc_ref, kp_ref, vc_ref, vp_ref)] for rw in rows]
            res = [_f_dilattn(has_prev, *x) for x in ins]
            for rw, (o, l) in zip(rows, res):
                o_ref[rw, :] = o
                l_ref[rw, :] = l
            return carry

        lax.fori_loop(0, d // together, residues, 0)

    shape = jax.ShapeDtypeStruct((T, 4 * HEAD_DIM), F32)
    return pl.pallas_call(
        body, grid=(parts, T // (DIL_BLOCK * d)), in_specs=[qs, kc, kp, vc, vp], out_specs=[out, out], out_shape=[shape, shape],
        compiler_params=_cparams(("parallel", "parallel")), name=name)(q, k, k, kv, kv)


def dil_bwd(q, k, kv, do, dl, gi, d, name):
    T = q.shape[0]
    (qs, kc, kp, vc, vp, out), parts, together = _dil_specs(gi, d)

    def body(q_ref, kc_ref, kp_ref, vc_ref, vp_ref, do_ref, dl_ref, *outs):
        f = functools.partial(_f_dilattn, pl.program_id(1) > 0)

        def residues(it, carry):
            rows = [_residue_rows(it * together + a, d) for a in range(together)]
            ins = [[ref[rw, :] for ref in (q_ref, kc_ref, kp_ref, vc_ref, vp_ref, do_ref, dl_ref)] for rw in rows]
            res = [jax.vjp(f, *x[:5])[1]((x[5], x[6])) for x in ins]
            for rw, gs in zip(rows, res):
                for o_ref, g in zip(outs, gs):
                    o_ref[rw, :] = g
            return carry

        lax.fori_loop(0, d // together, residues, 0)

    shape = jax.ShapeDtypeStruct((T, 4 * HEAD_DIM), F32)
    dq, dkc, dkp, dvc, dvp = pl.pallas_call(
        body, grid=(parts, T // (DIL_BLOCK * d)), in_specs=[qs, kc, kp, vc, vp, out, out], out_specs=[out] * 5, out_shape=[shape] * 5,
        compiler_params=_cparams(("parallel", "parallel")), name=name)(q, k, k, kv, kv, do, dl)

    def own_plus_next(c, p):
        return c + jnp.concatenate([p[DIL_BLOCK * d:], jnp.zeros_like(p[:DIL_BLOCK * d])], axis=0)

    return dq, own_plus_next(dkc, dkp), own_plus_next(dvc, dvp)


CONV_TILE = 256


def _conv3(before, u, w, b):
    ue = jnp.concatenate([before, u], axis=0)
    s1, s2 = pltpu.roll(ue, 1, 0)[8:], pltpu.roll(ue, 2, 0)[8:]
    return b + w[0:1] * s2 + w[1:2] * s1 + w[2:3] * u, s1, s2


def _conv_halves(u_ref, h_ref, cw_ref, cb_ref):
    F = D_FF
    res = []
    for lo in (0, F):
        before = jnp.where(pl.program_id(0) > 0, h_ref[:, lo:lo + F], 0.0)
        u = u_ref[:, lo:lo + F]
        res.append((u,) + _conv3(before, u, cw_ref[:, lo:lo + F], cb_ref[:, lo:lo + F]))
    return res


def _halo_before(C):
    return pl.BlockSpec((8, C), lambda i: (jnp.maximum(i * (CONV_TILE // 8) - 1, 0), 0))


def convgate_fwd(u, cw, cb, name):
    T, C = u.shape
    F = C // 2

    def body(u_ref, h_ref, cw_ref, cb_ref, z_ref):
        (_, cg, _, _), (_, cv, _, _) = _conv_halves(u_ref, h_ref, cw_ref, cb_ref)
        z_ref[...] = (cg * _sigmoid(cg) * cv).astype(BF16)

    return pl.pallas_call(
        body, grid=(T // CONV_TILE,),
        in_specs=[pl.BlockSpec((CONV_TILE, C), lambda i: (i, 0)), _halo_before(C), _full_spec(cw), _full_spec(cb)],
        out_specs=pl.BlockSpec((CONV_TILE, F), lambda i: (i, 0)), out_shape=jax.ShapeDtypeStruct((T, F), BF16),
        compiler_params=_cparams(("parallel",)), name=name)(u, u, cw, cb)


def convgate_bwd(u, cw, cb, dz, name):
    T, C = u.shape
    F = C // 2
    n = T // CONV_TILE
    E = CONV_TILE + 8

    def body(u_ref, hb_ref, ha_ref, cw_ref, cb_ref, dz_ref, dza_ref, du_ref, dcw_ref, dcb_ref):
        i = pl.program_id(0)
        dze = jnp.concatenate([dz_ref[...], jnp.where(i < n - 1, dza_ref[...], 0.0)], axis=0)

        @pl.when(i == 0)
        def _():
            dcw_ref[...] = jnp.zeros_like(dcw_ref)
            dcb_ref[...] = jnp.zeros_like(dcb_ref)

        halves = []
        for lo in (0, F):
            sl = slice(lo, lo + F)
            ue = jnp.concatenate([u_ref[:, sl], ha_ref[:, sl]], axis=0)
            c, s1, s2 = _conv3(jnp.where(i > 0, hb_ref[:, sl], 0.0), ue, cw_ref[:, sl], cb_ref[:, sl])
            halves.append((sl, ue, c, s1, s2))
        (_, _, cg, _, _), (_, _, cv, _, _) = halves
        sg = _sigmoid(cg)
        dcs = (dze * cv * sg * (1.0 + cg * (1.0 - sg)), dze * cg * sg)
        for (sl, ue, _, s1, s2), dc in zip(halves, dcs):
            own = lambda z: z[:CONV_TILE]
            dcb_ref[:, sl] += jnp.sum(own(dc), axis=0, keepdims=True)
            dcw_ref[0:1, sl] += jnp.sum(own(dc * s2), axis=0, keepdims=True)
            dcw_ref[1:2, sl] += jnp.sum(own(dc * s1), axis=0, keepdims=True)
            dcw_ref[2:3, sl] += jnp.sum(own(dc * ue), axis=0, keepdims=True)
            du = cw_ref[2:3, sl] * dc + cw_ref[1:2, sl] * pltpu.roll(dc, E - 1, 0) + cw_ref[0:1, sl] * pltpu.roll(dc, E - 2, 0)
            du_ref[:, sl] = own(du).astype(BF16)

    after = lambda w: pl.BlockSpec((8, w), lambda i: (jnp.minimum((i + 1) * (CONV_TILE // 8), T // 8 - 1), 0))
    return pl.pallas_call(
        body, grid=(n,),
        in_specs=[pl.BlockSpec((CONV_TILE, C), lambda i: (i, 0)), _halo_before(C), after(C), _full_spec(cw), _full_spec(cb),
                  pl.BlockSpec((CONV_TILE, F), lambda i: (i, 0)), after(F)],
        out_specs=[pl.BlockSpec((CONV_TILE, C), lambda i: (i, 0)), _full_spec(cw), _full_spec(cb)],
        out_shape=[jax.ShapeDtypeStruct((T, C), BF16), jax.ShapeDtypeStruct(cw.shape, F32), jax.ShapeDtypeStruct(cb.shape, F32)],
        compiler_params=_cparams(("arbitrary",)), name=name)(u, u, u, cw, cb, dz, dz)


def loss_head(y, tgt):
    T, D = y.shape
    tile = ROW_TILE

    def body(y_ref, t_ref, l_ref, d_ref, db_ref):
        d = y_ref[...] - t_ref[...]
        d_ref[...] = d * (1.0 / D)
        db_ref[...] = (d * (1.0 / D)).astype(BF16)

        @pl.when(pl.program_id(0) == 0)
        def _():
            l_ref[...] = jnp.zeros_like(l_ref)

        l_ref[...] += (0.5 / D) * jnp.sum(d * d)

    row = pl.BlockSpec((tile, D), lambda i: (i, 0))
    return pl.pallas_call(
        body, grid=(T // tile,), in_specs=[row, row], out_specs=[pl.BlockSpec((8, 128), lambda i: (0, 0)), row, row],
        out_shape=[jax.ShapeDtypeStruct((8, 128), F32), jax.ShapeDtypeStruct((T, D), F32), jax.ShapeDtypeStruct((T, D), BF16)],
        compiler_params=_cparams(("arbitrary",)), name="loss_head")(y, tgt)


def sum_parts(parts, name):
    S, R, C = parts.shape
    tile = _pick(R, (256, 128, 64, 32, 16, 8))

    def body(p_ref, o_ref):
        acc = p_ref[0]
        for s in range(1, S):
            acc = acc + p_ref[s]
        o_ref[...] = acc

    return pl.pallas_call(
        body, grid=(R // tile,), in_specs=[pl.BlockSpec((S, tile, C), lambda i: (0, i, 0))],
        out_specs=pl.BlockSpec((tile, C), lambda i: (i, 0)), out_shape=jax.ShapeDtypeStruct((R, C), F32),
        compiler_params=_cparams(("parallel",)), name=name)(parts)


def adamw(gparts, w, m, v, name):
    S, R, C = gparts.shape
    tile = _pick(R, (256, 128, 64, 32, 16, 8))
    c1 = 1.0 / (1.0 - ADAM_B1 ** ADAM_STEP)
    c2 = 1.0 / (1.0 - ADAM_B2 ** ADAM_STEP)

    def body(g_ref, w_ref, m_ref, v_ref, go_ref, d_ref, mo_ref, vo_ref):
        g = g_ref[0].astype(F32)
        for s in range(1, S):
            g = g + g_ref[s].astype(F32)
        m1 = ADAM_B1 * m_ref[...] + (1.0 - ADAM_B1) * g
        v1 = ADAM_B2 * v_ref[...] + (1.0 - ADAM_B2) * (g * g)
        go_ref[...] = g
        mo_ref[...] = m1
        vo_ref[...] = v1
        d_ref[...] = -ADAM_LR * ((m1 * c1) / (jnp.sqrt(v1 * c2) + ADAM_EPS) + ADAM_WD * w_ref[...])

    row = pl.BlockSpec((tile, C), lambda i: (i, 0))
    return pl.pallas_call(
        body, grid=(R // tile,), in_specs=[pl.BlockSpec((S, tile, C), lambda i: (0, i, 0)), row, row, row],
        out_specs=[row] * 4, out_shape=[jax.ShapeDtypeStruct((R, C), F32)] * 4,
        compiler_params=_cparams(("parallel",)), name=name)(gparts, w, m, v)


def _peers():
    x, y, c = lax.axis_index("x"), lax.axis_index("y"), lax.axis_index("c")
    peers = []
    for k in range(1, N_DEV):
        px = 1 - x if k & 4 else x
        py = 1 - y if k & 2 else y
        pc = 1 - c if k & 1 else c
        peers.append(((px, py, pc), 4 * px + 2 * py + pc))
    return 4 * x + 2 * y + c, peers


_ANY = pl.BlockSpec(memory_space=pl.ANY)


class Exchange:
    def __init__(self, gathers=(), scatters=()):
        self.gathers, self.scatters = list(gathers), list(scatters)
        self.n = len(self.gathers) + len(self.scatters)

    def operands(self):
        return self.gathers + self.scatters

    def out_shape(self):
        return ([jax.ShapeDtypeStruct((N_DEV,) + x.shape, x.dtype) for x in self.gathers]
                + [jax.ShapeDtypeStruct(x.shape, x.dtype) for x in self.scatters])

    def scratch(self):
        n = max(self.n, 1)
        return [pltpu.SemaphoreType.DMA((7 * n,)), pltpu.SemaphoreType.DMA((7 * n,)), pltpu.SemaphoreType.DMA((n,))]

    def _copies(self, in_refs, out_refs, send_sems, recv_sems, local_sems):
        me, peers = _peers()
        ng = len(self.gathers)
        local, sends, recvs = [], [], []
        for a in range(self.n):
            x, o = in_refs[a], out_refs[a]
            mine = x if a < ng else x.at[me]
            local.append(pltpu.make_async_copy(mine, o.at[me], local_sems.at[a]))
            s_a, r_a = {}, {}
            for k in range(1, N_DEV):
                peer, slot = peers[k - 1]
                sems = dict(send_sem=send_sems.at[7 * a + k - 1], recv_sem=recv_sems.at[7 * a + k - 1],
                            device_id_type=pl.DeviceIdType.MESH)
                if a >= ng:
                    s_a[k] = pltpu.make_async_remote_copy(src_ref=x.at[slot], dst_ref=o.at[me], device_id=peer, **sems)
                elif k in FORWARDED:
                    came = o.at[peers[k - 2][1]]
                    s_a[k] = pltpu.make_async_remote_copy(src_ref=came, dst_ref=came, device_id=peers[0][0], **sems)
                else:
                    s_a[k] = pltpu.make_async_remote_copy(src_ref=x, dst_ref=o.at[me], device_id=peer, **sems)
                r_a[k] = pltpu.make_async_remote_copy(src_ref=mine, dst_ref=o.at[slot], device_id=peer, **sems)
            sends.append(s_a)
            recvs.append(r_a)
        return local, sends, recvs

    def start(self, *refs):
        if self.n == 0:
            return
        local, sends, _ = self._copies(*refs)
        for a in range(self.n):
            local[a].start()
            for k in range(1, N_DEV):
                if a >= len(self.gathers) or k not in FORWARDED:
                    sends[a][k].start()

    def forward(self, *refs):
        if not self.gathers:
            return
        _, sends, recvs = self._copies(*refs)
        for a in range(len(self.gathers)):
            for k in FORWARDED:
                recvs[a][k - 1].wait_recv()
                sends[a][k].start()

    def wait(self, *refs):
        if self.n == 0:
            return
        local, sends, recvs = self._copies(*refs)
        for a in range(self.n):
            waited_early = [f - 1 for f in FORWARDED] if a < len(self.gathers) else []
            for k in range(1, N_DEV):
                if k not in waited_early:
                    recvs[a][k].wait_recv()
            for k in range(1, N_DEV):
                sends[a][k].wait_send()
            local[a].wait()


FORWARDED = (3, 5, 7)


def exchange(ex, name):
    n = ex.n

    def body(*refs):
        args = (refs[:n], refs[n:2 * n]) + tuple(refs[2 * n:])
        ex.start(*args)
        ex.forward(*args)
        ex.wait(*args)

    return pl.pallas_call(body, in_specs=[_ANY] * n, out_specs=[_ANY] * n, out_shape=ex.out_shape(),
                          scratch_shapes=ex.scratch(), name=name)(*ex.operands())


def _shift_up(z):
    return jnp.concatenate([z[1:], jnp.zeros_like(z[:1])], axis=0)


def _segments(width):
    seg = np.zeros((width, 128), np.float32)
    seg[np.arange(width), np.arange(width) // HEAD_DIM] = 1.0
    return jnp.asarray(seg), jnp.asarray(seg.T)


def _rope_consts(T):
    inv = ROPE_THETA ** (-jnp.arange(0, HEAD_DIM, 2, dtype=F32) / HEAD_DIM)
    ang = jnp.arange(T, dtype=F32)[:, None] * inv[None, :]
    return jnp.tile(jnp.cos(ang), (1, 4)), jnp.tile(jnp.sin(ang), (1, 4))


def _per_head(g, heads):
    return jnp.tile(g.reshape(1, HEAD_DIM), (1, heads))


def _sum_heads(g):
    return g.reshape(-1, HEAD_DIM).sum(axis=0, keepdims=True)


LORA_COLS = 256
RW_TILE = 256
ROW_TILE = 512


def _local_step(x0, memx, tgt, P, ex_weights=None, weights_done=None, ex_grads=None):
    T = x0.shape[0]
    P = dict(P)
    G = {}
    seg, seg_t = _segments(RWKV_WIDTH)
    mseg = (seg[:MEM_WIDTH], seg_t[:, :MEM_WIDTH])
    cos, sin = _rope_consts(T)
    row = lambda v: v.reshape(1, -1)

    def mem_fwd(i, q, into):
        memn = stage_fwd(f_rmsnorm, [memx], [P["mem_norm"][i:i + 1]], [], [], N_MEM, f"mem{i}_norm", [BF16])[0]
        kvm = matmul(memn, P["mem_w_kv"][i], "nn", f"mem{i}_kv")
        kn, qn = _per_head(P["mem_k_norm"][i], MEM_HEADS), _per_head(P["mem_q_norm"][i], MEM_HEADS)
        km = stage_fwd(f_headnorm, [Cols(kvm, MEM_WIDTH, 0)], [kn], [], mseg, N_MEM, f"mem{i}_knorm")[0]
        om = stage_fwd(f_memattn, [q], [km, Cols(kvm, MEM_WIDTH, 1), qn], [], mseg, ROW_TILE, f"mem{i}_attn", [BF16], into=into)[0]
        return om, (memn, kvm, km, kn, qn, q)

    def mem_bwd(i, saved, dymem):
        memn, kvm, km, kn, qn, q = saved
        (dq,), (dkm, dvm, g_qn) = stage_bwd(f_memattn, [q], [km, Cols(kvm, MEM_WIDTH, 1), qn], [], mseg, [dymem], ROW_TILE,
                                            f"mem{i}_attn_bwd")
        (dkraw,), (g_kn,) = stage_bwd(f_headnorm, [Cols(kvm, MEM_WIDTH, 0)], [kn], [], mseg, [dkm], N_MEM, f"mem{i}_knorm_bwd")
        dkvm = jnp.concatenate([dkraw, dvm], axis=1).astype(BF16)
        g_w = matmul(memn, dkvm, "tn", f"mem{i}_kv_dw")
        dmemn = matmul(dkvm, P["mem_w_kv"][i], "nt", f"mem{i}_kv_dx")
        _, (g_mn,) = stage_bwd(f_rmsnorm, [memx], [P["mem_norm"][i:i + 1]], [], [], [dmemn], N_MEM, f"mem{i}_norm_bwd")
        return dq, g_mn, g_w, _sum_heads(g_qn), _sum_heads(g_kn)

    def ffn_fwd(i, xin):
        hn = stage_fwd(f_rmsnorm, [xin], [P["ffn_norm"][i:i + 1]], [], [], ROW_TILE, f"ffn{i}_norm", [BF16])[0]
        u = matmul(hn, P["ffn_w_up"][i], "nn", f"ffn{i}_up")
        z = convgate_fwd(u, P["ffn_conv_w"][i], P["ffn_conv_b"][i:i + 1], f"ffn{i}_conv")
        return matmul(z, P["ffn_w_down"][i], "nn", f"ffn{i}_down", residual=xin), (hn, u, z)

    def ffn_bwd(i, xin, saved, dxo, dxo_b):
        hn, u, z = saved
        dz = matmul(dxo_b, P["ffn_w_down"][i], "nt", f"ffn{i}_down_dx")
        g_down = matmul(z, dxo_b, "tn", f"ffn{i}_down_dw")
        du, g_cw, g_cb = convgate_bwd(u, P["ffn_conv_w"][i], P["ffn_conv_b"][i:i + 1], dz, f"ffn{i}_conv_bwd")
        dhn = matmul(du, P["ffn_w_up"][i], "nt", f"ffn{i}_up_dx")
        g_up = matmul(hn, du, "tn", f"ffn{i}_up_dw")
        (dxin,), (g_n,), (dxin_b,) = stage_bwd(f_rmsnorm_res, [xin], [P["ffn_norm"][i:i + 1]], [], [], [dhn, dxo], ROW_TILE,
                                               f"ffn{i}_norm_bwd", bf16_copies=(0,))
        return dxin, dxin_b, g_n, g_up, g_cw, g_cb, g_down

    h0 = stage_fwd(f_rmsnorm, [x0], [P["attn_norm"][0:1]], [], [], ROW_TILE, "l0_norm", [BF16])[0]
    p0 = matmul(h0, P["a_w_in"][0], "nn", "l0_in")
    lora0 = 3 * RWKV_WIDTH // LORA_COLS
    pre_xs = [Cols(p0, RWKV_WIDTH, 0), Cols(p0, RWKV_WIDTH, 1), Cols(p0, RWKV_WIDTH, 2), Cols(p0, LORA_COLS, lora0)]
    mu = [Cols(P["a_mu"], RWKV_WIDTH, 0), Cols(P["a_mu"], RWKV_WIDTH, 1), Cols(P["a_mu"], RWKV_WIDTH, 2),
          Cols(P["a_mu"], LORA_COLS, lora0)]
    lora_rows = lambda w, lo: jnp.pad(w, ((lo, LORA_COLS - lo - w.shape[0]), (0, 0)))
    pre_ps = mu + [P["a_w0"], lora_rows(P["a_w2"][0], 0), P["a_a0"], lora_rows(P["a_a2"][0], 64), lora_rows(P["a_g2"][0], 128),
                   P["a_k_k"], P["a_k_a"]]
    r, lw, k2, v, kk, b, g = stage_fwd(f_rwkv_pre, pre_xs, pre_ps, [], [seg, seg_t], RW_TILE, "l0_rwkv_pre", with_prev=True)
    scan_in = [r, lw, k2, v, kk, b]
    y_h, h_states, got = rwkv_scan_fwd(*scan_in, ex_weights or Exchange())
    if weights_done is not None:
        P.update(weights_done(got))
    y_s = y_h
    post_ps = [P["a_lnx_w"], P["a_lnx_b"], P["a_r_k"].reshape(1, RWKV_WIDTH)]
    ycat0, mem0_saved = mem_fwd(0, Cols(p0, MEM_WIDTH, SHIFT_WIDTH // MEM_WIDTH), Into(None, D_MODEL, RWKV_WIDTH // MEM_WIDTH))
    ycat0 = stage_fwd(f_rwkv_post, [y_s, r, k2, v, g], post_ps, [], [seg, seg_t], RW_TILE, "l0_rwkv_post", [BF16],
                      into=Into(ycat0, D_MODEL, 0))[0]
    x1 = matmul(ycat0, P["a_w_out"][0], "nn", "l0_out", residual=x0)
    x2, ffn0_saved = ffn_fwd(0, x1)

    hk, h1 = stage_fwd(f_rmsnorm2, [x2], [row(P["kv_norm"]), P["attn_norm"][1:2]], [], [], ROW_TILE, "l1_norm", [BF16, BF16])
    kvp = matmul(hk, P["kv_w"][0], "nn", "l1_kv")
    p1 = matmul(h1, P["b_w_in"][0], "nn", "l1_in")
    kraw, qraw = Cols(kvp, DIL_WIDTH, 0), Cols(p1, DIL_WIDTH, 0)
    kgain, qgain = _per_head(P["kv_k_norm"], DIL_WIDTH // HEAD_DIM), _per_head(P["b_q_norm"], DIL_WIDTH // HEAD_DIM)
    ksh = stage_fwd(f_qkprep, [kraw], [kgain], [cos, sin], [seg, seg_t], ROW_TILE, "l1_kprep")[0]
    q = stage_fwd(f_qkprep, [qraw], [qgain], [cos, sin], [seg, seg_t], ROW_TILE, "l1_qprep")[0]
    outs, lses = [], []
    for gi, (_, d) in enumerate(DIL_GROUPS):
        og, lg = dil_fwd(q, ksh, kvp, gi, d, f"l1_dil{gi}")
        outs.append(og)
        lses.append(lg)
    ycat1 = stage_fwd(f_mix, outs + lses, [], [], [], ROW_TILE, "l1_mix", [BF16], into=Into(None, 2 * MEM_WIDTH, 0))[0]
    ycat1, mem1_saved = mem_fwd(1, Cols(p1, MEM_WIDTH, DIL_WIDTH // MEM_WIDTH), Into(ycat1, 2 * MEM_WIDTH, 1))
    x3 = matmul(ycat1, P["b_w_out"][0], "nn", "l1_out", residual=x2)
    x4, ffn1_saved = ffn_fwd(1, x3)
    loss_part, dx4, dx4_b = loss_head(x4, tgt)

    dx3, dx3_b, gn1, gup1, gcw1, gcb1, gdown1 = ffn_bwd(1, x3, ffn1_saved, dx4, dx4_b)
    dycat1 = matmul(dx3_b, P["b_w_out"][0], "nt", "l1_out_dx")
    G["b_w_out"] = [matmul(ycat1, dx3_b, "tn", "l1_out_dw")]
    dqmem1, gmn1, gmw1, gmq1, gmk1 = mem_bwd(1, mem1_saved, Cols(dycat1, MEM_WIDTH, 1))
    dmix, _ = stage_bwd(f_mix, outs + lses, [], [], [], [Cols(dycat1, MEM_WIDTH, 0)], ROW_TILE, "l1_mix_bwd")
    dq, dk, dv = zip(*[dil_bwd(q, ksh, kvp, dmix[gi], dmix[3 + gi], gi, d, f"l1_dil{gi}_bwd")
                       for gi, (_, d) in enumerate(DIL_GROUPS)])
    dq, dk, dv = jnp.concatenate(dq, axis=1), jnp.concatenate(dk, axis=1), jnp.concatenate(dv, axis=1)
    (dqraw,), (g_bq,) = stage_bwd(f_qkprep, [qraw], [qgain], [cos, sin], [seg, seg_t], [dq], ROW_TILE, "l1_qprep_bwd")
    (dkraw,), (g_kk,) = stage_bwd(f_qkprep, [kraw], [kgain], [cos, sin], [seg, seg_t], [dk], ROW_TILE, "l1_kprep_bwd")
    g_bq, g_kk = _sum_heads(g_bq), _sum_heads(g_kk)
    dp1 = jnp.concatenate([dqraw, dqmem1], axis=1).astype(BF16)
    dkvp = jnp.concatenate([dkraw, dv], axis=1).astype(BF16)
    dh1 = matmul(dp1, P["b_w_in"][0], "nt", "l1_in_dx")
    G["b_w_in"] = [matmul(h1, dp1, "tn", "l1_in_dw")]
    dhk = matmul(dkvp, P["kv_w"][0], "nt", "l1_kv_dx")
    G["kv_w"] = [matmul(hk, dkvp, "tn", "l1_kv_dw")]
    (dx2,), (g_kvn, g_an1), (dx2_b,) = stage_bwd(f_rmsnorm2_res, [x2], [row(P["kv_norm"]), P["attn_norm"][1:2]], [], [],
                                                 [dhk, dh1, dx3], ROW_TILE, "l1_norm_bwd", bf16_copies=(0,))

    dx1, dx1_b, gn0, gup0, gcw0, gcb0, gdown0 = ffn_bwd(0, x1, ffn0_saved, dx2, dx2_b)
    dycat0 = matmul(dx1_b, P["a_w_out"][0], "nt", "l0_out_dx")
    G["a_w_out"] = [matmul(ycat0, dx1_b, "tn", "l0_out_dw")]
    dqmem0, gmn0, gmw0, gmq0, gmk0 = mem_bwd(0, mem0_saved, Cols(dycat0, MEM_WIDTH, RWKV_WIDTH // MEM_WIDTH))
    (dy_s, dr_a, dk_a, dv_a, dg), (g_lw, g_lb, g_rk) = stage_bwd(
        f_rwkv_post, [y_s, r, k2, v, g], post_ps, [], [seg, seg_t], [Cols(dycat0, RWKV_WIDTH, 0)], RW_TILE, "l0_rwkv_post_bwd")
    G["mem_w_kv"], G["ffn_w_up"], G["ffn_w_down"] = [gmw0, gmw1], [gup0, gup1], [gdown0, gdown1]
    (dr_b, dlw, dk_b, dv_b, dkk, db), G["_exchanged"] = rwkv_scan_bwd(*scan_in, h_states, dy_s,
                                                                      ex_grads(G) if ex_grads else Exchange())
    dpre, gpre = stage_bwd(f_rwkv_pre, pre_xs, pre_ps, [], [seg, seg_t],
                           [[dr_a, dr_b], dlw, [dk_a, dk_b], [dv_a, dv_b], dkk, db, dg], RW_TILE, "l0_rwkv_pre_bwd", with_prev=True)
    dp_rw = jnp.concatenate(dpre[:4], axis=1) + _shift_up(jnp.concatenate(dpre[4:], axis=1))
    dp0 = jnp.concatenate([dp_rw, dqmem0], axis=1).astype(BF16)
    dh0 = matmul(dp0, P["a_w_in"][0], "nt", "l0_in_dx")
    G["a_w_in"] = [matmul(h0, dp0, "tn", "l0_in_dw")]
    (dx0,), (g_an0,) = stage_bwd(f_rmsnorm_res, [x0], [P["attn_norm"][0:1]], [], [], [dh0, dx1], ROW_TILE, "l0_norm_bwd")

    G["attn_norm"] = jnp.concatenate([g_an0, g_an1], axis=0)
    G["a_mu"] = jnp.concatenate(gpre[:4], axis=1)
    G["a_w0"], G["a_w2"], G["a_a0"], G["a_a2"], G["a_g2"] = gpre[4], gpre[5][None, :64], gpre[6], gpre[7][None, 64:128], gpre[8][None, 128:]
    G["a_k_k"], G["a_k_a"] = gpre[9], gpre[10]
    G["a_r_k"] = g_rk.reshape(1, RWKV_HEADS, HEAD_DIM)
    G["a_lnx_w"], G["a_lnx_b"] = g_lw, g_lb
    G["kv_norm"], G["kv_k_norm"], G["b_q_norm"] = g_kvn.reshape(-1), g_kk.reshape(-1), g_bq
    G["mem_norm"] = jnp.concatenate([gmn0, gmn1], axis=0)
    G["mem_w_kv"] = [gmw0, gmw1]
    G["mem_q_norm"] = jnp.concatenate([gmq0, gmq1], axis=0)
    G["mem_k_norm"] = jnp.concatenate([gmk0, gmk1], axis=0)
    G["ffn_norm"] = jnp.concatenate([gn0, gn1], axis=0)
    G["ffn_w_up"] = [gup0, gup1]
    G["ffn_conv_w"] = jnp.stack([gcw0, gcw1])
    G["ffn_conv_b"] = jnp.concatenate([gcb0, gcb1], axis=0)
    G["ffn_w_down"] = [gdown0, gdown1]
    return loss_part, dx0, G


PARAMS = (("attn_norm", None), ("a_w_in", 2), ("a_mu", 1), ("a_w0", 1), ("a_w2", 2), ("a_a0", 1), ("a_a2", 2), ("a_g2", 2),
          ("a_k_k", 1), ("a_k_a", 1), ("a_r_k", None), ("a_lnx_w", 1), ("a_lnx_b", 1), ("a_w_out", 1), ("kv_norm", None),
          ("kv_w", 1), ("kv_k_norm", None), ("b_w_in", 1), ("b_q_norm", None), ("b_w_out", 2), ("mem_norm", None),
          ("mem_w_kv", 1), ("mem_q_norm", None), ("mem_k_norm", None), ("ffn_norm", None), ("ffn_w_up", 2),
          ("ffn_conv_w", 2), ("ffn_conv_b", None), ("ffn_w_down", 1))
BIG = ("a_w_in", "a_w_out", "kv_w", "b_w_in", "b_w_out", "mem_w_kv", "ffn_w_up", "ffn_w_down")
AXIS = dict(PARAMS)
SMALL = tuple(n for n, _ in PARAMS if n not in BIG)
SMALL_SHARDED = tuple(n for n in SMALL if AXIS[n] is not None)
PACK_QUANTUM = 256 * 128


def _from_shards(xs, axis):
    full = jnp.moveaxis(xs, 0, axis)
    sh = full.shape
    return full.reshape(sh[:axis] + (sh[axis] * sh[axis + 1],) + sh[axis + 2:])


def _to_shards(g, axis):
    sh = g.shape
    return jnp.moveaxis(g.reshape(sh[:axis] + (N_DEV, sh[axis] // N_DEV) + sh[axis + 1:]), axis, 0)


def _pack(parts, lead=0):
    ld = parts[0].shape[:lead]
    flat = jnp.concatenate([p.reshape(ld + (-1,)) for p in parts], axis=-1)
    pad = (-flat.shape[-1]) % PACK_QUANTUM
    flat = jnp.pad(flat, [(0, 0)] * lead + [(0, pad)])
    return flat.reshape(ld + (-1, 128))


def _unpack(packed, shapes, lead=0):
    ld = packed.shape[:lead]
    flat = packed.reshape(ld + (-1,))
    out, off = [], 0
    for s in shapes:
        n = math.prod(s)
        out.append(flat[..., off:off + n].reshape(ld + tuple(s)))
        off += n
    return out


def kernel(x, mem, attn_norm, a_w_in, a_mu, a_w0, a_w2, a_a0, a_a2, a_g2, a_k_k, a_k_a, a_r_k, a_lnx_w, a_lnx_b, a_w_out, kv_norm, kv_w, kv_k_norm, b_w_in, b_q_norm, b_w_out, mem_norm, mem_w_kv, mem_q_norm, mem_k_norm, ffn_norm, ffn_w_up, ffn_conv_w, ffn_conv_b, ffn_w_down, loss_target, m_attn_norm, m_a_w_in, m_a_mu, m_a_w0, m_a_w2, m_a_a0, m_a_a2, m_a_g2, m_a_k_k, m_a_k_a, m_a_r_k, m_a_lnx_w, m_a_lnx_b, m_a_w_out, m_kv_norm, m_kv_w, m_kv_k_norm, m_b_w_in, m_b_q_norm, m_b_w_out, m_mem_norm, m_mem_w_kv, m_mem_q_norm, m_mem_k_norm, m_ffn_norm, m_ffn_w_up, m_ffn_conv_w, m_ffn_conv_b, m_ffn_w_down, v_attn_norm, v_a_w_in, v_a_mu, v_a_w0, v_a_w2, v_a_a0, v_a_a2, v_a_g2, v_a_k_k, v_a_k_a, v_a_r_k, v_a_lnx_w, v_a_lnx_b, v_a_w_out, v_kv_norm, v_kv_w, v_kv_k_norm, v_b_w_in, v_b_q_norm, v_b_w_out, v_mem_norm, v_mem_w_kv, v_mem_q_norm, v_mem_k_norm, v_ffn_norm, v_ffn_w_up, v_ffn_conv_w, v_ffn_conv_b, v_ffn_w_down):
    names = [n for n, _ in PARAMS]
    vals = (attn_norm, a_w_in, a_mu, a_w0, a_w2, a_a0, a_a2, a_g2, a_k_k, a_k_a, a_r_k, a_lnx_w, a_lnx_b, a_w_out, kv_norm, kv_w, kv_k_norm, b_w_in, b_q_norm, b_w_out, mem_norm, mem_w_kv, mem_q_norm, mem_k_norm, ffn_norm, ffn_w_up, ffn_conv_w, ffn_conv_b, ffn_w_down)
    m_vals = (m_attn_norm, m_a_w_in, m_a_mu, m_a_w0, m_a_w2, m_a_a0, m_a_a2, m_a_g2, m_a_k_k, m_a_k_a, m_a_r_k, m_a_lnx_w, m_a_lnx_b, m_a_w_out, m_kv_norm, m_kv_w, m_kv_k_norm, m_b_w_in, m_b_q_norm, m_b_w_out, m_mem_norm, m_mem_w_kv, m_mem_q_norm, m_mem_k_norm, m_ffn_norm, m_ffn_w_up, m_ffn_conv_w, m_ffn_conv_b, m_ffn_w_down)
    v_vals = (v_attn_norm, v_a_w_in, v_a_mu, v_a_w0, v_a_w2, v_a_a0, v_a_a2, v_a_g2, v_a_k_k, v_a_k_a, v_a_r_k, v_a_lnx_w, v_a_lnx_b, v_a_w_out, v_kv_norm, v_kv_w, v_kv_k_norm, v_b_w_in, v_b_q_norm, v_b_w_out, v_mem_norm, v_mem_w_kv, v_mem_q_norm, v_mem_k_norm, v_ffn_norm, v_ffn_w_up, v_ffn_conv_w, v_ffn_conv_b, v_ffn_w_down)
    W, M, V = dict(zip(names, vals)), dict(zip(names, m_vals)), dict(zip(names, v_vals))
    layers = lambda D, n: [D[n]] if D[n].ndim == 2 else [D[n][i] for i in range(D[n].shape[0])]
    ax2 = lambda n: AXIS[n] - (W[n].ndim - 2)
    later =[(n, i) for n in BIG if n != "a_w_in" for i in range(len(layers(W, n)))]

    small_shapes = [W[n].shape for n in SMALL_SHARDED]
    got_w, got_small = exchange(Exchange(gathers=[W["a_w_in"][0].astype(BF16), _pack([W[n] for n in SMALL_SHARDED])]),
                                "gather_first")
    P = {n: W[n] for n in SMALL}
    P["a_w_in"] = [_from_shards(got_w, ax2("a_w_in"))]
    for n, s in zip(SMALL_SHARDED, _unpack(got_small, small_shapes, lead=1)):
        P[n] = _from_shards(s, AXIS[n])
    ex_weights = Exchange(gathers=[layers(W, n)[i].astype(BF16) for n, i in later])

    def weights_done(got):
        out = {}
        for (n, _), g in zip(later, got):
            out.setdefault(n, []).append(_from_shards(g, ax2(n)))
        return out

    slots = lambda G, n: jnp.stack([_to_shards(g, ax2(n)) for g in G[n]], axis=1)
    later_names = [n for n in BIG if n != "a_w_in"]
    ex_grads = lambda G: Exchange(scatters=[slots(G, n) for n in later_names])
    loss_part, dx0, G = _local_step(x[0], mem[0], loss_target[0], P, ex_weights, weights_done, ex_grads)
    gparts = dict(zip(later_names, G.pop("_exchanged")))
    replicated = [n for n in SMALL if AXIS[n] is None]
    small_slots = _pack([_to_shards(G[n], AXIS[n]) for n in SMALL_SHARDED], lead=1)
    got_rep, gparts["a_w_in"], got_sharded = exchange(
        Exchange(gathers=[_pack([G[n] for n in replicated] + [loss_part[0:1, 0:1]])], scatters=[slots(G, "a_w_in"), small_slots]),
        "exchange_last")

    results = {}
    for n in BIG:
        rows = lambda z: z.reshape((-1,) + z.shape[-1:])
        res = adamw(gparts[n].reshape((N_DEV, -1) + gparts[n].shape[-1:]), rows(W[n]), rows(M[n]), rows(V[n]), f"adamw_{n}")
        results[n] = [r.reshape(W[n].shape) for r in res]
    *rep_sums, loss = _unpack(sum_parts(got_rep, "sum_replicated_grads"), [W[n].shape for n in replicated] + [()])
    g_mine = dict(zip(replicated, rep_sums))
    g_mine.update(zip(SMALL_SHARDED, _unpack(sum_parts(got_sharded, "sum_small_sharded_grads"), [W[n].shape for n in SMALL_SHARDED])))
    res = adamw(_pack([g_mine[n] for n in SMALL])[None], _pack([W[n] for n in SMALL]), _pack([M[n] for n in SMALL]),
                _pack([V[n] for n in SMALL]), "adamw_small")
    for n, parts in zip(SMALL, zip(*[_unpack(r, [W[n].shape for n in SMALL]) for r in res])):
        results[n] = list(parts)
    outs = [[results[n][j] for n in names] for j in range(4)]
    return (loss, dx0[None], *outs[0], *outs[1], *outs[2], *outs[3])
```

```python
import functools
import math

import jax
import jax.numpy as jnp
import numpy as np
from jax import lax
from jax.experimental import pallas as pl
from jax.experimental.pallas import tpu as pltpu

F32 = jnp.float32
BF16 = jnp.bfloat16
HI = lax.Precision.HIGHEST
H3 = lax.Precision.HIGH

N_DEV = 8
D_MODEL = 1024
HEAD_DIM = 64
N_MEM = 256
MEM_HEADS = 4
MEM_WIDTH = 256
RWKV_HEADS = 12
RWKV_WIDTH = 768
SHIFT_WIDTH = 2560
DIL_GROUPS = ((128, 1), (512, 4), (2048, 16))
DIL_BLOCK = 128
DIL_WIDTH = 768
D_FF = 2816
RMS_EPS = 1e-6
LNX_EPS = 64e-5
NEG_INF = -1e30
ROPE_THETA = 10000.0
ADAM_LR, ADAM_B1, ADAM_B2, ADAM_EPS, ADAM_WD, ADAM_STEP = 0.001, 0.9, 0.999, 1e-08, 0.01, 10

CHUNK = 64
SCAN_GROUPS_FWD, SCAN_GROUPS_BWD = 1, 1
MM_TILE_CAP = 1408
VMEM_LIMIT_V7X = 48 * 1024 * 1024


def _cparams(sem):
    return pltpu.CompilerParams(dimension_semantics=sem, vmem_limit_bytes=VMEM_LIMIT_V7X)


def _pick(n, cands):
    for c in cands:
        if n % c == 0:
            return c
    return n


def _tile(n, cap):
    if n <= cap:
        return n
    for d in range(cap - cap % 128, 0, -128):
        if n % d == 0:
            return d
    return n


def _dg(a, b, ca, cb, batch):
    dims = (((ca,), (cb,)), ((0,), (0,))) if batch else (((ca,), (cb,)), ((), ()))
    return lax.dot_general(a.astype(BF16), b.astype(BF16), dims, preferred_element_type=F32)


@jax.custom_vjp
def mm_nn(a, b):
    n = a.ndim
    return _dg(a, b, n - 1, n - 2, n == 3)


def _mm_nn_fwd(a, b):
    return mm_nn(a, b), (a, b)


def _mm_nn_bwd(res, g):
    a, b = res
    n = a.ndim
    return _dg(g, b, n - 1, n - 1, n == 3), _dg(a, g, n - 2, n - 2, n == 3)


mm_nn.defvjp(_mm_nn_fwd, _mm_nn_bwd)


@jax.custom_vjp
def mm_nt(a, b):
    n = a.ndim
    return _dg(a, b, n - 1, n - 1, n == 3)


def _mm_nt_fwd(a, b):
    return mm_nt(a, b), (a, b)


def _mm_nt_bwd(res, g):
    a, b = res
    n = a.ndim
    return _dg(g, b, n - 1, n - 2, n == 3), _dg(g, a, n - 2, n - 2, n == 3)


mm_nt.defvjp(_mm_nt_fwd, _mm_nt_bwd)


def mmh(a, b, precision=H3):
    n = a.ndim
    dims = (((n - 1,), (n - 2,)), ((0,), (0,))) if n == 3 else (((1,), (0,)), ((), ()))
    return lax.dot_general(a, b, dims, precision=precision, preferred_element_type=F32)


def mmh_nt(a, b):
    n = a.ndim
    dims = (((n - 1,), (n - 1,)), ((0,), (0,))) if n == 3 else (((1,), (1,)), ((), ()))
    return lax.dot_general(a, b, dims, precision=H3, preferred_element_type=F32)


def mmh_tn(a, b):
    n = a.ndim
    dims = (((n - 2,), (n - 2,)), ((0,), (0,))) if n == 3 else (((0,), (0,)), ((), ()))
    return lax.dot_general(a, b, dims, precision=H3, preferred_element_type=F32)


def matmul(a, b, mode, name, residual=None, ex=None):
    out_dtype = BF16 if mode == "tn" else F32
    ex = ex or Exchange()
    if mode == "nn":
        (M, K), (_, N) = a.shape, b.shape
    elif mode == "nt":
        (M, K), (N, _) = a.shape, b.shape
    else:
        (K, M), (_, N) = a.shape, b.shape
    tm = _tile(M, 2048 if mode == "nn" else MM_TILE_CAP)
    tn = _tile(N, 512 if mode == "nn" else MM_TILE_CAP)
    tk = _tile(K, MM_TILE_CAP)
    nk = K // tk
    if mode == "nn":
        a_spec = pl.BlockSpec((tm, tk), lambda i, j, k: (i, k))
        b_spec = pl.BlockSpec((tk, tn), lambda i, j, k: (k, j))
        dims = (((1,), (0,)), ((), ()))
    elif mode == "nt":
        a_spec = pl.BlockSpec((tm, tk), lambda i, j, k: (i, k))
        b_spec = pl.BlockSpec((tn, tk), lambda i, j, k: (j, k))
        dims = (((1,), (1,)), ((), ()))
    else:
        a_spec = pl.BlockSpec((tk, tm), lambda i, j, k: (k, i))
        b_spec = pl.BlockSpec((tk, tn), lambda i, j, k: (k, j))
        dims = (((0,), (0,)), ((), ()))
    o_spec = pl.BlockSpec((tm, tn), lambda i, j, k: (i, j))
    has_res = residual is not None
    ins = [a, b] + ([residual] if has_res else [])
    grid = (M // tm, N // tn, nk)

    def body(*refs):
        in_refs, (o_ref,), (acc_ref,), ex_refs = _ex_split(ex, refs, len(ins), 1)
        a_ref, b_ref = in_refs[:2]
        i, j, k = pl.program_id(0), pl.program_id(1), pl.program_id(2)

        @pl.when(jnp.logical_and(jnp.logical_and(i == 0, j == 0), k == 0))
        def _():
            ex.start(*ex_refs)

        @pl.when(k == 0)
        def _():
            acc_ref[...] = jnp.zeros_like(acc_ref)

        acc_ref[...] += lax.dot_general(a_ref[...].astype(BF16), b_ref[...].astype(BF16), dims,
                                        preferred_element_type=F32)

        @pl.when(k == nk - 1)
        def _():
            if has_res:
                o_ref[...] = (acc_ref[...] + in_refs[2][...]).astype(out_dtype)
            else:
                o_ref[...] = acc_ref[...].astype(out_dtype)

        @pl.when(jnp.logical_and(jnp.logical_and(i == grid[0] - 1, j == grid[1] - 1), k == nk - 1))
        def _():
            ex.forward(*ex_refs)
            ex.wait(*ex_refs)

    in_specs = [a_spec, b_spec] + ([o_spec] if has_res else [])
    res = pl.pallas_call(
        body, grid=grid, in_specs=in_specs + [_ANY] * ex.n, out_specs=[o_spec] + [_ANY] * ex.n,
        out_shape=[jax.ShapeDtypeStruct((M, N), out_dtype)] + ex.out_shape(), scratch_shapes=[pltpu.VMEM((tm, tn), F32)] + ex.scratch(),
        compiler_params=_cparams(("arbitrary",) * 3 if ex.n else ("parallel", "parallel", "arbitrary")), name=name)(*ins, *ex.operands())
    return (res[0], list(res[1:])) if ex.n else res[0]


class Cols:
    def __init__(self, arr, width, idx):
        self.arr, self.width, self.idx = arr, width, idx


def _arr(x):
    return x.arr if isinstance(x, Cols) else x


def _shape(x):
    return x.arr.shape[:-1] + (x.width,) if isinstance(x, Cols) else x.shape


def _col(x):
    return x.idx if isinstance(x, Cols) else 0


def _tok_spec(x, tile):
    shape, col = _shape(x), _col(x)
    return pl.BlockSpec(shape[:-2] + (tile, shape[-1]), lambda i: (0,) * (len(shape) - 2) + (i, col))


def _full_spec(x):
    shape, col = _shape(x), _col(x)
    return pl.BlockSpec(shape, lambda i: (0,) * (len(shape) - 1) + (col,))


def _halo_spec(x, tile):
    shape, col = _shape(x), _col(x)
    return pl.BlockSpec((8, shape[-1]), lambda i: (jnp.maximum(i * (tile // 8) - 1, 0), col))


def _blk(x, tile):
    shape = _shape(x)
    return jax.ShapeDtypeStruct(shape[:-2] + (tile, shape[-1]), _arr(x).dtype)


def _prev_rows(x, halo):
    rows = lax.broadcasted_iota(jnp.int32, (x.shape[0], 1), 0)
    before = jnp.where(pl.program_id(0) > 0, halo[7:8], 0.0)
    return jnp.where(rows == 0, before, pltpu.roll(x, 1, 0))


class Into:
    def __init__(self, buf, total, idx):
        self.buf, self.total, self.idx = buf, total, idx

    def place(self, shape, tile):
        idx = self.idx
        return shape.update(shape=(shape.shape[0], self.total)), pl.BlockSpec((tile, shape.shape[1]), lambda i: (i, idx))

    def operand(self, n_in, n_out_index):
        if self.buf is None:
            return [], [], {}
        return [self.buf], [_ANY], {n_in: n_out_index}


def stage_fwd(f, xs, ps, cts, cfs, tile, name, out_dtypes=None, with_prev=False, into=None):
    xs, ps, cts, cfs = list(xs), list(ps), list(cts), list(cfs)
    halos = xs if with_prev else []
    nx, nh, nct, np_ = len(xs), len(halos), len(cts), len(ps)
    T = _shape(xs[0])[-2]
    blk = [_blk(x, tile) for x in xs]
    out_avals = jax.eval_shape(f, *blk, *(blk if with_prev else []), *[_blk(p, _shape(p)[-2]) for p in ps],
                               *[_blk(c, tile) for c in cts], *[_blk(c, _shape(c)[-2]) for c in cfs])
    if out_dtypes is None:
        out_dtypes = [o.dtype for o in out_avals]
    out_shape = [jax.ShapeDtypeStruct(o.shape[:-2] + (T, o.shape[-1]), dt) for o, dt in zip(out_avals, out_dtypes)]
    out_specs = [_tok_spec(o, tile) for o in out_shape]
    n_in = nx + nh + nct + np_ + len(cfs)
    extra, extra_specs, alias = [], [], {}
    if into is not None:
        out_shape[0], out_specs[0] = into.place(out_shape[0], tile)
        extra, extra_specs, alias = into.operand(n_in, 0)

    def body(*refs):
        vals = [r[...] for r in refs[:n_in]]
        xv, hv, rest = vals[:nx], vals[nx:nx + nh], vals[nx + nh:]
        ctv, pv, cfv = rest[:nct], rest[nct:nct + np_], rest[nct + np_:]
        prev = [_prev_rows(x, h) for x, h in zip(xv, hv)]
        res = f(*xv, *prev, *pv, *ctv, *cfv)
        for o_ref, r in zip(refs[n_in + len(extra):], res):
            o_ref[...] = r.astype(o_ref.dtype)

    return pl.pallas_call(
        body, grid=(T // tile,),
        in_specs=([_tok_spec(x, tile) for x in xs] + [_halo_spec(x, tile) for x in halos] + [_tok_spec(c, tile) for c in cts]
                  + [_full_spec(p) for p in ps + cfs] + extra_specs),
        out_specs=out_specs, out_shape=out_shape, input_output_aliases=alias,
        compiler_params=_cparams(("parallel",)), name=name)(*[_arr(a) for a in xs + halos + cts + ps + cfs], *extra)


def stage_bwd(f, xs, ps, cts, cfs, gs, tile, name, bf16_copies=(), with_prev=False):
    xs, ps, cts, cfs = list(xs), list(ps), list(cts), list(cfs)
    gs = [list(g) if isinstance(g, (list, tuple)) else [g] for g in gs]
    g_flat = [a for g in gs for a in g]
    halos = xs if with_prev else []
    nx, nh, nct, ng, np_ = len(xs), len(halos), len(cts), len(g_flat), len(ps)
    T = _shape(xs[0])[-2]
    dx_like = xs + halos
    out_shape = ([jax.ShapeDtypeStruct(_shape(x), F32) for x in dx_like] + [jax.ShapeDtypeStruct(_shape(p), F32) for p in ps]
                 + [jax.ShapeDtypeStruct(_shape(xs[i]), BF16) for i in bf16_copies])
    n_in = nx + nh + nct + ng + np_ + len(cfs)
    ndx = nx + nh

    def body(*refs):
        vals = [r[...] for r in refs[:n_in]]
        outs = refs[n_in:]
        xv, hv, rest = vals[:nx], vals[nx:nx + nh], vals[nx + nh:]
        ctv, gparts, pv, cfv = rest[:nct], rest[nct:nct + ng], rest[nct + ng:nct + ng + np_], rest[nct + ng + np_:]
        gv = []
        for g in gs:
            gv.append(functools.reduce(lambda a, b: a + b, gparts[:len(g)]))
            gparts = gparts[len(g):]
        prev = [_prev_rows(x, h) for x, h in zip(xv, hv)]
        _, vjp = jax.vjp(lambda *xp: f(*xp, *ctv, *cfv), *xv, *prev, *pv)
        d = vjp(tuple(gv))
        for o_ref, r in zip(outs[:ndx], d[:ndx]):
            o_ref[...] = r
        for o_ref, i in zip(outs[ndx + np_:], bf16_copies):
            o_ref[...] = d[i].astype(BF16)

        @pl.when(pl.program_id(0) == 0)
        def _():
            for o_ref in outs[ndx:ndx + np_]:
                o_ref[...] = jnp.zeros_like(o_ref)

        for o_ref, r in zip(outs[ndx:ndx + np_], d[ndx:]):
            o_ref[...] += r

    plain = lambda x: jax.ShapeDtypeStruct(_shape(x), F32)
    res = pl.pallas_call(
        body, grid=(T // tile,),
        in_specs=([_tok_spec(x, tile) for x in xs] + [_halo_spec(x, tile) for x in halos]
                  + [_tok_spec(c, tile) for c in cts + g_flat] + [_full_spec(p) for p in ps + cfs]),
        out_specs=([_tok_spec(plain(x), tile) for x in dx_like] + [_full_spec(plain(p)) for p in ps]
                   + [_tok_spec(plain(xs[i]), tile) for i in bf16_copies]), out_shape=out_shape,
        compiler_params=_cparams(("arbitrary",)), name=name)(*[_arr(a) for a in xs + halos + cts + g_flat + ps + cfs])
    if bf16_copies:
        return list(res[:ndx]), list(res[ndx:ndx + np_]), list(res[ndx + np_:])
    return list(res[:ndx]), list(res[ndx:])


def _rms(x, g, eps=RMS_EPS):
    return x * lax.rsqrt(jnp.mean(x * x, axis=-1, keepdims=True) + eps) * g


def f_rmsnorm(x, g):
    return (_rms(x, g),)


def f_rmsnorm_res(x, g):
    return _rms(x, g), x


def f_rmsnorm2(x, g1, g2):
    n = x * lax.rsqrt(jnp.mean(x * x, axis=-1, keepdims=True) + RMS_EPS)
    return n * g1, n * g2


def f_rmsnorm2_res(x, g1, g2):
    return f_rmsnorm2(x, g1, g2) + (x,)


def _sigmoid(x):
    return 1.0 / (1.0 + jnp.exp(-x))


def _softplus(x):
    return jnp.maximum(x, 0.0) + jnp.log(1.0 + jnp.exp(-jnp.abs(x)))


def f_rwkv_pre(pr, pk, pv, pl_, qr, qk, qv, ql, mu_r, mu_k, mu_v, mu_l, w0, w2, a0, a2, g2, k_k, k_a, seg, seg_t):
    xr = pr + (qr - pr) * mu_r
    xk = pk + (qk - pk) * mu_k
    xv = pv + (qv - pv) * mu_v
    xl = pl_ + (ql - pl_) * mu_l
    w_log = -_softplus(-(w0 + mm_nn(jnp.tanh(xl), w2))) - 0.5
    lw = -jnp.exp(w_log)
    a = _sigmoid(a0 + mm_nn(xl, a2))
    g = mm_nn(_sigmoid(xl), g2)
    kkr = xk * k_k
    inv = lax.rsqrt(jnp.maximum(mmh(kkr * kkr, seg), 1e-24))
    kk = kkr * mmh(inv, seg_t)
    k2 = xk * (1.0 + (a - 1.0) * k_a)
    return xr, lw, k2, xv, kk, kk * a, g


def f_rwkv_post(y, r, k2, v, g, lnx_w, lnx_b, r_k, seg, seg_t):
    inv_n = 1.0 / HEAD_DIM
    m = mmh(mmh(y, seg) * inv_n, seg_t)
    yc = y - m
    rstd = lax.rsqrt(mmh(yc * yc, seg) * inv_n + LNX_EPS)
    yn = yc * mmh(rstd, seg_t) * lnx_w + lnx_b
    bonus = mmh(mmh(r * k2 * r_k, seg), seg_t) * v
    return ((yn + bonus) * g,)


def _headnorm(z, g, seg, seg_t):
    ms = mmh(z * z, seg) * (1.0 / HEAD_DIM)
    return z * mmh(lax.rsqrt(ms + RMS_EPS), seg_t) * g


def f_headnorm(z, g, seg, seg_t):
    return (_headnorm(z, g, seg, seg_t),)


def _rot_half(z):
    w = z.shape[1]
    half = HEAD_DIM // 2
    lane = lax.broadcasted_iota(jnp.int32, (1, w), 1)
    return jnp.where((lane & (HEAD_DIM - 1)) < half, -pltpu.roll(z, w - half, 1), pltpu.roll(z, half, 1))


@jax.custom_vjp
def _rotate_half(z):
    return _rot_half(z)


_rotate_half.defvjp(lambda z: (_rot_half(z), None), lambda _, g: (-_rot_half(g),))


def f_qkprep(z, g, cos, sin, seg, seg_t):
    zn = _headnorm(z, g, seg, seg_t)
    pairs = z.shape[1] // cos.shape[1]
    return (zn * jnp.tile(cos, (1, pairs)) + _rotate_half(zn) * jnp.tile(sin, (1, pairs)),)


def _head_mask(width, h):
    lane = lax.broadcasted_iota(jnp.int32, (1, width), 1)
    return jnp.where((lane >> 6) == h, jnp.ones((), F32), 0.0)


def f_memattn(q, k, v, q_norm, seg, seg_t):
    qn = _headnorm(q, q_norm, seg, seg_t)
    out = jnp.zeros_like(q)
    for h in range(MEM_HEADS):
        m = _head_mask(MEM_WIDTH, h)
        s = mm_nt(qn * m, k) * (1.0 / math.sqrt(HEAD_DIM))
        s = s - jnp.max(s, axis=-1, keepdims=True)
        p = jnp.exp(s)
        p = p / jnp.sum(p, axis=-1, keepdims=True)
        out = out + mm_nn(p, v) * m
    return (out,)


def f_mix(o1, o2, o3, l1, l2, l3):
    mx = jnp.maximum(jnp.maximum(l1, l2), l3)
    e1, e2, e3 = jnp.exp(l1 - mx), jnp.exp(l2 - mx), jnp.exp(l3 - mx)
    return ((e1 * o1 + e2 * o2 + e3 * o3) / (e1 + e2 + e3),)


def _chunk_masks(L):
    t = lax.broadcasted_iota(jnp.int32, (L, L), 0)
    s = lax.broadcasted_iota(jnp.int32, (L, L), 1)
    return t, s


def _unit_lower_inverse(a):
    L = a.shape[-1]
    t, s = _chunk_masks(L)
    one = jnp.ones((), F32)
    blk = lambda sh: jnp.where((t >> sh) == (s >> sh), one, 0.0)
    n0 = a * blk(3)
    x = jnp.where(t == s, one, 0.0) - n0
    n2 = mmh(n0, n0)
    x = x + mmh(x, n2)
    x = x + mmh(x, mmh(n2, n2))
    for sh in (3, 4, 5):
        if (1 << sh) >= L:
            break
        off = a * (blk(sh + 1) - blk(sh))
        x = x - mmh(x, mmh(off, x))
    return x


@jax.custom_vjp
def _inverse_known(a, x):
    return x


def _inverse_known_fwd(a, x):
    return x, x


def _inverse_known_bwd(x, dx):
    return -mmh_nt(mmh_tn(x, dx), x), jnp.zeros_like(x)


_inverse_known.defvjp(_inverse_known_fwd, _inverse_known_bwd)


def _running_sum(x, reverse):
    L = x.shape[1]
    pos = lax.broadcasted_iota(jnp.int32, (1, L, 1), 1)
    step = 1
    while step < L:
        if reverse:
            x = x + jnp.where(pos < L - step, pltpu.roll(x, L - step, 1), 0.0)
        else:
            x = x + jnp.where(pos >= step, pltpu.roll(x, step, 1), 0.0)
        step *= 2
    return x


@jax.custom_vjp
def _cumsum_tokens(x):
    return _running_sum(x, False)


_cumsum_tokens.defvjp(lambda x: (_running_sum(x, False), None), lambda _, g: (_running_sum(g, True),))


def f_rwkv_chunk(s0, r, lw, k, v, kk, b, x_known=None):
    H, L, _ = r.shape
    t, s = _chunk_masks(L)
    one = jnp.ones((), F32)
    incl = jnp.where(t >= s, one, 0.0)
    strict = jnp.where(t > s, one, 0.0)
    cum = _cumsum_tokens(lw)
    w_in = jnp.exp(cum)
    w_ex = jnp.exp(cum - lw)
    w_inv = jnp.exp(-cum)
    rt, kkt, kt, bt = r * w_in, kk * w_ex, k * w_inv, b * w_inv
    a_b = mmh_nt(kkt, bt) * strict
    a_k = mmh_nt(kkt, kt) * strict
    m_k = mmh_nt(rt, kt) * incl
    m_b = mmh_nt(rt, bt) * incl
    x = _unit_lower_inverse(a_b) if x_known is None else _inverse_known(a_b, x_known)
    u = mmh(x, mmh_nt(kkt, s0) + mmh(a_k, v))
    y = mmh_nt(rt, s0) + mmh(m_k, v) - mmh(m_b, u)
    w_last = jnp.exp(jnp.sum(lw, axis=1, keepdims=True))
    s1 = (s0 + mmh_tn(v, kt) - mmh_tn(u, bt)) * w_last
    return y, s1, x


def _ex_split(ex, refs, n_in, n_out):
    n = ex.n
    ins, ex_in = refs[:n_in], refs[n_in:n_in + n]
    outs, ex_out = refs[n_in + n:n_in + n + n_out], refs[n_in + n + n_out:n_in + 2 * n + n_out]
    rest = refs[n_in + 2 * n + n_out:]
    return ins, outs, rest[:len(rest) - 3], (ex_in, ex_out) + tuple(rest[len(rest) - 3:])


def _split_heads(x):
    return jnp.stack([x[:, h * HEAD_DIM:(h + 1) * HEAD_DIM] for h in range(x.shape[1] // HEAD_DIM)], axis=0)


def _merge_heads(x):
    return jnp.concatenate([x[h] for h in range(x.shape[0])], axis=1)


def rwkv_scan_fwd(r, lw, k, v, kk, b, ex):
    T, N = r.shape[0], HEAD_DIM
    H = r.shape[1] // N
    groups = SCAN_GROUPS_FWD
    nc, hg = T // CHUNK, H // groups
    seq = pl.BlockSpec((CHUNK, hg * N), lambda g, c: (c, g))

    def body(*refs):
        (r_ref, lw_ref, k_ref, v_ref, kk_ref, b_ref), (y_ref, hs_ref, xs_ref), (h_scr,), ex_refs = _ex_split(ex, refs, 6, 3)
        g, c = pl.program_id(0), pl.program_id(1)

        @pl.when(jnp.logical_and(g == 0, c == 0))
        def _():
            ex.start(*ex_refs)

        @pl.when(c == 0)
        def _():
            h_scr[...] = jnp.zeros_like(h_scr)

        h0 = h_scr[...]
        hs_ref[0] = h0
        y, h1, x = f_rwkv_chunk(h0, *[_split_heads(z[...]) for z in (r_ref, lw_ref, k_ref, v_ref, kk_ref, b_ref)])
        y_ref[...] = _merge_heads(y)
        xs_ref[0] = x
        h_scr[...] = h1

        @pl.when(jnp.logical_and(g == groups - 1, c == (3 * nc) // 4))
        def _():
            ex.forward(*ex_refs)

        @pl.when(jnp.logical_and(g == groups - 1, c == nc - 1))
        def _():
            ex.wait(*ex_refs)

    res = pl.pallas_call(
        body, grid=(groups, nc), in_specs=[seq] * 6 + [_ANY] * ex.n,
        out_specs=[seq, pl.BlockSpec((1, hg, N, N), lambda g, c: (c, g, 0, 0)),
                   pl.BlockSpec((1, hg, CHUNK, CHUNK), lambda g, c: (c, g, 0, 0))] + [_ANY] * ex.n,
        out_shape=[jax.ShapeDtypeStruct((T, H * N), F32), jax.ShapeDtypeStruct((nc, H, N, N), F32),
                   jax.ShapeDtypeStruct((nc, H, CHUNK, CHUNK), F32)] + ex.out_shape(),
        scratch_shapes=[pltpu.VMEM((hg, N, N), F32)] + ex.scratch(),
        compiler_params=_cparams(("arbitrary", "arbitrary")), name="rwkv_scan_fwd")(r, lw, k, v, kk, b, *ex.operands())
    return res[0], (res[1], res[2]), list(res[3:])


def rwkv_scan_bwd(r, lw, k, v, kk, b, saved, dy, ex):
    T, N = r.shape[0], HEAD_DIM
    H = r.shape[1] // N
    groups = SCAN_GROUPS_BWD
    nc, hg = T // CHUNK, H // groups
    seq = pl.BlockSpec((CHUNK, hg * N), lambda g, c: (nc - 1 - c, g))
    state = pl.BlockSpec((1, hg, N, N), lambda g, c: (nc - 1 - c, g, 0, 0))

    def body(*refs):
        (r_ref, lw_ref, k_ref, v_ref, kk_ref, b_ref, hs_ref, xs_ref, dy_ref), outs, (dh_scr,), ex_refs = _ex_split(ex, refs, 9, 6)
        g, c = pl.program_id(0), pl.program_id(1)

        @pl.when(jnp.logical_and(g == 0, c == 0))
        def _():
            ex.start(*ex_refs)

        @pl.when(c == 0)
        def _():
            dh_scr[...] = jnp.zeros_like(dh_scr)

        x_known = xs_ref[0]
        _, vjp = jax.vjp(lambda *a: f_rwkv_chunk(*a, x_known=x_known)[:2], hs_ref[0],
                         *[_split_heads(z[...]) for z in (r_ref, lw_ref, k_ref, v_ref, kk_ref, b_ref)])
        d = vjp((_split_heads(dy_ref[...]), dh_scr[...]))
        dh_scr[...] = d[0]
        for o_ref, dz in zip(outs, d[1:]):
            o_ref[...] = _merge_heads(dz)

        @pl.when(jnp.logical_and(g == groups - 1, c == nc - 1))
        def _():
            ex.forward(*ex_refs)
            ex.wait(*ex_refs)

    res = pl.pallas_call(
        body, grid=(groups, nc),
        in_specs=[seq] * 6 + [state, state, seq] + [_ANY] * ex.n,
        out_specs=[seq] * 6 + [_ANY] * ex.n, out_shape=[jax.ShapeDtypeStruct((T, H * N), F32)] * 6 + ex.out_shape(),
        scratch_shapes=[pltpu.VMEM((hg, N, N), F32)] + ex.scratch(),
        compiler_params=_cparams(("arbitrary", "arbitrary")), name="rwkv_scan_bwd")(r, lw, k, v, kk, b, *saved, dy, *ex.operands())
    return list(res[:6]), list(res[6:])


GROUP_COLS = 4 * HEAD_DIM


def _f_dilattn(has_prev, q, kc, kp, vc, vp):
    scale = 1.0 / math.sqrt(HEAD_DIM)
    i = lax.broadcasted_iota(jnp.int32, (DIL_BLOCK, DIL_BLOCK), 0)
    j = lax.broadcasted_iota(jnp.int32, (DIL_BLOCK, DIL_BLOCK), 1)
    o, l = jnp.zeros_like(q), jnp.zeros_like(q)
    for h in range(q.shape[1] // HEAD_DIM):
        m = _head_mask(q.shape[1], h)
        sc = jnp.where(j <= i, mm_nt(q * m, kc) * scale, NEG_INF)
        sp = jnp.where(jnp.logical_and(i <= j, has_prev), mm_nt(q * m, kp) * scale, NEG_INF)
        mx = jnp.maximum(jnp.max(sc, axis=-1, keepdims=True), jnp.max(sp, axis=-1, keepdims=True))
        pc, pp = jnp.exp(sc - mx), jnp.exp(sp - mx)
        den = jnp.sum(pc, axis=-1, keepdims=True) + jnp.sum(pp, axis=-1, keepdims=True)
        o = o + (mm_nn(pc, vc) + mm_nn(pp, vp)) / den * m
        l = l + (mx + jnp.log(den)) * m
    return o, l


def _dil_specs(gi, d):
    parts = 1 if d == 1 else 2
    blk = (DIL_BLOCK * d, GROUP_COLS // parts)
    at = lambda col: (lambda p, n: (n, col * parts + p))
    before = lambda col: (lambda p, n: (jnp.maximum(n - 1, 0), col * parts + p))
    v0 = DIL_WIDTH // GROUP_COLS + gi
    q = pl.BlockSpec(blk, at(gi))
    kc, kp = pl.BlockSpec(blk, at(gi)), pl.BlockSpec(blk, before(gi))
    vc, vp = pl.BlockSpec(blk, at(v0)), pl.BlockSpec(blk, before(v0))
    out = pl.BlockSpec(blk, at(0))
    together = min(d, 2)
    return (q, kc, kp, vc, vp, out), parts, together


def _residue_rows(r, d):
    return pl.ds(r, DIL_BLOCK, stride=d) if d > 1 else pl.ds(0, DIL_BLOCK)


def dil_fwd(q, k, kv, gi, d, name):
    T = q.shape[0]
    (qs, kc, kp, vc, vp, out), parts, together = _dil_specs(gi, d)

    def body(q_ref, kc_ref, kp_ref, vc_ref, vp_ref, o_ref, l_ref):
        has_prev = pl.program_id(1) > 0

        def residues(it, carry):
            rows = [_residue_rows(it * together + a, d) for a in range(together)]
            ins = [[ref[rw, :] for ref in (q_ref, kc_ref, kp_ref, vc_ref, vp_ref)] for rw in rows]
            res = [_f_dilattn(has_prev, *x) for x in ins]
            for rw, (o, l) in zip(rows, res):
                o_ref[rw, :] = o
                l_ref[rw, :] = l
            return carry

        lax.fori_loop(0, d // together, residues, 0)

    shape = jax.ShapeDtypeStruct((T, 4 * HEAD_DIM), F32)
    return pl.pallas_call(
        body, grid=(parts, T // (DIL_BLOCK * d)), in_specs=[qs, kc, kp, vc, vp], out_specs=[out, out], out_shape=[shape, shape],
        compiler_params=_cparams(("parallel", "parallel")), name=name)(q, k, k, kv, kv)


def dil_bwd(q, k, kv, do, dl, gi, d, name):
    T = q.shape[0]
    (qs, kc, kp, vc, vp, out), parts, together = _dil_specs(gi, d)

    def body(q_ref, kc_ref, kp_ref, vc_ref, vp_ref, do_ref, dl_ref, *outs):
        f = functools.partial(_f_dilattn, pl.program_id(1) > 0)

        def residues(it, carry):
            rows = [_residue_rows(it * together + a, d) for a in range(together)]
            ins = [[ref[rw, :] for ref in (q_ref, kc_ref, kp_ref, vc_ref, vp_ref, do_ref, dl_ref)] for rw in rows]
            res = [jax.vjp(f, *x[:5])[1]((x[5], x[6])) for x in ins]
            for rw, gs in zip(rows, res):
                for o_ref, g in zip(outs, gs):
                    o_ref[rw, :] = g
            return carry

        lax.fori_loop(0, d // together, residues, 0)

    shape = jax.ShapeDtypeStruct((T, 4 * HEAD_DIM), F32)
    dq, dkc, dkp, dvc, dvp = pl.pallas_call(
        body, grid=(parts, T // (DIL_BLOCK * d)), in_specs=[qs, kc, kp, vc, vp, out, out], out_specs=[out] * 5, out_shape=[shape] * 5,
        compiler_params=_cparams(("parallel", "parallel")), name=name)(q, k, k, kv, kv, do, dl)

    def own_plus_next(c, p):
        return c + jnp.concatenate([p[DIL_BLOCK * d:], jnp.zeros_like(p[:DIL_BLOCK * d])], axis=0)

    return dq, own_plus_next(dkc, dkp), own_plus_next(dvc, dvp)


CONV_TILE = 256


def _conv3(before, u, w, b):
    ue = jnp.concatenate([before, u], axis=0)
    s1, s2 = pltpu.roll(ue, 1, 0)[8:], pltpu.roll(ue, 2, 0)[8:]
    return b + w[0:1] * s2 + w[1:2] * s1 + w[2:3] * u, s1, s2


def _conv_halves(u_ref, h_ref, cw_ref, cb_ref):
    F = D_FF
    res = []
    for lo in (0, F):
        before = jnp.where(pl.program_id(0) > 0, h_ref[:, lo:lo + F], 0.0)
        u = u_ref[:, lo:lo + F]
        res.append((u,) + _conv3(before, u, cw_ref[:, lo:lo + F], cb_ref[:, lo:lo + F]))
    return res


def _halo_before(C):
    return pl.BlockSpec((8, C), lambda i: (jnp.maximum(i * (CONV_TILE // 8) - 1, 0), 0))


def convgate_fwd(u, cw, cb, name):
    T, C = u.shape
    F = C // 2

    def body(u_ref, h_ref, cw_ref, cb_ref, z_ref):
        (_, cg, _, _), (_, cv, _, _) = _conv_halves(u_ref, h_ref, cw_ref, cb_ref)
        z_ref[...] = (cg * _sigmoid(cg) * cv).astype(BF16)

    return pl.pallas_call(
        body, grid=(T // CONV_TILE,),
        in_specs=[pl.BlockSpec((CONV_TILE, C), lambda i: (i, 0)), _halo_before(C), _full_spec(cw), _full_spec(cb)],
        out_specs=pl.BlockSpec((CONV_TILE, F), lambda i: (i, 0)), out_shape=jax.ShapeDtypeStruct((T, F), BF16),
        compiler_params=_cparams(("parallel",)), name=name)(u, u, cw, cb)


def convgate_bwd(u, cw, cb, dz, name):
    T, C = u.shape
    F = C // 2
    n = T // CONV_TILE
    E = CONV_TILE + 8

    def body(u_ref, hb_ref, ha_ref, cw_ref, cb_ref, dz_ref, dza_ref, du_ref, dcw_ref, dcb_ref):
        i = pl.program_id(0)
        dze = jnp.concatenate([dz_ref[...], jnp.where(i < n - 1, dza_ref[...], 0.0)], axis=0)

        @pl.when(i == 0)
        def _():
            dcw_ref[...] = jnp.zeros_like(dcw_ref)
            dcb_ref[...] = jnp.zeros_like(dcb_ref)

        halves = []
        for lo in (0, F):
            sl = slice(lo, lo + F)
            ue = jnp.concatenate([u_ref[:, sl], ha_ref[:, sl]], axis=0)
            c, s1, s2 = _conv3(jnp.where(i > 0, hb_ref[:, sl], 0.0), ue, cw_ref[:, sl], cb_ref[:, sl])
            halves.append((sl, ue, c, s1, s2))
        (_, _, cg, _, _), (_, _, cv, _, _) = halves
        sg = _sigmoid(cg)
        dcs = (dze * cv * sg * (1.0 + cg * (1.0 - sg)), dze * cg * sg)
        for (sl, ue, _, s1, s2), dc in zip(halves, dcs):
            own = lambda z: z[:CONV_TILE]
            dcb_ref[:, sl] += jnp.sum(own(dc), axis=0, keepdims=True)
            dcw_ref[0:1, sl] += jnp.sum(own(dc * s2), axis=0, keepdims=True)
            dcw_ref[1:2, sl] += jnp.sum(own(dc * s1), axis=0, keepdims=True)
            dcw_ref[2:3, sl] += jnp.sum(own(dc * ue), axis=0, keepdims=True)
            du = cw_ref[2:3, sl] * dc + cw_ref[1:2, sl] * pltpu.roll(dc, E - 1, 0) + cw_ref[0:1, sl] * pltpu.roll(dc, E - 2, 0)
            du_ref[:, sl] = own(du).astype(BF16)

    after = lambda w: pl.BlockSpec((8, w), lambda i: (jnp.minimum((i + 1) * (CONV_TILE // 8), T // 8 - 1), 0))
    return pl.pallas_call(
        body, grid=(n,),
        in_specs=[pl.BlockSpec((CONV_TILE, C), lambda i: (i, 0)), _halo_before(C), after(C), _full_spec(cw), _full_spec(cb),
                  pl.BlockSpec((CONV_TILE, F), lambda i: (i, 0)), after(F)],
        out_specs=[pl.BlockSpec((CONV_TILE, C), lambda i: (i, 0)), _full_spec(cw), _full_spec(cb)],
        out_shape=[jax.ShapeDtypeStruct((T, C), BF16), jax.ShapeDtypeStruct(cw.shape, F32), jax.ShapeDtypeStruct(cb.shape, F32)],
        compiler_params=_cparams(("arbitrary",)), name=name)(u, u, u, cw, cb, dz, dz)


def loss_head(y, tgt):
    T, D = y.shape
    tile = ROW_TILE

    def body(y_ref, t_ref, l_ref, d_ref, db_ref):
        d = y_ref[...] - t_ref[...]
        d_ref[...] = d * (1.0 / D)
        db_ref[...] = (d * (1.0 / D)).astype(BF16)

        @pl.when(pl.program_id(0) == 0)
        def _():
            l_ref[...] = jnp.zeros_like(l_ref)

        l_ref[...] += (0.5 / D) * jnp.sum(d * d)

    row = pl.BlockSpec((tile, D), lambda i: (i, 0))
    return pl.pallas_call(
        body, grid=(T // tile,), in_specs=[row, row], out_specs=[pl.BlockSpec((8, 128), lambda i: (0, 0)), row, row],
        out_shape=[jax.ShapeDtypeStruct((8, 128), F32), jax.ShapeDtypeStruct((T, D), F32), jax.ShapeDtypeStruct((T, D), BF16)],
        compiler_params=_cparams(("arbitrary",)), name="loss_head")(y, tgt)


def sum_parts(parts, name):
    S, R, C = parts.shape
    tile = _pick(R, (256, 128, 64, 32, 16, 8))

    def body(p_ref, o_ref):
        acc = p_ref[0]
        for s in range(1, S):
            acc = acc + p_ref[s]
        o_ref[...] = acc

    return pl.pallas_call(
        body, grid=(R // tile,), in_specs=[pl.BlockSpec((S, tile, C), lambda i: (0, i, 0))],
        out_specs=pl.BlockSpec((tile, C), lambda i: (i, 0)), out_shape=jax.ShapeDtypeStruct((R, C), F32),
        compiler_params=_cparams(("parallel",)), name=name)(parts)


def adamw(gparts, w, m, v, name):
    S, R, C = gparts.shape
    tile = _pick(R, (256, 128, 64, 32, 16, 8))
    c1 = 1.0 / (1.0 - ADAM_B1 ** ADAM_STEP)
    c2 = 1.0 / (1.0 - ADAM_B2 ** ADAM_STEP)

    def body(g_ref, w_ref, m_ref, v_ref, go_ref, d_ref, mo_ref, vo_ref):
        g = g_ref[0].astype(F32)
        for s in range(1, S):
            g = g + g_ref[s].astype(F32)
        m1 = ADAM_B1 * m_ref[...] + (1.0 - ADAM_B1) * g
        v1 = ADAM_B2 * v_ref[...] + (1.0 - ADAM_B2) * (g * g)
        go_ref[...] = g
        mo_ref[...] = m1
        vo_ref[...] = v1
        d_ref[...] = -ADAM_LR * ((m1 * c1) / (jnp.sqrt(v1 * c2) + ADAM_EPS) + ADAM_WD * w_ref[...])

    row = pl.BlockSpec((tile, C), lambda i: (i, 0))
    return pl.pallas_call(
        body, grid=(R // tile,), in_specs=[pl.BlockSpec((S, tile, C), lambda i: (0, i, 0)), row, row, row],
        out_specs=[row] * 4, out_shape=[jax.ShapeDtypeStruct((R, C), F32)] * 4,
        compiler_params=_cparams(("parallel",)), name=name)(gparts, w, m, v)


def _peers():
    x, y, c = lax.axis_index("x"), lax.axis_index("y"), lax.axis_index("c")
    peers = []
    for k in range(1, N_DEV):
        px = 1 - x if k & 4 else x
        py = 1 - y if k & 2 else y
        pc = 1 - c if k & 1 else c
        peers.append(((px, py, pc), 4 * px + 2 * py + pc))
    return 4 * x + 2 * y + c, peers


_ANY = pl.BlockSpec(memory_space=pl.ANY)


class Exchange:
    def __init__(self, gathers=(), scatters=()):
        self.gathers, self.scatters = list(gathers), list(scatters)
        self.n = len(self.gathers) + len(self.scatters)

    def operands(self):
        return self.gathers + self.scatters

    def out_shape(self):
        return ([jax.ShapeDtypeStruct((N_DEV,) + x.shape, x.dtype) for x in self.gathers]
                + [jax.ShapeDtypeStruct(x.shape, x.dtype) for x in self.scatters])

    def scratch(self):
        n = max(self.n, 1)
        return [pltpu.SemaphoreType.DMA((7 * n,)), pltpu.SemaphoreType.DMA((7 * n,)), pltpu.SemaphoreType.DMA((n,))]

    def _copies(self, in_refs, out_refs, send_sems, recv_sems, local_sems):
        me, peers = _peers()
        ng = len(self.gathers)
        local, sends, recvs = [], [], []
        for a in range(self.n):
            x, o = in_refs[a], out_refs[a]
            mine = x if a < ng else x.at[me]
            local.append(pltpu.make_async_copy(mine, o.at[me], local_sems.at[a]))
            s_a, r_a = {}, {}
            for k in range(1, N_DEV):
                peer, slot = peers[k - 1]
                sems = dict(send_sem=send_sems.at[7 * a + k - 1], recv_sem=recv_sems.at[7 * a + k - 1],
                            device_id_type=pl.DeviceIdType.MESH)
                if a >= ng:
                    s_a[k] = pltpu.make_async_remote_copy(src_ref=x.at[slot], dst_ref=o.at[me], device_id=peer, **sems)
                elif k in FORWARDED:
                    came = o.at[peers[k - 2][1]]
                    s_a[k] = pltpu.make_async_remote_copy(src_ref=came, dst_ref=came, device_id=peers[0][0], **sems)
                else:
                    s_a[k] = pltpu.make_async_remote_copy(src_ref=x, dst_ref=o.at[me], device_id=peer, **sems)
                r_a[k] = pltpu.make_async_remote_copy(src_ref=mine, dst_ref=o.at[slot], device_id=peer, **sems)
            sends.append(s_a)
            recvs.append(r_a)
        return local, sends, recvs

    def start(self, *refs):
        if self.n == 0:
            return
        local, sends, _ = self._copies(*refs)
        for a in range(self.n):
            local[a].start()
            for k in range(1, N_DEV):
                if a >= len(self.gathers) or k not in FORWARDED:
                    sends[a][k].start()

    def forward(self, *refs):
        if not self.gathers:
            return
        _, sends, recvs = self._copies(*refs)
        for a in range(len(self.gathers)):
            for k in FORWARDED:
                recvs[a][k - 1].wait_recv()
                sends[a][k].start()

    def wait(self, *refs):
        if self.n == 0:
            return
        local, sends, recvs = self._copies(*refs)
        for a in range(self.n):
            waited_early = [f - 1 for f in FORWARDED] if a < len(self.gathers) else []
            for k in range(1, N_DEV):
                if k not in waited_early:
                    recvs[a][k].wait_recv()
            for k in range(1, N_DEV):
                sends[a][k].wait_send()
            local[a].wait()


FORWARDED = (3, 5, 7)


def exchange(ex, name):
    n = ex.n

    def body(*refs):
        args = (refs[:n], refs[n:2 * n]) + tuple(refs[2 * n:])
        ex.start(*args)
        ex.forward(*args)
        ex.wait(*args)

    return pl.pallas_call(body, in_specs=[_ANY] * n, out_specs=[_ANY] * n, out_shape=ex.out_shape(),
                          scratch_shapes=ex.scratch(), name=name)(*ex.operands())


def _shift_up(z):
    return jnp.concatenate([z[1:], jnp.zeros_like(z[:1])], axis=0)


def _segments(width):
    seg = np.zeros((width, 128), np.float32)
    seg[np.arange(width), np.arange(width) // HEAD_DIM] = 1.0
    return jnp.asarray(seg), jnp.asarray(seg.T)


def _rope_consts(T):
    inv = ROPE_THETA ** (-jnp.arange(0, HEAD_DIM, 2, dtype=F32) / HEAD_DIM)
    ang = jnp.arange(T, dtype=F32)[:, None] * inv[None, :]
    return jnp.tile(jnp.cos(ang), (1, 4)), jnp.tile(jnp.sin(ang), (1, 4))


def _per_head(g, heads):
    return jnp.tile(g.reshape(1, HEAD_DIM), (1, heads))


def _sum_heads(g):
    return g.reshape(-1, HEAD_DIM).sum(axis=0, keepdims=True)


LORA_COLS = 256
RW_TILE = 256
ROW_TILE = 512


def _local_step(x0, memx, tgt, P, ex_weights=None, weights_done=None, ex_grads=None, ex_last=None):
    T = x0.shape[0]
    P = dict(P)
    G = {}
    seg, seg_t = _segments(RWKV_WIDTH)
    mseg = (seg[:MEM_WIDTH], seg_t[:, :MEM_WIDTH])
    cos, sin = _rope_consts(T)
    row = lambda v: v.reshape(1, -1)

    def mem_fwd(i, q, into):
        memn = stage_fwd(f_rmsnorm, [memx], [P["mem_norm"][i:i + 1]], [], [], N_MEM, f"mem{i}_norm", [BF16])[0]
        kvm = matmul(memn, P["mem_w_kv"][i], "nn", f"mem{i}_kv")
        kn, qn = _per_head(P["mem_k_norm"][i], MEM_HEADS), _per_head(P["mem_q_norm"][i], MEM_HEADS)
        km = stage_fwd(f_headnorm, [Cols(kvm, MEM_WIDTH, 0)], [kn], [], mseg, N_MEM, f"mem{i}_knorm")[0]
        om = stage_fwd(f_memattn, [q], [km, Cols(kvm, MEM_WIDTH, 1), qn], [], mseg, ROW_TILE, f"mem{i}_attn", [BF16], into=into)[0]
        return om, (memn, kvm, km, kn, qn, q)

    def mem_bwd(i, saved, dymem):
        memn, kvm, km, kn, qn, q = saved
        (dq,), (dkm, dvm, g_qn) = stage_bwd(f_memattn, [q], [km, Cols(kvm, MEM_WIDTH, 1), qn], [], mseg, [dymem], ROW_TILE,
                                            f"mem{i}_attn_bwd")
        (dkraw,), (g_kn,) = stage_bwd(f_headnorm, [Cols(kvm, MEM_WIDTH, 0)], [kn], [], mseg, [dkm], N_MEM, f"mem{i}_knorm_bwd")
        dkvm = jnp.concatenate([dkraw, dvm], axis=1).astype(BF16)
        g_w = matmul(memn, dkvm, "tn", f"mem{i}_kv_dw")
        dmemn = matmul(dkvm, P["mem_w_kv"][i], "nt", f"mem{i}_kv_dx")
        _, (g_mn,) = stage_bwd(f_rmsnorm, [memx], [P["mem_norm"][i:i + 1]], [], [], [dmemn], N_MEM, f"mem{i}_norm_bwd")
        return dq, g_mn, g_w, _sum_heads(g_qn), _sum_heads(g_kn)

    def ffn_fwd(i, xin):
        hn = stage_fwd(f_rmsnorm, [xin], [P["ffn_norm"][i:i + 1]], [], [], ROW_TILE, f"ffn{i}_norm", [BF16])[0]
        u = matmul(hn, P["ffn_w_up"][i], "nn", f"ffn{i}_up")
        z = convgate_fwd(u, P["ffn_conv_w"][i], P["ffn_conv_b"][i:i + 1], f"ffn{i}_conv")
        return matmul(z, P["ffn_w_down"][i], "nn", f"ffn{i}_down", residual=xin), (hn, u, z)

    def ffn_bwd(i, xin, saved, dxo, dxo_b):
        hn, u, z = saved
        dz = matmul(dxo_b, P["ffn_w_down"][i], "nt", f"ffn{i}_down_dx")
        g_down = matmul(z, dxo_b, "tn", f"ffn{i}_down_dw")
        du, g_cw, g_cb = convgate_bwd(u, P["ffn_conv_w"][i], P["ffn_conv_b"][i:i + 1], dz, f"ffn{i}_conv_bwd")
        dhn = matmul(du, P["ffn_w_up"][i], "nt", f"ffn{i}_up_dx")
        g_up = matmul(hn, du, "tn", f"ffn{i}_up_dw")
        (dxin,), (g_n,), (dxin_b,) = stage_bwd(f_rmsnorm_res, [xin], [P["ffn_norm"][i:i + 1]], [], [], [dhn, dxo], ROW_TILE,
                                               f"ffn{i}_norm_bwd", bf16_copies=(0,))
        return dxin, dxin_b, g_n, g_up, g_cw, g_cb, g_down

    h0 = stage_fwd(f_rmsnorm, [x0], [P["attn_norm"][0:1]], [], [], ROW_TILE, "l0_norm", [BF16])[0]
    p0 = matmul(h0, P["a_w_in"][0], "nn", "l0_in")
    lora0 = 3 * RWKV_WIDTH // LORA_COLS
    pre_xs = [Cols(p0, RWKV_WIDTH, 0), Cols(p0, RWKV_WIDTH, 1), Cols(p0, RWKV_WIDTH, 2), Cols(p0, LORA_COLS, lora0)]
    mu = [Cols(P["a_mu"], RWKV_WIDTH, 0), Cols(P["a_mu"], RWKV_WIDTH, 1), Cols(P["a_mu"], RWKV_WIDTH, 2),
          Cols(P["a_mu"], LORA_COLS, lora0)]
    lora_rows = lambda w, lo: jnp.pad(w, ((lo, LORA_COLS - lo - w.shape[0]), (0, 0)))
    pre_ps = mu + [P["a_w0"], lora_rows(P["a_w2"][0], 0), P["a_a0"], lora_rows(P["a_a2"][0], 64), lora_rows(P["a_g2"][0], 128),
                   P["a_k_k"], P["a_k_a"]]
    r, lw, k2, v, kk, b, g = stage_fwd(f_rwkv_pre, pre_xs, pre_ps, [], [seg, seg_t], RW_TILE, "l0_rwkv_pre", with_prev=True)
    scan_in = [r, lw, k2, v, kk, b]
    y_h, h_states, got = rwkv_scan_fwd(*scan_in, ex_weights or Exchange())
    if weights_done is not None:
        P.update(weights_done(got))
    y_s = y_h
    post_ps = [P["a_lnx_w"], P["a_lnx_b"], P["a_r_k"].reshape(1, RWKV_WIDTH)]
    ycat0, mem0_saved = mem_fwd(0, Cols(p0, MEM_WIDTH, SHIFT_WIDTH // MEM_WIDTH), Into(None, D_MODEL, RWKV_WIDTH // MEM_WIDTH))
    ycat0 = stage_fwd(f_rwkv_post, [y_s, r, k2, v, g], post_ps, [], [seg, seg_t], RW_TILE, "l0_rwkv_post", [BF16],
                      into=Into(ycat0, D_MODEL, 0))[0]
    x1 = matmul(ycat0, P["a_w_out"][0], "nn", "l0_out", residual=x0)
    x2, ffn0_saved = ffn_fwd(0, x1)

    hk, h1 = stage_fwd(f_rmsnorm2, [x2], [row(P["kv_norm"]), P["attn_norm"][1:2]], [], [], ROW_TILE, "l1_norm", [BF16, BF16])
    kvp = matmul(hk, P["kv_w"][0], "nn", "l1_kv")
    p1 = matmul(h1, P["b_w_in"][0], "nn", "l1_in")
    kraw, qraw = Cols(kvp, DIL_WIDTH, 0), Cols(p1, DIL_WIDTH, 0)
    kgain, qgain = _per_head(P["kv_k_norm"], DIL_WIDTH // HEAD_DIM), _per_head(P["b_q_norm"], DIL_WIDTH // HEAD_DIM)
    ksh = stage_fwd(f_qkprep, [kraw], [kgain], [cos, sin], [seg, seg_t], ROW_TILE, "l1_kprep")[0]
    q = stage_fwd(f_qkprep, [qraw], [qgain], [cos, sin], [seg, seg_t], ROW_TILE, "l1_qprep")[0]
    outs, lses = [], []
    for gi, (_, d) in enumerate(DIL_GROUPS):
        og, lg = dil_fwd(q, ksh, kvp, gi, d, f"l1_dil{gi}")
        outs.append(og)
        lses.append(lg)
    ycat1 = stage_fwd(f_mix, outs + lses, [], [], [], ROW_TILE, "l1_mix", [BF16], into=Into(None, 2 * MEM_WIDTH, 0))[0]
    ycat1, mem1_saved = mem_fwd(1, Cols(p1, MEM_WIDTH, DIL_WIDTH // MEM_WIDTH), Into(ycat1, 2 * MEM_WIDTH, 1))
    x3 = matmul(ycat1, P["b_w_out"][0], "nn", "l1_out", residual=x2)
    x4, ffn1_saved = ffn_fwd(1, x3)
    loss_part, dx4, dx4_b = loss_head(x4, tgt)

    dx3, dx3_b, gn1, gup1, gcw1, gcb1, gdown1 = ffn_bwd(1, x3, ffn1_saved, dx4, dx4_b)
    dycat1 = matmul(dx3_b, P["b_w_out"][0], "nt", "l1_out_dx")
    G["b_w_out"] = [matmul(ycat1, dx3_b, "tn", "l1_out_dw")]
    dqmem1, gmn1, gmw1, gmq1, gmk1 = mem_bwd(1, mem1_saved, Cols(dycat1, MEM_WIDTH, 1))
    dmix, _ = stage_bwd(f_mix, outs + lses, [], [], [], [Cols(dycat1, MEM_WIDTH, 0)], ROW_TILE, "l1_mix_bwd")
    dq, dk, dv = zip(*[dil_bwd(q, ksh, kvp, dmix[gi], dmix[3 + gi], gi, d, f"l1_dil{gi}_bwd")
                       for gi, (_, d) in enumerate(DIL_GROUPS)])
    dq, dk, dv = jnp.concatenate(dq, axis=1), jnp.concatenate(dk, axis=1), jnp.concatenate(dv, axis=1)
    (dqraw,), (g_bq,) = stage_bwd(f_qkprep, [qraw], [qgain], [cos, sin], [seg, seg_t], [dq], ROW_TILE, "l1_qprep_bwd")
    (dkraw,), (g_kk,) = stage_bwd(f_qkprep, [kraw], [kgain], [cos, sin], [seg, seg_t], [dk], ROW_TILE, "l1_kprep_bwd")
    g_bq, g_kk = _sum_heads(g_bq), _sum_heads(g_kk)
    dp1 = jnp.concatenate([dqraw, dqmem1], axis=1).astype(BF16)
    dkvp = jnp.concatenate([dkraw, dv], axis=1).astype(BF16)
    dh1 = matmul(dp1, P["b_w_in"][0], "nt", "l1_in_dx")
    G["b_w_in"] = [matmul(h1, dp1, "tn", "l1_in_dw")]
    dhk = matmul(dkvp, P["kv_w"][0], "nt", "l1_kv_dx")
    G["kv_w"] = [matmul(hk, dkvp, "tn", "l1_kv_dw")]
    (dx2,), (g_kvn, g_an1), (dx2_b,) = stage_bwd(f_rmsnorm2_res, [x2], [row(P["kv_norm"]), P["attn_norm"][1:2]], [], [],
                                                 [dhk, dh1, dx3], ROW_TILE, "l1_norm_bwd", bf16_copies=(0,))

    dx1, dx1_b, gn0, gup0, gcw0, gcb0, gdown0 = ffn_bwd(0, x1, ffn0_saved, dx2, dx2_b)
    dycat0 = matmul(dx1_b, P["a_w_out"][0], "nt", "l0_out_dx")
    G["a_w_out"] = [matmul(ycat0, dx1_b, "tn", "l0_out_dw")]
    dqmem0, gmn0, gmw0, gmq0, gmk0 = mem_bwd(0, mem0_saved, Cols(dycat0, MEM_WIDTH, RWKV_WIDTH // MEM_WIDTH))
    (dy_s, dr_a, dk_a, dv_a, dg), (g_lw, g_lb, g_rk) = stage_bwd(
        f_rwkv_post, [y_s, r, k2, v, g], post_ps, [], [seg, seg_t], [Cols(dycat0, RWKV_WIDTH, 0)], RW_TILE, "l0_rwkv_post_bwd")
    G["mem_w_kv"], G["ffn_w_up"], G["ffn_w_down"] = [gmw0, gmw1], [gup0, gup1], [gdown0, gdown1]
    (dr_b, dlw, dk_b, dv_b, dkk, db), G["_exchanged"] = rwkv_scan_bwd(*scan_in, h_states, dy_s,
                                                                      ex_grads(G) if ex_grads else Exchange())
    dpre, gpre = stage_bwd(f_rwkv_pre, pre_xs, pre_ps, [], [seg, seg_t],
                           [[dr_a, dr_b], dlw, [dk_a, dk_b], [dv_a, dv_b], dkk, db, dg], RW_TILE, "l0_rwkv_pre_bwd", with_prev=True)
    dp_rw = jnp.concatenate(dpre[:4], axis=1) + _shift_up(jnp.concatenate(dpre[4:], axis=1))
    dp0 = jnp.concatenate([dp_rw, dqmem0], axis=1).astype(BF16)
    G["a_w_in"] = [matmul(h0, dp0, "tn", "l0_in_dw")]
    if ex_last is None:
        dh0 = matmul(dp0, P["a_w_in"][0], "nt", "l0_in_dx")
    else:
        dh0, G["_exchanged_last"] = matmul(dp0, P["a_w_in"][0], "nt", "l0_in_dx", ex=ex_last(G))
    (dx0,), (g_an0,) = stage_bwd(f_rmsnorm_res, [x0], [P["attn_norm"][0:1]], [], [], [dh0, dx1], ROW_TILE, "l0_norm_bwd")

    G["attn_norm"] = jnp.concatenate([g_an0, g_an1], axis=0)
    G["a_mu"] = jnp.concatenate(gpre[:4], axis=1)
    G["a_w0"], G["a_w2"], G["a_a0"], G["a_a2"], G["a_g2"] = gpre[4], gpre[5][None, :64], gpre[6], gpre[7][None, 64:128], gpre[8][None, 128:]
    G["a_k_k"], G["a_k_a"] = gpre[9], gpre[10]
    G["a_r_k"] = g_rk.reshape(1, RWKV_HEADS, HEAD_DIM)
    G["a_lnx_w"], G["a_lnx_b"] = g_lw, g_lb
    G["kv_norm"], G["kv_k_norm"], G["b_q_norm"] = g_kvn.reshape(-1), g_kk.reshape(-1), g_bq
    G["mem_norm"] = jnp.concatenate([gmn0, gmn1], axis=0)
    G["mem_w_kv"] = [gmw0, gmw1]
    G["mem_q_norm"] = jnp.concatenate([gmq0, gmq1], axis=0)
    G["mem_k_norm"] = jnp.concatenate([gmk0, gmk1], axis=0)
    G["ffn_norm"] = jnp.concatenate([gn0, gn1], axis=0)
    G["ffn_w_up"] = [gup0, gup1]
    G["ffn_conv_w"] = jnp.stack([gcw0, gcw1])
    G["ffn_conv_b"] = jnp.concatenate([gcb0, gcb1], axis=0)
    G["ffn_w_down"] = [gdown0, gdown1]
    return loss_part, dx0, G


PARAMS = (("attn_norm", None), ("a_w_in", 2), ("a_mu", 1), ("a_w0", 1), ("a_w2", 2), ("a_a0", 1), ("a_a2", 2), ("a_g2", 2),
          ("a_k_k", 1), ("a_k_a", 1), ("a_r_k", None), ("a_lnx_w", 1), ("a_lnx_b", 1), ("a_w_out", 1), ("kv_norm", None),
          ("kv_w", 1), ("kv_k_norm", None), ("b_w_in", 1), ("b_q_norm", None), ("b_w_out", 2), ("mem_norm", None),
          ("mem_w_kv", 1), ("mem_q_norm", None), ("mem_k_norm", None), ("ffn_norm", None), ("ffn_w_up", 2),
          ("ffn_conv_w", 2), ("ffn_conv_b", None), ("ffn_w_down", 1))
BIG = ("a_w_in", "a_w_out", "kv_w", "b_w_in", "b_w_out", "mem_w_kv", "ffn_w_up", "ffn_w_down")
AXIS = dict(PARAMS)
SMALL = tuple(n for n, _ in PARAMS if n not in BIG)
SMALL_SHARDED = tuple(n for n in SMALL if AXIS[n] is not None)
PACK_QUANTUM = 256 * 128


def _from_shards(xs, axis):
    full = jnp.moveaxis(xs, 0, axis)
    sh = full.shape
    return full.reshape(sh[:axis] + (sh[axis] * sh[axis + 1],) + sh[axis + 2:])


def _to_shards(g, axis):
    sh = g.shape
    return jnp.moveaxis(g.reshape(sh[:axis] + (N_DEV, sh[axis] // N_DEV) + sh[axis + 1:]), axis, 0)


def _pack(parts, lead=0):
    ld = parts[0].shape[:lead]
    flat = jnp.concatenate([p.reshape(ld + (-1,)) for p in parts], axis=-1)
    pad = (-flat.shape[-1]) % PACK_QUANTUM
    flat = jnp.pad(flat, [(0, 0)] * lead + [(0, pad)])
    return flat.reshape(ld + (-1, 128))


def _unpack(packed, shapes, lead=0):
    ld = packed.shape[:lead]
    flat = packed.reshape(ld + (-1,))
    out, off = [], 0
    for s in shapes:
        n = math.prod(s)
        out.append(flat[..., off:off + n].reshape(ld + tuple(s)))
        off += n
    return out


def kernel(x, mem, attn_norm, a_w_in, a_mu, a_w0, a_w2, a_a0, a_a2, a_g2, a_k_k, a_k_a, a_r_k, a_lnx_w, a_lnx_b, a_w_out, kv_norm, kv_w, kv_k_norm, b_w_in, b_q_norm, b_w_out, mem_norm, mem_w_kv, mem_q_norm, mem_k_norm, ffn_norm, ffn_w_up, ffn_conv_w, ffn_conv_b, ffn_w_down, loss_target, m_attn_norm, m_a_w_in, m_a_mu, m_a_w0, m_a_w2, m_a_a0, m_a_a2, m_a_g2, m_a_k_k, m_a_k_a, m_a_r_k, m_a_lnx_w, m_a_lnx_b, m_a_w_out, m_kv_norm, m_kv_w, m_kv_k_norm, m_b_w_in, m_b_q_norm, m_b_w_out, m_mem_norm, m_mem_w_kv, m_mem_q_norm, m_mem_k_norm, m_ffn_norm, m_ffn_w_up, m_ffn_conv_w, m_ffn_conv_b, m_ffn_w_down, v_attn_norm, v_a_w_in, v_a_mu, v_a_w0, v_a_w2, v_a_a0, v_a_a2, v_a_g2, v_a_k_k, v_a_k_a, v_a_r_k, v_a_lnx_w, v_a_lnx_b, v_a_w_out, v_kv_norm, v_kv_w, v_kv_k_norm, v_b_w_in, v_b_q_norm, v_b_w_out, v_mem_norm, v_mem_w_kv, v_mem_q_norm, v_mem_k_norm, v_ffn_norm, v_ffn_w_up, v_ffn_conv_w, v_ffn_conv_b, v_ffn_w_down):
    names = [n for n, _ in PARAMS]
    vals = (attn_norm, a_w_in, a_mu, a_w0, a_w2, a_a0, a_a2, a_g2, a_k_k, a_k_a, a_r_k, a_lnx_w, a_lnx_b, a_w_out, kv_norm, kv_w, kv_k_norm, b_w_in, b_q_norm, b_w_out, mem_norm, mem_w_kv, mem_q_norm, mem_k_norm, ffn_norm, ffn_w_up, ffn_conv_w, ffn_conv_b, ffn_w_down)
    m_vals = (m_attn_norm, m_a_w_in, m_a_mu, m_a_w0, m_a_w2, m_a_a0, m_a_a2, m_a_g2, m_a_k_k, m_a_k_a, m_a_r_k, m_a_lnx_w, m_a_lnx_b, m_a_w_out, m_kv_norm, m_kv_w, m_kv_k_norm, m_b_w_in, m_b_q_norm, m_b_w_out, m_mem_norm, m_mem_w_kv, m_mem_q_norm, m_mem_k_norm, m_ffn_norm, m_ffn_w_up, m_ffn_conv_w, m_ffn_conv_b, m_ffn_w_down)
    v_vals = (v_attn_norm, v_a_w_in, v_a_mu, v_a_w0, v_a_w2, v_a_a0, v_a_a2, v_a_g2, v_a_k_k, v_a_k_a, v_a_r_k, v_a_lnx_w, v_a_lnx_b, v_a_w_out, v_kv_norm, v_kv_w, v_kv_k_norm, v_b_w_in, v_b_q_norm, v_b_w_out, v_mem_norm, v_mem_w_kv, v_mem_q_norm, v_mem_k_norm, v_ffn_norm, v_ffn_w_up, v_ffn_conv_w, v_ffn_conv_b, v_ffn_w_down)
    W, M, V = dict(zip(names, vals)), dict(zip(names, m_vals)), dict(zip(names, v_vals))
    layers = lambda D, n: [D[n]] if D[n].ndim == 2 else [D[n][i] for i in range(D[n].shape[0])]
    ax2 = lambda n: AXIS[n] - (W[n].ndim - 2)
    later =[(n, i) for n in BIG if n != "a_w_in" for i in range(len(layers(W, n)))]

    small_shapes = [W[n].shape for n in SMALL_SHARDED]
    got_w, got_small = exchange(Exchange(gathers=[W["a_w_in"][0].astype(BF16), _pack([W[n] for n in SMALL_SHARDED])]),
                                "gather_first")
    P = {n: W[n] for n in SMALL}
    P["a_w_in"] = [_from_shards(got_w, ax2("a_w_in"))]
    for n, s in zip(SMALL_SHARDED, _unpack(got_small, small_shapes, lead=1)):
        P[n] = _from_shards(s, AXIS[n])
    ex_weights = Exchange(gathers=[layers(W, n)[i].astype(BF16) for n, i in later])

    def weights_done(got):
        out = {}
        for (n, _), g in zip(later, got):
            out.setdefault(n, []).append(_from_shards(g, ax2(n)))
        return out

    slots = lambda G, n: jnp.stack([_to_shards(g, ax2(n)) for g in G[n]], axis=1)
    later_names = [n for n in BIG if n != "a_w_in"]
    ex_grads = lambda G: Exchange(scatters=[slots(G, n) for n in later_names])
    ex_last = lambda G: Exchange(scatters=[slots(G, "a_w_in")])
    loss_part, dx0, G = _local_step(x[0], mem[0], loss_target[0], P, ex_weights, weights_done, ex_grads, ex_last)
    gparts = dict(zip(later_names, G.pop("_exchanged")))
    (gparts["a_w_in"],) = G.pop("_exchanged_last")
    replicated = [n for n in SMALL if AXIS[n] is None]
    small_slots = _pack([_to_shards(G[n], AXIS[n]) for n in SMALL_SHARDED], lead=1)
    got_rep, got_sharded = exchange(
        Exchange(gathers=[_pack([G[n] for n in replicated] + [loss_part[0:1, 0:1]])], scatters=[small_slots]), "exchange_last")

    results = {}
    for n in BIG:
        rows = lambda z: z.reshape((-1,) + z.shape[-1:])
        res = adamw(gparts[n].reshape((N_DEV, -1) + gparts[n].shape[-1:]), rows(W[n]), rows(M[n]), rows(V[n]), f"adamw_{n}")
        results[n] = [r.reshape(W[n].shape) for r in res]
    *rep_sums, loss = _unpack(sum_parts(got_rep, "sum_replicated_grads"), [W[n].shape for n in replicated] + [()])
    g_mine = dict(zip(replicated, rep_sums))
    g_mine.update(zip(SMALL_SHARDED, _unpack(sum_parts(got_sharded, "sum_small_sharded_grads"), [W[n].shape for n in SMALL_SHARDED])))
    res = adamw(_pack([g_mine[n] for n in SMALL])[None], _pack([W[n] for n in SMALL]), _pack([M[n] for n in SMALL]),
                _pack([V[n] for n in SMALL]), "adamw_small")
    for n, parts in zip(SMALL, zip(*[_unpack(r, [W[n].shape for n in SMALL]) for r in res])):
        results[n] = list(parts)
    outs = [[results[n][j] for n in names] for j in range(4)]
    return (loss, dx0[None], *outs[0], *outs[1], *outs[2], *outs[3])
```

```python
import functools
import math

import jax
import jax.numpy as jnp
import numpy as np
from jax import lax
from jax.experimental import pallas as pl
from jax.experimental.pallas import tpu as pltpu

F32 = jnp.float32
BF16 = jnp.bfloat16
H3 =lax.Precision.HIGH

N_DEV = 8
D_MODEL = 1024
HEAD_DIM = 64
N_MEM = 256
MEM_HEADS = 4
MEM_WIDTH = 256
RWKV_HEADS = 12
RWKV_WIDTH = 768
SHIFT_WIDTH = 2560
DIL_GROUPS = ((128, 1), (512, 4), (2048, 16))
DIL_BLOCK = 128
DIL_WIDTH = 768
D_FF = 2816
RMS_EPS = 1e-6
LNX_EPS = 64e-5
NEG_INF = -1e30
ROPE_THETA = 10000.0
ADAM_LR, ADAM_B1, ADAM_B2, ADAM_EPS, ADAM_WD, ADAM_STEP = 0.001, 0.9, 0.999, 1e-08, 0.01, 10

CHUNK = 64
SCAN_GROUPS_FWD, SCAN_GROUPS_BWD = 1, 1
MM_TILE_CAP = 1408
VMEM_LIMIT_V7X = 48 * 1024 * 1024


def _cparams(sem):
    return pltpu.CompilerParams(dimension_semantics=sem, vmem_limit_bytes=VMEM_LIMIT_V7X)


def _pick(n, cands):
    for c in cands:
        if n % c == 0:
            return c
    return n


def _tile(n, cap):
    if n <= cap:
        return n
    for d in range(cap - cap % 128, 0, -128):
        if n % d == 0:
            return d
    return n


def _dg(a, b, ca, cb, batch):
    dims = (((ca,), (cb,)), ((0,), (0,))) if batch else (((ca,), (cb,)), ((), ()))
    return lax.dot_general(a.astype(BF16), b.astype(BF16), dims, preferred_element_type=F32)


@jax.custom_vjp
def mm_nn(a, b):
    n = a.ndim
    return _dg(a, b, n - 1, n - 2, n == 3)


def _mm_nn_fwd(a, b):
    return mm_nn(a, b), (a, b)


def _mm_nn_bwd(res, g):
    a, b = res
    n = a.ndim
    return _dg(g, b, n - 1, n - 1, n == 3), _dg(a, g, n - 2, n - 2, n == 3)


mm_nn.defvjp(_mm_nn_fwd, _mm_nn_bwd)


@jax.custom_vjp
def mm_nt(a, b):
    n = a.ndim
    return _dg(a, b, n - 1, n - 1, n == 3)


def _mm_nt_fwd(a, b):
    return mm_nt(a, b), (a, b)


def _mm_nt_bwd(res, g):
    a, b = res
    n = a.ndim
    return _dg(g, b, n - 1, n - 2, n == 3), _dg(g, a, n - 2, n - 2, n == 3)


mm_nt.defvjp(_mm_nt_fwd, _mm_nt_bwd)


def mmh(a, b):
    n = a.ndim
    dims = (((n - 1,), (n - 2,)), ((0,), (0,))) if n == 3 else (((1,), (0,)), ((), ()))
    return lax.dot_general(a, b, dims, precision=H3, preferred_element_type=F32)


def mmh_nt(a, b):
    n = a.ndim
    dims = (((n - 1,), (n - 1,)), ((0,), (0,))) if n == 3 else (((1,), (1,)), ((), ()))
    return lax.dot_general(a, b, dims, precision=H3, preferred_element_type=F32)


def mmh_tn(a, b):
    n = a.ndim
    dims = (((n - 2,), (n - 2,)), ((0,), (0,))) if n == 3 else (((0,), (0,)), ((), ()))
    return lax.dot_general(a, b, dims, precision=H3, preferred_element_type=F32)


def matmul(a, b, mode, name, residual=None):
    out_dtype = BF16 if mode == "tn" else F32
    if mode == "nn":
        (M, K), (_, N) = a.shape, b.shape
    elif mode == "nt":
        (M, K), (N, _) = a.shape, b.shape
    else:
        (K, M), (_, N) = a.shape, b.shape
    tm = _tile(M, 2048 if mode == "nn" else MM_TILE_CAP)
    tn = _tile(N, 512 if mode == "nn" else MM_TILE_CAP)
    tk = _tile(K, MM_TILE_CAP)
    nk = K // tk
    if mode == "nn":
        a_spec = pl.BlockSpec((tm, tk), lambda i, j, k: (i, k))
        b_spec = pl.BlockSpec((tk, tn), lambda i, j, k: (k, j))
        dims = (((1,), (0,)), ((), ()))
    elif mode == "nt":
        a_spec = pl.BlockSpec((tm, tk), lambda i, j, k: (i, k))
        b_spec = pl.BlockSpec((tn, tk), lambda i, j, k: (j, k))
        dims = (((1,), (1,)), ((), ()))
    else:
        a_spec = pl.BlockSpec((tk, tm), lambda i, j, k: (k, i))
        b_spec = pl.BlockSpec((tk, tn), lambda i, j, k: (k, j))
        dims = (((0,), (0,)), ((), ()))
    o_spec = pl.BlockSpec((tm, tn), lambda i, j, k: (i, j))
    has_res = residual is not None

    def body(*refs):
        if has_res:
            a_ref, b_ref, r_ref, o_ref, acc_ref = refs
        else:
            a_ref, b_ref, o_ref, acc_ref = refs
        k = pl.program_id(2)

        @pl.when(k == 0)
        def _():
            acc_ref[...] = jnp.zeros_like(acc_ref)

        acc_ref[...] += lax.dot_general(a_ref[...].astype(BF16), b_ref[...].astype(BF16), dims,
                                        preferred_element_type=F32)

        @pl.when(k == nk - 1)
        def _():
            if has_res:
                o_ref[...] = (acc_ref[...] + r_ref[...]).astype(out_dtype)
            else:
                o_ref[...] = acc_ref[...].astype(out_dtype)

    ins = [a, b] + ([residual] if has_res else [])
    in_specs = [a_spec, b_spec] + ([o_spec] if has_res else [])
    return pl.pallas_call(
        body, grid=(M // tm, N // tn, nk), in_specs=in_specs, out_specs=o_spec,
        out_shape=jax.ShapeDtypeStruct((M, N), out_dtype), scratch_shapes=[pltpu.VMEM((tm, tn), F32)],
        compiler_params=_cparams(("parallel", "parallel", "arbitrary")), name=name)(*ins)


class Cols:
    def __init__(self, arr, width, idx):
        self.arr, self.width, self.idx = arr, width, idx


def _arr(x):
    return x.arr if isinstance(x, Cols) else x


def _shape(x):
    return x.arr.shape[:-1] + (x.width,) if isinstance(x, Cols) else x.shape


def _col(x):
    return x.idx if isinstance(x, Cols) else 0


def _tok_spec(x, tile):
    shape, col = _shape(x), _col(x)
    return pl.BlockSpec(shape[:-2] + (tile, shape[-1]), lambda i: (0,) * (len(shape) - 2) + (i, col))


def _full_spec(x):
    shape, col = _shape(x), _col(x)
    return pl.BlockSpec(shape, lambda i: (0,) * (len(shape) - 1) + (col,))


def _halo_spec(x, tile):
    shape, col = _shape(x), _col(x)
    return pl.BlockSpec((8, shape[-1]), lambda i: (jnp.maximum(i * (tile // 8) - 1, 0), col))


def _blk(x, tile):
    shape = _shape(x)
    return jax.ShapeDtypeStruct(shape[:-2] + (tile, shape[-1]), _arr(x).dtype)


def _prev_rows(x, halo):
    rows = lax.broadcasted_iota(jnp.int32, (x.shape[0], 1), 0)
    before = jnp.where(pl.program_id(0) > 0, halo[7:8], 0.0)
    return jnp.where(rows == 0, before, pltpu.roll(x, 1, 0))


class Into:
    def __init__(self, buf, total, idx):
        self.buf, self.total, self.idx = buf, total, idx

    def place(self, shape, tile):
        idx = self.idx
        return shape.update(shape=(shape.shape[0], self.total)), pl.BlockSpec((tile, shape.shape[1]), lambda i: (i, idx))

    def operand(self, n_in, n_out_index):
        if self.buf is None:
            return [], [], {}
        return [self.buf], [_ANY], {n_in: n_out_index}


def stage_fwd(f, xs, ps, cts, cfs, tile, name, out_dtypes=None, with_prev=False, into=None):
    xs, ps, cts, cfs = list(xs), list(ps), list(cts), list(cfs)
    halos = xs if with_prev else []
    nx, nh, nct, np_ = len(xs), len(halos), len(cts), len(ps)
    T = _shape(xs[0])[-2]
    blk = [_blk(x, tile) for x in xs]
    out_avals = jax.eval_shape(f, *blk, *(blk if with_prev else []), *[_blk(p, _shape(p)[-2]) for p in ps],
                               *[_blk(c, tile) for c in cts], *[_blk(c, _shape(c)[-2]) for c in cfs])
    if out_dtypes is None:
        out_dtypes = [o.dtype for o in out_avals]
    out_shape = [jax.ShapeDtypeStruct(o.shape[:-2] + (T, o.shape[-1]), dt) for o, dt in zip(out_avals, out_dtypes)]
    out_specs = [_tok_spec(o, tile) for o in out_shape]
    n_in = nx + nh + nct + np_ + len(cfs)
    extra, extra_specs, alias = [], [], {}
    if into is not None:
        out_shape[0], out_specs[0] = into.place(out_shape[0], tile)
        extra, extra_specs, alias = into.operand(n_in, 0)

    def body(*refs):
        vals = [r[...] for r in refs[:n_in]]
        xv, hv, rest = vals[:nx], vals[nx:nx + nh], vals[nx + nh:]
        ctv, pv, cfv = rest[:nct], rest[nct:nct + np_], rest[nct + np_:]
        prev = [_prev_rows(x, h) for x, h in zip(xv, hv)]
        res = f(*xv, *prev, *pv, *ctv, *cfv)
        for o_ref, r in zip(refs[n_in + len(extra):], res):
            o_ref[...] = r.astype(o_ref.dtype)

    return pl.pallas_call(
        body, grid=(T // tile,),
        in_specs=([_tok_spec(x, tile) for x in xs] + [_halo_spec(x, tile) for x in halos] + [_tok_spec(c, tile) for c in cts]
                  + [_full_spec(p) for p in ps + cfs] + extra_specs),
        out_specs=out_specs, out_shape=out_shape, input_output_aliases=alias,
        compiler_params=_cparams(("parallel",)), name=name)(*[_arr(a) for a in xs + halos + cts + ps + cfs], *extra)


def stage_bwd(f, xs, ps, cts, cfs, gs, tile, name, bf16_copies=(), with_prev=False, copy_into=None):
    xs, ps, cts, cfs = list(xs), list(ps), list(cts), list(cfs)
    gs = [list(g) if isinstance(g, (list, tuple)) else [g] for g in gs]
    g_flat = [a for g in gs for a in g]
    halos = xs if with_prev else []
    nx, nh, nct, ng, np_ = len(xs), len(halos), len(cts), len(g_flat), len(ps)
    T = _shape(xs[0])[-2]
    dx_like = xs + halos
    out_shape = ([jax.ShapeDtypeStruct(_shape(x), F32) for x in dx_like] + [jax.ShapeDtypeStruct(_shape(p), F32) for p in ps]
                 + [jax.ShapeDtypeStruct(_shape(xs[i]), BF16) for i in bf16_copies])
    n_in = nx + nh + nct + ng + np_ + len(cfs)
    ndx = nx + nh
    plain = lambda x: jax.ShapeDtypeStruct(_shape(x), F32)
    out_specs = ([_tok_spec(plain(x), tile) for x in dx_like] + [_full_spec(plain(p)) for p in ps]
                 + [_tok_spec(plain(xs[i]), tile) for i in bf16_copies])
    extra, extra_specs, alias = [], [], {}
    if copy_into is not None:
        out_shape[ndx + np_], out_specs[ndx + np_] = copy_into.place(out_shape[ndx + np_], tile)
        extra, extra_specs, alias = copy_into.operand(n_in, ndx + np_)

    def body(*refs):
        vals = [r[...] for r in refs[:n_in]]
        outs = refs[n_in + len(extra):]
        xv, hv, rest = vals[:nx], vals[nx:nx + nh], vals[nx + nh:]
        ctv, gparts, pv, cfv = rest[:nct], rest[nct:nct + ng], rest[nct + ng:nct + ng + np_], rest[nct + ng + np_:]
        gv = []
        for g in gs:
            gv.append(functools.reduce(lambda a, b: a + b, gparts[:len(g)]))
            gparts = gparts[len(g):]
        prev = [_prev_rows(x, h) for x, h in zip(xv, hv)]
        _, vjp = jax.vjp(lambda *xp: f(*xp, *ctv, *cfv), *xv, *prev, *pv)
        d = vjp(tuple(gv))
        for o_ref, r in zip(outs[:ndx], d[:ndx]):
            o_ref[...] = r
        for o_ref, i in zip(outs[ndx + np_:], bf16_copies):
            o_ref[...] = d[i].astype(BF16)

        @pl.when(pl.program_id(0) == 0)
        def _():
            for o_ref in outs[ndx:ndx + np_]:
                o_ref[...] = jnp.zeros_like(o_ref)

        for o_ref, r in zip(outs[ndx:ndx + np_], d[ndx:]):
            o_ref[...] += r

    res = pl.pallas_call(
        body, grid=(T // tile,),
        in_specs=([_tok_spec(x, tile) for x in xs] + [_halo_spec(x, tile) for x in halos]
                  + [_tok_spec(c, tile) for c in cts + g_flat] + [_full_spec(p) for p in ps + cfs] + extra_specs),
        out_specs=out_specs, out_shape=out_shape, input_output_aliases=alias,
        compiler_params=_cparams(("arbitrary",)), name=name)(*[_arr(a) for a in xs + halos + cts + g_flat + ps + cfs], *extra)
    if bf16_copies:
        return list(res[:ndx]), list(res[ndx:ndx + np_]), list(res[ndx + np_:])
    return list(res[:ndx]), list(res[ndx:])


def _rms(x, g, eps=RMS_EPS):
    return x * lax.rsqrt(jnp.mean(x * x, axis=-1, keepdims=True) + eps) * g


def f_rmsnorm(x, g):
    return (_rms(x, g),)


def f_rmsnorm_res(x, g):
    return _rms(x, g), x


def f_rmsnorm2(x, g1, g2):
    n = x * lax.rsqrt(jnp.mean(x * x, axis=-1, keepdims=True) + RMS_EPS)
    return n * g1, n * g2


def f_rmsnorm2_res(x, g1, g2):
    return f_rmsnorm2(x, g1, g2) + (x,)


def _sigmoid(x):
    return 1.0 / (1.0 + jnp.exp(-x))


def _softplus(x):
    return jnp.maximum(x, 0.0) + jnp.log(1.0 + jnp.exp(-jnp.abs(x)))


def f_rwkv_pre(pr, pk, pv, pl_, qr, qk, qv, ql, mu_r, mu_k, mu_v, mu_l, w0, w2, a0, a2, g2, k_k, k_a, seg, seg_t):
    xr = pr + (qr - pr) * mu_r
    xk = pk + (qk - pk) * mu_k
    xv = pv + (qv - pv) * mu_v
    xl = pl_ + (ql - pl_) * mu_l
    w_log = -_softplus(-(w0 + mm_nn(jnp.tanh(xl), w2))) - 0.5
    lw = -jnp.exp(w_log)
    a = _sigmoid(a0 + mm_nn(xl, a2))
    g = mm_nn(_sigmoid(xl), g2)
    kkr = xk * k_k
    inv = lax.rsqrt(jnp.maximum(mmh(kkr * kkr, seg), 1e-24))
    kk = kkr * mmh(inv, seg_t)
    k2 = xk * (1.0 + (a - 1.0) * k_a)
    return xr, lw, k2, xv, kk, kk * a, g


def f_rwkv_post(y, r, k2, v, g, lnx_w, lnx_b, r_k, seg, seg_t):
    inv_n = 1.0 / HEAD_DIM
    m = mmh(mmh(y, seg) * inv_n, seg_t)
    yc = y - m
    rstd = lax.rsqrt(mmh(yc * yc, seg) * inv_n + LNX_EPS)
    yn = yc * mmh(rstd, seg_t) * lnx_w + lnx_b
    bonus = mmh(mmh(r * k2 * r_k, seg), seg_t) * v
    return ((yn + bonus) * g,)


def _headnorm(z, g, seg, seg_t):
    ms = mmh(z * z, seg) * (1.0 / HEAD_DIM)
    return z * mmh(lax.rsqrt(ms + RMS_EPS), seg_t) * g


def f_headnorm(z, g, seg, seg_t):
    return (_headnorm(z, g, seg, seg_t),)


def _rot_half(z):
    w = z.shape[1]
    half = HEAD_DIM // 2
    lane = lax.broadcasted_iota(jnp.int32, (1, w), 1)
    return jnp.where((lane & (HEAD_DIM - 1)) < half, -pltpu.roll(z, w - half, 1), pltpu.roll(z, half, 1))


@jax.custom_vjp
def _rotate_half(z):
    return _rot_half(z)


_rotate_half.defvjp(lambda z: (_rot_half(z), None), lambda _, g: (-_rot_half(g),))


def f_qkprep(z, g, cos, sin, seg, seg_t):
    zn = _headnorm(z, g, seg, seg_t)
    pairs = z.shape[1] // cos.shape[1]
    return (zn * jnp.tile(cos, (1, pairs)) + _rotate_half(zn) * jnp.tile(sin, (1, pairs)),)


def _head_mask(width, h):
    lane = lax.broadcasted_iota(jnp.int32, (1, width), 1)
    return jnp.where((lane >> 6) == h, jnp.ones((), F32), 0.0)


def f_memattn(q, k, v, q_norm, seg, seg_t):
    qn = _headnorm(q, q_norm, seg, seg_t)
    out = jnp.zeros_like(q)
    for h in range(MEM_HEADS):
        m = _head_mask(MEM_WIDTH, h)
        s = mm_nt(qn * m, k) * (1.0 / math.sqrt(HEAD_DIM))
        s = s - jnp.max(s, axis=-1, keepdims=True)
        p = jnp.exp(s)
        p = p / jnp.sum(p, axis=-1, keepdims=True)
        out = out + mm_nn(p, v) * m
    return (out,)


def f_mix(o1, o2, o3, l1, l2, l3):
    mx = jnp.maximum(jnp.maximum(l1, l2), l3)
    e1, e2, e3 = jnp.exp(l1 - mx), jnp.exp(l2 - mx), jnp.exp(l3 - mx)
    return ((e1 * o1 + e2 * o2 + e3 * o3) / (e1 + e2 + e3),)


def _chunk_masks(L):
    t = lax.broadcasted_iota(jnp.int32, (L, L), 0)
    s = lax.broadcasted_iota(jnp.int32, (L, L), 1)
    return t, s


def _unit_lower_inverse(a):
    L = a.shape[-1]
    t, s = _chunk_masks(L)
    one = jnp.ones((), F32)
    blk = lambda sh: jnp.where((t >> sh) == (s >> sh), one, 0.0)
    n0 = a * blk(3)
    x = jnp.where(t == s, one, 0.0) - n0
    n2 = mmh(n0, n0)
    x = x + mmh(x, n2)
    x = x + mmh(x, mmh(n2, n2))
    for sh in (3, 4, 5):
        if (1 << sh) >= L:
            break
        off = a * (blk(sh + 1) - blk(sh))
        x = x - mmh(x, mmh(off, x))
    return x


@jax.custom_vjp
def _inverse_known(a, x):
    return x


def _inverse_known_fwd(a, x):
    return x, x


def _inverse_known_bwd(x, dx):
    return -mmh_nt(mmh_tn(x, dx), x), jnp.zeros_like(x)


_inverse_known.defvjp(_inverse_known_fwd, _inverse_known_bwd)


def _running_sum(x, reverse):
    L = x.shape[1]
    pos = lax.broadcasted_iota(jnp.int32, (1, L, 1), 1)
    step = 1
    while step < L:
        if reverse:
            x = x + jnp.where(pos < L - step, pltpu.roll(x, L - step, 1), 0.0)
        else:
            x = x + jnp.where(pos >= step, pltpu.roll(x, step, 1), 0.0)
        step *= 2
    return x


@jax.custom_vjp
def _cumsum_tokens(x):
    return _running_sum(x, False)


_cumsum_tokens.defvjp(lambda x: (_running_sum(x, False), None), lambda _, g: (_running_sum(g, True),))


def f_rwkv_chunk(s0, r, lw, k, v, kk, b, x_known=None):
    H, L, _ = r.shape
    t, s = _chunk_masks(L)
    one = jnp.ones((), F32)
    incl = jnp.where(t >= s, one, 0.0)
    strict = jnp.where(t > s, one, 0.0)
    cum = _cumsum_tokens(lw)
    w_in = jnp.exp(cum)
    w_ex = jnp.exp(cum - lw)
    w_inv = jnp.exp(-cum)
    rt, kkt, kt, bt = r * w_in, kk * w_ex, k * w_inv, b * w_inv
    a_b = mmh_nt(kkt, bt) * strict
    a_k = mmh_nt(kkt, kt) * strict
    m_k = mmh_nt(rt, kt) * incl
    m_b = mmh_nt(rt, bt) * incl
    x = _unit_lower_inverse(a_b) if x_known is None else _inverse_known(a_b, x_known)
    u = mmh(x, mmh_nt(kkt, s0) + mmh(a_k, v))
    y = mmh_nt(rt, s0) + mmh(m_k, v) - mmh(m_b, u)
    w_last = jnp.exp(jnp.sum(lw, axis=1, keepdims=True))
    s1 = (s0 + mmh_tn(v, kt) - mmh_tn(u, bt)) * w_last
    return y, s1, x


def _ex_split(ex, refs, n_in, n_out):
    n = ex.n
    ins, ex_in = refs[:n_in], refs[n_in:n_in + n]
    outs, ex_out = refs[n_in + n:n_in + n + n_out], refs[n_in + n + n_out:n_in + 2 * n + n_out]
    rest = refs[n_in + 2 * n + n_out:]
    return ins, outs, rest[:len(rest) - 3], (ex_in, ex_out) + tuple(rest[len(rest) - 3:])


def _split_heads(x):
    return jnp.stack([x[:, h * HEAD_DIM:(h + 1) * HEAD_DIM] for h in range(x.shape[1] // HEAD_DIM)], axis=0)


def _merge_heads(x):
    return jnp.concatenate([x[h] for h in range(x.shape[0])], axis=1)


def rwkv_scan_fwd(r, lw, k, v, kk, b, ex):
    T, N = r.shape[0], HEAD_DIM
    H = r.shape[1] // N
    groups = SCAN_GROUPS_FWD
    nc, hg = T // CHUNK, H // groups
    seq = pl.BlockSpec((CHUNK, hg * N), lambda g, c: (c, g))

    def body(*refs):
        (r_ref, lw_ref, k_ref, v_ref, kk_ref, b_ref), (y_ref, hs_ref, xs_ref), (h_scr,), ex_refs = _ex_split(ex, refs, 6, 3)
        g, c = pl.program_id(0), pl.program_id(1)

        @pl.when(jnp.logical_and(g == 0, c == 0))
        def _():
            ex.start(*ex_refs)

        @pl.when(c == 0)
        def _():
            h_scr[...] = jnp.zeros_like(h_scr)

        h0 = h_scr[...]
        hs_ref[0] = h0
        y, h1, x = f_rwkv_chunk(h0, *[_split_heads(z[...]) for z in (r_ref, lw_ref, k_ref, v_ref, kk_ref, b_ref)])
        y_ref[...] = _merge_heads(y)
        xs_ref[0] = x
        h_scr[...] = h1

        @pl.when(jnp.logical_and(g == groups - 1, c == (3 * nc) // 4))
        def _():
            ex.forward(*ex_refs)

        @pl.when(jnp.logical_and(g == groups - 1, c == nc - 1))
        def _():
            ex.wait(*ex_refs)

    res = pl.pallas_call(
        body, grid=(groups, nc), in_specs=[seq] * 6 + [_ANY] * ex.n,
        out_specs=[seq, pl.BlockSpec((1, hg, N, N), lambda g, c: (c, g, 0, 0)),
                   pl.BlockSpec((1, hg, CHUNK, CHUNK), lambda g, c: (c, g, 0, 0))] + [_ANY] * ex.n,
        out_shape=[jax.ShapeDtypeStruct((T, H * N), F32), jax.ShapeDtypeStruct((nc, H, N, N), F32),
                   jax.ShapeDtypeStruct((nc, H, CHUNK, CHUNK), F32)] + ex.out_shape(),
        scratch_shapes=[pltpu.VMEM((hg, N, N), F32)] + ex.scratch(),
        compiler_params=_cparams(("arbitrary", "arbitrary")), name="rwkv_scan_fwd")(r, lw, k, v, kk, b, *ex.operands())
    return res[0], (res[1], res[2]), list(res[3:])


def rwkv_scan_bwd(r, lw, k, v, kk, b, saved, dy, ex):
    T, N = r.shape[0], HEAD_DIM
    H = r.shape[1] // N
    groups = SCAN_GROUPS_BWD
    nc, hg = T // CHUNK, H // groups
    seq = pl.BlockSpec((CHUNK, hg * N), lambda g, c: (nc - 1 - c, g))
    state = pl.BlockSpec((1, hg, N, N), lambda g, c: (nc - 1 - c, g, 0, 0))

    def body(*refs):
        (r_ref, lw_ref, k_ref, v_ref, kk_ref, b_ref, hs_ref, xs_ref, dy_ref), outs, (dh_scr,), ex_refs = _ex_split(ex, refs, 9, 6)
        g, c = pl.program_id(0), pl.program_id(1)

        @pl.when(jnp.logical_and(g == 0, c == 0))
        def _():
            ex.start(*ex_refs)

        @pl.when(c == 0)
        def _():
            dh_scr[...] = jnp.zeros_like(dh_scr)

        x_known = xs_ref[0]
        _, vjp = jax.vjp(lambda *a: f_rwkv_chunk(*a, x_known=x_known)[:2], hs_ref[0],
                         *[_split_heads(z[...]) for z in (r_ref, lw_ref, k_ref, v_ref, kk_ref, b_ref)])
        d = vjp((_split_heads(dy_ref[...]), dh_scr[...]))
        dh_scr[...] = d[0]
        for o_ref, dz in zip(outs, d[1:]):
            o_ref[...] = _merge_heads(dz)

        @pl.when(jnp.logical_and(g == groups - 1, c == nc - 1))
        def _():
            ex.forward(*ex_refs)
            ex.wait(*ex_refs)

    res = pl.pallas_call(
        body, grid=(groups, nc),
        in_specs=[seq] * 6 + [state, state, seq] + [_ANY] * ex.n,
        out_specs=[seq] * 6 + [_ANY] * ex.n, out_shape=[jax.ShapeDtypeStruct((T, H * N), F32)] * 6 + ex.out_shape(),
        scratch_shapes=[pltpu.VMEM((hg, N, N), F32)] + ex.scratch(),
        compiler_params=_cparams(("arbitrary", "arbitrary")), name="rwkv_scan_bwd")(r, lw, k, v, kk, b, *saved, dy, *ex.operands())
    return list(res[:6]), list(res[6:])


GROUP_COLS = 4 * HEAD_DIM


def _f_dilattn(has_prev, q, kc, kp, vc, vp):
    scale = 1.0 / math.sqrt(HEAD_DIM)
    i = lax.broadcasted_iota(jnp.int32, (DIL_BLOCK, DIL_BLOCK), 0)
    j = lax.broadcasted_iota(jnp.int32, (DIL_BLOCK, DIL_BLOCK), 1)
    o, l = jnp.zeros_like(q), jnp.zeros_like(q)
    for h in range(q.shape[1] // HEAD_DIM):
        m = _head_mask(q.shape[1], h)
        sc = jnp.where(j <= i, mm_nt(q * m, kc) * scale, NEG_INF)
        sp = jnp.where(jnp.logical_and(i <= j, has_prev), mm_nt(q * m, kp) * scale, NEG_INF)
        mx = jnp.maximum(jnp.max(sc, axis=-1, keepdims=True), jnp.max(sp, axis=-1, keepdims=True))
        pc, pp = jnp.exp(sc - mx), jnp.exp(sp - mx)
        den = jnp.sum(pc, axis=-1, keepdims=True) + jnp.sum(pp, axis=-1, keepdims=True)
        o = o + (mm_nn(pc, vc) + mm_nn(pp, vp)) / den * m
        l = l + (mx + jnp.log(den)) * m
    return o, l


def _dil_specs(gi, d):
    parts = 1 if d == 1 else 2
    blk = (DIL_BLOCK * d, GROUP_COLS // parts)
    at = lambda col: (lambda p, n: (n, col * parts + p))
    before = lambda col: (lambda p, n: (jnp.maximum(n - 1, 0), col * parts + p))
    v0 = DIL_WIDTH // GROUP_COLS + gi
    q = pl.BlockSpec(blk, at(gi))
    kc, kp = pl.BlockSpec(blk, at(gi)), pl.BlockSpec(blk, before(gi))
    vc, vp = pl.BlockSpec(blk, at(v0)), pl.BlockSpec(blk, before(v0))
    out = pl.BlockSpec(blk, at(0))
    together = min(d, 2)
    return (q, kc, kp, vc, vp, out), parts, together


def _residue_rows(r, d):
    return pl.ds(r, DIL_BLOCK, stride=d) if d > 1 else pl.ds(0, DIL_BLOCK)


def dil_fwd(q, k, kv, gi, d, name):
    T = q.shape[0]
    (qs, kc, kp, vc, vp, out), parts, together = _dil_specs(gi, d)

    def body(q_ref, kc_ref, kp_ref, vc_ref, vp_ref, o_ref, l_ref):
        has_prev = pl.program_id(1) > 0

        def residues(it, carry):
            rows = [_residue_rows(it * together + a, d) for a in range(together)]
            ins = [[ref[rw, :] for ref in (q_ref, kc_ref, kp_ref, vc_ref, vp_ref)] for rw in rows]
            res = [_f_dilattn(has_prev, *x) for x in ins]
            for rw, (o, l) in zip(rows, res):
                o_ref[rw, :] = o
                l_ref[rw, :] = l
            return carry

        lax.fori_loop(0, d // together, residues, 0)

    shape = jax.ShapeDtypeStruct((T, 4 * HEAD_DIM), F32)
    return pl.pallas_call(
        body, grid=(parts, T // (DIL_BLOCK * d)), in_specs=[qs, kc, kp, vc, vp], out_specs=[out, out], out_shape=[shape, shape],
        compiler_params=_cparams(("parallel", "parallel")), name=name)(q, k, k, kv, kv)


def dil_bwd(q, k, kv, do, dl, gi, d, name):
    T = q.shape[0]
    (qs, kc, kp, vc, vp, out), parts, together = _dil_specs(gi, d)

    def body(q_ref, kc_ref, kp_ref, vc_ref, vp_ref, do_ref, dl_ref, *outs):
        f = functools.partial(_f_dilattn, pl.program_id(1) > 0)

        def residues(it, carry):
            rows = [_residue_rows(it * together + a, d) for a in range(together)]
            ins = [[ref[rw, :] for ref in (q_ref, kc_ref, kp_ref, vc_ref, vp_ref, do_ref, dl_ref)] for rw in rows]
            res = [jax.vjp(f, *x[:5])[1]((x[5], x[6])) for x in ins]
            for rw, gs in zip(rows, res):
                for o_ref, g in zip(outs, gs):
                    o_ref[rw, :] = g
            return carry

        lax.fori_loop(0, d // together, residues, 0)

    shape = jax.ShapeDtypeStruct((T, 4 * HEAD_DIM), F32)
    dq, dkc, dkp, dvc, dvp = pl.pallas_call(
        body, grid=(parts, T // (DIL_BLOCK * d)), in_specs=[qs, kc, kp, vc, vp, out, out], out_specs=[out] * 5, out_shape=[shape] * 5,
        compiler_params=_cparams(("parallel", "parallel")), name=name)(q, k, k, kv, kv, do, dl)

    def own_plus_next(c, p):
        return c + jnp.concatenate([p[DIL_BLOCK * d:], jnp.zeros_like(p[:DIL_BLOCK * d])], axis=0)

    return dq, own_plus_next(dkc, dkp), own_plus_next(dvc, dvp)


CONV_TILE = 256


def _conv3(before, u, w, b):
    ue = jnp.concatenate([before, u], axis=0)
    s1, s2 = pltpu.roll(ue, 1, 0)[8:], pltpu.roll(ue, 2, 0)[8:]
    return b + w[0:1] * s2 + w[1:2] * s1 + w[2:3] * u, s1, s2


def _conv_halves(u_ref, h_ref, cw_ref, cb_ref):
    F = D_FF
    res = []
    for lo in (0, F):
        before = jnp.where(pl.program_id(0) > 0, h_ref[:, lo:lo + F], 0.0)
        u = u_ref[:, lo:lo + F]
        res.append((u,) + _conv3(before, u, cw_ref[:, lo:lo + F], cb_ref[:, lo:lo + F]))
    return res


def _halo_before(C):
    return pl.BlockSpec((8, C), lambda i: (jnp.maximum(i * (CONV_TILE // 8) - 1, 0), 0))


def convgate_fwd(u, cw, cb, name):
    T, C = u.shape
    F = C // 2

    def body(u_ref, h_ref, cw_ref, cb_ref, z_ref):
        (_, cg, _, _), (_, cv, _, _) = _conv_halves(u_ref, h_ref, cw_ref, cb_ref)
        z_ref[...] = (cg * _sigmoid(cg) * cv).astype(BF16)

    return pl.pallas_call(
        body, grid=(T // CONV_TILE,),
        in_specs=[pl.BlockSpec((CONV_TILE, C), lambda i: (i, 0)), _halo_before(C), _full_spec(cw), _full_spec(cb)],
        out_specs=pl.BlockSpec((CONV_TILE, F), lambda i: (i, 0)), out_shape=jax.ShapeDtypeStruct((T, F), BF16),
        compiler_params=_cparams(("parallel",)), name=name)(u, u, cw, cb)


def convgate_bwd(u, cw, cb, dz, name):
    T, C = u.shape
    F = C // 2
    n = T // CONV_TILE
    E = CONV_TILE + 8

    def body(u_ref, hb_ref, ha_ref, cw_ref, cb_ref, dz_ref, dza_ref, du_ref, dcw_ref, dcb_ref):
        i = pl.program_id(0)
        dze = jnp.concatenate([dz_ref[...], jnp.where(i < n - 1, dza_ref[...], 0.0)], axis=0)

        @pl.when(i == 0)
        def _():
            dcw_ref[...] = jnp.zeros_like(dcw_ref)
            dcb_ref[...] = jnp.zeros_like(dcb_ref)

        halves = []
        for lo in (0, F):
            sl = slice(lo, lo + F)
            ue = jnp.concatenate([u_ref[:, sl], ha_ref[:, sl]], axis=0)
            c, s1, s2 = _conv3(jnp.where(i > 0, hb_ref[:, sl], 0.0), ue, cw_ref[:, sl], cb_ref[:, sl])
            halves.append((sl, ue, c, s1, s2))
        (_, _, cg, _, _), (_, _, cv, _, _) = halves
        sg = _sigmoid(cg)
        dcs = (dze * cv * sg * (1.0 + cg * (1.0 - sg)), dze * cg * sg)
        for (sl, ue, _, s1, s2), dc in zip(halves, dcs):
            own = lambda z: z[:CONV_TILE]
            dcb_ref[:, sl] += jnp.sum(own(dc), axis=0, keepdims=True)
            dcw_ref[0:1, sl] += jnp.sum(own(dc * s2), axis=0, keepdims=True)
            dcw_ref[1:2, sl] += jnp.sum(own(dc * s1), axis=0, keepdims=True)
            dcw_ref[2:3, sl] += jnp.sum(own(dc * ue), axis=0, keepdims=True)
            du = cw_ref[2:3, sl] * dc + cw_ref[1:2, sl] * pltpu.roll(dc, E - 1, 0) + cw_ref[0:1, sl] * pltpu.roll(dc, E - 2, 0)
            du_ref[:, sl] = own(du).astype(BF16)

    after = lambda w: pl.BlockSpec((8, w), lambda i: (jnp.minimum((i + 1) * (CONV_TILE // 8), T // 8 - 1), 0))
    return pl.pallas_call(
        body, grid=(n,),
        in_specs=[pl.BlockSpec((CONV_TILE, C), lambda i: (i, 0)), _halo_before(C), after(C), _full_spec(cw), _full_spec(cb),
                  pl.BlockSpec((CONV_TILE, F), lambda i: (i, 0)), after(F)],
        out_specs=[pl.BlockSpec((CONV_TILE, C), lambda i: (i, 0)), _full_spec(cw), _full_spec(cb)],
        out_shape=[jax.ShapeDtypeStruct((T, C), BF16), jax.ShapeDtypeStruct(cw.shape, F32), jax.ShapeDtypeStruct(cb.shape, F32)],
        compiler_params=_cparams(("arbitrary",)), name=name)(u, u, u, cw, cb, dz, dz)


def loss_head(y, tgt):
    T, D = y.shape
    tile = ROW_TILE

    def body(y_ref, t_ref, l_ref, d_ref, db_ref):
        d = y_ref[...] - t_ref[...]
        d_ref[...] = d * (1.0 / D)
        db_ref[...] = (d * (1.0 / D)).astype(BF16)

        @pl.when(pl.program_id(0) == 0)
        def _():
            l_ref[...] = jnp.zeros_like(l_ref)

        l_ref[...] += (0.5 / D) * jnp.sum(d * d)

    row = pl.BlockSpec((tile, D), lambda i: (i, 0))
    return pl.pallas_call(
        body, grid=(T // tile,), in_specs=[row, row], out_specs=[pl.BlockSpec((8, 128), lambda i: (0, 0)), row, row],
        out_shape=[jax.ShapeDtypeStruct((8, 128), F32), jax.ShapeDtypeStruct((T, D), F32), jax.ShapeDtypeStruct((T, D), BF16)],
        compiler_params=_cparams(("arbitrary",)), name="loss_head")(y, tgt)


def sum_parts(parts, name):
    S, R, C = parts.shape
    tile = _pick(R, (256, 128, 64, 32, 16, 8))

    def body(p_ref, o_ref):
        acc = p_ref[0]
        for s in range(1, S):
            acc = acc + p_ref[s]
        o_ref[...] = acc

    return pl.pallas_call(
        body, grid=(R // tile,), in_specs=[pl.BlockSpec((S, tile, C), lambda i: (0, i, 0))],
        out_specs=pl.BlockSpec((tile, C), lambda i: (i, 0)), out_shape=jax.ShapeDtypeStruct((R, C), F32),
        compiler_params=_cparams(("parallel",)), name=name)(parts)


def adamw(gparts, w, m, v, name):
    S, R, C = gparts.shape
    tile = _pick(R, (256, 128, 64, 32, 16, 8))
    c1 = 1.0 / (1.0 - ADAM_B1 ** ADAM_STEP)
    c2 = 1.0 / (1.0 - ADAM_B2 ** ADAM_STEP)

    def body(g_ref, w_ref, m_ref, v_ref, go_ref, d_ref, mo_ref, vo_ref):
        g = g_ref[0].astype(F32)
        for s in range(1, S):
            g = g + g_ref[s].astype(F32)
        m1 = ADAM_B1 * m_ref[...] + (1.0 - ADAM_B1) * g
        v1 = ADAM_B2 * v_ref[...] + (1.0 - ADAM_B2) * (g * g)
        go_ref[...] = g
        mo_ref[...] = m1
        vo_ref[...] = v1
        d_ref[...] = -ADAM_LR * ((m1 * c1) / (jnp.sqrt(v1 * c2) + ADAM_EPS) + ADAM_WD * w_ref[...])

    row = pl.BlockSpec((tile, C), lambda i: (i, 0))
    return pl.pallas_call(
        body, grid=(R // tile,), in_specs=[pl.BlockSpec((S, tile, C), lambda i: (0, i, 0)), row, row, row],
        out_specs=[row] * 4, out_shape=[jax.ShapeDtypeStruct((R, C), F32)] * 4,
        compiler_params=_cparams(("parallel",)), name=name)(gparts, w, m, v)


def _peers():
    x, y, c = lax.axis_index("x"), lax.axis_index("y"), lax.axis_index("c")
    peers = []
    for k in range(1, N_DEV):
        px = 1 - x if k & 4 else x
        py = 1 - y if k & 2 else y
        pc = 1 - c if k & 1 else c
        peers.append(((px, py, pc), 4 * px + 2 * py + pc))
    return 4 * x + 2 * y + c, peers


_ANY = pl.BlockSpec(memory_space=pl.ANY)


class Exchange:
    def __init__(self, gathers=(), scatters=()):
        self.gathers, self.scatters = list(gathers), list(scatters)
        self.n = len(self.gathers) + len(self.scatters)

    def operands(self):
        return self.gathers + self.scatters

    def out_shape(self):
        return ([jax.ShapeDtypeStruct((N_DEV,) + x.shape, x.dtype) for x in self.gathers]
                + [jax.ShapeDtypeStruct(x.shape, x.dtype) for x in self.scatters])

    def scratch(self):
        n = max(self.n, 1)
        return [pltpu.SemaphoreType.DMA((7 * n,)), pltpu.SemaphoreType.DMA((7 * n,)), pltpu.SemaphoreType.DMA((n,))]

    def _copies(self, in_refs, out_refs, send_sems, recv_sems, local_sems):
        me, peers = _peers()
        ng = len(self.gathers)
        local, sends, recvs = [], [], []
        for a in range(self.n):
            x, o = in_refs[a], out_refs[a]
            mine = x if a < ng else x.at[me]
            local.append(pltpu.make_async_copy(mine, o.at[me], local_sems.at[a]))
            s_a, r_a = {}, {}
            for k in range(1, N_DEV):
                peer, slot = peers[k - 1]
                sems = dict(send_sem=send_sems.at[7 * a + k - 1], recv_sem=recv_sems.at[7 * a + k - 1],
                            device_id_type=pl.DeviceIdType.MESH)
                if a >= ng:
                    s_a[k] = pltpu.make_async_remote_copy(src_ref=x.at[slot], dst_ref=o.at[me], device_id=peer, **sems)
                elif k in FORWARDED:
                    came = o.at[peers[k - 2][1]]
                    s_a[k] = pltpu.make_async_remote_copy(src_ref=came, dst_ref=came, device_id=peers[0][0], **sems)
                else:
                    s_a[k] = pltpu.make_async_remote_copy(src_ref=x, dst_ref=o.at[me], device_id=peer, **sems)
                r_a[k] = pltpu.make_async_remote_copy(src_ref=mine, dst_ref=o.at[slot], device_id=peer, **sems)
            sends.append(s_a)
            recvs.append(r_a)
        return local, sends, recvs

    def start(self, *refs):
        if self.n == 0:
            return
        local, sends, _ = self._copies(*refs)
        for a in range(self.n):
            local[a].start()
            for k in range(1, N_DEV):
                if a >= len(self.gathers) or k not in FORWARDED:
                    sends[a][k].start()

    def forward(self, *refs):
        if not self.gathers:
            return
        _, sends, recvs = self._copies(*refs)
        for a in range(len(self.gathers)):
            for k in FORWARDED:
                recvs[a][k - 1].wait_recv()
                sends[a][k].start()

    def wait(self, *refs):
        if self.n == 0:
            return
        local, sends, recvs = self._copies(*refs)
        for a in range(self.n):
            waited_early = [f - 1 for f in FORWARDED] if a < len(self.gathers) else []
            for k in range(1, N_DEV):
                if k not in waited_early:
                    recvs[a][k].wait_recv()
            for k in range(1, N_DEV):
                sends[a][k].wait_send()
            local[a].wait()


FORWARDED = (3, 5, 7)


def exchange(ex, name):
    n = ex.n

    def body(*refs):
        args = (refs[:n], refs[n:2 * n]) + tuple(refs[2 * n:])
        ex.start(*args)
        ex.forward(*args)
        ex.wait(*args)

    return pl.pallas_call(body, in_specs=[_ANY] * n, out_specs=[_ANY] * n, out_shape=ex.out_shape(),
                          scratch_shapes=ex.scratch(), name=name)(*ex.operands())


def _shift_up(z):
    return jnp.concatenate([z[1:], jnp.zeros_like(z[:1])], axis=0)


def _segments(width):
    seg = np.zeros((width, 128), np.float32)
    seg[np.arange(width), np.arange(width) // HEAD_DIM] = 1.0
    return jnp.asarray(seg), jnp.asarray(seg.T)


def _rope_consts(T):
    inv = ROPE_THETA ** (-jnp.arange(0, HEAD_DIM, 2, dtype=F32) / HEAD_DIM)
    ang = jnp.arange(T, dtype=F32)[:, None] * inv[None, :]
    return jnp.tile(jnp.cos(ang), (1, 4)), jnp.tile(jnp.sin(ang), (1, 4))


def _per_head(g, heads):
    return jnp.tile(g.reshape(1, HEAD_DIM), (1, heads))


def _sum_heads(g):
    return g.reshape(-1, HEAD_DIM).sum(axis=0, keepdims=True)


LORA_COLS = 256
RW_TILE = 256
ROW_TILE = 512


def _local_step(x0, memx, tgt, P, ex_weights=None, weights_done=None, ex_grads=None):
    T = x0.shape[0]
    P = dict(P)
    G = {}
    seg, seg_t = _segments(RWKV_WIDTH)
    mseg = (seg[:MEM_WIDTH], seg_t[:, :MEM_WIDTH])
    cos, sin = _rope_consts(T)
    row = lambda v: v.reshape(1, -1)

    def mem_fwd(i, q, into):
        memn = stage_fwd(f_rmsnorm, [memx], [P["mem_norm"][i:i + 1]], [], [], N_MEM, f"mem{i}_norm", [BF16])[0]
        kvm = matmul(memn, P["mem_w_kv"][i], "nn", f"mem{i}_kv")
        kn, qn = _per_head(P["mem_k_norm"][i], MEM_HEADS), _per_head(P["mem_q_norm"][i], MEM_HEADS)
        km = stage_fwd(f_headnorm, [Cols(kvm, MEM_WIDTH, 0)], [kn], [], mseg, N_MEM, f"mem{i}_knorm")[0]
        om = stage_fwd(f_memattn, [q], [km, Cols(kvm, MEM_WIDTH, 1), qn], [], mseg, ROW_TILE, f"mem{i}_attn", [BF16], into=into)[0]
        return om, (memn, kvm, km, kn, qn, q)

    def mem_bwd(i, saved, dymem, copy_into=None):
        memn, kvm, km, kn, qn, q = saved
        (dq,), (dkm, dvm, g_qn), *copy = stage_bwd(f_memattn, [q], [km, Cols(kvm, MEM_WIDTH, 1), qn], [], mseg, [dymem], ROW_TILE,
                                                   f"mem{i}_attn_bwd", bf16_copies=(0,) if copy_into else (), copy_into=copy_into)
        dq = copy[0][0] if copy_into else dq
        (dkraw,), (g_kn,) = stage_bwd(f_headnorm, [Cols(kvm, MEM_WIDTH, 0)], [kn], [], mseg, [dkm], N_MEM, f"mem{i}_knorm_bwd")
        dkvm = jnp.concatenate([dkraw, dvm], axis=1).astype(BF16)
        g_w = matmul(memn, dkvm, "tn", f"mem{i}_kv_dw")
        dmemn = matmul(dkvm, P["mem_w_kv"][i], "nt", f"mem{i}_kv_dx")
        _, (g_mn,) = stage_bwd(f_rmsnorm, [memx], [P["mem_norm"][i:i + 1]], [], [], [dmemn], N_MEM, f"mem{i}_norm_bwd")
        return dq, g_mn, g_w, _sum_heads(g_qn), _sum_heads(g_kn)

    def ffn_fwd(i, xin):
        hn = stage_fwd(f_rmsnorm, [xin], [P["ffn_norm"][i:i + 1]], [], [], ROW_TILE, f"ffn{i}_norm", [BF16])[0]
        u = matmul(hn, P["ffn_w_up"][i], "nn", f"ffn{i}_up")
        z = convgate_fwd(u, P["ffn_conv_w"][i], P["ffn_conv_b"][i:i + 1], f"ffn{i}_conv")
        return matmul(z, P["ffn_w_down"][i], "nn", f"ffn{i}_down", residual=xin), (hn, u, z)

    def ffn_bwd(i, xin, saved, dxo, dxo_b):
        hn, u, z = saved
        dz = matmul(dxo_b, P["ffn_w_down"][i], "nt", f"ffn{i}_down_dx")
        g_down = matmul(z, dxo_b, "tn", f"ffn{i}_down_dw")
        du, g_cw, g_cb = convgate_bwd(u, P["ffn_conv_w"][i], P["ffn_conv_b"][i:i + 1], dz, f"ffn{i}_conv_bwd")
        dhn = matmul(du, P["ffn_w_up"][i], "nt", f"ffn{i}_up_dx")
        g_up = matmul(hn, du, "tn", f"ffn{i}_up_dw")
        (dxin,), (g_n,), (dxin_b,) = stage_bwd(f_rmsnorm_res, [xin], [P["ffn_norm"][i:i + 1]], [], [], [dhn, dxo], ROW_TILE,
                                               f"ffn{i}_norm_bwd", bf16_copies=(0,))
        return dxin, dxin_b, g_n, g_up, g_cw, g_cb, g_down

    h0 = stage_fwd(f_rmsnorm, [x0], [P["attn_norm"][0:1]], [], [], ROW_TILE, "l0_norm", [BF16])[0]
    p0 = matmul(h0, P["a_w_in"][0], "nn", "l0_in")
    lora0 = 3 * RWKV_WIDTH // LORA_COLS
    pre_xs = [Cols(p0, RWKV_WIDTH, 0), Cols(p0, RWKV_WIDTH, 1), Cols(p0, RWKV_WIDTH, 2), Cols(p0, LORA_COLS, lora0)]
    mu = [Cols(P["a_mu"], RWKV_WIDTH, 0), Cols(P["a_mu"], RWKV_WIDTH, 1), Cols(P["a_mu"], RWKV_WIDTH, 2),
          Cols(P["a_mu"], LORA_COLS, lora0)]
    lora_rows = lambda w, lo: jnp.pad(w, ((lo, LORA_COLS - lo - w.shape[0]), (0, 0)))
    pre_ps = mu + [P["a_w0"], lora_rows(P["a_w2"][0], 0), P["a_a0"], lora_rows(P["a_a2"][0], 64), lora_rows(P["a_g2"][0], 128),
                   P["a_k_k"], P["a_k_a"]]
    r, lw, k2, v, kk, b, g = stage_fwd(f_rwkv_pre, pre_xs, pre_ps, [], [seg, seg_t], RW_TILE, "l0_rwkv_pre", with_prev=True)
    scan_in = [r, lw, k2, v, kk, b]
    y_h, h_states, got = rwkv_scan_fwd(*scan_in, ex_weights or Exchange())
    if weights_done is not None:
        P.update(weights_done(got))
    y_s = y_h
    post_ps = [P["a_lnx_w"], P["a_lnx_b"], P["a_r_k"].reshape(1, RWKV_WIDTH)]
    ycat0, mem0_saved = mem_fwd(0, Cols(p0, MEM_WIDTH, SHIFT_WIDTH // MEM_WIDTH), Into(None, D_MODEL, RWKV_WIDTH // MEM_WIDTH))
    ycat0 = stage_fwd(f_rwkv_post, [y_s, r, k2, v, g], post_ps, [], [seg, seg_t], RW_TILE, "l0_rwkv_post", [BF16],
                      into=Into(ycat0, D_MODEL, 0))[0]
    x1 = matmul(ycat0, P["a_w_out"][0], "nn", "l0_out", residual=x0)
    x2, ffn0_saved = ffn_fwd(0, x1)

    hk, h1 = stage_fwd(f_rmsnorm2, [x2], [row(P["kv_norm"]), P["attn_norm"][1:2]], [], [], ROW_TILE, "l1_norm", [BF16, BF16])
    kvp = matmul(hk, P["kv_w"][0], "nn", "l1_kv")
    p1 = matmul(h1, P["b_w_in"][0], "nn", "l1_in")
    kraw, qraw = Cols(kvp, DIL_WIDTH, 0), Cols(p1, DIL_WIDTH, 0)
    kgain, qgain = _per_head(P["kv_k_norm"], DIL_WIDTH // HEAD_DIM), _per_head(P["b_q_norm"], DIL_WIDTH // HEAD_DIM)
    ksh = stage_fwd(f_qkprep, [kraw], [kgain], [cos, sin], [seg, seg_t], ROW_TILE, "l1_kprep")[0]
    q = stage_fwd(f_qkprep, [qraw], [qgain], [cos, sin], [seg, seg_t], ROW_TILE, "l1_qprep")[0]
    outs, lses = [], []
    for gi, (_, d) in enumerate(DIL_GROUPS):
        og, lg = dil_fwd(q, ksh, kvp, gi, d, f"l1_dil{gi}")
        outs.append(og)
        lses.append(lg)
    ycat1 = stage_fwd(f_mix, outs + lses, [], [], [], ROW_TILE, "l1_mix", [BF16], into=Into(None, 2 * MEM_WIDTH, 0))[0]
    ycat1, mem1_saved = mem_fwd(1, Cols(p1, MEM_WIDTH, DIL_WIDTH // MEM_WIDTH), Into(ycat1, 2 * MEM_WIDTH, 1))
    x3 = matmul(ycat1, P["b_w_out"][0], "nn", "l1_out", residual=x2)
    x4, ffn1_saved = ffn_fwd(1, x3)
    loss_part, dx4, dx4_b = loss_head(x4, tgt)

    dx3, dx3_b, gn1, gup1, gcw1, gcb1, gdown1 = ffn_bwd(1, x3, ffn1_saved, dx4, dx4_b)
    dycat1 = matmul(dx3_b, P["b_w_out"][0], "nt", "l1_out_dx")
    G["b_w_out"] = [matmul(ycat1, dx3_b, "tn", "l1_out_dw")]
    dp1, gmn1, gmw1, gmq1, gmk1 = mem_bwd(1, mem1_saved, Cols(dycat1, MEM_WIDTH, 1), Into(None, D_MODEL, DIL_WIDTH // MEM_WIDTH))
    dmix, _ = stage_bwd(f_mix, outs + lses, [], [], [], [Cols(dycat1, MEM_WIDTH, 0)], ROW_TILE, "l1_mix_bwd")
    dq, dk, dv = zip(*[dil_bwd(q, ksh, kvp, dmix[gi], dmix[3 + gi], gi, d, f"l1_dil{gi}_bwd")
                       for gi, (_, d) in enumerate(DIL_GROUPS)])
    dq, dk, dv = jnp.concatenate(dq, axis=1), jnp.concatenate(dk, axis=1), jnp.concatenate(dv, axis=1)
    _, (g_bq,), (dp1,) = stage_bwd(f_qkprep, [qraw], [qgain], [cos, sin], [seg, seg_t], [dq], ROW_TILE, "l1_qprep_bwd",
                                   bf16_copies=(0,), copy_into=Into(dp1, D_MODEL, 0))
    dkvp = jnp.pad(dv.astype(BF16), ((0, 0), (DIL_WIDTH, 0)))
    _, (g_kk,), (dkvp,) = stage_bwd(f_qkprep, [kraw], [kgain], [cos, sin], [seg, seg_t], [dk], ROW_TILE, "l1_kprep_bwd",
                                    bf16_copies=(0,), copy_into=Into(dkvp, 2 * DIL_WIDTH, 0))
    g_bq, g_kk = _sum_heads(g_bq), _sum_heads(g_kk)
    dh1 =matmul(dp1, P["b_w_in"][0], "nt", "l1_in_dx")
    G["b_w_in"] = [matmul(h1, dp1, "tn", "l1_in_dw")]
    dhk = matmul(dkvp, P["kv_w"][0], "nt", "l1_kv_dx")
    G["kv_w"] = [matmul(hk, dkvp, "tn", "l1_kv_dw")]
    (dx2,), (g_kvn, g_an1), (dx2_b,) = stage_bwd(f_rmsnorm2_res, [x2], [row(P["kv_norm"]), P["attn_norm"][1:2]], [], [],
                                                 [dhk, dh1, dx3], ROW_TILE, "l1_norm_bwd", bf16_copies=(0,))

    dx1, dx1_b, gn0, gup0, gcw0, gcb0, gdown0 = ffn_bwd(0, x1, ffn0_saved, dx2, dx2_b)
    dycat0 = matmul(dx1_b, P["a_w_out"][0], "nt", "l0_out_dx")
    G["a_w_out"] = [matmul(ycat0, dx1_b, "tn", "l0_out_dw")]
    dqmem0, gmn0, gmw0, gmq0, gmk0 = mem_bwd(0, mem0_saved, Cols(dycat0, MEM_WIDTH, RWKV_WIDTH // MEM_WIDTH))
    (dy_s, dr_a, dk_a, dv_a, dg), (g_lw, g_lb, g_rk) = stage_bwd(
        f_rwkv_post, [y_s, r, k2, v, g], post_ps, [], [seg, seg_t], [Cols(dycat0, RWKV_WIDTH, 0)], RW_TILE, "l0_rwkv_post_bwd")
    G["mem_w_kv"], G["ffn_w_up"], G["ffn_w_down"] = [gmw0, gmw1], [gup0, gup1], [gdown0, gdown1]
    (dr_b, dlw, dk_b, dv_b, dkk, db), G["_exchanged"] = rwkv_scan_bwd(*scan_in, h_states, dy_s,
                                                                      ex_grads(G) if ex_grads else Exchange())
    dpre, gpre = stage_bwd(f_rwkv_pre, pre_xs, pre_ps, [], [seg, seg_t],
                           [[dr_a, dr_b], dlw, [dk_a, dk_b], [dv_a, dv_b], dkk, db, dg], RW_TILE, "l0_rwkv_pre_bwd", with_prev=True)
    dp_rw = jnp.concatenate(dpre[:4], axis=1) + _shift_up(jnp.concatenate(dpre[4:], axis=1))
    dp0 = jnp.concatenate([dp_rw, dqmem0], axis=1).astype(BF16)
    dh0 = matmul(dp0, P["a_w_in"][0], "nt", "l0_in_dx")
    G["a_w_in"] = [matmul(h0, dp0, "tn", "l0_in_dw")]
    (dx0,), (g_an0,) = stage_bwd(f_rmsnorm_res, [x0], [P["attn_norm"][0:1]], [], [], [dh0, dx1], ROW_TILE, "l0_norm_bwd")

    G["attn_norm"] = jnp.concatenate([g_an0, g_an1], axis=0)
    G["a_mu"] = jnp.concatenate(gpre[:4], axis=1)
    G["a_w0"], G["a_w2"], G["a_a0"], G["a_a2"], G["a_g2"] = gpre[4], gpre[5][None, :64], gpre[6], gpre[7][None, 64:128], gpre[8][None, 128:]
    G["a_k_k"], G["a_k_a"] = gpre[9], gpre[10]
    G["a_r_k"] = g_rk.reshape(1, RWKV_HEADS, HEAD_DIM)
    G["a_lnx_w"], G["a_lnx_b"] = g_lw, g_lb
    G["kv_norm"], G["kv_k_norm"], G["b_q_norm"] = g_kvn.reshape(-1), g_kk.reshape(-1), g_bq
    G["mem_norm"] = jnp.concatenate([gmn0, gmn1], axis=0)
    G["mem_w_kv"] = [gmw0, gmw1]
    G["mem_q_norm"] = jnp.concatenate([gmq0, gmq1], axis=0)
    G["mem_k_norm"] = jnp.concatenate([gmk0, gmk1], axis=0)
    G["ffn_norm"] = jnp.concatenate([gn0, gn1], axis=0)
    G["ffn_w_up"] = [gup0, gup1]
    G["ffn_conv_w"] = jnp.stack([gcw0, gcw1])
    G["ffn_conv_b"] = jnp.concatenate([gcb0, gcb1], axis=0)
    G["ffn_w_down"] = [gdown0, gdown1]
    return loss_part, dx0, G


PARAMS = (("attn_norm", None), ("a_w_in", 2), ("a_mu", 1), ("a_w0", 1), ("a_w2", 2), ("a_a0", 1), ("a_a2", 2), ("a_g2", 2),
          ("a_k_k", 1), ("a_k_a", 1), ("a_r_k", None), ("a_lnx_w", 1), ("a_lnx_b", 1), ("a_w_out", 1), ("kv_norm", None),
          ("kv_w", 1), ("kv_k_norm", None), ("b_w_in", 1), ("b_q_norm", None), ("b_w_out", 2), ("mem_norm", None),
          ("mem_w_kv", 1), ("mem_q_norm", None), ("mem_k_norm", None), ("ffn_norm", None), ("ffn_w_up", 2),
          ("ffn_conv_w", 2), ("ffn_conv_b", None), ("ffn_w_down", 1))
BIG = ("a_w_in", "a_w_out", "kv_w", "b_w_in", "b_w_out", "mem_w_kv", "ffn_w_up", "ffn_w_down")
AXIS = dict(PARAMS)
SMALL = tuple(n for n, _ in PARAMS if n not in BIG)
SMALL_SHARDED = tuple(n for n in SMALL if AXIS[n] is not None)
PACK_QUANTUM = 256 * 128


def _from_shards(xs, axis):
    full = jnp.moveaxis(xs, 0, axis)
    sh = full.shape
    return full.reshape(sh[:axis] + (sh[axis] * sh[axis + 1],) + sh[axis + 2:])


def _to_shards(g, axis):
    sh = g.shape
    return jnp.moveaxis(g.reshape(sh[:axis] + (N_DEV, sh[axis] // N_DEV) + sh[axis + 1:]), axis, 0)


def _pack(parts, lead=0):
    ld = parts[0].shape[:lead]
    flat = jnp.concatenate([p.reshape(ld + (-1,)) for p in parts], axis=-1)
    pad = (-flat.shape[-1]) % PACK_QUANTUM
    flat = jnp.pad(flat, [(0, 0)] * lead + [(0, pad)])
    return flat.reshape(ld + (-1, 128))


def _unpack(packed, shapes, lead=0):
    ld = packed.shape[:lead]
    flat = packed.reshape(ld + (-1,))
    out, off = [], 0
    for s in shapes:
        n = math.prod(s)
        out.append(flat[..., off:off + n].reshape(ld + tuple(s)))
        off += n
    return out


def kernel(x, mem, attn_norm, a_w_in, a_mu, a_w0, a_w2, a_a0, a_a2, a_g2, a_k_k, a_k_a, a_r_k, a_lnx_w, a_lnx_b, a_w_out, kv_norm, kv_w, kv_k_norm, b_w_in, b_q_norm, b_w_out, mem_norm, mem_w_kv, mem_q_norm, mem_k_norm, ffn_norm, ffn_w_up, ffn_conv_w, ffn_conv_b, ffn_w_down, loss_target, m_attn_norm, m_a_w_in, m_a_mu, m_a_w0, m_a_w2, m_a_a0, m_a_a2, m_a_g2, m_a_k_k, m_a_k_a, m_a_r_k, m_a_lnx_w, m_a_lnx_b, m_a_w_out, m_kv_norm, m_kv_w, m_kv_k_norm, m_b_w_in, m_b_q_norm, m_b_w_out, m_mem_norm, m_mem_w_kv, m_mem_q_norm, m_mem_k_norm, m_ffn_norm, m_ffn_w_up, m_ffn_conv_w, m_ffn_conv_b, m_ffn_w_down, v_attn_norm, v_a_w_in, v_a_mu, v_a_w0, v_a_w2, v_a_a0, v_a_a2, v_a_g2, v_a_k_k, v_a_k_a, v_a_r_k, v_a_lnx_w, v_a_lnx_b, v_a_w_out, v_kv_norm, v_kv_w, v_kv_k_norm, v_b_w_in, v_b_q_norm, v_b_w_out, v_mem_norm, v_mem_w_kv, v_mem_q_norm, v_mem_k_norm, v_ffn_norm, v_ffn_w_up, v_ffn_conv_w, v_ffn_conv_b, v_ffn_w_down):
    names = [n for n, _ in PARAMS]
    vals = (attn_norm, a_w_in, a_mu, a_w0, a_w2, a_a0, a_a2, a_g2, a_k_k, a_k_a, a_r_k, a_lnx_w, a_lnx_b, a_w_out, kv_norm, kv_w, kv_k_norm, b_w_in, b_q_norm, b_w_out, mem_norm, mem_w_kv, mem_q_norm, mem_k_norm, ffn_norm, ffn_w_up, ffn_conv_w, ffn_conv_b, ffn_w_down)
    m_vals = (m_attn_norm, m_a_w_in, m_a_mu, m_a_w0, m_a_w2, m_a_a0, m_a_a2, m_a_g2, m_a_k_k, m_a_k_a, m_a_r_k, m_a_lnx_w, m_a_lnx_b, m_a_w_out, m_kv_norm, m_kv_w, m_kv_k_norm, m_b_w_in, m_b_q_norm, m_b_w_out, m_mem_norm, m_mem_w_kv, m_mem_q_norm, m_mem_k_norm, m_ffn_norm, m_ffn_w_up, m_ffn_conv_w, m_ffn_conv_b, m_ffn_w_down)
    v_vals = (v_attn_norm, v_a_w_in, v_a_mu, v_a_w0, v_a_w2, v_a_a0, v_a_a2, v_a_g2, v_a_k_k, v_a_k_a, v_a_r_k, v_a_lnx_w, v_a_lnx_b, v_a_w_out, v_kv_norm, v_kv_w, v_kv_k_norm, v_b_w_in, v_b_q_norm, v_b_w_out, v_mem_norm, v_mem_w_kv, v_mem_q_norm, v_mem_k_norm, v_ffn_norm, v_ffn_w_up, v_ffn_conv_w, v_ffn_conv_b, v_ffn_w_down)
    W, M, V = dict(zip(names, vals)), dict(zip(names, m_vals)), dict(zip(names, v_vals))
    layers = lambda D, n: [D[n]] if D[n].ndim == 2 else [D[n][i] for i in range(D[n].shape[0])]
    ax2 = lambda n: AXIS[n] - (W[n].ndim - 2)
    later =[(n, i) for n in BIG if n != "a_w_in" for i in range(len(layers(W, n)))]

    small_shapes = [W[n].shape for n in SMALL_SHARDED]
    got_w, got_small = exchange(Exchange(gathers=[W["a_w_in"][0].astype(BF16), _pack([W[n] for n in SMALL_SHARDED])]),
                                "gather_first")
    P = {n: W[n] for n in SMALL}
    P["a_w_in"] = [_from_shards(got_w, ax2("a_w_in"))]
    for n, s in zip(SMALL_SHARDED, _unpack(got_small, small_shapes, lead=1)):
        P[n] = _from_shards(s, AXIS[n])
    ex_weights = Exchange(gathers=[layers(W, n)[i].astype(BF16) for n, i in later])

    def weights_done(got):
        out = {}
        for (n, _), g in zip(later, got):
            out.setdefault(n, []).append(_from_shards(g, ax2(n)))
        return out

    slots = lambda G, n: jnp.stack([_to_shards(g, ax2(n)) for g in G[n]], axis=1)
    later_names = [n for n in BIG if n != "a_w_in"]
    ex_grads = lambda G: Exchange(scatters=[slots(G, n) for n in later_names])
    loss_part, dx0, G = _local_step(x[0], mem[0], loss_target[0], P, ex_weights, weights_done, ex_grads)
    gparts = dict(zip(later_names, G.pop("_exchanged")))
    replicated = [n for n in SMALL if AXIS[n] is None]
    small_slots = _pack([_to_shards(G[n], AXIS[n]) for n in SMALL_SHARDED], lead=1)
    got_rep, gparts["a_w_in"], got_sharded = exchange(
        Exchange(gathers=[_pack([G[n] for n in replicated] + [loss_part[0:1, 0:1]])], scatters=[slots(G, "a_w_in"), small_slots]),
        "exchange_last")

    results = {}
    for n in BIG:
        rows = lambda z: z.reshape((-1,) + z.shape[-1:])
        res = adamw(gparts[n].reshape((N_DEV, -1) + gparts[n].shape[-1:]), rows(W[n]), rows(M[n]), rows(V[n]), f"adamw_{n}")
        results[n] = [r.reshape(W[n].shape) for r in res]
    *rep_sums, loss = _unpack(sum_parts(got_rep, "sum_replicated_grads"), [W[n].shape for n in replicated] + [()])
    g_mine = dict(zip(replicated, rep_sums))
    g_mine.update(zip(SMALL_SHARDED, _unpack(sum_parts(got_sharded, "sum_small_sharded_grads"), [W[n].shape for n in SMALL_SHARDED])))
    res = adamw(_pack([g_mine[n] for n in SMALL])[None], _pack([W[n] for n in SMALL]), _pack([M[n] for n in SMALL]),
                _pack([V[n] for n in SMALL]), "adamw_small")
    for n, parts in zip(SMALL, zip(*[_unpack(r, [W[n].shape for n in SMALL]) for r in res])):
        results[n] = list(parts)
    outs = [[results[n][j] for n in names] for j in range(4)]
    return (loss, dx0[None], *outs[0], *outs[1], *outs[2], *outs[3])
```

```python
import functools
import math

import jax
import jax.numpy as jnp
import numpy as np
from jax import lax
from jax.experimental import pallas as pl
from jax.experimental.pallas import tpu as pltpu

F32 = jnp.float32
BF16 = jnp.bfloat16
H3 =lax.Precision.HIGH

N_DEV = 8
D_MODEL = 1024
HEAD_DIM = 64
N_MEM = 256
MEM_HEADS = 4
MEM_WIDTH = 256
RWKV_HEADS = 12
RWKV_WIDTH = 768
SHIFT_WIDTH = 2560
DIL_GROUPS = ((128, 1), (512, 4), (2048, 16))
DIL_BLOCK = 128
DIL_WIDTH = 768
D_FF = 2816
RMS_EPS = 1e-6
LNX_EPS = 64e-5
NEG_INF = -1e30
ROPE_THETA = 10000.0
ADAM_LR, ADAM_B1, ADAM_B2, ADAM_EPS, ADAM_WD, ADAM_STEP = 0.001, 0.9, 0.999, 1e-08, 0.01, 10

CHUNK = 64
SCAN_GROUPS_FWD, SCAN_GROUPS_BWD = 1, 1
MM_TILE_CAP = 1408
VMEM_LIMIT_V7X = 48 * 1024 * 1024


def _cparams(sem):
    return pltpu.CompilerParams(dimension_semantics=sem, vmem_limit_bytes=VMEM_LIMIT_V7X)


def _pick(n, cands):
    for c in cands:
        if n % c == 0:
            return c
    return n


def _tile(n, cap):
    if n <= cap:
        return n
    for d in range(cap - cap % 128, 0, -128):
        if n % d == 0:
            return d
    return n


def _dg(a, b, ca, cb, batch):
    dims = (((ca,), (cb,)), ((0,), (0,))) if batch else (((ca,), (cb,)), ((), ()))
    return lax.dot_general(a.astype(BF16), b.astype(BF16), dims, preferred_element_type=F32)


@jax.custom_vjp
def mm_nn(a, b):
    n = a.ndim
    return _dg(a, b, n - 1, n - 2, n == 3)


def _mm_nn_fwd(a, b):
    return mm_nn(a, b), (a, b)


def _mm_nn_bwd(res, g):
    a, b = res
    n = a.ndim
    return _dg(g, b, n - 1, n - 1, n == 3), _dg(a, g, n - 2, n - 2, n == 3)


mm_nn.defvjp(_mm_nn_fwd, _mm_nn_bwd)


@jax.custom_vjp
def mm_nt(a, b):
    n = a.ndim
    return _dg(a, b, n - 1, n - 1, n == 3)


def _mm_nt_fwd(a, b):
    return mm_nt(a, b), (a, b)


def _mm_nt_bwd(res, g):
    a, b = res
    n = a.ndim
    return _dg(g, b, n - 1, n - 2, n == 3), _dg(g, a, n - 2, n - 2, n == 3)


mm_nt.defvjp(_mm_nt_fwd, _mm_nt_bwd)


def mmh(a, b):
    n = a.ndim
    dims = (((n - 1,), (n - 2,)), ((0,), (0,))) if n == 3 else (((1,), (0,)), ((), ()))
    return lax.dot_general(a, b, dims, precision=H3, preferred_element_type=F32)


def mmh_nt(a, b):
    n = a.ndim
    dims = (((n - 1,), (n - 1,)), ((0,), (0,))) if n == 3 else (((1,), (1,)), ((), ()))
    return lax.dot_general(a, b, dims, precision=H3, preferred_element_type=F32)


def mmh_tn(a, b):
    n = a.ndim
    dims = (((n - 2,), (n - 2,)), ((0,), (0,))) if n == 3 else (((0,), (0,)), ((), ()))
    return lax.dot_general(a, b, dims, precision=H3, preferred_element_type=F32)


def matmul(a, b, mode, name, residual=None):
    out_dtype = BF16 if mode == "tn" else F32
    if mode == "nn":
        (M, K), (_, N) = a.shape, b.shape
    elif mode == "nt":
        (M, K), (N, _) = a.shape, b.shape
    else:
        (K, M), (_, N) = a.shape, b.shape
    tm = _tile(M, 2048 if mode == "nn" else MM_TILE_CAP)
    tn = _tile(N, 512 if mode == "nn" else MM_TILE_CAP)
    tk = _tile(K, MM_TILE_CAP)
    nk = K // tk
    if mode == "nn":
        a_spec = pl.BlockSpec((tm, tk), lambda i, j, k: (i, k))
        b_spec = pl.BlockSpec((tk, tn), lambda i, j, k: (k, j))
        dims = (((1,), (0,)), ((), ()))
    elif mode == "nt":
        a_spec = pl.BlockSpec((tm, tk), lambda i, j, k: (i, k))
        b_spec = pl.BlockSpec((tn, tk), lambda i, j, k: (j, k))
        dims = (((1,), (1,)), ((), ()))
    else:
        a_spec = pl.BlockSpec((tk, tm), lambda i, j, k: (k, i))
        b_spec = pl.BlockSpec((tk, tn), lambda i, j, k: (k, j))
        dims = (((0,), (0,)), ((), ()))
    o_spec = pl.BlockSpec((tm, tn), lambda i, j, k: (i, j))
    has_res = residual is not None

    def body(*refs):
        if has_res:
            a_ref, b_ref, r_ref, o_ref, acc_ref = refs
        else:
            a_ref, b_ref, o_ref, acc_ref = refs
        k = pl.program_id(2)

        @pl.when(k == 0)
        def _():
            acc_ref[...] = jnp.zeros_like(acc_ref)

        acc_ref[...] += lax.dot_general(a_ref[...].astype(BF16), b_ref[...].astype(BF16), dims,
                                        preferred_element_type=F32)

        @pl.when(k == nk - 1)
        def _():
            if has_res:
                o_ref[...] = (acc_ref[...] + r_ref[...]).astype(out_dtype)
            else:
                o_ref[...] = acc_ref[...].astype(out_dtype)

    ins = [a, b] + ([residual] if has_res else [])
    in_specs = [a_spec, b_spec] + ([o_spec] if has_res else [])
    return pl.pallas_call(
        body, grid=(M // tm, N // tn, nk), in_specs=in_specs, out_specs=o_spec,
        out_shape=jax.ShapeDtypeStruct((M, N), out_dtype), scratch_shapes=[pltpu.VMEM((tm, tn), F32)],
        compiler_params=_cparams(("parallel", "parallel", "arbitrary")), name=name)(*ins)


class Cols:
    def __init__(self, arr, width, idx):
        self.arr, self.width, self.idx = arr, width, idx


def _arr(x):
    return x.arr if isinstance(x, Cols) else x


def _shape(x):
    return x.arr.shape[:-1] + (x.width,) if isinstance(x, Cols) else x.shape


def _col(x):
    return x.idx if isinstance(x, Cols) else 0


def _tok_spec(x, tile):
    shape, col = _shape(x), _col(x)
    return pl.BlockSpec(shape[:-2] + (tile, shape[-1]), lambda i: (0,) * (len(shape) - 2) + (i, col))


def _full_spec(x):
    shape, col = _shape(x), _col(x)
    return pl.BlockSpec(shape, lambda i: (0,) * (len(shape) - 1) + (col,))


def _halo_spec(x, tile):
    shape, col = _shape(x), _col(x)
    return pl.BlockSpec((8, shape[-1]), lambda i: (jnp.maximum(i * (tile // 8) - 1, 0), col))


def _blk(x, tile):
    shape = _shape(x)
    return jax.ShapeDtypeStruct(shape[:-2] + (tile, shape[-1]), _arr(x).dtype)


def _prev_rows(x, halo):
    rows = lax.broadcasted_iota(jnp.int32, (x.shape[0], 1), 0)
    before = jnp.where(pl.program_id(0) > 0, halo[7:8], 0.0)
    return jnp.where(rows == 0, before, pltpu.roll(x, 1, 0))


class Into:
    def __init__(self, buf, total, idx):
        self.buf, self.total, self.idx = buf, total, idx

    def place(self, shape, tile):
        idx = self.idx
        return shape.update(shape=(shape.shape[0], self.total)), pl.BlockSpec((tile, shape.shape[1]), lambda i: (i, idx))

    def operand(self, n_in, n_out_index):
        if self.buf is None:
            return [], [], {}
        return [self.buf], [_ANY], {n_in: n_out_index}


def stage_fwd(f, xs, ps, cts, cfs, tile, name, out_dtypes=None, with_prev=False, into=None):
    xs, ps, cts, cfs = list(xs), list(ps), list(cts), list(cfs)
    halos = xs if with_prev else []
    nx, nh, nct, np_ = len(xs), len(halos), len(cts), len(ps)
    T = _shape(xs[0])[-2]
    blk = [_blk(x, tile) for x in xs]
    out_avals = jax.eval_shape(f, *blk, *(blk if with_prev else []), *[_blk(p, _shape(p)[-2]) for p in ps],
                               *[_blk(c, tile) for c in cts], *[_blk(c, _shape(c)[-2]) for c in cfs])
    if out_dtypes is None:
        out_dtypes = [o.dtype for o in out_avals]
    out_shape = [jax.ShapeDtypeStruct(o.shape[:-2] + (T, o.shape[-1]), dt) for o, dt in zip(out_avals, out_dtypes)]
    out_specs = [_tok_spec(o, tile) for o in out_shape]
    n_in = nx + nh + nct + np_ + len(cfs)
    extra, extra_specs, alias = [], [], {}
    if into is not None:
        out_shape[0], out_specs[0] = into.place(out_shape[0], tile)
        extra, extra_specs, alias = into.operand(n_in, 0)

    def body(*refs):
        vals = [r[...] for r in refs[:n_in]]
        xv, hv, rest = vals[:nx], vals[nx:nx + nh], vals[nx + nh:]
        ctv, pv, cfv = rest[:nct], rest[nct:nct + np_], rest[nct + np_:]
        prev = [_prev_rows(x, h) for x, h in zip(xv, hv)]
        res = f(*xv, *prev, *pv, *ctv, *cfv)
        for o_ref, r in zip(refs[n_in + len(extra):], res):
            o_ref[...] = r.astype(o_ref.dtype)

    return pl.pallas_call(
        body, grid=(T // tile,),
        in_specs=([_tok_spec(x, tile) for x in xs] + [_halo_spec(x, tile) for x in halos] + [_tok_spec(c, tile) for c in cts]
                  + [_full_spec(p) for p in ps + cfs] + extra_specs),
        out_specs=out_specs, out_shape=out_shape, input_output_aliases=alias,
        compiler_params=_cparams(("parallel",)), name=name)(*[_arr(a) for a in xs + halos + cts + ps + cfs], *extra)


def stage_bwd(f, xs, ps, cts, cfs, gs, tile, name, bf16_copies=(), with_prev=False, copy_into=None):
    xs, ps, cts, cfs = list(xs), list(ps), list(cts), list(cfs)
    gs = [list(g) if isinstance(g, (list, tuple)) else [g] for g in gs]
    g_flat = [a for g in gs for a in g]
    halos = xs if with_prev else []
    nx, nh, nct, ng, np_ = len(xs), len(halos), len(cts), len(g_flat), len(ps)
    T = _shape(xs[0])[-2]
    dx_like = xs + halos
    out_shape = ([jax.ShapeDtypeStruct(_shape(x), F32) for x in dx_like] + [jax.ShapeDtypeStruct(_shape(p), F32) for p in ps]
                 + [jax.ShapeDtypeStruct(_shape(xs[i]), BF16) for i in bf16_copies])
    n_in = nx + nh + nct + ng + np_ + len(cfs)
    ndx = nx + nh
    plain = lambda x: jax.ShapeDtypeStruct(_shape(x), F32)
    out_specs = ([_tok_spec(plain(x), tile) for x in dx_like] + [_full_spec(plain(p)) for p in ps]
                 + [_tok_spec(plain(xs[i]), tile) for i in bf16_copies])
    extra, extra_specs, alias = [], [], {}
    if copy_into is not None:
        out_shape[ndx + np_], out_specs[ndx + np_] = copy_into.place(out_shape[ndx + np_], tile)
        extra, extra_specs, alias = copy_into.operand(n_in, ndx + np_)

    def body(*refs):
        vals = [r[...] for r in refs[:n_in]]
        outs = refs[n_in + len(extra):]
        xv, hv, rest = vals[:nx], vals[nx:nx + nh], vals[nx + nh:]
        ctv, gparts, pv, cfv = rest[:nct], rest[nct:nct + ng], rest[nct + ng:nct + ng + np_], rest[nct + ng + np_:]
        gv = []
        for g in gs:
            gv.append(functools.reduce(lambda a, b: a + b, gparts[:len(g)]))
            gparts = gparts[len(g):]
        prev = [_prev_rows(x, h) for x, h in zip(xv, hv)]
        _, vjp = jax.vjp(lambda *xp: f(*xp, *ctv, *cfv), *xv, *prev, *pv)
        d = vjp(tuple(gv))
        for o_ref, r in zip(outs[:ndx], d[:ndx]):
            o_ref[...] = r
        for o_ref, i in zip(outs[ndx + np_:], bf16_copies):
            o_ref[...] = d[i].astype(BF16)

        @pl.when(pl.program_id(0) == 0)
        def _():
            for o_ref in outs[ndx:ndx + np_]:
                o_ref[...] = jnp.zeros_like(o_ref)

        for o_ref, r in zip(outs[ndx:ndx + np_], d[ndx:]):
            o_ref[...] += r

    res = pl.pallas_call(
        body, grid=(T // tile,),
        in_specs=([_tok_spec(x, tile) for x in xs] + [_halo_spec(x, tile) for x in halos]
                  + [_tok_spec(c, tile) for c in cts + g_flat] + [_full_spec(p) for p in ps + cfs] + extra_specs),
        out_specs=out_specs, out_shape=out_shape, input_output_aliases=alias,
        compiler_params=_cparams(("arbitrary",)), name=name)(*[_arr(a) for a in xs + halos + cts + g_flat + ps + cfs], *extra)
    if bf16_copies:
        return list(res[:ndx]), list(res[ndx:ndx + np_]), list(res[ndx + np_:])
    return list(res[:ndx]), list(res[ndx:])


def _rms(x, g, eps=RMS_EPS):
    return x * lax.rsqrt(jnp.mean(x * x, axis=-1, keepdims=True) + eps) * g


def f_rmsnorm(x, g):
    return (_rms(x, g),)


def f_rmsnorm_res(x, g):
    return _rms(x, g), x


def f_rmsnorm2(x, g1, g2):
    n = x * lax.rsqrt(jnp.mean(x * x, axis=-1, keepdims=True) + RMS_EPS)
    return n * g1, n * g2


def f_rmsnorm2_res(x, g1, g2):
    return f_rmsnorm2(x, g1, g2) + (x,)


def _sigmoid(x):
    return 1.0 / (1.0 + jnp.exp(-x))


def _softplus(x):
    return jnp.maximum(x, 0.0) + jnp.log(1.0 + jnp.exp(-jnp.abs(x)))


def f_rwkv_pre(pr, pk, pv, pl_, qr, qk, qv, ql, mu_r, mu_k, mu_v, mu_l, w0, w2, a0, a2, g2, k_k, k_a, seg, seg_t):
    xr = pr + (qr - pr) * mu_r
    xk = pk + (qk - pk) * mu_k
    xv = pv + (qv - pv) * mu_v
    xl = pl_ + (ql - pl_) * mu_l
    w_log = -_softplus(-(w0 + mm_nn(jnp.tanh(xl), w2))) - 0.5
    lw = -jnp.exp(w_log)
    a = _sigmoid(a0 + mm_nn(xl, a2))
    g = mm_nn(_sigmoid(xl), g2)
    kkr = xk * k_k
    inv = lax.rsqrt(jnp.maximum(mmh(kkr * kkr, seg), 1e-24))
    kk = kkr * mmh(inv, seg_t)
    k2 = xk * (1.0 + (a - 1.0) * k_a)
    return xr, lw, k2, xv, kk, kk * a, g


def f_rwkv_post(y, r, k2, v, g, lnx_w, lnx_b, r_k, seg, seg_t):
    inv_n = 1.0 / HEAD_DIM
    m = mmh(mmh(y, seg) * inv_n, seg_t)
    yc = y - m
    rstd = lax.rsqrt(mmh(yc * yc, seg) * inv_n + LNX_EPS)
    yn = yc * mmh(rstd, seg_t) * lnx_w + lnx_b
    bonus = mmh(mmh(r * k2 * r_k, seg), seg_t) * v
    return ((yn + bonus) * g,)


def _headnorm(z, g, seg, seg_t):
    ms = mmh(z * z, seg) * (1.0 / HEAD_DIM)
    return z * mmh(lax.rsqrt(ms + RMS_EPS), seg_t) * g


def f_headnorm(z, g, seg, seg_t):
    return (_headnorm(z, g, seg, seg_t),)


def _rot_half(z):
    w = z.shape[1]
    half = HEAD_DIM // 2
    lane = lax.broadcasted_iota(jnp.int32, (1, w), 1)
    return jnp.where((lane & (HEAD_DIM - 1)) < half, -pltpu.roll(z, w - half, 1), pltpu.roll(z, half, 1))


@jax.custom_vjp
def _rotate_half(z):
    return _rot_half(z)


_rotate_half.defvjp(lambda z: (_rot_half(z), None), lambda _, g: (-_rot_half(g),))


def f_qkprep(z, g, cos, sin, seg, seg_t):
    zn = _headnorm(z, g, seg, seg_t)
    pairs = z.shape[1] // cos.shape[1]
    return (zn * jnp.tile(cos, (1, pairs)) + _rotate_half(zn) * jnp.tile(sin, (1, pairs)),)


def _head_mask(width, h):
    lane = lax.broadcasted_iota(jnp.int32, (1, width), 1)
    return jnp.where((lane >> 6) == h, jnp.ones((), F32), 0.0)


def f_memattn(q, k, v, q_norm, seg, seg_t):
    qn = _headnorm(q, q_norm, seg, seg_t)
    out = jnp.zeros_like(q)
    for h in range(MEM_HEADS):
        m = _head_mask(MEM_WIDTH, h)
        s = mm_nt(qn * m, k) * (1.0 / math.sqrt(HEAD_DIM))
        s = s - jnp.max(s, axis=-1, keepdims=True)
        p = jnp.exp(s)
        p = p / jnp.sum(p, axis=-1, keepdims=True)
        out = out + mm_nn(p, v) * m
    return (out,)


def f_mix(o1, o2, o3, l1, l2, l3):
    mx = jnp.maximum(jnp.maximum(l1, l2), l3)
    e1, e2, e3 = jnp.exp(l1 - mx), jnp.exp(l2 - mx), jnp.exp(l3 - mx)
    return ((e1 * o1 + e2 * o2 + e3 * o3) / (e1 + e2 + e3),)


def _chunk_masks(L):
    t = lax.broadcasted_iota(jnp.int32, (L, L), 0)
    s = lax.broadcasted_iota(jnp.int32, (L, L), 1)
    return t, s


def _unit_lower_inverse(a):
    L = a.shape[-1]
    t, s = _chunk_masks(L)
    one = jnp.ones((), F32)
    blk = lambda sh: jnp.where((t >> sh) == (s >> sh), one, 0.0)
    n0 = a * blk(3)
    x = jnp.where(t == s, one, 0.0) - n0
    n2 = mmh(n0, n0)
    x = x + mmh(x, n2)
    x = x + mmh(x, mmh(n2, n2))
    for sh in (3, 4, 5):
        if (1 << sh) >= L:
            break
        off = a * (blk(sh + 1) - blk(sh))
        x = x - mmh(x, mmh(off, x))
    return x


@jax.custom_vjp
def _inverse_known(a, x):
    return x


def _inverse_known_fwd(a, x):
    return x, x


def _inverse_known_bwd(x, dx):
    return -mmh_nt(mmh_tn(x, dx), x), jnp.zeros_like(x)


_inverse_known.defvjp(_inverse_known_fwd, _inverse_known_bwd)


def _running_sum(x, reverse):
    L = x.shape[1]
    pos = lax.broadcasted_iota(jnp.int32, (1, L, 1), 1)
    step = 1
    while step < L:
        if reverse:
            x = x + jnp.where(pos < L - step, pltpu.roll(x, L - step, 1), 0.0)
        else:
            x = x + jnp.where(pos >= step, pltpu.roll(x, step, 1), 0.0)
        step *= 2
    return x


@jax.custom_vjp
def _cumsum_tokens(x):
    return _running_sum(x, False)


_cumsum_tokens.defvjp(lambda x: (_running_sum(x, False), None), lambda _, g: (_running_sum(g, True),))


def f_rwkv_chunk(s0, r, lw, k, v, kk, b, x_known=None):
    H, L, _ = r.shape
    t, s = _chunk_masks(L)
    one = jnp.ones((), F32)
    incl = jnp.where(t >= s, one, 0.0)
    strict = jnp.where(t > s, one, 0.0)
    cum = _cumsum_tokens(lw)
    w_in = jnp.exp(cum)
    w_ex = jnp.exp(cum - lw)
    w_inv = jnp.exp(-cum)
    rt, kkt, kt, bt = r * w_in, kk * w_ex, k * w_inv, b * w_inv
    a_b = mmh_nt(kkt, bt) * strict
    a_k = mmh_nt(kkt, kt) * strict
    m_k = mmh_nt(rt, kt) * incl
    m_b = mmh_nt(rt, bt) * incl
    x = _unit_lower_inverse(a_b) if x_known is None else _inverse_known(a_b, x_known)
    u = mmh(x, mmh_nt(kkt, s0) + mmh(a_k, v))
    y = mmh_nt(rt, s0) + mmh(m_k, v) - mmh(m_b, u)
    w_last = jnp.exp(jnp.sum(lw, axis=1, keepdims=True))
    s1 = (s0 + mmh_tn(v, kt) - mmh_tn(u, bt)) * w_last
    return y, s1, x


def _ex_split(ex, refs, n_in, n_out):
    n = ex.n
    ins, ex_in = refs[:n_in], refs[n_in:n_in + n]
    outs, ex_out = refs[n_in + n:n_in + n + n_out], refs[n_in + n + n_out:n_in + 2 * n + n_out]
    rest = refs[n_in + 2 * n + n_out:]
    return ins, outs, rest[:len(rest) - 3], (ex_in, ex_out) + tuple(rest[len(rest) - 3:])


def _split_heads(x):
    return jnp.stack([x[:, h * HEAD_DIM:(h + 1) * HEAD_DIM] for h in range(x.shape[1] // HEAD_DIM)], axis=0)


def _merge_heads(x):
    return jnp.concatenate([x[h] for h in range(x.shape[0])], axis=1)


def rwkv_scan_fwd(r, lw, k, v, kk, b, ex):
    T, N = r.shape[0], HEAD_DIM
    H = r.shape[1] // N
    groups = SCAN_GROUPS_FWD
    nc, hg = T // CHUNK, H // groups
    seq = pl.BlockSpec((CHUNK, hg * N), lambda g, c: (c, g))

    def body(*refs):
        (r_ref, lw_ref, k_ref, v_ref, kk_ref, b_ref), (y_ref, hs_ref, xs_ref), (h_scr,), ex_refs = _ex_split(ex, refs, 6, 3)
        g, c = pl.program_id(0), pl.program_id(1)

        @pl.when(jnp.logical_and(g == 0, c == 0))
        def _():
            ex.start(*ex_refs)

        @pl.when(c == 0)
        def _():
            h_scr[...] = jnp.zeros_like(h_scr)

        h0 = h_scr[...]
        hs_ref[0] = h0
        y, h1, x = f_rwkv_chunk(h0, *[_split_heads(z[...]) for z in (r_ref, lw_ref, k_ref, v_ref, kk_ref, b_ref)])
        y_ref[...] = _merge_heads(y)
        xs_ref[0] = x
        h_scr[...] = h1

        @pl.when(jnp.logical_and(g == groups - 1, c == (3 * nc) // 4))
        def _():
            ex.forward(*ex_refs)

        @pl.when(jnp.logical_and(g == groups - 1, c == nc - 1))
        def _():
            ex.wait(*ex_refs)

    res = pl.pallas_call(
        body, grid=(groups, nc), in_specs=[seq] * 6 + [_ANY] * ex.n,
        out_specs=[seq, pl.BlockSpec((1, hg, N, N), lambda g, c: (c, g, 0, 0)),
                   pl.BlockSpec((1, hg, CHUNK, CHUNK), lambda g, c: (c, g, 0, 0))] + [_ANY] * ex.n,
        out_shape=[jax.ShapeDtypeStruct((T, H * N), F32), jax.ShapeDtypeStruct((nc, H, N, N), F32),
                   jax.ShapeDtypeStruct((nc, H, CHUNK, CHUNK), F32)] + ex.out_shape(),
        scratch_shapes=[pltpu.VMEM((hg, N, N), F32)] + ex.scratch(),
        compiler_params=_cparams(("arbitrary", "arbitrary")), name="rwkv_scan_fwd")(r, lw, k, v, kk, b, *ex.operands())
    return res[0], (res[1], res[2]), list(res[3:])


def rwkv_scan_bwd(r, lw, k, v, kk, b, saved, dy, ex):
    T, N = r.shape[0], HEAD_DIM
    H = r.shape[1] // N
    groups = SCAN_GROUPS_BWD
    nc, hg = T // CHUNK, H // groups
    seq = pl.BlockSpec((CHUNK, hg * N), lambda g, c: (nc - 1 - c, g))
    state = pl.BlockSpec((1, hg, N, N), lambda g, c: (nc - 1 - c, g, 0, 0))

    def body(*refs):
        (r_ref, lw_ref, k_ref, v_ref, kk_ref, b_ref, hs_ref, xs_ref, dy_ref), outs, (dh_scr,), ex_refs = _ex_split(ex, refs, 9, 6)
        g, c = pl.program_id(0), pl.program_id(1)

        @pl.when(jnp.logical_and(g == 0, c == 0))
        def _():
            ex.start(*ex_refs)

        @pl.when(c == 0)
        def _():
            dh_scr[...] = jnp.zeros_like(dh_scr)

        x_known = xs_ref[0]
        _, vjp = jax.vjp(lambda *a: f_rwkv_chunk(*a, x_known=x_known)[:2], hs_ref[0],
                         *[_split_heads(z[...]) for z in (r_ref, lw_ref, k_ref, v_ref, kk_ref, b_ref)])
        d = vjp((_split_heads(dy_ref[...]), dh_scr[...]))
        dh_scr[...] = d[0]
        for o_ref, dz in zip(outs, d[1:]):
            o_ref[...] = _merge_heads(dz)

        @pl.when(jnp.logical_and(g == groups - 1, c == nc - 1))
        def _():
            ex.forward(*ex_refs)
            ex.wait(*ex_refs)

    res = pl.pallas_call(
        body, grid=(groups, nc),
        in_specs=[seq] * 6 + [state, state, seq] + [_ANY] * ex.n,
        out_specs=[seq] * 6 + [_ANY] * ex.n, out_shape=[jax.ShapeDtypeStruct((T, H * N), F32)] * 6 + ex.out_shape(),
        scratch_shapes=[pltpu.VMEM((hg, N, N), F32)] + ex.scratch(),
        compiler_params=_cparams(("arbitrary", "arbitrary")), name="rwkv_scan_bwd")(r, lw, k, v, kk, b, *saved, dy, *ex.operands())
    return list(res[:6]), list(res[6:])


GROUP_COLS = 4 * HEAD_DIM


def _f_dilattn(has_prev, q, kc, kp, vc, vp):
    scale = 1.0 / math.sqrt(HEAD_DIM)
    i = lax.broadcasted_iota(jnp.int32, (DIL_BLOCK, DIL_BLOCK), 0)
    j = lax.broadcasted_iota(jnp.int32, (DIL_BLOCK, DIL_BLOCK), 1)
    o, l = jnp.zeros_like(q), jnp.zeros_like(q)
    for h in range(q.shape[1] // HEAD_DIM):
        m = _head_mask(q.shape[1], h)
        sc = jnp.where(j <= i, mm_nt(q * m, kc) * scale, NEG_INF)
        sp = jnp.where(jnp.logical_and(i <= j, has_prev), mm_nt(q * m, kp) * scale, NEG_INF)
        mx = jnp.maximum(jnp.max(sc, axis=-1, keepdims=True), jnp.max(sp, axis=-1, keepdims=True))
        pc, pp = jnp.exp(sc - mx), jnp.exp(sp - mx)
        den = jnp.sum(pc, axis=-1, keepdims=True) + jnp.sum(pp, axis=-1, keepdims=True)
        o = o + (mm_nn(pc, vc) + mm_nn(pp, vp)) / den * m
        l = l + (mx + jnp.log(den)) * m
    return o, l


def _dil_specs(gi, d):
    parts = 1 if d == 1 else 2
    blk = (DIL_BLOCK * d, GROUP_COLS // parts)
    at = lambda col: (lambda p, n: (n, col * parts + p))
    before = lambda col: (lambda p, n: (jnp.maximum(n - 1, 0), col * parts + p))
    v0 = DIL_WIDTH // GROUP_COLS + gi
    q = pl.BlockSpec(blk, at(gi))
    kc, kp = pl.BlockSpec(blk, at(gi)), pl.BlockSpec(blk, before(gi))
    vc, vp = pl.BlockSpec(blk, at(v0)), pl.BlockSpec(blk, before(v0))
    out = pl.BlockSpec(blk, at(0))
    together = min(d, 2)
    return (q, kc, kp, vc, vp, out), parts, together


def _residue_rows(r, d):
    return pl.ds(r, DIL_BLOCK, stride=d) if d > 1 else pl.ds(0, DIL_BLOCK)


def dil_fwd(q, k, kv, gi, d, name):
    T = q.shape[0]
    (qs, kc, kp, vc, vp, out), parts, together = _dil_specs(gi, d)

    def body(q_ref, kc_ref, kp_ref, vc_ref, vp_ref, o_ref, l_ref):
        has_prev = pl.program_id(1) > 0

        def residues(it, carry):
            rows = [_residue_rows(it * together + a, d) for a in range(together)]
            ins = [[ref[rw, :] for ref in (q_ref, kc_ref, kp_ref, vc_ref, vp_ref)] for rw in rows]
            res = [_f_dilattn(has_prev, *x) for x in ins]
            for rw, (o, l) in zip(rows, res):
                o_ref[rw, :] = o
                l_ref[rw, :] = l
            return carry

        lax.fori_loop(0, d // together, residues, 0)

    shape = jax.ShapeDtypeStruct((T, 4 * HEAD_DIM), F32)
    return pl.pallas_call(
        body, grid=(parts, T // (DIL_BLOCK * d)), in_specs=[qs, kc, kp, vc, vp], out_specs=[out, out], out_shape=[shape, shape],
        compiler_params=_cparams(("parallel", "parallel")), name=name)(q, k, k, kv, kv)


def dil_bwd(q, k, kv, do, dl, gi, d, name):
    T = q.shape[0]
    (qs, kc, kp, vc, vp, out), parts, together = _dil_specs(gi, d)

    def body(q_ref, kc_ref, kp_ref, vc_ref, vp_ref, do_ref, dl_ref, *outs):
        f = functools.partial(_f_dilattn, pl.program_id(1) > 0)

        def residues(it, carry):
            rows = [_residue_rows(it * together + a, d) for a in range(together)]
            ins = [[ref[rw, :] for ref in (q_ref, kc_ref, kp_ref, vc_ref, vp_ref, do_ref, dl_ref)] for rw in rows]
            res = [jax.vjp(f, *x[:5])[1]((x[5], x[6])) for x in ins]
            for rw, gs in zip(rows, res):
                for o_ref, g in zip(outs, gs):
                    o_ref[rw, :] = g
            return carry

        lax.fori_loop(0, d // together, residues, 0)

    shape = jax.ShapeDtypeStruct((T, 4 * HEAD_DIM), F32)
    dq, dkc, dkp, dvc, dvp = pl.pallas_call(
        body, grid=(parts, T // (DIL_BLOCK * d)), in_specs=[qs, kc, kp, vc, vp, out, out], out_specs=[out] * 5, out_shape=[shape] * 5,
        compiler_params=_cparams(("parallel", "parallel")), name=name)(q, k, k, kv, kv, do, dl)

    def own_plus_next(c, p):
        return c + jnp.concatenate([p[DIL_BLOCK * d:], jnp.zeros_like(p[:DIL_BLOCK * d])], axis=0)

    return dq, own_plus_next(dkc, dkp), own_plus_next(dvc, dvp)


CONV_TILE = 256


def _conv3(before, u, w, b):
    ue = jnp.concatenate([before, u], axis=0)
    s1, s2 = pltpu.roll(ue, 1, 0)[8:], pltpu.roll(ue, 2, 0)[8:]
    return b + w[0:1] * s2 + w[1:2] * s1 + w[2:3] * u, s1, s2


def _conv_halves(u_ref, h_ref, cw_ref, cb_ref):
    F = D_FF
    res = []
    for lo in (0, F):
        before = jnp.where(pl.program_id(0) > 0, h_ref[:, lo:lo + F], 0.0)
        u = u_ref[:, lo:lo + F]
        res.append((u,) + _conv3(before, u, cw_ref[:, lo:lo + F], cb_ref[:, lo:lo + F]))
    return res


def _halo_before(C):
    return pl.BlockSpec((8, C), lambda i: (jnp.maximum(i * (CONV_TILE // 8) - 1, 0), 0))


def convgate_fwd(u, cw, cb, name):
    T, C = u.shape
    F = C // 2

    def body(u_ref, h_ref, cw_ref, cb_ref, z_ref):
        (_, cg, _, _), (_, cv, _, _) = _conv_halves(u_ref, h_ref, cw_ref, cb_ref)
        z_ref[...] = (cg * _sigmoid(cg) * cv).astype(BF16)

    return pl.pallas_call(
        body, grid=(T // CONV_TILE,),
        in_specs=[pl.BlockSpec((CONV_TILE, C), lambda i: (i, 0)), _halo_before(C), _full_spec(cw), _full_spec(cb)],
        out_specs=pl.BlockSpec((CONV_TILE, F), lambda i: (i, 0)), out_shape=jax.ShapeDtypeStruct((T, F), BF16),
        compiler_params=_cparams(("parallel",)), name=name)(u, u, cw, cb)


def convgate_bwd(u, cw, cb, dz, name):
    T, C = u.shape
    F = C // 2
    n = T // CONV_TILE
    E = CONV_TILE + 8

    def body(u_ref, hb_ref, ha_ref, cw_ref, cb_ref, dz_ref, dza_ref, du_ref, dcw_ref, dcb_ref):
        i = pl.program_id(0)
        dze = jnp.concatenate([dz_ref[...], jnp.where(i < n - 1, dza_ref[...], 0.0)], axis=0)

        @pl.when(i == 0)
        def _():
            dcw_ref[...] = jnp.zeros_like(dcw_ref)
            dcb_ref[...] = jnp.zeros_like(dcb_ref)

        halves = []
        for lo in (0, F):
            sl = slice(lo, lo + F)
            ue = jnp.concatenate([u_ref[:, sl], ha_ref[:, sl]], axis=0)
            c, s1, s2 = _conv3(jnp.where(i > 0, hb_ref[:, sl], 0.0), ue, cw_ref[:, sl], cb_ref[:, sl])
            halves.append((sl, ue, c, s1, s2))
        (_, _, cg, _, _), (_, _, cv, _, _) = halves
        sg = _sigmoid(cg)
        dcs = (dze * cv * sg * (1.0 + cg * (1.0 - sg)), dze * cg * sg)
        for (sl, ue, _, s1, s2), dc in zip(halves, dcs):
            own = lambda z: z[:CONV_TILE]
            dcb_ref[:, sl] += jnp.sum(own(dc), axis=0, keepdims=True)
            dcw_ref[0:1, sl] += jnp.sum(own(dc * s2), axis=0, keepdims=True)
            dcw_ref[1:2, sl] += jnp.sum(own(dc * s1), axis=0, keepdims=True)
            dcw_ref[2:3, sl] += jnp.sum(own(dc * ue), axis=0, keepdims=True)
            du = cw_ref[2:3, sl] * dc + cw_ref[1:2, sl] * pltpu.roll(dc, E - 1, 0) + cw_ref[0:1, sl] * pltpu.roll(dc, E - 2, 0)
            du_ref[:, sl] = own(du).astype(BF16)

    after = lambda w: pl.BlockSpec((8, w), lambda i: (jnp.minimum((i + 1) * (CONV_TILE // 8), T // 8 - 1), 0))
    return pl.pallas_call(
        body, grid=(n,),
        in_specs=[pl.BlockSpec((CONV_TILE, C), lambda i: (i, 0)), _halo_before(C), after(C), _full_spec(cw), _full_spec(cb),
                  pl.BlockSpec((CONV_TILE, F), lambda i: (i, 0)), after(F)],
        out_specs=[pl.BlockSpec((CONV_TILE, C), lambda i: (i, 0)), _full_spec(cw), _full_spec(cb)],
        out_shape=[jax.ShapeDtypeStruct((T, C), BF16), jax.ShapeDtypeStruct(cw.shape, F32), jax.ShapeDtypeStruct(cb.shape, F32)],
        compiler_params=_cparams(("arbitrary",)), name=name)(u, u, u, cw, cb, dz, dz)


def loss_head(y, tgt):
    T, D = y.shape
    tile = ROW_TILE

    def body(y_ref, t_ref, l_ref, d_ref, db_ref):
        d = y_ref[...] - t_ref[...]
        d_ref[...] = d * (1.0 / D)
        db_ref[...] = (d * (1.0 / D)).astype(BF16)

        @pl.when(pl.program_id(0) == 0)
        def _():
            l_ref[...] = jnp.zeros_like(l_ref)

        l_ref[...] += (0.5 / D) * jnp.sum(d * d)

    row = pl.BlockSpec((tile, D), lambda i: (i, 0))
    return pl.pallas_call(
        body, grid=(T // tile,), in_specs=[row, row], out_specs=[pl.BlockSpec((8, 128), lambda i: (0, 0)), row, row],
        out_shape=[jax.ShapeDtypeStruct((8, 128), F32), jax.ShapeDtypeStruct((T, D), F32), jax.ShapeDtypeStruct((T, D), BF16)],
        compiler_params=_cparams(("arbitrary",)), name="loss_head")(y, tgt)


def sum_parts(parts, name):
    S, R, C = parts.shape
    tile = _pick(R, (256, 128, 64, 32, 16, 8))

    def body(p_ref, o_ref):
        acc = p_ref[0].astype(F32)
        for s in range(1, S):
            acc = acc + p_ref[s].astype(F32)
        o_ref[...] = acc

    return pl.pallas_call(
        body, grid=(R // tile,), in_specs=[pl.BlockSpec((S, tile, C), lambda i: (0, i, 0))],
        out_specs=pl.BlockSpec((tile, C), lambda i: (i, 0)), out_shape=jax.ShapeDtypeStruct((R, C), F32),
        compiler_params=_cparams(("parallel",)), name=name)(parts)


def adamw(gparts, w, m, v, name):
    S, R, C = gparts.shape
    tile = _pick(R, (256, 128, 64, 32, 16, 8))
    c1 = 1.0 / (1.0 - ADAM_B1 ** ADAM_STEP)
    c2 = 1.0 / (1.0 - ADAM_B2 ** ADAM_STEP)

    def body(g_ref, w_ref, m_ref, v_ref, go_ref, d_ref, mo_ref, vo_ref):
        g = g_ref[0].astype(F32)
        for s in range(1, S):
            g = g + g_ref[s].astype(F32)
        m1 = ADAM_B1 * m_ref[...] + (1.0 - ADAM_B1) * g
        v1 = ADAM_B2 * v_ref[...] + (1.0 - ADAM_B2) * (g * g)
        go_ref[...] = g
        mo_ref[...] = m1
        vo_ref[...] = v1
        d_ref[...] = -ADAM_LR * ((m1 * c1) / (jnp.sqrt(v1 * c2) + ADAM_EPS) + ADAM_WD * w_ref[...])

    row = pl.BlockSpec((tile, C), lambda i: (i, 0))
    return pl.pallas_call(
        body, grid=(R // tile,), in_specs=[pl.BlockSpec((S, tile, C), lambda i: (0, i, 0)), row, row, row],
        out_specs=[row] * 4, out_shape=[jax.ShapeDtypeStruct((R, C), F32)] * 4,
        compiler_params=_cparams(("parallel",)), name=name)(gparts, w, m, v)


def _peers():
    x, y, c = lax.axis_index("x"), lax.axis_index("y"), lax.axis_index("c")
    peers = []
    for k in range(1, N_DEV):
        px = 1 - x if k & 4 else x
        py = 1 - y if k & 2 else y
        pc = 1 - c if k & 1 else c
        peers.append(((px, py, pc), 4 * px + 2 * py + pc))
    return 4 * x + 2 * y + c, peers


_ANY = pl.BlockSpec(memory_space=pl.ANY)


class Exchange:
    def __init__(self, gathers=(), scatters=()):
        self.gathers, self.scatters = list(gathers), list(scatters)
        self.n = len(self.gathers) + len(self.scatters)

    def operands(self):
        return self.gathers + self.scatters

    def out_shape(self):
        return ([jax.ShapeDtypeStruct((N_DEV,) + x.shape, x.dtype) for x in self.gathers]
                + [jax.ShapeDtypeStruct(x.shape, x.dtype) for x in self.scatters])

    def scratch(self):
        n = max(self.n, 1)
        return [pltpu.SemaphoreType.DMA((7 * n,)), pltpu.SemaphoreType.DMA((7 * n,)), pltpu.SemaphoreType.DMA((n,))]

    def _copies(self, in_refs, out_refs, send_sems, recv_sems, local_sems):
        me, peers = _peers()
        ng = len(self.gathers)
        local, sends, recvs = [], [], []
        for a in range(self.n):
            x, o = in_refs[a], out_refs[a]
            mine = x if a < ng else x.at[me]
            local.append(pltpu.make_async_copy(mine, o.at[me], local_sems.at[a]))
            s_a, r_a = {}, {}
            for k in range(1, N_DEV):
                peer, slot = peers[k - 1]
                sems = dict(send_sem=send_sems.at[7 * a + k - 1], recv_sem=recv_sems.at[7 * a + k - 1],
                            device_id_type=pl.DeviceIdType.MESH)
                if a >= ng:
                    s_a[k] = pltpu.make_async_remote_copy(src_ref=x.at[slot], dst_ref=o.at[me], device_id=peer, **sems)
                elif k in FORWARDED:
                    came = o.at[peers[k - 2][1]]
                    s_a[k] = pltpu.make_async_remote_copy(src_ref=came, dst_ref=came, device_id=peers[0][0], **sems)
                else:
                    s_a[k] = pltpu.make_async_remote_copy(src_ref=x, dst_ref=o.at[me], device_id=peer, **sems)
                r_a[k] = pltpu.make_async_remote_copy(src_ref=mine, dst_ref=o.at[slot], device_id=peer, **sems)
            sends.append(s_a)
            recvs.append(r_a)
        return local, sends, recvs

    def start(self, *refs):
        if self.n == 0:
            return
        local, sends, _ = self._copies(*refs)
        for a in range(self.n):
            local[a].start()
            for k in range(1, N_DEV):
                if a >= len(self.gathers) or k not in FORWARDED:
                    sends[a][k].start()

    def forward(self, *refs):
        if not self.gathers:
            return
        _, sends, recvs = self._copies(*refs)
        for a in range(len(self.gathers)):
            for k in FORWARDED:
                recvs[a][k - 1].wait_recv()
                sends[a][k].start()

    def wait(self, *refs):
        if self.n == 0:
            return
        local, sends, recvs = self._copies(*refs)
        for a in range(self.n):
            waited_early = [f - 1 for f in FORWARDED] if a < len(self.gathers) else []
            for k in range(1, N_DEV):
                if k not in waited_early:
                    recvs[a][k].wait_recv()
            for k in range(1, N_DEV):
                sends[a][k].wait_send()
            local[a].wait()


FORWARDED = (3, 5, 7)


def exchange(ex, name):
    n = ex.n

    def body(*refs):
        args = (refs[:n], refs[n:2 * n]) + tuple(refs[2 * n:])
        ex.start(*args)
        ex.forward(*args)
        ex.wait(*args)

    return pl.pallas_call(body, in_specs=[_ANY] * n, out_specs=[_ANY] * n, out_shape=ex.out_shape(),
                          scratch_shapes=ex.scratch(), name=name)(*ex.operands())


def _shift_up(z):
    return jnp.concatenate([z[1:], jnp.zeros_like(z[:1])], axis=0)


def _segments(width):
    seg = np.zeros((width, 128), np.float32)
    seg[np.arange(width), np.arange(width) // HEAD_DIM] = 1.0
    return jnp.asarray(seg), jnp.asarray(seg.T)


def _rope_consts(T):
    inv = ROPE_THETA ** (-jnp.arange(0, HEAD_DIM, 2, dtype=F32) / HEAD_DIM)
    ang = jnp.arange(T, dtype=F32)[:, None] * inv[None, :]
    return jnp.tile(jnp.cos(ang), (1, 4)), jnp.tile(jnp.sin(ang), (1, 4))


def _per_head(g, heads):
    return jnp.tile(g.reshape(1, HEAD_DIM), (1, heads))


def _sum_heads(g):
    return g.reshape(-1, HEAD_DIM).sum(axis=0, keepdims=True)


LORA_COLS = 256
RW_TILE = 256
ROW_TILE = 512


def _local_step(x0, memx, tgt, P, ex_weights=None, weights_done=None, ex_grads=None):
    T = x0.shape[0]
    P = dict(P)
    G = {}
    seg, seg_t = _segments(RWKV_WIDTH)
    mseg = (seg[:MEM_WIDTH], seg_t[:, :MEM_WIDTH])
    cos, sin = _rope_consts(T)
    row = lambda v: v.reshape(1, -1)

    def mem_fwd(i, q, into):
        memn = stage_fwd(f_rmsnorm, [memx], [P["mem_norm"][i:i + 1]], [], [], N_MEM, f"mem{i}_norm", [BF16])[0]
        kvm = matmul(memn, P["mem_w_kv"][i], "nn", f"mem{i}_kv")
        kn, qn = _per_head(P["mem_k_norm"][i], MEM_HEADS), _per_head(P["mem_q_norm"][i], MEM_HEADS)
        km = stage_fwd(f_headnorm, [Cols(kvm, MEM_WIDTH, 0)], [kn], [], mseg, N_MEM, f"mem{i}_knorm")[0]
        om = stage_fwd(f_memattn, [q], [km, Cols(kvm, MEM_WIDTH, 1), qn], [], mseg, ROW_TILE, f"mem{i}_attn", [BF16], into=into)[0]
        return om, (memn, kvm, km, kn, qn, q)

    def mem_bwd(i, saved, dymem, copy_into=None):
        memn, kvm, km, kn, qn, q = saved
        (dq,), (dkm, dvm, g_qn), *copy = stage_bwd(f_memattn, [q], [km, Cols(kvm, MEM_WIDTH, 1), qn], [], mseg, [dymem], ROW_TILE,
                                                   f"mem{i}_attn_bwd", bf16_copies=(0,) if copy_into else (), copy_into=copy_into)
        dq = copy[0][0] if copy_into else dq
        (dkraw,), (g_kn,) = stage_bwd(f_headnorm, [Cols(kvm, MEM_WIDTH, 0)], [kn], [], mseg, [dkm], N_MEM, f"mem{i}_knorm_bwd")
        dkvm = jnp.concatenate([dkraw, dvm], axis=1).astype(BF16)
        g_w = matmul(memn, dkvm, "tn", f"mem{i}_kv_dw")
        dmemn = matmul(dkvm, P["mem_w_kv"][i], "nt", f"mem{i}_kv_dx")
        _, (g_mn,) = stage_bwd(f_rmsnorm, [memx], [P["mem_norm"][i:i + 1]], [], [], [dmemn], N_MEM, f"mem{i}_norm_bwd")
        return dq, g_mn, g_w, _sum_heads(g_qn), _sum_heads(g_kn)

    def ffn_fwd(i, xin):
        hn = stage_fwd(f_rmsnorm, [xin], [P["ffn_norm"][i:i + 1]], [], [], ROW_TILE, f"ffn{i}_norm", [BF16])[0]
        u = matmul(hn, P["ffn_w_up"][i], "nt", f"ffn{i}_up")
        z = convgate_fwd(u, P["ffn_conv_w"][i], P["ffn_conv_b"][i:i + 1], f"ffn{i}_conv")
        return matmul(z, P["ffn_w_down"][i], "nn", f"ffn{i}_down", residual=xin), (hn, u, z)

    def ffn_bwd(i, xin, saved, dxo, dxo_b):
        hn, u, z = saved
        dz = matmul(dxo_b, P["ffn_w_down"][i], "nt", f"ffn{i}_down_dx")
        g_down = matmul(z, dxo_b, "tn", f"ffn{i}_down_dw")
        du, g_cw, g_cb = convgate_bwd(u, P["ffn_conv_w"][i], P["ffn_conv_b"][i:i + 1], dz, f"ffn{i}_conv_bwd")
        dhn = matmul(du, P["ffn_w_up"][i], "nn", f"ffn{i}_up_dx")
        g_up = matmul(du, hn, "tn", f"ffn{i}_up_dw")
        (dxin,), (g_n,), (dxin_b,) = stage_bwd(f_rmsnorm_res, [xin], [P["ffn_norm"][i:i + 1]], [], [], [dhn, dxo], ROW_TILE,
                                               f"ffn{i}_norm_bwd", bf16_copies=(0,))
        return dxin, dxin_b, g_n, g_up, g_cw, g_cb, g_down

    h0 = stage_fwd(f_rmsnorm, [x0], [P["attn_norm"][0:1]], [], [], ROW_TILE, "l0_norm", [BF16])[0]
    p0 = matmul(h0, P["a_w_in"][0], "nn", "l0_in")
    lora0 = 3 * RWKV_WIDTH // LORA_COLS
    pre_xs = [Cols(p0, RWKV_WIDTH, 0), Cols(p0, RWKV_WIDTH, 1), Cols(p0, RWKV_WIDTH, 2), Cols(p0, LORA_COLS, lora0)]
    mu = [Cols(P["a_mu"], RWKV_WIDTH, 0), Cols(P["a_mu"], RWKV_WIDTH, 1), Cols(P["a_mu"], RWKV_WIDTH, 2),
          Cols(P["a_mu"], LORA_COLS, lora0)]
    lora_rows = lambda w, lo: jnp.pad(w, ((lo, LORA_COLS - lo - w.shape[0]), (0, 0)))
    pre_ps = mu + [P["a_w0"], lora_rows(P["a_w2"][0], 0), P["a_a0"], lora_rows(P["a_a2"][0], 64), lora_rows(P["a_g2"][0], 128),
                   P["a_k_k"], P["a_k_a"]]
    r, lw, k2, v, kk, b, g = stage_fwd(f_rwkv_pre, pre_xs, pre_ps, [], [seg, seg_t], RW_TILE, "l0_rwkv_pre", with_prev=True)
    scan_in = [r, lw, k2, v, kk, b]
    y_h, h_states, got = rwkv_scan_fwd(*scan_in, ex_weights or Exchange())
    if weights_done is not None:
        P.update(weights_done(got))
    y_s = y_h
    post_ps = [P["a_lnx_w"], P["a_lnx_b"], P["a_r_k"].reshape(1, RWKV_WIDTH)]
    ycat0, mem0_saved = mem_fwd(0, Cols(p0, MEM_WIDTH, SHIFT_WIDTH // MEM_WIDTH), Into(None, D_MODEL, RWKV_WIDTH // MEM_WIDTH))
    ycat0 = stage_fwd(f_rwkv_post, [y_s, r, k2, v, g], post_ps, [], [seg, seg_t], RW_TILE, "l0_rwkv_post", [BF16],
                      into=Into(ycat0, D_MODEL, 0))[0]
    x1 = matmul(ycat0, P["a_w_out"][0], "nn", "l0_out", residual=x0)
    x2, ffn0_saved = ffn_fwd(0, x1)

    hk, h1 = stage_fwd(f_rmsnorm2, [x2], [row(P["kv_norm"]), P["attn_norm"][1:2]], [], [], ROW_TILE, "l1_norm", [BF16, BF16])
    kvp = matmul(hk, P["kv_w"][0], "nn", "l1_kv")
    p1 = matmul(h1, P["b_w_in"][0], "nn", "l1_in")
    kraw, qraw = Cols(kvp, DIL_WIDTH, 0), Cols(p1, DIL_WIDTH, 0)
    kgain, qgain = _per_head(P["kv_k_norm"], DIL_WIDTH // HEAD_DIM), _per_head(P["b_q_norm"], DIL_WIDTH // HEAD_DIM)
    ksh = stage_fwd(f_qkprep, [kraw], [kgain], [cos, sin], [seg, seg_t], ROW_TILE, "l1_kprep")[0]
    q = stage_fwd(f_qkprep, [qraw], [qgain], [cos, sin], [seg, seg_t], ROW_TILE, "l1_qprep")[0]
    outs, lses = [], []
    for gi, (_, d) in enumerate(DIL_GROUPS):
        og, lg = dil_fwd(q, ksh, kvp, gi, d, f"l1_dil{gi}")
        outs.append(og)
        lses.append(lg)
    ycat1 = stage_fwd(f_mix, outs + lses, [], [], [], ROW_TILE, "l1_mix", [BF16], into=Into(None, 2 * MEM_WIDTH, 0))[0]
    ycat1, mem1_saved = mem_fwd(1, Cols(p1, MEM_WIDTH, DIL_WIDTH // MEM_WIDTH), Into(ycat1, 2 * MEM_WIDTH, 1))
    x3 = matmul(ycat1, P["b_w_out"][0], "nn", "l1_out", residual=x2)
    x4, ffn1_saved = ffn_fwd(1, x3)
    loss_part, dx4, dx4_b = loss_head(x4, tgt)

    dx3, dx3_b, gn1, gup1, gcw1, gcb1, gdown1 = ffn_bwd(1, x3, ffn1_saved, dx4, dx4_b)
    dycat1 = matmul(dx3_b, P["b_w_out"][0], "nt", "l1_out_dx")
    G["b_w_out"] = [matmul(ycat1, dx3_b, "tn", "l1_out_dw")]
    dp1, gmn1, gmw1, gmq1, gmk1 = mem_bwd(1, mem1_saved, Cols(dycat1, MEM_WIDTH, 1), Into(None, D_MODEL, DIL_WIDTH // MEM_WIDTH))
    dmix, _ = stage_bwd(f_mix, outs + lses, [], [], [], [Cols(dycat1, MEM_WIDTH, 0)], ROW_TILE, "l1_mix_bwd")
    dq, dk, dv = zip(*[dil_bwd(q, ksh, kvp, dmix[gi], dmix[3 + gi], gi, d, f"l1_dil{gi}_bwd")
                       for gi, (_, d) in enumerate(DIL_GROUPS)])
    dq, dk, dv = jnp.concatenate(dq, axis=1), jnp.concatenate(dk, axis=1), jnp.concatenate(dv, axis=1)
    _, (g_bq,), (dp1,) = stage_bwd(f_qkprep, [qraw], [qgain], [cos, sin], [seg, seg_t], [dq], ROW_TILE, "l1_qprep_bwd",
                                   bf16_copies=(0,), copy_into=Into(dp1, D_MODEL, 0))
    dkvp = jnp.pad(dv.astype(BF16), ((0, 0), (DIL_WIDTH, 0)))
    _, (g_kk,), (dkvp,) = stage_bwd(f_qkprep, [kraw], [kgain], [cos, sin], [seg, seg_t], [dk], ROW_TILE, "l1_kprep_bwd",
                                    bf16_copies=(0,), copy_into=Into(dkvp, 2 * DIL_WIDTH, 0))
    g_bq, g_kk = _sum_heads(g_bq), _sum_heads(g_kk)
    dh1 =matmul(dp1, P["b_w_in"][0], "nt", "l1_in_dx")
    G["b_w_in"] = [matmul(h1, dp1, "tn", "l1_in_dw")]
    dhk = matmul(dkvp, P["kv_w"][0], "nt", "l1_kv_dx")
    G["kv_w"] = [matmul(hk, dkvp, "tn", "l1_kv_dw")]
    (dx2,), (g_kvn, g_an1), (dx2_b,) = stage_bwd(f_rmsnorm2_res, [x2], [row(P["kv_norm"]), P["attn_norm"][1:2]], [], [],
                                                 [dhk, dh1, dx3], ROW_TILE, "l1_norm_bwd", bf16_copies=(0,))

    dx1, dx1_b, gn0, gup0, gcw0, gcb0, gdown0 = ffn_bwd(0, x1, ffn0_saved, dx2, dx2_b)
    dycat0 = matmul(dx1_b, P["a_w_out"][0], "nt", "l0_out_dx")
    G["a_w_out"] = [matmul(ycat0, dx1_b, "tn", "l0_out_dw")]
    dqmem0, gmn0, gmw0, gmq0, gmk0 = mem_bwd(0, mem0_saved, Cols(dycat0, MEM_WIDTH, RWKV_WIDTH // MEM_WIDTH))
    (dy_s, dr_a, dk_a, dv_a, dg), (g_lw, g_lb, g_rk) = stage_bwd(
        f_rwkv_post, [y_s, r, k2, v, g], post_ps, [], [seg, seg_t], [Cols(dycat0, RWKV_WIDTH, 0)], RW_TILE, "l0_rwkv_post_bwd")
    G["mem_w_kv"], G["ffn_w_up"], G["ffn_w_down"] = [gmw0, gmw1], [gup0, gup1], [gdown0, gdown1]
    (dr_b, dlw, dk_b, dv_b, dkk, db), G["_exchanged"] = rwkv_scan_bwd(*scan_in, h_states, dy_s,
                                                                      ex_grads(G) if ex_grads else Exchange())
    dpre, gpre = stage_bwd(f_rwkv_pre, pre_xs, pre_ps, [], [seg, seg_t],
                           [[dr_a, dr_b], dlw, [dk_a, dk_b], [dv_a, dv_b], dkk, db, dg], RW_TILE, "l0_rwkv_pre_bwd", with_prev=True)
    dp_rw = jnp.concatenate(dpre[:4], axis=1) + _shift_up(jnp.concatenate(dpre[4:], axis=1))
    dp0 = jnp.concatenate([dp_rw, dqmem0], axis=1).astype(BF16)
    dh0 = matmul(dp0, P["a_w_in"][0], "nt", "l0_in_dx")
    G["a_w_in"] = [matmul(h0, dp0, "tn", "l0_in_dw")]
    (dx0,), (g_an0,) = stage_bwd(f_rmsnorm_res, [x0], [P["attn_norm"][0:1]], [], [], [dh0, dx1], ROW_TILE, "l0_norm_bwd")

    G["attn_norm"] = jnp.concatenate([g_an0, g_an1], axis=0)
    G["a_mu"] = jnp.concatenate(gpre[:4], axis=1)
    G["a_w0"], G["a_w2"], G["a_a0"], G["a_a2"], G["a_g2"] = gpre[4], gpre[5][None, :64], gpre[6], gpre[7][None, 64:128], gpre[8][None, 128:]
    G["a_k_k"], G["a_k_a"] = gpre[9], gpre[10]
    G["a_r_k"] = g_rk.reshape(1, RWKV_HEADS, HEAD_DIM)
    G["a_lnx_w"], G["a_lnx_b"] = g_lw, g_lb
    G["kv_norm"], G["kv_k_norm"], G["b_q_norm"] = g_kvn.reshape(-1), g_kk.reshape(-1), g_bq
    G["mem_norm"] = jnp.concatenate([gmn0, gmn1], axis=0)
    G["mem_w_kv"] = [gmw0, gmw1]
    G["mem_q_norm"] = jnp.concatenate([gmq0, gmq1], axis=0)
    G["mem_k_norm"] = jnp.concatenate([gmk0, gmk1], axis=0)
    G["ffn_norm"] = jnp.concatenate([gn0, gn1], axis=0)
    G["ffn_w_up"] = [gup0, gup1]
    G["ffn_conv_w"] = jnp.stack([gcw0, gcw1])
    G["ffn_conv_b"] = jnp.concatenate([gcb0, gcb1], axis=0)
    G["ffn_w_down"] = [gdown0, gdown1]
    return loss_part, dx0, G


PARAMS = (("attn_norm", None), ("a_w_in", 2), ("a_mu", 1), ("a_w0", 1), ("a_w2", 2), ("a_a0", 1), ("a_a2", 2), ("a_g2", 2),
          ("a_k_k", 1), ("a_k_a", 1), ("a_r_k", None), ("a_lnx_w", 1), ("a_lnx_b", 1), ("a_w_out", 1), ("kv_norm", None),
          ("kv_w", 1), ("kv_k_norm", None), ("b_w_in", 1), ("b_q_norm", None), ("b_w_out", 2), ("mem_norm", None),
          ("mem_w_kv", 1), ("mem_q_norm", None), ("mem_k_norm", None), ("ffn_norm", None), ("ffn_w_up", 2),
          ("ffn_conv_w", 2), ("ffn_conv_b", None), ("ffn_w_down", 1))
BIG = ("a_w_in", "a_w_out", "kv_w", "b_w_in", "b_w_out", "mem_w_kv", "ffn_w_up", "ffn_w_down")
TRANSPOSED = ("ffn_w_up",)
AXIS = dict(PARAMS)
SMALL = tuple(n for n, _ in PARAMS if n not in BIG)
SMALL_SHARDED = tuple(n for n in SMALL if AXIS[n] is not None)
PACK_QUANTUM = 256 * 128


def _from_shards(xs, axis):
    full = jnp.moveaxis(xs, 0, axis)
    sh = full.shape
    return full.reshape(sh[:axis] + (sh[axis] * sh[axis + 1],) + sh[axis + 2:])


def _to_shards(g, axis):
    sh = g.shape
    return jnp.moveaxis(g.reshape(sh[:axis] + (N_DEV, sh[axis] // N_DEV) + sh[axis + 1:]), axis, 0)


def _pack(parts, lead=0):
    ld = parts[0].shape[:lead]
    flat = jnp.concatenate([p.reshape(ld + (-1,)) for p in parts], axis=-1)
    pad = (-flat.shape[-1]) % PACK_QUANTUM
    flat = jnp.pad(flat, [(0, 0)] * lead + [(0, pad)])
    return flat.reshape(ld + (-1, 128))


def _unpack(packed, shapes, lead=0):
    ld = packed.shape[:lead]
    flat = packed.reshape(ld + (-1,))
    out, off = [], 0
    for s in shapes:
        n = math.prod(s)
        out.append(flat[..., off:off + n].reshape(ld + tuple(s)))
        off += n
    return out


def kernel(x, mem, attn_norm, a_w_in, a_mu, a_w0, a_w2, a_a0, a_a2, a_g2, a_k_k, a_k_a, a_r_k, a_lnx_w, a_lnx_b, a_w_out, kv_norm, kv_w, kv_k_norm, b_w_in, b_q_norm, b_w_out, mem_norm, mem_w_kv, mem_q_norm, mem_k_norm, ffn_norm, ffn_w_up, ffn_conv_w, ffn_conv_b, ffn_w_down, loss_target, m_attn_norm, m_a_w_in, m_a_mu, m_a_w0, m_a_w2, m_a_a0, m_a_a2, m_a_g2, m_a_k_k, m_a_k_a, m_a_r_k, m_a_lnx_w, m_a_lnx_b, m_a_w_out, m_kv_norm, m_kv_w, m_kv_k_norm, m_b_w_in, m_b_q_norm, m_b_w_out, m_mem_norm, m_mem_w_kv, m_mem_q_norm, m_mem_k_norm, m_ffn_norm, m_ffn_w_up, m_ffn_conv_w, m_ffn_conv_b, m_ffn_w_down, v_attn_norm, v_a_w_in, v_a_mu, v_a_w0, v_a_w2, v_a_a0, v_a_a2, v_a_g2, v_a_k_k, v_a_k_a, v_a_r_k, v_a_lnx_w, v_a_lnx_b, v_a_w_out, v_kv_norm, v_kv_w, v_kv_k_norm, v_b_w_in, v_b_q_norm, v_b_w_out, v_mem_norm, v_mem_w_kv, v_mem_q_norm, v_mem_k_norm, v_ffn_norm, v_ffn_w_up, v_ffn_conv_w, v_ffn_conv_b, v_ffn_w_down):
    names = [n for n, _ in PARAMS]
    vals = (attn_norm, a_w_in, a_mu, a_w0, a_w2, a_a0, a_a2, a_g2, a_k_k, a_k_a, a_r_k, a_lnx_w, a_lnx_b, a_w_out, kv_norm, kv_w, kv_k_norm, b_w_in, b_q_norm, b_w_out, mem_norm, mem_w_kv, mem_q_norm, mem_k_norm, ffn_norm, ffn_w_up, ffn_conv_w, ffn_conv_b, ffn_w_down)
    m_vals = (m_attn_norm, m_a_w_in, m_a_mu, m_a_w0, m_a_w2, m_a_a0, m_a_a2, m_a_g2, m_a_k_k, m_a_k_a, m_a_r_k, m_a_lnx_w, m_a_lnx_b, m_a_w_out, m_kv_norm, m_kv_w, m_kv_k_norm, m_b_w_in, m_b_q_norm, m_b_w_out, m_mem_norm, m_mem_w_kv, m_mem_q_norm, m_mem_k_norm, m_ffn_norm, m_ffn_w_up, m_ffn_conv_w, m_ffn_conv_b, m_ffn_w_down)
    v_vals = (v_attn_norm, v_a_w_in, v_a_mu, v_a_w0, v_a_w2, v_a_a0, v_a_a2, v_a_g2, v_a_k_k, v_a_k_a, v_a_r_k, v_a_lnx_w, v_a_lnx_b, v_a_w_out, v_kv_norm, v_kv_w, v_kv_k_norm, v_b_w_in, v_b_q_norm, v_b_w_out, v_mem_norm, v_mem_w_kv, v_mem_q_norm, v_mem_k_norm, v_ffn_norm, v_ffn_w_up, v_ffn_conv_w, v_ffn_conv_b, v_ffn_w_down)
    W, M, V = dict(zip(names, vals)), dict(zip(names, m_vals)), dict(zip(names, v_vals))
    layers = lambda D, n: [D[n]] if D[n].ndim == 2 else [D[n][i] for i in range(D[n].shape[0])]
    ax2 = lambda n: AXIS[n] - (W[n].ndim - 2)
    later =[(n, i) for n in BIG if n != "a_w_in" for i in range(len(layers(W, n)))]

    small_shapes = [W[n].shape for n in SMALL_SHARDED]
    got_w, got_small = exchange(Exchange(gathers=[W["a_w_in"][0].astype(BF16), _pack([W[n] for n in SMALL_SHARDED])]),
                                "gather_first")
    P = {n: W[n] for n in SMALL}
    P["a_w_in"] = [_from_shards(got_w, ax2("a_w_in"))]
    for n, s in zip(SMALL_SHARDED, _unpack(got_small, small_shapes, lead=1)):
        P[n] = _from_shards(s, AXIS[n])
    sent = lambda n, w: w.T if n in TRANSPOSED else w
    whole = lambda n, g: g.reshape(-1, g.shape[-1]) if n in TRANSPOSED else _from_shards(g, ax2(n))
    ex_weights = Exchange(gathers=[sent(n, layers(W, n)[i]).astype(BF16) for n, i in later])

    def weights_done(got):
        out = {}
        for (n, _), g in zip(later, got):
            out.setdefault(n, []).append(whole(n, g))
        return out

    slots = lambda G, n: jnp.stack([_to_shards(g, 0 if n in TRANSPOSED else ax2(n)) for g in G[n]], axis=1)
    later_names = [n for n in BIG if n != "a_w_in"]
    ex_grads = lambda G: Exchange(scatters=[slots(G, n) for n in later_names])
    loss_part, dx0, G = _local_step(x[0], mem[0], loss_target[0], P, ex_weights, weights_done, ex_grads)
    gparts = dict(zip(later_names, G.pop("_exchanged")))
    replicated = [n for n in SMALL if AXIS[n] is None]
    small_slots = _pack([_to_shards(G[n], AXIS[n]) for n in SMALL_SHARDED], lead=1)
    got_rep, gparts["a_w_in"], got_sharded = exchange(
        Exchange(gathers=[_pack([G[n] for n in replicated] + [loss_part[0:1, 0:1]])], scatters=[slots(G, "a_w_in"), small_slots]),
        "exchange_last")

    results = {}
    for n in BIG:
        rows = lambda z: z.reshape((-1,) + z.shape[-1:])
        gp = gparts[n].reshape((N_DEV, -1) + gparts[n].shape[-1:])
        if n in TRANSPOSED:
            gp = jnp.swapaxes(sum_parts(gp, f"sum_{n}").reshape(gparts[n].shape[1:]), -1, -2).reshape((1,) + rows(W[n]).shape)
        res = adamw(gp, rows(W[n]), rows(M[n]), rows(V[n]), f"adamw_{n}")
        results[n] = [r.reshape(W[n].shape) for r in res]
    *rep_sums, loss = _unpack(sum_parts(got_rep, "sum_replicated_grads"), [W[n].shape for n in replicated] + [()])
    g_mine = dict(zip(replicated, rep_sums))
    g_mine.update(zip(SMALL_SHARDED, _unpack(sum_parts(got_sharded, "sum_small_sharded_grads"), [W[n].shape for n in SMALL_SHARDED])))
    res = adamw(_pack([g_mine[n] for n in SMALL])[None], _pack([W[n] for n in SMALL]), _pack([M[n] for n in SMALL]),
                _pack([V[n] for n in SMALL]), "adamw_small")
    for n, parts in zip(SMALL, zip(*[_unpack(r, [W[n].shape for n in SMALL]) for r in res])):
        results[n] = list(parts)
    outs = [[results[n][j] for n in names] for j in range(4)]
    return (loss, dx0[None], *outs[0], *outs[1], *outs[2], *outs[3])
```

```python
import functools
import math

import jax
import jax.numpy as jnp
import numpy as np
from jax import lax
from jax.experimental import pallas as pl
from jax.experimental.pallas import tpu as pltpu

F32 = jnp.float32
BF16 = jnp.bfloat16
H3 =lax.Precision.HIGH

N_DEV = 8
D_MODEL = 1024
HEAD_DIM = 64
N_MEM = 256
MEM_HEADS = 4
MEM_WIDTH = 256
RWKV_HEADS = 12
RWKV_WIDTH = 768
SHIFT_WIDTH = 2560
DIL_GROUPS = ((128, 1), (512, 4), (2048, 16))
DIL_BLOCK = 128
DIL_WIDTH = 768
D_FF = 2816
RMS_EPS = 1e-6
LNX_EPS = 64e-5
NEG_INF = -1e30
ROPE_THETA = 10000.0
ADAM_LR, ADAM_B1, ADAM_B2, ADAM_EPS, ADAM_WD, ADAM_STEP = 0.001, 0.9, 0.999, 1e-08, 0.01, 10

CHUNK = 64
SCAN_GROUPS_FWD, SCAN_GROUPS_BWD = 1, 1
MM_TILE_CAP = 1408
VMEM_LIMIT_V7X = 48 * 1024 * 1024


def _cparams(sem):
    return pltpu.CompilerParams(dimension_semantics=sem, vmem_limit_bytes=VMEM_LIMIT_V7X)


def _pick(n, cands):
    for c in cands:
        if n % c == 0:
            return c
    return n


def _tile(n, cap):
    if n <= cap:
        return n
    for d in range(cap - cap % 128, 0, -128):
        if n % d == 0:
            return d
    return n


def _dg(a, b, ca, cb, batch):
    dims = (((ca,), (cb,)), ((0,), (0,))) if batch else (((ca,), (cb,)), ((), ()))
    return lax.dot_general(a.astype(BF16), b.astype(BF16), dims, preferred_element_type=F32)


@jax.custom_vjp
def mm_nn(a, b):
    n = a.ndim
    return _dg(a, b, n - 1, n - 2, n == 3)


def _mm_nn_fwd(a, b):
    return mm_nn(a, b), (a, b)


def _mm_nn_bwd(res, g):
    a, b = res
    n = a.ndim
    return _dg(g, b, n - 1, n - 1, n == 3), _dg(a, g, n - 2, n - 2, n == 3)


mm_nn.defvjp(_mm_nn_fwd, _mm_nn_bwd)


@jax.custom_vjp
def mm_nt(a, b):
    n = a.ndim
    return _dg(a, b, n - 1, n - 1, n == 3)


def _mm_nt_fwd(a, b):
    return mm_nt(a, b), (a, b)


def _mm_nt_bwd(res, g):
    a, b = res
    n = a.ndim
    return _dg(g, b, n - 1, n - 2, n == 3), _dg(g, a, n - 2, n - 2, n == 3)


mm_nt.defvjp(_mm_nt_fwd, _mm_nt_bwd)


def mmh(a, b):
    n = a.ndim
    dims = (((n - 1,), (n - 2,)), ((0,), (0,))) if n == 3 else (((1,), (0,)), ((), ()))
    return lax.dot_general(a, b, dims, precision=H3, preferred_element_type=F32)


def mmh_nt(a, b):
    n = a.ndim
    dims = (((n - 1,), (n - 1,)), ((0,), (0,))) if n == 3 else (((1,), (1,)), ((), ()))
    return lax.dot_general(a, b, dims, precision=H3, preferred_element_type=F32)


def mmh_tn(a, b):
    n = a.ndim
    dims = (((n - 2,), (n - 2,)), ((0,), (0,))) if n == 3 else (((0,), (0,)), ((), ()))
    return lax.dot_general(a, b, dims, precision=H3, preferred_element_type=F32)


def matmul(a, b, mode, name, residual=None):
    out_dtype = BF16 if mode == "tn" else F32
    if mode == "nn":
        (M, K), (_, N) = a.shape, b.shape
    elif mode == "nt":
        (M, K), (N, _) = a.shape, b.shape
    else:
        (K, M), (_, N) = a.shape, b.shape
    tm = _tile(M, 2048 if mode == "nn" else MM_TILE_CAP)
    tn = _tile(N, 512 if mode == "nn" else MM_TILE_CAP)
    tk = _tile(K, MM_TILE_CAP)
    nk = K // tk
    if mode == "nn":
        a_spec = pl.BlockSpec((tm, tk), lambda i, j, k: (i, k))
        b_spec = pl.BlockSpec((tk, tn), lambda i, j, k: (k, j))
        dims = (((1,), (0,)), ((), ()))
    elif mode == "nt":
        a_spec = pl.BlockSpec((tm, tk), lambda i, j, k: (i, k))
        b_spec = pl.BlockSpec((tn, tk), lambda i, j, k: (j, k))
        dims = (((1,), (1,)), ((), ()))
    else:
        a_spec = pl.BlockSpec((tk, tm), lambda i, j, k: (k, i))
        b_spec = pl.BlockSpec((tk, tn), lambda i, j, k: (k, j))
        dims = (((0,), (0,)), ((), ()))
    o_spec = pl.BlockSpec((tm, tn), lambda i, j, k: (i, j))
    has_res = residual is not None

    def body(*refs):
        if has_res:
            a_ref, b_ref, r_ref, o_ref, acc_ref = refs
        else:
            a_ref, b_ref, o_ref, acc_ref = refs
        k = pl.program_id(2)

        @pl.when(k == 0)
        def _():
            acc_ref[...] = jnp.zeros_like(acc_ref)

        acc_ref[...] += lax.dot_general(a_ref[...].astype(BF16), b_ref[...].astype(BF16), dims,
                                        preferred_element_type=F32)

        @pl.when(k == nk - 1)
        def _():
            if has_res:
                o_ref[...] = (acc_ref[...] + r_ref[...]).astype(out_dtype)
            else:
                o_ref[...] = acc_ref[...].astype(out_dtype)

    ins = [a, b] + ([residual] if has_res else [])
    in_specs = [a_spec, b_spec] + ([o_spec] if has_res else [])
    return pl.pallas_call(
        body, grid=(M // tm, N // tn, nk), in_specs=in_specs, out_specs=o_spec,
        out_shape=jax.ShapeDtypeStruct((M, N), out_dtype), scratch_shapes=[pltpu.VMEM((tm, tn), F32)],
        compiler_params=_cparams(("parallel", "parallel", "arbitrary")), name=name)(*ins)


class Cols:
    def __init__(self, arr, width, idx):
        self.arr, self.width, self.idx = arr, width, idx


def _arr(x):
    return x.arr if isinstance(x, Cols) else x


def _shape(x):
    return x.arr.shape[:-1] + (x.width,) if isinstance(x, Cols) else x.shape


def _col(x):
    return x.idx if isinstance(x, Cols) else 0


def _tok_spec(x, tile):
    shape, col = _shape(x), _col(x)
    return pl.BlockSpec(shape[:-2] + (tile, shape[-1]), lambda i: (0,) * (len(shape) - 2) + (i, col))


def _full_spec(x):
    shape, col = _shape(x), _col(x)
    return pl.BlockSpec(shape, lambda i: (0,) * (len(shape) - 1) + (col,))


def _halo_spec(x, tile):
    shape, col = _shape(x), _col(x)
    return pl.BlockSpec((8, shape[-1]), lambda i: (jnp.maximum(i * (tile // 8) - 1, 0), col))


def _blk(x, tile):
    shape = _shape(x)
    return jax.ShapeDtypeStruct(shape[:-2] + (tile, shape[-1]), _arr(x).dtype)


def _prev_rows(x, halo):
    rows = lax.broadcasted_iota(jnp.int32, (x.shape[0], 1), 0)
    before = jnp.where(pl.program_id(0) > 0, halo[7:8], 0.0)
    return jnp.where(rows == 0, before, pltpu.roll(x, 1, 0))


class Into:
    def __init__(self, buf, total, idx):
        self.buf, self.total, self.idx = buf, total, idx

    def place(self, shape, tile):
        idx = self.idx
        return shape.update(shape=(shape.shape[0], self.total)), pl.BlockSpec((tile, shape.shape[1]), lambda i: (i, idx))

    def operand(self, n_in, n_out_index):
        if self.buf is None:
            return [], [], {}
        return [self.buf], [_ANY], {n_in: n_out_index}


def stage_fwd(f, xs, ps, cts, cfs, tile, name, out_dtypes=None, with_prev=False, into=None):
    xs, ps, cts, cfs = list(xs), list(ps), list(cts), list(cfs)
    halos = xs if with_prev else []
    nx, nh, nct, np_ = len(xs), len(halos), len(cts), len(ps)
    T = _shape(xs[0])[-2]
    blk = [_blk(x, tile) for x in xs]
    out_avals = jax.eval_shape(f, *blk, *(blk if with_prev else []), *[_blk(p, _shape(p)[-2]) for p in ps],
                               *[_blk(c, tile) for c in cts], *[_blk(c, _shape(c)[-2]) for c in cfs])
    if out_dtypes is None:
        out_dtypes = [o.dtype for o in out_avals]
    out_shape = [jax.ShapeDtypeStruct(o.shape[:-2] + (T, o.shape[-1]), dt) for o, dt in zip(out_avals, out_dtypes)]
    out_specs = [_tok_spec(o, tile) for o in out_shape]
    n_in = nx + nh + nct + np_ + len(cfs)
    extra, extra_specs, alias = [], [], {}
    if into is not None:
        out_shape[0], out_specs[0] = into.place(out_shape[0], tile)
        extra, extra_specs, alias = into.operand(n_in, 0)

    def body(*refs):
        vals = [r[...] for r in refs[:n_in]]
        xv, hv, rest = vals[:nx], vals[nx:nx + nh], vals[nx + nh:]
        ctv, pv, cfv = rest[:nct], rest[nct:nct + np_], rest[nct + np_:]
        prev = [_prev_rows(x, h) for x, h in zip(xv, hv)]
        res = f(*xv, *prev, *pv, *ctv, *cfv)
        for o_ref, r in zip(refs[n_in + len(extra):], res):
            o_ref[...] = r.astype(o_ref.dtype)

    return pl.pallas_call(
        body, grid=(T // tile,),
        in_specs=([_tok_spec(x, tile) for x in xs] + [_halo_spec(x, tile) for x in halos] + [_tok_spec(c, tile) for c in cts]
                  + [_full_spec(p) for p in ps + cfs] + extra_specs),
        out_specs=out_specs, out_shape=out_shape, input_output_aliases=alias,
        compiler_params=_cparams(("parallel",)), name=name)(*[_arr(a) for a in xs + halos + cts + ps + cfs], *extra)


def stage_bwd(f, xs, ps, cts, cfs, gs, tile, name, bf16_copies=(), with_prev=False, copy_into=None):
    xs, ps, cts, cfs = list(xs), list(ps), list(cts), list(cfs)
    gs = [list(g) if isinstance(g, (list, tuple)) else [g] for g in gs]
    g_flat = [a for g in gs for a in g]
    halos = xs if with_prev else []
    nx, nh, nct, ng, np_ = len(xs), len(halos), len(cts), len(g_flat), len(ps)
    T = _shape(xs[0])[-2]
    dx_like = xs + halos
    out_shape = ([jax.ShapeDtypeStruct(_shape(x), F32) for x in dx_like] + [jax.ShapeDtypeStruct(_shape(p), F32) for p in ps]
                 + [jax.ShapeDtypeStruct(_shape(xs[i]), BF16) for i in bf16_copies])
    n_in = nx + nh + nct + ng + np_ + len(cfs)
    ndx = nx + nh
    plain = lambda x: jax.ShapeDtypeStruct(_shape(x), F32)
    out_specs = ([_tok_spec(plain(x), tile) for x in dx_like] + [_full_spec(plain(p)) for p in ps]
                 + [_tok_spec(plain(xs[i]), tile) for i in bf16_copies])
    extra, extra_specs, alias = [], [], {}
    if copy_into is not None:
        out_shape[ndx + np_], out_specs[ndx + np_] = copy_into.place(out_shape[ndx + np_], tile)
        extra, extra_specs, alias = copy_into.operand(n_in, ndx + np_)

    def body(*refs):
        vals = [r[...] for r in refs[:n_in]]
        outs = refs[n_in + len(extra):]
        xv, hv, rest = vals[:nx], vals[nx:nx + nh], vals[nx + nh:]
        ctv, gparts, pv, cfv = rest[:nct], rest[nct:nct + ng], rest[nct + ng:nct + ng + np_], rest[nct + ng + np_:]
        gv = []
        for g in gs:
            gv.append(functools.reduce(lambda a, b: a + b, gparts[:len(g)]))
            gparts = gparts[len(g):]
        prev = [_prev_rows(x, h) for x, h in zip(xv, hv)]
        _, vjp = jax.vjp(lambda *xp: f(*xp, *ctv, *cfv), *xv, *prev, *pv)
        d = vjp(tuple(gv))
        for o_ref, r in zip(outs[:ndx], d[:ndx]):
            o_ref[...] = r
        for o_ref, i in zip(outs[ndx + np_:], bf16_copies):
            o_ref[...] = d[i].astype(BF16)

        @pl.when(pl.program_id(0) == 0)
        def _():
            for o_ref in outs[ndx:ndx + np_]:
                o_ref[...] = jnp.zeros_like(o_ref)

        for o_ref, r in zip(outs[ndx:ndx + np_], d[ndx:]):
            o_ref[...] += r

    res = pl.pallas_call(
        body, grid=(T // tile,),
        in_specs=([_tok_spec(x, tile) for x in xs] + [_halo_spec(x, tile) for x in halos]
                  + [_tok_spec(c, tile) for c in cts + g_flat] + [_full_spec(p) for p in ps + cfs] + extra_specs),
        out_specs=out_specs, out_shape=out_shape, input_output_aliases=alias,
        compiler_params=_cparams(("arbitrary",)), name=name)(*[_arr(a) for a in xs + halos + cts + g_flat + ps + cfs], *extra)
    if bf16_copies:
        return list(res[:ndx]), list(res[ndx:ndx + np_]), list(res[ndx + np_:])
    return list(res[:ndx]), list(res[ndx:])


def _rms(x, g, eps=RMS_EPS):
    return x * lax.rsqrt(jnp.mean(x * x, axis=-1, keepdims=True) + eps) * g


def f_rmsnorm(x, g):
    return (_rms(x, g),)


def f_rmsnorm_res(x, g):
    return _rms(x, g), x


def f_rmsnorm2(x, g1, g2):
    n = x * lax.rsqrt(jnp.mean(x * x, axis=-1, keepdims=True) + RMS_EPS)
    return n * g1, n * g2


def f_rmsnorm2_res(x, g1, g2):
    return f_rmsnorm2(x, g1, g2) + (x,)


def _sigmoid(x):
    return 1.0 / (1.0 + jnp.exp(-x))


def _softplus(x):
    return jnp.maximum(x, 0.0) + jnp.log(1.0 + jnp.exp(-jnp.abs(x)))


def f_rwkv_pre(pr, pk, pv, pl_, qr, qk, qv, ql, mu_r, mu_k, mu_v, mu_l, w0, w2, a0, a2, g2, k_k, k_a, seg, seg_t):
    xr = pr + (qr - pr) * mu_r
    xk = pk + (qk - pk) * mu_k
    xv = pv + (qv - pv) * mu_v
    xl = pl_ + (ql - pl_) * mu_l
    w_log = -_softplus(-(w0 + mm_nn(jnp.tanh(xl), w2))) - 0.5
    lw = -jnp.exp(w_log)
    a = _sigmoid(a0 + mm_nn(xl, a2))
    g = mm_nn(_sigmoid(xl), g2)
    kkr = xk * k_k
    inv = lax.rsqrt(jnp.maximum(mmh(kkr * kkr, seg), 1e-24))
    kk = kkr * mmh(inv, seg_t)
    k2 = xk * (1.0 + (a - 1.0) * k_a)
    return xr, lw, k2, xv, kk, kk * a, g


def f_rwkv_post(y, r, k2, v, g, lnx_w, lnx_b, r_k, seg, seg_t):
    inv_n = 1.0 / HEAD_DIM
    m = mmh(mmh(y, seg) * inv_n, seg_t)
    yc = y - m
    rstd = lax.rsqrt(mmh(yc * yc, seg) * inv_n + LNX_EPS)
    yn = yc * mmh(rstd, seg_t) * lnx_w + lnx_b
    bonus = mmh(mmh(r * k2 * r_k, seg), seg_t) * v
    return ((yn + bonus) * g,)


def _headnorm(z, g, seg, seg_t):
    ms = mmh(z * z, seg) * (1.0 / HEAD_DIM)
    return z * mmh(lax.rsqrt(ms + RMS_EPS), seg_t) * g


def f_headnorm(z, g, seg, seg_t):
    return (_headnorm(z, g, seg, seg_t),)


def _rot_half(z):
    w = z.shape[1]
    half = HEAD_DIM // 2
    lane = lax.broadcasted_iota(jnp.int32, (1, w), 1)
    return jnp.where((lane & (HEAD_DIM - 1)) < half, -pltpu.roll(z, w - half, 1), pltpu.roll(z, half, 1))


@jax.custom_vjp
def _rotate_half(z):
    return _rot_half(z)


_rotate_half.defvjp(lambda z: (_rot_half(z), None), lambda _, g: (-_rot_half(g),))


def f_qkprep(z, g, cos, sin, seg, seg_t):
    zn = _headnorm(z, g, seg, seg_t)
    pairs = z.shape[1] // cos.shape[1]
    return (zn * jnp.tile(cos, (1, pairs)) + _rotate_half(zn) * jnp.tile(sin, (1, pairs)),)


def _head_mask(width, h):
    lane = lax.broadcasted_iota(jnp.int32, (1, width), 1)
    return jnp.where((lane >> 6) == h, jnp.ones((), F32), 0.0)


def f_memattn(q, k, v, q_norm, seg, seg_t):
    qn = _headnorm(q, q_norm, seg, seg_t)
    out = jnp.zeros_like(q)
    for h in range(MEM_HEADS):
        m = _head_mask(MEM_WIDTH, h)
        s = mm_nt(qn * m, k) * (1.0 / math.sqrt(HEAD_DIM))
        s = s - jnp.max(s, axis=-1, keepdims=True)
        p = jnp.exp(s)
        p = p / jnp.sum(p, axis=-1, keepdims=True)
        out = out + mm_nn(p, v) * m
    return (out,)


def f_mix(o1, o2, o3, l1, l2, l3):
    mx = jnp.maximum(jnp.maximum(l1, l2), l3)
    e1, e2, e3 = jnp.exp(l1 - mx), jnp.exp(l2 - mx), jnp.exp(l3 - mx)
    return ((e1 * o1 + e2 * o2 + e3 * o3) / (e1 + e2 + e3),)


def _chunk_masks(L):
    t = lax.broadcasted_iota(jnp.int32, (L, L), 0)
    s = lax.broadcasted_iota(jnp.int32, (L, L), 1)
    return t, s


def _unit_lower_inverse(a):
    L = a.shape[-1]
    t, s = _chunk_masks(L)
    one = jnp.ones((), F32)
    blk = lambda sh: jnp.where((t >> sh) == (s >> sh), one, 0.0)
    n0 = a * blk(3)
    x = jnp.where(t == s, one, 0.0) - n0
    n2 = mmh(n0, n0)
    x = x + mmh(x, n2)
    x = x + mmh(x, mmh(n2, n2))
    for sh in (3, 4, 5):
        if (1 << sh) >= L:
            break
        off = a * (blk(sh + 1) - blk(sh))
        x = x - mmh(x, mmh(off, x))
    return x


@jax.custom_vjp
def _inverse_known(a, x):
    return x


def _inverse_known_fwd(a, x):
    return x, x


def _inverse_known_bwd(x, dx):
    return -mmh_nt(mmh_tn(x, dx), x), jnp.zeros_like(x)


_inverse_known.defvjp(_inverse_known_fwd, _inverse_known_bwd)


def _running_sum(x, reverse):
    L = x.shape[1]
    pos = lax.broadcasted_iota(jnp.int32, (1, L, 1), 1)
    step = 1
    while step < L:
        if reverse:
            x = x + jnp.where(pos < L - step, pltpu.roll(x, L - step, 1), 0.0)
        else:
            x = x + jnp.where(pos >= step, pltpu.roll(x, step, 1), 0.0)
        step *= 2
    return x


@jax.custom_vjp
def _cumsum_tokens(x):
    return _running_sum(x, False)


_cumsum_tokens.defvjp(lambda x: (_running_sum(x, False), None), lambda _, g: (_running_sum(g, True),))


def f_rwkv_chunk(s0, r, lw, k, v, kk, b, x_known=None):
    H, L, _ = r.shape
    t, s = _chunk_masks(L)
    one = jnp.ones((), F32)
    incl = jnp.where(t >= s, one, 0.0)
    strict = jnp.where(t > s, one, 0.0)
    cum = _cumsum_tokens(lw)
    w_in = jnp.exp(cum)
    w_ex = jnp.exp(cum - lw)
    w_inv = jnp.exp(-cum)
    rt, kkt, kt, bt = r * w_in, kk * w_ex, k * w_inv, b * w_inv
    a_b = mmh_nt(kkt, bt) * strict
    a_k = mmh_nt(kkt, kt) * strict
    m_k = mmh_nt(rt, kt) * incl
    m_b = mmh_nt(rt, bt) * incl
    x = _unit_lower_inverse(a_b) if x_known is None else _inverse_known(a_b, x_known)
    u = mmh(x, mmh_nt(kkt, s0) + mmh(a_k, v))
    y = mmh_nt(rt, s0) + mmh(m_k, v) - mmh(m_b, u)
    w_last = jnp.exp(jnp.sum(lw, axis=1, keepdims=True))
    s1 = (s0 + mmh_tn(v, kt) - mmh_tn(u, bt)) * w_last
    return y, s1, x


def _ex_split(ex, refs, n_in, n_out):
    n = ex.n
    ins, ex_in = refs[:n_in], refs[n_in:n_in + n]
    outs, ex_out = refs[n_in + n:n_in + n + n_out], refs[n_in + n + n_out:n_in + 2 * n + n_out]
    rest = refs[n_in + 2 * n + n_out:]
    return ins, outs, rest[:len(rest) - 3], (ex_in, ex_out) + tuple(rest[len(rest) - 3:])


def _split_heads(x):
    return jnp.stack([x[:, h * HEAD_DIM:(h + 1) * HEAD_DIM] for h in range(x.shape[1] // HEAD_DIM)], axis=0)


def _merge_heads(x):
    return jnp.concatenate([x[h] for h in range(x.shape[0])], axis=1)


def rwkv_scan_fwd(r, lw, k, v, kk, b, ex):
    T, N = r.shape[0], HEAD_DIM
    H = r.shape[1] // N
    groups = SCAN_GROUPS_FWD
    nc, hg = T // CHUNK, H // groups
    seq = pl.BlockSpec((CHUNK, hg * N), lambda g, c: (c, g))

    def body(*refs):
        (r_ref, lw_ref, k_ref, v_ref, kk_ref, b_ref), (y_ref, hs_ref, xs_ref), (h_scr,), ex_refs = _ex_split(ex, refs, 6, 3)
        g, c = pl.program_id(0), pl.program_id(1)

        @pl.when(jnp.logical_and(g == 0, c == 0))
        def _():
            ex.start(*ex_refs)

        @pl.when(c == 0)
        def _():
            h_scr[...] = jnp.zeros_like(h_scr)

        h0 = h_scr[...]
        hs_ref[0] = h0
        y, h1, x = f_rwkv_chunk(h0, *[_split_heads(z[...]) for z in (r_ref, lw_ref, k_ref, v_ref, kk_ref, b_ref)])
        y_ref[...] = _merge_heads(y)
        xs_ref[0] = x
        h_scr[...] = h1

        @pl.when(jnp.logical_and(g == groups - 1, c == (3 * nc) // 4))
        def _():
            ex.forward(*ex_refs)

        @pl.when(jnp.logical_and(g == groups - 1, c == nc - 1))
        def _():
            ex.wait(*ex_refs)

    res = pl.pallas_call(
        body, grid=(groups, nc), in_specs=[seq] * 6 + [_ANY] * ex.n,
        out_specs=[seq, pl.BlockSpec((1, hg, N, N), lambda g, c: (c, g, 0, 0)),
                   pl.BlockSpec((1, hg, CHUNK, CHUNK), lambda g, c: (c, g, 0, 0))] + [_ANY] * ex.n,
        out_shape=[jax.ShapeDtypeStruct((T, H * N), F32), jax.ShapeDtypeStruct((nc, H, N, N), F32),
                   jax.ShapeDtypeStruct((nc, H, CHUNK, CHUNK), F32)] + ex.out_shape(),
        scratch_shapes=[pltpu.VMEM((hg, N, N), F32)] + ex.scratch(),
        compiler_params=_cparams(("arbitrary", "arbitrary")), name="rwkv_scan_fwd")(r, lw, k, v, kk, b, *ex.operands())
    return res[0], (res[1], res[2]), list(res[3:])


def rwkv_scan_bwd(r, lw, k, v, kk, b, saved, dy, ex):
    T, N = r.shape[0], HEAD_DIM
    H = r.shape[1] // N
    groups = SCAN_GROUPS_BWD
    nc, hg = T // CHUNK, H // groups
    seq = pl.BlockSpec((CHUNK, hg * N), lambda g, c: (nc - 1 - c, g))
    state = pl.BlockSpec((1, hg, N, N), lambda g, c: (nc - 1 - c, g, 0, 0))

    def body(*refs):
        (r_ref, lw_ref, k_ref, v_ref, kk_ref, b_ref, hs_ref, xs_ref, dy_ref), outs, (dh_scr,), ex_refs = _ex_split(ex, refs, 9, 6)
        g, c = pl.program_id(0), pl.program_id(1)

        @pl.when(jnp.logical_and(g == 0, c == 0))
        def _():
            ex.start(*ex_refs)

        @pl.when(c == 0)
        def _():
            dh_scr[...] = jnp.zeros_like(dh_scr)

        x_known = xs_ref[0]
        _, vjp = jax.vjp(lambda *a: f_rwkv_chunk(*a, x_known=x_known)[:2], hs_ref[0],
                         *[_split_heads(z[...]) for z in (r_ref, lw_ref, k_ref, v_ref, kk_ref, b_ref)])
        d = vjp((_split_heads(dy_ref[...]), dh_scr[...]))
        dh_scr[...] = d[0]
        for o_ref, dz in zip(outs, d[1:]):
            o_ref[...] = _merge_heads(dz)

        @pl.when(jnp.logical_and(g == groups - 1, c == nc - 1))
        def _():
            ex.forward(*ex_refs)
            ex.wait(*ex_refs)

    res = pl.pallas_call(
        body, grid=(groups, nc),
        in_specs=[seq] * 6 + [state, state, seq] + [_ANY] * ex.n,
        out_specs=[seq] * 6 + [_ANY] * ex.n, out_shape=[jax.ShapeDtypeStruct((T, H * N), F32)] * 6 + ex.out_shape(),
        scratch_shapes=[pltpu.VMEM((hg, N, N), F32)] + ex.scratch(),
        compiler_params=_cparams(("arbitrary", "arbitrary")), name="rwkv_scan_bwd")(r, lw, k, v, kk, b, *saved, dy, *ex.operands())
    return list(res[:6]), list(res[6:])


GROUP_COLS = 4 * HEAD_DIM


def _f_dilattn(has_prev, q, kc, kp, vc, vp):
    scale = 1.0 / math.sqrt(HEAD_DIM)
    i = lax.broadcasted_iota(jnp.int32, (DIL_BLOCK, DIL_BLOCK), 0)
    j = lax.broadcasted_iota(jnp.int32, (DIL_BLOCK, DIL_BLOCK), 1)
    o, l = jnp.zeros_like(q), jnp.zeros_like(q)
    for h in range(q.shape[1] // HEAD_DIM):
        m = _head_mask(q.shape[1], h)
        sc = jnp.where(j <= i, mm_nt(q * m, kc) * scale, NEG_INF)
        sp = jnp.where(jnp.logical_and(i <= j, has_prev), mm_nt(q * m, kp) * scale, NEG_INF)
        mx = jnp.maximum(jnp.max(sc, axis=-1, keepdims=True), jnp.max(sp, axis=-1, keepdims=True))
        pc, pp = jnp.exp(sc - mx), jnp.exp(sp - mx)
        den = jnp.sum(pc, axis=-1, keepdims=True) + jnp.sum(pp, axis=-1, keepdims=True)
        o = o + (mm_nn(pc, vc) + mm_nn(pp, vp)) / den * m
        l = l + (mx + jnp.log(den)) * m
    return o, l


def _dil_specs(gi, d):
    parts = 1 if d == 1 else 2
    blk = (DIL_BLOCK * d, GROUP_COLS // parts)
    at = lambda col: (lambda p, n: (n, col * parts + p))
    before = lambda col: (lambda p, n: (jnp.maximum(n - 1, 0), col * parts + p))
    v0 = DIL_WIDTH // GROUP_COLS + gi
    q = pl.BlockSpec(blk, at(gi))
    kc, kp = pl.BlockSpec(blk, at(gi)), pl.BlockSpec(blk, before(gi))
    vc, vp = pl.BlockSpec(blk, at(v0)), pl.BlockSpec(blk, before(v0))
    out = pl.BlockSpec(blk, at(0))
    together = min(d, 2)
    return (q, kc, kp, vc, vp, out), parts, together


def _residue_rows(r, d):
    return pl.ds(r, DIL_BLOCK, stride=d) if d > 1 else pl.ds(0, DIL_BLOCK)


def dil_fwd(q, k, kv, gi, d, name):
    T = q.shape[0]
    (qs, kc, kp, vc, vp, out), parts, together = _dil_specs(gi, d)

    def body(q_ref, kc_ref, kp_ref, vc_ref, vp_ref, o_ref, l_ref):
        has_prev = pl.program_id(1) > 0

        def residues(it, carry):
            rows = [_residue_rows(it * together + a, d) for a in range(together)]
            ins = [[ref[rw, :] for ref in (q_ref, kc_ref, kp_ref, vc_ref, vp_ref)] for rw in rows]
            res = [_f_dilattn(has_prev, *x) for x in ins]
            for rw, (o, l) in zip(rows, res):
                o_ref[rw, :] = o
                l_ref[rw, :] = l
            return carry

        lax.fori_loop(0, d // together, residues, 0)

    shape = jax.ShapeDtypeStruct((T, 4 * HEAD_DIM), F32)
    return pl.pallas_call(
        body, grid=(parts, T // (DIL_BLOCK * d)), in_specs=[qs, kc, kp, vc, vp], out_specs=[out, out], out_shape=[shape, shape],
        compiler_params=_cparams(("parallel", "parallel")), name=name)(q, k, k, kv, kv)


def dil_bwd(q, k, kv, do, dl, gi, d, name):
    T = q.shape[0]
    (qs, kc, kp, vc, vp, out), parts, together = _dil_specs(gi, d)

    def body(q_ref, kc_ref, kp_ref, vc_ref, vp_ref, do_ref, dl_ref, *outs):
        f = functools.partial(_f_dilattn, pl.program_id(1) > 0)

        def residues(it, carry):
            rows = [_residue_rows(it * together + a, d) for a in range(together)]
            ins = [[ref[rw, :] for ref in (q_ref, kc_ref, kp_ref, vc_ref, vp_ref, do_ref, dl_ref)] for rw in rows]
            res = [jax.vjp(f, *x[:5])[1]((x[5], x[6])) for x in ins]
            for rw, gs in zip(rows, res):
                for o_ref, g in zip(outs, gs):
                    o_ref[rw, :] = g
            return carry

        lax.fori_loop(0, d // together, residues, 0)

    shape = jax.ShapeDtypeStruct((T, 4 * HEAD_DIM), F32)
    dq, dkc, dkp, dvc, dvp = pl.pallas_call(
        body, grid=(parts, T // (DIL_BLOCK * d)), in_specs=[qs, kc, kp, vc, vp, out, out], out_specs=[out] * 5, out_shape=[shape] * 5,
        compiler_params=_cparams(("parallel", "parallel")), name=name)(q, k, k, kv, kv, do, dl)

    def own_plus_next(c, p):
        return c + jnp.concatenate([p[DIL_BLOCK * d:], jnp.zeros_like(p[:DIL_BLOCK * d])], axis=0)

    return dq, own_plus_next(dkc, dkp), own_plus_next(dvc, dvp)


CONV_TILE = 256


def _conv3(before, u, w, b):
    ue = jnp.concatenate([before, u], axis=0)
    s1, s2 = pltpu.roll(ue, 1, 0)[8:], pltpu.roll(ue, 2, 0)[8:]
    return b + w[0:1] * s2 + w[1:2] * s1 + w[2:3] * u, s1, s2


def _conv_halves(u_ref, h_ref, cw_ref, cb_ref):
    F = D_FF
    res = []
    for lo in (0, F):
        before = jnp.where(pl.program_id(0) > 0, h_ref[:, lo:lo + F], 0.0)
        u = u_ref[:, lo:lo + F]
        res.append((u,) + _conv3(before, u, cw_ref[:, lo:lo + F], cb_ref[:, lo:lo + F]))
    return res


def _halo_before(C):
    return pl.BlockSpec((8, C), lambda i: (jnp.maximum(i * (CONV_TILE // 8) - 1, 0), 0))


def convgate_fwd(u, cw, cb, name):
    T, C = u.shape
    F = C // 2

    def body(u_ref, h_ref, cw_ref, cb_ref, z_ref):
        (_, cg, _, _), (_, cv, _, _) = _conv_halves(u_ref, h_ref, cw_ref, cb_ref)
        z_ref[...] = (cg * _sigmoid(cg) * cv).astype(BF16)

    return pl.pallas_call(
        body, grid=(T // CONV_TILE,),
        in_specs=[pl.BlockSpec((CONV_TILE, C), lambda i: (i, 0)), _halo_before(C), _full_spec(cw), _full_spec(cb)],
        out_specs=pl.BlockSpec((CONV_TILE, F), lambda i: (i, 0)), out_shape=jax.ShapeDtypeStruct((T, F), BF16),
        compiler_params=_cparams(("parallel",)), name=name)(u, u, cw, cb)


def convgate_bwd(u, cw, cb, dz, name):
    T, C = u.shape
    F = C // 2
    n = T // CONV_TILE
    E = CONV_TILE + 8

    def body(u_ref, hb_ref, ha_ref, cw_ref, cb_ref, dz_ref, dza_ref, du_ref, dcw_ref, dcb_ref):
        i = pl.program_id(0)
        dze = jnp.concatenate([dz_ref[...], jnp.where(i < n - 1, dza_ref[...], 0.0)], axis=0)

        @pl.when(i == 0)
        def _():
            dcw_ref[...] = jnp.zeros_like(dcw_ref)
            dcb_ref[...] = jnp.zeros_like(dcb_ref)

        halves = []
        for lo in (0, F):
            sl = slice(lo, lo + F)
            ue = jnp.concatenate([u_ref[:, sl], ha_ref[:, sl]], axis=0)
            c, s1, s2 = _conv3(jnp.where(i > 0, hb_ref[:, sl], 0.0), ue, cw_ref[:, sl], cb_ref[:, sl])
            halves.append((sl, ue, c, s1, s2))
        (_, _, cg, _, _), (_, _, cv, _, _) = halves
        sg = _sigmoid(cg)
        dcs = (dze * cv * sg * (1.0 + cg * (1.0 - sg)), dze * cg * sg)
        for (sl, ue, _, s1, s2), dc in zip(halves, dcs):
            own = lambda z: z[:CONV_TILE]
            dcb_ref[:, sl] += jnp.sum(own(dc), axis=0, keepdims=True)
            dcw_ref[0:1, sl] += jnp.sum(own(dc * s2), axis=0, keepdims=True)
            dcw_ref[1:2, sl] += jnp.sum(own(dc * s1), axis=0, keepdims=True)
            dcw_ref[2:3, sl] += jnp.sum(own(dc * ue), axis=0, keepdims=True)
            du = cw_ref[2:3, sl] * dc + cw_ref[1:2, sl] * pltpu.roll(dc, E - 1, 0) + cw_ref[0:1, sl] * pltpu.roll(dc, E - 2, 0)
            du_ref[:, sl] = own(du).astype(BF16)

    after = lambda w: pl.BlockSpec((8, w), lambda i: (jnp.minimum((i + 1) * (CONV_TILE // 8), T // 8 - 1), 0))
    return pl.pallas_call(
        body, grid=(n,),
        in_specs=[pl.BlockSpec((CONV_TILE, C), lambda i: (i, 0)), _halo_before(C), after(C), _full_spec(cw), _full_spec(cb),
                  pl.BlockSpec((CONV_TILE, F), lambda i: (i, 0)), after(F)],
        out_specs=[pl.BlockSpec((CONV_TILE, C), lambda i: (i, 0)), _full_spec(cw), _full_spec(cb)],
        out_shape=[jax.ShapeDtypeStruct((T, C), BF16), jax.ShapeDtypeStruct(cw.shape, F32), jax.ShapeDtypeStruct(cb.shape, F32)],
        compiler_params=_cparams(("arbitrary",)), name=name)(u, u, u, cw, cb, dz, dz)


def loss_head(y, tgt):
    T, D = y.shape
    tile = ROW_TILE

    def body(y_ref, t_ref, l_ref, d_ref, db_ref):
        d = y_ref[...] - t_ref[...]
        d_ref[...] = d * (1.0 / D)
        db_ref[...] = (d * (1.0 / D)).astype(BF16)

        @pl.when(pl.program_id(0) == 0)
        def _():
            l_ref[...] = jnp.zeros_like(l_ref)

        l_ref[...] += (0.5 / D) * jnp.sum(d * d)

    row = pl.BlockSpec((tile, D), lambda i: (i, 0))
    return pl.pallas_call(
        body, grid=(T // tile,), in_specs=[row, row], out_specs=[pl.BlockSpec((8, 128), lambda i: (0, 0)), row, row],
        out_shape=[jax.ShapeDtypeStruct((8, 128), F32), jax.ShapeDtypeStruct((T, D), F32), jax.ShapeDtypeStruct((T, D), BF16)],
        compiler_params=_cparams(("arbitrary",)), name="loss_head")(y, tgt)


def sum_parts(parts, name):
    S, R, C = parts.shape
    tile = _pick(R, (256, 128, 64, 32, 16, 8))

    def body(p_ref, o_ref):
        acc = p_ref[0].astype(F32)
        for s in range(1, S):
            acc = acc + p_ref[s].astype(F32)
        o_ref[...] = acc

    return pl.pallas_call(
        body, grid=(R // tile,), in_specs=[pl.BlockSpec((S, tile, C), lambda i: (0, i, 0))],
        out_specs=pl.BlockSpec((tile, C), lambda i: (i, 0)), out_shape=jax.ShapeDtypeStruct((R, C), F32),
        compiler_params=_cparams(("parallel",)), name=name)(parts)


def adamw(gparts, w, m, v, name, ex=None):
    S, R, C = gparts.shape
    tile = _pick(R, (256, 128, 64, 32, 16, 8))
    steps = R // tile
    c1 = 1.0 / (1.0 - ADAM_B1 ** ADAM_STEP)
    c2 = 1.0 / (1.0 - ADAM_B2 ** ADAM_STEP)
    ex = ex or Exchange()

    def body(*refs):
        (g_ref, w_ref, m_ref, v_ref), (go_ref, d_ref, mo_ref, vo_ref), _, ex_refs = _ex_split(ex, refs, 4, 4)

        @pl.when(pl.program_id(0) == 0)
        def _():
            ex.start(*ex_refs)

        g = g_ref[0].astype(F32)
        for s in range(1, S):
            g = g + g_ref[s].astype(F32)
        m1 = ADAM_B1 * m_ref[...] + (1.0 - ADAM_B1) * g
        v1 = ADAM_B2 * v_ref[...] + (1.0 - ADAM_B2) * (g * g)
        go_ref[...] = g
        mo_ref[...] = m1
        vo_ref[...] = v1
        d_ref[...] = -ADAM_LR * ((m1 * c1) / (jnp.sqrt(v1 * c2) + ADAM_EPS) + ADAM_WD * w_ref[...])

        @pl.when(pl.program_id(0) == steps - 1)
        def _():
            ex.forward(*ex_refs)
            ex.wait(*ex_refs)

    row = pl.BlockSpec((tile, C), lambda i: (i, 0))
    res = pl.pallas_call(
        body, grid=(steps,), in_specs=[pl.BlockSpec((S, tile, C), lambda i: (0, i, 0)), row, row, row] + [_ANY] * ex.n,
        out_specs=[row] * 4 + [_ANY] * ex.n, out_shape=[jax.ShapeDtypeStruct((R, C), F32)] * 4 + ex.out_shape(),
        scratch_shapes=ex.scratch() if ex.n else [],
        compiler_params=_cparams(("arbitrary",) if ex.n else ("parallel",)), name=name)(gparts, w, m, v, *ex.operands())
    return (list(res[:4]), list(res[4:])) if ex.n else list(res)


def _peers():
    x, y, c = lax.axis_index("x"), lax.axis_index("y"), lax.axis_index("c")
    peers = []
    for k in range(1, N_DEV):
        px = 1 - x if k & 4 else x
        py = 1 - y if k & 2 else y
        pc = 1 - c if k & 1 else c
        peers.append(((px, py, pc), 4 * px + 2 * py + pc))
    return 4 * x + 2 * y + c, peers


_ANY = pl.BlockSpec(memory_space=pl.ANY)


class Exchange:
    def __init__(self, gathers=(), scatters=()):
        self.gathers, self.scatters = list(gathers), list(scatters)
        self.n = len(self.gathers) + len(self.scatters)

    def operands(self):
        return self.gathers + self.scatters

    def out_shape(self):
        return ([jax.ShapeDtypeStruct((N_DEV,) + x.shape, x.dtype) for x in self.gathers]
                + [jax.ShapeDtypeStruct(x.shape, x.dtype) for x in self.scatters])

    def scratch(self):
        n = max(self.n, 1)
        return [pltpu.SemaphoreType.DMA((7 * n,)), pltpu.SemaphoreType.DMA((7 * n,)), pltpu.SemaphoreType.DMA((n,))]

    def _copies(self, in_refs, out_refs, send_sems, recv_sems, local_sems):
        me, peers = _peers()
        ng = len(self.gathers)
        local, sends, recvs = [], [], []
        for a in range(self.n):
            x, o = in_refs[a], out_refs[a]
            mine = x if a < ng else x.at[me]
            local.append(pltpu.make_async_copy(mine, o.at[me], local_sems.at[a]))
            s_a, r_a = {}, {}
            for k in range(1, N_DEV):
                peer, slot = peers[k - 1]
                sems = dict(send_sem=send_sems.at[7 * a + k - 1], recv_sem=recv_sems.at[7 * a + k - 1],
                            device_id_type=pl.DeviceIdType.MESH)
                if a >= ng:
                    s_a[k] = pltpu.make_async_remote_copy(src_ref=x.at[slot], dst_ref=o.at[me], device_id=peer, **sems)
                elif k in FORWARDED:
                    came = o.at[peers[k - 2][1]]
                    s_a[k] = pltpu.make_async_remote_copy(src_ref=came, dst_ref=came, device_id=peers[0][0], **sems)
                else:
                    s_a[k] = pltpu.make_async_remote_copy(src_ref=x, dst_ref=o.at[me], device_id=peer, **sems)
                r_a[k] = pltpu.make_async_remote_copy(src_ref=mine, dst_ref=o.at[slot], device_id=peer, **sems)
            sends.append(s_a)
            recvs.append(r_a)
        return local, sends, recvs

    def start(self, *refs):
        if self.n == 0:
            return
        local, sends, _ = self._copies(*refs)
        for a in range(self.n):
            local[a].start()
            for k in range(1, N_DEV):
                if a >= len(self.gathers) or k not in FORWARDED:
                    sends[a][k].start()

    def forward(self, *refs):
        if not self.gathers:
            return
        _, sends, recvs = self._copies(*refs)
        for a in range(len(self.gathers)):
            for k in FORWARDED:
                recvs[a][k - 1].wait_recv()
                sends[a][k].start()

    def wait(self, *refs):
        if self.n == 0:
            return
        local, sends, recvs = self._copies(*refs)
        for a in range(self.n):
            waited_early = [f - 1 for f in FORWARDED] if a < len(self.gathers) else []
            for k in range(1, N_DEV):
                if k not in waited_early:
                    recvs[a][k].wait_recv()
            for k in range(1, N_DEV):
                sends[a][k].wait_send()
            local[a].wait()


FORWARDED = (3, 5, 7)


def exchange(ex, name):
    n = ex.n

    def body(*refs):
        args = (refs[:n], refs[n:2 * n]) + tuple(refs[2 * n:])
        ex.start(*args)
        ex.forward(*args)
        ex.wait(*args)

    return pl.pallas_call(body, in_specs=[_ANY] * n, out_specs=[_ANY] * n, out_shape=ex.out_shape(),
                          scratch_shapes=ex.scratch(), name=name)(*ex.operands())


def _shift_up(z):
    return jnp.concatenate([z[1:], jnp.zeros_like(z[:1])], axis=0)


def _segments(width):
    seg = np.zeros((width, 128), np.float32)
    seg[np.arange(width), np.arange(width) // HEAD_DIM] = 1.0
    return jnp.asarray(seg), jnp.asarray(seg.T)


def _rope_consts(T):
    inv = ROPE_THETA ** (-jnp.arange(0, HEAD_DIM, 2, dtype=F32) / HEAD_DIM)
    ang = jnp.arange(T, dtype=F32)[:, None] * inv[None, :]
    return jnp.tile(jnp.cos(ang), (1, 4)), jnp.tile(jnp.sin(ang), (1, 4))


def _per_head(g, heads):
    return jnp.tile(g.reshape(1, HEAD_DIM), (1, heads))


def _sum_heads(g):
    return g.reshape(-1, HEAD_DIM).sum(axis=0, keepdims=True)


LORA_COLS = 256
RW_TILE = 256
ROW_TILE = 512


def _local_step(x0, memx, tgt, P, ex_weights=None, weights_done=None, ex_grads=None):
    T = x0.shape[0]
    P = dict(P)
    G = {}
    seg, seg_t = _segments(RWKV_WIDTH)
    mseg = (seg[:MEM_WIDTH], seg_t[:, :MEM_WIDTH])
    cos, sin = _rope_consts(T)
    row = lambda v: v.reshape(1, -1)

    def mem_fwd(i, q, into):
        memn = stage_fwd(f_rmsnorm, [memx], [P["mem_norm"][i:i + 1]], [], [], N_MEM, f"mem{i}_norm", [BF16])[0]
        kvm = matmul(memn, P["mem_w_kv"][i], "nn", f"mem{i}_kv")
        kn, qn = _per_head(P["mem_k_norm"][i], MEM_HEADS), _per_head(P["mem_q_norm"][i], MEM_HEADS)
        km = stage_fwd(f_headnorm, [Cols(kvm, MEM_WIDTH, 0)], [kn], [], mseg, N_MEM, f"mem{i}_knorm")[0]
        om = stage_fwd(f_memattn, [q], [km, Cols(kvm, MEM_WIDTH, 1), qn], [], mseg, ROW_TILE, f"mem{i}_attn", [BF16], into=into)[0]
        return om, (memn, kvm, km, kn, qn, q)

    def mem_bwd(i, saved, dymem, copy_into=None):
        memn, kvm, km, kn, qn, q = saved
        (dq,), (dkm, dvm, g_qn), *copy = stage_bwd(f_memattn, [q], [km, Cols(kvm, MEM_WIDTH, 1), qn], [], mseg, [dymem], ROW_TILE,
                                                   f"mem{i}_attn_bwd", bf16_copies=(0,) if copy_into else (), copy_into=copy_into)
        dq = copy[0][0] if copy_into else dq
        (dkraw,), (g_kn,) = stage_bwd(f_headnorm, [Cols(kvm, MEM_WIDTH, 0)], [kn], [], mseg, [dkm], N_MEM, f"mem{i}_knorm_bwd")
        dkvm = jnp.concatenate([dkraw, dvm], axis=1).astype(BF16)
        g_w = matmul(memn, dkvm, "tn", f"mem{i}_kv_dw")
        dmemn = matmul(dkvm, P["mem_w_kv"][i], "nt", f"mem{i}_kv_dx")
        _, (g_mn,) = stage_bwd(f_rmsnorm, [memx], [P["mem_norm"][i:i + 1]], [], [], [dmemn], N_MEM, f"mem{i}_norm_bwd")
        return dq, g_mn, g_w, _sum_heads(g_qn), _sum_heads(g_kn)

    def ffn_fwd(i, xin):
        hn = stage_fwd(f_rmsnorm, [xin], [P["ffn_norm"][i:i + 1]], [], [], ROW_TILE, f"ffn{i}_norm", [BF16])[0]
        u = matmul(hn, P["ffn_w_up"][i], "nt", f"ffn{i}_up")
        z = convgate_fwd(u, P["ffn_conv_w"][i], P["ffn_conv_b"][i:i + 1], f"ffn{i}_conv")
        return matmul(z, P["ffn_w_down"][i], "nn", f"ffn{i}_down", residual=xin), (hn, u, z)

    def ffn_bwd(i, xin, saved, dxo, dxo_b):
        hn, u, z = saved
        dz = matmul(dxo_b, P["ffn_w_down"][i], "nt", f"ffn{i}_down_dx")
        g_down = matmul(z, dxo_b, "tn", f"ffn{i}_down_dw")
        du, g_cw, g_cb = convgate_bwd(u, P["ffn_conv_w"][i], P["ffn_conv_b"][i:i + 1], dz, f"ffn{i}_conv_bwd")
        dhn = matmul(du, P["ffn_w_up"][i], "nn", f"ffn{i}_up_dx")
        g_up = matmul(du, hn, "tn", f"ffn{i}_up_dw")
        (dxin,), (g_n,), (dxin_b,) = stage_bwd(f_rmsnorm_res, [xin], [P["ffn_norm"][i:i + 1]], [], [], [dhn, dxo], ROW_TILE,
                                               f"ffn{i}_norm_bwd", bf16_copies=(0,))
        return dxin, dxin_b, g_n, g_up, g_cw, g_cb, g_down

    h0 = stage_fwd(f_rmsnorm, [x0], [P["attn_norm"][0:1]], [], [], ROW_TILE, "l0_norm", [BF16])[0]
    p0 = matmul(h0, P["a_w_in"][0], "nn", "l0_in")
    lora0 = 3 * RWKV_WIDTH // LORA_COLS
    pre_xs = [Cols(p0, RWKV_WIDTH, 0), Cols(p0, RWKV_WIDTH, 1), Cols(p0, RWKV_WIDTH, 2), Cols(p0, LORA_COLS, lora0)]
    mu = [Cols(P["a_mu"], RWKV_WIDTH, 0), Cols(P["a_mu"], RWKV_WIDTH, 1), Cols(P["a_mu"], RWKV_WIDTH, 2),
          Cols(P["a_mu"], LORA_COLS, lora0)]
    lora_rows = lambda w, lo: jnp.pad(w, ((lo, LORA_COLS - lo - w.shape[0]), (0, 0)))
    pre_ps = mu + [P["a_w0"], lora_rows(P["a_w2"][0], 0), P["a_a0"], lora_rows(P["a_a2"][0], 64), lora_rows(P["a_g2"][0], 128),
                   P["a_k_k"], P["a_k_a"]]
    r, lw, k2, v, kk, b, g = stage_fwd(f_rwkv_pre, pre_xs, pre_ps, [], [seg, seg_t], RW_TILE, "l0_rwkv_pre", with_prev=True)
    scan_in = [r, lw, k2, v, kk, b]
    y_h, h_states, got = rwkv_scan_fwd(*scan_in, ex_weights or Exchange())
    if weights_done is not None:
        P.update(weights_done(got))
    y_s = y_h
    post_ps = [P["a_lnx_w"], P["a_lnx_b"], P["a_r_k"].reshape(1, RWKV_WIDTH)]
    ycat0, mem0_saved = mem_fwd(0, Cols(p0, MEM_WIDTH, SHIFT_WIDTH // MEM_WIDTH), Into(None, D_MODEL, RWKV_WIDTH // MEM_WIDTH))
    ycat0 = stage_fwd(f_rwkv_post, [y_s, r, k2, v, g], post_ps, [], [seg, seg_t], RW_TILE, "l0_rwkv_post", [BF16],
                      into=Into(ycat0, D_MODEL, 0))[0]
    x1 = matmul(ycat0, P["a_w_out"][0], "nn", "l0_out", residual=x0)
    x2, ffn0_saved = ffn_fwd(0, x1)

    hk, h1 = stage_fwd(f_rmsnorm2, [x2], [row(P["kv_norm"]), P["attn_norm"][1:2]], [], [], ROW_TILE, "l1_norm", [BF16, BF16])
    kvp = matmul(hk, P["kv_w"][0], "nn", "l1_kv")
    p1 = matmul(h1, P["b_w_in"][0], "nn", "l1_in")
    kraw, qraw = Cols(kvp, DIL_WIDTH, 0), Cols(p1, DIL_WIDTH, 0)
    kgain, qgain = _per_head(P["kv_k_norm"], DIL_WIDTH // HEAD_DIM), _per_head(P["b_q_norm"], DIL_WIDTH // HEAD_DIM)
    ksh = stage_fwd(f_qkprep, [kraw], [kgain], [cos, sin], [seg, seg_t], ROW_TILE, "l1_kprep")[0]
    q = stage_fwd(f_qkprep, [qraw], [qgain], [cos, sin], [seg, seg_t], ROW_TILE, "l1_qprep")[0]
    outs, lses = [], []
    for gi, (_, d) in enumerate(DIL_GROUPS):
        og, lg = dil_fwd(q, ksh, kvp, gi, d, f"l1_dil{gi}")
        outs.append(og)
        lses.append(lg)
    ycat1 = stage_fwd(f_mix, outs + lses, [], [], [], ROW_TILE, "l1_mix", [BF16], into=Into(None, 2 * MEM_WIDTH, 0))[0]
    ycat1, mem1_saved = mem_fwd(1, Cols(p1, MEM_WIDTH, DIL_WIDTH // MEM_WIDTH), Into(ycat1, 2 * MEM_WIDTH, 1))
    x3 = matmul(ycat1, P["b_w_out"][0], "nn", "l1_out", residual=x2)
    x4, ffn1_saved = ffn_fwd(1, x3)
    loss_part, dx4, dx4_b = loss_head(x4, tgt)

    dx3, dx3_b, gn1, gup1, gcw1, gcb1, gdown1 = ffn_bwd(1, x3, ffn1_saved, dx4, dx4_b)
    dycat1 = matmul(dx3_b, P["b_w_out"][0], "nt", "l1_out_dx")
    G["b_w_out"] = [matmul(ycat1, dx3_b, "tn", "l1_out_dw")]
    dp1, gmn1, gmw1, gmq1, gmk1 = mem_bwd(1, mem1_saved, Cols(dycat1, MEM_WIDTH, 1), Into(None, D_MODEL, DIL_WIDTH // MEM_WIDTH))
    dmix, _ = stage_bwd(f_mix, outs + lses, [], [], [], [Cols(dycat1, MEM_WIDTH, 0)], ROW_TILE, "l1_mix_bwd")
    dq, dk, dv = zip(*[dil_bwd(q, ksh, kvp, dmix[gi], dmix[3 + gi], gi, d, f"l1_dil{gi}_bwd")
                       for gi, (_, d) in enumerate(DIL_GROUPS)])
    dq, dk, dv = jnp.concatenate(dq, axis=1), jnp.concatenate(dk, axis=1), jnp.concatenate(dv, axis=1)
    _, (g_bq,), (dp1,) = stage_bwd(f_qkprep, [qraw], [qgain], [cos, sin], [seg, seg_t], [dq], ROW_TILE, "l1_qprep_bwd",
                                   bf16_copies=(0,), copy_into=Into(dp1, D_MODEL, 0))
    dkvp = jnp.pad(dv.astype(BF16), ((0, 0), (DIL_WIDTH, 0)))
    _, (g_kk,), (dkvp,) = stage_bwd(f_qkprep, [kraw], [kgain], [cos, sin], [seg, seg_t], [dk], ROW_TILE, "l1_kprep_bwd",
                                    bf16_copies=(0,), copy_into=Into(dkvp, 2 * DIL_WIDTH, 0))
    g_bq, g_kk = _sum_heads(g_bq), _sum_heads(g_kk)
    dh1 =matmul(dp1, P["b_w_in"][0], "nt", "l1_in_dx")
    G["b_w_in"] = [matmul(h1, dp1, "tn", "l1_in_dw")]
    dhk = matmul(dkvp, P["kv_w"][0], "nt", "l1_kv_dx")
    G["kv_w"] = [matmul(hk, dkvp, "tn", "l1_kv_dw")]
    (dx2,), (g_kvn, g_an1), (dx2_b,) = stage_bwd(f_rmsnorm2_res, [x2], [row(P["kv_norm"]), P["attn_norm"][1:2]], [], [],
                                                 [dhk, dh1, dx3], ROW_TILE, "l1_norm_bwd", bf16_copies=(0,))

    dx1, dx1_b, gn0, gup0, gcw0, gcb0, gdown0 = ffn_bwd(0, x1, ffn0_saved, dx2, dx2_b)
    dycat0 = matmul(dx1_b, P["a_w_out"][0], "nt", "l0_out_dx")
    G["a_w_out"] = [matmul(ycat0, dx1_b, "tn", "l0_out_dw")]
    dqmem0, gmn0, gmw0, gmq0, gmk0 = mem_bwd(0, mem0_saved, Cols(dycat0, MEM_WIDTH, RWKV_WIDTH // MEM_WIDTH))
    (dy_s, dr_a, dk_a, dv_a, dg), (g_lw, g_lb, g_rk) = stage_bwd(
        f_rwkv_post, [y_s, r, k2, v, g], post_ps, [], [seg, seg_t], [Cols(dycat0, RWKV_WIDTH, 0)], RW_TILE, "l0_rwkv_post_bwd")
    G["mem_w_kv"], G["ffn_w_up"], G["ffn_w_down"] = [gmw0, gmw1], [gup0, gup1], [gdown0, gdown1]
    (dr_b, dlw, dk_b, dv_b, dkk, db), G["_exchanged"] = rwkv_scan_bwd(*scan_in, h_states, dy_s,
                                                                      ex_grads(G) if ex_grads else Exchange())
    dpre, gpre = stage_bwd(f_rwkv_pre, pre_xs, pre_ps, [], [seg, seg_t],
                           [[dr_a, dr_b], dlw, [dk_a, dk_b], [dv_a, dv_b], dkk, db, dg], RW_TILE, "l0_rwkv_pre_bwd", with_prev=True)
    dp_rw = jnp.concatenate(dpre[:4], axis=1) + _shift_up(jnp.concatenate(dpre[4:], axis=1))
    dp0 = jnp.concatenate([dp_rw, dqmem0], axis=1).astype(BF16)
    dh0 = matmul(dp0, P["a_w_in"][0], "nt", "l0_in_dx")
    G["a_w_in"] = [matmul(h0, dp0, "tn", "l0_in_dw")]
    (dx0,), (g_an0,) = stage_bwd(f_rmsnorm_res, [x0], [P["attn_norm"][0:1]], [], [], [dh0, dx1], ROW_TILE, "l0_norm_bwd")

    G["attn_norm"] = jnp.concatenate([g_an0, g_an1], axis=0)
    G["a_mu"] = jnp.concatenate(gpre[:4], axis=1)
    G["a_w0"], G["a_w2"], G["a_a0"], G["a_a2"], G["a_g2"] = gpre[4], gpre[5][None, :64], gpre[6], gpre[7][None, 64:128], gpre[8][None, 128:]
    G["a_k_k"], G["a_k_a"] = gpre[9], gpre[10]
    G["a_r_k"] = g_rk.reshape(1, RWKV_HEADS, HEAD_DIM)
    G["a_lnx_w"], G["a_lnx_b"] = g_lw, g_lb
    G["kv_norm"], G["kv_k_norm"], G["b_q_norm"] = g_kvn.reshape(-1), g_kk.reshape(-1), g_bq
    G["mem_norm"] = jnp.concatenate([gmn0, gmn1], axis=0)
    G["mem_w_kv"] = [gmw0, gmw1]
    G["mem_q_norm"] = jnp.concatenate([gmq0, gmq1], axis=0)
    G["mem_k_norm"] = jnp.concatenate([gmk0, gmk1], axis=0)
    G["ffn_norm"] = jnp.concatenate([gn0, gn1], axis=0)
    G["ffn_w_up"] = [gup0, gup1]
    G["ffn_conv_w"] = jnp.stack([gcw0, gcw1])
    G["ffn_conv_b"] = jnp.concatenate([gcb0, gcb1], axis=0)
    G["ffn_w_down"] = [gdown0, gdown1]
    return loss_part, dx0, G


PARAMS = (("attn_norm", None), ("a_w_in", 2), ("a_mu", 1), ("a_w0", 1), ("a_w2", 2), ("a_a0", 1), ("a_a2", 2), ("a_g2", 2),
          ("a_k_k", 1), ("a_k_a", 1), ("a_r_k", None), ("a_lnx_w", 1), ("a_lnx_b", 1), ("a_w_out", 1), ("kv_norm", None),
          ("kv_w", 1), ("kv_k_norm", None), ("b_w_in", 1), ("b_q_norm", None), ("b_w_out", 2), ("mem_norm", None),
          ("mem_w_kv", 1), ("mem_q_norm", None), ("mem_k_norm", None), ("ffn_norm", None), ("ffn_w_up", 2),
          ("ffn_conv_w", 2), ("ffn_conv_b", None), ("ffn_w_down", 1))
BIG = ("a_w_in", "a_w_out", "kv_w", "b_w_in", "b_w_out", "mem_w_kv", "ffn_w_up", "ffn_w_down")
TRANSPOSED = ("ffn_w_up",)
AXIS = dict(PARAMS)
SMALL = tuple(n for n, _ in PARAMS if n not in BIG)
SMALL_SHARDED = tuple(n for n in SMALL if AXIS[n] is not None)
PACK_QUANTUM = 256 * 128


def _from_shards(xs, axis):
    full = jnp.moveaxis(xs, 0, axis)
    sh = full.shape
    return full.reshape(sh[:axis] + (sh[axis] * sh[axis + 1],) + sh[axis + 2:])


def _to_shards(g, axis):
    sh = g.shape
    return jnp.moveaxis(g.reshape(sh[:axis] + (N_DEV, sh[axis] // N_DEV) + sh[axis + 1:]), axis, 0)


def _pack(parts, lead=0):
    ld = parts[0].shape[:lead]
    flat = jnp.concatenate([p.reshape(ld + (-1,)) for p in parts], axis=-1)
    pad = (-flat.shape[-1]) % PACK_QUANTUM
    flat = jnp.pad(flat, [(0, 0)] * lead + [(0, pad)])
    return flat.reshape(ld + (-1, 128))


def _unpack(packed, shapes, lead=0):
    ld = packed.shape[:lead]
    flat = packed.reshape(ld + (-1,))
    out, off = [], 0
    for s in shapes:
        n = math.prod(s)
        out.append(flat[..., off:off + n].reshape(ld + tuple(s)))
        off += n
    return out


def kernel(x, mem, attn_norm, a_w_in, a_mu, a_w0, a_w2, a_a0, a_a2, a_g2, a_k_k, a_k_a, a_r_k, a_lnx_w, a_lnx_b, a_w_out, kv_norm, kv_w, kv_k_norm, b_w_in, b_q_norm, b_w_out, mem_norm, mem_w_kv, mem_q_norm, mem_k_norm, ffn_norm, ffn_w_up, ffn_conv_w, ffn_conv_b, ffn_w_down, loss_target, m_attn_norm, m_a_w_in, m_a_mu, m_a_w0, m_a_w2, m_a_a0, m_a_a2, m_a_g2, m_a_k_k, m_a_k_a, m_a_r_k, m_a_lnx_w, m_a_lnx_b, m_a_w_out, m_kv_norm, m_kv_w, m_kv_k_norm, m_b_w_in, m_b_q_norm, m_b_w_out, m_mem_norm, m_mem_w_kv, m_mem_q_norm, m_mem_k_norm, m_ffn_norm, m_ffn_w_up, m_ffn_conv_w, m_ffn_conv_b, m_ffn_w_down, v_attn_norm, v_a_w_in, v_a_mu, v_a_w0, v_a_w2, v_a_a0, v_a_a2, v_a_g2, v_a_k_k, v_a_k_a, v_a_r_k, v_a_lnx_w, v_a_lnx_b, v_a_w_out, v_kv_norm, v_kv_w, v_kv_k_norm, v_b_w_in, v_b_q_norm, v_b_w_out, v_mem_norm, v_mem_w_kv, v_mem_q_norm, v_mem_k_norm, v_ffn_norm, v_ffn_w_up, v_ffn_conv_w, v_ffn_conv_b, v_ffn_w_down):
    names = [n for n, _ in PARAMS]
    vals = (attn_norm, a_w_in, a_mu, a_w0, a_w2, a_a0, a_a2, a_g2, a_k_k, a_k_a, a_r_k, a_lnx_w, a_lnx_b, a_w_out, kv_norm, kv_w, kv_k_norm, b_w_in, b_q_norm, b_w_out, mem_norm, mem_w_kv, mem_q_norm, mem_k_norm, ffn_norm, ffn_w_up, ffn_conv_w, ffn_conv_b, ffn_w_down)
    m_vals = (m_attn_norm, m_a_w_in, m_a_mu, m_a_w0, m_a_w2, m_a_a0, m_a_a2, m_a_g2, m_a_k_k, m_a_k_a, m_a_r_k, m_a_lnx_w, m_a_lnx_b, m_a_w_out, m_kv_norm, m_kv_w, m_kv_k_norm, m_b_w_in, m_b_q_norm, m_b_w_out, m_mem_norm, m_mem_w_kv, m_mem_q_norm, m_mem_k_norm, m_ffn_norm, m_ffn_w_up, m_ffn_conv_w, m_ffn_conv_b, m_ffn_w_down)
    v_vals = (v_attn_norm, v_a_w_in, v_a_mu, v_a_w0, v_a_w2, v_a_a0, v_a_a2, v_a_g2, v_a_k_k, v_a_k_a, v_a_r_k, v_a_lnx_w, v_a_lnx_b, v_a_w_out, v_kv_norm, v_kv_w, v_kv_k_norm, v_b_w_in, v_b_q_norm, v_b_w_out, v_mem_norm, v_mem_w_kv, v_mem_q_norm, v_mem_k_norm, v_ffn_norm, v_ffn_w_up, v_ffn_conv_w, v_ffn_conv_b, v_ffn_w_down)
    W, M, V = dict(zip(names, vals)), dict(zip(names, m_vals)), dict(zip(names, v_vals))
    layers = lambda D, n: [D[n]] if D[n].ndim == 2 else [D[n][i] for i in range(D[n].shape[0])]
    ax2 = lambda n: AXIS[n] - (W[n].ndim - 2)
    later =[(n, i) for n in BIG if n != "a_w_in" for i in range(len(layers(W, n)))]

    small_shapes = [W[n].shape for n in SMALL_SHARDED]
    got_w, got_small = exchange(Exchange(gathers=[W["a_w_in"][0].astype(BF16), _pack([W[n] for n in SMALL_SHARDED])]),
                                "gather_first")
    P = {n: W[n] for n in SMALL}
    P["a_w_in"] = [_from_shards(got_w, ax2("a_w_in"))]
    for n, s in zip(SMALL_SHARDED, _unpack(got_small, small_shapes, lead=1)):
        P[n] = _from_shards(s, AXIS[n])
    sent = lambda n, w: w.T if n in TRANSPOSED else w
    whole = lambda n, g: g.reshape(-1, g.shape[-1]) if n in TRANSPOSED else _from_shards(g, ax2(n))
    ex_weights = Exchange(gathers=[sent(n, layers(W, n)[i]).astype(BF16) for n, i in later])

    def weights_done(got):
        out = {}
        for (n, _), g in zip(later, got):
            out.setdefault(n, []).append(whole(n, g))
        return out

    slots = lambda G, n: jnp.stack([_to_shards(g, 0 if n in TRANSPOSED else ax2(n)) for g in G[n]], axis=1)
    later_names = [n for n in BIG if n != "a_w_in"]
    ex_grads = lambda G: Exchange(scatters=[slots(G, n) for n in later_names])
    loss_part, dx0, G = _local_step(x[0], mem[0], loss_target[0], P, ex_weights, weights_done, ex_grads)
    gparts = dict(zip(later_names, G.pop("_exchanged")))
    replicated = [n for n in SMALL if AXIS[n] is None]
    small_slots = _pack([_to_shards(G[n], AXIS[n]) for n in SMALL_SHARDED], lead=1)
    ex_last = Exchange(gathers=[_pack([G[n] for n in replicated] + [loss_part[0:1, 0:1]])],
                       scatters=[slots(G, "a_w_in"), small_slots])
    host = "ffn_w_up"

    results = {}
    for n in (host,) + tuple(b for b in BIG if b != host):
        rows = lambda z: z.reshape((-1,) + z.shape[-1:])
        gp = gparts[n].reshape((N_DEV, -1) + gparts[n].shape[-1:])
        if n in TRANSPOSED:
            gp = jnp.swapaxes(sum_parts(gp, f"sum_{n}").reshape(gparts[n].shape[1:]), -1, -2).reshape((1,) + rows(W[n]).shape)
        if n == host:
            res, (got_rep, gparts["a_w_in"], got_sharded) = adamw(gp, rows(W[n]), rows(M[n]), rows(V[n]), f"adamw_{n}", ex=ex_last)
        else:
            res = adamw(gp, rows(W[n]), rows(M[n]), rows(V[n]), f"adamw_{n}")
        results[n] = [r.reshape(W[n].shape) for r in res]
    *rep_sums, loss = _unpack(sum_parts(got_rep, "sum_replicated_grads"), [W[n].shape for n in replicated] + [()])
    g_mine = dict(zip(replicated, rep_sums))
    g_mine.update(zip(SMALL_SHARDED, _unpack(sum_parts(got_sharded, "sum_small_sharded_grads"), [W[n].shape for n in SMALL_SHARDED])))
    res = adamw(_pack([g_mine[n] for n in SMALL])[None], _pack([W[n] for n in SMALL]), _pack([M[n] for n in SMALL]),
                _pack([V[n] for n in SMALL]), "adamw_small")
    for n, parts in zip(SMALL, zip(*[_unpack(r, [W[n].shape for n in SMALL]) for r in res])):
        results[n] = list(parts)
    outs = [[results[n][j] for n in names] for j in range(4)]
    return (loss, dx0[None], *outs[0], *outs[1], *outs[2], *outs[3])
```

```python
import functools
import math

import jax
import jax.numpy as jnp
import numpy as np
from jax import lax
from jax.experimental import pallas as pl
from jax.experimental.pallas import tpu as pltpu

F32 = jnp.float32
BF16 = jnp.bfloat16
H3 =lax.Precision.HIGH

N_DEV = 8
D_MODEL = 1024
HEAD_DIM = 64
N_MEM = 256
MEM_HEADS = 4
MEM_WIDTH = 256
RWKV_HEADS = 12
RWKV_WIDTH = 768
SHIFT_WIDTH = 2560
DIL_GROUPS = ((128, 1), (512, 4), (2048, 16))
DIL_BLOCK = 128
DIL_WIDTH = 768
D_FF = 2816
RMS_EPS = 1e-6
LNX_EPS = 64e-5
NEG_INF = -1e30
ROPE_THETA = 10000.0
ADAM_LR, ADAM_B1, ADAM_B2, ADAM_EPS, ADAM_WD, ADAM_STEP = 0.001, 0.9, 0.999, 1e-08, 0.01, 10

CHUNK = 64
SCAN_GROUPS_FWD, SCAN_GROUPS_BWD = 1, 1
MM_TILE_CAP = 1408
VMEM_LIMIT_V7X = 48 * 1024 * 1024


def _cparams(sem):
    return pltpu.CompilerParams(dimension_semantics=sem, vmem_limit_bytes=VMEM_LIMIT_V7X)


def _pick(n, cands):
    for c in cands:
        if n % c == 0:
            return c
    return n


def _tile(n, cap):
    if n <= cap:
        return n
    for d in range(cap - cap % 128, 0, -128):
        if n % d == 0:
            return d
    return n


def _dg(a, b, ca, cb, batch):
    dims = (((ca,), (cb,)), ((0,), (0,))) if batch else (((ca,), (cb,)), ((), ()))
    return lax.dot_general(a.astype(BF16), b.astype(BF16), dims, preferred_element_type=F32)


@jax.custom_vjp
def mm_nn(a, b):
    n = a.ndim
    return _dg(a, b, n - 1, n - 2, n == 3)


def _mm_nn_fwd(a, b):
    return mm_nn(a, b), (a, b)


def _mm_nn_bwd(res, g):
    a, b = res
    n = a.ndim
    return _dg(g, b, n - 1, n - 1, n == 3), _dg(a, g, n - 2, n - 2, n == 3)


mm_nn.defvjp(_mm_nn_fwd, _mm_nn_bwd)


@jax.custom_vjp
def mm_nt(a, b):
    n = a.ndim
    return _dg(a, b, n - 1, n - 1, n == 3)


def _mm_nt_fwd(a, b):
    return mm_nt(a, b), (a, b)


def _mm_nt_bwd(res, g):
    a, b = res
    n = a.ndim
    return _dg(g, b, n - 1, n - 2, n == 3), _dg(g, a, n - 2, n - 2, n == 3)


mm_nt.defvjp(_mm_nt_fwd, _mm_nt_bwd)


def mmh(a, b):
    n = a.ndim
    dims = (((n - 1,), (n - 2,)), ((0,), (0,))) if n == 3 else (((1,), (0,)), ((), ()))
    return lax.dot_general(a, b, dims, precision=H3, preferred_element_type=F32)


def mmh_nt(a, b):
    n = a.ndim
    dims = (((n - 1,), (n - 1,)), ((0,), (0,))) if n == 3 else (((1,), (1,)), ((), ()))
    return lax.dot_general(a, b, dims, precision=H3, preferred_element_type=F32)


def mmh_tn(a, b):
    n = a.ndim
    dims = (((n - 2,), (n - 2,)), ((0,), (0,))) if n == 3 else (((0,), (0,)), ((), ()))
    return lax.dot_general(a, b, dims, precision=H3, preferred_element_type=F32)


def matmul(a, b, mode, name, residual=None):
    out_dtype = BF16 if mode == "tn" else F32
    if mode == "nn":
        (M, K), (_, N) = a.shape, b.shape
    elif mode == "nt":
        (M, K), (N, _) = a.shape, b.shape
    else:
        (K, M), (_, N) = a.shape, b.shape
    tm = _tile(M, 2048 if mode == "nn" else MM_TILE_CAP)
    tn = _tile(N, 512 if mode == "nn" else MM_TILE_CAP)
    tk = _tile(K, MM_TILE_CAP)
    nk = K // tk
    if mode == "nn":
        a_spec = pl.BlockSpec((tm, tk), lambda i, j, k: (i, k))
        b_spec = pl.BlockSpec((tk, tn), lambda i, j, k: (k, j))
        dims = (((1,), (0,)), ((), ()))
    elif mode == "nt":
        a_spec = pl.BlockSpec((tm, tk), lambda i, j, k: (i, k))
        b_spec = pl.BlockSpec((tn, tk), lambda i, j, k: (j, k))
        dims = (((1,), (1,)), ((), ()))
    else:
        a_spec = pl.BlockSpec((tk, tm), lambda i, j, k: (k, i))
        b_spec = pl.BlockSpec((tk, tn), lambda i, j, k: (k, j))
        dims = (((0,), (0,)), ((), ()))
    o_spec = pl.BlockSpec((tm, tn), lambda i, j, k: (i, j))
    has_res = residual is not None

    def body(*refs):
        if has_res:
            a_ref, b_ref, r_ref, o_ref, acc_ref = refs
        else:
            a_ref, b_ref, o_ref, acc_ref = refs
        k = pl.program_id(2)

        @pl.when(k == 0)
        def _():
            acc_ref[...] = jnp.zeros_like(acc_ref)

        acc_ref[...] += lax.dot_general(a_ref[...].astype(BF16), b_ref[...].astype(BF16), dims,
                                        preferred_element_type=F32)

        @pl.when(k == nk - 1)
        def _():
            if has_res:
                o_ref[...] = (acc_ref[...] + r_ref[...]).astype(out_dtype)
            else:
                o_ref[...] = acc_ref[...].astype(out_dtype)

    ins = [a, b] + ([residual] if has_res else [])
    in_specs = [a_spec, b_spec] + ([o_spec] if has_res else [])
    return pl.pallas_call(
        body, grid=(M // tm, N // tn, nk), in_specs=in_specs, out_specs=o_spec,
        out_shape=jax.ShapeDtypeStruct((M, N), out_dtype), scratch_shapes=[pltpu.VMEM((tm, tn), F32)],
        compiler_params=_cparams(("parallel", "parallel", "arbitrary")), name=name)(*ins)


class Cols:
    def __init__(self, arr, width, idx):
        self.arr, self.width, self.idx = arr, width, idx


def _arr(x):
    return x.arr if isinstance(x, Cols) else x


def _shape(x):
    return x.arr.shape[:-1] + (x.width,) if isinstance(x, Cols) else x.shape


def _col(x):
    return x.idx if isinstance(x, Cols) else 0


def _tok_spec(x, tile):
    shape, col = _shape(x), _col(x)
    return pl.BlockSpec(shape[:-2] + (tile, shape[-1]), lambda i: (0,) * (len(shape) - 2) + (i, col))


def _full_spec(x):
    shape, col = _shape(x), _col(x)
    return pl.BlockSpec(shape, lambda i: (0,) * (len(shape) - 1) + (col,))


def _halo_spec(x, tile):
    shape, col = _shape(x), _col(x)
    return pl.BlockSpec((8, shape[-1]), lambda i: (jnp.maximum(i * (tile // 8) - 1, 0), col))


def _blk(x, tile):
    shape = _shape(x)
    return jax.ShapeDtypeStruct(shape[:-2] + (tile, shape[-1]), _arr(x).dtype)


def _prev_rows(x, halo):
    rows = lax.broadcasted_iota(jnp.int32, (x.shape[0], 1), 0)
    before = jnp.where(pl.program_id(0) > 0, halo[7:8], 0.0)
    return jnp.where(rows == 0, before, pltpu.roll(x, 1, 0))


class Into:
    def __init__(self, buf, total, idx):
        self.buf, self.total, self.idx = buf, total, idx

    def place(self, shape, tile):
        idx = self.idx
        return shape.update(shape=(shape.shape[0], self.total)), pl.BlockSpec((tile, shape.shape[1]), lambda i: (i, idx))

    def operand(self, n_in, n_out_index):
        if self.buf is None:
            return [], [], {}
        return [self.buf], [_ANY], {n_in: n_out_index}


def stage_fwd(f, xs, ps, cts, cfs, tile, name, out_dtypes=None, with_prev=False, into=None):
    xs, ps, cts, cfs = list(xs), list(ps), list(cts), list(cfs)
    halos = xs if with_prev else []
    nx, nh, nct, np_ = len(xs), len(halos), len(cts), len(ps)
    T = _shape(xs[0])[-2]
    blk = [_blk(x, tile) for x in xs]
    out_avals = jax.eval_shape(f, *blk, *(blk if with_prev else []), *[_blk(p, _shape(p)[-2]) for p in ps],
                               *[_blk(c, tile) for c in cts], *[_blk(c, _shape(c)[-2]) for c in cfs])
    if out_dtypes is None:
        out_dtypes = [o.dtype for o in out_avals]
    out_shape = [jax.ShapeDtypeStruct(o.shape[:-2] + (T, o.shape[-1]), dt) for o, dt in zip(out_avals, out_dtypes)]
    out_specs = [_tok_spec(o, tile) for o in out_shape]
    n_in = nx + nh + nct + np_ + len(cfs)
    extra, extra_specs, alias = [], [], {}
    if into is not None:
        out_shape[0], out_specs[0] = into.place(out_shape[0], tile)
        extra, extra_specs, alias = into.operand(n_in, 0)

    def body(*refs):
        vals = [r[...] for r in refs[:n_in]]
        xv, hv, rest = vals[:nx], vals[nx:nx + nh], vals[nx + nh:]
        ctv, pv, cfv = rest[:nct], rest[nct:nct + np_], rest[nct + np_:]
        prev = [_prev_rows(x, h) for x, h in zip(xv, hv)]
        res = f(*xv, *prev, *pv, *ctv, *cfv)
        for o_ref, r in zip(refs[n_in + len(extra):], res):
            o_ref[...] = r.astype(o_ref.dtype)

    return pl.pallas_call(
        body, grid=(T // tile,),
        in_specs=([_tok_spec(x, tile) for x in xs] + [_halo_spec(x, tile) for x in halos] + [_tok_spec(c, tile) for c in cts]
                  + [_full_spec(p) for p in ps + cfs] + extra_specs),
        out_specs=out_specs, out_shape=out_shape, input_output_aliases=alias,
        compiler_params=_cparams(("parallel",)), name=name)(*[_arr(a) for a in xs + halos + cts + ps + cfs], *extra)


def stage_bwd(f, xs, ps, cts, cfs, gs, tile, name, bf16_copies=(), with_prev=False, copy_into=None):
    xs, ps, cts, cfs = list(xs), list(ps), list(cts), list(cfs)
    gs = [list(g) if isinstance(g, (list, tuple)) else [g] for g in gs]
    g_flat = [a for g in gs for a in g]
    halos = xs if with_prev else []
    nx, nh, nct, ng, np_ = len(xs), len(halos), len(cts), len(g_flat), len(ps)
    T = _shape(xs[0])[-2]
    dx_like = xs + halos
    out_shape = ([jax.ShapeDtypeStruct(_shape(x), F32) for x in dx_like] + [jax.ShapeDtypeStruct(_shape(p), F32) for p in ps]
                 + [jax.ShapeDtypeStruct(_shape(xs[i]), BF16) for i in bf16_copies])
    n_in = nx + nh + nct + ng + np_ + len(cfs)
    ndx = nx + nh
    plain = lambda x: jax.ShapeDtypeStruct(_shape(x), F32)
    out_specs = ([_tok_spec(plain(x), tile) for x in dx_like] + [_full_spec(plain(p)) for p in ps]
                 + [_tok_spec(plain(xs[i]), tile) for i in bf16_copies])
    extra, extra_specs, alias = [], [], {}
    if copy_into is not None:
        out_shape[ndx + np_], out_specs[ndx + np_] = copy_into.place(out_shape[ndx + np_], tile)
        extra, extra_specs, alias = copy_into.operand(n_in, ndx + np_)

    def body(*refs):
        vals = [r[...] for r in refs[:n_in]]
        outs = refs[n_in + len(extra):]
        xv, hv, rest = vals[:nx], vals[nx:nx + nh], vals[nx + nh:]
        ctv, gparts, pv, cfv = rest[:nct], rest[nct:nct + ng], rest[nct + ng:nct + ng + np_], rest[nct + ng + np_:]
        gv = []
        for g in gs:
            gv.append(functools.reduce(lambda a, b: a + b, gparts[:len(g)]))
            gparts = gparts[len(g):]
        prev = [_prev_rows(x, h) for x, h in zip(xv, hv)]
        _, vjp = jax.vjp(lambda *xp: f(*xp, *ctv, *cfv), *xv, *prev, *pv)
        d = vjp(tuple(gv))
        for o_ref, r in zip(outs[:ndx], d[:ndx]):
            o_ref[...] = r
        for o_ref, i in zip(outs[ndx + np_:], bf16_copies):
            o_ref[...] = d[i].astype(BF16)

        @pl.when(pl.program_id(0) == 0)
        def _():
            for o_ref in outs[ndx:ndx + np_]:
                o_ref[...] = jnp.zeros_like(o_ref)

        for o_ref, r in zip(outs[ndx:ndx + np_], d[ndx:]):
            o_ref[...] += r

    res = pl.pallas_call(
        body, grid=(T // tile,),
        in_specs=([_tok_spec(x, tile) for x in xs] + [_halo_spec(x, tile) for x in halos]
                  + [_tok_spec(c, tile) for c in cts + g_flat] + [_full_spec(p) for p in ps + cfs] + extra_specs),
        out_specs=out_specs, out_shape=out_shape, input_output_aliases=alias,
        compiler_params=_cparams(("arbitrary",)), name=name)(*[_arr(a) for a in xs + halos + cts + g_flat + ps + cfs], *extra)
    if bf16_copies:
        return list(res[:ndx]), list(res[ndx:ndx + np_]), list(res[ndx + np_:])
    return list(res[:ndx]), list(res[ndx:])


def _rms(x, g, eps=RMS_EPS):
    return x * lax.rsqrt(jnp.mean(x * x, axis=-1, keepdims=True) + eps) * g


def f_rmsnorm(x, g):
    return (_rms(x, g),)


def f_rmsnorm_res(x, g):
    return _rms(x, g), x


def f_rmsnorm2(x, g1, g2):
    n = x * lax.rsqrt(jnp.mean(x * x, axis=-1, keepdims=True) + RMS_EPS)
    return n * g1, n * g2


def f_rmsnorm2_res(x, g1, g2):
    return f_rmsnorm2(x, g1, g2) + (x,)


def _sigmoid(x):
    return 1.0 / (1.0 + jnp.exp(-x))


def _softplus(x):
    return jnp.maximum(x, 0.0) + jnp.log(1.0 + jnp.exp(-jnp.abs(x)))


def f_rwkv_pre(pr, pk, pv, pl_, qr, qk, qv, ql, mu_r, mu_k, mu_v, mu_l, w0, w2, a0, a2, g2, k_k, k_a, seg, seg_t):
    xr = pr + (qr - pr) * mu_r
    xk = pk + (qk - pk) * mu_k
    xv = pv + (qv - pv) * mu_v
    xl = pl_ + (ql - pl_) * mu_l
    w_log = -_softplus(-(w0 + mm_nn(jnp.tanh(xl), w2))) - 0.5
    lw = -jnp.exp(w_log)
    a = _sigmoid(a0 + mm_nn(xl, a2))
    g = mm_nn(_sigmoid(xl), g2)
    kkr = xk * k_k
    inv = lax.rsqrt(jnp.maximum(mmh(kkr * kkr, seg), 1e-24))
    kk = kkr * mmh(inv, seg_t)
    k2 = xk * (1.0 + (a - 1.0) * k_a)
    return xr, lw, k2, xv, kk, kk * a, g


def f_rwkv_post(y, r, k2, v, g, lnx_w, lnx_b, r_k, seg, seg_t):
    inv_n = 1.0 / HEAD_DIM
    m = mmh(mmh(y, seg) * inv_n, seg_t)
    yc = y - m
    rstd = lax.rsqrt(mmh(yc * yc, seg) * inv_n + LNX_EPS)
    yn = yc * mmh(rstd, seg_t) * lnx_w + lnx_b
    bonus = mmh(mmh(r * k2 * r_k, seg), seg_t) * v
    return ((yn + bonus) * g,)


def _headnorm(z, g, seg, seg_t):
    ms = mmh(z * z, seg) * (1.0 / HEAD_DIM)
    return z * mmh(lax.rsqrt(ms + RMS_EPS), seg_t) * g


def f_headnorm(z, g, seg, seg_t):
    return (_headnorm(z, g, seg, seg_t),)


def _rot_half(z):
    w = z.shape[1]
    half = HEAD_DIM // 2
    lane = lax.broadcasted_iota(jnp.int32, (1, w), 1)
    return jnp.where((lane & (HEAD_DIM - 1)) < half, -pltpu.roll(z, w - half, 1), pltpu.roll(z, half, 1))


@jax.custom_vjp
def _rotate_half(z):
    return _rot_half(z)


_rotate_half.defvjp(lambda z: (_rot_half(z), None), lambda _, g: (-_rot_half(g),))


def f_qkprep(z, g, cos, sin, seg, seg_t):
    zn = _headnorm(z, g, seg, seg_t)
    pairs = z.shape[1] // cos.shape[1]
    return (zn * jnp.tile(cos, (1, pairs)) + _rotate_half(zn) * jnp.tile(sin, (1, pairs)),)


def _head_mask(width, h):
    lane = lax.broadcasted_iota(jnp.int32, (1, width), 1)
    return jnp.where((lane >> 6) == h, jnp.ones((), F32), 0.0)


def f_memattn(q, k, v, q_norm, seg, seg_t):
    qn = _headnorm(q, q_norm, seg, seg_t)
    out = jnp.zeros_like(q)
    for h in range(MEM_HEADS):
        m = _head_mask(MEM_WIDTH, h)
        s = mm_nt(qn * m, k) * (1.0 / math.sqrt(HEAD_DIM))
        s = s - jnp.max(s, axis=-1, keepdims=True)
        p = jnp.exp(s)
        p = p / jnp.sum(p, axis=-1, keepdims=True)
        out = out + mm_nn(p, v) * m
    return (out,)


def f_mix(o1, o2, o3, l1, l2, l3):
    mx = jnp.maximum(jnp.maximum(l1, l2), l3)
    e1, e2, e3 = jnp.exp(l1 - mx), jnp.exp(l2 - mx), jnp.exp(l3 - mx)
    return ((e1 * o1 + e2 * o2 + e3 * o3) / (e1 + e2 + e3),)


def _chunk_masks(L):
    t = lax.broadcasted_iota(jnp.int32, (L, L), 0)
    s = lax.broadcasted_iota(jnp.int32, (L, L), 1)
    return t, s


def _unit_lower_inverse(a):
    L = a.shape[-1]
    t, s = _chunk_masks(L)
    one = jnp.ones((), F32)
    blk = lambda sh: jnp.where((t >> sh) == (s >> sh), one, 0.0)
    n0 = a * blk(3)
    x = jnp.where(t == s, one, 0.0) - n0
    n2 = mmh(n0, n0)
    x = x + mmh(x, n2)
    x = x + mmh(x, mmh(n2, n2))
    for sh in (3, 4, 5):
        if (1 << sh) >= L:
            break
        off = a * (blk(sh + 1) - blk(sh))
        x = x - mmh(x, mmh(off, x))
    return x


@jax.custom_vjp
def _inverse_known(a, x):
    return x


def _inverse_known_fwd(a, x):
    return x, x


def _inverse_known_bwd(x, dx):
    return -mmh_nt(mmh_tn(x, dx), x), jnp.zeros_like(x)


_inverse_known.defvjp(_inverse_known_fwd, _inverse_known_bwd)


def _running_sum(x, reverse):
    L = x.shape[1]
    pos = lax.broadcasted_iota(jnp.int32, (1, L, 1), 1)
    step = 1
    while step < L:
        if reverse:
            x = x + jnp.where(pos < L - step, pltpu.roll(x, L - step, 1), 0.0)
        else:
            x = x + jnp.where(pos >= step, pltpu.roll(x, step, 1), 0.0)
        step *= 2
    return x


@jax.custom_vjp
def _cumsum_tokens(x):
    return _running_sum(x, False)


_cumsum_tokens.defvjp(lambda x: (_running_sum(x, False), None), lambda _, g: (_running_sum(g, True),))


def f_rwkv_chunk(s0, r, lw, k, v, kk, b, x_known=None):
    H, L, _ = r.shape
    t, s = _chunk_masks(L)
    one = jnp.ones((), F32)
    incl = jnp.where(t >= s, one, 0.0)
    strict = jnp.where(t > s, one, 0.0)
    cum = _cumsum_tokens(lw)
    w_in = jnp.exp(cum)
    w_ex = jnp.exp(cum - lw)
    w_inv = jnp.exp(-cum)
    rt, kkt, kt, bt = r * w_in, kk * w_ex, k * w_inv, b * w_inv
    a_b = mmh_nt(kkt, bt) * strict
    a_k = mmh_nt(kkt, kt) * strict
    m_k = mmh_nt(rt, kt) * incl
    m_b = mmh_nt(rt, bt) * incl
    x = _unit_lower_inverse(a_b) if x_known is None else _inverse_known(a_b, x_known)
    u = mmh(x, mmh_nt(kkt, s0) + mmh(a_k, v))
    y = mmh_nt(rt, s0) + mmh(m_k, v) - mmh(m_b, u)
    w_last = jnp.exp(jnp.sum(lw, axis=1, keepdims=True))
    s1 = (s0 + mmh_tn(v, kt) - mmh_tn(u, bt)) * w_last
    return y, s1, x


def _ex_split(ex, refs, n_in, n_out):
    n = ex.n
    ins, ex_in = refs[:n_in], refs[n_in:n_in + n]
    outs, ex_out = refs[n_in + n:n_in + n + n_out], refs[n_in + n + n_out:n_in + 2 * n + n_out]
    rest = refs[n_in + 2 * n + n_out:]
    return ins, outs, rest[:len(rest) - 3], (ex_in, ex_out) + tuple(rest[len(rest) - 3:])


def _split_heads(x):
    return jnp.stack([x[:, h * HEAD_DIM:(h + 1) * HEAD_DIM] for h in range(x.shape[1] // HEAD_DIM)], axis=0)


def _merge_heads(x):
    return jnp.concatenate([x[h] for h in range(x.shape[0])], axis=1)


def rwkv_scan_fwd(r, lw, k, v, kk, b, ex):
    T, N = r.shape[0], HEAD_DIM
    H = r.shape[1] // N
    groups = SCAN_GROUPS_FWD
    nc, hg = T // CHUNK, H // groups
    seq = pl.BlockSpec((CHUNK, hg * N), lambda g, c: (c, g))

    def body(*refs):
        (r_ref, lw_ref, k_ref, v_ref, kk_ref, b_ref), (y_ref, hs_ref, xs_ref), (h_scr,), ex_refs = _ex_split(ex, refs, 6, 3)
        g, c = pl.program_id(0), pl.program_id(1)

        @pl.when(jnp.logical_and(g == 0, c == 0))
        def _():
            ex.start(*ex_refs)

        @pl.when(c == 0)
        def _():
            h_scr[...] = jnp.zeros_like(h_scr)

        h0 = h_scr[...]
        hs_ref[0] = h0
        y, h1, x = f_rwkv_chunk(h0, *[_split_heads(z[...]) for z in (r_ref, lw_ref, k_ref, v_ref, kk_ref, b_ref)])
        y_ref[...] = _merge_heads(y)
        xs_ref[0] = x
        h_scr[...] = h1

        @pl.when(jnp.logical_and(g == groups - 1, c == (3 * nc) // 4))
        def _():
            ex.forward(*ex_refs)

        @pl.when(jnp.logical_and(g == groups - 1, c == nc - 1))
        def _():
            ex.wait(*ex_refs)

    res = pl.pallas_call(
        body, grid=(groups, nc), in_specs=[seq] * 6 + [_ANY] * ex.n,
        out_specs=[seq, pl.BlockSpec((1, hg, N, N), lambda g, c: (c, g, 0, 0)),
                   pl.BlockSpec((1, hg, CHUNK, CHUNK), lambda g, c: (c, g, 0, 0))] + [_ANY] * ex.n,
        out_shape=[jax.ShapeDtypeStruct((T, H * N), F32), jax.ShapeDtypeStruct((nc, H, N, N), F32),
                   jax.ShapeDtypeStruct((nc, H, CHUNK, CHUNK), F32)] + ex.out_shape(),
        scratch_shapes=[pltpu.VMEM((hg, N, N), F32)] + ex.scratch(),
        compiler_params=_cparams(("arbitrary", "arbitrary")), name="rwkv_scan_fwd")(r, lw, k, v, kk, b, *ex.operands())
    return res[0], (res[1], res[2]), list(res[3:])


def rwkv_scan_bwd(r, lw, k, v, kk, b, saved, dy, ex):
    T, N = r.shape[0], HEAD_DIM
    H = r.shape[1] // N
    groups = SCAN_GROUPS_BWD
    nc, hg = T // CHUNK, H // groups
    seq = pl.BlockSpec((CHUNK, hg * N), lambda g, c: (nc - 1 - c, g))
    state = pl.BlockSpec((1, hg, N, N), lambda g, c: (nc - 1 - c, g, 0, 0))

    def body(*refs):
        (r_ref, lw_ref, k_ref, v_ref, kk_ref, b_ref, hs_ref, xs_ref, dy_ref), outs, (dh_scr,), ex_refs = _ex_split(ex, refs, 9, 6)
        g, c = pl.program_id(0), pl.program_id(1)

        @pl.when(jnp.logical_and(g == 0, c == 0))
        def _():
            ex.start(*ex_refs)

        @pl.when(c == 0)
        def _():
            dh_scr[...] = jnp.zeros_like(dh_scr)

        x_known = xs_ref[0]
        _, vjp = jax.vjp(lambda *a: f_rwkv_chunk(*a, x_known=x_known)[:2], hs_ref[0],
                         *[_split_heads(z[...]) for z in (r_ref, lw_ref, k_ref, v_ref, kk_ref, b_ref)])
        d = vjp((_split_heads(dy_ref[...]), dh_scr[...]))
        dh_scr[...] = d[0]
        for o_ref, dz in zip(outs, d[1:]):
            o_ref[...] = _merge_heads(dz)

        @pl.when(jnp.logical_and(g == groups - 1, c == nc - 1))
        def _():
            ex.forward(*ex_refs)
            ex.wait(*ex_refs)

    res = pl.pallas_call(
        body, grid=(groups, nc),
        in_specs=[seq] * 6 + [state, state, seq] + [_ANY] * ex.n,
        out_specs=[seq] * 6 + [_ANY] * ex.n, out_shape=[jax.ShapeDtypeStruct((T, H * N), F32)] * 6 + ex.out_shape(),
        scratch_shapes=[pltpu.VMEM((hg, N, N), F32)] + ex.scratch(),
        compiler_params=_cparams(("arbitrary", "arbitrary")), name="rwkv_scan_bwd")(r, lw, k, v, kk, b, *saved, dy, *ex.operands())
    return list(res[:6]), list(res[6:])


GROUP_COLS = 4 * HEAD_DIM


def _f_dilattn(has_prev, q, kc, kp, vc, vp):
    scale = 1.0 / math.sqrt(HEAD_DIM)
    i = lax.broadcasted_iota(jnp.int32, (DIL_BLOCK, DIL_BLOCK), 0)
    j = lax.broadcasted_iota(jnp.int32, (DIL_BLOCK, DIL_BLOCK), 1)
    o, l = jnp.zeros_like(q), jnp.zeros_like(q)
    for h in range(q.shape[1] // HEAD_DIM):
        m = _head_mask(q.shape[1], h)
        sc = jnp.where(j <= i, mm_nt(q * m, kc) * scale, NEG_INF)
        sp = jnp.where(jnp.logical_and(i <= j, has_prev), mm_nt(q * m, kp) * scale, NEG_INF)
        mx = jnp.maximum(jnp.max(sc, axis=-1, keepdims=True), jnp.max(sp, axis=-1, keepdims=True))
        pc, pp = jnp.exp(sc - mx), jnp.exp(sp - mx)
        den = jnp.sum(pc, axis=-1, keepdims=True) + jnp.sum(pp, axis=-1, keepdims=True)
        o = o + (mm_nn(pc, vc) + mm_nn(pp, vp)) / den * m
        l = l + (mx + jnp.log(den)) * m
    return o, l


def _dil_specs(gi, d):
    parts = 1 if d == 1 else 2
    blk = (DIL_BLOCK * d, GROUP_COLS // parts)
    at = lambda col: (lambda p, n: (n, col * parts + p))
    before = lambda col: (lambda p, n: (jnp.maximum(n - 1, 0), col * parts + p))
    v0 = DIL_WIDTH // GROUP_COLS + gi
    q = pl.BlockSpec(blk, at(gi))
    kc, kp = pl.BlockSpec(blk, at(gi)), pl.BlockSpec(blk, before(gi))
    vc, vp = pl.BlockSpec(blk, at(v0)), pl.BlockSpec(blk, before(v0))
    out = pl.BlockSpec(blk, at(0))
    together = min(d, 2)
    return (q, kc, kp, vc, vp, out), parts, together


def _residue_rows(r, d):
    return pl.ds(r, DIL_BLOCK, stride=d) if d > 1 else pl.ds(0, DIL_BLOCK)


def dil_fwd(q, k, kv, gi, d, name):
    T = q.shape[0]
    (qs, kc, kp, vc, vp, out), parts, together = _dil_specs(gi, d)

    def body(q_ref, kc_ref, kp_ref, vc_ref, vp_ref, o_ref, l_ref):
        has_prev = pl.program_id(1) > 0

        def residues(it, carry):
            rows = [_residue_rows(it * together + a, d) for a in range(together)]
            ins = [[ref[rw, :] for ref in (q_ref, kc_ref, kp_ref, vc_ref, vp_ref)] for rw in rows]
            res = [_f_dilattn(has_prev, *x) for x in ins]
            for rw, (o, l) in zip(rows, res):
                o_ref[rw, :] = o
                l_ref[rw, :] = l
            return carry

        lax.fori_loop(0, d // together, residues, 0)

    shape = jax.ShapeDtypeStruct((T, 4 * HEAD_DIM), F32)
    return pl.pallas_call(
        body, grid=(parts, T // (DIL_BLOCK * d)), in_specs=[qs, kc, kp, vc, vp], out_specs=[out, out], out_shape=[shape, shape],
        compiler_params=_cparams(("parallel", "parallel")), name=name)(q, k, k, kv, kv)


def dil_bwd(q, k, kv, do, dl, gi, d, name):
    T = q.shape[0]
    (qs, kc, kp, vc, vp, out), parts, together = _dil_specs(gi, d)

    def body(q_ref, kc_ref, kp_ref, vc_ref, vp_ref, do_ref, dl_ref, *outs):
        f = functools.partial(_f_dilattn, pl.program_id(1) > 0)

        def residues(it, carry):
            rows = [_residue_rows(it * together + a, d) for a in range(together)]
            ins = [[ref[rw, :] for ref in (q_ref, kc_ref, kp_ref, vc_ref, vp_ref, do_ref, dl_ref)] for rw in rows]
            res = [jax.vjp(f, *x[:5])[1]((x[5], x[6])) for x in ins]
            for rw, gs in zip(rows, res):
                for o_ref, g in zip(outs, gs):
                    o_ref[rw, :] = g
            return carry

        lax.fori_loop(0, d // together, residues, 0)

    shape = jax.ShapeDtypeStruct((T, 4 * HEAD_DIM), F32)
    dq, dkc, dkp, dvc, dvp = pl.pallas_call(
        body, grid=(parts, T // (DIL_BLOCK * d)), in_specs=[qs, kc, kp, vc, vp, out, out], out_specs=[out] * 5, out_shape=[shape] * 5,
        compiler_params=_cparams(("parallel", "parallel")), name=name)(q, k, k, kv, kv, do, dl)

    def own_plus_next(c, p):
        return c + jnp.concatenate([p[DIL_BLOCK * d:], jnp.zeros_like(p[:DIL_BLOCK * d])], axis=0)

    return dq, own_plus_next(dkc, dkp), own_plus_next(dvc, dvp)


CONV_TILE = 256


def _conv3(before, u, w, b):
    ue = jnp.concatenate([before, u], axis=0)
    s1, s2 = pltpu.roll(ue, 1, 0)[8:], pltpu.roll(ue, 2, 0)[8:]
    return b + w[0:1] * s2 + w[1:2] * s1 + w[2:3] * u, s1, s2


def _conv_halves(u_ref, h_ref, cw_ref, cb_ref):
    F = D_FF
    res = []
    for lo in (0, F):
        before = jnp.where(pl.program_id(0) > 0, h_ref[:, lo:lo + F], 0.0)
        u = u_ref[:, lo:lo + F]
        res.append((u,) + _conv3(before, u, cw_ref[:, lo:lo + F], cb_ref[:, lo:lo + F]))
    return res


def _halo_before(C):
    return pl.BlockSpec((8, C), lambda i: (jnp.maximum(i * (CONV_TILE // 8) - 1, 0), 0))


def convgate_fwd(u, cw, cb, name):
    T, C = u.shape
    F = C // 2

    def body(u_ref, h_ref, cw_ref, cb_ref, z_ref):
        (_, cg, _, _), (_, cv, _, _) = _conv_halves(u_ref, h_ref, cw_ref, cb_ref)
        z_ref[...] = (cg * _sigmoid(cg) * cv).astype(BF16)

    return pl.pallas_call(
        body, grid=(T // CONV_TILE,),
        in_specs=[pl.BlockSpec((CONV_TILE, C), lambda i: (i, 0)), _halo_before(C), _full_spec(cw), _full_spec(cb)],
        out_specs=pl.BlockSpec((CONV_TILE, F), lambda i: (i, 0)), out_shape=jax.ShapeDtypeStruct((T, F), BF16),
        compiler_params=_cparams(("parallel",)), name=name)(u, u, cw, cb)


def convgate_bwd(u, cw, cb, dz, name):
    T, C = u.shape
    F = C // 2
    n = T // CONV_TILE
    E = CONV_TILE + 8

    def body(u_ref, hb_ref, ha_ref, cw_ref, cb_ref, dz_ref, dza_ref, du_ref, dcw_ref, dcb_ref):
        i = pl.program_id(0)
        dze = jnp.concatenate([dz_ref[...], jnp.where(i < n - 1, dza_ref[...], 0.0)], axis=0)

        @pl.when(i == 0)
        def _():
            dcw_ref[...] = jnp.zeros_like(dcw_ref)
            dcb_ref[...] = jnp.zeros_like(dcb_ref)

        halves = []
        for lo in (0, F):
            sl = slice(lo, lo + F)
            ue = jnp.concatenate([u_ref[:, sl], ha_ref[:, sl]], axis=0)
            c, s1, s2 = _conv3(jnp.where(i > 0, hb_ref[:, sl], 0.0), ue, cw_ref[:, sl], cb_ref[:, sl])
            halves.append((sl, ue, c, s1, s2))
        (_, _, cg, _, _), (_, _, cv, _, _) = halves
        sg = _sigmoid(cg)
        dcs = (dze * cv * sg * (1.0 + cg * (1.0 - sg)), dze * cg * sg)
        for (sl, ue, _, s1, s2), dc in zip(halves, dcs):
            own = lambda z: z[:CONV_TILE]
            dcb_ref[:, sl] += jnp.sum(own(dc), axis=0, keepdims=True)
            dcw_ref[0:1, sl] += jnp.sum(own(dc * s2), axis=0, keepdims=True)
            dcw_ref[1:2, sl] += jnp.sum(own(dc * s1), axis=0, keepdims=True)
            dcw_ref[2:3, sl] += jnp.sum(own(dc * ue), axis=0, keepdims=True)
            du = cw_ref[2:3, sl] * dc + cw_ref[1:2, sl] * pltpu.roll(dc, E - 1, 0) + cw_ref[0:1, sl] * pltpu.roll(dc, E - 2, 0)
            du_ref[:, sl] = own(du).astype(BF16)

    after = lambda w: pl.BlockSpec((8, w), lambda i: (jnp.minimum((i + 1) * (CONV_TILE // 8), T // 8 - 1), 0))
    return pl.pallas_call(
        body, grid=(n,),
        in_specs=[pl.BlockSpec((CONV_TILE, C), lambda i: (i, 0)), _halo_before(C), after(C), _full_spec(cw), _full_spec(cb),
                  pl.BlockSpec((CONV_TILE, F), lambda i: (i, 0)), after(F)],
        out_specs=[pl.BlockSpec((CONV_TILE, C), lambda i: (i, 0)), _full_spec(cw), _full_spec(cb)],
        out_shape=[jax.ShapeDtypeStruct((T, C), BF16), jax.ShapeDtypeStruct(cw.shape, F32), jax.ShapeDtypeStruct(cb.shape, F32)],
        compiler_params=_cparams(("arbitrary",)), name=name)(u, u, u, cw, cb, dz, dz)


def loss_head(y, tgt):
    T, D = y.shape
    tile = ROW_TILE

    def body(y_ref, t_ref, l_ref, d_ref, db_ref):
        d = y_ref[...] - t_ref[...]
        d_ref[...] = d * (1.0 / D)
        db_ref[...] = (d * (1.0 / D)).astype(BF16)

        @pl.when(pl.program_id(0) == 0)
        def _():
            l_ref[...] = jnp.zeros_like(l_ref)

        l_ref[...] += (0.5 / D) * jnp.sum(d * d)

    row = pl.BlockSpec((tile, D), lambda i: (i, 0))
    return pl.pallas_call(
        body, grid=(T // tile,), in_specs=[row, row], out_specs=[pl.BlockSpec((8, 128), lambda i: (0, 0)), row, row],
        out_shape=[jax.ShapeDtypeStruct((8, 128), F32), jax.ShapeDtypeStruct((T, D), F32), jax.ShapeDtypeStruct((T, D), BF16)],
        compiler_params=_cparams(("arbitrary",)), name="loss_head")(y, tgt)


def sum_parts(parts, name):
    S, R, C = parts.shape
    tile = _pick(R, (256, 128, 64, 32, 16, 8))

    def body(p_ref, o_ref):
        acc = p_ref[0].astype(F32)
        for s in range(1, S):
            acc = acc + p_ref[s].astype(F32)
        o_ref[...] = acc

    return pl.pallas_call(
        body, grid=(R // tile,), in_specs=[pl.BlockSpec((S, tile, C), lambda i: (0, i, 0))],
        out_specs=pl.BlockSpec((tile, C), lambda i: (i, 0)), out_shape=jax.ShapeDtypeStruct((R, C), F32),
        compiler_params=_cparams(("parallel",)), name=name)(parts)


def adamw(gparts, w, m, v, name):
    S, R, C = gparts.shape
    tile = _pick(R, (256, 128, 64, 32, 16, 8))
    c1 = 1.0 / (1.0 - ADAM_B1 ** ADAM_STEP)
    c2 = 1.0 / (1.0 - ADAM_B2 ** ADAM_STEP)

    def body(g_ref, w_ref, m_ref, v_ref, go_ref, d_ref, mo_ref, vo_ref):
        g = g_ref[0].astype(F32)
        for s in range(1, S):
            g = g + g_ref[s].astype(F32)
        m1 = ADAM_B1 * m_ref[...] + (1.0 - ADAM_B1) * g
        v1 = ADAM_B2 * v_ref[...] + (1.0 - ADAM_B2) * (g * g)
        go_ref[...] = g
        mo_ref[...] = m1
        vo_ref[...] = v1
        d_ref[...] = -ADAM_LR * ((m1 * c1) / (jnp.sqrt(v1 * c2) + ADAM_EPS) + ADAM_WD * w_ref[...])

    row = pl.BlockSpec((tile, C), lambda i: (i, 0))
    return pl.pallas_call(
        body, grid=(R // tile,), in_specs=[pl.BlockSpec((S, tile, C), lambda i: (0, i, 0)), row, row, row],
        out_specs=[row] * 4, out_shape=[jax.ShapeDtypeStruct((R, C), F32)] * 4,
        compiler_params=_cparams(("parallel",)), name=name)(gparts, w, m, v)


def _peers():
    x, y, c = lax.axis_index("x"), lax.axis_index("y"), lax.axis_index("c")
    peers = []
    for k in range(1, N_DEV):
        px = 1 - x if k & 4 else x
        py = 1 - y if k & 2 else y
        pc = 1 - c if k & 1 else c
        peers.append(((px, py, pc), 4 * px + 2 * py + pc))
    return 4 * x + 2 * y + c, peers


_ANY = pl.BlockSpec(memory_space=pl.ANY)


class Exchange:
    def __init__(self, gathers=(), scatters=()):
        self.gathers, self.scatters = list(gathers), list(scatters)
        self.n = len(self.gathers) + len(self.scatters)

    def operands(self):
        return self.gathers + self.scatters

    def out_shape(self):
        return ([jax.ShapeDtypeStruct((N_DEV,) + x.shape, x.dtype) for x in self.gathers]
                + [jax.ShapeDtypeStruct(x.shape, x.dtype) for x in self.scatters])

    def scratch(self):
        n = max(self.n, 1)
        return [pltpu.SemaphoreType.DMA((7 * n,)), pltpu.SemaphoreType.DMA((7 * n,)), pltpu.SemaphoreType.DMA((n,))]

    def _copies(self, in_refs, out_refs, send_sems, recv_sems, local_sems):
        me, peers = _peers()
        ng = len(self.gathers)
        local, sends, recvs = [], [], []
        for a in range(self.n):
            x, o = in_refs[a], out_refs[a]
            mine = x if a < ng else x.at[me]
            local.append(pltpu.make_async_copy(mine, o.at[me], local_sems.at[a]))
            s_a, r_a = {}, {}
            for k in range(1, N_DEV):
                peer, slot = peers[k - 1]
                sems = dict(send_sem=send_sems.at[7 * a + k - 1], recv_sem=recv_sems.at[7 * a + k - 1],
                            device_id_type=pl.DeviceIdType.MESH)
                if a >= ng:
                    s_a[k] = pltpu.make_async_remote_copy(src_ref=x.at[slot], dst_ref=o.at[me], device_id=peer, **sems)
                elif k in FORWARDED:
                    came = o.at[peers[k - 2][1]]
                    s_a[k] = pltpu.make_async_remote_copy(src_ref=came, dst_ref=came, device_id=peers[0][0], **sems)
                else:
                    s_a[k] = pltpu.make_async_remote_copy(src_ref=x, dst_ref=o.at[me], device_id=peer, **sems)
                r_a[k] = pltpu.make_async_remote_copy(src_ref=mine, dst_ref=o.at[slot], device_id=peer, **sems)
            sends.append(s_a)
            recvs.append(r_a)
        return local, sends, recvs

    def start(self, *refs):
        if self.n == 0:
            return
        local, sends, _ = self._copies(*refs)
        for a in range(self.n):
            local[a].start()
            for k in range(1, N_DEV):
                if a >= len(self.gathers) or k not in FORWARDED:
                    sends[a][k].start()

    def forward(self, *refs):
        if not self.gathers:
            return
        _, sends, recvs = self._copies(*refs)
        for a in range(len(self.gathers)):
            for k in FORWARDED:
                recvs[a][k - 1].wait_recv()
                sends[a][k].start()

    def wait(self, *refs):
        if self.n == 0:
            return
        local, sends, recvs = self._copies(*refs)
        for a in range(self.n):
            waited_early = [f - 1 for f in FORWARDED] if a < len(self.gathers) else []
            for k in range(1, N_DEV):
                if k not in waited_early:
                    recvs[a][k].wait_recv()
            for k in range(1, N_DEV):
                sends[a][k].wait_send()
            local[a].wait()


FORWARDED = (3, 5, 7)


def exchange(ex, name):
    n = ex.n

    def body(*refs):
        args = (refs[:n], refs[n:2 * n]) + tuple(refs[2 * n:])
        ex.start(*args)
        ex.forward(*args)
        ex.wait(*args)

    return pl.pallas_call(body, in_specs=[_ANY] * n, out_specs=[_ANY] * n, out_shape=ex.out_shape(),
                          scratch_shapes=ex.scratch(), name=name)(*ex.operands())


def _shift_up(z):
    return jnp.concatenate([z[1:], jnp.zeros_like(z[:1])], axis=0)


def _segments(width):
    seg = np.zeros((width, 128), np.float32)
    seg[np.arange(width), np.arange(width) // HEAD_DIM] = 1.0
    return jnp.asarray(seg), jnp.asarray(seg.T)


def _rope_consts(T):
    inv = ROPE_THETA ** (-jnp.arange(0, HEAD_DIM, 2, dtype=F32) / HEAD_DIM)
    ang = jnp.arange(T, dtype=F32)[:, None] * inv[None, :]
    return jnp.tile(jnp.cos(ang), (1, 4)), jnp.tile(jnp.sin(ang), (1, 4))


def _per_head(g, heads):
    return jnp.tile(g.reshape(1, HEAD_DIM), (1, heads))


def _sum_heads(g):
    return g.reshape(-1, HEAD_DIM).sum(axis=0, keepdims=True)


LORA_COLS = 256
RW_TILE = 256
ROW_TILE = 512


def _local_step(x0, memx, tgt, P, ex_weights=None, weights_done=None, ex_grads=None):
    T = x0.shape[0]
    P = dict(P)
    G = {}
    seg, seg_t = _segments(RWKV_WIDTH)
    mseg = (seg[:MEM_WIDTH], seg_t[:, :MEM_WIDTH])
    cos, sin = _rope_consts(T)
    row = lambda v: v.reshape(1, -1)

    def mem_fwd(i, q, into):
        memn = stage_fwd(f_rmsnorm, [memx], [P["mem_norm"][i:i + 1]], [], [], N_MEM, f"mem{i}_norm", [BF16])[0]
        kvm = matmul(memn, P["mem_w_kv"][i], "nn", f"mem{i}_kv")
        kn, qn = _per_head(P["mem_k_norm"][i], MEM_HEADS), _per_head(P["mem_q_norm"][i], MEM_HEADS)
        km = stage_fwd(f_headnorm, [Cols(kvm, MEM_WIDTH, 0)], [kn], [], mseg, N_MEM, f"mem{i}_knorm")[0]
        om = stage_fwd(f_memattn, [q], [km, Cols(kvm, MEM_WIDTH, 1), qn], [], mseg, ROW_TILE, f"mem{i}_attn", [BF16], into=into)[0]
        return om, (memn, kvm, km, kn, qn, q)

    def mem_bwd(i, saved, dymem, copy_into=None):
        memn, kvm, km, kn, qn, q = saved
        (dq,), (dkm, dvm, g_qn), *copy = stage_bwd(f_memattn, [q], [km, Cols(kvm, MEM_WIDTH, 1), qn], [], mseg, [dymem], ROW_TILE,
                                                   f"mem{i}_attn_bwd", bf16_copies=(0,) if copy_into else (), copy_into=copy_into)
        dq = copy[0][0] if copy_into else dq
        (dkraw,), (g_kn,) = stage_bwd(f_headnorm, [Cols(kvm, MEM_WIDTH, 0)], [kn], [], mseg, [dkm], N_MEM, f"mem{i}_knorm_bwd")
        dkvm = jnp.concatenate([dkraw, dvm], axis=1).astype(BF16)
        g_w = matmul(memn, dkvm, "tn", f"mem{i}_kv_dw")
        dmemn = matmul(dkvm, P["mem_w_kv"][i], "nt", f"mem{i}_kv_dx")
        _, (g_mn,) = stage_bwd(f_rmsnorm, [memx], [P["mem_norm"][i:i + 1]], [], [], [dmemn], N_MEM, f"mem{i}_norm_bwd")
        return dq, g_mn, g_w, _sum_heads(g_qn), _sum_heads(g_kn)

    def ffn_fwd(i, xin):
        hn = stage_fwd(f_rmsnorm, [xin], [P["ffn_norm"][i:i + 1]], [], [], ROW_TILE, f"ffn{i}_norm", [BF16])[0]
        u = matmul(hn, P["ffn_w_up"][i], "nt", f"ffn{i}_up")
        z = convgate_fwd(u, P["ffn_conv_w"][i], P["ffn_conv_b"][i:i + 1], f"ffn{i}_conv")
        return matmul(z, P["ffn_w_down"][i], "nn", f"ffn{i}_down", residual=xin), (hn, u, z)

    def ffn_bwd(i, xin, saved, dxo, dxo_b):
        hn, u, z = saved
        dz = matmul(dxo_b, P["ffn_w_down"][i], "nt", f"ffn{i}_down_dx")
        g_down = matmul(z, dxo_b, "tn", f"ffn{i}_down_dw")
        du, g_cw, g_cb = convgate_bwd(u, P["ffn_conv_w"][i], P["ffn_conv_b"][i:i + 1], dz, f"ffn{i}_conv_bwd")
        dhn = matmul(du, P["ffn_w_up"][i], "nn", f"ffn{i}_up_dx")
        g_up = matmul(du, hn, "tn", f"ffn{i}_up_dw")
        (dxin,), (g_n,), (dxin_b,) = stage_bwd(f_rmsnorm_res, [xin], [P["ffn_norm"][i:i + 1]], [], [], [dhn, dxo], ROW_TILE,
                                               f"ffn{i}_norm_bwd", bf16_copies=(0,))
        return dxin, dxin_b, g_n, g_up, g_cw, g_cb, g_down

    h0 = stage_fwd(f_rmsnorm, [x0], [P["attn_norm"][0:1]], [], [], ROW_TILE, "l0_norm", [BF16])[0]
    p0 = matmul(h0, P["a_w_in"][0], "nt", "l0_in")
    lora0 = 3 * RWKV_WIDTH // LORA_COLS
    pre_xs = [Cols(p0, RWKV_WIDTH, 0), Cols(p0, RWKV_WIDTH, 1), Cols(p0, RWKV_WIDTH, 2), Cols(p0, LORA_COLS, lora0)]
    mu = [Cols(P["a_mu"], RWKV_WIDTH, 0), Cols(P["a_mu"], RWKV_WIDTH, 1), Cols(P["a_mu"], RWKV_WIDTH, 2),
          Cols(P["a_mu"], LORA_COLS, lora0)]
    lora_rows = lambda w, lo: jnp.pad(w, ((lo, LORA_COLS - lo - w.shape[0]), (0, 0)))
    pre_ps = mu + [P["a_w0"], lora_rows(P["a_w2"][0], 0), P["a_a0"], lora_rows(P["a_a2"][0], 64), lora_rows(P["a_g2"][0], 128),
                   P["a_k_k"], P["a_k_a"]]
    r, lw, k2, v, kk, b, g = stage_fwd(f_rwkv_pre, pre_xs, pre_ps, [], [seg, seg_t], RW_TILE, "l0_rwkv_pre", with_prev=True)
    scan_in = [r, lw, k2, v, kk, b]
    y_h, h_states, got = rwkv_scan_fwd(*scan_in, ex_weights or Exchange())
    if weights_done is not None:
        P.update(weights_done(got))
    y_s = y_h
    post_ps = [P["a_lnx_w"], P["a_lnx_b"], P["a_r_k"].reshape(1, RWKV_WIDTH)]
    ycat0, mem0_saved = mem_fwd(0, Cols(p0, MEM_WIDTH, SHIFT_WIDTH // MEM_WIDTH), Into(None, D_MODEL, RWKV_WIDTH // MEM_WIDTH))
    ycat0 = stage_fwd(f_rwkv_post, [y_s, r, k2, v, g], post_ps, [], [seg, seg_t], RW_TILE, "l0_rwkv_post", [BF16],
                      into=Into(ycat0, D_MODEL, 0))[0]
    x1 = matmul(ycat0, P["a_w_out"][0], "nn", "l0_out", residual=x0)
    x2, ffn0_saved = ffn_fwd(0, x1)

    hk, h1 = stage_fwd(f_rmsnorm2, [x2], [row(P["kv_norm"]), P["attn_norm"][1:2]], [], [], ROW_TILE, "l1_norm", [BF16, BF16])
    kvp = matmul(hk, P["kv_w"][0], "nt", "l1_kv")
    p1 = matmul(h1, P["b_w_in"][0], "nn", "l1_in")
    kraw, qraw = Cols(kvp, DIL_WIDTH, 0), Cols(p1, DIL_WIDTH, 0)
    kgain, qgain = _per_head(P["kv_k_norm"], DIL_WIDTH // HEAD_DIM), _per_head(P["b_q_norm"], DIL_WIDTH // HEAD_DIM)
    ksh = stage_fwd(f_qkprep, [kraw], [kgain], [cos, sin], [seg, seg_t], ROW_TILE, "l1_kprep")[0]
    q = stage_fwd(f_qkprep, [qraw], [qgain], [cos, sin], [seg, seg_t], ROW_TILE, "l1_qprep")[0]
    outs, lses = [], []
    for gi, (_, d) in enumerate(DIL_GROUPS):
        og, lg = dil_fwd(q, ksh, kvp, gi, d, f"l1_dil{gi}")
        outs.append(og)
        lses.append(lg)
    ycat1 = stage_fwd(f_mix, outs + lses, [], [], [], ROW_TILE, "l1_mix", [BF16], into=Into(None, 2 * MEM_WIDTH, 0))[0]
    ycat1, mem1_saved = mem_fwd(1, Cols(p1, MEM_WIDTH, DIL_WIDTH // MEM_WIDTH), Into(ycat1, 2 * MEM_WIDTH, 1))
    x3 = matmul(ycat1, P["b_w_out"][0], "nt", "l1_out", residual=x2)
    x4, ffn1_saved = ffn_fwd(1, x3)
    loss_part, dx4, dx4_b = loss_head(x4, tgt)

    dx3, dx3_b, gn1, gup1, gcw1, gcb1, gdown1 = ffn_bwd(1, x3, ffn1_saved, dx4, dx4_b)
    dycat1 = matmul(dx3_b, P["b_w_out"][0], "nn", "l1_out_dx")
    G["b_w_out"] = [matmul(dx3_b, ycat1, "tn", "l1_out_dw")]
    dp1, gmn1, gmw1, gmq1, gmk1 = mem_bwd(1, mem1_saved, Cols(dycat1, MEM_WIDTH, 1), Into(None, D_MODEL, DIL_WIDTH // MEM_WIDTH))
    dmix, _ = stage_bwd(f_mix, outs + lses, [], [], [], [Cols(dycat1, MEM_WIDTH, 0)], ROW_TILE, "l1_mix_bwd")
    dq, dk, dv = zip(*[dil_bwd(q, ksh, kvp, dmix[gi], dmix[3 + gi], gi, d, f"l1_dil{gi}_bwd")
                       for gi, (_, d) in enumerate(DIL_GROUPS)])
    dq, dk, dv = jnp.concatenate(dq, axis=1), jnp.concatenate(dk, axis=1), jnp.concatenate(dv, axis=1)
    _, (g_bq,), (dp1,) = stage_bwd(f_qkprep, [qraw], [qgain], [cos, sin], [seg, seg_t], [dq], ROW_TILE, "l1_qprep_bwd",
                                   bf16_copies=(0,), copy_into=Into(dp1, D_MODEL, 0))
    dkvp = jnp.pad(dv.astype(BF16), ((0, 0), (DIL_WIDTH, 0)))
    _, (g_kk,), (dkvp,) = stage_bwd(f_qkprep, [kraw], [kgain], [cos, sin], [seg, seg_t], [dk], ROW_TILE, "l1_kprep_bwd",
                                    bf16_copies=(0,), copy_into=Into(dkvp, 2 * DIL_WIDTH, 0))
    g_bq, g_kk = _sum_heads(g_bq), _sum_heads(g_kk)
    dh1 =matmul(dp1, P["b_w_in"][0], "nt", "l1_in_dx")
    G["b_w_in"] = [matmul(h1, dp1, "tn", "l1_in_dw")]
    dhk = matmul(dkvp, P["kv_w"][0], "nn", "l1_kv_dx")
    G["kv_w"] = [matmul(dkvp, hk, "tn", "l1_kv_dw")]
    (dx2,), (g_kvn, g_an1), (dx2_b,) = stage_bwd(f_rmsnorm2_res, [x2], [row(P["kv_norm"]), P["attn_norm"][1:2]], [], [],
                                                 [dhk, dh1, dx3], ROW_TILE, "l1_norm_bwd", bf16_copies=(0,))

    dx1, dx1_b, gn0, gup0, gcw0, gcb0, gdown0 = ffn_bwd(0, x1, ffn0_saved, dx2, dx2_b)
    dycat0 = matmul(dx1_b, P["a_w_out"][0], "nt", "l0_out_dx")
    G["a_w_out"] = [matmul(ycat0, dx1_b, "tn", "l0_out_dw")]
    dqmem0, gmn0, gmw0, gmq0, gmk0 = mem_bwd(0, mem0_saved, Cols(dycat0, MEM_WIDTH, RWKV_WIDTH // MEM_WIDTH))
    (dy_s, dr_a, dk_a, dv_a, dg), (g_lw, g_lb, g_rk) = stage_bwd(
        f_rwkv_post, [y_s, r, k2, v, g], post_ps, [], [seg, seg_t], [Cols(dycat0, RWKV_WIDTH, 0)], RW_TILE, "l0_rwkv_post_bwd")
    G["mem_w_kv"], G["ffn_w_up"], G["ffn_w_down"] = [gmw0, gmw1], [gup0, gup1], [gdown0, gdown1]
    (dr_b, dlw, dk_b, dv_b, dkk, db), G["_exchanged"] = rwkv_scan_bwd(*scan_in, h_states, dy_s,
                                                                      ex_grads(G) if ex_grads else Exchange())
    dpre, gpre = stage_bwd(f_rwkv_pre, pre_xs, pre_ps, [], [seg, seg_t],
                           [[dr_a, dr_b], dlw, [dk_a, dk_b], [dv_a, dv_b], dkk, db, dg], RW_TILE, "l0_rwkv_pre_bwd", with_prev=True)
    dp_rw = jnp.concatenate(dpre[:4], axis=1) + _shift_up(jnp.concatenate(dpre[4:], axis=1))
    dp0 = jnp.concatenate([dp_rw, dqmem0], axis=1).astype(BF16)
    dh0 = matmul(dp0, P["a_w_in"][0], "nn", "l0_in_dx")
    G["a_w_in"] = [matmul(dp0, h0, "tn", "l0_in_dw")]
    (dx0,), (g_an0,) = stage_bwd(f_rmsnorm_res, [x0], [P["attn_norm"][0:1]], [], [], [dh0, dx1], ROW_TILE, "l0_norm_bwd")

    G["attn_norm"] = jnp.concatenate([g_an0, g_an1], axis=0)
    G["a_mu"] = jnp.concatenate(gpre[:4], axis=1)
    G["a_w0"], G["a_w2"], G["a_a0"], G["a_a2"], G["a_g2"] = gpre[4], gpre[5][None, :64], gpre[6], gpre[7][None, 64:128], gpre[8][None, 128:]
    G["a_k_k"], G["a_k_a"] = gpre[9], gpre[10]
    G["a_r_k"] = g_rk.reshape(1, RWKV_HEADS, HEAD_DIM)
    G["a_lnx_w"], G["a_lnx_b"] = g_lw, g_lb
    G["kv_norm"], G["kv_k_norm"], G["b_q_norm"] = g_kvn.reshape(-1), g_kk.reshape(-1), g_bq
    G["mem_norm"] = jnp.concatenate([gmn0, gmn1], axis=0)
    G["mem_w_kv"] = [gmw0, gmw1]
    G["mem_q_norm"] = jnp.concatenate([gmq0, gmq1], axis=0)
    G["mem_k_norm"] = jnp.concatenate([gmk0, gmk1], axis=0)
    G["ffn_norm"] = jnp.concatenate([gn0, gn1], axis=0)
    G["ffn_w_up"] = [gup0, gup1]
    G["ffn_conv_w"] = jnp.stack([gcw0, gcw1])
    G["ffn_conv_b"] = jnp.concatenate([gcb0, gcb1], axis=0)
    G["ffn_w_down"] = [gdown0, gdown1]
    return loss_part, dx0, G


PARAMS = (("attn_norm", None), ("a_w_in", 2), ("a_mu", 1), ("a_w0", 1), ("a_w2", 2), ("a_a0", 1), ("a_a2", 2), ("a_g2", 2),
          ("a_k_k", 1), ("a_k_a", 1), ("a_r_k", None), ("a_lnx_w", 1), ("a_lnx_b", 1), ("a_w_out", 1), ("kv_norm", None),
          ("kv_w", 1), ("kv_k_norm", None), ("b_w_in", 1), ("b_q_norm", None), ("b_w_out", 2), ("mem_norm", None),
          ("mem_w_kv", 1), ("mem_q_norm", None), ("mem_k_norm", None), ("ffn_norm", None), ("ffn_w_up", 2),
          ("ffn_conv_w", 2), ("ffn_conv_b", None), ("ffn_w_down", 1))
BIG = ("a_w_in", "a_w_out", "kv_w", "b_w_in", "b_w_out", "mem_w_kv", "ffn_w_up", "ffn_w_down")
TRANSPOSED = ("a_w_in", "kv_w", "b_w_out", "ffn_w_up")
AXIS = dict(PARAMS)
SMALL = tuple(n for n, _ in PARAMS if n not in BIG)
SMALL_SHARDED = tuple(n for n in SMALL if AXIS[n] is not None)
PACK_QUANTUM = 256 * 128


def _from_shards(xs, axis):
    full = jnp.moveaxis(xs, 0, axis)
    sh = full.shape
    return full.reshape(sh[:axis] + (sh[axis] * sh[axis + 1],) + sh[axis + 2:])


def _to_shards(g, axis):
    sh = g.shape
    return jnp.moveaxis(g.reshape(sh[:axis] + (N_DEV, sh[axis] // N_DEV) + sh[axis + 1:]), axis, 0)


def _pack(parts, lead=0):
    ld = parts[0].shape[:lead]
    flat = jnp.concatenate([p.reshape(ld + (-1,)) for p in parts], axis=-1)
    pad = (-flat.shape[-1]) % PACK_QUANTUM
    flat = jnp.pad(flat, [(0, 0)] * lead + [(0, pad)])
    return flat.reshape(ld + (-1, 128))


def _unpack(packed, shapes, lead=0):
    ld = packed.shape[:lead]
    flat = packed.reshape(ld + (-1,))
    out, off = [], 0
    for s in shapes:
        n = math.prod(s)
        out.append(flat[..., off:off + n].reshape(ld + tuple(s)))
        off += n
    return out


def kernel(x, mem, attn_norm, a_w_in, a_mu, a_w0, a_w2, a_a0, a_a2, a_g2, a_k_k, a_k_a, a_r_k, a_lnx_w, a_lnx_b, a_w_out, kv_norm, kv_w, kv_k_norm, b_w_in, b_q_norm, b_w_out, mem_norm, mem_w_kv, mem_q_norm, mem_k_norm, ffn_norm, ffn_w_up, ffn_conv_w, ffn_conv_b, ffn_w_down, loss_target, m_attn_norm, m_a_w_in, m_a_mu, m_a_w0, m_a_w2, m_a_a0, m_a_a2, m_a_g2, m_a_k_k, m_a_k_a, m_a_r_k, m_a_lnx_w, m_a_lnx_b, m_a_w_out, m_kv_norm, m_kv_w, m_kv_k_norm, m_b_w_in, m_b_q_norm, m_b_w_out, m_mem_norm, m_mem_w_kv, m_mem_q_norm, m_mem_k_norm, m_ffn_norm, m_ffn_w_up, m_ffn_conv_w, m_ffn_conv_b, m_ffn_w_down, v_attn_norm, v_a_w_in, v_a_mu, v_a_w0, v_a_w2, v_a_a0, v_a_a2, v_a_g2, v_a_k_k, v_a_k_a, v_a_r_k, v_a_lnx_w, v_a_lnx_b, v_a_w_out, v_kv_norm, v_kv_w, v_kv_k_norm, v_b_w_in, v_b_q_norm, v_b_w_out, v_mem_norm, v_mem_w_kv, v_mem_q_norm, v_mem_k_norm, v_ffn_norm, v_ffn_w_up, v_ffn_conv_w, v_ffn_conv_b, v_ffn_w_down):
    names = [n for n, _ in PARAMS]
    vals = (attn_norm, a_w_in, a_mu, a_w0, a_w2, a_a0, a_a2, a_g2, a_k_k, a_k_a, a_r_k, a_lnx_w, a_lnx_b, a_w_out, kv_norm, kv_w, kv_k_norm, b_w_in, b_q_norm, b_w_out, mem_norm, mem_w_kv, mem_q_norm, mem_k_norm, ffn_norm, ffn_w_up, ffn_conv_w, ffn_conv_b, ffn_w_down)
    m_vals = (m_attn_norm, m_a_w_in, m_a_mu, m_a_w0, m_a_w2, m_a_a0, m_a_a2, m_a_g2, m_a_k_k, m_a_k_a, m_a_r_k, m_a_lnx_w, m_a_lnx_b, m_a_w_out, m_kv_norm, m_kv_w, m_kv_k_norm, m_b_w_in, m_b_q_norm, m_b_w_out, m_mem_norm, m_mem_w_kv, m_mem_q_norm, m_mem_k_norm, m_ffn_norm, m_ffn_w_up, m_ffn_conv_w, m_ffn_conv_b, m_ffn_w_down)
    v_vals = (v_attn_norm, v_a_w_in, v_a_mu, v_a_w0, v_a_w2, v_a_a0, v_a_a2, v_a_g2, v_a_k_k, v_a_k_a, v_a_r_k, v_a_lnx_w, v_a_lnx_b, v_a_w_out, v_kv_norm, v_kv_w, v_kv_k_norm, v_b_w_in, v_b_q_norm, v_b_w_out, v_mem_norm, v_mem_w_kv, v_mem_q_norm, v_mem_k_norm, v_ffn_norm, v_ffn_w_up, v_ffn_conv_w, v_ffn_conv_b, v_ffn_w_down)
    W, M, V = dict(zip(names, vals)), dict(zip(names, m_vals)), dict(zip(names, v_vals))
    layers = lambda D, n: [D[n]] if D[n].ndim == 2 else [D[n][i] for i in range(D[n].shape[0])]
    ax2 = lambda n: AXIS[n] - (W[n].ndim - 2)
    later =[(n, i) for n in BIG if n != "a_w_in" for i in range(len(layers(W, n)))]

    sent = lambda n, w: w.T if n in TRANSPOSED else w
    whole = lambda n, g: g.reshape(-1, g.shape[-1]) if n in TRANSPOSED else _from_shards(g, ax2(n))
    small_shapes = [W[n].shape for n in SMALL_SHARDED]
    got_w, got_small = exchange(Exchange(gathers=[sent("a_w_in", W["a_w_in"][0]).astype(BF16),
                                                  _pack([W[n] for n in SMALL_SHARDED])]), "gather_first")
    P = {n: W[n] for n in SMALL}
    P["a_w_in"] = [whole("a_w_in", got_w)]
    for n, s in zip(SMALL_SHARDED, _unpack(got_small, small_shapes, lead=1)):
        P[n] = _from_shards(s, AXIS[n])
    ex_weights =Exchange(gathers=[sent(n, layers(W, n)[i]).astype(BF16) for n, i in later])

    def weights_done(got):
        out = {}
        for (n, _), g in zip(later, got):
            out.setdefault(n, []).append(whole(n, g))
        return out

    slots = lambda G, n: jnp.stack([_to_shards(g, 0 if n in TRANSPOSED else ax2(n)) for g in G[n]], axis=1)
    later_names = [n for n in BIG if n != "a_w_in"]
    ex_grads = lambda G: Exchange(scatters=[slots(G, n) for n in later_names])
    loss_part, dx0, G = _local_step(x[0], mem[0], loss_target[0], P, ex_weights, weights_done, ex_grads)
    gparts = dict(zip(later_names, G.pop("_exchanged")))
    replicated = [n for n in SMALL if AXIS[n] is None]
    small_slots = _pack([_to_shards(G[n], AXIS[n]) for n in SMALL_SHARDED], lead=1)
    got_rep, gparts["a_w_in"], got_sharded = exchange(
        Exchange(gathers=[_pack([G[n] for n in replicated] + [loss_part[0:1, 0:1]])], scatters=[slots(G, "a_w_in"), small_slots]),
        "exchange_last")

    results = {}
    for n in BIG:
        rows = lambda z: z.reshape((-1,) + z.shape[-1:])
        gp = gparts[n].reshape((N_DEV, -1) + gparts[n].shape[-1:])
        if n in TRANSPOSED:
            gp = jnp.swapaxes(sum_parts(gp, f"sum_{n}").reshape(gparts[n].shape[1:]), -1, -2).reshape((1,) + rows(W[n]).shape)
        res = adamw(gp, rows(W[n]), rows(M[n]), rows(V[n]), f"adamw_{n}")
        results[n] = [r.reshape(W[n].shape) for r in res]
    *rep_sums, loss = _unpack(sum_parts(got_rep, "sum_replicated_grads"), [W[n].shape for n in replicated] + [()])
    g_mine = dict(zip(replicated, rep_sums))
    g_mine.update(zip(SMALL_SHARDED, _unpack(sum_parts(got_sharded, "sum_small_sharded_grads"), [W[n].shape for n in SMALL_SHARDED])))
    res = adamw(_pack([g_mine[n] for n in SMALL])[None], _pack([W[n] for n in SMALL]), _pack([M[n] for n in SMALL]),
                _pack([V[n] for n in SMALL]), "adamw_small")
    for n, parts in zip(SMALL, zip(*[_unpack(r, [W[n].shape for n in SMALL]) for r in res])):
        results[n] = list(parts)
    outs = [[results[n][j] for n in names] for j in range(4)]
    return (loss, dx0[None], *outs[0], *outs[1], *outs[2], *outs[3])
```

```python
import functools
import math

import jax
import jax.numpy as jnp
import numpy as np
from jax import lax
from jax.experimental import pallas as pl
from jax.experimental.pallas import tpu as pltpu

F32 = jnp.float32
BF16 = jnp.bfloat16
H3 =lax.Precision.HIGH

N_DEV = 8
D_MODEL = 1024
HEAD_DIM = 64
N_MEM = 256
MEM_HEADS = 4
MEM_WIDTH = 256
RWKV_HEADS = 12
RWKV_WIDTH = 768
SHIFT_WIDTH = 2560
DIL_GROUPS = ((128, 1), (512, 4), (2048, 16))
DIL_BLOCK = 128
DIL_WIDTH = 768
D_FF = 2816
RMS_EPS = 1e-6
LNX_EPS = 64e-5
NEG_INF = -1e30
ROPE_THETA = 10000.0
ADAM_LR, ADAM_B1, ADAM_B2, ADAM_EPS, ADAM_WD, ADAM_STEP = 0.001, 0.9, 0.999, 1e-08, 0.01, 10

CHUNK = 64
SCAN_GROUPS_FWD, SCAN_GROUPS_BWD = 1, 1
MM_TILE_CAP = 1408
VMEM_LIMIT_V7X = 48 * 1024 * 1024


def _cparams(sem):
    return pltpu.CompilerParams(dimension_semantics=sem, vmem_limit_bytes=VMEM_LIMIT_V7X)


def _pick(n, cands):
    for c in cands:
        if n % c == 0:
            return c
    return n


def _tile(n, cap):
    if n <= cap:
        return n
    for d in range(cap - cap % 128, 0, -128):
        if n % d == 0:
            return d
    return n


def _dg(a, b, ca, cb, batch):
    dims = (((ca,), (cb,)), ((0,), (0,))) if batch else (((ca,), (cb,)), ((), ()))
    return lax.dot_general(a.astype(BF16), b.astype(BF16), dims, preferred_element_type=F32)


@jax.custom_vjp
def mm_nn(a, b):
    n = a.ndim
    return _dg(a, b, n - 1, n - 2, n == 3)


def _mm_nn_fwd(a, b):
    return mm_nn(a, b), (a, b)


def _mm_nn_bwd(res, g):
    a, b = res
    n = a.ndim
    return _dg(g, b, n - 1, n - 1, n == 3), _dg(a, g, n - 2, n - 2, n == 3)


mm_nn.defvjp(_mm_nn_fwd, _mm_nn_bwd)


@jax.custom_vjp
def mm_nt(a, b):
    n = a.ndim
    return _dg(a, b, n - 1, n - 1, n == 3)


def _mm_nt_fwd(a, b):
    return mm_nt(a, b), (a, b)


def _mm_nt_bwd(res, g):
    a, b = res
    n = a.ndim
    return _dg(g, b, n - 1, n - 2, n == 3), _dg(g, a, n - 2, n - 2, n == 3)


mm_nt.defvjp(_mm_nt_fwd, _mm_nt_bwd)


def mmh(a, b):
    n = a.ndim
    dims = (((n - 1,), (n - 2,)), ((0,), (0,))) if n == 3 else (((1,), (0,)), ((), ()))
    return lax.dot_general(a, b, dims, precision=H3, preferred_element_type=F32)


def mmh_nt(a, b):
    n = a.ndim
    dims = (((n - 1,), (n - 1,)), ((0,), (0,))) if n == 3 else (((1,), (1,)), ((), ()))
    return lax.dot_general(a, b, dims, precision=H3, preferred_element_type=F32)


def mmh_tn(a, b):
    n = a.ndim
    dims = (((n - 2,), (n - 2,)), ((0,), (0,))) if n == 3 else (((0,), (0,)), ((), ()))
    return lax.dot_general(a, b, dims, precision=H3, preferred_element_type=F32)


def matmul(a, b, mode, name, residual=None):
    out_dtype = BF16 if mode == "tn" else F32
    if mode == "nn":
        (M, K), (_, N) = a.shape, b.shape
    elif mode == "nt":
        (M, K), (N, _) = a.shape, b.shape
    else:
        (K, M), (_, N) = a.shape, b.shape
    tm = _tile(M, 2048 if mode == "nn" else MM_TILE_CAP)
    tn = _tile(N, 512 if mode == "nn" else MM_TILE_CAP)
    tk = _tile(K, MM_TILE_CAP)
    nk = K // tk
    if mode == "nn":
        a_spec = pl.BlockSpec((tm, tk), lambda i, j, k: (i, k))
        b_spec = pl.BlockSpec((tk, tn), lambda i, j, k: (k, j))
        dims = (((1,), (0,)), ((), ()))
    elif mode == "nt":
        a_spec = pl.BlockSpec((tm, tk), lambda i, j, k: (i, k))
        b_spec = pl.BlockSpec((tn, tk), lambda i, j, k: (j, k))
        dims = (((1,), (1,)), ((), ()))
    else:
        a_spec = pl.BlockSpec((tk, tm), lambda i, j, k: (k, i))
        b_spec = pl.BlockSpec((tk, tn), lambda i, j, k: (k, j))
        dims = (((0,), (0,)), ((), ()))
    o_spec = pl.BlockSpec((tm, tn), lambda i, j, k: (i, j))
    has_res = residual is not None

    def body(*refs):
        if has_res:
            a_ref, b_ref, r_ref, o_ref, acc_ref = refs
        else:
            a_ref, b_ref, o_ref, acc_ref = refs
        k = pl.program_id(2)

        @pl.when(k == 0)
        def _():
            acc_ref[...] = jnp.zeros_like(acc_ref)

        acc_ref[...] += lax.dot_general(a_ref[...].astype(BF16), b_ref[...].astype(BF16), dims,
                                        preferred_element_type=F32)

        @pl.when(k == nk - 1)
        def _():
            if has_res:
                o_ref[...] = (acc_ref[...] + r_ref[...]).astype(out_dtype)
            else:
                o_ref[...] = acc_ref[...].astype(out_dtype)

    ins = [a, b] + ([residual] if has_res else [])
    in_specs = [a_spec, b_spec] + ([o_spec] if has_res else [])
    return pl.pallas_call(
        body, grid=(M // tm, N // tn, nk), in_specs=in_specs, out_specs=o_spec,
        out_shape=jax.ShapeDtypeStruct((M, N), out_dtype), scratch_shapes=[pltpu.VMEM((tm, tn), F32)],
        compiler_params=_cparams(("parallel", "parallel", "arbitrary")), name=name)(*ins)


class Cols:
    def __init__(self, arr, width, idx):
        self.arr, self.width, self.idx = arr, width, idx


def _arr(x):
    return x.arr if isinstance(x, Cols) else x


def _shape(x):
    return x.arr.shape[:-1] + (x.width,) if isinstance(x, Cols) else x.shape


def _col(x):
    return x.idx if isinstance(x, Cols) else 0


def _tok_spec(x, tile):
    shape, col = _shape(x), _col(x)
    return pl.BlockSpec(shape[:-2] + (tile, shape[-1]), lambda i: (0,) * (len(shape) - 2) + (i, col))


def _full_spec(x):
    shape, col = _shape(x), _col(x)
    return pl.BlockSpec(shape, lambda i: (0,) * (len(shape) - 1) + (col,))


def _halo_spec(x, tile):
    shape, col = _shape(x), _col(x)
    return pl.BlockSpec((8, shape[-1]), lambda i: (jnp.maximum(i * (tile // 8) - 1, 0), col))


def _blk(x, tile):
    shape = _shape(x)
    return jax.ShapeDtypeStruct(shape[:-2] + (tile, shape[-1]), _arr(x).dtype)


def _prev_rows(x, halo):
    rows = lax.broadcasted_iota(jnp.int32, (x.shape[0], 1), 0)
    before = jnp.where(pl.program_id(0) > 0, halo[7:8], 0.0)
    return jnp.where(rows == 0, before, pltpu.roll(x, 1, 0))


class Into:
    def __init__(self, buf, total, idx):
        self.buf, self.total, self.idx = buf, total, idx

    def place(self, shape, tile):
        idx = self.idx
        return shape.update(shape=(shape.shape[0], self.total)), pl.BlockSpec((tile, shape.shape[1]), lambda i: (i, idx))

    def operand(self, n_in, n_out_index):
        if self.buf is None:
            return [], [], {}
        return [self.buf], [_ANY], {n_in: n_out_index}


def stage_fwd(f, xs, ps, cts, cfs, tile, name, out_dtypes=None, with_prev=False, into=None):
    xs, ps, cts, cfs = list(xs), list(ps), list(cts), list(cfs)
    halos = xs if with_prev else []
    nx, nh, nct, np_ = len(xs), len(halos), len(cts), len(ps)
    T = _shape(xs[0])[-2]
    blk = [_blk(x, tile) for x in xs]
    out_avals = jax.eval_shape(f, *blk, *(blk if with_prev else []), *[_blk(p, _shape(p)[-2]) for p in ps],
                               *[_blk(c, tile) for c in cts], *[_blk(c, _shape(c)[-2]) for c in cfs])
    if out_dtypes is None:
        out_dtypes = [o.dtype for o in out_avals]
    out_shape = [jax.ShapeDtypeStruct(o.shape[:-2] + (T, o.shape[-1]), dt) for o, dt in zip(out_avals, out_dtypes)]
    out_specs = [_tok_spec(o, tile) for o in out_shape]
    n_in = nx + nh + nct + np_ + len(cfs)
    extra, extra_specs, alias = [], [], {}
    if into is not None:
        out_shape[0], out_specs[0] = into.place(out_shape[0], tile)
        extra, extra_specs, alias = into.operand(n_in, 0)

    def body(*refs):
        vals = [r[...] for r in refs[:n_in]]
        xv, hv, rest = vals[:nx], vals[nx:nx + nh], vals[nx + nh:]
        ctv, pv, cfv = rest[:nct], rest[nct:nct + np_], rest[nct + np_:]
        prev = [_prev_rows(x, h) for x, h in zip(xv, hv)]
        res = f(*xv, *prev, *pv, *ctv, *cfv)
        for o_ref, r in zip(refs[n_in + len(extra):], res):
            o_ref[...] = r.astype(o_ref.dtype)

    return pl.pallas_call(
        body, grid=(T // tile,),
        in_specs=([_tok_spec(x, tile) for x in xs] + [_halo_spec(x, tile) for x in halos] + [_tok_spec(c, tile) for c in cts]
                  + [_full_spec(p) for p in ps + cfs] + extra_specs),
        out_specs=out_specs, out_shape=out_shape, input_output_aliases=alias,
        compiler_params=_cparams(("parallel",)), name=name)(*[_arr(a) for a in xs + halos + cts + ps + cfs], *extra)


def stage_bwd(f, xs, ps, cts, cfs, gs, tile, name, bf16_copies=(), with_prev=False, copy_into=None):
    xs, ps, cts, cfs = list(xs), list(ps), list(cts), list(cfs)
    gs = [list(g) if isinstance(g, (list, tuple)) else [g] for g in gs]
    g_flat = [a for g in gs for a in g]
    halos = xs if with_prev else []
    nx, nh, nct, ng, np_ = len(xs), len(halos), len(cts), len(g_flat), len(ps)
    T = _shape(xs[0])[-2]
    dx_like = xs + halos
    out_shape = ([jax.ShapeDtypeStruct(_shape(x), F32) for x in dx_like] + [jax.ShapeDtypeStruct(_shape(p), F32) for p in ps]
                 + [jax.ShapeDtypeStruct(_shape(xs[i]), BF16) for i in bf16_copies])
    n_in = nx + nh + nct + ng + np_ + len(cfs)
    ndx = nx + nh
    plain = lambda x: jax.ShapeDtypeStruct(_shape(x), F32)
    out_specs = ([_tok_spec(plain(x), tile) for x in dx_like] + [_full_spec(plain(p)) for p in ps]
                 + [_tok_spec(plain(xs[i]), tile) for i in bf16_copies])
    extra, extra_specs, alias = [], [], {}
    if copy_into is not None:
        out_shape[ndx + np_], out_specs[ndx + np_] = copy_into.place(out_shape[ndx + np_], tile)
        extra, extra_specs, alias = copy_into.operand(n_in, ndx + np_)

    def body(*refs):
        vals = [r[...] for r in refs[:n_in]]
        outs = refs[n_in + len(extra):]
        xv, hv, rest = vals[:nx], vals[nx:nx + nh], vals[nx + nh:]
        ctv, gparts, pv, cfv = rest[:nct], rest[nct:nct + ng], rest[nct + ng:nct + ng + np_], rest[nct + ng + np_:]
        gv = []
        for g in gs:
            gv.append(functools.reduce(lambda a, b: a + b, gparts[:len(g)]))
            gparts = gparts[len(g):]
        prev = [_prev_rows(x, h) for x, h in zip(xv, hv)]
        _, vjp = jax.vjp(lambda *xp: f(*xp, *ctv, *cfv), *xv, *prev, *pv)
        d = vjp(tuple(gv))
        for o_ref, r in zip(outs[:ndx], d[:ndx]):
            o_ref[...] = r
        for o_ref, i in zip(outs[ndx + np_:], bf16_copies):
            o_ref[...] = d[i].astype(BF16)

        @pl.when(pl.program_id(0) == 0)
        def _():
            for o_ref in outs[ndx:ndx + np_]:
                o_ref[...] = jnp.zeros_like(o_ref)

        for o_ref, r in zip(outs[ndx:ndx + np_], d[ndx:]):
            o_ref[...] += r

    res = pl.pallas_call(
        body, grid=(T // tile,),
        in_specs=([_tok_spec(x, tile) for x in xs] + [_halo_spec(x, tile) for x in halos]
                  + [_tok_spec(c, tile) for c in cts + g_flat] + [_full_spec(p) for p in ps + cfs] + extra_specs),
        out_specs=out_specs, out_shape=out_shape, input_output_aliases=alias,
        compiler_params=_cparams(("arbitrary",)), name=name)(*[_arr(a) for a in xs + halos + cts + g_flat + ps + cfs], *extra)
    if bf16_copies:
        return list(res[:ndx]), list(res[ndx:ndx + np_]), list(res[ndx + np_:])
    return list(res[:ndx]), list(res[ndx:])


def _rms(x, g, eps=RMS_EPS):
    return x * lax.rsqrt(jnp.mean(x * x, axis=-1, keepdims=True) + eps) * g


def f_rmsnorm(x, g):
    return (_rms(x, g),)


def f_rmsnorm_res(x, g):
    return _rms(x, g), x


def f_rmsnorm2(x, g1, g2):
    n = x * lax.rsqrt(jnp.mean(x * x, axis=-1, keepdims=True) + RMS_EPS)
    return n * g1, n * g2


def f_rmsnorm2_res(x, g1, g2):
    return f_rmsnorm2(x, g1, g2) + (x,)


def _sigmoid(x):
    return 1.0 / (1.0 + jnp.exp(-x))


def _softplus(x):
    return jnp.maximum(x, 0.0) + jnp.log(1.0 + jnp.exp(-jnp.abs(x)))


def f_rwkv_pre(pr, pk, pv, pl_, qr, qk, qv, ql, mu_r, mu_k, mu_v, mu_l, w0, w2, a0, a2, g2, k_k, k_a, seg, seg_t):
    xr = pr + (qr - pr) * mu_r
    xk = pk + (qk - pk) * mu_k
    xv = pv + (qv - pv) * mu_v
    xl = pl_ + (ql - pl_) * mu_l
    w_log = -_softplus(-(w0 + mm_nn(jnp.tanh(xl), w2))) - 0.5
    lw = -jnp.exp(w_log)
    a = _sigmoid(a0 + mm_nn(xl, a2))
    g = mm_nn(_sigmoid(xl), g2)
    kkr = xk * k_k
    inv = lax.rsqrt(jnp.maximum(mmh(kkr * kkr, seg), 1e-24))
    kk = kkr * mmh(inv, seg_t)
    k2 = xk * (1.0 + (a - 1.0) * k_a)
    return xr, lw, k2, xv, kk, kk * a, g


def f_rwkv_post(y, r, k2, v, g, lnx_w, lnx_b, r_k, seg, seg_t):
    inv_n = 1.0 / HEAD_DIM
    m = mmh(mmh(y, seg) * inv_n, seg_t)
    yc = y - m
    rstd = lax.rsqrt(mmh(yc * yc, seg) * inv_n + LNX_EPS)
    yn = yc * mmh(rstd, seg_t) * lnx_w + lnx_b
    bonus = mmh(mmh(r * k2 * r_k, seg), seg_t) * v
    return ((yn + bonus) * g,)


def _headnorm(z, g, seg, seg_t):
    ms = mmh(z * z, seg) * (1.0 / HEAD_DIM)
    return z * mmh(lax.rsqrt(ms + RMS_EPS), seg_t) * g


def f_headnorm(z, g, seg, seg_t):
    return (_headnorm(z, g, seg, seg_t),)


def _rot_half(z):
    w = z.shape[1]
    half = HEAD_DIM // 2
    lane = lax.broadcasted_iota(jnp.int32, (1, w), 1)
    return jnp.where((lane & (HEAD_DIM - 1)) < half, -pltpu.roll(z, w - half, 1), pltpu.roll(z, half, 1))


@jax.custom_vjp
def _rotate_half(z):
    return _rot_half(z)


_rotate_half.defvjp(lambda z: (_rot_half(z), None), lambda _, g: (-_rot_half(g),))


def f_qkprep(z, g, cos, sin, seg, seg_t):
    zn = _headnorm(z, g, seg, seg_t)
    pairs = z.shape[1] // cos.shape[1]
    return (zn * jnp.tile(cos, (1, pairs)) + _rotate_half(zn) * jnp.tile(sin, (1, pairs)),)


def _head_mask(width, h):
    lane = lax.broadcasted_iota(jnp.int32, (1, width), 1)
    return jnp.where((lane >> 6) == h, jnp.ones((), F32), 0.0)


def f_memattn(q, k, v, q_norm, seg, seg_t):
    qn = _headnorm(q, q_norm, seg, seg_t)
    out = jnp.zeros_like(q)
    for h in range(MEM_HEADS):
        m = _head_mask(MEM_WIDTH, h)
        s = mm_nt(qn * m, k) * (1.0 / math.sqrt(HEAD_DIM))
        s = s - jnp.max(s, axis=-1, keepdims=True)
        p = jnp.exp(s)
        p = p / jnp.sum(p, axis=-1, keepdims=True)
        out = out + mm_nn(p, v) * m
    return (out,)


def f_mix(o1, o2, o3, l1, l2, l3):
    mx = jnp.maximum(jnp.maximum(l1, l2), l3)
    e1, e2, e3 = jnp.exp(l1 - mx), jnp.exp(l2 - mx), jnp.exp(l3 - mx)
    return ((e1 * o1 + e2 * o2 + e3 * o3) / (e1 + e2 + e3),)


def _chunk_masks(L):
    t = lax.broadcasted_iota(jnp.int32, (L, L), 0)
    s = lax.broadcasted_iota(jnp.int32, (L, L), 1)
    return t, s


def _unit_lower_inverse(a):
    L = a.shape[-1]
    t, s = _chunk_masks(L)
    one = jnp.ones((), F32)
    blk = lambda sh: jnp.where((t >> sh) == (s >> sh), one, 0.0)
    n0 = a * blk(3)
    x = jnp.where(t == s, one, 0.0) - n0
    n2 = mmh(n0, n0)
    x = x + mmh(x, n2)
    x = x + mmh(x, mmh(n2, n2))
    for sh in (3, 4, 5):
        if (1 << sh) >= L:
            break
        off = a * (blk(sh + 1) - blk(sh))
        x = x - mmh(x, mmh(off, x))
    return x


@jax.custom_vjp
def _inverse_known(a, x):
    return x


def _inverse_known_fwd(a, x):
    return x, x


def _inverse_known_bwd(x, dx):
    return -mmh_nt(mmh_tn(x, dx), x), jnp.zeros_like(x)


_inverse_known.defvjp(_inverse_known_fwd, _inverse_known_bwd)


def _running_sum(x, reverse):
    L = x.shape[1]
    pos = lax.broadcasted_iota(jnp.int32, (1, L, 1), 1)
    step = 1
    while step < L:
        if reverse:
            x = x + jnp.where(pos < L - step, pltpu.roll(x, L - step, 1), 0.0)
        else:
            x = x + jnp.where(pos >= step, pltpu.roll(x, step, 1), 0.0)
        step *= 2
    return x


@jax.custom_vjp
def _cumsum_tokens(x):
    return _running_sum(x, False)


_cumsum_tokens.defvjp(lambda x: (_running_sum(x, False), None), lambda _, g: (_running_sum(g, True),))


def f_rwkv_chunk(s0, r, lw, k, v, kk, b, x_known=None):
    H, L, _ = r.shape
    t, s = _chunk_masks(L)
    one = jnp.ones((), F32)
    incl = jnp.where(t >= s, one, 0.0)
    strict = jnp.where(t > s, one, 0.0)
    cum = _cumsum_tokens(lw)
    w_in = jnp.exp(cum)
    w_ex = jnp.exp(cum - lw)
    w_inv = jnp.exp(-cum)
    rt, kkt, kt, bt = r * w_in, kk * w_ex, k * w_inv, b * w_inv
    a_b = mmh_nt(kkt, bt) * strict
    a_k = mmh_nt(kkt, kt) * strict
    m_k = mmh_nt(rt, kt) * incl
    m_b = mmh_nt(rt, bt) * incl
    x = _unit_lower_inverse(a_b) if x_known is None else _inverse_known(a_b, x_known)
    u = mmh(x, mmh_nt(kkt, s0) + mmh(a_k, v))
    y = mmh_nt(rt, s0) + mmh(m_k, v) - mmh(m_b, u)
    w_last = jnp.exp(jnp.sum(lw, axis=1, keepdims=True))
    s1 = (s0 + mmh_tn(v, kt) - mmh_tn(u, bt)) * w_last
    return y, s1, x


def _ex_split(ex, refs, n_in, n_out):
    n = ex.n
    ins, ex_in = refs[:n_in], refs[n_in:n_in + n]
    outs, ex_out = refs[n_in + n:n_in + n + n_out], refs[n_in + n + n_out:n_in + 2 * n + n_out]
    rest = refs[n_in + 2 * n + n_out:]
    return ins, outs, rest[:len(rest) - 3], (ex_in, ex_out) + tuple(rest[len(rest) - 3:])


def _split_heads(x):
    return jnp.stack([x[:, h * HEAD_DIM:(h + 1) * HEAD_DIM] for h in range(x.shape[1] // HEAD_DIM)], axis=0)


def _merge_heads(x):
    return jnp.concatenate([x[h] for h in range(x.shape[0])], axis=1)


def rwkv_scan_fwd(r, lw, k, v, kk, b, ex):
    T, N = r.shape[0], HEAD_DIM
    H = r.shape[1] // N
    groups = SCAN_GROUPS_FWD
    nc, hg = T // CHUNK, H // groups
    seq = pl.BlockSpec((CHUNK, hg * N), lambda g, c: (c, g))

    def body(*refs):
        (r_ref, lw_ref, k_ref, v_ref, kk_ref, b_ref), (y_ref, hs_ref, xs_ref), (h_scr,), ex_refs = _ex_split(ex, refs, 6, 3)
        g, c = pl.program_id(0), pl.program_id(1)

        @pl.when(jnp.logical_and(g == 0, c == 0))
        def _():
            ex.start(*ex_refs)

        @pl.when(c == 0)
        def _():
            h_scr[...] = jnp.zeros_like(h_scr)

        h0 = h_scr[...]
        hs_ref[0] = h0
        y, h1, x = f_rwkv_chunk(h0, *[_split_heads(z[...]) for z in (r_ref, lw_ref, k_ref, v_ref, kk_ref, b_ref)])
        y_ref[...] = _merge_heads(y)
        xs_ref[0] = x
        h_scr[...] = h1

        @pl.when(jnp.logical_and(g == groups - 1, c == (3 * nc) // 4))
        def _():
            ex.forward(*ex_refs)

        @pl.when(jnp.logical_and(g == groups - 1, c == nc - 1))
        def _():
            ex.wait(*ex_refs)

    res = pl.pallas_call(
        body, grid=(groups, nc), in_specs=[seq] * 6 + [_ANY] * ex.n,
        out_specs=[seq, pl.BlockSpec((1, hg, N, N), lambda g, c: (c, g, 0, 0)),
                   pl.BlockSpec((1, hg, CHUNK, CHUNK), lambda g, c: (c, g, 0, 0))] + [_ANY] * ex.n,
        out_shape=[jax.ShapeDtypeStruct((T, H * N), F32), jax.ShapeDtypeStruct((nc, H, N, N), F32),
                   jax.ShapeDtypeStruct((nc, H, CHUNK, CHUNK), F32)] + ex.out_shape(),
        scratch_shapes=[pltpu.VMEM((hg, N, N), F32)] + ex.scratch(),
        compiler_params=_cparams(("arbitrary", "arbitrary")), name="rwkv_scan_fwd")(r, lw, k, v, kk, b, *ex.operands())
    return res[0], (res[1], res[2]), list(res[3:])


def rwkv_scan_bwd(r, lw, k, v, kk, b, saved, dy, ex):
    T, N = r.shape[0], HEAD_DIM
    H = r.shape[1] // N
    groups = SCAN_GROUPS_BWD
    nc, hg = T // CHUNK, H // groups
    seq = pl.BlockSpec((CHUNK, hg * N), lambda g, c: (nc - 1 - c, g))
    state = pl.BlockSpec((1, hg, N, N), lambda g, c: (nc - 1 - c, g, 0, 0))

    def body(*refs):
        (r_ref, lw_ref, k_ref, v_ref, kk_ref, b_ref, hs_ref, xs_ref, dy_ref), outs, (dh_scr,), ex_refs = _ex_split(ex, refs, 9, 6)
        g, c = pl.program_id(0), pl.program_id(1)

        @pl.when(jnp.logical_and(g == 0, c == 0))
        def _():
            ex.start(*ex_refs)

        @pl.when(c == 0)
        def _():
            dh_scr[...] = jnp.zeros_like(dh_scr)

        x_known = xs_ref[0]
        _, vjp = jax.vjp(lambda *a: f_rwkv_chunk(*a, x_known=x_known)[:2], hs_ref[0],
                         *[_split_heads(z[...]) for z in (r_ref, lw_ref, k_ref, v_ref, kk_ref, b_ref)])
        d = vjp((_split_heads(dy_ref[...]), dh_scr[...]))
        dh_scr[...] = d[0]
        for o_ref, dz in zip(outs, d[1:]):
            o_ref[...] = _merge_heads(dz)

        @pl.when(jnp.logical_and(g == groups - 1, c == nc - 1))
        def _():
            ex.forward(*ex_refs)
            ex.wait(*ex_refs)

    res = pl.pallas_call(
        body, grid=(groups, nc),
        in_specs=[seq] * 6 + [state, state, seq] + [_ANY] * ex.n,
        out_specs=[seq] * 6 + [_ANY] * ex.n, out_shape=[jax.ShapeDtypeStruct((T, H * N), F32)] * 6 + ex.out_shape(),
        scratch_shapes=[pltpu.VMEM((hg, N, N), F32)] + ex.scratch(),
        compiler_params=_cparams(("arbitrary", "arbitrary")), name="rwkv_scan_bwd")(r, lw, k, v, kk, b, *saved, dy, *ex.operands())
    return list(res[:6]), list(res[6:])


GROUP_COLS = 4 * HEAD_DIM


def _f_dilattn(has_prev, q, kc, kp, vc, vp):
    scale = 1.0 / math.sqrt(HEAD_DIM)
    i = lax.broadcasted_iota(jnp.int32, (DIL_BLOCK, DIL_BLOCK), 0)
    j = lax.broadcasted_iota(jnp.int32, (DIL_BLOCK, DIL_BLOCK), 1)
    o, l = jnp.zeros_like(q), jnp.zeros_like(q)
    for h in range(q.shape[1] // HEAD_DIM):
        m = _head_mask(q.shape[1], h)
        sc = jnp.where(j <= i, mm_nt(q * m, kc) * scale, NEG_INF)
        sp = jnp.where(jnp.logical_and(i <= j, has_prev), mm_nt(q * m, kp) * scale, NEG_INF)
        mx = jnp.maximum(jnp.max(sc, axis=-1, keepdims=True), jnp.max(sp, axis=-1, keepdims=True))
        pc, pp = jnp.exp(sc - mx), jnp.exp(sp - mx)
        den = jnp.sum(pc, axis=-1, keepdims=True) + jnp.sum(pp, axis=-1, keepdims=True)
        o = o + (mm_nn(pc, vc) + mm_nn(pp, vp)) / den * m
        l = l + (mx + jnp.log(den)) * m
    return o, l


def _dil_specs(gi, d):
    parts = 1 if d == 1 else 2
    blk = (DIL_BLOCK * d, GROUP_COLS // parts)
    at = lambda col: (lambda p, n: (n, col * parts + p))
    before = lambda col: (lambda p, n: (jnp.maximum(n - 1, 0), col * parts + p))
    v0 = DIL_WIDTH // GROUP_COLS + gi
    q = pl.BlockSpec(blk, at(gi))
    kc, kp = pl.BlockSpec(blk, at(gi)), pl.BlockSpec(blk, before(gi))
    vc, vp = pl.BlockSpec(blk, at(v0)), pl.BlockSpec(blk, before(v0))
    out = pl.BlockSpec(blk, at(0))
    together = min(d, 2)
    return (q, kc, kp, vc, vp, out), parts, together


def _residue_rows(r, d):
    return pl.ds(r, DIL_BLOCK, stride=d) if d > 1 else pl.ds(0, DIL_BLOCK)


def dil_fwd(q, k, kv, gi, d, name):
    T = q.shape[0]
    (qs, kc, kp, vc, vp, out), parts, together = _dil_specs(gi, d)

    def body(q_ref, kc_ref, kp_ref, vc_ref, vp_ref, o_ref, l_ref):
        has_prev = pl.program_id(1) > 0

        def residues(it, carry):
            rows = [_residue_rows(it * together + a, d) for a in range(together)]
            ins = [[ref[rw, :] for ref in (q_ref, kc_ref, kp_ref, vc_ref, vp_ref)] for rw in rows]
            res = [_f_dilattn(has_prev, *x) for x in ins]
            for rw, (o, l) in zip(rows, res):
                o_ref[rw, :] = o
                l_ref[rw, :] = l
            return carry

        lax.fori_loop(0, d // together, residues, 0)

    shape = jax.ShapeDtypeStruct((T, 4 * HEAD_DIM), F32)
    return pl.pallas_call(
        body, grid=(parts, T // (DIL_BLOCK * d)), in_specs=[qs, kc, kp, vc, vp], out_specs=[out, out], out_shape=[shape, shape],
        compiler_params=_cparams(("parallel", "parallel")), name=name)(q, k, k, kv, kv)


def dil_bwd(q, k, kv, do, dl, gi, d, name):
    T = q.shape[0]
    (qs, kc, kp, vc, vp, out), parts, together = _dil_specs(gi, d)

    def body(q_ref, kc_ref, kp_ref, vc_ref, vp_ref, do_ref, dl_ref, *outs):
        f = functools.partial(_f_dilattn, pl.program_id(1) > 0)

        def residues(it, carry):
            rows = [_residue_rows(it * together + a, d) for a in range(together)]
            ins = [[ref[rw, :] for ref in (q_ref, kc_ref, kp_ref, vc_ref, vp_ref, do_ref, dl_ref)] for rw in rows]
            res = [jax.vjp(f, *x[:5])[1]((x[5], x[6])) for x in ins]
            for rw, gs in zip(rows, res):
                for o_ref, g in zip(outs, gs):
                    o_ref[rw, :] = g
            return carry

        lax.fori_loop(0, d // together, residues, 0)

    shape = jax.ShapeDtypeStruct((T, 4 * HEAD_DIM), F32)
    dq, dkc, dkp, dvc, dvp = pl.pallas_call(
        body, grid=(parts, T // (DIL_BLOCK * d)), in_specs=[qs, kc, kp, vc, vp, out, out], out_specs=[out] * 5, out_shape=[shape] * 5,
        compiler_params=_cparams(("parallel", "parallel")), name=name)(q, k, k, kv, kv, do, dl)

    def own_plus_next(c, p):
        return c + jnp.concatenate([p[DIL_BLOCK * d:], jnp.zeros_like(p[:DIL_BLOCK * d])], axis=0)

    return dq, own_plus_next(dkc, dkp), own_plus_next(dvc, dvp)


CONV_TILE = 256


def _conv3(before, u, w, b):
    ue = jnp.concatenate([before, u], axis=0)
    s1, s2 = pltpu.roll(ue, 1, 0)[8:], pltpu.roll(ue, 2, 0)[8:]
    return b + w[0:1] * s2 + w[1:2] * s1 + w[2:3] * u, s1, s2


def _conv_halves(u_ref, h_ref, cw_ref, cb_ref):
    F = D_FF
    res = []
    for lo in (0, F):
        before = jnp.where(pl.program_id(0) > 0, h_ref[:, lo:lo + F], 0.0)
        u = u_ref[:, lo:lo + F]
        res.append((u,) + _conv3(before, u, cw_ref[:, lo:lo + F], cb_ref[:, lo:lo + F]))
    return res


def _halo_before(C):
    return pl.BlockSpec((8, C), lambda i: (jnp.maximum(i * (CONV_TILE // 8) - 1, 0), 0))


def convgate_fwd(u, cw, cb, name):
    T, C = u.shape
    F = C // 2

    def body(u_ref, h_ref, cw_ref, cb_ref, z_ref):
        (_, cg, _, _), (_, cv, _, _) = _conv_halves(u_ref, h_ref, cw_ref, cb_ref)
        z_ref[...] = (cg * _sigmoid(cg) * cv).astype(BF16)

    return pl.pallas_call(
        body, grid=(T // CONV_TILE,),
        in_specs=[pl.BlockSpec((CONV_TILE, C), lambda i: (i, 0)), _halo_before(C), _full_spec(cw), _full_spec(cb)],
        out_specs=pl.BlockSpec((CONV_TILE, F), lambda i: (i, 0)), out_shape=jax.ShapeDtypeStruct((T, F), BF16),
        compiler_params=_cparams(("parallel",)), name=name)(u, u, cw, cb)


def convgate_bwd(u, cw, cb, dz, name):
    T, C = u.shape
    F = C // 2
    n = T // CONV_TILE
    E = CONV_TILE + 8

    def body(u_ref, hb_ref, ha_ref, cw_ref, cb_ref, dz_ref, dza_ref, du_ref, dcw_ref, dcb_ref):
        i = pl.program_id(0)
        dze = jnp.concatenate([dz_ref[...], jnp.where(i < n - 1, dza_ref[...], 0.0)], axis=0)

        @pl.when(i == 0)
        def _():
            dcw_ref[...] = jnp.zeros_like(dcw_ref)
            dcb_ref[...] = jnp.zeros_like(dcb_ref)

        halves = []
        for lo in (0, F):
            sl = slice(lo, lo + F)
            ue = jnp.concatenate([u_ref[:, sl], ha_ref[:, sl]], axis=0)
            c, s1, s2 = _conv3(jnp.where(i > 0, hb_ref[:, sl], 0.0), ue, cw_ref[:, sl], cb_ref[:, sl])
            halves.append((sl, ue, c, s1, s2))
        (_, _, cg, _, _), (_, _, cv, _, _) = halves
        sg = _sigmoid(cg)
        dcs = (dze * cv * sg * (1.0 + cg * (1.0 - sg)), dze * cg * sg)
        for (sl, ue, _, s1, s2), dc in zip(halves, dcs):
            own = lambda z: z[:CONV_TILE]
            dcb_ref[:, sl] += jnp.sum(own(dc), axis=0, keepdims=True)
            dcw_ref[0:1, sl] += jnp.sum(own(dc * s2), axis=0, keepdims=True)
            dcw_ref[1:2, sl] += jnp.sum(own(dc * s1), axis=0, keepdims=True)
            dcw_ref[2:3, sl] += jnp.sum(own(dc * ue), axis=0, keepdims=True)
            du = cw_ref[2:3, sl] * dc + cw_ref[1:2, sl] * pltpu.roll(dc, E - 1, 0) + cw_ref[0:1, sl] * pltpu.roll(dc, E - 2, 0)
            du_ref[:, sl] = own(du).astype(BF16)

    after = lambda w: pl.BlockSpec((8, w), lambda i: (jnp.minimum((i + 1) * (CONV_TILE // 8), T // 8 - 1), 0))
    return pl.pallas_call(
        body, grid=(n,),
        in_specs=[pl.BlockSpec((CONV_TILE, C), lambda i: (i, 0)), _halo_before(C), after(C), _full_spec(cw), _full_spec(cb),
                  pl.BlockSpec((CONV_TILE, F), lambda i: (i, 0)), after(F)],
        out_specs=[pl.BlockSpec((CONV_TILE, C), lambda i: (i, 0)), _full_spec(cw), _full_spec(cb)],
        out_shape=[jax.ShapeDtypeStruct((T, C), BF16), jax.ShapeDtypeStruct(cw.shape, F32), jax.ShapeDtypeStruct(cb.shape, F32)],
        compiler_params=_cparams(("arbitrary",)), name=name)(u, u, u, cw, cb, dz, dz)


def loss_head(y, tgt):
    T, D = y.shape
    tile = ROW_TILE

    def body(y_ref, t_ref, l_ref, d_ref, db_ref):
        d = y_ref[...] - t_ref[...]
        d_ref[...] = d * (1.0 / D)
        db_ref[...] = (d * (1.0 / D)).astype(BF16)

        @pl.when(pl.program_id(0) == 0)
        def _():
            l_ref[...] = jnp.zeros_like(l_ref)

        l_ref[...] += (0.5 / D) * jnp.sum(d * d)

    row = pl.BlockSpec((tile, D), lambda i: (i, 0))
    return pl.pallas_call(
        body, grid=(T // tile,), in_specs=[row, row], out_specs=[pl.BlockSpec((8, 128), lambda i: (0, 0)), row, row],
        out_shape=[jax.ShapeDtypeStruct((8, 128), F32), jax.ShapeDtypeStruct((T, D), F32), jax.ShapeDtypeStruct((T, D), BF16)],
        compiler_params=_cparams(("arbitrary",)), name="loss_head")(y, tgt)


def sum_parts(parts, name):
    S, R, C = parts.shape
    tile = _pick(R, (512, 256, 128, 64, 32, 16, 8))

    def body(p_ref, o_ref):
        acc = p_ref[0].astype(F32)
        for s in range(1, S):
            acc = acc + p_ref[s].astype(F32)
        o_ref[...] = acc

    return pl.pallas_call(
        body, grid=(R // tile,), in_specs=[pl.BlockSpec((S, tile, C), lambda i: (0, i, 0))],
        out_specs=pl.BlockSpec((tile, C), lambda i: (i, 0)), out_shape=jax.ShapeDtypeStruct((R, C), F32),
        compiler_params=_cparams(("parallel",)), name=name)(parts)


def adamw(gparts, w, m, v, name):
    S, R, C = gparts.shape
    tile = _pick(R, (512, 256, 128, 64, 32, 16, 8))
    c1 = 1.0 / (1.0 - ADAM_B1 ** ADAM_STEP)
    c2 = 1.0 / (1.0 - ADAM_B2 ** ADAM_STEP)

    def body(g_ref, w_ref, m_ref, v_ref, go_ref, d_ref, mo_ref, vo_ref):
        g = g_ref[0].astype(F32)
        for s in range(1, S):
            g = g + g_ref[s].astype(F32)
        m1 = ADAM_B1 * m_ref[...] + (1.0 - ADAM_B1) * g
        v1 = ADAM_B2 * v_ref[...] + (1.0 - ADAM_B2) * (g * g)
        go_ref[...] = g
        mo_ref[...] = m1
        vo_ref[...] = v1
        d_ref[...] = -ADAM_LR * ((m1 * c1) / (jnp.sqrt(v1 * c2) + ADAM_EPS) + ADAM_WD * w_ref[...])

    row = pl.BlockSpec((tile, C), lambda i: (i, 0))
    return pl.pallas_call(
        body, grid=(R // tile,), in_specs=[pl.BlockSpec((S, tile, C), lambda i: (0, i, 0)), row, row, row],
        out_specs=[row] * 4, out_shape=[jax.ShapeDtypeStruct((R, C), F32)] * 4,
        compiler_params=_cparams(("parallel",)), name=name)(gparts, w, m, v)


def _peers():
    x, y, c = lax.axis_index("x"), lax.axis_index("y"), lax.axis_index("c")
    peers = []
    for k in range(1, N_DEV):
        px = 1 - x if k & 4 else x
        py = 1 - y if k & 2 else y
        pc = 1 - c if k & 1 else c
        peers.append(((px, py, pc), 4 * px + 2 * py + pc))
    return 4 * x + 2 * y + c, peers


_ANY = pl.BlockSpec(memory_space=pl.ANY)


class Exchange:
    def __init__(self, gathers=(), scatters=()):
        self.gathers, self.scatters = list(gathers), list(scatters)
        self.n = len(self.gathers) + len(self.scatters)

    def operands(self):
        return self.gathers + self.scatters

    def out_shape(self):
        return ([jax.ShapeDtypeStruct((N_DEV,) + x.shape, x.dtype) for x in self.gathers]
                + [jax.ShapeDtypeStruct(x.shape, x.dtype) for x in self.scatters])

    def scratch(self):
        n = max(self.n, 1)
        return [pltpu.SemaphoreType.DMA((7 * n,)), pltpu.SemaphoreType.DMA((7 * n,)), pltpu.SemaphoreType.DMA((n,))]

    def _copies(self, in_refs, out_refs, send_sems, recv_sems, local_sems):
        me, peers = _peers()
        ng = len(self.gathers)
        local, sends, recvs = [], [], []
        for a in range(self.n):
            x, o = in_refs[a], out_refs[a]
            mine = x if a < ng else x.at[me]
            local.append(pltpu.make_async_copy(mine, o.at[me], local_sems.at[a]))
            s_a, r_a = {}, {}
            for k in range(1, N_DEV):
                peer, slot = peers[k - 1]
                sems = dict(send_sem=send_sems.at[7 * a + k - 1], recv_sem=recv_sems.at[7 * a + k - 1],
                            device_id_type=pl.DeviceIdType.MESH)
                if a >= ng:
                    s_a[k] = pltpu.make_async_remote_copy(src_ref=x.at[slot], dst_ref=o.at[me], device_id=peer, **sems)
                elif k in FORWARDED:
                    came = o.at[peers[k - 2][1]]
                    s_a[k] = pltpu.make_async_remote_copy(src_ref=came, dst_ref=came, device_id=peers[0][0], **sems)
                else:
                    s_a[k] = pltpu.make_async_remote_copy(src_ref=x, dst_ref=o.at[me], device_id=peer, **sems)
                r_a[k] = pltpu.make_async_remote_copy(src_ref=mine, dst_ref=o.at[slot], device_id=peer, **sems)
            sends.append(s_a)
            recvs.append(r_a)
        return local, sends, recvs

    def start(self, *refs):
        if self.n == 0:
            return
        local, sends, _ = self._copies(*refs)
        for a in range(self.n):
            local[a].start()
            for k in range(1, N_DEV):
                if a >= len(self.gathers) or k not in FORWARDED:
                    sends[a][k].start()

    def forward(self, *refs):
        if not self.gathers:
            return
        _, sends, recvs = self._copies(*refs)
        for a in range(len(self.gathers)):
            for k in FORWARDED:
                recvs[a][k - 1].wait_recv()
                sends[a][k].start()

    def wait(self, *refs):
        if self.n == 0:
            return
        local, sends, recvs = self._copies(*refs)
        for a in range(self.n):
            waited_early = [f - 1 for f in FORWARDED] if a < len(self.gathers) else []
            for k in range(1, N_DEV):
                if k not in waited_early:
                    recvs[a][k].wait_recv()
            for k in range(1, N_DEV):
                sends[a][k].wait_send()
            local[a].wait()


FORWARDED = (3, 5, 7)


def exchange(ex, name):
    n = ex.n

    def body(*refs):
        args = (refs[:n], refs[n:2 * n]) + tuple(refs[2 * n:])
        ex.start(*args)
        ex.forward(*args)
        ex.wait(*args)

    return pl.pallas_call(body, in_specs=[_ANY] * n, out_specs=[_ANY] * n, out_shape=ex.out_shape(),
                          scratch_shapes=ex.scratch(), name=name)(*ex.operands())


def _shift_up(z):
    return jnp.concatenate([z[1:], jnp.zeros_like(z[:1])], axis=0)


def _segments(width):
    seg = np.zeros((width, 128), np.float32)
    seg[np.arange(width), np.arange(width) // HEAD_DIM] = 1.0
    return jnp.asarray(seg), jnp.asarray(seg.T)


def _rope_consts(T):
    inv = ROPE_THETA ** (-jnp.arange(0, HEAD_DIM, 2, dtype=F32) / HEAD_DIM)
    ang = jnp.arange(T, dtype=F32)[:, None] * inv[None, :]
    return jnp.tile(jnp.cos(ang), (1, 4)), jnp.tile(jnp.sin(ang), (1, 4))


def _per_head(g, heads):
    return jnp.tile(g.reshape(1, HEAD_DIM), (1, heads))


def _sum_heads(g):
    return g.reshape(-1, HEAD_DIM).sum(axis=0, keepdims=True)


LORA_COLS = 256
RW_TILE = 256
ROW_TILE = 512
NORM_TILE = 1024


def _local_step(x0, memx, tgt, P, ex_weights=None, weights_done=None, ex_grads=None):
    T = x0.shape[0]
    P = dict(P)
    G = {}
    seg, seg_t = _segments(RWKV_WIDTH)
    mseg = (seg[:MEM_WIDTH], seg_t[:, :MEM_WIDTH])
    cos, sin = _rope_consts(T)
    row = lambda v: v.reshape(1, -1)

    def mem_fwd(i, q, into):
        memn = stage_fwd(f_rmsnorm, [memx], [P["mem_norm"][i:i + 1]], [], [], N_MEM, f"mem{i}_norm", [BF16])[0]
        kvm = matmul(memn, P["mem_w_kv"][i], "nn", f"mem{i}_kv")
        kn, qn = _per_head(P["mem_k_norm"][i], MEM_HEADS), _per_head(P["mem_q_norm"][i], MEM_HEADS)
        km = stage_fwd(f_headnorm, [Cols(kvm, MEM_WIDTH, 0)], [kn], [], mseg, N_MEM, f"mem{i}_knorm")[0]
        om = stage_fwd(f_memattn, [q], [km, Cols(kvm, MEM_WIDTH, 1), qn], [], mseg, ROW_TILE, f"mem{i}_attn", [BF16], into=into)[0]
        return om, (memn, kvm, km, kn, qn, q)

    def mem_bwd(i, saved, dymem, copy_into=None):
        memn, kvm, km, kn, qn, q = saved
        (dq,), (dkm, dvm, g_qn), *copy = stage_bwd(f_memattn, [q], [km, Cols(kvm, MEM_WIDTH, 1), qn], [], mseg, [dymem], ROW_TILE,
                                                   f"mem{i}_attn_bwd", bf16_copies=(0,) if copy_into else (), copy_into=copy_into)
        dq = copy[0][0] if copy_into else dq
        (dkraw,), (g_kn,) = stage_bwd(f_headnorm, [Cols(kvm, MEM_WIDTH, 0)], [kn], [], mseg, [dkm], N_MEM, f"mem{i}_knorm_bwd")
        dkvm = jnp.concatenate([dkraw, dvm], axis=1).astype(BF16)
        g_w = matmul(memn, dkvm, "tn", f"mem{i}_kv_dw")
        dmemn = matmul(dkvm, P["mem_w_kv"][i], "nt", f"mem{i}_kv_dx")
        _, (g_mn,) = stage_bwd(f_rmsnorm, [memx], [P["mem_norm"][i:i + 1]], [], [], [dmemn], N_MEM, f"mem{i}_norm_bwd")
        return dq, g_mn, g_w, _sum_heads(g_qn), _sum_heads(g_kn)

    def ffn_fwd(i, xin):
        hn = stage_fwd(f_rmsnorm, [xin], [P["ffn_norm"][i:i + 1]], [], [], NORM_TILE, f"ffn{i}_norm", [BF16])[0]
        u = matmul(hn, P["ffn_w_up"][i], "nt", f"ffn{i}_up")
        z = convgate_fwd(u, P["ffn_conv_w"][i], P["ffn_conv_b"][i:i + 1], f"ffn{i}_conv")
        return matmul(z, P["ffn_w_down"][i], "nn", f"ffn{i}_down", residual=xin), (hn, u, z)

    def ffn_bwd(i, xin, saved, dxo, dxo_b):
        hn, u, z = saved
        dz = matmul(dxo_b, P["ffn_w_down"][i], "nt", f"ffn{i}_down_dx")
        g_down = matmul(z, dxo_b, "tn", f"ffn{i}_down_dw")
        du, g_cw, g_cb = convgate_bwd(u, P["ffn_conv_w"][i], P["ffn_conv_b"][i:i + 1], dz, f"ffn{i}_conv_bwd")
        dhn = matmul(du, P["ffn_w_up"][i], "nn", f"ffn{i}_up_dx")
        g_up = matmul(du, hn, "tn", f"ffn{i}_up_dw")
        (dxin,), (g_n,), (dxin_b,) = stage_bwd(f_rmsnorm_res, [xin], [P["ffn_norm"][i:i + 1]], [], [], [dhn, dxo], ROW_TILE,
                                               f"ffn{i}_norm_bwd", bf16_copies=(0,))
        return dxin, dxin_b, g_n, g_up, g_cw, g_cb, g_down

    h0 = stage_fwd(f_rmsnorm, [x0], [P["attn_norm"][0:1]], [], [], NORM_TILE, "l0_norm", [BF16])[0]
    p0 = matmul(h0, P["a_w_in"][0], "nt", "l0_in")
    lora0 = 3 * RWKV_WIDTH // LORA_COLS
    pre_xs = [Cols(p0, RWKV_WIDTH, 0), Cols(p0, RWKV_WIDTH, 1), Cols(p0, RWKV_WIDTH, 2), Cols(p0, LORA_COLS, lora0)]
    mu = [Cols(P["a_mu"], RWKV_WIDTH, 0), Cols(P["a_mu"], RWKV_WIDTH, 1), Cols(P["a_mu"], RWKV_WIDTH, 2),
          Cols(P["a_mu"], LORA_COLS, lora0)]
    lora_rows = lambda w, lo: jnp.pad(w, ((lo, LORA_COLS - lo - w.shape[0]), (0, 0)))
    pre_ps = mu + [P["a_w0"], lora_rows(P["a_w2"][0], 0), P["a_a0"], lora_rows(P["a_a2"][0], 64), lora_rows(P["a_g2"][0], 128),
                   P["a_k_k"], P["a_k_a"]]
    r, lw, k2, v, kk, b, g = stage_fwd(f_rwkv_pre, pre_xs, pre_ps, [], [seg, seg_t], RW_TILE, "l0_rwkv_pre", with_prev=True)
    scan_in = [r, lw, k2, v, kk, b]
    y_h, h_states, got = rwkv_scan_fwd(*scan_in, ex_weights or Exchange())
    if weights_done is not None:
        P.update(weights_done(got))
    y_s = y_h
    post_ps = [P["a_lnx_w"], P["a_lnx_b"], P["a_r_k"].reshape(1, RWKV_WIDTH)]
    ycat0, mem0_saved = mem_fwd(0, Cols(p0, MEM_WIDTH, SHIFT_WIDTH // MEM_WIDTH), Into(None, D_MODEL, RWKV_WIDTH // MEM_WIDTH))
    ycat0 = stage_fwd(f_rwkv_post, [y_s, r, k2, v, g], post_ps, [], [seg, seg_t], RW_TILE, "l0_rwkv_post", [BF16],
                      into=Into(ycat0, D_MODEL, 0))[0]
    x1 = matmul(ycat0, P["a_w_out"][0], "nn", "l0_out", residual=x0)
    x2, ffn0_saved = ffn_fwd(0, x1)

    hk, h1 = stage_fwd(f_rmsnorm2, [x2], [row(P["kv_norm"]), P["attn_norm"][1:2]], [], [], NORM_TILE, "l1_norm", [BF16, BF16])
    kvp = matmul(hk, P["kv_w"][0], "nt", "l1_kv")
    p1 = matmul(h1, P["b_w_in"][0], "nn", "l1_in")
    kraw, qraw = Cols(kvp, DIL_WIDTH, 0), Cols(p1, DIL_WIDTH, 0)
    kgain, qgain = _per_head(P["kv_k_norm"], DIL_WIDTH // HEAD_DIM), _per_head(P["b_q_norm"], DIL_WIDTH // HEAD_DIM)
    ksh = stage_fwd(f_qkprep, [kraw], [kgain], [cos, sin], [seg, seg_t], ROW_TILE, "l1_kprep")[0]
    q = stage_fwd(f_qkprep, [qraw], [qgain], [cos, sin], [seg, seg_t], ROW_TILE, "l1_qprep")[0]
    outs, lses = [], []
    for gi, (_, d) in enumerate(DIL_GROUPS):
        og, lg = dil_fwd(q, ksh, kvp, gi, d, f"l1_dil{gi}")
        outs.append(og)
        lses.append(lg)
    ycat1 = stage_fwd(f_mix, outs + lses, [], [], [], ROW_TILE, "l1_mix", [BF16], into=Into(None, 2 * MEM_WIDTH, 0))[0]
    ycat1, mem1_saved = mem_fwd(1, Cols(p1, MEM_WIDTH, DIL_WIDTH // MEM_WIDTH), Into(ycat1, 2 * MEM_WIDTH, 1))
    x3 = matmul(ycat1, P["b_w_out"][0], "nt", "l1_out", residual=x2)
    x4, ffn1_saved = ffn_fwd(1, x3)
    loss_part, dx4, dx4_b = loss_head(x4, tgt)

    dx3, dx3_b, gn1, gup1, gcw1, gcb1, gdown1 = ffn_bwd(1, x3, ffn1_saved, dx4, dx4_b)
    dycat1 = matmul(dx3_b, P["b_w_out"][0], "nn", "l1_out_dx")
    G["b_w_out"] = [matmul(dx3_b, ycat1, "tn", "l1_out_dw")]
    dp1, gmn1, gmw1, gmq1, gmk1 = mem_bwd(1, mem1_saved, Cols(dycat1, MEM_WIDTH, 1), Into(None, D_MODEL, DIL_WIDTH // MEM_WIDTH))
    dmix, _ = stage_bwd(f_mix, outs + lses, [], [], [], [Cols(dycat1, MEM_WIDTH, 0)], ROW_TILE, "l1_mix_bwd")
    dq, dk, dv = zip(*[dil_bwd(q, ksh, kvp, dmix[gi], dmix[3 + gi], gi, d, f"l1_dil{gi}_bwd")
                       for gi, (_, d) in enumerate(DIL_GROUPS)])
    dq, dk, dv = jnp.concatenate(dq, axis=1), jnp.concatenate(dk, axis=1), jnp.concatenate(dv, axis=1)
    _, (g_bq,), (dp1,) = stage_bwd(f_qkprep, [qraw], [qgain], [cos, sin], [seg, seg_t], [dq], ROW_TILE, "l1_qprep_bwd",
                                   bf16_copies=(0,), copy_into=Into(dp1, D_MODEL, 0))
    dkvp = jnp.pad(dv.astype(BF16), ((0, 0), (DIL_WIDTH, 0)))
    _, (g_kk,), (dkvp,) = stage_bwd(f_qkprep, [kraw], [kgain], [cos, sin], [seg, seg_t], [dk], ROW_TILE, "l1_kprep_bwd",
                                    bf16_copies=(0,), copy_into=Into(dkvp, 2 * DIL_WIDTH, 0))
    g_bq, g_kk = _sum_heads(g_bq), _sum_heads(g_kk)
    dh1 =matmul(dp1, P["b_w_in"][0], "nt", "l1_in_dx")
    G["b_w_in"] = [matmul(h1, dp1, "tn", "l1_in_dw")]
    dhk = matmul(dkvp, P["kv_w"][0], "nn", "l1_kv_dx")
    G["kv_w"] = [matmul(dkvp, hk, "tn", "l1_kv_dw")]
    (dx2,), (g_kvn, g_an1), (dx2_b,) = stage_bwd(f_rmsnorm2_res, [x2], [row(P["kv_norm"]), P["attn_norm"][1:2]], [], [],
                                                 [dhk, dh1, dx3], ROW_TILE, "l1_norm_bwd", bf16_copies=(0,))

    dx1, dx1_b, gn0, gup0, gcw0, gcb0, gdown0 = ffn_bwd(0, x1, ffn0_saved, dx2, dx2_b)
    dycat0 = matmul(dx1_b, P["a_w_out"][0], "nt", "l0_out_dx")
    G["a_w_out"] = [matmul(ycat0, dx1_b, "tn", "l0_out_dw")]
    dqmem0, gmn0, gmw0, gmq0, gmk0 = mem_bwd(0, mem0_saved, Cols(dycat0, MEM_WIDTH, RWKV_WIDTH // MEM_WIDTH))
    (dy_s, dr_a, dk_a, dv_a, dg), (g_lw, g_lb, g_rk) = stage_bwd(
        f_rwkv_post, [y_s, r, k2, v, g], post_ps, [], [seg, seg_t], [Cols(dycat0, RWKV_WIDTH, 0)], RW_TILE, "l0_rwkv_post_bwd")
    G["mem_w_kv"], G["ffn_w_up"], G["ffn_w_down"] = [gmw0, gmw1], [gup0, gup1], [gdown0, gdown1]
    (dr_b, dlw, dk_b, dv_b, dkk, db), G["_exchanged"] = rwkv_scan_bwd(*scan_in, h_states, dy_s,
                                                                      ex_grads(G) if ex_grads else Exchange())
    dpre, gpre = stage_bwd(f_rwkv_pre, pre_xs, pre_ps, [], [seg, seg_t],
                           [[dr_a, dr_b], dlw, [dk_a, dk_b], [dv_a, dv_b], dkk, db, dg], RW_TILE, "l0_rwkv_pre_bwd", with_prev=True)
    dp_rw = jnp.concatenate(dpre[:4], axis=1) + _shift_up(jnp.concatenate(dpre[4:], axis=1))
    dp0 = jnp.concatenate([dp_rw, dqmem0], axis=1).astype(BF16)
    dh0 = matmul(dp0, P["a_w_in"][0], "nn", "l0_in_dx")
    G["a_w_in"] = [matmul(dp0, h0, "tn", "l0_in_dw")]
    (dx0,), (g_an0,) = stage_bwd(f_rmsnorm_res, [x0], [P["attn_norm"][0:1]], [], [], [dh0, dx1], ROW_TILE, "l0_norm_bwd")

    G["attn_norm"] = jnp.concatenate([g_an0, g_an1], axis=0)
    G["a_mu"] = jnp.concatenate(gpre[:4], axis=1)
    G["a_w0"], G["a_w2"], G["a_a0"], G["a_a2"], G["a_g2"] = gpre[4], gpre[5][None, :64], gpre[6], gpre[7][None, 64:128], gpre[8][None, 128:]
    G["a_k_k"], G["a_k_a"] = gpre[9], gpre[10]
    G["a_r_k"] = g_rk.reshape(1, RWKV_HEADS, HEAD_DIM)
    G["a_lnx_w"], G["a_lnx_b"] = g_lw, g_lb
    G["kv_norm"], G["kv_k_norm"], G["b_q_norm"] = g_kvn.reshape(-1), g_kk.reshape(-1), g_bq
    G["mem_norm"] = jnp.concatenate([gmn0, gmn1], axis=0)
    G["mem_w_kv"] = [gmw0, gmw1]
    G["mem_q_norm"] = jnp.concatenate([gmq0, gmq1], axis=0)
    G["mem_k_norm"] = jnp.concatenate([gmk0, gmk1], axis=0)
    G["ffn_norm"] = jnp.concatenate([gn0, gn1], axis=0)
    G["ffn_w_up"] = [gup0, gup1]
    G["ffn_conv_w"] = jnp.stack([gcw0, gcw1])
    G["ffn_conv_b"] = jnp.concatenate([gcb0, gcb1], axis=0)
    G["ffn_w_down"] = [gdown0, gdown1]
    return loss_part, dx0, G


PARAMS = (("attn_norm", None), ("a_w_in", 2), ("a_mu", 1), ("a_w0", 1), ("a_w2", 2), ("a_a0", 1), ("a_a2", 2), ("a_g2", 2),
          ("a_k_k", 1), ("a_k_a", 1), ("a_r_k", None), ("a_lnx_w", 1), ("a_lnx_b", 1), ("a_w_out", 1), ("kv_norm", None),
          ("kv_w", 1), ("kv_k_norm", None), ("b_w_in", 1), ("b_q_norm", None), ("b_w_out", 2), ("mem_norm", None),
          ("mem_w_kv", 1), ("mem_q_norm", None), ("mem_k_norm", None), ("ffn_norm", None), ("ffn_w_up", 2),
          ("ffn_conv_w", 2), ("ffn_conv_b", None), ("ffn_w_down", 1))
BIG = ("a_w_in", "a_w_out", "kv_w", "b_w_in", "b_w_out", "mem_w_kv", "ffn_w_up", "ffn_w_down")
TRANSPOSED = ("a_w_in", "kv_w", "b_w_out", "ffn_w_up")
AXIS = dict(PARAMS)
SMALL = tuple(n for n, _ in PARAMS if n not in BIG)
SMALL_SHARDED = tuple(n for n in SMALL if AXIS[n] is not None)
PACK_QUANTUM = 256 * 128


def _from_shards(xs, axis):
    full = jnp.moveaxis(xs, 0, axis)
    sh = full.shape
    return full.reshape(sh[:axis] + (sh[axis] * sh[axis + 1],) + sh[axis + 2:])


def _to_shards(g, axis):
    sh = g.shape
    return jnp.moveaxis(g.reshape(sh[:axis] + (N_DEV, sh[axis] // N_DEV) + sh[axis + 1:]), axis, 0)


def _pack(parts, lead=0):
    ld = parts[0].shape[:lead]
    flat = jnp.concatenate([p.reshape(ld + (-1,)) for p in parts], axis=-1)
    pad = (-flat.shape[-1]) % PACK_QUANTUM
    flat = jnp.pad(flat, [(0, 0)] * lead + [(0, pad)])
    return flat.reshape(ld + (-1, 128))


def _unpack(packed, shapes, lead=0):
    ld = packed.shape[:lead]
    flat = packed.reshape(ld + (-1,))
    out, off = [], 0
    for s in shapes:
        n = math.prod(s)
        out.append(flat[..., off:off + n].reshape(ld + tuple(s)))
        off += n
    return out


def kernel(x, mem, attn_norm, a_w_in, a_mu, a_w0, a_w2, a_a0, a_a2, a_g2, a_k_k, a_k_a, a_r_k, a_lnx_w, a_lnx_b, a_w_out, kv_norm, kv_w, kv_k_norm, b_w_in, b_q_norm, b_w_out, mem_norm, mem_w_kv, mem_q_norm, mem_k_norm, ffn_norm, ffn_w_up, ffn_conv_w, ffn_conv_b, ffn_w_down, loss_target, m_attn_norm, m_a_w_in, m_a_mu, m_a_w0, m_a_w2, m_a_a0, m_a_a2, m_a_g2, m_a_k_k, m_a_k_a, m_a_r_k, m_a_lnx_w, m_a_lnx_b, m_a_w_out, m_kv_norm, m_kv_w, m_kv_k_norm, m_b_w_in, m_b_q_norm, m_b_w_out, m_mem_norm, m_mem_w_kv, m_mem_q_norm, m_mem_k_norm, m_ffn_norm, m_ffn_w_up, m_ffn_conv_w, m_ffn_conv_b, m_ffn_w_down, v_attn_norm, v_a_w_in, v_a_mu, v_a_w0, v_a_w2, v_a_a0, v_a_a2, v_a_g2, v_a_k_k, v_a_k_a, v_a_r_k, v_a_lnx_w, v_a_lnx_b, v_a_w_out, v_kv_norm, v_kv_w, v_kv_k_norm, v_b_w_in, v_b_q_norm, v_b_w_out, v_mem_norm, v_mem_w_kv, v_mem_q_norm, v_mem_k_norm, v_ffn_norm, v_ffn_w_up, v_ffn_conv_w, v_ffn_conv_b, v_ffn_w_down):
    names = [n for n, _ in PARAMS]
    vals = (attn_norm, a_w_in, a_mu, a_w0, a_w2, a_a0, a_a2, a_g2, a_k_k, a_k_a, a_r_k, a_lnx_w, a_lnx_b, a_w_out, kv_norm, kv_w, kv_k_norm, b_w_in, b_q_norm, b_w_out, mem_norm, mem_w_kv, mem_q_norm, mem_k_norm, ffn_norm, ffn_w_up, ffn_conv_w, ffn_conv_b, ffn_w_down)
    m_vals = (m_attn_norm, m_a_w_in, m_a_mu, m_a_w0, m_a_w2, m_a_a0, m_a_a2, m_a_g2, m_a_k_k, m_a_k_a, m_a_r_k, m_a_lnx_w, m_a_lnx_b, m_a_w_out, m_kv_norm, m_kv_w, m_kv_k_norm, m_b_w_in, m_b_q_norm, m_b_w_out, m_mem_norm, m_mem_w_kv, m_mem_q_norm, m_mem_k_norm, m_ffn_norm, m_ffn_w_up, m_ffn_conv_w, m_ffn_conv_b, m_ffn_w_down)
    v_vals = (v_attn_norm, v_a_w_in, v_a_mu, v_a_w0, v_a_w2, v_a_a0, v_a_a2, v_a_g2, v_a_k_k, v_a_k_a, v_a_r_k, v_a_lnx_w, v_a_lnx_b, v_a_w_out, v_kv_norm, v_kv_w, v_kv_k_norm, v_b_w_in, v_b_q_norm, v_b_w_out, v_mem_norm, v_mem_w_kv, v_mem_q_norm, v_mem_k_norm, v_ffn_norm, v_ffn_w_up, v_ffn_conv_w, v_ffn_conv_b, v_ffn_w_down)
    W, M, V = dict(zip(names, vals)), dict(zip(names, m_vals)), dict(zip(names, v_vals))
    layers = lambda D, n: [D[n]] if D[n].ndim == 2 else [D[n][i] for i in range(D[n].shape[0])]
    ax2 = lambda n: AXIS[n] - (W[n].ndim - 2)
    later =[(n, i) for n in BIG if n != "a_w_in" for i in range(len(layers(W, n)))]

    sent = lambda n, w: w.T if n in TRANSPOSED else w
    whole = lambda n, g: g.reshape(-1, g.shape[-1]) if n in TRANSPOSED else _from_shards(g, ax2(n))
    small_shapes = [W[n].shape for n in SMALL_SHARDED]
    got_w, got_small = exchange(Exchange(gathers=[sent("a_w_in", W["a_w_in"][0]).astype(BF16),
                                                  _pack([W[n] for n in SMALL_SHARDED])]), "gather_first")
    P = {n: W[n] for n in SMALL}
    P["a_w_in"] = [whole("a_w_in", got_w)]
    for n, s in zip(SMALL_SHARDED, _unpack(got_small, small_shapes, lead=1)):
        P[n] = _from_shards(s, AXIS[n])
    ex_weights =Exchange(gathers=[sent(n, layers(W, n)[i]).astype(BF16) for n, i in later])

    def weights_done(got):
        out = {}
        for (n, _), g in zip(later, got):
            out.setdefault(n, []).append(whole(n, g))
        return out

    slots = lambda G, n: jnp.stack([_to_shards(g, 0 if n in TRANSPOSED else ax2(n)) for g in G[n]], axis=1)
    later_names = [n for n in BIG if n != "a_w_in"]
    ex_grads = lambda G: Exchange(scatters=[slots(G, n) for n in later_names])
    loss_part, dx0, G = _local_step(x[0], mem[0], loss_target[0], P, ex_weights, weights_done, ex_grads)
    gparts = dict(zip(later_names, G.pop("_exchanged")))
    replicated = [n for n in SMALL if AXIS[n] is None]
    small_slots = _pack([_to_shards(G[n], AXIS[n]) for n in SMALL_SHARDED], lead=1)
    got_rep, gparts["a_w_in"], got_sharded = exchange(
        Exchange(gathers=[_pack([G[n] for n in replicated] + [loss_part[0:1, 0:1]])], scatters=[slots(G, "a_w_in"), small_slots]),
        "exchange_last")

    results = {}
    for n in BIG:
        rows = lambda z: z.reshape((-1,) + z.shape[-1:])
        gp = gparts[n].reshape((N_DEV, -1) + gparts[n].shape[-1:])
        if n in TRANSPOSED:
            gp = jnp.swapaxes(sum_parts(gp, f"sum_{n}").reshape(gparts[n].shape[1:]), -1, -2).reshape((1,) + rows(W[n]).shape)
        res = adamw(gp, rows(W[n]), rows(M[n]), rows(V[n]), f"adamw_{n}")
        results[n] = [r.reshape(W[n].shape) for r in res]
    *rep_sums, loss = _unpack(sum_parts(got_rep, "sum_replicated_grads"), [W[n].shape for n in replicated] + [()])
    g_mine = dict(zip(replicated, rep_sums))
    g_mine.update(zip(SMALL_SHARDED, _unpack(sum_parts(got_sharded, "sum_small_sharded_grads"), [W[n].shape for n in SMALL_SHARDED])))
    res = adamw(_pack([g_mine[n] for n in SMALL])[None], _pack([W[n] for n in SMALL]), _pack([M[n] for n in SMALL]),
                _pack([V[n] for n in SMALL]), "adamw_small")
    for n, parts in zip(SMALL, zip(*[_unpack(r, [W[n].shape for n in SMALL]) for r in res])):
        results[n] = list(parts)
    outs = [[results[n][j] for n in names] for j in range(4)]
    return (loss, dx0[None], *outs[0], *outs[1], *outs[2], *outs[3])
```

```python
import functools
import math

import jax
import jax.numpy as jnp
import numpy as np
from jax import lax
from jax.experimental import pallas as pl
from jax.experimental.pallas import tpu as pltpu

F32 = jnp.float32
BF16 = jnp.bfloat16
H3 =lax.Precision.HIGH

N_DEV = 8
D_MODEL = 1024
HEAD_DIM = 64
N_MEM = 256
MEM_HEADS = 4
MEM_WIDTH = 256
RWKV_HEADS = 12
RWKV_WIDTH = 768
SHIFT_WIDTH = 2560
DIL_GROUPS = ((128, 1), (512, 4), (2048, 16))
DIL_BLOCK = 128
DIL_WIDTH = 768
D_FF = 2816
RMS_EPS = 1e-6
LNX_EPS = 64e-5
NEG_INF = -1e30
ROPE_THETA = 10000.0
ADAM_LR, ADAM_B1, ADAM_B2, ADAM_EPS, ADAM_WD, ADAM_STEP = 0.001, 0.9, 0.999, 1e-08, 0.01, 10

CHUNK = 64
SCAN_GROUPS_FWD, SCAN_GROUPS_BWD = 1, 1
MM_TILE_CAP = 1408
VMEM_LIMIT_V7X = 48 * 1024 * 1024


def _cparams(sem):
    return pltpu.CompilerParams(dimension_semantics=sem, vmem_limit_bytes=VMEM_LIMIT_V7X)


def _pick(n, cands):
    for c in cands:
        if n % c == 0:
            return c
    return n


def _tile(n, cap):
    if n <= cap:
        return n
    for d in range(cap - cap % 128, 0, -128):
        if n % d == 0:
            return d
    return n


def _dg(a, b, ca, cb, batch):
    dims = (((ca,), (cb,)), ((0,), (0,))) if batch else (((ca,), (cb,)), ((), ()))
    return lax.dot_general(a.astype(BF16), b.astype(BF16), dims, preferred_element_type=F32)


@jax.custom_vjp
def mm_nn(a, b):
    n = a.ndim
    return _dg(a, b, n - 1, n - 2, n == 3)


def _mm_nn_fwd(a, b):
    return mm_nn(a, b), (a, b)


def _mm_nn_bwd(res, g):
    a, b = res
    n = a.ndim
    return _dg(g, b, n - 1, n - 1, n == 3), _dg(a, g, n - 2, n - 2, n == 3)


mm_nn.defvjp(_mm_nn_fwd, _mm_nn_bwd)


@jax.custom_vjp
def mm_nt(a, b):
    n = a.ndim
    return _dg(a, b, n - 1, n - 1, n == 3)


def _mm_nt_fwd(a, b):
    return mm_nt(a, b), (a, b)


def _mm_nt_bwd(res, g):
    a, b = res
    n = a.ndim
    return _dg(g, b, n - 1, n - 2, n == 3), _dg(g, a, n - 2, n - 2, n == 3)


mm_nt.defvjp(_mm_nt_fwd, _mm_nt_bwd)


def mmh(a, b):
    n = a.ndim
    dims = (((n - 1,), (n - 2,)), ((0,), (0,))) if n == 3 else (((1,), (0,)), ((), ()))
    return lax.dot_general(a, b, dims, precision=H3, preferred_element_type=F32)


def mmh_nt(a, b):
    n = a.ndim
    dims = (((n - 1,), (n - 1,)), ((0,), (0,))) if n == 3 else (((1,), (1,)), ((), ()))
    return lax.dot_general(a, b, dims, precision=H3, preferred_element_type=F32)


def mmh_tn(a, b):
    n = a.ndim
    dims = (((n - 2,), (n - 2,)), ((0,), (0,))) if n == 3 else (((0,), (0,)), ((), ()))
    return lax.dot_general(a, b, dims, precision=H3, preferred_element_type=F32)


def matmul(a, b, mode, name, residual=None):
    out_dtype = BF16 if mode == "tn" else F32
    if mode == "nn":
        (M, K), (_, N) = a.shape, b.shape
    elif mode == "nt":
        (M, K), (N, _) = a.shape, b.shape
    else:
        (K, M), (_, N) = a.shape, b.shape
    tm = _tile(M, 2048 if mode == "nn" else MM_TILE_CAP)
    tn = _tile(N, 512 if mode == "nn" else MM_TILE_CAP)
    tk = _tile(K, MM_TILE_CAP)
    nk = K // tk
    if mode == "nn":
        a_spec = pl.BlockSpec((tm, tk), lambda i, j, k: (i, k))
        b_spec = pl.BlockSpec((tk, tn), lambda i, j, k: (k, j))
        dims = (((1,), (0,)), ((), ()))
    elif mode == "nt":
        a_spec = pl.BlockSpec((tm, tk), lambda i, j, k: (i, k))
        b_spec = pl.BlockSpec((tn, tk), lambda i, j, k: (j, k))
        dims = (((1,), (1,)), ((), ()))
    else:
        a_spec = pl.BlockSpec((tk, tm), lambda i, j, k: (k, i))
        b_spec = pl.BlockSpec((tk, tn), lambda i, j, k: (k, j))
        dims = (((0,), (0,)), ((), ()))
    o_spec = pl.BlockSpec((tm, tn), lambda i, j, k: (i, j))
    has_res = residual is not None

    def body(*refs):
        if has_res:
            a_ref, b_ref, r_ref, o_ref, acc_ref = refs
        else:
            a_ref, b_ref, o_ref, acc_ref = refs
        k = pl.program_id(2)

        @pl.when(k == 0)
        def _():
            acc_ref[...] = jnp.zeros_like(acc_ref)

        acc_ref[...] += lax.dot_general(a_ref[...].astype(BF16), b_ref[...].astype(BF16), dims,
                                        preferred_element_type=F32)

        @pl.when(k == nk - 1)
        def _():
            if has_res:
                o_ref[...] = (acc_ref[...] + r_ref[...]).astype(out_dtype)
            else:
                o_ref[...] = acc_ref[...].astype(out_dtype)

    ins = [a, b] + ([residual] if has_res else [])
    in_specs = [a_spec, b_spec] + ([o_spec] if has_res else [])
    return pl.pallas_call(
        body, grid=(M // tm, N // tn, nk), in_specs=in_specs, out_specs=o_spec,
        out_shape=jax.ShapeDtypeStruct((M, N), out_dtype), scratch_shapes=[pltpu.VMEM((tm, tn), F32)],
        compiler_params=_cparams(("parallel", "parallel", "arbitrary")), name=name)(*ins)


class Cols:
    def __init__(self, arr, width, idx):
        self.arr, self.width, self.idx = arr, width, idx


def _arr(x):
    return x.arr if isinstance(x, Cols) else x


def _shape(x):
    return x.arr.shape[:-1] + (x.width,) if isinstance(x, Cols) else x.shape


def _col(x):
    return x.idx if isinstance(x, Cols) else 0


def _tok_spec(x, tile):
    shape, col = _shape(x), _col(x)
    return pl.BlockSpec(shape[:-2] + (tile, shape[-1]), lambda i: (0,) * (len(shape) - 2) + (i, col))


def _full_spec(x):
    shape, col = _shape(x), _col(x)
    return pl.BlockSpec(shape, lambda i: (0,) * (len(shape) - 1) + (col,))


def _halo_spec(x, tile):
    shape, col = _shape(x), _col(x)
    return pl.BlockSpec((8, shape[-1]), lambda i: (jnp.maximum(i * (tile // 8) - 1, 0), col))


def _blk(x, tile):
    shape = _shape(x)
    return jax.ShapeDtypeStruct(shape[:-2] + (tile, shape[-1]), _arr(x).dtype)


def _prev_rows(x, halo):
    rows = lax.broadcasted_iota(jnp.int32, (x.shape[0], 1), 0)
    before = jnp.where(pl.program_id(0) > 0, halo[7:8], 0.0)
    return jnp.where(rows == 0, before, pltpu.roll(x, 1, 0))


class Into:
    def __init__(self, buf, total, idx):
        self.buf, self.total, self.idx = buf, total, idx

    def place(self, shape, tile):
        idx = self.idx
        return shape.update(shape=(shape.shape[0], self.total)), pl.BlockSpec((tile, shape.shape[1]), lambda i: (i, idx))

    def operand(self, n_in, n_out_index):
        if self.buf is None:
            return [], [], {}
        return [self.buf], [_ANY], {n_in: n_out_index}


def stage_fwd(f, xs, ps, cts, cfs, tile, name, out_dtypes=None, with_prev=False, into=None):
    xs, ps, cts, cfs = list(xs), list(ps), list(cts), list(cfs)
    halos = xs if with_prev else []
    nx, nh, nct, np_ = len(xs), len(halos), len(cts), len(ps)
    T = _shape(xs[0])[-2]
    blk = [_blk(x, tile) for x in xs]
    out_avals = jax.eval_shape(f, *blk, *(blk if with_prev else []), *[_blk(p, _shape(p)[-2]) for p in ps],
                               *[_blk(c, tile) for c in cts], *[_blk(c, _shape(c)[-2]) for c in cfs])
    if out_dtypes is None:
        out_dtypes = [o.dtype for o in out_avals]
    out_shape = [jax.ShapeDtypeStruct(o.shape[:-2] + (T, o.shape[-1]), dt) for o, dt in zip(out_avals, out_dtypes)]
    out_specs = [_tok_spec(o, tile) for o in out_shape]
    n_in = nx + nh + nct + np_ + len(cfs)
    extra, extra_specs, alias = [], [], {}
    if into is not None:
        out_shape[0], out_specs[0] = into.place(out_shape[0], tile)
        extra, extra_specs, alias = into.operand(n_in, 0)

    def body(*refs):
        vals = [r[...] for r in refs[:n_in]]
        xv, hv, rest = vals[:nx], vals[nx:nx + nh], vals[nx + nh:]
        ctv, pv, cfv = rest[:nct], rest[nct:nct + np_], rest[nct + np_:]
        prev = [_prev_rows(x, h) for x, h in zip(xv, hv)]
        res = f(*xv, *prev, *pv, *ctv, *cfv)
        for o_ref, r in zip(refs[n_in + len(extra):], res):
            o_ref[...] = r.astype(o_ref.dtype)

    return pl.pallas_call(
        body, grid=(T // tile,),
        in_specs=([_tok_spec(x, tile) for x in xs] + [_halo_spec(x, tile) for x in halos] + [_tok_spec(c, tile) for c in cts]
                  + [_full_spec(p) for p in ps + cfs] + extra_specs),
        out_specs=out_specs, out_shape=out_shape, input_output_aliases=alias,
        compiler_params=_cparams(("parallel",)), name=name)(*[_arr(a) for a in xs + halos + cts + ps + cfs], *extra)


def stage_bwd(f, xs, ps, cts, cfs, gs, tile, name, bf16_copies=(), with_prev=False, copy_into=None):
    xs, ps, cts, cfs = list(xs), list(ps), list(cts), list(cfs)
    gs = [list(g) if isinstance(g, (list, tuple)) else [g] for g in gs]
    g_flat = [a for g in gs for a in g]
    halos = xs if with_prev else []
    nx, nh, nct, ng, np_ = len(xs), len(halos), len(cts), len(g_flat), len(ps)
    T = _shape(xs[0])[-2]
    dx_like = xs + halos
    out_shape = ([jax.ShapeDtypeStruct(_shape(x), F32) for x in dx_like] + [jax.ShapeDtypeStruct(_shape(p), F32) for p in ps]
                 + [jax.ShapeDtypeStruct(_shape(xs[i]), BF16) for i in bf16_copies])
    n_in = nx + nh + nct + ng + np_ + len(cfs)
    ndx = nx + nh
    plain = lambda x: jax.ShapeDtypeStruct(_shape(x), F32)
    out_specs = ([_tok_spec(plain(x), tile) for x in dx_like] + [_full_spec(plain(p)) for p in ps]
                 + [_tok_spec(plain(xs[i]), tile) for i in bf16_copies])
    extra, extra_specs, alias = [], [], {}
    if copy_into is not None:
        out_shape[ndx + np_], out_specs[ndx + np_] = copy_into.place(out_shape[ndx + np_], tile)
        extra, extra_specs, alias = copy_into.operand(n_in, ndx + np_)

    def body(*refs):
        vals = [r[...] for r in refs[:n_in]]
        outs = refs[n_in + len(extra):]
        xv, hv, rest = vals[:nx], vals[nx:nx + nh], vals[nx + nh:]
        ctv, gparts, pv, cfv = rest[:nct], rest[nct:nct + ng], rest[nct + ng:nct + ng + np_], rest[nct + ng + np_:]
        gv = []
        for g in gs:
            gv.append(functools.reduce(lambda a, b: a + b, gparts[:len(g)]))
            gparts = gparts[len(g):]
        prev = [_prev_rows(x, h) for x, h in zip(xv, hv)]
        _, vjp = jax.vjp(lambda *xp: f(*xp, *ctv, *cfv), *xv, *prev, *pv)
        d = vjp(tuple(gv))
        for o_ref, r in zip(outs[:ndx], d[:ndx]):
            o_ref[...] = r
        for o_ref, i in zip(outs[ndx + np_:], bf16_copies):
            o_ref[...] = d[i].astype(BF16)

        @pl.when(pl.program_id(0) == 0)
        def _():
            for o_ref in outs[ndx:ndx + np_]:
                o_ref[...] = jnp.zeros_like(o_ref)

        for o_ref, r in zip(outs[ndx:ndx + np_], d[ndx:]):
            o_ref[...] += r

    res = pl.pallas_call(
        body, grid=(T // tile,),
        in_specs=([_tok_spec(x, tile) for x in xs] + [_halo_spec(x, tile) for x in halos]
                  + [_tok_spec(c, tile) for c in cts + g_flat] + [_full_spec(p) for p in ps + cfs] + extra_specs),
        out_specs=out_specs, out_shape=out_shape, input_output_aliases=alias,
        compiler_params=_cparams(("arbitrary",)), name=name)(*[_arr(a) for a in xs + halos + cts + g_flat + ps + cfs], *extra)
    if bf16_copies:
        return list(res[:ndx]), list(res[ndx:ndx + np_]), list(res[ndx + np_:])
    return list(res[:ndx]), list(res[ndx:])


def _rms(x, g, eps=RMS_EPS):
    return x * lax.rsqrt(jnp.mean(x * x, axis=-1, keepdims=True) + eps) * g


def f_rmsnorm(x, g):
    return (_rms(x, g),)


def f_rmsnorm_res(x, g):
    return _rms(x, g), x


def f_rmsnorm2(x, g1, g2):
    n = x * lax.rsqrt(jnp.mean(x * x, axis=-1, keepdims=True) + RMS_EPS)
    return n * g1, n * g2


def f_rmsnorm2_res(x, g1, g2):
    return f_rmsnorm2(x, g1, g2) + (x,)


def _sigmoid(x):
    return 1.0 / (1.0 + jnp.exp(-x))


def _softplus(x):
    return jnp.maximum(x, 0.0) + jnp.log(1.0 + jnp.exp(-jnp.abs(x)))


def f_rwkv_pre(pr, pk, pv, pl_, qr, qk, qv, ql, mu_r, mu_k, mu_v, mu_l, w0, w2, a0, a2, g2, k_k, k_a, seg, seg_t):
    xr = pr + (qr - pr) * mu_r
    xk = pk + (qk - pk) * mu_k
    xv = pv + (qv - pv) * mu_v
    xl = pl_ + (ql - pl_) * mu_l
    w_log = -_softplus(-(w0 + mm_nn(jnp.tanh(xl), w2))) - 0.5
    lw = -jnp.exp(w_log)
    a = _sigmoid(a0 + mm_nn(xl, a2))
    g = mm_nn(_sigmoid(xl), g2)
    kkr = xk * k_k
    inv = lax.rsqrt(jnp.maximum(mmh(kkr * kkr, seg), 1e-24))
    kk = kkr * mmh(inv, seg_t)
    k2 = xk * (1.0 + (a - 1.0) * k_a)
    return xr, lw, k2, xv, kk, kk * a, g


def f_rwkv_post(y, r, k2, v, g, lnx_w, lnx_b, r_k, seg, seg_t):
    inv_n = 1.0 / HEAD_DIM
    m = mmh(mmh(y, seg) * inv_n, seg_t)
    yc = y - m
    rstd = lax.rsqrt(mmh(yc * yc, seg) * inv_n + LNX_EPS)
    yn = yc * mmh(rstd, seg_t) * lnx_w + lnx_b
    bonus = mmh(mmh(r * k2 * r_k, seg), seg_t) * v
    return ((yn + bonus) * g,)


def _headnorm(z, g, seg, seg_t):
    ms = mmh(z * z, seg) * (1.0 / HEAD_DIM)
    return z * mmh(lax.rsqrt(ms + RMS_EPS), seg_t) * g


def f_headnorm(z, g, seg, seg_t):
    return (_headnorm(z, g, seg, seg_t),)


def _rot_half(z):
    w = z.shape[1]
    half = HEAD_DIM // 2
    lane = lax.broadcasted_iota(jnp.int32, (1, w), 1)
    return jnp.where((lane & (HEAD_DIM - 1)) < half, -pltpu.roll(z, w - half, 1), pltpu.roll(z, half, 1))


@jax.custom_vjp
def _rotate_half(z):
    return _rot_half(z)


_rotate_half.defvjp(lambda z: (_rot_half(z), None), lambda _, g: (-_rot_half(g),))


def f_qkprep(z, g, cos, sin, seg, seg_t):
    zn = _headnorm(z, g, seg, seg_t)
    pairs = z.shape[1] // cos.shape[1]
    return (zn * jnp.tile(cos, (1, pairs)) + _rotate_half(zn) * jnp.tile(sin, (1, pairs)),)


def _head_mask(width, h):
    lane = lax.broadcasted_iota(jnp.int32, (1, width), 1)
    return jnp.where((lane >> 6) == h, jnp.ones((), F32), 0.0)


def f_memattn(q, k, v, q_norm, seg, seg_t):
    qn = _headnorm(q, q_norm, seg, seg_t)
    out = jnp.zeros_like(q)
    for h in range(MEM_HEADS):
        m = _head_mask(MEM_WIDTH, h)
        s = mm_nt(qn * m, k) * (1.0 / math.sqrt(HEAD_DIM))
        s = s - jnp.max(s, axis=-1, keepdims=True)
        p = jnp.exp(s)
        p = p / jnp.sum(p, axis=-1, keepdims=True)
        out = out + mm_nn(p, v) * m
    return (out,)


def f_mix(o1, o2, o3, l1, l2, l3):
    mx = jnp.maximum(jnp.maximum(l1, l2), l3)
    e1, e2, e3 = jnp.exp(l1 - mx), jnp.exp(l2 - mx), jnp.exp(l3 - mx)
    return ((e1 * o1 + e2 * o2 + e3 * o3) / (e1 + e2 + e3),)


def _chunk_masks(L):
    t = lax.broadcasted_iota(jnp.int32, (L, L), 0)
    s = lax.broadcasted_iota(jnp.int32, (L, L), 1)
    return t, s


def _unit_lower_inverse(a):
    L = a.shape[-1]
    t, s = _chunk_masks(L)
    one = jnp.ones((), F32)
    blk = lambda sh: jnp.where((t >> sh) == (s >> sh), one, 0.0)
    n0 = a * blk(3)
    x = jnp.where(t == s, one, 0.0) - n0
    n2 = mmh(n0, n0)
    x = x + mmh(x, n2)
    x = x + mmh(x, mmh(n2, n2))
    for sh in (3, 4, 5):
        if (1 << sh) >= L:
            break
        off = a * (blk(sh + 1) - blk(sh))
        x = x - mmh(x, mmh(off, x))
    return x


@jax.custom_vjp
def _inverse_known(a, x):
    return x


def _inverse_known_fwd(a, x):
    return x, x


def _inverse_known_bwd(x, dx):
    return -mmh_nt(mmh_tn(x, dx), x), jnp.zeros_like(x)


_inverse_known.defvjp(_inverse_known_fwd, _inverse_known_bwd)


def _running_sum(x, reverse):
    L = x.shape[1]
    pos = lax.broadcasted_iota(jnp.int32, (1, L, 1), 1)
    step = 1
    while step < L:
        if reverse:
            x = x + jnp.where(pos < L - step, pltpu.roll(x, L - step, 1), 0.0)
        else:
            x = x + jnp.where(pos >= step, pltpu.roll(x, step, 1), 0.0)
        step *= 2
    return x


@jax.custom_vjp
def _cumsum_tokens(x):
    return _running_sum(x, False)


_cumsum_tokens.defvjp(lambda x: (_running_sum(x, False), None), lambda _, g: (_running_sum(g, True),))


def f_rwkv_chunk(s0, r, lw, k, v, kk, b, x_known=None):
    H, L, _ = r.shape
    t, s = _chunk_masks(L)
    one = jnp.ones((), F32)
    incl = jnp.where(t >= s, one, 0.0)
    strict = jnp.where(t > s, one, 0.0)
    cum = _cumsum_tokens(lw)
    w_in = jnp.exp(cum)
    w_ex = jnp.exp(cum - lw)
    w_inv = jnp.exp(-cum)
    rt, kkt, kt, bt = r * w_in, kk * w_ex, k * w_inv, b * w_inv
    a_b = mmh_nt(kkt, bt) * strict
    a_k = mmh_nt(kkt, kt) * strict
    m_k = mmh_nt(rt, kt) * incl
    m_b = mmh_nt(rt, bt) * incl
    x = _unit_lower_inverse(a_b) if x_known is None else _inverse_known(a_b, x_known)
    u = mmh(x, mmh_nt(kkt, s0) + mmh(a_k, v))
    y = mmh_nt(rt, s0) + mmh(m_k, v) - mmh(m_b, u)
    w_last = jnp.exp(jnp.sum(lw, axis=1, keepdims=True))
    s1 = (s0 + mmh_tn(v, kt) - mmh_tn(u, bt)) * w_last
    return y, s1, x


def _ex_split(ex, refs, n_in, n_out):
    n = ex.n
    ins, ex_in = refs[:n_in], refs[n_in:n_in + n]
    outs, ex_out = refs[n_in + n:n_in + n + n_out], refs[n_in + n + n_out:n_in + 2 * n + n_out]
    rest = refs[n_in + 2 * n + n_out:]
    return ins, outs, rest[:len(rest) - 3], (ex_in, ex_out) + tuple(rest[len(rest) - 3:])


def _split_heads(x):
    return jnp.stack([x[:, h * HEAD_DIM:(h + 1) * HEAD_DIM] for h in range(x.shape[1] // HEAD_DIM)], axis=0)


def _merge_heads(x):
    return jnp.concatenate([x[h] for h in range(x.shape[0])], axis=1)


def rwkv_scan_fwd(r, lw, k, v, kk, b, ex):
    T, N = r.shape[0], HEAD_DIM
    H = r.shape[1] // N
    groups = SCAN_GROUPS_FWD
    nc, hg = T // CHUNK, H // groups
    seq = pl.BlockSpec((CHUNK, hg * N), lambda g, c: (c, g))

    def body(*refs):
        (r_ref, lw_ref, k_ref, v_ref, kk_ref, b_ref), (y_ref, hs_ref, xs_ref), (h_scr,), ex_refs = _ex_split(ex, refs, 6, 3)
        g, c = pl.program_id(0), pl.program_id(1)

        @pl.when(jnp.logical_and(g == 0, c == 0))
        def _():
            ex.start(*ex_refs)

        @pl.when(c == 0)
        def _():
            h_scr[...] = jnp.zeros_like(h_scr)

        h0 = h_scr[...]
        hs_ref[0] = h0
        y, h1, x = f_rwkv_chunk(h0, *[_split_heads(z[...]) for z in (r_ref, lw_ref, k_ref, v_ref, kk_ref, b_ref)])
        y_ref[...] = _merge_heads(y)
        xs_ref[0] = x
        h_scr[...] = h1

        @pl.when(jnp.logical_and(g == groups - 1, c == (3 * nc) // 4))
        def _():
            ex.forward(*ex_refs)

        @pl.when(jnp.logical_and(g == groups - 1, c == nc - 1))
        def _():
            ex.wait(*ex_refs)

    res = pl.pallas_call(
        body, grid=(groups, nc), in_specs=[seq] * 6 + [_ANY] * ex.n,
        out_specs=[seq, pl.BlockSpec((1, hg, N, N), lambda g, c: (c, g, 0, 0)),
                   pl.BlockSpec((1, hg, CHUNK, CHUNK), lambda g, c: (c, g, 0, 0))] + [_ANY] * ex.n,
        out_shape=[jax.ShapeDtypeStruct((T, H * N), F32), jax.ShapeDtypeStruct((nc, H, N, N), F32),
                   jax.ShapeDtypeStruct((nc, H, CHUNK, CHUNK), F32)] + ex.out_shape(),
        scratch_shapes=[pltpu.VMEM((hg, N, N), F32)] + ex.scratch(),
        compiler_params=_cparams(("arbitrary", "arbitrary")), name="rwkv_scan_fwd")(r, lw, k, v, kk, b, *ex.operands())
    return res[0], (res[1], res[2]), list(res[3:])


def rwkv_scan_bwd(r, lw, k, v, kk, b, saved, dy, ex):
    T, N = r.shape[0], HEAD_DIM
    H = r.shape[1] // N
    groups = SCAN_GROUPS_BWD
    nc, hg = T // CHUNK, H // groups
    seq = pl.BlockSpec((CHUNK, hg * N), lambda g, c: (nc - 1 - c, g))
    state = pl.BlockSpec((1, hg, N, N), lambda g, c: (nc - 1 - c, g, 0, 0))

    def body(*refs):
        (r_ref, lw_ref, k_ref, v_ref, kk_ref, b_ref, hs_ref, xs_ref, dy_ref), outs, (dh_scr,), ex_refs = _ex_split(ex, refs, 9, 6)
        g, c = pl.program_id(0), pl.program_id(1)

        @pl.when(jnp.logical_and(g == 0, c == 0))
        def _():
            ex.start(*ex_refs)

        @pl.when(c == 0)
        def _():
            dh_scr[...] = jnp.zeros_like(dh_scr)

        x_known = xs_ref[0]
        _, vjp = jax.vjp(lambda *a: f_rwkv_chunk(*a, x_known=x_known)[:2], hs_ref[0],
                         *[_split_heads(z[...]) for z in (r_ref, lw_ref, k_ref, v_ref, kk_ref, b_ref)])
        d = vjp((_split_heads(dy_ref[...]), dh_scr[...]))
        dh_scr[...] = d[0]
        for o_ref, dz in zip(outs, d[1:]):
            o_ref[...] = _merge_heads(dz)

        @pl.when(jnp.logical_and(g == groups - 1, c == nc - 1))
        def _():
            ex.forward(*ex_refs)
            ex.wait(*ex_refs)

    res = pl.pallas_call(
        body, grid=(groups, nc),
        in_specs=[seq] * 6 + [state, state, seq] + [_ANY] * ex.n,
        out_specs=[seq] * 6 + [_ANY] * ex.n, out_shape=[jax.ShapeDtypeStruct((T, H * N), F32)] * 6 + ex.out_shape(),
        scratch_shapes=[pltpu.VMEM((hg, N, N), F32)] + ex.scratch(),
        compiler_params=_cparams(("arbitrary", "arbitrary")), name="rwkv_scan_bwd")(r, lw, k, v, kk, b, *saved, dy, *ex.operands())
    return list(res[:6]), list(res[6:])


GROUP_COLS = 4 * HEAD_DIM


def _f_dilattn(has_prev, q, kc, kp, vc, vp):
    scale = 1.0 / math.sqrt(HEAD_DIM)
    i = lax.broadcasted_iota(jnp.int32, (DIL_BLOCK, DIL_BLOCK), 0)
    j = lax.broadcasted_iota(jnp.int32, (DIL_BLOCK, DIL_BLOCK), 1)
    o, l = jnp.zeros_like(q), jnp.zeros_like(q)
    for h in range(q.shape[1] // HEAD_DIM):
        m = _head_mask(q.shape[1], h)
        sc = jnp.where(j <= i, mm_nt(q * m, kc) * scale, NEG_INF)
        sp = jnp.where(jnp.logical_and(i <= j, has_prev), mm_nt(q * m, kp) * scale, NEG_INF)
        mx = jnp.maximum(jnp.max(sc, axis=-1, keepdims=True), jnp.max(sp, axis=-1, keepdims=True))
        pc, pp = jnp.exp(sc - mx), jnp.exp(sp - mx)
        den = jnp.sum(pc, axis=-1, keepdims=True) + jnp.sum(pp, axis=-1, keepdims=True)
        o = o + (mm_nn(pc, vc) + mm_nn(pp, vp)) / den * m
        l = l + (mx + jnp.log(den)) * m
    return o, l


def _dil_specs(gi, d):
    parts = 1 if d == 1 else 2
    blk = (DIL_BLOCK * d, GROUP_COLS // parts)
    at = lambda col: (lambda p, n: (n, col * parts + p))
    before = lambda col: (lambda p, n: (jnp.maximum(n - 1, 0), col * parts + p))
    v0 = DIL_WIDTH // GROUP_COLS + gi
    q = pl.BlockSpec(blk, at(gi))
    kc, kp = pl.BlockSpec(blk, at(gi)), pl.BlockSpec(blk, before(gi))
    vc, vp = pl.BlockSpec(blk, at(v0)), pl.BlockSpec(blk, before(v0))
    out = pl.BlockSpec(blk, at(0))
    together = min(d, 2)
    return (q, kc, kp, vc, vp, out), parts, together


def _residue_rows(r, d):
    return pl.ds(r, DIL_BLOCK, stride=d) if d > 1 else pl.ds(0, DIL_BLOCK)


def dil_fwd(q, k, kv, gi, d, name):
    T = q.shape[0]
    (qs, kc, kp, vc, vp, out), parts, together = _dil_specs(gi, d)

    def body(q_ref, kc_ref, kp_ref, vc_ref, vp_ref, o_ref, l_ref):
        has_prev = pl.program_id(1) > 0

        def residues(it, carry):
            rows = [_residue_rows(it * together + a, d) for a in range(together)]
            ins = [[ref[rw, :] for ref in (q_ref, kc_ref, kp_ref, vc_ref, vp_ref)] for rw in rows]
            res = [_f_dilattn(has_prev, *x) for x in ins]
            for rw, (o, l) in zip(rows, res):
                o_ref[rw, :] = o
                l_ref[rw, :] = l
            return carry

        lax.fori_loop(0, d // together, residues, 0)

    shape = jax.ShapeDtypeStruct((T, 4 * HEAD_DIM), F32)
    return pl.pallas_call(
        body, grid=(parts, T // (DIL_BLOCK * d)), in_specs=[qs, kc, kp, vc, vp], out_specs=[out, out], out_shape=[shape, shape],
        compiler_params=_cparams(("parallel", "parallel")), name=name)(q, k, k, kv, kv)


def dil_bwd(q, k, kv, do, dl, gi, d, name, dq_all=None):
    T = q.shape[0]
    (qs, kc, kp, vc, vp, out), parts, together = _dil_specs(gi, d)
    begun = [] if dq_all is None else [dq_all]

    def body(q_ref, kc_ref, kp_ref, vc_ref, vp_ref, do_ref, dl_ref, *rest):
        outs = rest[len(begun):]
        f = functools.partial(_f_dilattn, pl.program_id(1) > 0)

        def residues(it, carry):
            rows = [_residue_rows(it * together + a, d) for a in range(together)]
            ins = [[ref[rw, :] for ref in (q_ref, kc_ref, kp_ref, vc_ref, vp_ref, do_ref, dl_ref)] for rw in rows]
            res = [jax.vjp(f, *x[:5])[1]((x[5], x[6])) for x in ins]
            for rw, gs in zip(rows, res):
                for o_ref, g in zip(outs, gs):
                    o_ref[rw, :] = g
            return carry

        lax.fori_loop(0, d // together, residues, 0)

    shape = jax.ShapeDtypeStruct((T, 4 * HEAD_DIM), F32)
    dq, dkc, dkp, dvc, dvp = pl.pallas_call(
        body, grid=(parts, T // (DIL_BLOCK * d)), in_specs=[qs, kc, kp, vc, vp, out, out] + [_ANY] * len(begun),
        out_specs=[qs] + [out] * 4, out_shape=[jax.ShapeDtypeStruct(q.shape, F32)] + [shape] * 4,
        input_output_aliases={7: 0} if begun else {},
        compiler_params=_cparams(("parallel", "parallel")), name=name)(q, k, k, kv, kv, do, dl, *begun)

    def own_plus_next(c, p):
        return c + jnp.concatenate([p[DIL_BLOCK * d:], jnp.zeros_like(p[:DIL_BLOCK * d])], axis=0)

    return dq, own_plus_next(dkc, dkp), own_plus_next(dvc, dvp)


CONV_TILE = 256


def _conv3(before, u, w, b):
    ue = jnp.concatenate([before, u], axis=0)
    s1, s2 = pltpu.roll(ue, 1, 0)[8:], pltpu.roll(ue, 2, 0)[8:]
    return b + w[0:1] * s2 + w[1:2] * s1 + w[2:3] * u, s1, s2


def _conv_halves(u_ref, h_ref, cw_ref, cb_ref):
    F = D_FF
    res = []
    for lo in (0, F):
        before = jnp.where(pl.program_id(0) > 0, h_ref[:, lo:lo + F], 0.0)
        u = u_ref[:, lo:lo + F]
        res.append((u,) + _conv3(before, u, cw_ref[:, lo:lo + F], cb_ref[:, lo:lo + F]))
    return res


def _halo_before(C):
    return pl.BlockSpec((8, C), lambda i: (jnp.maximum(i * (CONV_TILE // 8) - 1, 0), 0))


def convgate_fwd(u, cw, cb, name):
    T, C = u.shape
    F = C // 2

    def body(u_ref, h_ref, cw_ref, cb_ref, z_ref):
        (_, cg, _, _), (_, cv, _, _) = _conv_halves(u_ref, h_ref, cw_ref, cb_ref)
        z_ref[...] = (cg * _sigmoid(cg) * cv).astype(BF16)

    return pl.pallas_call(
        body, grid=(T // CONV_TILE,),
        in_specs=[pl.BlockSpec((CONV_TILE, C), lambda i: (i, 0)), _halo_before(C), _full_spec(cw), _full_spec(cb)],
        out_specs=pl.BlockSpec((CONV_TILE, F), lambda i: (i, 0)), out_shape=jax.ShapeDtypeStruct((T, F), BF16),
        compiler_params=_cparams(("parallel",)), name=name)(u, u, cw, cb)


def convgate_bwd(u, cw, cb, dz, name):
    T, C = u.shape
    F = C // 2
    n = T // CONV_TILE
    E = CONV_TILE + 8

    def body(u_ref, hb_ref, ha_ref, cw_ref, cb_ref, dz_ref, dza_ref, du_ref, dcw_ref, dcb_ref):
        i = pl.program_id(0)
        dze = jnp.concatenate([dz_ref[...], jnp.where(i < n - 1, dza_ref[...], 0.0)], axis=0)

        @pl.when(i == 0)
        def _():
            dcw_ref[...] = jnp.zeros_like(dcw_ref)
            dcb_ref[...] = jnp.zeros_like(dcb_ref)

        halves = []
        for lo in (0, F):
            sl = slice(lo, lo + F)
            ue = jnp.concatenate([u_ref[:, sl], ha_ref[:, sl]], axis=0)
            c, s1, s2 = _conv3(jnp.where(i > 0, hb_ref[:, sl], 0.0), ue, cw_ref[:, sl], cb_ref[:, sl])
            halves.append((sl, ue, c, s1, s2))
        (_, _, cg, _, _), (_, _, cv, _, _) = halves
        sg = _sigmoid(cg)
        dcs = (dze * cv * sg * (1.0 + cg * (1.0 - sg)), dze * cg * sg)
        for (sl, ue, _, s1, s2), dc in zip(halves, dcs):
            own = lambda z: z[:CONV_TILE]
            dcb_ref[:, sl] += jnp.sum(own(dc), axis=0, keepdims=True)
            dcw_ref[0:1, sl] += jnp.sum(own(dc * s2), axis=0, keepdims=True)
            dcw_ref[1:2, sl] += jnp.sum(own(dc * s1), axis=0, keepdims=True)
            dcw_ref[2:3, sl] += jnp.sum(own(dc * ue), axis=0, keepdims=True)
            du = cw_ref[2:3, sl] * dc + cw_ref[1:2, sl] * pltpu.roll(dc, E - 1, 0) + cw_ref[0:1, sl] * pltpu.roll(dc, E - 2, 0)
            du_ref[:, sl] = own(du).astype(BF16)

    after = lambda w: pl.BlockSpec((8, w), lambda i: (jnp.minimum((i + 1) * (CONV_TILE // 8), T // 8 - 1), 0))
    return pl.pallas_call(
        body, grid=(n,),
        in_specs=[pl.BlockSpec((CONV_TILE, C), lambda i: (i, 0)), _halo_before(C), after(C), _full_spec(cw), _full_spec(cb),
                  pl.BlockSpec((CONV_TILE, F), lambda i: (i, 0)), after(F)],
        out_specs=[pl.BlockSpec((CONV_TILE, C), lambda i: (i, 0)), _full_spec(cw), _full_spec(cb)],
        out_shape=[jax.ShapeDtypeStruct((T, C), BF16), jax.ShapeDtypeStruct(cw.shape, F32), jax.ShapeDtypeStruct(cb.shape, F32)],
        compiler_params=_cparams(("arbitrary",)), name=name)(u, u, u, cw, cb, dz, dz)


def loss_head(y, tgt):
    T, D = y.shape
    tile = ROW_TILE

    def body(y_ref, t_ref, l_ref, d_ref, db_ref):
        d = y_ref[...] - t_ref[...]
        d_ref[...] = d * (1.0 / D)
        db_ref[...] = (d * (1.0 / D)).astype(BF16)

        @pl.when(pl.program_id(0) == 0)
        def _():
            l_ref[...] = jnp.zeros_like(l_ref)

        l_ref[...] += (0.5 / D) * jnp.sum(d * d)

    row = pl.BlockSpec((tile, D), lambda i: (i, 0))
    return pl.pallas_call(
        body, grid=(T // tile,), in_specs=[row, row], out_specs=[pl.BlockSpec((8, 128), lambda i: (0, 0)), row, row],
        out_shape=[jax.ShapeDtypeStruct((8, 128), F32), jax.ShapeDtypeStruct((T, D), F32), jax.ShapeDtypeStruct((T, D), BF16)],
        compiler_params=_cparams(("arbitrary",)), name="loss_head")(y, tgt)


def sum_parts(parts, name):
    S, R, C = parts.shape
    tile = _pick(R, (512, 256, 128, 64, 32, 16, 8))

    def body(p_ref, o_ref):
        acc = p_ref[0].astype(F32)
        for s in range(1, S):
            acc = acc + p_ref[s].astype(F32)
        o_ref[...] = acc

    return pl.pallas_call(
        body, grid=(R // tile,), in_specs=[pl.BlockSpec((S, tile, C), lambda i: (0, i, 0))],
        out_specs=pl.BlockSpec((tile, C), lambda i: (i, 0)), out_shape=jax.ShapeDtypeStruct((R, C), F32),
        compiler_params=_cparams(("parallel",)), name=name)(parts)


def adamw(gparts, w, m, v, name):
    S, R, C = gparts.shape
    tile = _pick(R, (512, 256, 128, 64, 32, 16, 8))
    c1 = 1.0 / (1.0 - ADAM_B1 ** ADAM_STEP)
    c2 = 1.0 / (1.0 - ADAM_B2 ** ADAM_STEP)

    def body(g_ref, w_ref, m_ref, v_ref, go_ref, d_ref, mo_ref, vo_ref):
        g = g_ref[0].astype(F32)
        for s in range(1, S):
            g = g + g_ref[s].astype(F32)
        m1 = ADAM_B1 * m_ref[...] + (1.0 - ADAM_B1) * g
        v1 = ADAM_B2 * v_ref[...] + (1.0 - ADAM_B2) * (g * g)
        go_ref[...] = g
        mo_ref[...] = m1
        vo_ref[...] = v1
        d_ref[...] = -ADAM_LR * ((m1 * c1) / (jnp.sqrt(v1 * c2) + ADAM_EPS) + ADAM_WD * w_ref[...])

    row = pl.BlockSpec((tile, C), lambda i: (i, 0))
    return pl.pallas_call(
        body, grid=(R // tile,), in_specs=[pl.BlockSpec((S, tile, C), lambda i: (0, i, 0)), row, row, row],
        out_specs=[row] * 4, out_shape=[jax.ShapeDtypeStruct((R, C), F32)] * 4,
        compiler_params=_cparams(("parallel",)), name=name)(gparts, w, m, v)


def _peers():
    x, y, c = lax.axis_index("x"), lax.axis_index("y"), lax.axis_index("c")
    peers = []
    for k in range(1, N_DEV):
        px = 1 - x if k & 4 else x
        py = 1 - y if k & 2 else y
        pc = 1 - c if k & 1 else c
        peers.append(((px, py, pc), 4 * px + 2 * py + pc))
    return 4 * x + 2 * y + c, peers


_ANY = pl.BlockSpec(memory_space=pl.ANY)


class Exchange:
    def __init__(self, gathers=(), scatters=()):
        self.gathers, self.scatters = list(gathers), list(scatters)
        self.n = len(self.gathers) + len(self.scatters)

    def operands(self):
        return self.gathers + self.scatters

    def out_shape(self):
        return ([jax.ShapeDtypeStruct((N_DEV,) + x.shape, x.dtype) for x in self.gathers]
                + [jax.ShapeDtypeStruct(x.shape, x.dtype) for x in self.scatters])

    def scratch(self):
        n = max(self.n, 1)
        return [pltpu.SemaphoreType.DMA((7 * n,)), pltpu.SemaphoreType.DMA((7 * n,)), pltpu.SemaphoreType.DMA((n,))]

    def _copies(self, in_refs, out_refs, send_sems, recv_sems, local_sems):
        me, peers = _peers()
        ng = len(self.gathers)
        local, sends, recvs = [], [], []
        for a in range(self.n):
            x, o = in_refs[a], out_refs[a]
            mine = x if a < ng else x.at[me]
            local.append(pltpu.make_async_copy(mine, o.at[me], local_sems.at[a]))
            s_a, r_a = {}, {}
            for k in range(1, N_DEV):
                peer, slot = peers[k - 1]
                sems = dict(send_sem=send_sems.at[7 * a + k - 1], recv_sem=recv_sems.at[7 * a + k - 1],
                            device_id_type=pl.DeviceIdType.MESH)
                if a >= ng:
                    s_a[k] = pltpu.make_async_remote_copy(src_ref=x.at[slot], dst_ref=o.at[me], device_id=peer, **sems)
                elif k in FORWARDED:
                    came = o.at[peers[k - 2][1]]
                    s_a[k] = pltpu.make_async_remote_copy(src_ref=came, dst_ref=came, device_id=peers[0][0], **sems)
                else:
                    s_a[k] = pltpu.make_async_remote_copy(src_ref=x, dst_ref=o.at[me], device_id=peer, **sems)
                r_a[k] = pltpu.make_async_remote_copy(src_ref=mine, dst_ref=o.at[slot], device_id=peer, **sems)
            sends.append(s_a)
            recvs.append(r_a)
        return local, sends, recvs

    def start(self, *refs):
        if self.n == 0:
            return
        local, sends, _ = self._copies(*refs)
        for a in range(self.n):
            local[a].start()
            for k in range(1, N_DEV):
                if a >= len(self.gathers) or k not in FORWARDED:
                    sends[a][k].start()

    def forward(self, *refs):
        if not self.gathers:
            return
        _, sends, recvs = self._copies(*refs)
        for a in range(len(self.gathers)):
            for k in FORWARDED:
                recvs[a][k - 1].wait_recv()
                sends[a][k].start()

    def wait(self, *refs):
        if self.n == 0:
            return
        local, sends, recvs = self._copies(*refs)
        for a in range(self.n):
            waited_early = [f - 1 for f in FORWARDED] if a < len(self.gathers) else []
            for k in range(1, N_DEV):
                if k not in waited_early:
                    recvs[a][k].wait_recv()
            for k in range(1, N_DEV):
                sends[a][k].wait_send()
            local[a].wait()


FORWARDED = (3, 5, 7)


def exchange(ex, name):
    n = ex.n

    def body(*refs):
        args = (refs[:n], refs[n:2 * n]) + tuple(refs[2 * n:])
        ex.start(*args)
        ex.forward(*args)
        ex.wait(*args)

    return pl.pallas_call(body, in_specs=[_ANY] * n, out_specs=[_ANY] * n, out_shape=ex.out_shape(),
                          scratch_shapes=ex.scratch(), name=name)(*ex.operands())


def _shift_up(z):
    return jnp.concatenate([z[1:], jnp.zeros_like(z[:1])], axis=0)


def _segments(width):
    seg = np.zeros((width, 128), np.float32)
    seg[np.arange(width), np.arange(width) // HEAD_DIM] = 1.0
    return jnp.asarray(seg), jnp.asarray(seg.T)


def _rope_consts(T):
    inv = ROPE_THETA ** (-jnp.arange(0, HEAD_DIM, 2, dtype=F32) / HEAD_DIM)
    ang = jnp.arange(T, dtype=F32)[:, None] * inv[None, :]
    return jnp.tile(jnp.cos(ang), (1, 4)), jnp.tile(jnp.sin(ang), (1, 4))


def _per_head(g, heads):
    return jnp.tile(g.reshape(1, HEAD_DIM), (1, heads))


def _sum_heads(g):
    return g.reshape(-1, HEAD_DIM).sum(axis=0, keepdims=True)


LORA_COLS = 256
RW_TILE = 256
ROW_TILE = 512
NORM_TILE = 1024


def _local_step(x0, memx, tgt, P, ex_weights=None, weights_done=None, ex_grads=None):
    T = x0.shape[0]
    P = dict(P)
    G = {}
    seg, seg_t = _segments(RWKV_WIDTH)
    mseg = (seg[:MEM_WIDTH], seg_t[:, :MEM_WIDTH])
    cos, sin = _rope_consts(T)
    row = lambda v: v.reshape(1, -1)

    def mem_fwd(i, q, into):
        memn = stage_fwd(f_rmsnorm, [memx], [P["mem_norm"][i:i + 1]], [], [], N_MEM, f"mem{i}_norm", [BF16])[0]
        kvm = matmul(memn, P["mem_w_kv"][i], "nn", f"mem{i}_kv")
        kn, qn = _per_head(P["mem_k_norm"][i], MEM_HEADS), _per_head(P["mem_q_norm"][i], MEM_HEADS)
        km = stage_fwd(f_headnorm, [Cols(kvm, MEM_WIDTH, 0)], [kn], [], mseg, N_MEM, f"mem{i}_knorm")[0]
        om = stage_fwd(f_memattn, [q], [km, Cols(kvm, MEM_WIDTH, 1), qn], [], mseg, ROW_TILE, f"mem{i}_attn", [BF16], into=into)[0]
        return om, (memn, kvm, km, kn, qn, q)

    def mem_bwd(i, saved, dymem, copy_into=None):
        memn, kvm, km, kn, qn, q = saved
        (dq,), (dkm, dvm, g_qn), *copy = stage_bwd(f_memattn, [q], [km, Cols(kvm, MEM_WIDTH, 1), qn], [], mseg, [dymem], ROW_TILE,
                                                   f"mem{i}_attn_bwd", bf16_copies=(0,) if copy_into else (), copy_into=copy_into)
        dq = copy[0][0] if copy_into else dq
        (dkraw,), (g_kn,) = stage_bwd(f_headnorm, [Cols(kvm, MEM_WIDTH, 0)], [kn], [], mseg, [dkm], N_MEM, f"mem{i}_knorm_bwd")
        dkvm = jnp.concatenate([dkraw, dvm], axis=1).astype(BF16)
        g_w = matmul(memn, dkvm, "tn", f"mem{i}_kv_dw")
        dmemn = matmul(dkvm, P["mem_w_kv"][i], "nt", f"mem{i}_kv_dx")
        _, (g_mn,) = stage_bwd(f_rmsnorm, [memx], [P["mem_norm"][i:i + 1]], [], [], [dmemn], N_MEM, f"mem{i}_norm_bwd")
        return dq, g_mn, g_w, _sum_heads(g_qn), _sum_heads(g_kn)

    def ffn_fwd(i, xin):
        hn = stage_fwd(f_rmsnorm, [xin], [P["ffn_norm"][i:i + 1]], [], [], NORM_TILE, f"ffn{i}_norm", [BF16])[0]
        u = matmul(hn, P["ffn_w_up"][i], "nt", f"ffn{i}_up")
        z = convgate_fwd(u, P["ffn_conv_w"][i], P["ffn_conv_b"][i:i + 1], f"ffn{i}_conv")
        return matmul(z, P["ffn_w_down"][i], "nn", f"ffn{i}_down", residual=xin), (hn, u, z)

    def ffn_bwd(i, xin, saved, dxo, dxo_b):
        hn, u, z = saved
        dz = matmul(dxo_b, P["ffn_w_down"][i], "nt", f"ffn{i}_down_dx")
        g_down = matmul(z, dxo_b, "tn", f"ffn{i}_down_dw")
        du, g_cw, g_cb = convgate_bwd(u, P["ffn_conv_w"][i], P["ffn_conv_b"][i:i + 1], dz, f"ffn{i}_conv_bwd")
        dhn = matmul(du, P["ffn_w_up"][i], "nn", f"ffn{i}_up_dx")
        g_up = matmul(du, hn, "tn", f"ffn{i}_up_dw")
        (dxin,), (g_n,), (dxin_b,) = stage_bwd(f_rmsnorm_res, [xin], [P["ffn_norm"][i:i + 1]], [], [], [dhn, dxo], ROW_TILE,
                                               f"ffn{i}_norm_bwd", bf16_copies=(0,))
        return dxin, dxin_b, g_n, g_up, g_cw, g_cb, g_down

    h0 = stage_fwd(f_rmsnorm, [x0], [P["attn_norm"][0:1]], [], [], NORM_TILE, "l0_norm", [BF16])[0]
    p0 = matmul(h0, P["a_w_in"][0], "nt", "l0_in")
    lora0 = 3 * RWKV_WIDTH // LORA_COLS
    pre_xs = [Cols(p0, RWKV_WIDTH, 0), Cols(p0, RWKV_WIDTH, 1), Cols(p0, RWKV_WIDTH, 2), Cols(p0, LORA_COLS, lora0)]
    mu = [Cols(P["a_mu"], RWKV_WIDTH, 0), Cols(P["a_mu"], RWKV_WIDTH, 1), Cols(P["a_mu"], RWKV_WIDTH, 2),
          Cols(P["a_mu"], LORA_COLS, lora0)]
    lora_rows = lambda w, lo: jnp.pad(w, ((lo, LORA_COLS - lo - w.shape[0]), (0, 0)))
    pre_ps = mu + [P["a_w0"], lora_rows(P["a_w2"][0], 0), P["a_a0"], lora_rows(P["a_a2"][0], 64), lora_rows(P["a_g2"][0], 128),
                   P["a_k_k"], P["a_k_a"]]
    r, lw, k2, v, kk, b, g = stage_fwd(f_rwkv_pre, pre_xs, pre_ps, [], [seg, seg_t], RW_TILE, "l0_rwkv_pre", with_prev=True)
    scan_in = [r, lw, k2, v, kk, b]
    y_h, h_states, got = rwkv_scan_fwd(*scan_in, ex_weights or Exchange())
    if weights_done is not None:
        P.update(weights_done(got))
    y_s = y_h
    post_ps = [P["a_lnx_w"], P["a_lnx_b"], P["a_r_k"].reshape(1, RWKV_WIDTH)]
    ycat0, mem0_saved = mem_fwd(0, Cols(p0, MEM_WIDTH, SHIFT_WIDTH // MEM_WIDTH), Into(None, D_MODEL, RWKV_WIDTH // MEM_WIDTH))
    ycat0 = stage_fwd(f_rwkv_post, [y_s, r, k2, v, g], post_ps, [], [seg, seg_t], RW_TILE, "l0_rwkv_post", [BF16],
                      into=Into(ycat0, D_MODEL, 0))[0]
    x1 = matmul(ycat0, P["a_w_out"][0], "nn", "l0_out", residual=x0)
    x2, ffn0_saved = ffn_fwd(0, x1)

    hk, h1 = stage_fwd(f_rmsnorm2, [x2], [row(P["kv_norm"]), P["attn_norm"][1:2]], [], [], NORM_TILE, "l1_norm", [BF16, BF16])
    kvp = matmul(hk, P["kv_w"][0], "nt", "l1_kv")
    p1 = matmul(h1, P["b_w_in"][0], "nn", "l1_in")
    kraw, qraw = Cols(kvp, DIL_WIDTH, 0), Cols(p1, DIL_WIDTH, 0)
    kgain, qgain = _per_head(P["kv_k_norm"], DIL_WIDTH // HEAD_DIM), _per_head(P["b_q_norm"], DIL_WIDTH // HEAD_DIM)
    ksh = stage_fwd(f_qkprep, [kraw], [kgain], [cos, sin], [seg, seg_t], ROW_TILE, "l1_kprep")[0]
    q = stage_fwd(f_qkprep, [qraw], [qgain], [cos, sin], [seg, seg_t], ROW_TILE, "l1_qprep")[0]
    outs, lses = [], []
    for gi, (_, d) in enumerate(DIL_GROUPS):
        og, lg = dil_fwd(q, ksh, kvp, gi, d, f"l1_dil{gi}")
        outs.append(og)
        lses.append(lg)
    ycat1 = stage_fwd(f_mix, outs + lses, [], [], [], ROW_TILE, "l1_mix", [BF16], into=Into(None, 2 * MEM_WIDTH, 0))[0]
    ycat1, mem1_saved = mem_fwd(1, Cols(p1, MEM_WIDTH, DIL_WIDTH // MEM_WIDTH), Into(ycat1, 2 * MEM_WIDTH, 1))
    x3 = matmul(ycat1, P["b_w_out"][0], "nt", "l1_out", residual=x2)
    x4, ffn1_saved = ffn_fwd(1, x3)
    loss_part, dx4, dx4_b = loss_head(x4, tgt)

    dx3, dx3_b, gn1, gup1, gcw1, gcb1, gdown1 = ffn_bwd(1, x3, ffn1_saved, dx4, dx4_b)
    dycat1 = matmul(dx3_b, P["b_w_out"][0], "nn", "l1_out_dx")
    G["b_w_out"] = [matmul(dx3_b, ycat1, "tn", "l1_out_dw")]
    dp1, gmn1, gmw1, gmq1, gmk1 = mem_bwd(1, mem1_saved, Cols(dycat1, MEM_WIDTH, 1), Into(None, D_MODEL, DIL_WIDTH // MEM_WIDTH))
    dmix, _ = stage_bwd(f_mix, outs + lses, [], [], [], [Cols(dycat1, MEM_WIDTH, 0)], ROW_TILE, "l1_mix_bwd")
    dq, dk, dv = None, [], []
    for gi, (_, d) in enumerate(DIL_GROUPS):
        dq, dk_g, dv_g = dil_bwd(q, ksh, kvp, dmix[gi], dmix[3 + gi], gi, d, f"l1_dil{gi}_bwd", dq_all=dq)
        dk.append(dk_g)
        dv.append(dv_g)
    dk, dv = jnp.concatenate(dk, axis=1), jnp.concatenate(dv, axis=1)
    _, (g_bq,), (dp1,) = stage_bwd(f_qkprep, [qraw], [qgain], [cos, sin], [seg, seg_t], [dq], ROW_TILE, "l1_qprep_bwd",
                                   bf16_copies=(0,), copy_into=Into(dp1, D_MODEL, 0))
    dkvp = jnp.pad(dv.astype(BF16), ((0, 0), (DIL_WIDTH, 0)))
    _, (g_kk,), (dkvp,) = stage_bwd(f_qkprep, [kraw], [kgain], [cos, sin], [seg, seg_t], [dk], ROW_TILE, "l1_kprep_bwd",
                                    bf16_copies=(0,), copy_into=Into(dkvp, 2 * DIL_WIDTH, 0))
    g_bq, g_kk = _sum_heads(g_bq), _sum_heads(g_kk)
    dh1 =matmul(dp1, P["b_w_in"][0], "nt", "l1_in_dx")
    G["b_w_in"] = [matmul(h1, dp1, "tn", "l1_in_dw")]
    dhk = matmul(dkvp, P["kv_w"][0], "nn", "l1_kv_dx")
    G["kv_w"] = [matmul(dkvp, hk, "tn", "l1_kv_dw")]
    (dx2,), (g_kvn, g_an1), (dx2_b,) = stage_bwd(f_rmsnorm2_res, [x2], [row(P["kv_norm"]), P["attn_norm"][1:2]], [], [],
                                                 [dhk, dh1, dx3], ROW_TILE, "l1_norm_bwd", bf16_copies=(0,))

    dx1, dx1_b, gn0, gup0, gcw0, gcb0, gdown0 = ffn_bwd(0, x1, ffn0_saved, dx2, dx2_b)
    dycat0 = matmul(dx1_b, P["a_w_out"][0], "nt", "l0_out_dx")
    G["a_w_out"] = [matmul(ycat0, dx1_b, "tn", "l0_out_dw")]
    dqmem0, gmn0, gmw0, gmq0, gmk0 = mem_bwd(0, mem0_saved, Cols(dycat0, MEM_WIDTH, RWKV_WIDTH // MEM_WIDTH))
    (dy_s, dr_a, dk_a, dv_a, dg), (g_lw, g_lb, g_rk) = stage_bwd(
        f_rwkv_post, [y_s, r, k2, v, g], post_ps, [], [seg, seg_t], [Cols(dycat0, RWKV_WIDTH, 0)], RW_TILE, "l0_rwkv_post_bwd")
    G["mem_w_kv"], G["ffn_w_up"], G["ffn_w_down"] = [gmw0, gmw1], [gup0, gup1], [gdown0, gdown1]
    (dr_b, dlw, dk_b, dv_b, dkk, db), G["_exchanged"] = rwkv_scan_bwd(*scan_in, h_states, dy_s,
                                                                      ex_grads(G) if ex_grads else Exchange())
    dpre, gpre = stage_bwd(f_rwkv_pre, pre_xs, pre_ps, [], [seg, seg_t],
                           [[dr_a, dr_b], dlw, [dk_a, dk_b], [dv_a, dv_b], dkk, db, dg], RW_TILE, "l0_rwkv_pre_bwd", with_prev=True)
    dp_rw = jnp.concatenate(dpre[:4], axis=1) + _shift_up(jnp.concatenate(dpre[4:], axis=1))
    dp0 = jnp.concatenate([dp_rw, dqmem0], axis=1).astype(BF16)
    dh0 = matmul(dp0, P["a_w_in"][0], "nn", "l0_in_dx")
    G["a_w_in"] = [matmul(dp0, h0, "tn", "l0_in_dw")]
    (dx0,), (g_an0,) = stage_bwd(f_rmsnorm_res, [x0], [P["attn_norm"][0:1]], [], [], [dh0, dx1], ROW_TILE, "l0_norm_bwd")

    G["attn_norm"] = jnp.concatenate([g_an0, g_an1], axis=0)
    G["a_mu"] = jnp.concatenate(gpre[:4], axis=1)
    G["a_w0"], G["a_w2"], G["a_a0"], G["a_a2"], G["a_g2"] = gpre[4], gpre[5][None, :64], gpre[6], gpre[7][None, 64:128], gpre[8][None, 128:]
    G["a_k_k"], G["a_k_a"] = gpre[9], gpre[10]
    G["a_r_k"] = g_rk.reshape(1, RWKV_HEADS, HEAD_DIM)
    G["a_lnx_w"], G["a_lnx_b"] = g_lw, g_lb
    G["kv_norm"], G["kv_k_norm"], G["b_q_norm"] = g_kvn.reshape(-1), g_kk.reshape(-1), g_bq
    G["mem_norm"] = jnp.concatenate([gmn0, gmn1], axis=0)
    G["mem_w_kv"] = [gmw0, gmw1]
    G["mem_q_norm"] = jnp.concatenate([gmq0, gmq1], axis=0)
    G["mem_k_norm"] = jnp.concatenate([gmk0, gmk1], axis=0)
    G["ffn_norm"] = jnp.concatenate([gn0, gn1], axis=0)
    G["ffn_w_up"] = [gup0, gup1]
    G["ffn_conv_w"] = jnp.stack([gcw0, gcw1])
    G["ffn_conv_b"] = jnp.concatenate([gcb0, gcb1], axis=0)
    G["ffn_w_down"] = [gdown0, gdown1]
    return loss_part, dx0, G


PARAMS = (("attn_norm", None), ("a_w_in", 2), ("a_mu", 1), ("a_w0", 1), ("a_w2", 2), ("a_a0", 1), ("a_a2", 2), ("a_g2", 2),
          ("a_k_k", 1), ("a_k_a", 1), ("a_r_k", None), ("a_lnx_w", 1), ("a_lnx_b", 1), ("a_w_out", 1), ("kv_norm", None),
          ("kv_w", 1), ("kv_k_norm", None), ("b_w_in", 1), ("b_q_norm", None), ("b_w_out", 2), ("mem_norm", None),
          ("mem_w_kv", 1), ("mem_q_norm", None), ("mem_k_norm", None), ("ffn_norm", None), ("ffn_w_up", 2),
          ("ffn_conv_w", 2), ("ffn_conv_b", None), ("ffn_w_down", 1))
BIG = ("a_w_in", "a_w_out", "kv_w", "b_w_in", "b_w_out", "mem_w_kv", "ffn_w_up", "ffn_w_down")
TRANSPOSED = ("a_w_in", "kv_w", "b_w_out", "ffn_w_up")
AXIS = dict(PARAMS)
SMALL = tuple(n for n, _ in PARAMS if n not in BIG)
SMALL_SHARDED = tuple(n for n in SMALL if AXIS[n] is not None)
PACK_QUANTUM = 256 * 128


def _from_shards(xs, axis):
    full = jnp.moveaxis(xs, 0, axis)
    sh = full.shape
    return full.reshape(sh[:axis] + (sh[axis] * sh[axis + 1],) + sh[axis + 2:])


def _to_shards(g, axis):
    sh = g.shape
    return jnp.moveaxis(g.reshape(sh[:axis] + (N_DEV, sh[axis] // N_DEV) + sh[axis + 1:]), axis, 0)


def _pack(parts, lead=0):
    ld = parts[0].shape[:lead]
    flat = jnp.concatenate([p.reshape(ld + (-1,)) for p in parts], axis=-1)
    pad = (-flat.shape[-1]) % PACK_QUANTUM
    flat = jnp.pad(flat, [(0, 0)] * lead + [(0, pad)])
    return flat.reshape(ld + (-1, 128))


def _unpack(packed, shapes, lead=0):
    ld = packed.shape[:lead]
    flat = packed.reshape(ld + (-1,))
    out, off = [], 0
    for s in shapes:
        n = math.prod(s)
        out.append(flat[..., off:off + n].reshape(ld + tuple(s)))
        off += n
    return out


def kernel(x, mem, attn_norm, a_w_in, a_mu, a_w0, a_w2, a_a0, a_a2, a_g2, a_k_k, a_k_a, a_r_k, a_lnx_w, a_lnx_b, a_w_out, kv_norm, kv_w, kv_k_norm, b_w_in, b_q_norm, b_w_out, mem_norm, mem_w_kv, mem_q_norm, mem_k_norm, ffn_norm, ffn_w_up, ffn_conv_w, ffn_conv_b, ffn_w_down, loss_target, m_attn_norm, m_a_w_in, m_a_mu, m_a_w0, m_a_w2, m_a_a0, m_a_a2, m_a_g2, m_a_k_k, m_a_k_a, m_a_r_k, m_a_lnx_w, m_a_lnx_b, m_a_w_out, m_kv_norm, m_kv_w, m_kv_k_norm, m_b_w_in, m_b_q_norm, m_b_w_out, m_mem_norm, m_mem_w_kv, m_mem_q_norm, m_mem_k_norm, m_ffn_norm, m_ffn_w_up, m_ffn_conv_w, m_ffn_conv_b, m_ffn_w_down, v_attn_norm, v_a_w_in, v_a_mu, v_a_w0, v_a_w2, v_a_a0, v_a_a2, v_a_g2, v_a_k_k, v_a_k_a, v_a_r_k, v_a_lnx_w, v_a_lnx_b, v_a_w_out, v_kv_norm, v_kv_w, v_kv_k_norm, v_b_w_in, v_b_q_norm, v_b_w_out, v_mem_norm, v_mem_w_kv, v_mem_q_norm, v_mem_k_norm, v_ffn_norm, v_ffn_w_up, v_ffn_conv_w, v_ffn_conv_b, v_ffn_w_down):
    names = [n for n, _ in PARAMS]
    vals = (attn_norm, a_w_in, a_mu, a_w0, a_w2, a_a0, a_a2, a_g2, a_k_k, a_k_a, a_r_k, a_lnx_w, a_lnx_b, a_w_out, kv_norm, kv_w, kv_k_norm, b_w_in, b_q_norm, b_w_out, mem_norm, mem_w_kv, mem_q_norm, mem_k_norm, ffn_norm, ffn_w_up, ffn_conv_w, ffn_conv_b, ffn_w_down)
    m_vals = (m_attn_norm, m_a_w_in, m_a_mu, m_a_w0, m_a_w2, m_a_a0, m_a_a2, m_a_g2, m_a_k_k, m_a_k_a, m_a_r_k, m_a_lnx_w, m_a_lnx_b, m_a_w_out, m_kv_norm, m_kv_w, m_kv_k_norm, m_b_w_in, m_b_q_norm, m_b_w_out, m_mem_norm, m_mem_w_kv, m_mem_q_norm, m_mem_k_norm, m_ffn_norm, m_ffn_w_up, m_ffn_conv_w, m_ffn_conv_b, m_ffn_w_down)
    v_vals = (v_attn_norm, v_a_w_in, v_a_mu, v_a_w0, v_a_w2, v_a_a0, v_a_a2, v_a_g2, v_a_k_k, v_a_k_a, v_a_r_k, v_a_lnx_w, v_a_lnx_b, v_a_w_out, v_kv_norm, v_kv_w, v_kv_k_norm, v_b_w_in, v_b_q_norm, v_b_w_out, v_mem_norm, v_mem_w_kv, v_mem_q_norm, v_mem_k_norm, v_ffn_norm, v_ffn_w_up, v_ffn_conv_w, v_ffn_conv_b, v_ffn_w_down)
    W, M, V = dict(zip(names, vals)), dict(zip(names, m_vals)), dict(zip(names, v_vals))
    layers = lambda D, n: [D[n]] if D[n].ndim == 2 else [D[n][i] for i in range(D[n].shape[0])]
    ax2 = lambda n: AXIS[n] - (W[n].ndim - 2)
    later =[(n, i) for n in BIG if n != "a_w_in" for i in range(len(layers(W, n)))]

    sent = lambda n, w: w.T if n in TRANSPOSED else w
    whole = lambda n, g: g.reshape(-1, g.shape[-1]) if n in TRANSPOSED else _from_shards(g, ax2(n))
    small_shapes = [W[n].shape for n in SMALL_SHARDED]
    got_w, got_small = exchange(Exchange(gathers=[sent("a_w_in", W["a_w_in"][0]).astype(BF16),
                                                  _pack([W[n] for n in SMALL_SHARDED])]), "gather_first")
    P = {n: W[n] for n in SMALL}
    P["a_w_in"] = [whole("a_w_in", got_w)]
    for n, s in zip(SMALL_SHARDED, _unpack(got_small, small_shapes, lead=1)):
        P[n] = _from_shards(s, AXIS[n])
    ex_weights =Exchange(gathers=[sent(n, layers(W, n)[i]).astype(BF16) for n, i in later])

    def weights_done(got):
        out = {}
        for (n, _), g in zip(later, got):
            out.setdefault(n, []).append(whole(n, g))
        return out

    slots = lambda G, n: jnp.stack([_to_shards(g, 0 if n in TRANSPOSED else ax2(n)) for g in G[n]], axis=1)
    later_names = [n for n in BIG if n != "a_w_in"]
    ex_grads = lambda G: Exchange(scatters=[slots(G, n) for n in later_names])
    loss_part, dx0, G = _local_step(x[0], mem[0], loss_target[0], P, ex_weights, weights_done, ex_grads)
    gparts = dict(zip(later_names, G.pop("_exchanged")))
    replicated = [n for n in SMALL if AXIS[n] is None]
    small_slots = _pack([_to_shards(G[n], AXIS[n]) for n in SMALL_SHARDED], lead=1)
    got_rep, gparts["a_w_in"], got_sharded = exchange(
        Exchange(gathers=[_pack([G[n] for n in replicated] + [loss_part[0:1, 0:1]])], scatters=[slots(G, "a_w_in"), small_slots]),
        "exchange_last")

    results = {}
    for n in BIG:
        rows = lambda z: z.reshape((-1,) + z.shape[-1:])
        gp = gparts[n].reshape((N_DEV, -1) + gparts[n].shape[-1:])
        if n in TRANSPOSED:
            gp = jnp.swapaxes(sum_parts(gp, f"sum_{n}").reshape(gparts[n].shape[1:]), -1, -2).reshape((1,) + rows(W[n]).shape)
        res = adamw(gp, rows(W[n]), rows(M[n]), rows(V[n]), f"adamw_{n}")
        results[n] = [r.reshape(W[n].shape) for r in res]
    *rep_sums, loss = _unpack(sum_parts(got_rep, "sum_replicated_grads"), [W[n].shape for n in replicated] + [()])
    g_mine = dict(zip(replicated, rep_sums))
    g_mine.update(zip(SMALL_SHARDED, _unpack(sum_parts(got_sharded, "sum_small_sharded_grads"), [W[n].shape for n in SMALL_SHARDED])))
    res = adamw(_pack([g_mine[n] for n in SMALL])[None], _pack([W[n] for n in SMALL]), _pack([M[n] for n in SMALL]),
                _pack([V[n] for n in SMALL]), "adamw_small")
    for n, parts in zip(SMALL, zip(*[_unpack(r, [W[n].shape for n in SMALL]) for r in res])):
        results[n] = list(parts)
    outs = [[results[n][j] for n in names] for j in range(4)]
    return (loss, dx0[None], *outs[0], *outs[1], *outs[2], *outs[3])
```

```python
import functools
import math

import jax
import jax.numpy as jnp
import numpy as np
from jax import lax
from jax.experimental import pallas as pl
from jax.experimental.pallas import tpu as pltpu

F32 = jnp.float32
BF16 = jnp.bfloat16
H3 =lax.Precision.HIGH

N_DEV = 8
D_MODEL = 1024
HEAD_DIM = 64
N_MEM = 256
MEM_HEADS = 4
MEM_WIDTH = 256
RWKV_HEADS = 12
RWKV_WIDTH = 768
SHIFT_WIDTH = 2560
DIL_GROUPS = ((128, 1), (512, 4), (2048, 16))
DIL_BLOCK = 128
DIL_WIDTH = 768
D_FF = 2816
RMS_EPS = 1e-6
LNX_EPS = 64e-5
NEG_INF = -1e30
ROPE_THETA = 10000.0
ADAM_LR, ADAM_B1, ADAM_B2, ADAM_EPS, ADAM_WD, ADAM_STEP = 0.001, 0.9, 0.999, 1e-08, 0.01, 10

CHUNK = 64
SCAN_GROUPS_FWD, SCAN_GROUPS_BWD = 1, 1
MM_TILE_CAP = 1408
VMEM_LIMIT_V7X = 48 * 1024 * 1024


def _cparams(sem):
    return pltpu.CompilerParams(dimension_semantics=sem, vmem_limit_bytes=VMEM_LIMIT_V7X)


def _pick(n, cands):
    for c in cands:
        if n % c == 0:
            return c
    return n


def _tile(n, cap):
    if n <= cap:
        return n
    for d in range(cap - cap % 128, 0, -128):
        if n % d == 0:
            return d
    return n


def _dg(a, b, ca, cb, batch):
    dims = (((ca,), (cb,)), ((0,), (0,))) if batch else (((ca,), (cb,)), ((), ()))
    return lax.dot_general(a.astype(BF16), b.astype(BF16), dims, preferred_element_type=F32)


@jax.custom_vjp
def mm_nn(a, b):
    n = a.ndim
    return _dg(a, b, n - 1, n - 2, n == 3)


def _mm_nn_fwd(a, b):
    return mm_nn(a, b), (a, b)


def _mm_nn_bwd(res, g):
    a, b = res
    n = a.ndim
    return _dg(g, b, n - 1, n - 1, n == 3), _dg(a, g, n - 2, n - 2, n == 3)


mm_nn.defvjp(_mm_nn_fwd, _mm_nn_bwd)


@jax.custom_vjp
def mm_nt(a, b):
    n = a.ndim
    return _dg(a, b, n - 1, n - 1, n == 3)


def _mm_nt_fwd(a, b):
    return mm_nt(a, b), (a, b)


def _mm_nt_bwd(res, g):
    a, b = res
    n = a.ndim
    return _dg(g, b, n - 1, n - 2, n == 3), _dg(g, a, n - 2, n - 2, n == 3)


mm_nt.defvjp(_mm_nt_fwd, _mm_nt_bwd)


def mmh(a, b):
    n = a.ndim
    dims = (((n - 1,), (n - 2,)), ((0,), (0,))) if n == 3 else (((1,), (0,)), ((), ()))
    return lax.dot_general(a, b, dims, precision=H3, preferred_element_type=F32)


def mmh_nt(a, b):
    n = a.ndim
    dims = (((n - 1,), (n - 1,)), ((0,), (0,))) if n == 3 else (((1,), (1,)), ((), ()))
    return lax.dot_general(a, b, dims, precision=H3, preferred_element_type=F32)


def mmh_tn(a, b):
    n = a.ndim
    dims = (((n - 2,), (n - 2,)), ((0,), (0,))) if n == 3 else (((0,), (0,)), ((), ()))
    return lax.dot_general(a, b, dims, precision=H3, preferred_element_type=F32)


def matmul(a, b, mode, name, residual=None):
    out_dtype = BF16 if mode == "tn" else F32
    if mode == "nn":
        (M, K), (_, N) = a.shape, b.shape
    elif mode == "nt":
        (M, K), (N, _) = a.shape, b.shape
    else:
        (K, M), (_, N) = a.shape, b.shape
    tm = _tile(M, 2048 if mode == "nn" else MM_TILE_CAP)
    tn = _tile(N, 512 if mode == "nn" else MM_TILE_CAP)
    tk = _tile(K, MM_TILE_CAP)
    nk = K // tk
    if mode == "nn":
        a_spec = pl.BlockSpec((tm, tk), lambda i, j, k: (i, k))
        b_spec = pl.BlockSpec((tk, tn), lambda i, j, k: (k, j))
        dims = (((1,), (0,)), ((), ()))
    elif mode == "nt":
        a_spec = pl.BlockSpec((tm, tk), lambda i, j, k: (i, k))
        b_spec = pl.BlockSpec((tn, tk), lambda i, j, k: (j, k))
        dims = (((1,), (1,)), ((), ()))
    else:
        a_spec = pl.BlockSpec((tk, tm), lambda i, j, k: (k, i))
        b_spec = pl.BlockSpec((tk, tn), lambda i, j, k: (k, j))
        dims = (((0,), (0,)), ((), ()))
    o_spec = pl.BlockSpec((tm, tn), lambda i, j, k: (i, j))
    has_res = residual is not None

    def body(*refs):
        if has_res:
            a_ref, b_ref, r_ref, o_ref, acc_ref = refs
        else:
            a_ref, b_ref, o_ref, acc_ref = refs
        k = pl.program_id(2)

        @pl.when(k == 0)
        def _():
            acc_ref[...] = jnp.zeros_like(acc_ref)

        acc_ref[...] += lax.dot_general(a_ref[...].astype(BF16), b_ref[...].astype(BF16), dims,
                                        preferred_element_type=F32)

        @pl.when(k == nk - 1)
        def _():
            if has_res:
                o_ref[...] = (acc_ref[...] + r_ref[...]).astype(out_dtype)
            else:
                o_ref[...] = acc_ref[...].astype(out_dtype)

    ins = [a, b] + ([residual] if has_res else [])
    in_specs = [a_spec, b_spec] + ([o_spec] if has_res else [])
    return pl.pallas_call(
        body, grid=(M // tm, N // tn, nk), in_specs=in_specs, out_specs=o_spec,
        out_shape=jax.ShapeDtypeStruct((M, N), out_dtype), scratch_shapes=[pltpu.VMEM((tm, tn), F32)],
        compiler_params=_cparams(("parallel", "parallel", "arbitrary")), name=name)(*ins)


class Cols:
    def __init__(self, arr, width, idx):
        self.arr, self.width, self.idx = arr, width, idx


def _arr(x):
    return x.arr if isinstance(x, Cols) else x


def _shape(x):
    return x.arr.shape[:-1] + (x.width,) if isinstance(x, Cols) else x.shape


def _col(x):
    return x.idx if isinstance(x, Cols) else 0


def _tok_spec(x, tile):
    shape, col = _shape(x), _col(x)
    return pl.BlockSpec(shape[:-2] + (tile, shape[-1]), lambda i: (0,) * (len(shape) - 2) + (i, col))


def _full_spec(x):
    shape, col = _shape(x), _col(x)
    return pl.BlockSpec(shape, lambda i: (0,) * (len(shape) - 1) + (col,))


def _halo_spec(x, tile):
    shape, col = _shape(x), _col(x)
    return pl.BlockSpec((8, shape[-1]), lambda i: (jnp.maximum(i * (tile // 8) - 1, 0), col))


def _blk(x, tile):
    shape = _shape(x)
    return jax.ShapeDtypeStruct(shape[:-2] + (tile, shape[-1]), _arr(x).dtype)


def _prev_rows(x, halo):
    rows = lax.broadcasted_iota(jnp.int32, (x.shape[0], 1), 0)
    before = jnp.where(pl.program_id(0) > 0, halo[7:8], 0.0)
    return jnp.where(rows == 0, before, pltpu.roll(x, 1, 0))


class Into:
    def __init__(self, buf, total, idx):
        self.buf, self.total, self.idx = buf, total, idx

    def place(self, shape, tile):
        idx = self.idx
        return shape.update(shape=(shape.shape[0], self.total)), pl.BlockSpec((tile, shape.shape[1]), lambda i: (i, idx))

    def operand(self, n_in, n_out_index):
        if self.buf is None:
            return [], [], {}
        return [self.buf], [_ANY], {n_in: n_out_index}


def stage_fwd(f, xs, ps, cts, cfs, tile, name, out_dtypes=None, with_prev=False, into=None):
    xs, ps, cts, cfs = list(xs), list(ps), list(cts), list(cfs)
    halos = xs if with_prev else []
    nx, nh, nct, np_ = len(xs), len(halos), len(cts), len(ps)
    T = _shape(xs[0])[-2]
    blk = [_blk(x, tile) for x in xs]
    out_avals = jax.eval_shape(f, *blk, *(blk if with_prev else []), *[_blk(p, _shape(p)[-2]) for p in ps],
                               *[_blk(c, tile) for c in cts], *[_blk(c, _shape(c)[-2]) for c in cfs])
    if out_dtypes is None:
        out_dtypes = [o.dtype for o in out_avals]
    out_shape = [jax.ShapeDtypeStruct(o.shape[:-2] + (T, o.shape[-1]), dt) for o, dt in zip(out_avals, out_dtypes)]
    out_specs = [_tok_spec(o, tile) for o in out_shape]
    n_in = nx + nh + nct + np_ + len(cfs)
    extra, extra_specs, alias = [], [], {}
    if into is not None:
        out_shape[0], out_specs[0] = into.place(out_shape[0], tile)
        extra, extra_specs, alias = into.operand(n_in, 0)

    def body(*refs):
        vals = [r[...] for r in refs[:n_in]]
        xv, hv, rest = vals[:nx], vals[nx:nx + nh], vals[nx + nh:]
        ctv, pv, cfv = rest[:nct], rest[nct:nct + np_], rest[nct + np_:]
        prev = [_prev_rows(x, h) for x, h in zip(xv, hv)]
        res = f(*xv, *prev, *pv, *ctv, *cfv)
        for o_ref, r in zip(refs[n_in + len(extra):], res):
            o_ref[...] = r.astype(o_ref.dtype)

    return pl.pallas_call(
        body, grid=(T // tile,),
        in_specs=([_tok_spec(x, tile) for x in xs] + [_halo_spec(x, tile) for x in halos] + [_tok_spec(c, tile) for c in cts]
                  + [_full_spec(p) for p in ps + cfs] + extra_specs),
        out_specs=out_specs, out_shape=out_shape, input_output_aliases=alias,
        compiler_params=_cparams(("parallel",)), name=name)(*[_arr(a) for a in xs + halos + cts + ps + cfs], *extra)


def stage_bwd(f, xs, ps, cts, cfs, gs, tile, name, bf16_copies=(), with_prev=False, copy_into=None):
    xs, ps, cts, cfs = list(xs), list(ps), list(cts), list(cfs)
    gs = [list(g) if isinstance(g, (list, tuple)) else [g] for g in gs]
    g_flat = [a for g in gs for a in g]
    halos = xs if with_prev else []
    nx, nh, nct, ng, np_ = len(xs), len(halos), len(cts), len(g_flat), len(ps)
    T = _shape(xs[0])[-2]
    dx_like = xs + halos
    out_shape = ([jax.ShapeDtypeStruct(_shape(x), F32) for x in dx_like] + [jax.ShapeDtypeStruct(_shape(p), F32) for p in ps]
                 + [jax.ShapeDtypeStruct(_shape(xs[i]), BF16) for i in bf16_copies])
    n_in = nx + nh + nct + ng + np_ + len(cfs)
    ndx = nx + nh
    plain = lambda x: jax.ShapeDtypeStruct(_shape(x), F32)
    out_specs = ([_tok_spec(plain(x), tile) for x in dx_like] + [_full_spec(plain(p)) for p in ps]
                 + [_tok_spec(plain(xs[i]), tile) for i in bf16_copies])
    extra, extra_specs, alias = [], [], {}
    if copy_into is not None:
        out_shape[ndx + np_], out_specs[ndx + np_] = copy_into.place(out_shape[ndx + np_], tile)
        extra, extra_specs, alias = copy_into.operand(n_in, ndx + np_)

    def body(*refs):
        vals = [r[...] for r in refs[:n_in]]
        outs = refs[n_in + len(extra):]
        xv, hv, rest = vals[:nx], vals[nx:nx + nh], vals[nx + nh:]
        ctv, gparts, pv, cfv = rest[:nct], rest[nct:nct + ng], rest[nct + ng:nct + ng + np_], rest[nct + ng + np_:]
        gv = []
        for g in gs:
            gv.append(functools.reduce(lambda a, b: a + b, gparts[:len(g)]))
            gparts = gparts[len(g):]
        prev = [_prev_rows(x, h) for x, h in zip(xv, hv)]
        _, vjp = jax.vjp(lambda *xp: f(*xp, *ctv, *cfv), *xv, *prev, *pv)
        d = vjp(tuple(gv))
        for o_ref, r in zip(outs[:ndx], d[:ndx]):
            o_ref[...] = r
        for o_ref, i in zip(outs[ndx + np_:], bf16_copies):
            o_ref[...] = d[i].astype(BF16)

        @pl.when(pl.program_id(0) == 0)
        def _():
            for o_ref in outs[ndx:ndx + np_]:
                o_ref[...] = jnp.zeros_like(o_ref)

        for o_ref, r in zip(outs[ndx:ndx + np_], d[ndx:]):
            o_ref[...] += r

    res = pl.pallas_call(
        body, grid=(T // tile,),
        in_specs=([_tok_spec(x, tile) for x in xs] + [_halo_spec(x, tile) for x in halos]
                  + [_tok_spec(c, tile) for c in cts + g_flat] + [_full_spec(p) for p in ps + cfs] + extra_specs),
        out_specs=out_specs, out_shape=out_shape, input_output_aliases=alias,
        compiler_params=_cparams(("arbitrary",)), name=name)(*[_arr(a) for a in xs + halos + cts + g_flat + ps + cfs], *extra)
    if bf16_copies:
        return list(res[:ndx]), list(res[ndx:ndx + np_]), list(res[ndx + np_:])
    return list(res[:ndx]), list(res[ndx:])


def _rms(x, g, eps=RMS_EPS):
    return x * lax.rsqrt(jnp.mean(x * x, axis=-1, keepdims=True) + eps) * g


def f_rmsnorm(x, g):
    return (_rms(x, g),)


def f_rmsnorm_res(x, g):
    return _rms(x, g), x


def f_rmsnorm2(x, g1, g2):
    n = x * lax.rsqrt(jnp.mean(x * x, axis=-1, keepdims=True) + RMS_EPS)
    return n * g1, n * g2


def f_rmsnorm2_res(x, g1, g2):
    return f_rmsnorm2(x, g1, g2) + (x,)


def _sigmoid(x):
    return 1.0 / (1.0 + jnp.exp(-x))


def _softplus(x):
    return jnp.maximum(x, 0.0) + jnp.log(1.0 + jnp.exp(-jnp.abs(x)))


def f_rwkv_pre(pr, pk, pv, pl_, qr, qk, qv, ql, mu_r, mu_k, mu_v, mu_l, w0, w2, a0, a2, g2, k_k, k_a, seg, seg_t):
    xr = pr + (qr - pr) * mu_r
    xk = pk + (qk - pk) * mu_k
    xv = pv + (qv - pv) * mu_v
    xl = pl_ + (ql - pl_) * mu_l
    w_log = -_softplus(-(w0 + mm_nn(jnp.tanh(xl), w2))) - 0.5
    lw = -jnp.exp(w_log)
    a = _sigmoid(a0 + mm_nn(xl, a2))
    g = mm_nn(_sigmoid(xl), g2)
    kkr = xk * k_k
    inv = lax.rsqrt(jnp.maximum(mmh(kkr * kkr, seg), 1e-24))
    kk = kkr * mmh(inv, seg_t)
    k2 = xk * (1.0 + (a - 1.0) * k_a)
    return xr, lw, k2, xv, kk, kk * a, g


def f_rwkv_post(y, r, k2, v, g, lnx_w, lnx_b, r_k, seg, seg_t):
    inv_n = 1.0 / HEAD_DIM
    m = mmh(mmh(y, seg) * inv_n, seg_t)
    yc = y - m
    rstd = lax.rsqrt(mmh(yc * yc, seg) * inv_n + LNX_EPS)
    yn = yc * mmh(rstd, seg_t) * lnx_w + lnx_b
    bonus = mmh(mmh(r * k2 * r_k, seg), seg_t) * v
    return ((yn + bonus) * g,)


def _headnorm(z, g, seg, seg_t):
    ms = mmh(z * z, seg) * (1.0 / HEAD_DIM)
    return z * mmh(lax.rsqrt(ms + RMS_EPS), seg_t) * g


def f_headnorm(z, g, seg, seg_t):
    return (_headnorm(z, g, seg, seg_t),)


def _rot_half(z):
    w = z.shape[1]
    half = HEAD_DIM // 2
    lane = lax.broadcasted_iota(jnp.int32, (1, w), 1)
    return jnp.where((lane & (HEAD_DIM - 1)) < half, -pltpu.roll(z, w - half, 1), pltpu.roll(z, half, 1))


@jax.custom_vjp
def _rotate_half(z):
    return _rot_half(z)


_rotate_half.defvjp(lambda z: (_rot_half(z), None), lambda _, g: (-_rot_half(g),))


def f_qkprep(z, g, cos, sin, seg, seg_t):
    zn = _headnorm(z, g, seg, seg_t)
    pairs = z.shape[1] // cos.shape[1]
    return (zn * jnp.tile(cos, (1, pairs)) + _rotate_half(zn) * jnp.tile(sin, (1, pairs)),)


def _head_mask(width, h):
    lane = lax.broadcasted_iota(jnp.int32, (1, width), 1)
    return jnp.where((lane >> 6) == h, jnp.ones((), F32), 0.0)


def f_memattn(q, k, v, q_norm, seg, seg_t):
    qn = _headnorm(q, q_norm, seg, seg_t)
    out = jnp.zeros_like(q)
    for h in range(MEM_HEADS):
        m = _head_mask(MEM_WIDTH, h)
        s = mm_nt(qn * m, k) * (1.0 / math.sqrt(HEAD_DIM))
        s = s - jnp.max(s, axis=-1, keepdims=True)
        p = jnp.exp(s)
        p = p / jnp.sum(p, axis=-1, keepdims=True)
        out = out + mm_nn(p, v) * m
    return (out,)


def f_mix(o1, o2, o3, l1, l2, l3):
    mx = jnp.maximum(jnp.maximum(l1, l2), l3)
    e1, e2, e3 = jnp.exp(l1 - mx), jnp.exp(l2 - mx), jnp.exp(l3 - mx)
    return ((e1 * o1 + e2 * o2 + e3 * o3) / (e1 + e2 + e3),)


def _chunk_masks(L):
    t = lax.broadcasted_iota(jnp.int32, (L, L), 0)
    s = lax.broadcasted_iota(jnp.int32, (L, L), 1)
    return t, s


def _unit_lower_inverse(a):
    L = a.shape[-1]
    t, s = _chunk_masks(L)
    one = jnp.ones((), F32)
    blk = lambda sh: jnp.where((t >> sh) == (s >> sh), one, 0.0)
    n0 = a * blk(3)
    x = jnp.where(t == s, one, 0.0) - n0
    n2 = mmh(n0, n0)
    x = x + mmh(x, n2)
    x = x + mmh(x, mmh(n2, n2))
    for sh in (3, 4, 5):
        if (1 << sh) >= L:
            break
        off = a * (blk(sh + 1) - blk(sh))
        x = x - mmh(x, mmh(off, x))
    return x


@jax.custom_vjp
def _inverse_known(a, x):
    return x


def _inverse_known_fwd(a, x):
    return x, x


def _inverse_known_bwd(x, dx):
    return -mmh_nt(mmh_tn(x, dx), x), jnp.zeros_like(x)


_inverse_known.defvjp(_inverse_known_fwd, _inverse_known_bwd)


def _running_sum(x, reverse):
    L = x.shape[1]
    pos = lax.broadcasted_iota(jnp.int32, (1, L, 1), 1)
    step = 1
    while step < L:
        if reverse:
            x = x + jnp.where(pos < L - step, pltpu.roll(x, L - step, 1), 0.0)
        else:
            x = x + jnp.where(pos >= step, pltpu.roll(x, step, 1), 0.0)
        step *= 2
    return x


@jax.custom_vjp
def _cumsum_tokens(x):
    return _running_sum(x, False)


_cumsum_tokens.defvjp(lambda x: (_running_sum(x, False), None), lambda _, g: (_running_sum(g, True),))


def f_rwkv_chunk(s0, r, lw, k, v, kk, b, x_known=None):
    H, L, _ = r.shape
    t, s = _chunk_masks(L)
    one = jnp.ones((), F32)
    incl = jnp.where(t >= s, one, 0.0)
    strict = jnp.where(t > s, one, 0.0)
    cum = _cumsum_tokens(lw)
    w_in = jnp.exp(cum)
    w_ex = jnp.exp(cum - lw)
    w_inv = jnp.exp(-cum)
    rt, kkt, kt, bt = r * w_in, kk * w_ex, k * w_inv, b * w_inv
    a_b = mmh_nt(kkt, bt) * strict
    a_k = mmh_nt(kkt, kt) * strict
    m_k = mmh_nt(rt, kt) * incl
    m_b = mmh_nt(rt, bt) * incl
    x = _unit_lower_inverse(a_b) if x_known is None else _inverse_known(a_b, x_known)
    u = mmh(x, mmh_nt(kkt, s0) + mmh(a_k, v))
    y = mmh_nt(rt, s0) + mmh(m_k, v) - mmh(m_b, u)
    w_last = jnp.exp(jnp.sum(lw, axis=1, keepdims=True))
    s1 = (s0 + mmh_tn(v, kt) - mmh_tn(u, bt)) * w_last
    return y, s1, x


def _ex_split(ex, refs, n_in, n_out):
    n = ex.n
    ins, ex_in = refs[:n_in], refs[n_in:n_in + n]
    outs, ex_out = refs[n_in + n:n_in + n + n_out], refs[n_in + n + n_out:n_in + 2 * n + n_out]
    rest = refs[n_in + 2 * n + n_out:]
    return ins, outs, rest[:len(rest) - 3], (ex_in, ex_out) + tuple(rest[len(rest) - 3:])


def _split_heads(x):
    return jnp.stack([x[:, h * HEAD_DIM:(h + 1) * HEAD_DIM] for h in range(x.shape[1] // HEAD_DIM)], axis=0)


def _merge_heads(x):
    return jnp.concatenate([x[h] for h in range(x.shape[0])], axis=1)


def rwkv_scan_fwd(r, lw, k, v, kk, b, ex):
    T, N = r.shape[0], HEAD_DIM
    H = r.shape[1] // N
    groups = SCAN_GROUPS_FWD
    nc, hg = T // CHUNK, H // groups
    seq = pl.BlockSpec((CHUNK, hg * N), lambda g, c: (c, g))

    def body(*refs):
        (r_ref, lw_ref, k_ref, v_ref, kk_ref, b_ref), (y_ref, hs_ref, xs_ref), (h_scr,), ex_refs = _ex_split(ex, refs, 6, 3)
        g, c = pl.program_id(0), pl.program_id(1)

        @pl.when(jnp.logical_and(g == 0, c == 0))
        def _():
            ex.start(*ex_refs)

        @pl.when(c == 0)
        def _():
            h_scr[...] = jnp.zeros_like(h_scr)

        h0 = h_scr[...]
        hs_ref[0] = h0
        y, h1, x = f_rwkv_chunk(h0, *[_split_heads(z[...]) for z in (r_ref, lw_ref, k_ref, v_ref, kk_ref, b_ref)])
        y_ref[...] = _merge_heads(y)
        xs_ref[0] = x
        h_scr[...] = h1

        @pl.when(jnp.logical_and(g == groups - 1, c == (7 * nc) // 8))
        def _():
            ex.forward(*ex_refs)

        @pl.when(jnp.logical_and(g == groups - 1, c == nc - 1))
        def _():
            ex.wait(*ex_refs)

    res = pl.pallas_call(
        body, grid=(groups, nc), in_specs=[seq] * 6 + [_ANY] * ex.n,
        out_specs=[seq, pl.BlockSpec((1, hg, N, N), lambda g, c: (c, g, 0, 0)),
                   pl.BlockSpec((1, hg, CHUNK, CHUNK), lambda g, c: (c, g, 0, 0))] + [_ANY] * ex.n,
        out_shape=[jax.ShapeDtypeStruct((T, H * N), F32), jax.ShapeDtypeStruct((nc, H, N, N), F32),
                   jax.ShapeDtypeStruct((nc, H, CHUNK, CHUNK), F32)] + ex.out_shape(),
        scratch_shapes=[pltpu.VMEM((hg, N, N), F32)] + ex.scratch(),
        compiler_params=_cparams(("arbitrary", "arbitrary")), name="rwkv_scan_fwd")(r, lw, k, v, kk, b, *ex.operands())
    return res[0], (res[1], res[2]), list(res[3:])


def rwkv_scan_bwd(r, lw, k, v, kk, b, saved, dy, ex):
    T, N = r.shape[0], HEAD_DIM
    H = r.shape[1] // N
    groups = SCAN_GROUPS_BWD
    nc, hg = T // CHUNK, H // groups
    seq = pl.BlockSpec((CHUNK, hg * N), lambda g, c: (nc - 1 - c, g))
    state = pl.BlockSpec((1, hg, N, N), lambda g, c: (nc - 1 - c, g, 0, 0))

    def body(*refs):
        (r_ref, lw_ref, k_ref, v_ref, kk_ref, b_ref, hs_ref, xs_ref, dy_ref), outs, (dh_scr,), ex_refs = _ex_split(ex, refs, 9, 6)
        g, c = pl.program_id(0), pl.program_id(1)

        @pl.when(jnp.logical_and(g == 0, c == 0))
        def _():
            ex.start(*ex_refs)

        @pl.when(c == 0)
        def _():
            dh_scr[...] = jnp.zeros_like(dh_scr)

        x_known = xs_ref[0]
        _, vjp = jax.vjp(lambda *a: f_rwkv_chunk(*a, x_known=x_known)[:2], hs_ref[0],
                         *[_split_heads(z[...]) for z in (r_ref, lw_ref, k_ref, v_ref, kk_ref, b_ref)])
        d = vjp((_split_heads(dy_ref[...]), dh_scr[...]))
        dh_scr[...] = d[0]
        for o_ref, dz in zip(outs, d[1:]):
            o_ref[...] = _merge_heads(dz)

        @pl.when(jnp.logical_and(g == groups - 1, c == nc - 1))
        def _():
            ex.forward(*ex_refs)
            ex.wait(*ex_refs)

    res = pl.pallas_call(
        body, grid=(groups, nc),
        in_specs=[seq] * 6 + [state, state, seq] + [_ANY] * ex.n,
        out_specs=[seq] * 6 + [_ANY] * ex.n, out_shape=[jax.ShapeDtypeStruct((T, H * N), F32)] * 6 + ex.out_shape(),
        scratch_shapes=[pltpu.VMEM((hg, N, N), F32)] + ex.scratch(),
        compiler_params=_cparams(("arbitrary", "arbitrary")), name="rwkv_scan_bwd")(r, lw, k, v, kk, b, *saved, dy, *ex.operands())
    return list(res[:6]), list(res[6:])


GROUP_COLS = 4 * HEAD_DIM


def _f_dilattn(has_prev, q, kc, kp, vc, vp):
    scale = 1.0 / math.sqrt(HEAD_DIM)
    i = lax.broadcasted_iota(jnp.int32, (DIL_BLOCK, DIL_BLOCK), 0)
    j = lax.broadcasted_iota(jnp.int32, (DIL_BLOCK, DIL_BLOCK), 1)
    o, l = jnp.zeros_like(q), jnp.zeros_like(q)
    for h in range(q.shape[1] // HEAD_DIM):
        m = _head_mask(q.shape[1], h)
        sc = jnp.where(j <= i, mm_nt(q * m, kc) * scale, NEG_INF)
        sp = jnp.where(jnp.logical_and(i <= j, has_prev), mm_nt(q * m, kp) * scale, NEG_INF)
        mx = jnp.maximum(jnp.max(sc, axis=-1, keepdims=True), jnp.max(sp, axis=-1, keepdims=True))
        pc, pp = jnp.exp(sc - mx), jnp.exp(sp - mx)
        den = jnp.sum(pc, axis=-1, keepdims=True) + jnp.sum(pp, axis=-1, keepdims=True)
        o = o + (mm_nn(pc, vc) + mm_nn(pp, vp)) / den * m
        l = l + (mx + jnp.log(den)) * m
    return o, l


def _dil_specs(gi, d):
    parts = 1 if d == 1 else 2
    blk = (DIL_BLOCK * d, GROUP_COLS // parts)
    at = lambda col: (lambda p, n: (n, col * parts + p))
    before = lambda col: (lambda p, n: (jnp.maximum(n - 1, 0), col * parts + p))
    v0 = DIL_WIDTH // GROUP_COLS + gi
    q = pl.BlockSpec(blk, at(gi))
    kc, kp = pl.BlockSpec(blk, at(gi)), pl.BlockSpec(blk, before(gi))
    vc, vp = pl.BlockSpec(blk, at(v0)), pl.BlockSpec(blk, before(v0))
    out = pl.BlockSpec(blk, at(0))
    together = min(d, 2)
    return (q, kc, kp, vc, vp, out), parts, together


def _residue_rows(r, d):
    return pl.ds(r, DIL_BLOCK, stride=d) if d > 1 else pl.ds(0, DIL_BLOCK)


def dil_fwd(q, k, kv, gi, d, name):
    T = q.shape[0]
    (qs, kc, kp, vc, vp, out), parts, together = _dil_specs(gi, d)

    def body(q_ref, kc_ref, kp_ref, vc_ref, vp_ref, o_ref, l_ref):
        has_prev = pl.program_id(1) > 0

        def residues(it, carry):
            rows = [_residue_rows(it * together + a, d) for a in range(together)]
            ins = [[ref[rw, :] for ref in (q_ref, kc_ref, kp_ref, vc_ref, vp_ref)] for rw in rows]
            res = [_f_dilattn(has_prev, *x) for x in ins]
            for rw, (o, l) in zip(rows, res):
                o_ref[rw, :] = o
                l_ref[rw, :] = l
            return carry

        lax.fori_loop(0, d // together, residues, 0)

    shape = jax.ShapeDtypeStruct((T, 4 * HEAD_DIM), F32)
    return pl.pallas_call(
        body, grid=(parts, T // (DIL_BLOCK * d)), in_specs=[qs, kc, kp, vc, vp], out_specs=[out, out], out_shape=[shape, shape],
        compiler_params=_cparams(("parallel", "parallel")), name=name)(q, k, k, kv, kv)


def dil_bwd(q, k, kv, do, dl, gi, d, name):
    T = q.shape[0]
    (qs, kc, kp, vc, vp, out), parts, together = _dil_specs(gi, d)

    def body(q_ref, kc_ref, kp_ref, vc_ref, vp_ref, do_ref, dl_ref, *outs):
        f = functools.partial(_f_dilattn, pl.program_id(1) > 0)

        def residues(it, carry):
            rows = [_residue_rows(it * together + a, d) for a in range(together)]
            ins = [[ref[rw, :] for ref in (q_ref, kc_ref, kp_ref, vc_ref, vp_ref, do_ref, dl_ref)] for rw in rows]
            res = [jax.vjp(f, *x[:5])[1]((x[5], x[6])) for x in ins]
            for rw, gs in zip(rows, res):
                for o_ref, g in zip(outs, gs):
                    o_ref[rw, :] = g
            return carry

        lax.fori_loop(0, d // together, residues, 0)

    shape = jax.ShapeDtypeStruct((T, 4 * HEAD_DIM), F32)
    dq, dkc, dkp, dvc, dvp = pl.pallas_call(
        body, grid=(parts, T // (DIL_BLOCK * d)), in_specs=[qs, kc, kp, vc, vp, out, out], out_specs=[out] * 5, out_shape=[shape] * 5,
        compiler_params=_cparams(("parallel", "parallel")), name=name)(q, k, k, kv, kv, do, dl)

    def own_plus_next(c, p):
        return c + jnp.concatenate([p[DIL_BLOCK * d:], jnp.zeros_like(p[:DIL_BLOCK * d])], axis=0)

    return dq, own_plus_next(dkc, dkp), own_plus_next(dvc, dvp)


CONV_TILE = 256


def _conv3(before, u, w, b):
    ue = jnp.concatenate([before, u], axis=0)
    s1, s2 = pltpu.roll(ue, 1, 0)[8:], pltpu.roll(ue, 2, 0)[8:]
    return b + w[0:1] * s2 + w[1:2] * s1 + w[2:3] * u, s1, s2


def _conv_halves(u_ref, h_ref, cw_ref, cb_ref):
    F = D_FF
    res = []
    for lo in (0, F):
        before = jnp.where(pl.program_id(0) > 0, h_ref[:, lo:lo + F], 0.0)
        u = u_ref[:, lo:lo + F]
        res.append((u,) + _conv3(before, u, cw_ref[:, lo:lo + F], cb_ref[:, lo:lo + F]))
    return res


def _halo_before(C):
    return pl.BlockSpec((8, C), lambda i: (jnp.maximum(i * (CONV_TILE // 8) - 1, 0), 0))


def convgate_fwd(u, cw, cb, name):
    T, C = u.shape
    F = C // 2

    def body(u_ref, h_ref, cw_ref, cb_ref, z_ref):
        (_, cg, _, _), (_, cv, _, _) = _conv_halves(u_ref, h_ref, cw_ref, cb_ref)
        z_ref[...] = (cg * _sigmoid(cg) * cv).astype(BF16)

    return pl.pallas_call(
        body, grid=(T // CONV_TILE,),
        in_specs=[pl.BlockSpec((CONV_TILE, C), lambda i: (i, 0)), _halo_before(C), _full_spec(cw), _full_spec(cb)],
        out_specs=pl.BlockSpec((CONV_TILE, F), lambda i: (i, 0)), out_shape=jax.ShapeDtypeStruct((T, F), BF16),
        compiler_params=_cparams(("parallel",)), name=name)(u, u, cw, cb)


def convgate_bwd(u, cw, cb, dz, name):
    T, C = u.shape
    F = C // 2
    n = T // CONV_TILE
    E = CONV_TILE + 8

    def body(u_ref, hb_ref, ha_ref, cw_ref, cb_ref, dz_ref, dza_ref, du_ref, dcw_ref, dcb_ref):
        i = pl.program_id(0)
        dze = jnp.concatenate([dz_ref[...], jnp.where(i < n - 1, dza_ref[...], 0.0)], axis=0)

        @pl.when(i == 0)
        def _():
            dcw_ref[...] = jnp.zeros_like(dcw_ref)
            dcb_ref[...] = jnp.zeros_like(dcb_ref)

        halves = []
        for lo in (0, F):
            sl = slice(lo, lo + F)
            ue = jnp.concatenate([u_ref[:, sl], ha_ref[:, sl]], axis=0)
            c, s1, s2 = _conv3(jnp.where(i > 0, hb_ref[:, sl], 0.0), ue, cw_ref[:, sl], cb_ref[:, sl])
            halves.append((sl, ue, c, s1, s2))
        (_, _, cg, _, _), (_, _, cv, _, _) = halves
        sg = _sigmoid(cg)
        dcs = (dze * cv * sg * (1.0 + cg * (1.0 - sg)), dze * cg * sg)
        for (sl, ue, _, s1, s2), dc in zip(halves, dcs):
            own = lambda z: z[:CONV_TILE]
            dcb_ref[:, sl] += jnp.sum(own(dc), axis=0, keepdims=True)
            dcw_ref[0:1, sl] += jnp.sum(own(dc * s2), axis=0, keepdims=True)
            dcw_ref[1:2, sl] += jnp.sum(own(dc * s1), axis=0, keepdims=True)
            dcw_ref[2:3, sl] += jnp.sum(own(dc * ue), axis=0, keepdims=True)
            du = cw_ref[2:3, sl] * dc + cw_ref[1:2, sl] * pltpu.roll(dc, E - 1, 0) + cw_ref[0:1, sl] * pltpu.roll(dc, E - 2, 0)
            du_ref[:, sl] = own(du).astype(BF16)

    after = lambda w: pl.BlockSpec((8, w), lambda i: (jnp.minimum((i + 1) * (CONV_TILE // 8), T // 8 - 1), 0))
    return pl.pallas_call(
        body, grid=(n,),
        in_specs=[pl.BlockSpec((CONV_TILE, C), lambda i: (i, 0)), _halo_before(C), after(C), _full_spec(cw), _full_spec(cb),
                  pl.BlockSpec((CONV_TILE, F), lambda i: (i, 0)), after(F)],
        out_specs=[pl.BlockSpec((CONV_TILE, C), lambda i: (i, 0)), _full_spec(cw), _full_spec(cb)],
        out_shape=[jax.ShapeDtypeStruct((T, C), BF16), jax.ShapeDtypeStruct(cw.shape, F32), jax.ShapeDtypeStruct(cb.shape, F32)],
        compiler_params=_cparams(("arbitrary",)), name=name)(u, u, u, cw, cb, dz, dz)


def loss_head(y, tgt):
    T, D = y.shape
    tile = ROW_TILE

    def body(y_ref, t_ref, l_ref, d_ref, db_ref):
        d = y_ref[...] - t_ref[...]
        d_ref[...] = d * (1.0 / D)
        db_ref[...] = (d * (1.0 / D)).astype(BF16)

        @pl.when(pl.program_id(0) == 0)
        def _():
            l_ref[...] = jnp.zeros_like(l_ref)

        l_ref[...] += (0.5 / D) * jnp.sum(d * d)

    row = pl.BlockSpec((tile, D), lambda i: (i, 0))
    return pl.pallas_call(
        body, grid=(T // tile,), in_specs=[row, row], out_specs=[pl.BlockSpec((8, 128), lambda i: (0, 0)), row, row],
        out_shape=[jax.ShapeDtypeStruct((8, 128), F32), jax.ShapeDtypeStruct((T, D), F32), jax.ShapeDtypeStruct((T, D), BF16)],
        compiler_params=_cparams(("arbitrary",)), name="loss_head")(y, tgt)


def sum_parts(parts, name):
    S, R, C = parts.shape
    tile = _pick(R, (512, 256, 128, 64, 32, 16, 8))

    def body(p_ref, o_ref):
        acc = p_ref[0].astype(F32)
        for s in range(1, S):
            acc = acc + p_ref[s].astype(F32)
        o_ref[...] = acc

    return pl.pallas_call(
        body, grid=(R // tile,), in_specs=[pl.BlockSpec((S, tile, C), lambda i: (0, i, 0))],
        out_specs=pl.BlockSpec((tile, C), lambda i: (i, 0)), out_shape=jax.ShapeDtypeStruct((R, C), F32),
        compiler_params=_cparams(("parallel",)), name=name)(parts)


def adamw(gparts, w, m, v, name):
    S, R, C = gparts.shape
    tile = _pick(R, (512, 256, 128, 64, 32, 16, 8))
    c1 = 1.0 / (1.0 - ADAM_B1 ** ADAM_STEP)
    c2 = 1.0 / (1.0 - ADAM_B2 ** ADAM_STEP)

    def body(g_ref, w_ref, m_ref, v_ref, go_ref, d_ref, mo_ref, vo_ref):
        g = g_ref[0].astype(F32)
        for s in range(1, S):
            g = g + g_ref[s].astype(F32)
        m1 = ADAM_B1 * m_ref[...] + (1.0 - ADAM_B1) * g
        v1 = ADAM_B2 * v_ref[...] + (1.0 - ADAM_B2) * (g * g)
        go_ref[...] = g
        mo_ref[...] = m1
        vo_ref[...] = v1
        d_ref[...] = -ADAM_LR * ((m1 * c1) / (jnp.sqrt(v1 * c2) + ADAM_EPS) + ADAM_WD * w_ref[...])

    row = pl.BlockSpec((tile, C), lambda i: (i, 0))
    return pl.pallas_call(
        body, grid=(R // tile,), in_specs=[pl.BlockSpec((S, tile, C), lambda i: (0, i, 0)), row, row, row],
        out_specs=[row] * 4, out_shape=[jax.ShapeDtypeStruct((R, C), F32)] * 4,
        compiler_params=_cparams(("parallel",)), name=name)(gparts, w, m, v)


def _peers():
    x, y, c = lax.axis_index("x"), lax.axis_index("y"), lax.axis_index("c")
    peers = []
    for k in range(1, N_DEV):
        px = 1 - x if k & 4 else x
        py = 1 - y if k & 2 else y
        pc = 1 - c if k & 1 else c
        peers.append(((px, py, pc), 4 * px + 2 * py + pc))
    return 4 * x + 2 * y + c, peers


_ANY = pl.BlockSpec(memory_space=pl.ANY)


class Exchange:
    def __init__(self, gathers=(), scatters=()):
        self.gathers, self.scatters = list(gathers), list(scatters)
        self.n = len(self.gathers) + len(self.scatters)

    def operands(self):
        return self.gathers + self.scatters

    def out_shape(self):
        return ([jax.ShapeDtypeStruct((N_DEV,) + x.shape, x.dtype) for x in self.gathers]
                + [jax.ShapeDtypeStruct(x.shape, x.dtype) for x in self.scatters])

    def scratch(self):
        n = max(self.n, 1)
        return [pltpu.SemaphoreType.DMA((7 * n,)), pltpu.SemaphoreType.DMA((7 * n,)), pltpu.SemaphoreType.DMA((n,))]

    def _copies(self, in_refs, out_refs, send_sems, recv_sems, local_sems):
        me, peers = _peers()
        ng = len(self.gathers)
        local, sends, recvs = [], [], []
        for a in range(self.n):
            x, o = in_refs[a], out_refs[a]
            mine = x if a < ng else x.at[me]
            local.append(pltpu.make_async_copy(mine, o.at[me], local_sems.at[a]))
            s_a, r_a = {}, {}
            for k in range(1, N_DEV):
                peer, slot = peers[k - 1]
                sems = dict(send_sem=send_sems.at[7 * a + k - 1], recv_sem=recv_sems.at[7 * a + k - 1],
                            device_id_type=pl.DeviceIdType.MESH)
                if a >= ng:
                    s_a[k] = pltpu.make_async_remote_copy(src_ref=x.at[slot], dst_ref=o.at[me], device_id=peer, **sems)
                elif k in FORWARDED:
                    came = o.at[peers[k - 2][1]]
                    s_a[k] = pltpu.make_async_remote_copy(src_ref=came, dst_ref=came, device_id=peers[0][0], **sems)
                else:
                    s_a[k] = pltpu.make_async_remote_copy(src_ref=x, dst_ref=o.at[me], device_id=peer, **sems)
                r_a[k] = pltpu.make_async_remote_copy(src_ref=mine, dst_ref=o.at[slot], device_id=peer, **sems)
            sends.append(s_a)
            recvs.append(r_a)
        return local, sends, recvs

    def start(self, *refs):
        if self.n == 0:
            return
        local, sends, _ = self._copies(*refs)
        for a in range(self.n):
            local[a].start()
            for k in range(1, N_DEV):
                if a >= len(self.gathers) or k not in FORWARDED:
                    sends[a][k].start()

    def forward(self, *refs):
        if not self.gathers:
            return
        _, sends, recvs = self._copies(*refs)
        for a in range(len(self.gathers)):
            for k in FORWARDED:
                recvs[a][k - 1].wait_recv()
                sends[a][k].start()

    def wait(self, *refs):
        if self.n == 0:
            return
        local, sends, recvs = self._copies(*refs)
        for a in range(self.n):
            waited_early = [f - 1 for f in FORWARDED] if a < len(self.gathers) else []
            for k in range(1, N_DEV):
                if k not in waited_early:
                    recvs[a][k].wait_recv()
            for k in range(1, N_DEV):
                sends[a][k].wait_send()
            local[a].wait()


FORWARDED = (3, 5, 7)


def exchange(ex, name):
    n = ex.n

    def body(*refs):
        args = (refs[:n], refs[n:2 * n]) + tuple(refs[2 * n:])
        ex.start(*args)
        ex.forward(*args)
        ex.wait(*args)

    return pl.pallas_call(body, in_specs=[_ANY] * n, out_specs=[_ANY] * n, out_shape=ex.out_shape(),
                          scratch_shapes=ex.scratch(), name=name)(*ex.operands())


def _shift_up(z):
    return jnp.concatenate([z[1:], jnp.zeros_like(z[:1])], axis=0)


def _segments(width):
    seg = np.zeros((width, 128), np.float32)
    seg[np.arange(width), np.arange(width) // HEAD_DIM] = 1.0
    return jnp.asarray(seg), jnp.asarray(seg.T)


def _rope_consts(T):
    inv = ROPE_THETA ** (-jnp.arange(0, HEAD_DIM, 2, dtype=F32) / HEAD_DIM)
    ang = jnp.arange(T, dtype=F32)[:, None] * inv[None, :]
    return jnp.tile(jnp.cos(ang), (1, 4)), jnp.tile(jnp.sin(ang), (1, 4))


def _per_head(g, heads):
    return jnp.tile(g.reshape(1, HEAD_DIM), (1, heads))


def _sum_heads(g):
    return g.reshape(-1, HEAD_DIM).sum(axis=0, keepdims=True)


LORA_COLS = 256
RW_TILE = 256
ROW_TILE = 512
NORM_TILE = 1024


def _local_step(x0, memx, tgt, P, ex_weights=None, weights_done=None, ex_grads=None):
    T = x0.shape[0]
    P = dict(P)
    G = {}
    seg, seg_t = _segments(RWKV_WIDTH)
    mseg = (seg[:MEM_WIDTH], seg_t[:, :MEM_WIDTH])
    cos, sin = _rope_consts(T)
    row = lambda v: v.reshape(1, -1)

    def mem_fwd(i, q, into):
        memn = stage_fwd(f_rmsnorm, [memx], [P["mem_norm"][i:i + 1]], [], [], N_MEM, f"mem{i}_norm", [BF16])[0]
        kvm = matmul(memn, P["mem_w_kv"][i], "nn", f"mem{i}_kv")
        kn, qn = _per_head(P["mem_k_norm"][i], MEM_HEADS), _per_head(P["mem_q_norm"][i], MEM_HEADS)
        km = stage_fwd(f_headnorm, [Cols(kvm, MEM_WIDTH, 0)], [kn], [], mseg, N_MEM, f"mem{i}_knorm")[0]
        om = stage_fwd(f_memattn, [q], [km, Cols(kvm, MEM_WIDTH, 1), qn], [], mseg, ROW_TILE, f"mem{i}_attn", [BF16], into=into)[0]
        return om, (memn, kvm, km, kn, qn, q)

    def mem_bwd(i, saved, dymem, copy_into=None):
        memn, kvm, km, kn, qn, q = saved
        (dq,), (dkm, dvm, g_qn), *copy = stage_bwd(f_memattn, [q], [km, Cols(kvm, MEM_WIDTH, 1), qn], [], mseg, [dymem], ROW_TILE,
                                                   f"mem{i}_attn_bwd", bf16_copies=(0,) if copy_into else (), copy_into=copy_into)
        dq = copy[0][0] if copy_into else dq
        (dkraw,), (g_kn,) = stage_bwd(f_headnorm, [Cols(kvm, MEM_WIDTH, 0)], [kn], [], mseg, [dkm], N_MEM, f"mem{i}_knorm_bwd")
        dkvm = jnp.concatenate([dkraw, dvm], axis=1).astype(BF16)
        g_w = matmul(memn, dkvm, "tn", f"mem{i}_kv_dw")
        dmemn = matmul(dkvm, P["mem_w_kv"][i], "nt", f"mem{i}_kv_dx")
        _, (g_mn,) = stage_bwd(f_rmsnorm, [memx], [P["mem_norm"][i:i + 1]], [], [], [dmemn], N_MEM, f"mem{i}_norm_bwd")
        return dq, g_mn, g_w, _sum_heads(g_qn), _sum_heads(g_kn)

    def ffn_fwd(i, xin):
        hn = stage_fwd(f_rmsnorm, [xin], [P["ffn_norm"][i:i + 1]], [], [], NORM_TILE, f"ffn{i}_norm", [BF16])[0]
        u = matmul(hn, P["ffn_w_up"][i], "nt", f"ffn{i}_up")
        z = convgate_fwd(u, P["ffn_conv_w"][i], P["ffn_conv_b"][i:i + 1], f"ffn{i}_conv")
        return matmul(z, P["ffn_w_down"][i], "nn", f"ffn{i}_down", residual=xin), (hn, u, z)

    def ffn_bwd(i, xin, saved, dxo, dxo_b):
        hn, u, z = saved
        dz = matmul(dxo_b, P["ffn_w_down"][i], "nt", f"ffn{i}_down_dx")
        g_down = matmul(z, dxo_b, "tn", f"ffn{i}_down_dw")
        du, g_cw, g_cb = convgate_bwd(u, P["ffn_conv_w"][i], P["ffn_conv_b"][i:i + 1], dz, f"ffn{i}_conv_bwd")
        dhn = matmul(du, P["ffn_w_up"][i], "nn", f"ffn{i}_up_dx")
        g_up = matmul(du, hn, "tn", f"ffn{i}_up_dw")
        (dxin,), (g_n,), (dxin_b,) = stage_bwd(f_rmsnorm_res, [xin], [P["ffn_norm"][i:i + 1]], [], [], [dhn, dxo], ROW_TILE,
                                               f"ffn{i}_norm_bwd", bf16_copies=(0,))
        return dxin, dxin_b, g_n, g_up, g_cw, g_cb, g_down

    h0 = stage_fwd(f_rmsnorm, [x0], [P["attn_norm"][0:1]], [], [], NORM_TILE, "l0_norm", [BF16])[0]
    p0 = matmul(h0, P["a_w_in"][0], "nt", "l0_in")
    lora0 = 3 * RWKV_WIDTH // LORA_COLS
    pre_xs = [Cols(p0, RWKV_WIDTH, 0), Cols(p0, RWKV_WIDTH, 1), Cols(p0, RWKV_WIDTH, 2), Cols(p0, LORA_COLS, lora0)]
    mu = [Cols(P["a_mu"], RWKV_WIDTH, 0), Cols(P["a_mu"], RWKV_WIDTH, 1), Cols(P["a_mu"], RWKV_WIDTH, 2),
          Cols(P["a_mu"], LORA_COLS, lora0)]
    lora_rows = lambda w, lo: jnp.pad(w, ((lo, LORA_COLS - lo - w.shape[0]), (0, 0)))
    pre_ps = mu + [P["a_w0"], lora_rows(P["a_w2"][0], 0), P["a_a0"], lora_rows(P["a_a2"][0], 64), lora_rows(P["a_g2"][0], 128),
                   P["a_k_k"], P["a_k_a"]]
    r, lw, k2, v, kk, b, g = stage_fwd(f_rwkv_pre, pre_xs, pre_ps, [], [seg, seg_t], RW_TILE, "l0_rwkv_pre", with_prev=True)
    scan_in = [r, lw, k2, v, kk, b]
    y_h, h_states, got = rwkv_scan_fwd(*scan_in, ex_weights or Exchange())
    if weights_done is not None:
        P.update(weights_done(got))
    y_s = y_h
    post_ps = [P["a_lnx_w"], P["a_lnx_b"], P["a_r_k"].reshape(1, RWKV_WIDTH)]
    ycat0, mem0_saved = mem_fwd(0, Cols(p0, MEM_WIDTH, SHIFT_WIDTH // MEM_WIDTH), Into(None, D_MODEL, RWKV_WIDTH // MEM_WIDTH))
    ycat0 = stage_fwd(f_rwkv_post, [y_s, r, k2, v, g], post_ps, [], [seg, seg_t], RW_TILE, "l0_rwkv_post", [BF16],
                      into=Into(ycat0, D_MODEL, 0))[0]
    x1 = matmul(ycat0, P["a_w_out"][0], "nn", "l0_out", residual=x0)
    x2, ffn0_saved = ffn_fwd(0, x1)

    hk, h1 = stage_fwd(f_rmsnorm2, [x2], [row(P["kv_norm"]), P["attn_norm"][1:2]], [], [], NORM_TILE, "l1_norm", [BF16, BF16])
    kvp = matmul(hk, P["kv_w"][0], "nt", "l1_kv")
    p1 = matmul(h1, P["b_w_in"][0], "nn", "l1_in")
    kraw, qraw = Cols(kvp, DIL_WIDTH, 0), Cols(p1, DIL_WIDTH, 0)
    kgain, qgain = _per_head(P["kv_k_norm"], DIL_WIDTH // HEAD_DIM), _per_head(P["b_q_norm"], DIL_WIDTH // HEAD_DIM)
    ksh = stage_fwd(f_qkprep, [kraw], [kgain], [cos, sin], [seg, seg_t], ROW_TILE, "l1_kprep")[0]
    q = stage_fwd(f_qkprep, [qraw], [qgain], [cos, sin], [seg, seg_t], ROW_TILE, "l1_qprep")[0]
    outs, lses = [], []
    for gi, (_, d) in enumerate(DIL_GROUPS):
        og, lg = dil_fwd(q, ksh, kvp, gi, d, f"l1_dil{gi}")
        outs.append(og)
        lses.append(lg)
    ycat1 = stage_fwd(f_mix, outs + lses, [], [], [], ROW_TILE, "l1_mix", [BF16], into=Into(None, 2 * MEM_WIDTH, 0))[0]
    ycat1, mem1_saved = mem_fwd(1, Cols(p1, MEM_WIDTH, DIL_WIDTH // MEM_WIDTH), Into(ycat1, 2 * MEM_WIDTH, 1))
    x3 = matmul(ycat1, P["b_w_out"][0], "nt", "l1_out", residual=x2)
    x4, ffn1_saved = ffn_fwd(1, x3)
    loss_part, dx4, dx4_b = loss_head(x4, tgt)

    dx3, dx3_b, gn1, gup1, gcw1, gcb1, gdown1 = ffn_bwd(1, x3, ffn1_saved, dx4, dx4_b)
    dycat1 = matmul(dx3_b, P["b_w_out"][0], "nn", "l1_out_dx")
    G["b_w_out"] = [matmul(dx3_b, ycat1, "tn", "l1_out_dw")]
    dp1, gmn1, gmw1, gmq1, gmk1 = mem_bwd(1, mem1_saved, Cols(dycat1, MEM_WIDTH, 1), Into(None, D_MODEL, DIL_WIDTH // MEM_WIDTH))
    dmix, _ = stage_bwd(f_mix, outs + lses, [], [], [], [Cols(dycat1, MEM_WIDTH, 0)], ROW_TILE, "l1_mix_bwd")
    dq, dk, dv = zip(*[dil_bwd(q, ksh, kvp, dmix[gi], dmix[3 + gi], gi, d, f"l1_dil{gi}_bwd")
                       for gi, (_, d) in enumerate(DIL_GROUPS)])
    dq, dk, dv = jnp.concatenate(dq, axis=1), jnp.concatenate(dk, axis=1), jnp.concatenate(dv, axis=1)
    _, (g_bq,), (dp1,) = stage_bwd(f_qkprep, [qraw], [qgain], [cos, sin], [seg, seg_t], [dq], ROW_TILE, "l1_qprep_bwd",
                                   bf16_copies=(0,), copy_into=Into(dp1, D_MODEL, 0))
    dkvp = jnp.pad(dv.astype(BF16), ((0, 0), (DIL_WIDTH, 0)))
    _, (g_kk,), (dkvp,) = stage_bwd(f_qkprep, [kraw], [kgain], [cos, sin], [seg, seg_t], [dk], ROW_TILE, "l1_kprep_bwd",
                                    bf16_copies=(0,), copy_into=Into(dkvp, 2 * DIL_WIDTH, 0))
    g_bq, g_kk = _sum_heads(g_bq), _sum_heads(g_kk)
    dh1 =matmul(dp1, P["b_w_in"][0], "nt", "l1_in_dx")
    G["b_w_in"] = [matmul(h1, dp1, "tn", "l1_in_dw")]
    dhk = matmul(dkvp, P["kv_w"][0], "nn", "l1_kv_dx")
    G["kv_w"] = [matmul(dkvp, hk, "tn", "l1_kv_dw")]
    (dx2,), (g_kvn, g_an1), (dx2_b,) = stage_bwd(f_rmsnorm2_res, [x2], [row(P["kv_norm"]), P["attn_norm"][1:2]], [], [],
                                                 [dhk, dh1, dx3], ROW_TILE, "l1_norm_bwd", bf16_copies=(0,))

    dx1, dx1_b, gn0, gup0, gcw0, gcb0, gdown0 = ffn_bwd(0, x1, ffn0_saved, dx2, dx2_b)
    dycat0 = matmul(dx1_b, P["a_w_out"][0], "nt", "l0_out_dx")
    G["a_w_out"] = [matmul(ycat0, dx1_b, "tn", "l0_out_dw")]
    dqmem0, gmn0, gmw0, gmq0, gmk0 = mem_bwd(0, mem0_saved, Cols(dycat0, MEM_WIDTH, RWKV_WIDTH // MEM_WIDTH))
    (dy_s, dr_a, dk_a, dv_a, dg), (g_lw, g_lb, g_rk) = stage_bwd(
        f_rwkv_post, [y_s, r, k2, v, g], post_ps, [], [seg, seg_t], [Cols(dycat0, RWKV_WIDTH, 0)], RW_TILE, "l0_rwkv_post_bwd")
    G["mem_w_kv"], G["ffn_w_up"], G["ffn_w_down"] = [gmw0, gmw1], [gup0, gup1], [gdown0, gdown1]
    (dr_b, dlw, dk_b, dv_b, dkk, db), G["_exchanged"] = rwkv_scan_bwd(*scan_in, h_states, dy_s,
                                                                      ex_grads(G) if ex_grads else Exchange())
    dpre, gpre = stage_bwd(f_rwkv_pre, pre_xs, pre_ps, [], [seg, seg_t],
                           [[dr_a, dr_b], dlw, [dk_a, dk_b], [dv_a, dv_b], dkk, db, dg], RW_TILE, "l0_rwkv_pre_bwd", with_prev=True)
    dp_rw = jnp.concatenate(dpre[:4], axis=1) + _shift_up(jnp.concatenate(dpre[4:], axis=1))
    dp0 = jnp.concatenate([dp_rw, dqmem0], axis=1).astype(BF16)
    dh0 = matmul(dp0, P["a_w_in"][0], "nn", "l0_in_dx")
    G["a_w_in"] = [matmul(dp0, h0, "tn", "l0_in_dw")]
    (dx0,), (g_an0,) = stage_bwd(f_rmsnorm_res, [x0], [P["attn_norm"][0:1]], [], [], [dh0, dx1], ROW_TILE, "l0_norm_bwd")

    G["attn_norm"] = jnp.concatenate([g_an0, g_an1], axis=0)
    G["a_mu"] = jnp.concatenate(gpre[:4], axis=1)
    G["a_w0"], G["a_w2"], G["a_a0"], G["a_a2"], G["a_g2"] = gpre[4], gpre[5][None, :64], gpre[6], gpre[7][None, 64:128], gpre[8][None, 128:]
    G["a_k_k"], G["a_k_a"] = gpre[9], gpre[10]
    G["a_r_k"] = g_rk.reshape(1, RWKV_HEADS, HEAD_DIM)
    G["a_lnx_w"], G["a_lnx_b"] = g_lw, g_lb
    G["kv_norm"], G["kv_k_norm"], G["b_q_norm"] = g_kvn.reshape(-1), g_kk.reshape(-1), g_bq
    G["mem_norm"] = jnp.concatenate([gmn0, gmn1], axis=0)
    G["mem_w_kv"] = [gmw0, gmw1]
    G["mem_q_norm"] = jnp.concatenate([gmq0, gmq1], axis=0)
    G["mem_k_norm"] = jnp.concatenate([gmk0, gmk1], axis=0)
    G["ffn_norm"] = jnp.concatenate([gn0, gn1], axis=0)
    G["ffn_w_up"] = [gup0, gup1]
    G["ffn_conv_w"] = jnp.stack([gcw0, gcw1])
    G["ffn_conv_b"] = jnp.concatenate([gcb0, gcb1], axis=0)
    G["ffn_w_down"] = [gdown0, gdown1]
    return loss_part, dx0, G


PARAMS = (("attn_norm", None), ("a_w_in", 2), ("a_mu", 1), ("a_w0", 1), ("a_w2", 2), ("a_a0", 1), ("a_a2", 2), ("a_g2", 2),
          ("a_k_k", 1), ("a_k_a", 1), ("a_r_k", None), ("a_lnx_w", 1), ("a_lnx_b", 1), ("a_w_out", 1), ("kv_norm", None),
          ("kv_w", 1), ("kv_k_norm", None), ("b_w_in", 1), ("b_q_norm", None), ("b_w_out", 2), ("mem_norm", None),
          ("mem_w_kv", 1), ("mem_q_norm", None), ("mem_k_norm", None), ("ffn_norm", None), ("ffn_w_up", 2),
          ("ffn_conv_w", 2), ("ffn_conv_b", None), ("ffn_w_down", 1))
BIG = ("a_w_in", "a_w_out", "kv_w", "b_w_in", "b_w_out", "mem_w_kv", "ffn_w_up", "ffn_w_down")
TRANSPOSED = ("a_w_in", "kv_w", "b_w_out", "ffn_w_up")
AXIS = dict(PARAMS)
SMALL = tuple(n for n, _ in PARAMS if n not in BIG)
SMALL_SHARDED = tuple(n for n in SMALL if AXIS[n] is not None)
PACK_QUANTUM = 256 * 128


def _from_shards(xs, axis):
    full = jnp.moveaxis(xs, 0, axis)
    sh = full.shape
    return full.reshape(sh[:axis] + (sh[axis] * sh[axis + 1],) + sh[axis + 2:])


def _to_shards(g, axis):
    sh = g.shape
    return jnp.moveaxis(g.reshape(sh[:axis] + (N_DEV, sh[axis] // N_DEV) + sh[axis + 1:]), axis, 0)


def _pack(parts, lead=0):
    ld = parts[0].shape[:lead]
    flat = jnp.concatenate([p.reshape(ld + (-1,)) for p in parts], axis=-1)
    pad = (-flat.shape[-1]) % PACK_QUANTUM
    flat = jnp.pad(flat, [(0, 0)] * lead + [(0, pad)])
    return flat.reshape(ld + (-1, 128))


def _unpack(packed, shapes, lead=0):
    ld = packed.shape[:lead]
    flat = packed.reshape(ld + (-1,))
    out, off = [], 0
    for s in shapes:
        n = math.prod(s)
        out.append(flat[..., off:off + n].reshape(ld + tuple(s)))
        off += n
    return out


def kernel(x, mem, attn_norm, a_w_in, a_mu, a_w0, a_w2, a_a0, a_a2, a_g2, a_k_k, a_k_a, a_r_k, a_lnx_w, a_lnx_b, a_w_out, kv_norm, kv_w, kv_k_norm, b_w_in, b_q_norm, b_w_out, mem_norm, mem_w_kv, mem_q_norm, mem_k_norm, ffn_norm, ffn_w_up, ffn_conv_w, ffn_conv_b, ffn_w_down, loss_target, m_attn_norm, m_a_w_in, m_a_mu, m_a_w0, m_a_w2, m_a_a0, m_a_a2, m_a_g2, m_a_k_k, m_a_k_a, m_a_r_k, m_a_lnx_w, m_a_lnx_b, m_a_w_out, m_kv_norm, m_kv_w, m_kv_k_norm, m_b_w_in, m_b_q_norm, m_b_w_out, m_mem_norm, m_mem_w_kv, m_mem_q_norm, m_mem_k_norm, m_ffn_norm, m_ffn_w_up, m_ffn_conv_w, m_ffn_conv_b, m_ffn_w_down, v_attn_norm, v_a_w_in, v_a_mu, v_a_w0, v_a_w2, v_a_a0, v_a_a2, v_a_g2, v_a_k_k, v_a_k_a, v_a_r_k, v_a_lnx_w, v_a_lnx_b, v_a_w_out, v_kv_norm, v_kv_w, v_kv_k_norm, v_b_w_in, v_b_q_norm, v_b_w_out, v_mem_norm, v_mem_w_kv, v_mem_q_norm, v_mem_k_norm, v_ffn_norm, v_ffn_w_up, v_ffn_conv_w, v_ffn_conv_b, v_ffn_w_down):
    names = [n for n, _ in PARAMS]
    vals = (attn_norm, a_w_in, a_mu, a_w0, a_w2, a_a0, a_a2, a_g2, a_k_k, a_k_a, a_r_k, a_lnx_w, a_lnx_b, a_w_out, kv_norm, kv_w, kv_k_norm, b_w_in, b_q_norm, b_w_out, mem_norm, mem_w_kv, mem_q_norm, mem_k_norm, ffn_norm, ffn_w_up, ffn_conv_w, ffn_conv_b, ffn_w_down)
    m_vals = (m_attn_norm, m_a_w_in, m_a_mu, m_a_w0, m_a_w2, m_a_a0, m_a_a2, m_a_g2, m_a_k_k, m_a_k_a, m_a_r_k, m_a_lnx_w, m_a_lnx_b, m_a_w_out, m_kv_norm, m_kv_w, m_kv_k_norm, m_b_w_in, m_b_q_norm, m_b_w_out, m_mem_norm, m_mem_w_kv, m_mem_q_norm, m_mem_k_norm, m_ffn_norm, m_ffn_w_up, m_ffn_conv_w, m_ffn_conv_b, m_ffn_w_down)
    v_vals = (v_attn_norm, v_a_w_in, v_a_mu, v_a_w0, v_a_w2, v_a_a0, v_a_a2, v_a_g2, v_a_k_k, v_a_k_a, v_a_r_k, v_a_lnx_w, v_a_lnx_b, v_a_w_out, v_kv_norm, v_kv_w, v_kv_k_norm, v_b_w_in, v_b_q_norm, v_b_w_out, v_mem_norm, v_mem_w_kv, v_mem_q_norm, v_mem_k_norm, v_ffn_norm, v_ffn_w_up, v_ffn_conv_w, v_ffn_conv_b, v_ffn_w_down)
    W, M, V = dict(zip(names, vals)), dict(zip(names, m_vals)), dict(zip(names, v_vals))
    layers = lambda D, n: [D[n]] if D[n].ndim == 2 else [D[n][i] for i in range(D[n].shape[0])]
    ax2 = lambda n: AXIS[n] - (W[n].ndim - 2)
    later =[(n, i) for n in BIG if n != "a_w_in" for i in range(len(layers(W, n)))]

    sent = lambda n, w: w.T if n in TRANSPOSED else w
    whole = lambda n, g: g.reshape(-1, g.shape[-1]) if n in TRANSPOSED else _from_shards(g, ax2(n))
    small_shapes = [W[n].shape for n in SMALL_SHARDED]
    got_w, got_small = exchange(Exchange(gathers=[sent("a_w_in", W["a_w_in"][0]).astype(BF16),
                                                  _pack([W[n] for n in SMALL_SHARDED])]), "gather_first")
    P = {n: W[n] for n in SMALL}
    P["a_w_in"] = [whole("a_w_in", got_w)]
    for n, s in zip(SMALL_SHARDED, _unpack(got_small, small_shapes, lead=1)):
        P[n] = _from_shards(s, AXIS[n])
    ex_weights =Exchange(gathers=[sent(n, layers(W, n)[i]).astype(BF16) for n, i in later])

    def weights_done(got):
        out = {}
        for (n, _), g in zip(later, got):
            out.setdefault(n, []).append(whole(n, g))
        return out

    slots = lambda G, n: jnp.stack([_to_shards(g, 0 if n in TRANSPOSED else ax2(n)) for g in G[n]], axis=1)
    later_names = [n for n in BIG if n != "a_w_in"]
    ex_grads = lambda G: Exchange(scatters=[slots(G, n) for n in later_names])
    loss_part, dx0, G = _local_step(x[0], mem[0], loss_target[0], P, ex_weights, weights_done, ex_grads)
    gparts = dict(zip(later_names, G.pop("_exchanged")))
    replicated = [n for n in SMALL if AXIS[n] is None]
    small_slots = _pack([_to_shards(G[n], AXIS[n]) for n in SMALL_SHARDED], lead=1)
    got_rep, gparts["a_w_in"], got_sharded = exchange(
        Exchange(gathers=[_pack([G[n] for n in replicated] + [loss_part[0:1, 0:1]])], scatters=[slots(G, "a_w_in"), small_slots]),
        "exchange_last")

    results = {}
    for n in BIG:
        rows = lambda z: z.reshape((-1,) + z.shape[-1:])
        gp = gparts[n].reshape((N_DEV, -1) + gparts[n].shape[-1:])
        if n in TRANSPOSED:
            gp = jnp.swapaxes(sum_parts(gp, f"sum_{n}").reshape(gparts[n].shape[1:]), -1, -2).reshape((1,) + rows(W[n]).shape)
        res = adamw(gp, rows(W[n]), rows(M[n]), rows(V[n]), f"adamw_{n}")
        results[n] = [r.reshape(W[n].shape) for r in res]
    *rep_sums, loss = _unpack(sum_parts(got_rep, "sum_replicated_grads"), [W[n].shape for n in replicated] + [()])
    g_mine = dict(zip(replicated, rep_sums))
    g_mine.update(zip(SMALL_SHARDED, _unpack(sum_parts(got_sharded, "sum_small_sharded_grads"), [W[n].shape for n in SMALL_SHARDED])))
    res = adamw(_pack([g_mine[n] for n in SMALL])[None], _pack([W[n] for n in SMALL]), _pack([M[n] for n in SMALL]),
                _pack([V[n] for n in SMALL]), "adamw_small")
    for n, parts in zip(SMALL, zip(*[_unpack(r, [W[n].shape for n in SMALL]) for r in res])):
        results[n] = list(parts)
    outs = [[results[n][j] for n in names] for j in range(4)]
    return (loss, dx0[None], *outs[0], *outs[1], *outs[2], *outs[3])
```
